```python
import math
import jax, jax.numpy as jnp
from jax import lax
import numpy as np

D_MODEL = 1024
BATCH = 32
SEQ = 2048
DEPTH = 1

MEM_LEN = 256
CHUNK = 128
RET_HEADS = 4
RET_DK = 64
RET_DV = 128
ML_HEADS = 4
ML_DK = 128
ML_DV = 128
CONV_W = 4
XA_HEADS = 4
XA_DH = D_MODEL // XA_HEADS
N_GROUPS = 4
EXP_PER_GROUP = 8
N_EXPERTS = N_GROUPS * EXP_PER_GROUP
TOP_K = 2
D_EXPERT = 512
MOE_BLOCK = 128
ROPE_BASE = 10000.0
EPS = 1e-6

RET_QK = RET_HEADS * RET_DK
RET_V = RET_HEADS * RET_DV
ML_QK = ML_HEADS * ML_DK
ML_V = ML_HEADS * ML_DV
MIX_WIDTH = RET_V + ML_V
SPLITS = (RET_QK, RET_QK, RET_V, RET_V, 2 * ML_QK, ML_V, ML_V, 2 * ML_HEADS)
IN_WIDTH = sum(SPLITS)

kernel_name = "hymba_retnet_mlstm_xattn_hmoe"


def rmsnorm(x, w):
    xf = x.astype(jnp.float32)
    y = xf * lax.rsqrt(jnp.mean(xf * xf, axis=-1, keepdims=True) + EPS)
    return (y * w.astype(jnp.float32)).astype(x.dtype)


def head_norm(h, w):
    mu = jnp.mean(h, axis=-1, keepdims=True)
    var = jnp.mean(jnp.square(h - mu), axis=-1, keepdims=True)
    y = (h - mu) * lax.rsqrt(var + EPS)
    B, S, H, d = h.shape
    return y.reshape(B, S, H * d) * w.astype(jnp.float32)


def rope(x, pos):
    half = x.shape[-1] // 2
    inv = ROPE_BASE ** (-jnp.arange(half, dtype=jnp.float32) / half)
    ang = pos.astype(jnp.float32)[:, None] * inv[None, :]
    cos = jnp.cos(ang)[None, :, None, :]
    sin = jnp.sin(ang)[None, :, None, :]
    x1, x2 = x[..., :half], x[..., half:]
    return jnp.concatenate([x1 * cos - x2 * sin, x1 * sin + x2 * cos], axis=-1)


def causal_conv(x, w, b):
    C = x.shape[-1]
    y = lax.conv_general_dilated(x, w[:, None, :].astype(x.dtype), window_strides=(1,),
                                 padding=[(CONV_W - 1, 0)],
                                 dimension_numbers=("NWC", "WIO", "NWC"),
                                 feature_group_count=C)
    return y + b.astype(x.dtype)


def retention(q, k, v):
    B, S, H, dk = q.shape
    dv = v.shape[-1]
    L = CHUNK
    nc = S // L
    log_g = jnp.log1p(-jnp.exp2(-5.0 - jnp.arange(H, dtype=jnp.float32)))
    q = q.reshape(B, nc, L, H, dk)
    k = k.reshape(B, nc, L, H, dk)
    v = v.reshape(B, nc, L, H, dv)
    n = jnp.arange(L, dtype=jnp.float32)
    diff = n[:, None] - n[None, :]
    dmat = jnp.where(diff >= 0, jnp.exp(log_g[:, None, None] * jnp.maximum(diff, 0.0)[None]), 0.0)
    scores = jnp.einsum('bclhd,bcmhd->bchlm', q, k) * dmat
    inner = jnp.einsum('bchlm,bcmhe->bclhe', scores, v)
    k_dec = k * jnp.exp((L - 1 - n)[:, None] * log_g[None, :])[:, :, None]
    kv = jnp.einsum('bclhd,bclhe->cbhde', k_dec, v)
    chunk_decay = jnp.exp(L * log_g)[None, :, None, None]

    def step(R, kv_c):
        return chunk_decay * R + kv_c, R

    _, R_prev = lax.scan(step, jnp.zeros((B, H, dk, dv), jnp.float32), kv)
    q_dec = q * jnp.exp((n + 1)[:, None] * log_g[None, :])[:, :, None]
    cross = jnp.einsum('bclhd,cbhde->bclhe', q_dec, R_prev)
    return (inner + cross).reshape(B, S, H, dv)


def mlstm(q, k, v, i_pre, f_pre):
    B, S, H, dk = q.shape
    dv = v.shape[-1]
    L = CHUNK
    nc = S // L

    def chunked(t):
        return t.reshape(B, nc, L, H, -1).transpose(0, 1, 3, 2, 4)

    q = chunked(q)
    k = chunked(k) * (dk ** -0.5)
    v = chunked(v)
    logf = jax.nn.log_sigmoid(f_pre).reshape(B, nc, L, H).transpose(0, 1, 3, 2)
    ig = i_pre.reshape(B, nc, L, H).transpose(0, 1, 3, 2)
    b = jnp.cumsum(logf, axis=-1)
    b_tot = b[..., -1]
    causal = jnp.tril(jnp.ones((L, L), bool))
    logD = jnp.where(causal, b[..., :, None] - b[..., None, :] + ig[..., None, :], -jnp.inf)
    m_intra = jnp.max(logD, axis=-1)
    a = b_tot[..., None] - b + ig
    m_loc = jnp.max(a, axis=-1)
    wa = jnp.exp(a - m_loc[..., None])
    kw = k * wa[..., None]
    kv_loc = jnp.einsum('bchld,bchle->cbhde', kw, v)
    n_loc = jnp.moveaxis(jnp.sum(kw, axis=3), 1, 0)

    def step(carry, inp):
        C, nvec, m = carry
        g, ml, kvc, nlc = inp
        m_new = jnp.maximum(g + m, ml)
        s_old = jnp.exp(g + m - m_new)
        s_loc = jnp.exp(ml - m_new)
        C_new = s_old[..., None, None] * C + s_loc[..., None, None] * kvc
        n_new = s_old[..., None] * nvec + s_loc[..., None] * nlc
        return (C_new, n_new, m_new), (C, nvec, m)

    init = (jnp.zeros((B, H, dk, dv), jnp.float32), jnp.zeros((B, H, dk), jnp.float32),
            jnp.zeros((B, H), jnp.float32))
    _, (C_prev, n_prev, m_prev) = lax.scan(
        step, init, (jnp.moveaxis(b_tot, 1, 0), jnp.moveaxis(m_loc, 1, 0), kv_loc, n_loc))
    m_prev = jnp.moveaxis(m_prev, 0, 1)
    m_inter = b + m_prev[..., None]
    m_t = jnp.maximum(m_intra, m_inter)
    s = jnp.einsum('bchtd,bchsd->bchts', q, k) * jnp.exp(logD - m_t[..., None])
    inter = jnp.exp(m_inter - m_t)
    num = jnp.einsum('bchts,bchse->bchte', s, v) + inter[..., None] * jnp.einsum('bchtd,cbhde->bchte', q, C_prev)
    den = jnp.sum(s, axis=-1) + inter * jnp.einsum('bchtd,cbhd->bcht', q, n_prev)
    h = num / jnp.maximum(jnp.abs(den), jnp.exp(-m_t))[..., None]
    return h.transpose(0, 1, 3, 2, 4).reshape(B, S, H, dv)


def hybrid_mixer(h, w_in, ret_norm_w, ml_conv_w, ml_conv_b, ml_gate_b, ml_norm_w, w_out):
    B, S, _ = h.shape
    proj = h @ w_in
    r_q, r_k, r_v, r_g, m_qk, m_v, m_o, m_if = jnp.split(proj, [int(c) for c in np.cumsum(SPLITS)[:-1]], axis=-1)
    pos = jnp.arange(S)
    rq = rope(r_q.reshape(B, S, RET_HEADS, RET_DK).astype(jnp.float32), pos)
    rk = rope(r_k.reshape(B, S, RET_HEADS, RET_DK).astype(jnp.float32), pos) * (RET_DK ** -0.5)
    rv = r_v.reshape(B, S, RET_HEADS, RET_DV).astype(jnp.float32)
    ret = head_norm(retention(rq, rk, rv), ret_norm_w) * jax.nn.silu(r_g.astype(jnp.float32))
    qk = jax.nn.silu(causal_conv(m_qk, ml_conv_w, ml_conv_b)).astype(jnp.float32)
    mq = qk[..., :ML_QK].reshape(B, S, ML_HEADS, ML_DK)
    mk = qk[..., ML_QK:].reshape(B, S, ML_HEADS, ML_DK)
    mv = m_v.reshape(B, S, ML_HEADS, ML_DV).astype(jnp.float32)
    gates = m_if.astype(jnp.float32) + ml_gate_b.astype(jnp.float32)
    hm = mlstm(mq, mk, mv, gates[..., :ML_HEADS], gates[..., ML_HEADS:])
    ml = jax.nn.sigmoid(m_o.astype(jnp.float32)) * head_norm(hm, ml_norm_w)
    return jnp.concatenate([ret, ml], axis=-1).astype(h.dtype) @ w_out


def mem_cross_attention(h, mem_n, wq, wkv, wo):
    B, S, D = h.shape
    M = mem_n.shape[1]
    q = (h @ wq).reshape(B, S, XA_HEADS, XA_DH)
    kv = mem_n @ wkv
    k = kv[..., :D].reshape(B, M, XA_HEADS, XA_DH)
    v = kv[..., D:].reshape(B, M, XA_HEADS, XA_DH)
    logits = jnp.einsum('bshd,bmhd->bhsm', q, k).astype(jnp.float32) * (XA_DH ** -0.5)
    p = jax.nn.softmax(logits, axis=-1).astype(h.dtype)
    o = jnp.einsum('bhsm,bmhd->bshd', p, v).reshape(B, S, D)
    return o @ wo


def hier_moe(h, w_group, b_group, w_router, b_router, w_gate, w_up, w_down):
    B, S, D = h.shape
    N = B * S
    xt = h.reshape(N, D)
    xf = xt.astype(jnp.float32)
    g_logits = xf @ w_group.astype(jnp.float32) + b_group.astype(jnp.float32)
    g_prob = jax.nn.softmax(g_logits, axis=-1)
    g_sel = jnp.argmax(g_logits, axis=-1)
    g_w = jnp.take_along_axis(g_prob, g_sel[:, None], axis=-1)[:, 0]
    e_logits = (xf @ w_router.astype(jnp.float32) + b_router.astype(jnp.float32)).reshape(N, N_GROUPS, EXP_PER_GROUP)
    e_logits = jnp.take_along_axis(e_logits, g_sel[:, None, None], axis=1)[:, 0]
    e_prob = jax.nn.softmax(e_logits, axis=-1)
    top_p, top_i = lax.top_k(e_prob, TOP_K)
    top_p = top_p / jnp.sum(top_p, axis=-1, keepdims=True)
    weights = g_w[:, None] * top_p
    expert = g_sel[:, None] * EXP_PER_GROUP + top_i
    NK = N * TOP_K
    flat_e = expert.reshape(NK).astype(jnp.int32)
    flat_tok = jnp.repeat(jnp.arange(N, dtype=jnp.int32), TOP_K)
    order = jnp.argsort(flat_e)
    sorted_e = flat_e[order]
    counts = jnp.zeros((N_EXPERTS,), jnp.int32).at[flat_e].add(1)
    starts = jnp.cumsum(counts) - counts
    padded = (counts + MOE_BLOCK - 1) // MOE_BLOCK * MOE_BLOCK
    pstarts = jnp.cumsum(padded) - padded
    pends = pstarts + padded
    dest_sorted = pstarts[sorted_e] + (jnp.arange(NK, dtype=jnp.int32) - starts[sorted_e])
    cap = NK + N_EXPERTS * MOE_BLOCK
    nblk = cap // MOE_BLOCK
    row_tok = jnp.full((cap,), N, jnp.int32).at[dest_sorted].set(flat_tok[order])
    blk_start = jnp.arange(nblk, dtype=jnp.int32) * MOE_BLOCK
    blk_e = jnp.minimum(jnp.sum(blk_start[:, None] >= pends[None, :], axis=1), N_EXPERTS - 1).astype(jnp.int32)
    x_pad = jnp.concatenate([xt, jnp.zeros((1, D), xt.dtype)], axis=0)
    xs = x_pad[row_tok].reshape(nblk, MOE_BLOCK, D)

    def expert_block(args):
        xb, e = args
        hid = jax.nn.silu(xb @ w_gate[e]) * (xb @ w_up[e])
        return hid @ w_down[e]

    ys = lax.map(expert_block, (xs, blk_e)).reshape(cap, D)
    dest = jnp.zeros((NK,), jnp.int32).at[order].set(dest_sorted)
    y = ys[dest].reshape(N, TOP_K, D)
    y = jnp.sum(y * weights[..., None].astype(y.dtype), axis=1)
    return y.reshape(B, S, D)


def setup_inputs(seed: int = 0) -> dict:
    key = jax.random.key(seed)
    ks = jax.random.split(key, 24)
    f32 = jnp.float32

    def nrm(k, shape, scale):
        return jax.random.normal(k, shape, f32) * scale

    def gain(k, shape):
        return 1.0 + 0.05 * jax.random.normal(k, shape, f32)

    Dp = DEPTH
    i_bias = 0.1 * jax.random.normal(ks[6], (Dp, ML_HEADS), f32)
    f_bias = jnp.linspace(3.0, 6.0, ML_HEADS, dtype=f32)[None] + 0.1 * jax.random.normal(ks[7], (Dp, ML_HEADS), f32)
    return {
        "x": jax.random.normal(ks[0], (BATCH, SEQ, D_MODEL), f32),
        "mem": jax.random.normal(ks[1], (BATCH, MEM_LEN, D_MODEL), f32),
        "norm_mix_w": gain(ks[2], (Dp, D_MODEL)),
        "w_in": nrm(ks[3], (Dp, D_MODEL, IN_WIDTH), D_MODEL ** -0.5),
        "ret_norm_w": gain(ks[4], (Dp, RET_V)),
        "ml_conv_w": nrm(ks[5], (Dp, CONV_W, 2 * ML_QK), CONV_W ** -0.5),
        "ml_conv_b": nrm(ks[8], (Dp, 2 * ML_QK), 0.01),
        "ml_gate_b": jnp.concatenate([i_bias, f_bias], axis=-1),
        "ml_norm_w": gain(ks[9], (Dp, ML_V)),
        "w_out": nrm(ks[10], (Dp, MIX_WIDTH, D_MODEL), MIX_WIDTH ** -0.5),
        "norm_xa_w": gain(ks[11], (Dp, D_MODEL)),
        "norm_mem_w": gain(ks[12], (Dp, D_MODEL)),
        "xa_wq": nrm(ks[13], (Dp, D_MODEL, D_MODEL), D_MODEL ** -0.5),
        "xa_wkv": nrm(ks[14], (Dp, D_MODEL, 2 * D_MODEL), D_MODEL ** -0.5),
        "xa_wo": nrm(ks[15], (Dp, D_MODEL, D_MODEL), D_MODEL ** -0.5),
        "norm_moe_w": gain(ks[16], (Dp, D_MODEL)),
        "moe_w_group": nrm(ks[17], (Dp, D_MODEL, N_GROUPS), D_MODEL ** -0.5),
        "moe_b_group": nrm(ks[18], (Dp, N_GROUPS), 0.01),
        "moe_w_router": nrm(ks[19], (Dp, D_MODEL, N_EXPERTS), D_MODEL ** -0.5),
        "moe_b_router": nrm(ks[20], (Dp, N_EXPERTS), 0.01),
        "moe_w_gate": nrm(ks[21], (Dp, N_EXPERTS, D_MODEL, D_EXPERT), D_MODEL ** -0.5),
        "moe_w_up": nrm(ks[22], (Dp, N_EXPERTS, D_MODEL, D_EXPERT), D_MODEL ** -0.5),
        "moe_w_down": nrm(ks[23], (Dp, N_EXPERTS, D_EXPERT, D_MODEL), D_EXPERT ** -0.5),
        "norm_final_w": gain(jax.random.fold_in(key, 99), (D_MODEL,)),
    }


def reference(x, mem, norm_mix_w, w_in, ret_norm_w, ml_conv_w, ml_conv_b, ml_gate_b, ml_norm_w, w_out,
              norm_xa_w, norm_mem_w, xa_wq, xa_wkv, xa_wo, norm_moe_w, moe_w_group, moe_b_group,
              moe_w_router, moe_b_router, moe_w_gate, moe_w_up, moe_w_down, norm_final_w):
    for l in range(DEPTH):
        h = rmsnorm(x, norm_mix_w[l])
        x = x + hybrid_mixer(h, w_in[l], ret_norm_w[l], ml_conv_w[l], ml_conv_b[l], ml_gate_b[l],
                             ml_norm_w[l], w_out[l])
        h = rmsnorm(x, norm_xa_w[l])
        x = x + mem_cross_attention(h, rmsnorm(mem, norm_mem_w[l]), xa_wq[l], xa_wkv[l], xa_wo[l])
        h = rmsnorm(x, norm_moe_w[l])
        x = x + hier_moe(h, moe_w_group[l], moe_b_group[l], moe_w_router[l], moe_b_router[l],
                         moe_w_gate[l], moe_w_up[l], moe_w_down[l])
    return rmsnorm(x, norm_final_w)
```

```python
import functools

import numpy as np
import jax
import jax.numpy as jnp
from jax import lax
from jax.experimental import pallas as pl
from jax.experimental.pallas import tpu as pltpu

F32 = jnp.float32
BF16 = jnp.bfloat16
HIGHEST = lax.Precision.HIGHEST

CHUNK = 128
RET_HEADS = 4
RET_DK = 64
RET_DV = 128
ML_HEADS = 4
ML_DK = 128
ML_DV = 128
CONV_W = 4
XA_HEADS = 4
N_GROUPS = 4
EXP_PER_GROUP = 8
N_EXPERTS = N_GROUPS * EXP_PER_GROUP
TOP_K = 2
ROPE_BASE = 10000.0
EPS = 1e-6

RET_QK = RET_HEADS * RET_DK
RET_V = RET_HEADS * RET_DV
ML_QK = ML_HEADS * ML_DK
ML_V = ML_HEADS * ML_DV
OFF_RQ = 0
OFF_RK = OFF_RQ + RET_QK
OFF_RV = OFF_RK + RET_QK
OFF_RG = OFF_RV + RET_V
OFF_MQK = OFF_RG + RET_V
OFF_MV = OFF_MQK + 2 * ML_QK
OFF_MO = OFF_MV + ML_V
MAIN_WIDTH = OFF_MO + ML_V
N_GATES = 2 * ML_HEADS

ROUTE_ROWS = 40
TOKEN_TILE = 512
MOE_ROWS = 256
VMEM_LIMIT = 56 * 1024 * 1024


def _dot(a, b):
    return jnp.dot(a, b, preferred_element_type=F32)


def _dot_nt(a, b, precision=None):
    return lax.dot_general(a, b, (((1,), (1,)), ((), ())), preferred_element_type=F32, precision=precision)


def _dot_tn(a, b):
    return lax.dot_general(a, b, (((0,), (0,)), ((), ())), preferred_element_type=F32)


def _rms(x, w):
    return x * lax.rsqrt(jnp.mean(x * x, axis=-1, keepdims=True) + EPS) * w


def _sigmoid(x):
    return 1.0 / (1.0 + jnp.exp(-x))


def _silu(x):
    return x * _sigmoid(x)


def _log_sigmoid(x):
    return jnp.minimum(x, 0.0) - jnp.log1p(jnp.exp(-jnp.abs(x)))


def _head_norm(h):
    mu = jnp.mean(h, axis=-1, keepdims=True)
    d = h - mu
    var = jnp.mean(d * d, axis=-1, keepdims=True)
    return d * lax.rsqrt(var + EPS)


def _inproj_kernel(x_ref, nw_ref, w_ref, wif_ref, wift_ref, proj_ref, g_ref, gt_ref):
    h = _rms(x_ref[...], nw_ref[...]).astype(BF16)
    for j in range(0, MAIN_WIDTH, 512):
        proj_ref[:, j:j + 512] = _dot(h, w_ref[:, j:j + 512]).astype(BF16)
    g_ref[...] = _dot(h, wif_ref[...])
    gt_ref[...] = _dot_nt(wift_ref[...], h)


def _inproj(xf, norm_w, w_main, w_if, w_ift):
    n, d = xf.shape
    tm = TOKEN_TILE
    return pl.pallas_call(
        _inproj_kernel,
        grid=(n // tm,),
        in_specs=[
            pl.BlockSpec((tm, d), lambda i: (i, 0)),
            pl.BlockSpec((1, d), lambda i: (0, 0)),
            pl.BlockSpec((d, MAIN_WIDTH), lambda i: (0, 0)),
            pl.BlockSpec((d, N_GATES), lambda i: (0, 0)),
            pl.BlockSpec((N_GATES, d), lambda i: (0, 0)),
        ],
        out_specs=[
            pl.BlockSpec((tm, MAIN_WIDTH), lambda i: (i, 0)),
            pl.BlockSpec((tm, N_GATES), lambda i: (i, 0)),
            pl.BlockSpec((N_GATES, tm), lambda i: (0, i)),
        ],
        out_shape=[
            jax.ShapeDtypeStruct((n, MAIN_WIDTH), BF16),
            jax.ShapeDtypeStruct((n, N_GATES), F32),
            jax.ShapeDtypeStruct((N_GATES, n), F32),
        ],
        compiler_params=pltpu.CompilerParams(
            dimension_semantics=("arbitrary",), vmem_limit_bytes=VMEM_LIMIT),
        name="inproj",
    )(xf, norm_w, w_main, w_if, w_ift)


def _mixer_kernel(proj_ref, g_ref, gt_ref, cos_ref, sin_ref, qdec_ref, kdec_ref, dmat_ref,
                  bmask_ref, cdec_ref, hmask_ref, tril_ref, triu_ref, retw_ref, mlw_ref,
                  convw_ref, convb_ref, gbc_ref, gbr_ref,
                  out_ref, r_ref, c_ref, n_ref, m_ref, tail_ref):
    L = CHUNK

    @pl.when(pl.program_id(1) == 0)
    def _():
        r_ref[...] = jnp.zeros_like(r_ref)
        c_ref[...] = jnp.zeros_like(c_ref)
        n_ref[...] = jnp.zeros_like(n_ref)
        m_ref[...] = jnp.zeros_like(m_ref)
        tail_ref[...] = jnp.zeros_like(tail_ref)

    cos = cos_ref[...]
    sin = sin_ref[...]
    half = RET_QK // 2

    def rope(t):
        t1, t2 = t[:, :half], t[:, half:]
        return jnp.concatenate([t1 * cos - t2 * sin, t1 * sin + t2 * cos], axis=1)

    q = rope(proj_ref[:, OFF_RQ:OFF_RQ + RET_QK].astype(F32))
    k = rope(proj_ref[:, OFF_RK:OFF_RK + RET_QK].astype(F32)) * (RET_DK ** -0.5)
    v = proj_ref[:, OFF_RV:OFF_RV + RET_V]
    k_b = k.astype(BF16)
    r_prev = r_ref[...]
    cross = _dot((q * qdec_ref[...]).astype(BF16), r_prev.astype(BF16))
    kv = _dot_tn((k * kdec_ref[...]).astype(BF16), v) * bmask_ref[...]
    r_ref[...] = cdec_ref[...] * r_prev + kv
    ret_heads = []
    for h in range(RET_HEADS):
        qh = (q * hmask_ref[h:h + 1, :]).astype(BF16)
        sc = _dot_nt(qh, k_b) * dmat_ref[h]
        inner = _dot(sc.astype(BF16), v[:, h * RET_DV:(h + 1) * RET_DV])
        ret_heads.append(_head_norm(inner + cross[:, h * RET_DV:(h + 1) * RET_DV]))
    ret = jnp.concatenate(ret_heads, axis=1) * retw_ref[...]
    ret = ret * _silu(proj_ref[:, OFF_RG:OFF_RG + RET_V].astype(F32))
    out_ref[:, 0:RET_V] = ret.astype(BF16)

    xqk = proj_ref[:, OFF_MQK:OFF_MQK + 2 * ML_QK].astype(F32)
    tail = tail_ref[...]
    row8 = lax.broadcasted_iota(jnp.int32, (8, 2 * ML_QK), 0)
    acc = xqk * convw_ref[CONV_W - 1:CONV_W, :] + convb_ref[...]
    for s in range(1, CONV_W):
        rolled = pltpu.roll(xqk, s, 0)
        head8 = jnp.where(row8 < s, pltpu.roll(tail, s, 0), rolled[0:8])
        shifted = jnp.concatenate([head8, rolled[8:]], axis=0)
        acc = acc + shifted * convw_ref[CONV_W - 1 - s:CONV_W - s, :]
    tail_ref[...] = xqk[L - 8:L]
    qk = _silu(acc)
    mq = qk[:, :ML_QK]
    mk = qk[:, ML_QK:] * (ML_DK ** -0.5)
    mv = proj_ref[:, OFF_MV:OFF_MV + ML_V]

    gc = g_ref[...] + gbc_ref[...]
    gr = gt_ref[...] + gbr_ref[...]
    b_c = jnp.dot(tril_ref[...], _log_sigmoid(gc), preferred_element_type=F32, precision=HIGHEST)
    b_r = jnp.dot(_log_sigmoid(gr), triu_ref[...], preferred_element_type=F32, precision=HIGHEST)
    causal = (lax.broadcasted_iota(jnp.int32, (L, L), 0) >= lax.broadcasted_iota(jnp.int32, (L, L), 1))
    ml_heads = []
    for h in range(ML_HEADS):
        bc = b_c[:, ML_HEADS + h:ML_HEADS + h + 1]
        br = b_r[ML_HEADS + h:ML_HEADS + h + 1, :]
        igc = gc[:, h:h + 1]
        igr = gr[h:h + 1, :]
        btot = br[:, L - 1:L]
        log_d = jnp.where(causal, bc - br + igr, -jnp.inf)
        m_intra = jnp.max(log_d, axis=1, keepdims=True)
        m_loc = jnp.max(btot - br + igr, axis=1, keepdims=True)
        wa = jnp.exp(btot - bc + igc - m_loc)
        qh = mq[:, h * ML_DK:(h + 1) * ML_DK]
        kh = mk[:, h * ML_DK:(h + 1) * ML_DK]
        vh = mv[:, h * ML_DV:(h + 1) * ML_DV]
        kw = kh * wa
        kv_loc = _dot_tn(kw.astype(BF16), vh)
        n_loc = jnp.sum(kw, axis=0, keepdims=True)
        c_prev = c_ref[h]
        n_prev = n_ref[h][0:1, :]
        m_prev = m_ref[h][0:1, 0:1]
        m_inter = bc + m_prev
        m_t = jnp.maximum(m_intra, m_inter)
        qh_b = qh.astype(BF16)
        s_mat = _dot_nt(qh_b, kh.astype(BF16)) * jnp.exp(log_d - m_t)
        inter = jnp.exp(m_inter - m_t)
        num = _dot(s_mat.astype(BF16), vh) + inter * _dot(qh_b, c_prev.astype(BF16))
        den = jnp.sum(s_mat, axis=1, keepdims=True) + inter * jnp.sum(qh * n_prev, axis=1, keepdims=True)
        hh = num / jnp.maximum(jnp.abs(den), jnp.exp(-m_t))
        ml_heads.append(_head_norm(hh))
        m_new = jnp.maximum(btot + m_prev, m_loc)
        s_old = jnp.exp(btot + m_prev - m_new)
        s_loc = jnp.exp(m_loc - m_new)
        c_ref[h] = s_old * c_prev + s_loc * kv_loc
        n_ref[h] = jnp.broadcast_to(s_old * n_prev + s_loc * n_loc, (8, ML_DK))
        m_ref[h] = jnp.broadcast_to(m_new, (8, 128))
    ml = jnp.concatenate(ml_heads, axis=1) * mlw_ref[...]
    ml = ml * _sigmoid(proj_ref[:, OFF_MO:OFF_MO + ML_V].astype(F32))
    out_ref[:, RET_V:RET_V + ML_V] = ml.astype(BF16)


def _mixer_tables(seq):
    L = CHUNK
    half = RET_DK // 2
    inv = ROPE_BASE ** (-np.arange(half, dtype=np.float64) / half)
    ang = np.arange(seq, dtype=np.float64)[:, None] * inv[None, :].astype(np.float32).astype(np.float64)
    cos = np.tile(np.cos(ang), (1, RET_HEADS)).astype(np.float32)
    sin = np.tile(np.sin(ang), (1, RET_HEADS)).astype(np.float32)
    log_g = np.log1p(-np.exp2(-5.0 - np.arange(RET_HEADS, dtype=np.float64)))
    n = np.arange(L, dtype=np.float64)
    lane_head = (np.arange(RET_QK) % (RET_QK // 2)) // half
    qdec = np.exp((n + 1)[:, None] * log_g[lane_head][None, :]).astype(np.float32)
    kdec = np.exp((L - 1 - n)[:, None] * log_g[lane_head][None, :]).astype(np.float32)
    diff = n[:, None] - n[None, :]
    dmat = np.where(diff >= 0, np.exp(log_g[:, None, None] * np.maximum(diff, 0.0)[None]), 0.0).astype(np.float32)
    col_head = np.arange(RET_V) // RET_DV
    bmask = (lane_head[:, None] == col_head[None, :]).astype(np.float32)
    cdec = np.exp(L * log_g[col_head])[None, :].astype(np.float32)
    hmask = (lane_head[None, :] == np.arange(RET_HEADS)[:, None]).astype(np.float32)
    hmask = np.concatenate([hmask, np.zeros((8 - RET_HEADS, RET_QK), np.float32)], axis=0)
    tril = np.tril(np.ones((L, L), np.float32))
    return dict(cos=cos, sin=sin, qdec=qdec, kdec=kdec, dmat=dmat, bmask=bmask, cdec=cdec, hmask=hmask,
                tril=tril, triu=np.ascontiguousarray(tril.T))


def _mixer(proj, g, gt, tabs, ret_norm_w, ml_norm_w, conv_w, conv_b, gate_b, batch, seq):
    L = CHUNK
    nc = seq // L
    n = batch * seq
    const2 = lambda b, c: (0, 0)
    tok = lambda b, c: (b * nc + c, 0)
    in_specs = [
        pl.BlockSpec((L, MAIN_WIDTH), tok),
        pl.BlockSpec((L, N_GATES), tok),
        pl.BlockSpec((N_GATES, L), lambda b, c: (0, b * nc + c)),
        pl.BlockSpec((L, RET_QK // 2), lambda b, c: (c, 0)),
        pl.BlockSpec((L, RET_QK // 2), lambda b, c: (c, 0)),
        pl.BlockSpec((L, RET_QK), const2),
        pl.BlockSpec((L, RET_QK), const2),
        pl.BlockSpec((RET_HEADS, L, L), lambda b, c: (0, 0, 0)),
        pl.BlockSpec((RET_QK, RET_V), const2),
        pl.BlockSpec((1, RET_V), const2),
        pl.BlockSpec((8, RET_QK), const2),
        pl.BlockSpec((L, L), const2),
        pl.BlockSpec((L, L), const2),
        pl.BlockSpec((1, RET_V), const2),
        pl.BlockSpec((1, ML_V), const2),
        pl.BlockSpec((CONV_W, 2 * ML_QK), const2),
        pl.BlockSpec((1, 2 * ML_QK), const2),
        pl.BlockSpec((1, N_GATES), const2),
        pl.BlockSpec((N_GATES, 1), const2),
    ]
    return pl.pallas_call(
        _mixer_kernel,
        grid=(batch, nc),
        in_specs=in_specs,
        out_specs=pl.BlockSpec((L, RET_V + ML_V), tok),
        out_shape=jax.ShapeDtypeStruct((n, RET_V + ML_V), BF16),
        scratch_shapes=[
            pltpu.VMEM((RET_QK, RET_V), F32),
            pltpu.VMEM((ML_HEADS, ML_DK, ML_DV), F32),
            pltpu.VMEM((ML_HEADS, 8, ML_DK), F32),
            pltpu.VMEM((ML_HEADS, 8, 128), F32),
            pltpu.VMEM((8, 2 * ML_QK), F32),
        ],
        compiler_params=pltpu.CompilerParams(
            dimension_semantics=("arbitrary", "arbitrary"), vmem_limit_bytes=VMEM_LIMIT),
        name="mixer",
    )(proj, g, gt, tabs["cos"], tabs["sin"], tabs["qdec"], tabs["kdec"], tabs["dmat"], tabs["bmask"],
      tabs["cdec"], tabs["hmask"], tabs["tril"], tabs["triu"], ret_norm_w, ml_norm_w, conv_w, conv_b,
      gate_b.reshape(1, N_GATES), gate_b.reshape(N_GATES, 1))


def _memkv_kernel(mem_ref, nw_ref, wkv_ref, k_ref, v_ref):
    d = mem_ref.shape[-1]
    mn = _rms(mem_ref[0], nw_ref[...]).astype(BF16)
    k_ref[0] = _dot(mn, wkv_ref[:, :d]).astype(BF16)
    v_ref[0] = _dot(mn, wkv_ref[:, d:]).astype(BF16)


def _memkv(mem, norm_w, wkv):
    b, m, d = mem.shape
    return pl.pallas_call(
        _memkv_kernel,
        grid=(b,),
        in_specs=[
            pl.BlockSpec((1, m, d), lambda i: (i, 0, 0)),
            pl.BlockSpec((1, d), lambda i: (0, 0)),
            pl.BlockSpec((d, 2 * d), lambda i: (0, 0)),
        ],
        out_specs=[pl.BlockSpec((1, m, d), lambda i: (i, 0, 0))] * 2,
        out_shape=[jax.ShapeDtypeStruct((b, m, d), BF16)] * 2,
        compiler_params=pltpu.CompilerParams(
            dimension_semantics=("arbitrary",), vmem_limit_bytes=VMEM_LIMIT),
        name="memkv",
    )(mem, norm_w, wkv)


def _attn_route_kernel(x_ref, mix_ref, k_ref, v_ref, wout_ref, nxa_ref, wq_ref, wo_ref, nmoe_ref,
                       wr_ref, br_ref, sut_ref,
                       x2_ref, h3_ref, ri_ref, rw_ref, cnt_ref, carry_ref):
    tm, d = x_ref.shape
    dh = d // XA_HEADS

    @pl.when((pl.program_id(0) == 0) & (pl.program_id(1) == 0))
    def _():
        carry_ref[...] = jnp.zeros_like(carry_ref)

    x1 = x_ref[...] + _dot(mix_ref[...], wout_ref[...])
    h2 = _rms(x1, nxa_ref[...]).astype(BF16)
    q = _dot(h2, wq_ref[...]).astype(BF16)
    heads = []
    for h in range(XA_HEADS):
        logits = _dot_nt(q[:, h * dh:(h + 1) * dh], k_ref[0, :, h * dh:(h + 1) * dh]) * (dh ** -0.5)
        mx = jnp.max(logits, axis=-1, keepdims=True)
        e = jnp.exp(logits - mx)
        p = (e / jnp.sum(e, axis=-1, keepdims=True)).astype(BF16)
        heads.append(_dot(p, v_ref[0, :, h * dh:(h + 1) * dh]).astype(BF16))
    x2 = x1 + _dot(jnp.concatenate(heads, axis=1), wo_ref[...])
    x2_ref[...] = x2
    h3 = _rms(x2, nmoe_ref[...])
    h3_ref[...] = h3.astype(BF16)

    lt = _dot_nt(wr_ref[...], h3, precision=HIGHEST) + br_ref[...]
    gl = lt[N_EXPERTS:N_EXPERTS + N_GROUPS]
    gmax = jnp.max(gl, axis=0, keepdims=True)
    g_w = 1.0 / jnp.sum(jnp.exp(gl - gmax), axis=0, keepdims=True)
    giota = lax.broadcasted_iota(jnp.int32, gl.shape, 0)
    g_sel = jnp.min(jnp.where(gl == gmax, giota, N_GROUPS), axis=0, keepdims=True)
    el = lt[0:N_EXPERTS]
    eiota = lax.broadcasted_iota(jnp.int32, el.shape, 0)
    in_grp = (eiota // EXP_PER_GROUP) == g_sel
    elm = jnp.where(in_grp, el, -jnp.inf)
    m1 = jnp.max(elm, axis=0, keepdims=True)
    esum = jnp.sum(jnp.where(in_grp, jnp.exp(el - m1), 0.0), axis=0, keepdims=True)
    i1 = jnp.min(jnp.where(elm == m1, eiota, N_EXPERTS), axis=0, keepdims=True)
    elm2 = jnp.where(eiota == i1, -jnp.inf, elm)
    m2 = jnp.max(elm2, axis=0, keepdims=True)
    i2 = jnp.min(jnp.where(elm2 == m2, eiota, N_EXPERTS), axis=0, keepdims=True)
    p1 = 1.0 / esum
    p2 = jnp.exp(m2 - m1) / esum
    psum = p1 + p2
    w1 = g_w * (p1 / psum)
    w2 = g_w * (p2 / psum)

    oh1 = (eiota == i1).astype(F32)
    oh2 = (eiota == i2).astype(F32)
    cnt = oh1 + oh2
    base = carry_ref[:, 0:1] + _dot(cnt.astype(BF16), sut_ref[...])
    r1 = jnp.sum(oh1 * base, axis=0, keepdims=True)
    r2 = jnp.sum(oh2 * base, axis=0, keepdims=True)
    new_carry = carry_ref[...] + jnp.sum(cnt, axis=1, keepdims=True)
    carry_ref[...] = new_carry
    cnt_ref[...] = new_carry

    zi = jnp.zeros((4, tm), jnp.int32)
    ri_ref[...] = jnp.concatenate([i1, i2, r1.astype(jnp.int32), r2.astype(jnp.int32), zi], axis=0)
    rw_ref[...] = jnp.concatenate([w1, w2, jnp.zeros((6, tm), F32)], axis=0)


def _attn_route(xf, mixed, kmem, vmem, w_out, norm_xa_w, wq, wo, norm_moe_w, w_route_t, b_route, sut,
                batch, seq):
    n, d = xf.shape
    tm = TOKEN_TILE
    nt = seq // tm
    m = kmem.shape[1]
    tok = lambda b, t: (b * nt + t, 0)
    lane_tok = lambda b, t: (0, b * nt + t)
    const2 = lambda b, t: (0, 0)
    return pl.pallas_call(
        _attn_route_kernel,
        grid=(batch, nt),
        in_specs=[
            pl.BlockSpec((tm, d), tok),
            pl.BlockSpec((tm, d), tok),
            pl.BlockSpec((1, m, d), lambda b, t: (b, 0, 0)),
            pl.BlockSpec((1, m, d), lambda b, t: (b, 0, 0)),
            pl.BlockSpec((d, d), const2),
            pl.BlockSpec((1, d), const2),
            pl.BlockSpec((d, d), const2),
            pl.BlockSpec((d, d), const2),
            pl.BlockSpec((1, d), const2),
            pl.BlockSpec((ROUTE_ROWS, d), const2),
            pl.BlockSpec((ROUTE_ROWS, 1), const2),
            pl.BlockSpec((tm, tm), const2),
        ],
        out_specs=[
            pl.BlockSpec((tm, d), tok),
            pl.BlockSpec((tm, d), tok),
            pl.BlockSpec((8, tm), lane_tok),
            pl.BlockSpec((8, tm), lane_tok),
            pl.BlockSpec((N_EXPERTS, 128), const2),
        ],
        out_shape=[
            jax.ShapeDtypeStruct((n, d), F32),
            jax.ShapeDtypeStruct((n, d), BF16),
            jax.ShapeDtypeStruct((8, n), jnp.int32),
            jax.ShapeDtypeStruct((8, n), F32),
            jax.ShapeDtypeStruct((N_EXPERTS, 128), F32),
        ],
        scratch_shapes=[pltpu.VMEM((N_EXPERTS, 128), F32)],
        compiler_params=pltpu.CompilerParams(
            dimension_semantics=("arbitrary", "arbitrary"), vmem_limit_bytes=VMEM_LIMIT),
        name="attn_route",
    )(xf, mixed, kmem, vmem, w_out, norm_xa_w, wq, wo, norm_moe_w, w_route_t, b_route, sut)


def _expert_kernel(blk_e_ref, nused_ref, xs_ref, wg_ref, wu_ref, wd_ref, ys_ref, wg_b, wu_b, wd_b):
    i = pl.program_id(0)
    prev = blk_e_ref[jnp.maximum(i - 1, 0)]
    fresh = (i == 0) | (blk_e_ref[i] != prev)

    @pl.when(fresh)
    def _():
        wg_b[...] = wg_ref[0].astype(BF16)
        wu_b[...] = wu_ref[0].astype(BF16)
        wd_b[...] = wd_ref[0].astype(BF16)

    @pl.when(i < nused_ref[0])
    def _():
        xb = xs_ref[...]
        hid = _silu(_dot(xb, wg_b[...])) * _dot(xb, wu_b[...])
        ys_ref[...] = _dot(hid.astype(BF16), wd_b[...]).astype(BF16)

    @pl.when(i >= nused_ref[0])
    def _():
        ys_ref[...] = jnp.zeros_like(ys_ref)


def _experts(xs, blk_e, nused, w_gate, w_up, w_down):
    cap, d = xs.shape
    de = w_gate.shape[-1]
    bm = MOE_ROWS
    grid_spec = pltpu.PrefetchScalarGridSpec(
        num_scalar_prefetch=2,
        grid=(cap // bm,),
        in_specs=[
            pl.BlockSpec((bm, d), lambda i, be, nu: (i, 0)),
            pl.BlockSpec((1, d, de), lambda i, be, nu: (be[i], 0, 0)),
            pl.BlockSpec((1, d, de), lambda i, be, nu: (be[i], 0, 0)),
            pl.BlockSpec((1, de, d), lambda i, be, nu: (be[i], 0, 0)),
        ],
        out_specs=pl.BlockSpec((bm, d), lambda i, be, nu: (i, 0)),
        scratch_shapes=[
            pltpu.VMEM((d, de), BF16),
            pltpu.VMEM((d, de), BF16),
            pltpu.VMEM((de, d), BF16),
        ],
    )
    return pl.pallas_call(
        _expert_kernel,
        grid_spec=grid_spec,
        out_shape=jax.ShapeDtypeStruct((cap, d), BF16),
        compiler_params=pltpu.CompilerParams(
            dimension_semantics=("arbitrary",), vmem_limit_bytes=VMEM_LIMIT),
        name="experts",
    )(blk_e, nused, xs, w_gate, w_up, w_down)


def _combine_kernel(x2_ref, y_ref, rw_ref, eye_ref, nw_ref, o_ref):
    wcol = _dot_nt(eye_ref[...], rw_ref[...], precision=HIGHEST)
    y = y_ref[0].astype(F32) * wcol[:, 0:1] + y_ref[1].astype(F32) * wcol[:, 1:2]
    o_ref[...] = _rms(x2_ref[...] + y, nw_ref[...])


def _combine(x2, yg, rw, eye, norm_w):
    n, d = x2.shape
    tm = TOKEN_TILE
    return pl.pallas_call(
        _combine_kernel,
        grid=(n // tm,),
        in_specs=[
            pl.BlockSpec((tm, d), lambda i: (i, 0)),
            pl.BlockSpec((TOP_K, tm, d), lambda i: (0, i, 0)),
            pl.BlockSpec((8, tm), lambda i: (0, i)),
            pl.BlockSpec((tm, tm), lambda i: (0, 0)),
            pl.BlockSpec((1, d), lambda i: (0, 0)),
        ],
        out_specs=pl.BlockSpec((tm, d), lambda i: (i, 0)),
        out_shape=jax.ShapeDtypeStruct((n, d), F32),
        compiler_params=pltpu.CompilerParams(
            dimension_semantics=("arbitrary",), vmem_limit_bytes=VMEM_LIMIT),
        name="combine",
    )(x2, yg, rw, eye, norm_w)


def _rope_column_order():
    half = RET_DK // 2
    first = [h * RET_DK + j for h in range(RET_HEADS) for j in range(half)]
    second = [h * RET_DK + half + j for h in range(RET_HEADS) for j in range(half)]
    return np.array(first + second, dtype=np.int32)


def _layer(xf, mem, batch, seq, norm_mix_w, w_in, ret_norm_w, ml_conv_w, ml_conv_b, ml_gate_b, ml_norm_w,
           w_out, norm_xa_w, norm_mem_w, xa_wq, xa_wkv, xa_wo, norm_moe_w, moe_w_group, moe_b_group,
           moe_w_router, moe_b_router, moe_w_gate, moe_w_up, moe_w_down, final_norm_w):
    n, d = xf.shape
    perm = _rope_column_order()
    cols = np.concatenate([perm, RET_QK + perm, np.arange(2 * RET_QK, MAIN_WIDTH)])
    w_main = w_in[:, cols].astype(BF16)
    w_if = w_in[:, MAIN_WIDTH:].astype(BF16)
    proj, g, gt = _inproj(xf, norm_mix_w.reshape(1, d), w_main, w_if, w_if.T)

    tabs = {k_: jnp.asarray(v_) for k_, v_ in _mixer_tables(seq).items()}
    mixed = _mixer(proj, g, gt, tabs, ret_norm_w.reshape(1, RET_V), ml_norm_w.reshape(1, ML_V), ml_conv_w,
                   ml_conv_b.reshape(1, 2 * ML_QK), ml_gate_b, batch, seq)

    kmem, vmem = _memkv(mem, norm_mem_w.reshape(1, d), xa_wkv.astype(BF16))

    w_route_t = jnp.concatenate(
        [moe_w_router.T, moe_w_group.T, jnp.zeros((ROUTE_ROWS - N_EXPERTS - N_GROUPS, d), F32)], axis=0)
    b_route = jnp.concatenate(
        [moe_b_router, moe_b_group, jnp.zeros((ROUTE_ROWS - N_EXPERTS - N_GROUPS,), F32)]).reshape(ROUTE_ROWS, 1)
    tm = TOKEN_TILE
    sut = jnp.asarray(np.triu(np.ones((tm, tm), np.float32), 1), dtype=BF16)
    x2, h3, ri, rw, cnt = _attn_route(xf, mixed, kmem, vmem, w_out.astype(BF16), norm_xa_w.reshape(1, d),
                                      xa_wq.astype(BF16), xa_wo.astype(BF16), norm_moe_w.reshape(1, d),
                                      w_route_t, b_route, sut, batch, seq)

    bm = MOE_ROWS
    counts = cnt[:, 0].astype(jnp.int32)
    padded = (counts + bm - 1) // bm * bm
    pends = jnp.cumsum(padded)
    pstarts = pends - padded
    expert = ri[0:TOP_K]
    dest = pstarts[expert] + ri[TOP_K:2 * TOP_K]
    cap = n * TOP_K + N_EXPERTS * bm
    nblk = cap // bm
    tok_ids = jnp.broadcast_to(jnp.arange(n, dtype=jnp.int32)[None, :], (TOP_K, n))
    row_tok = jnp.zeros((cap,), jnp.int32).at[dest.reshape(-1)].set(tok_ids.reshape(-1))
    blk_start = jnp.arange(nblk, dtype=jnp.int32) * bm
    blk_e = jnp.minimum(jnp.sum(blk_start[:, None] >= pends[None, :], axis=1), N_EXPERTS - 1).astype(jnp.int32)
    nused = (pends[-1] // bm).astype(jnp.int32).reshape(1)

    xs = h3[row_tok]
    ys = _experts(xs, blk_e, nused, moe_w_gate, moe_w_up, moe_w_down)
    yg = ys[dest]
    eye = jnp.asarray(np.eye(tm, dtype=np.float32))
    return _combine(x2, yg, rw, eye, final_norm_w.reshape(1, d))


def kernel(x, mem, norm_mix_w, w_in, ret_norm_w, ml_conv_w, ml_conv_b, ml_gate_b, ml_norm_w, w_out, norm_xa_w, norm_mem_w, xa_wq, xa_wkv, xa_wo, norm_moe_w, moe_w_group, moe_b_group, moe_w_router, moe_b_router, moe_w_gate, moe_w_up, moe_w_down, norm_final_w):
    batch, seq, d = x.shape
    depth = w_in.shape[0]
    assert depth == 1, "the final norm is fused into the last layer's combine kernel"
    l = 0
    out = _layer(x.reshape(batch * seq, d), mem, batch, seq, norm_mix_w[l], w_in[l], ret_norm_w[l], ml_conv_w[l],
                 ml_conv_b[l], ml_gate_b[l], ml_norm_w[l], w_out[l], norm_xa_w[l], norm_mem_w[l], xa_wq[l],
                 xa_wkv[l], xa_wo[l], norm_moe_w[l], moe_w_group[l], moe_b_group[l], moe_w_router[l],
                 moe_b_router[l], moe_w_gate[l], moe_w_up[l], moe_w_down[l], norm_final_w)
    return out.reshape(batch, seq, d)
```

```python
import functools

import numpy as np
import jax
import jax.numpy as jnp
from jax import lax
from jax.experimental import pallas as pl
from jax.experimental.pallas import tpu as pltpu

F32 = jnp.float32
BF16 = jnp.bfloat16
HIGHEST = lax.Precision.HIGHEST

CHUNK = 128
RET_HEADS = 4
RET_DK = 64
RET_DV = 128
ML_HEADS = 4
ML_DK = 128
ML_DV = 128
CONV_W = 4
XA_HEADS = 4
N_GROUPS = 4
EXP_PER_GROUP = 8
N_EXPERTS = N_GROUPS * EXP_PER_GROUP
TOP_K = 2
ROPE_BASE = 10000.0
EPS = 1e-6

RET_QK = RET_HEADS * RET_DK
RET_V = RET_HEADS * RET_DV
ML_QK = ML_HEADS * ML_DK
ML_V = ML_HEADS * ML_DV
OFF_RQ = 0
OFF_RK = OFF_RQ + RET_QK
OFF_RV = OFF_RK + RET_QK
OFF_RG = OFF_RV + RET_V
OFF_MQK = OFF_RG + RET_V
OFF_MV = OFF_MQK + 2 * ML_QK
OFF_MO = OFF_MV + ML_V
MAIN_WIDTH = OFF_MO + ML_V
N_GATES = 2 * ML_HEADS

ROUTE_ROWS = 40
TOKEN_TILE = 512
MOE_ROWS = 256
DISPATCH_TILE = 1024
VMEM_LIMIT = 56 * 1024 * 1024


def _dot(a, b):
    return jnp.dot(a, b, preferred_element_type=F32)


def _dot_nt(a, b, precision=None):
    return lax.dot_general(a, b, (((1,), (1,)), ((), ())), preferred_element_type=F32, precision=precision)


def _dot_tn(a, b):
    return lax.dot_general(a, b, (((0,), (0,)), ((), ())), preferred_element_type=F32)


def _rms(x, w):
    return x * lax.rsqrt(jnp.mean(x * x, axis=-1, keepdims=True) + EPS) * w


def _sigmoid(x):
    return 1.0 / (1.0 + jnp.exp(-x))


def _silu(x):
    return x * _sigmoid(x)


def _log_sigmoid(x):
    return jnp.minimum(x, 0.0) - jnp.log1p(jnp.exp(-jnp.abs(x)))


def _head_norm(h):
    mu = jnp.mean(h, axis=-1, keepdims=True)
    d = h - mu
    var = jnp.mean(d * d, axis=-1, keepdims=True)
    return d * lax.rsqrt(var + EPS)


def _pack_rows(lo, hi):
    def rne(t):
        b = lax.bitcast_convert_type(t, jnp.uint32)
        return b + jnp.uint32(0x7FFF) + ((b >> 16) & jnp.uint32(1))
    return (rne(lo) >> 16) | (rne(hi) & jnp.uint32(0xFFFF0000))


def _unpack_rows(u):
    lo = lax.bitcast_convert_type(u << 16, F32)
    hi = lax.bitcast_convert_type(u & jnp.uint32(0xFFFF0000), F32)
    return lo, hi


def _inproj_kernel(x_ref, nw_ref, w_ref, wif_ref, wift_ref, proj_ref, g_ref, gt_ref):
    h = _rms(x_ref[...], nw_ref[...]).astype(BF16)
    for j in range(0, MAIN_WIDTH, 512):
        proj_ref[:, j:j + 512] = _dot(h, w_ref[:, j:j + 512]).astype(BF16)
    g_ref[...] = _dot(h, wif_ref[...])
    gt_ref[...] = _dot_nt(wift_ref[...], h)


def _inproj(xf, norm_w, w_main, w_if, w_ift):
    n, d = xf.shape
    tm = TOKEN_TILE
    return pl.pallas_call(
        _inproj_kernel,
        grid=(n // tm,),
        in_specs=[
            pl.BlockSpec((tm, d), lambda i: (i, 0)),
            pl.BlockSpec((1, d), lambda i: (0, 0)),
            pl.BlockSpec((d, MAIN_WIDTH), lambda i: (0, 0)),
            pl.BlockSpec((d, N_GATES), lambda i: (0, 0)),
            pl.BlockSpec((N_GATES, d), lambda i: (0, 0)),
        ],
        out_specs=[
            pl.BlockSpec((tm, MAIN_WIDTH), lambda i: (i, 0)),
            pl.BlockSpec((tm, N_GATES), lambda i: (i, 0)),
            pl.BlockSpec((N_GATES, tm), lambda i: (0, i)),
        ],
        out_shape=[
            jax.ShapeDtypeStruct((n, MAIN_WIDTH), BF16),
            jax.ShapeDtypeStruct((n, N_GATES), F32),
            jax.ShapeDtypeStruct((N_GATES, n), F32),
        ],
        compiler_params=pltpu.CompilerParams(
            dimension_semantics=("arbitrary",), vmem_limit_bytes=VMEM_LIMIT),
        name="inproj",
    )(xf, norm_w, w_main, w_if, w_ift)


def _mixer_kernel(proj_ref, g_ref, gt_ref, cos_ref, sin_ref, qdec_ref, kdec_ref, dmat_ref,
                  bmask_ref, cdec_ref, hmask_ref, tril_ref, triu_ref, retw_ref, mlw_ref,
                  convw_ref, convb_ref, gbc_ref, gbr_ref,
                  out_ref, r_ref, c_ref, n_ref, m_ref, tail_ref):
    L = CHUNK

    @pl.when(pl.program_id(1) == 0)
    def _():
        r_ref[...] = jnp.zeros_like(r_ref)
        c_ref[...] = jnp.zeros_like(c_ref)
        n_ref[...] = jnp.zeros_like(n_ref)
        m_ref[...] = jnp.zeros_like(m_ref)
        tail_ref[...] = jnp.zeros_like(tail_ref)

    cos = cos_ref[...]
    sin = sin_ref[...]
    half = RET_QK // 2

    def rope(t):
        t1, t2 = t[:, :half], t[:, half:]
        return jnp.concatenate([t1 * cos - t2 * sin, t1 * sin + t2 * cos], axis=1)

    q = rope(proj_ref[:, OFF_RQ:OFF_RQ + RET_QK].astype(F32))
    k = rope(proj_ref[:, OFF_RK:OFF_RK + RET_QK].astype(F32)) * (RET_DK ** -0.5)
    v = proj_ref[:, OFF_RV:OFF_RV + RET_V]
    k_b = k.astype(BF16)
    r_prev = r_ref[...]
    cross = _dot((q * qdec_ref[...]).astype(BF16), r_prev.astype(BF16))
    kv = _dot_tn((k * kdec_ref[...]).astype(BF16), v) * bmask_ref[...]
    r_ref[...] = cdec_ref[...] * r_prev + kv
    ret_heads = []
    for h in range(RET_HEADS):
        qh = (q * hmask_ref[h:h + 1, :]).astype(BF16)
        sc = _dot_nt(qh, k_b) * dmat_ref[h]
        inner = _dot(sc.astype(BF16), v[:, h * RET_DV:(h + 1) * RET_DV])
        ret_heads.append(_head_norm(inner + cross[:, h * RET_DV:(h + 1) * RET_DV]))
    ret = jnp.concatenate(ret_heads, axis=1) * retw_ref[...]
    ret = ret * _silu(proj_ref[:, OFF_RG:OFF_RG + RET_V].astype(F32))
    out_ref[:, 0:RET_V] = ret.astype(BF16)

    xqk = proj_ref[:, OFF_MQK:OFF_MQK + 2 * ML_QK].astype(F32)
    tail = tail_ref[...]
    row8 = lax.broadcasted_iota(jnp.int32, (8, 2 * ML_QK), 0)
    acc = xqk * convw_ref[CONV_W - 1:CONV_W, :] + convb_ref[...]
    for s in range(1, CONV_W):
        rolled = pltpu.roll(xqk, s, 0)
        head8 = jnp.where(row8 < s, pltpu.roll(tail, s, 0), rolled[0:8])
        shifted = jnp.concatenate([head8, rolled[8:]], axis=0)
        acc = acc + shifted * convw_ref[CONV_W - 1 - s:CONV_W - s, :]
    tail_ref[...] = xqk[L - 8:L]
    qk = _silu(acc)
    mq = qk[:, :ML_QK]
    mk = qk[:, ML_QK:] * (ML_DK ** -0.5)
    mv = proj_ref[:, OFF_MV:OFF_MV + ML_V]

    gc = g_ref[...] + gbc_ref[...]
    gr = gt_ref[...] + gbr_ref[...]
    b_c = jnp.dot(tril_ref[...], _log_sigmoid(gc), preferred_element_type=F32, precision=HIGHEST)
    b_r = jnp.dot(_log_sigmoid(gr), triu_ref[...], preferred_element_type=F32, precision=HIGHEST)
    causal = (lax.broadcasted_iota(jnp.int32, (L, L), 0) >= lax.broadcasted_iota(jnp.int32, (L, L), 1))
    ml_heads = []
    for h in range(ML_HEADS):
        bc = b_c[:, ML_HEADS + h:ML_HEADS + h + 1]
        br = b_r[ML_HEADS + h:ML_HEADS + h + 1, :]
        igc = gc[:, h:h + 1]
        igr = gr[h:h + 1, :]
        btot = br[:, L - 1:L]
        log_d = jnp.where(causal, bc - br + igr, -jnp.inf)
        m_intra = jnp.max(log_d, axis=1, keepdims=True)
        m_loc = jnp.max(btot - br + igr, axis=1, keepdims=True)
        wa = jnp.exp(btot - bc + igc - m_loc)
        qh = mq[:, h * ML_DK:(h + 1) * ML_DK]
        kh = mk[:, h * ML_DK:(h + 1) * ML_DK]
        vh = mv[:, h * ML_DV:(h + 1) * ML_DV]
        kw = kh * wa
        kv_loc = _dot_tn(kw.astype(BF16), vh)
        n_loc = jnp.sum(kw, axis=0, keepdims=True)
        c_prev = c_ref[h]
        n_prev = n_ref[h][0:1, :]
        m_prev = m_ref[h][0:1, 0:1]
        m_inter = bc + m_prev
        m_t = jnp.maximum(m_intra, m_inter)
        qh_b = qh.astype(BF16)
        s_mat = _dot_nt(qh_b, kh.astype(BF16)) * jnp.exp(log_d - m_t)
        inter = jnp.exp(m_inter - m_t)
        num = _dot(s_mat.astype(BF16), vh) + inter * _dot(qh_b, c_prev.astype(BF16))
        den = jnp.sum(s_mat, axis=1, keepdims=True) + inter * jnp.sum(qh * n_prev, axis=1, keepdims=True)
        hh = num / jnp.maximum(jnp.abs(den), jnp.exp(-m_t))
        ml_heads.append(_head_norm(hh))
        m_new = jnp.maximum(btot + m_prev, m_loc)
        s_old = jnp.exp(btot + m_prev - m_new)
        s_loc = jnp.exp(m_loc - m_new)
        c_ref[h] = s_old * c_prev + s_loc * kv_loc
        n_ref[h] = jnp.broadcast_to(s_old * n_prev + s_loc * n_loc, (8, ML_DK))
        m_ref[h] = jnp.broadcast_to(m_new, (8, 128))
    ml = jnp.concatenate(ml_heads, axis=1) * mlw_ref[...]
    ml = ml * _sigmoid(proj_ref[:, OFF_MO:OFF_MO + ML_V].astype(F32))
    out_ref[:, RET_V:RET_V + ML_V] = ml.astype(BF16)


def _mixer_tables(seq):
    L = CHUNK
    half = RET_DK // 2
    inv = ROPE_BASE ** (-np.arange(half, dtype=np.float64) / half)
    ang = np.arange(seq, dtype=np.float64)[:, None] * inv[None, :].astype(np.float32).astype(np.float64)
    cos = np.tile(np.cos(ang), (1, RET_HEADS)).astype(np.float32)
    sin = np.tile(np.sin(ang), (1, RET_HEADS)).astype(np.float32)
    log_g = np.log1p(-np.exp2(-5.0 - np.arange(RET_HEADS, dtype=np.float64)))
    n = np.arange(L, dtype=np.float64)
    lane_head = (np.arange(RET_QK) % (RET_QK // 2)) // half
    qdec = np.exp((n + 1)[:, None] * log_g[lane_head][None, :]).astype(np.float32)
    kdec = np.exp((L - 1 - n)[:, None] * log_g[lane_head][None, :]).astype(np.float32)
    diff = n[:, None] - n[None, :]
    dmat = np.where(diff >= 0, np.exp(log_g[:, None, None] * np.maximum(diff, 0.0)[None]), 0.0).astype(np.float32)
    col_head = np.arange(RET_V) // RET_DV
    bmask = (lane_head[:, None] == col_head[None, :]).astype(np.float32)
    cdec = np.exp(L * log_g[col_head])[None, :].astype(np.float32)
    hmask = (lane_head[None, :] == np.arange(RET_HEADS)[:, None]).astype(np.float32)
    hmask = np.concatenate([hmask, np.zeros((8 - RET_HEADS, RET_QK), np.float32)], axis=0)
    tril = np.tril(np.ones((L, L), np.float32))
    return dict(cos=cos, sin=sin, qdec=qdec, kdec=kdec, dmat=dmat, bmask=bmask, cdec=cdec, hmask=hmask,
                tril=tril, triu=np.ascontiguousarray(tril.T))


def _mixer(proj, g, gt, tabs, ret_norm_w, ml_norm_w, conv_w, conv_b, gate_b, batch, seq):
    L = CHUNK
    nc = seq // L
    n = batch * seq
    const2 = lambda b, c: (0, 0)
    tok = lambda b, c: (b * nc + c, 0)
    in_specs = [
        pl.BlockSpec((L, MAIN_WIDTH), tok),
        pl.BlockSpec((L, N_GATES), tok),
        pl.BlockSpec((N_GATES, L), lambda b, c: (0, b * nc + c)),
        pl.BlockSpec((L, RET_QK // 2), lambda b, c: (c, 0)),
        pl.BlockSpec((L, RET_QK // 2), lambda b, c: (c, 0)),
        pl.BlockSpec((L, RET_QK), const2),
        pl.BlockSpec((L, RET_QK), const2),
        pl.BlockSpec((RET_HEADS, L, L), lambda b, c: (0, 0, 0)),
        pl.BlockSpec((RET_QK, RET_V), const2),
        pl.BlockSpec((1, RET_V), const2),
        pl.BlockSpec((8, RET_QK), const2),
        pl.BlockSpec((L, L), const2),
        pl.BlockSpec((L, L), const2),
        pl.BlockSpec((1, RET_V), const2),
        pl.BlockSpec((1, ML_V), const2),
        pl.BlockSpec((CONV_W, 2 * ML_QK), const2),
        pl.BlockSpec((1, 2 * ML_QK), const2),
        pl.BlockSpec((1, N_GATES), const2),
        pl.BlockSpec((N_GATES, 1), const2),
    ]
    return pl.pallas_call(
        _mixer_kernel,
        grid=(batch, nc),
        in_specs=in_specs,
        out_specs=pl.BlockSpec((L, RET_V + ML_V), tok),
        out_shape=jax.ShapeDtypeStruct((n, RET_V + ML_V), BF16),
        scratch_shapes=[
            pltpu.VMEM((RET_QK, RET_V), F32),
            pltpu.VMEM((ML_HEADS, ML_DK, ML_DV), F32),
            pltpu.VMEM((ML_HEADS, 8, ML_DK), F32),
            pltpu.VMEM((ML_HEADS, 8, 128), F32),
            pltpu.VMEM((8, 2 * ML_QK), F32),
        ],
        compiler_params=pltpu.CompilerParams(
            dimension_semantics=("arbitrary", "arbitrary"), vmem_limit_bytes=VMEM_LIMIT),
        name="mixer",
    )(proj, g, gt, tabs["cos"], tabs["sin"], tabs["qdec"], tabs["kdec"], tabs["dmat"], tabs["bmask"],
      tabs["cdec"], tabs["hmask"], tabs["tril"], tabs["triu"], ret_norm_w, ml_norm_w, conv_w, conv_b,
      gate_b.reshape(1, N_GATES), gate_b.reshape(N_GATES, 1))


def _memkv_kernel(mem_ref, nw_ref, wkv_ref, k_ref, v_ref):
    d = mem_ref.shape[-1]
    mn = _rms(mem_ref[0], nw_ref[...]).astype(BF16)
    k_ref[0] = _dot(mn, wkv_ref[:, :d]).astype(BF16)
    v_ref[0] = _dot(mn, wkv_ref[:, d:]).astype(BF16)


def _memkv(mem, norm_w, wkv):
    b, m, d = mem.shape
    return pl.pallas_call(
        _memkv_kernel,
        grid=(b,),
        in_specs=[
            pl.BlockSpec((1, m, d), lambda i: (i, 0, 0)),
            pl.BlockSpec((1, d), lambda i: (0, 0)),
            pl.BlockSpec((d, 2 * d), lambda i: (0, 0)),
        ],
        out_specs=[pl.BlockSpec((1, m, d), lambda i: (i, 0, 0))] * 2,
        out_shape=[jax.ShapeDtypeStruct((b, m, d), BF16)] * 2,
        compiler_params=pltpu.CompilerParams(
            dimension_semantics=("arbitrary",), vmem_limit_bytes=VMEM_LIMIT),
        name="memkv",
    )(mem, norm_w, wkv)


def _attn_route_kernel(x_ref, mix_ref, k_ref, v_ref, wout_ref, nxa_ref, wq_ref, wo_ref, nmoe_ref,
                       wr_ref, br_ref, sut_ref,
                       x2_ref, h3_ref, ri_ref, rw_ref, cnt_ref, carry_ref):
    tm, d = x_ref.shape
    dh = d // XA_HEADS

    @pl.when((pl.program_id(0) == 0) & (pl.program_id(1) == 0))
    def _():
        carry_ref[...] = jnp.zeros_like(carry_ref)

    x1 = x_ref[...] + _dot(mix_ref[...], wout_ref[...])
    h2 = _rms(x1, nxa_ref[...]).astype(BF16)
    q = _dot(h2, wq_ref[...]).astype(BF16)
    heads = []
    for h in range(XA_HEADS):
        logits = _dot_nt(q[:, h * dh:(h + 1) * dh], k_ref[0, :, h * dh:(h + 1) * dh]) * (dh ** -0.5)
        mx = jnp.max(logits, axis=-1, keepdims=True)
        e = jnp.exp(logits - mx)
        p = (e / jnp.sum(e, axis=-1, keepdims=True)).astype(BF16)
        heads.append(_dot(p, v_ref[0, :, h * dh:(h + 1) * dh]).astype(BF16))
    x2 = x1 + _dot(jnp.concatenate(heads, axis=1), wo_ref[...])
    x2_ref[...] = x2
    h3 = _rms(x2, nmoe_ref[...])
    h3_ref[...] = _pack_rows(h3[:, :d // 2], h3[:, d // 2:])

    lt = _dot_nt(wr_ref[...], h3, precision=HIGHEST) + br_ref[...]
    gl = lt[N_EXPERTS:N_EXPERTS + N_GROUPS]
    gmax = jnp.max(gl, axis=0, keepdims=True)
    g_w = 1.0 / jnp.sum(jnp.exp(gl - gmax), axis=0, keepdims=True)
    giota = lax.broadcasted_iota(jnp.int32, gl.shape, 0)
    g_sel = jnp.min(jnp.where(gl == gmax, giota, N_GROUPS), axis=0, keepdims=True)
    el = lt[0:N_EXPERTS]
    eiota = lax.broadcasted_iota(jnp.int32, el.shape, 0)
    in_grp = (eiota // EXP_PER_GROUP) == g_sel
    elm = jnp.where(in_grp, el, -jnp.inf)
    m1 = jnp.max(elm, axis=0, keepdims=True)
    esum = jnp.sum(jnp.where(in_grp, jnp.exp(el - m1), 0.0), axis=0, keepdims=True)
    i1 = jnp.min(jnp.where(elm == m1, eiota, N_EXPERTS), axis=0, keepdims=True)
    elm2 = jnp.where(eiota == i1, -jnp.inf, elm)
    m2 = jnp.max(elm2, axis=0, keepdims=True)
    i2 = jnp.min(jnp.where(elm2 == m2, eiota, N_EXPERTS), axis=0, keepdims=True)
    p1 = 1.0 / esum
    p2 = jnp.exp(m2 - m1) / esum
    psum = p1 + p2
    w1 = g_w * (p1 / psum)
    w2 = g_w * (p2 / psum)

    oh1 = (eiota == i1).astype(F32)
    oh2 = (eiota == i2).astype(F32)
    cnt = oh1 + oh2
    base = carry_ref[:, 0:1] + _dot(cnt.astype(BF16), sut_ref[...])
    r1 = jnp.sum(oh1 * base, axis=0, keepdims=True)
    r2 = jnp.sum(oh2 * base, axis=0, keepdims=True)
    new_carry = carry_ref[...] + jnp.sum(cnt, axis=1, keepdims=True)
    carry_ref[...] = new_carry
    cnt_ref[...] = new_carry

    zi = jnp.zeros((4, tm), jnp.int32)
    ri_ref[...] = jnp.concatenate([i1, i2, r1.astype(jnp.int32), r2.astype(jnp.int32), zi], axis=0)
    rw_ref[...] = jnp.concatenate([w1, w2, jnp.zeros((6, tm), F32)], axis=0)


def _attn_route(xf, mixed, kmem, vmem, w_out, norm_xa_w, wq, wo, norm_moe_w, w_route_t, b_route, sut,
                batch, seq):
    n, d = xf.shape
    tm = TOKEN_TILE
    nt = seq // tm
    m = kmem.shape[1]
    tok = lambda b, t: (b * nt + t, 0)
    lane_tok = lambda b, t: (0, b * nt + t)
    const2 = lambda b, t: (0, 0)
    return pl.pallas_call(
        _attn_route_kernel,
        grid=(batch, nt),
        in_specs=[
            pl.BlockSpec((tm, d), tok),
            pl.BlockSpec((tm, d), tok),
            pl.BlockSpec((1, m, d), lambda b, t: (b, 0, 0)),
            pl.BlockSpec((1, m, d), lambda b, t: (b, 0, 0)),
            pl.BlockSpec((d, d), const2),
            pl.BlockSpec((1, d), const2),
            pl.BlockSpec((d, d), const2),
            pl.BlockSpec((d, d), const2),
            pl.BlockSpec((1, d), const2),
            pl.BlockSpec((ROUTE_ROWS, d), const2),
            pl.BlockSpec((ROUTE_ROWS, 1), const2),
            pl.BlockSpec((tm, tm), const2),
        ],
        out_specs=[
            pl.BlockSpec((tm, d), tok),
            pl.BlockSpec((tm, d // 2), tok),
            pl.BlockSpec((8, tm), lane_tok),
            pl.BlockSpec((8, tm), lane_tok),
            pl.BlockSpec((N_EXPERTS, 128), const2),
        ],
        out_shape=[
            jax.ShapeDtypeStruct((n, d), F32),
            jax.ShapeDtypeStruct((n, d // 2), jnp.uint32),
            jax.ShapeDtypeStruct((8, n), jnp.int32),
            jax.ShapeDtypeStruct((8, n), F32),
            jax.ShapeDtypeStruct((N_EXPERTS, 128), F32),
        ],
        scratch_shapes=[pltpu.VMEM((N_EXPERTS, 128), F32)],
        compiler_params=pltpu.CompilerParams(
            dimension_semantics=("arbitrary", "arbitrary"), vmem_limit_bytes=VMEM_LIMIT),
        name="attn_route",
    )(xf, mixed, kmem, vmem, w_out, norm_xa_w, wq, wo, norm_moe_w, w_route_t, b_route, sut)


def _row_copy(src, s, dst, d, sem):
    return pltpu.make_async_copy(src.at[pl.ds(s, 1)], dst.at[pl.ds(d, 1)], sem)


def _dispatch_kernel(zpos_ref, dest_ref, h_ref, xs_ref, idx_ref, idx_sem, row_sem, zero_ref, zero_sem):
    i = pl.program_id(0)
    nsteps = pl.num_programs(0)
    td = h_ref.shape[0]
    bm = zero_ref.shape[0]
    slot = i % 2

    def idx_copy(step, sl):
        off = pl.multiple_of(sl * (2 * td), 2 * td)
        return pltpu.make_async_copy(dest_ref.at[step], idx_ref.at[pl.ds(off, 2 * td)], idx_sem.at[sl])

    @pl.when(i == 0)
    def _():
        zero_ref[...] = jnp.zeros_like(zero_ref)

        def zero_copy(e):
            return pltpu.make_async_copy(zero_ref, xs_ref.at[pl.ds(pl.multiple_of(zpos_ref[e], bm), bm)], zero_sem)

        def tail_copy(b):
            return pltpu.make_async_copy(zero_ref, xs_ref.at[pl.ds(pl.multiple_of(b * bm, bm), bm)], zero_sem)

        nused = zpos_ref[N_EXPERTS]
        nblk = xs_ref.shape[0] // bm
        for e in range(N_EXPERTS):
            pl.when(zpos_ref[e] >= 0)(lambda e=e: zero_copy(e).start())
        lax.fori_loop(nused, nblk, lambda b, c: (tail_copy(b).start(), c)[1], 0)
        for e in range(N_EXPERTS):
            pl.when(zpos_ref[e] >= 0)(lambda e=e: zero_copy(e).wait())
        lax.fori_loop(nused, nblk, lambda b, c: (tail_copy(b).wait(), c)[1], 0)
        idx_copy(0, 0).start()

    idx_copy(i, slot).wait()

    @pl.when(i + 1 < nsteps)
    def _():
        idx_copy(i + 1, 1 - slot).start()

    base = slot * (2 * td)

    def issue(t, carry):
        _row_copy(h_ref, t, xs_ref, idx_ref[base + t], row_sem).start()
        _row_copy(h_ref, t, xs_ref, idx_ref[base + td + t], row_sem).start()
        return carry

    lax.fori_loop(0, td, issue, 0, unroll=8)
    for _ in range(TOP_K):
        pltpu.make_async_copy(h_ref, xs_ref.at[pl.ds(0, td)], row_sem).wait()


def _dispatch(h3p, dest_tiles, zpos, cap):
    n, w = h3p.shape
    nt, td2 = dest_tiles.shape
    td = td2 // 2
    grid_spec = pltpu.PrefetchScalarGridSpec(
        num_scalar_prefetch=1,
        grid=(nt,),
        in_specs=[
            pl.BlockSpec(memory_space=pl.ANY),
            pl.BlockSpec((td, w), lambda i, zp: (i, 0)),
        ],
        out_specs=pl.BlockSpec(memory_space=pl.ANY),
        scratch_shapes=[
            pltpu.SMEM((2 * td2,), jnp.int32),
            pltpu.SemaphoreType.DMA((2,)),
            pltpu.SemaphoreType.DMA,
            pltpu.VMEM((MOE_ROWS, w), jnp.uint32),
            pltpu.SemaphoreType.DMA,
        ],
    )
    return pl.pallas_call(
        _dispatch_kernel,
        grid_spec=grid_spec,
        out_shape=jax.ShapeDtypeStruct((cap, w), jnp.uint32),
        compiler_params=pltpu.CompilerParams(
            dimension_semantics=("arbitrary",), vmem_limit_bytes=VMEM_LIMIT),
        name="dispatch",
    )(zpos, dest_tiles, h3p)


def _expert_kernel(blk_e_ref, nused_ref, xs_ref, wg_ref, wu_ref, wd_ref, ys_ref, wg_b, wu_b, wd_b):
    i = pl.program_id(0)
    prev = blk_e_ref[jnp.maximum(i - 1, 0)]
    fresh = (i == 0) | (blk_e_ref[i] != prev)
    half = wd_b.shape[1] // 2

    @pl.when(fresh)
    def _():
        wg_b[...] = wg_ref[0].astype(BF16)
        wu_b[...] = wu_ref[0].astype(BF16)
        wd_b[...] = wd_ref[0].astype(BF16)

    @pl.when(i < nused_ref[0])
    def _():
        lo, hi = _unpack_rows(xs_ref[...])
        lo = lo.astype(BF16)
        hi = hi.astype(BF16)
        gate = _dot(lo, wg_b[:half, :]) + _dot(hi, wg_b[half:, :])
        up = _dot(lo, wu_b[:half, :]) + _dot(hi, wu_b[half:, :])
        hid = (_silu(gate) * up).astype(BF16)
        ys_ref[...] = _pack_rows(_dot(hid, wd_b[:, :half]), _dot(hid, wd_b[:, half:]))

    @pl.when(i >= nused_ref[0])
    def _():
        ys_ref[...] = jnp.zeros_like(ys_ref)


def _experts(xs, blk_e, nused, w_gate, w_up, w_down):
    cap, w = xs.shape
    _, d, de = w_gate.shape
    bm = MOE_ROWS
    used = lambda i, be, nu: (jnp.minimum(i, nu[0] - 1), 0)
    grid_spec = pltpu.PrefetchScalarGridSpec(
        num_scalar_prefetch=2,
        grid=(cap // bm,),
        in_specs=[
            pl.BlockSpec((bm, w), used),
            pl.BlockSpec((1, d, de), lambda i, be, nu: (be[i], 0, 0)),
            pl.BlockSpec((1, d, de), lambda i, be, nu: (be[i], 0, 0)),
            pl.BlockSpec((1, de, d), lambda i, be, nu: (be[i], 0, 0)),
        ],
        out_specs=pl.BlockSpec((bm, w), lambda i, be, nu: (i, 0)),
        scratch_shapes=[
            pltpu.VMEM((d, de), BF16),
            pltpu.VMEM((d, de), BF16),
            pltpu.VMEM((de, d), BF16),
        ],
    )
    return pl.pallas_call(
        _expert_kernel,
        grid_spec=grid_spec,
        out_shape=jax.ShapeDtypeStruct((cap, w), jnp.uint32),
        compiler_params=pltpu.CompilerParams(
            dimension_semantics=("arbitrary",), vmem_limit_bytes=VMEM_LIMIT),
        name="experts",
    )(blk_e, nused, xs, w_gate, w_up, w_down)


def _combine_kernel(dest_ref, ys_ref, x2_ref, rw_ref, eye_ref, nw_ref, o_ref, idx_ref, idx_sem, ybuf, ysem):
    i = pl.program_id(0)
    nsteps = pl.num_programs(0)
    tc, d = x2_ref.shape
    half = d // 2
    n_idx = 2 * tc

    def idx_copy(step):
        sl = step % 3
        off = pl.multiple_of(sl * n_idx, n_idx)
        return pltpu.make_async_copy(dest_ref.at[step], idx_ref.at[pl.ds(off, n_idx)], idx_sem.at[sl])

    def gather(step):
        base = (step % 3) * n_idx
        buf = ybuf.at[step % 2]
        sem = ysem.at[step % 2]

        def issue(t, carry):
            _row_copy(ys_ref, idx_ref[base + t], buf, t, sem).start()
            _row_copy(ys_ref, idx_ref[base + tc + t], buf, tc + t, sem).start()
            return carry

        lax.fori_loop(0, tc, issue, 0, unroll=8)

    @pl.when(i == 0)
    def _():
        idx_copy(0).start()
        idx_copy(0).wait()
        gather(0)

        @pl.when(nsteps > 1)
        def _():
            idx_copy(1).start()

    @pl.when(i + 1 < nsteps)
    def _():
        idx_copy(i + 1).wait()

        @pl.when(i + 2 < nsteps)
        def _():
            idx_copy(i + 2).start()

        gather(i + 1)

    slot = i % 2
    pltpu.make_async_copy(ys_ref.at[pl.ds(0, n_idx)], ybuf.at[slot], ysem.at[slot]).wait()
    wcol = _dot_nt(eye_ref[...], rw_ref[...], precision=HIGHEST)
    lo1, hi1 = _unpack_rows(ybuf[slot, 0:tc])
    lo2, hi2 = _unpack_rows(ybuf[slot, tc:n_idx])
    w1 = wcol[:, 0:1]
    w2 = wcol[:, 1:2]
    z_lo = x2_ref[:, :half] + (lo1 * w1 + lo2 * w2)
    z_hi = x2_ref[:, half:] + (hi1 * w1 + hi2 * w2)
    ms = (jnp.sum(z_lo * z_lo, axis=-1, keepdims=True) + jnp.sum(z_hi * z_hi, axis=-1, keepdims=True)) / d
    scale = lax.rsqrt(ms + EPS)
    o_ref[:, :half] = z_lo * scale * nw_ref[:, :half]
    o_ref[:, half:] = z_hi * scale * nw_ref[:, half:]


def _combine(x2, ys, dest_tiles, rw, eye, norm_w):
    n, d = x2.shape
    nt, n_idx = dest_tiles.shape
    tc = n_idx // 2
    w = ys.shape[1]
    return pl.pallas_call(
        _combine_kernel,
        grid=(nt,),
        in_specs=[
            pl.BlockSpec(memory_space=pl.ANY),
            pl.BlockSpec(memory_space=pl.ANY),
            pl.BlockSpec((tc, d), lambda i: (i, 0)),
            pl.BlockSpec((8, tc), lambda i: (0, i)),
            pl.BlockSpec((tc, tc), lambda i: (0, 0)),
            pl.BlockSpec((1, d), lambda i: (0, 0)),
        ],
        out_specs=pl.BlockSpec((tc, d), lambda i: (i, 0)),
        out_shape=jax.ShapeDtypeStruct((n, d), F32),
        scratch_shapes=[
            pltpu.SMEM((3 * n_idx,), jnp.int32),
            pltpu.SemaphoreType.DMA((3,)),
            pltpu.VMEM((2, n_idx, w), jnp.uint32),
            pltpu.SemaphoreType.DMA((2,)),
        ],
        compiler_params=pltpu.CompilerParams(
            dimension_semantics=("arbitrary",), vmem_limit_bytes=VMEM_LIMIT),
        name="combine",
    )(dest_tiles, ys, x2, rw, eye, norm_w)


def _rope_column_order():
    half = RET_DK // 2
    first = [h * RET_DK + j for h in range(RET_HEADS) for j in range(half)]
    second = [h * RET_DK + half + j for h in range(RET_HEADS) for j in range(half)]
    return np.array(first + second, dtype=np.int32)


def _layer(xf, mem, batch, seq, norm_mix_w, w_in, ret_norm_w, ml_conv_w, ml_conv_b, ml_gate_b, ml_norm_w,
           w_out, norm_xa_w, norm_mem_w, xa_wq, xa_wkv, xa_wo, norm_moe_w, moe_w_group, moe_b_group,
           moe_w_router, moe_b_router, moe_w_gate, moe_w_up, moe_w_down, final_norm_w):
    n, d = xf.shape
    perm = _rope_column_order()
    cols = np.concatenate([perm, RET_QK + perm, np.arange(2 * RET_QK, MAIN_WIDTH)])
    w_main = w_in[:, cols].astype(BF16)
    w_if = w_in[:, MAIN_WIDTH:].astype(BF16)
    proj, g, gt = _inproj(xf, norm_mix_w.reshape(1, d), w_main, w_if, w_if.T)

    tabs = {k_: jnp.asarray(v_) for k_, v_ in _mixer_tables(seq).items()}
    mixed = _mixer(proj, g, gt, tabs, ret_norm_w.reshape(1, RET_V), ml_norm_w.reshape(1, ML_V), ml_conv_w,
                   ml_conv_b.reshape(1, 2 * ML_QK), ml_gate_b, batch, seq)

    kmem, vmem = _memkv(mem, norm_mem_w.reshape(1, d), xa_wkv.astype(BF16))

    w_route_t = jnp.concatenate(
        [moe_w_router.T, moe_w_group.T, jnp.zeros((ROUTE_ROWS - N_EXPERTS - N_GROUPS, d), F32)], axis=0)
    b_route = jnp.concatenate(
        [moe_b_router, moe_b_group, jnp.zeros((ROUTE_ROWS - N_EXPERTS - N_GROUPS,), F32)]).reshape(ROUTE_ROWS, 1)
    tm = TOKEN_TILE
    sut = jnp.asarray(np.triu(np.ones((tm, tm), np.float32), 1), dtype=BF16)
    x2, h3, ri, rw, cnt = _attn_route(xf, mixed, kmem, vmem, w_out.astype(BF16), norm_xa_w.reshape(1, d),
                                      xa_wq.astype(BF16), xa_wo.astype(BF16), norm_moe_w.reshape(1, d),
                                      w_route_t, b_route, sut, batch, seq)

    bm = MOE_ROWS
    counts = cnt[:, 0].astype(jnp.int32)
    padded = (counts + bm - 1) // bm * bm
    pends = jnp.cumsum(padded)
    pstarts = pends - padded
    expert = ri[0:TOP_K]
    onehot = expert[None] == jnp.arange(N_EXPERTS, dtype=jnp.int32)[:, None, None]
    dest = jnp.sum(jnp.where(onehot, pstarts[:, None, None], 0), axis=0) + ri[TOP_K:2 * TOP_K]
    cap = n * TOP_K + N_EXPERTS * bm
    nblk = cap // bm
    blk_start = jnp.arange(nblk, dtype=jnp.int32) * bm
    blk_e = jnp.minimum(jnp.sum(blk_start[:, None] >= pends[None, :], axis=1), N_EXPERTS - 1).astype(jnp.int32)
    nused = (pends[-1] // bm).astype(jnp.int32).reshape(1)
    zpos = jnp.where(padded > counts, pends - bm, -1).astype(jnp.int32)
    zpos = jnp.concatenate([zpos, nused])

    def tiles(rows):
        return dest.reshape(TOP_K, n // rows, rows).transpose(1, 0, 2).reshape(n // rows, TOP_K * rows)

    xs = _dispatch(h3, tiles(DISPATCH_TILE), zpos, cap)
    ys = _experts(xs, blk_e, nused, moe_w_gate, moe_w_up, moe_w_down)
    eye = jnp.asarray(np.eye(tm, dtype=np.float32))
    return _combine(x2, ys, tiles(tm), rw, eye, final_norm_w.reshape(1, d))


def kernel(x, mem, norm_mix_w, w_in, ret_norm_w, ml_conv_w, ml_conv_b, ml_gate_b, ml_norm_w, w_out, norm_xa_w, norm_mem_w, xa_wq, xa_wkv, xa_wo, norm_moe_w, moe_w_group, moe_b_group, moe_w_router, moe_b_router, moe_w_gate, moe_w_up, moe_w_down, norm_final_w):
    batch, seq, d = x.shape
    depth = w_in.shape[0]
    assert depth == 1, "the final norm is fused into the last layer's combine kernel"
    l = 0
    out = _layer(x.reshape(batch * seq, d), mem, batch, seq, norm_mix_w[l], w_in[l], ret_norm_w[l], ml_conv_w[l],
                 ml_conv_b[l], ml_gate_b[l], ml_norm_w[l], w_out[l], norm_xa_w[l], norm_mem_w[l], xa_wq[l],
                 xa_wkv[l], xa_wo[l], norm_moe_w[l], moe_w_group[l], moe_b_group[l], moe_w_router[l],
                 moe_b_router[l], moe_w_gate[l], moe_w_up[l], moe_w_down[l], norm_final_w)
    return out.reshape(batch, seq, d)
```

```python
import functools

import numpy as np
import jax
import jax.numpy as jnp
from jax import lax
from jax.experimental import pallas as pl
from jax.experimental.pallas import tpu as pltpu

F32 = jnp.float32
BF16 = jnp.bfloat16
HIGHEST = lax.Precision.HIGHEST

CHUNK = 128
RET_HEADS = 4
RET_DK = 64
RET_DV = 128
ML_HEADS = 4
ML_DK = 128
ML_DV = 128
CONV_W = 4
XA_HEADS = 4
N_GROUPS = 4
EXP_PER_GROUP = 8
N_EXPERTS = N_GROUPS * EXP_PER_GROUP
TOP_K = 2
ROPE_BASE = 10000.0
EPS = 1e-6

RET_QK = RET_HEADS * RET_DK
RET_V = RET_HEADS * RET_DV
ML_QK = ML_HEADS * ML_DK
ML_V = ML_HEADS * ML_DV
OFF_RQ = 0
OFF_RK = OFF_RQ + RET_QK
OFF_RV = OFF_RK + RET_QK
OFF_RG = OFF_RV + RET_V
OFF_MQK = OFF_RG + RET_V
OFF_MV = OFF_MQK + 2 * ML_QK
OFF_MO = OFF_MV + ML_V
MAIN_WIDTH = OFF_MO + ML_V
N_GATES = 2 * ML_HEADS

ROUTE_ROWS = 40
TOKEN_TILE = 512
MOE_ROWS = 512
DISPATCH_TILE = 1024
MIXER_BATCHES = 2
VMEM_LIMIT = 56 * 1024 * 1024


def _dot(a, b):
    return jnp.dot(a, b, preferred_element_type=F32)


def _dot_nt(a, b, precision=None):
    return lax.dot_general(a, b, (((1,), (1,)), ((), ())), preferred_element_type=F32, precision=precision)


def _dot_tn(a, b):
    return lax.dot_general(a, b, (((0,), (0,)), ((), ())), preferred_element_type=F32)


def _rms(x, w):
    return x * lax.rsqrt(jnp.mean(x * x, axis=-1, keepdims=True) + EPS) * w


def _sigmoid(x):
    return 1.0 / (1.0 + jnp.exp(-x))


def _silu(x):
    return x * _sigmoid(x)


def _log_sigmoid(x):
    return jnp.minimum(x, 0.0) - jnp.log1p(jnp.exp(-jnp.abs(x)))


def _head_norm(h):
    mu = jnp.mean(h, axis=-1, keepdims=True)
    d = h - mu
    var = jnp.mean(d * d, axis=-1, keepdims=True)
    return d * lax.rsqrt(var + EPS)


def _pack_rows(lo, hi):
    def rne(t):
        b = lax.bitcast_convert_type(t, jnp.uint32)
        return b + jnp.uint32(0x7FFF) + ((b >> 16) & jnp.uint32(1))
    return (rne(lo) >> 16) | (rne(hi) & jnp.uint32(0xFFFF0000))


def _unpack_rows(u):
    lo = lax.bitcast_convert_type(u << 16, F32)
    hi = lax.bitcast_convert_type(u & jnp.uint32(0xFFFF0000), F32)
    return lo, hi


def _inproj_kernel(x_ref, nw_ref, w_ref, wif_ref, wift_ref, proj_ref, g_ref, gt_ref):
    h = _rms(x_ref[...], nw_ref[...]).astype(BF16)
    for j in range(0, MAIN_WIDTH, 512):
        proj_ref[:, j:j + 512] = _dot(h, w_ref[:, j:j + 512]).astype(BF16)
    g_ref[...] = _dot(h, wif_ref[...])
    gt_ref[...] = _dot_nt(wift_ref[...], h)


def _inproj(xf, norm_w, w_main, w_if, w_ift):
    n, d = xf.shape
    tm = TOKEN_TILE
    return pl.pallas_call(
        _inproj_kernel,
        grid=(n // tm,),
        in_specs=[
            pl.BlockSpec((tm, d), lambda i: (i, 0)),
            pl.BlockSpec((1, d), lambda i: (0, 0)),
            pl.BlockSpec((d, MAIN_WIDTH), lambda i: (0, 0)),
            pl.BlockSpec((d, N_GATES), lambda i: (0, 0)),
            pl.BlockSpec((N_GATES, d), lambda i: (0, 0)),
        ],
        out_specs=[
            pl.BlockSpec((tm, MAIN_WIDTH), lambda i: (i, 0)),
            pl.BlockSpec((tm, N_GATES), lambda i: (i, 0)),
            pl.BlockSpec((N_GATES, tm), lambda i: (0, i)),
        ],
        out_shape=[
            jax.ShapeDtypeStruct((n, MAIN_WIDTH), BF16),
            jax.ShapeDtypeStruct((n, N_GATES), F32),
            jax.ShapeDtypeStruct((N_GATES, n), F32),
        ],
        compiler_params=pltpu.CompilerParams(
            dimension_semantics=("arbitrary",), vmem_limit_bytes=VMEM_LIMIT),
        name="inproj",
    )(xf, norm_w, w_main, w_if, w_ift)


def _mixer_kernel(*refs):
    @pl.when(pl.program_id(1) == 0)
    def _():
        for state_ref in refs[-5:]:
            state_ref[...] = jnp.zeros_like(state_ref)

    for bi in range(refs[0].shape[0]):
        _mixer_one(bi, *refs)


def _mixer_one(bi, proj_ref, g_ref, gt_ref, cos_ref, sin_ref, qdec_ref, kdec_ref, dmat_ref,
               bmask_ref, cdec_ref, hmask_ref, tril_ref, triu_ref, shift_ref, retw_ref, mlw_ref,
               convw_ref, convb_ref, gbc_ref, gbr_ref,
               out_ref, r_ref, c_ref, n_ref, m_ref, xbuf_ref):
    L = CHUNK
    proj_ref, g_ref, gt_ref, out_ref = proj_ref.at[bi], g_ref.at[bi], gt_ref.at[bi], out_ref.at[bi]
    r_ref, c_ref, n_ref, m_ref, xbuf_ref = r_ref.at[bi], c_ref.at[bi], n_ref.at[bi], m_ref.at[bi], xbuf_ref.at[bi]
    chunk = pl.program_id(1)

    cos = cos_ref[...]
    sin = sin_ref[...]
    half = RET_QK // 2

    def rope(t):
        t1, t2 = t[:, :half], t[:, half:]
        return jnp.concatenate([t1 * cos - t2 * sin, t1 * sin + t2 * cos], axis=1)

    q = rope(proj_ref[:, OFF_RQ:OFF_RQ + RET_QK].astype(F32))
    k = rope(proj_ref[:, OFF_RK:OFF_RK + RET_QK].astype(F32)) * (RET_DK ** -0.5)
    v = proj_ref[:, OFF_RV:OFF_RV + RET_V]
    k_b = k.astype(BF16)
    r_prev = r_ref[...]
    cross = _dot((q * qdec_ref[...]).astype(BF16), r_prev.astype(BF16))
    kv = _dot_tn((k * kdec_ref[...]).astype(BF16), v) * bmask_ref[...]
    r_ref[...] = cdec_ref[...] * r_prev + kv
    ret_heads = []
    for h in range(RET_HEADS):
        qh = (q * hmask_ref[h:h + 1, :]).astype(BF16)
        sc = _dot_nt(qh, k_b) * dmat_ref[h]
        inner = _dot(sc.astype(BF16), v[:, h * RET_DV:(h + 1) * RET_DV])
        ret_heads.append(_head_norm(inner + cross[:, h * RET_DV:(h + 1) * RET_DV]))
    ret = jnp.concatenate(ret_heads, axis=1) * retw_ref[...]
    ret = ret * _silu(proj_ref[:, OFF_RG:OFF_RG + RET_V].astype(F32))
    out_ref[:, 0:RET_V] = ret.astype(BF16)

    parity = chunk % 2
    x_b = proj_ref[:, OFF_MQK:OFF_MQK + 2 * ML_QK]
    xbuf_ref[pl.ds(pl.multiple_of(parity * L, L), L), :] = x_b
    shifted = _dot(shift_ref[parity], xbuf_ref[...])
    acc = x_b.astype(F32) * convw_ref[CONV_W - 1:CONV_W, :] + convb_ref[...]
    for s in range(1, CONV_W):
        acc = acc + shifted[(s - 1) * L:s * L] * convw_ref[CONV_W - 1 - s:CONV_W - s, :]
    qk = _silu(acc)
    mq = qk[:, :ML_QK]
    mk = qk[:, ML_QK:] * (ML_DK ** -0.5)
    mv = proj_ref[:, OFF_MV:OFF_MV + ML_V]

    gc = g_ref[...] + gbc_ref[...]
    gr = gt_ref[...] + gbr_ref[...]
    b_c = jnp.dot(tril_ref[...], _log_sigmoid(gc), preferred_element_type=F32, precision=HIGHEST)
    b_r = jnp.dot(_log_sigmoid(gr), triu_ref[...], preferred_element_type=F32, precision=HIGHEST)
    causal = (lax.broadcasted_iota(jnp.int32, (L, L), 0) >= lax.broadcasted_iota(jnp.int32, (L, L), 1))
    ml_heads = []
    for h in range(ML_HEADS):
        bc = b_c[:, ML_HEADS + h:ML_HEADS + h + 1]
        br = b_r[ML_HEADS + h:ML_HEADS + h + 1, :]
        igc = gc[:, h:h + 1]
        igr = gr[h:h + 1, :]
        btot = br[:, L - 1:L]
        log_d = jnp.where(causal, bc - br + igr, -jnp.inf)
        m_intra = jnp.max(log_d, axis=1, keepdims=True)
        m_loc = jnp.max(btot - br + igr, axis=1, keepdims=True)
        wa = jnp.exp(btot - bc + igc - m_loc)
        qh = mq[:, h * ML_DK:(h + 1) * ML_DK]
        kh = mk[:, h * ML_DK:(h + 1) * ML_DK]
        vh = mv[:, h * ML_DV:(h + 1) * ML_DV]
        kw = kh * wa
        kv_loc = _dot_tn(kw.astype(BF16), vh)
        n_loc = jnp.sum(kw, axis=0, keepdims=True)
        c_prev = c_ref[h]
        n_prev = n_ref[h][0:1, :]
        m_prev = m_ref[h][0:1, 0:1]
        m_inter = bc + m_prev
        m_t = jnp.maximum(m_intra, m_inter)
        qh_b = qh.astype(BF16)
        s_mat = _dot_nt(qh_b, kh.astype(BF16)) * jnp.exp(log_d - m_t)
        inter = jnp.exp(m_inter - m_t)
        num = _dot(s_mat.astype(BF16), vh) + inter * _dot(qh_b, c_prev.astype(BF16))
        den = jnp.sum(s_mat, axis=1, keepdims=True) + inter * jnp.sum(qh * n_prev, axis=1, keepdims=True)
        hh = num / jnp.maximum(jnp.abs(den), jnp.exp(-m_t))
        ml_heads.append(_head_norm(hh))
        m_new = jnp.maximum(btot + m_prev, m_loc)
        s_old = jnp.exp(btot + m_prev - m_new)
        s_loc = jnp.exp(m_loc - m_new)
        c_ref[h] = s_old * c_prev + s_loc * kv_loc
        n_ref[h] = jnp.broadcast_to(s_old * n_prev + s_loc * n_loc, (8, ML_DK))
        m_ref[h] = jnp.broadcast_to(m_new, (8, 128))
    ml = jnp.concatenate(ml_heads, axis=1) * mlw_ref[...]
    ml = ml * _sigmoid(proj_ref[:, OFF_MO:OFF_MO + ML_V].astype(F32))
    out_ref[:, RET_V:RET_V + ML_V] = ml.astype(BF16)


def _mixer_tables(seq):
    L = CHUNK
    half = RET_DK // 2
    inv = ROPE_BASE ** (-np.arange(half, dtype=np.float64) / half)
    ang = np.arange(seq, dtype=np.float64)[:, None] * inv[None, :].astype(np.float32).astype(np.float64)
    cos = np.tile(np.cos(ang), (1, RET_HEADS)).astype(np.float32)
    sin = np.tile(np.sin(ang), (1, RET_HEADS)).astype(np.float32)
    log_g = np.log1p(-np.exp2(-5.0 - np.arange(RET_HEADS, dtype=np.float64)))
    n = np.arange(L, dtype=np.float64)
    lane_head = (np.arange(RET_QK) % (RET_QK // 2)) // half
    qdec = np.exp((n + 1)[:, None] * log_g[lane_head][None, :]).astype(np.float32)
    kdec = np.exp((L - 1 - n)[:, None] * log_g[lane_head][None, :]).astype(np.float32)
    diff = n[:, None] - n[None, :]
    dmat = np.where(diff >= 0, np.exp(log_g[:, None, None] * np.maximum(diff, 0.0)[None]), 0.0).astype(np.float32)
    col_head = np.arange(RET_V) // RET_DV
    bmask = (lane_head[:, None] == col_head[None, :]).astype(np.float32)
    cdec = np.exp(L * log_g[col_head])[None, :].astype(np.float32)
    hmask = (lane_head[None, :] == np.arange(RET_HEADS)[:, None]).astype(np.float32)
    hmask = np.concatenate([hmask, np.zeros((8 - RET_HEADS, RET_QK), np.float32)], axis=0)
    tril = np.tril(np.ones((L, L), np.float32))
    shift = np.zeros((2, (CONV_W - 1) * L, 2 * L), np.float32)
    for p in range(2):
        for s in range(1, CONV_W):
            for t in range(L):
                col = p * L + t - s if t >= s else (1 - p) * L + L + t - s
                shift[p, (s - 1) * L + t, col] = 1.0
    return dict(cos=cos, sin=sin, qdec=qdec, kdec=kdec, dmat=dmat, bmask=bmask, cdec=cdec, hmask=hmask,
                tril=tril, triu=np.ascontiguousarray(tril.T), shift=shift)


def _mixer(proj, g, gt, tabs, ret_norm_w, ml_norm_w, conv_w, conv_b, gate_b, batch, seq):
    L = CHUNK
    nc = seq // L
    n = batch * seq
    nb = MIXER_BATCHES if batch % MIXER_BATCHES == 0 else 1
    proj = proj.reshape(batch, seq, MAIN_WIDTH)
    g = g.reshape(batch, seq, N_GATES)
    gt = gt.reshape(N_GATES, batch, seq).transpose(1, 0, 2)
    const2 = lambda b, c: (0, 0)
    const3 = lambda b, c: (0, 0, 0)
    tok = lambda b, c: (b, c, 0)
    in_specs = [
        pl.BlockSpec((nb, L, MAIN_WIDTH), tok),
        pl.BlockSpec((nb, L, N_GATES), tok),
        pl.BlockSpec((nb, N_GATES, L), lambda b, c: (b, 0, c)),
        pl.BlockSpec((L, RET_QK // 2), lambda b, c: (c, 0)),
        pl.BlockSpec((L, RET_QK // 2), lambda b, c: (c, 0)),
        pl.BlockSpec((L, RET_QK), const2),
        pl.BlockSpec((L, RET_QK), const2),
        pl.BlockSpec((RET_HEADS, L, L), const3),
        pl.BlockSpec((RET_QK, RET_V), const2),
        pl.BlockSpec((1, RET_V), const2),
        pl.BlockSpec((8, RET_QK), const2),
        pl.BlockSpec((L, L), const2),
        pl.BlockSpec((L, L), const2),
        pl.BlockSpec((2, (CONV_W - 1) * L, 2 * L), const3),
        pl.BlockSpec((1, RET_V), const2),
        pl.BlockSpec((1, ML_V), const2),
        pl.BlockSpec((CONV_W, 2 * ML_QK), const2),
        pl.BlockSpec((1, 2 * ML_QK), const2),
        pl.BlockSpec((1, N_GATES), const2),
        pl.BlockSpec((N_GATES, 1), const2),
    ]
    return pl.pallas_call(
        _mixer_kernel,
        grid=(batch // nb, nc),
        in_specs=in_specs,
        out_specs=pl.BlockSpec((nb, L, RET_V + ML_V), tok),
        out_shape=jax.ShapeDtypeStruct((batch, seq, RET_V + ML_V), BF16),
        scratch_shapes=[
            pltpu.VMEM((nb, RET_QK, RET_V), F32),
            pltpu.VMEM((nb, ML_HEADS, ML_DK, ML_DV), F32),
            pltpu.VMEM((nb, ML_HEADS, 8, ML_DK), F32),
            pltpu.VMEM((nb, ML_HEADS, 8, 128), F32),
            pltpu.VMEM((nb, 2 * L, 2 * ML_QK), BF16),
        ],
        compiler_params=pltpu.CompilerParams(
            dimension_semantics=("arbitrary", "arbitrary"), vmem_limit_bytes=VMEM_LIMIT),
        name="mixer",
    )(proj, g, gt, tabs["cos"], tabs["sin"], tabs["qdec"], tabs["kdec"], tabs["dmat"], tabs["bmask"],
      tabs["cdec"], tabs["hmask"], tabs["tril"], tabs["triu"], tabs["shift"].astype(BF16), ret_norm_w, ml_norm_w,
      conv_w, conv_b, gate_b.reshape(1, N_GATES), gate_b.reshape(N_GATES, 1)).reshape(n, RET_V + ML_V)


def _memkv_kernel(mem_ref, nw_ref, wkv_ref, k_ref, v_ref):
    d = mem_ref.shape[-1]
    mn = _rms(mem_ref[0], nw_ref[...]).astype(BF16)
    k_ref[0] = _dot(mn, wkv_ref[:, :d]).astype(BF16)
    v_ref[0] = _dot(mn, wkv_ref[:, d:]).astype(BF16)


def _memkv(mem, norm_w, wkv):
    b, m, d = mem.shape
    return pl.pallas_call(
        _memkv_kernel,
        grid=(b,),
        in_specs=[
            pl.BlockSpec((1, m, d), lambda i: (i, 0, 0)),
            pl.BlockSpec((1, d), lambda i: (0, 0)),
            pl.BlockSpec((d, 2 * d), lambda i: (0, 0)),
        ],
        out_specs=[pl.BlockSpec((1, m, d), lambda i: (i, 0, 0))] * 2,
        out_shape=[jax.ShapeDtypeStruct((b, m, d), BF16)] * 2,
        compiler_params=pltpu.CompilerParams(
            dimension_semantics=("arbitrary",), vmem_limit_bytes=VMEM_LIMIT),
        name="memkv",
    )(mem, norm_w, wkv)


def _attn_route_kernel(x_ref, mix_ref, k_ref, v_ref, wout_ref, nxa_ref, wq_ref, wo_ref, nmoe_ref,
                       wr_ref, br_ref, sut_ref,
                       x2_ref, h3_ref, ri_ref, rw_ref, cnt_ref, carry_ref):
    tm, d = x_ref.shape
    dh = d // XA_HEADS

    @pl.when((pl.program_id(0) == 0) & (pl.program_id(1) == 0))
    def _():
        carry_ref[...] = jnp.zeros_like(carry_ref)

    x1 = x_ref[...] + _dot(mix_ref[...], wout_ref[...])
    h2 = _rms(x1, nxa_ref[...]).astype(BF16)
    q = _dot(h2, wq_ref[...]).astype(BF16)
    heads = []
    for h in range(XA_HEADS):
        logits = _dot_nt(q[:, h * dh:(h + 1) * dh], k_ref[0, :, h * dh:(h + 1) * dh]) * (dh ** -0.5)
        mx = jnp.max(logits, axis=-1, keepdims=True)
        e = jnp.exp(logits - mx)
        p = (e / jnp.sum(e, axis=-1, keepdims=True)).astype(BF16)
        heads.append(_dot(p, v_ref[0, :, h * dh:(h + 1) * dh]).astype(BF16))
    x2 = x1 + _dot(jnp.concatenate(heads, axis=1), wo_ref[...])
    x2_ref[...] = x2
    h3 = _rms(x2, nmoe_ref[...])
    h3_ref[...] = _pack_rows(h3[:, :d // 2], h3[:, d // 2:])

    lt = _dot_nt(wr_ref[...], h3, precision=HIGHEST) + br_ref[...]
    gl = lt[N_EXPERTS:N_EXPERTS + N_GROUPS]
    gmax = jnp.max(gl, axis=0, keepdims=True)
    g_w = 1.0 / jnp.sum(jnp.exp(gl - gmax), axis=0, keepdims=True)
    giota = lax.broadcasted_iota(jnp.int32, gl.shape, 0)
    g_sel = jnp.min(jnp.where(gl == gmax, giota, N_GROUPS), axis=0, keepdims=True)
    el = lt[0:N_EXPERTS]
    eiota = lax.broadcasted_iota(jnp.int32, el.shape, 0)
    in_grp = (eiota // EXP_PER_GROUP) == g_sel
    elm = jnp.where(in_grp, el, -jnp.inf)
    m1 = jnp.max(elm, axis=0, keepdims=True)
    esum = jnp.sum(jnp.where(in_grp, jnp.exp(el - m1), 0.0), axis=0, keepdims=True)
    i1 = jnp.min(jnp.where(elm == m1, eiota, N_EXPERTS), axis=0, keepdims=True)
    elm2 = jnp.where(eiota == i1, -jnp.inf, elm)
    m2 = jnp.max(elm2, axis=0, keepdims=True)
    i2 = jnp.min(jnp.where(elm2 == m2, eiota, N_EXPERTS), axis=0, keepdims=True)
    p1 = 1.0 / esum
    p2 = jnp.exp(m2 - m1) / esum
    psum = p1 + p2
    w1 = g_w * (p1 / psum)
    w2 = g_w * (p2 / psum)

    oh1 = (eiota == i1).astype(F32)
    oh2 = (eiota == i2).astype(F32)
    cnt = oh1 + oh2
    base = carry_ref[:, 0:1] + _dot(cnt.astype(BF16), sut_ref[...])
    r1 = jnp.sum(oh1 * base, axis=0, keepdims=True)
    r2 = jnp.sum(oh2 * base, axis=0, keepdims=True)
    new_carry = carry_ref[...] + jnp.sum(cnt, axis=1, keepdims=True)
    carry_ref[...] = new_carry
    cnt_ref[...] = new_carry

    zi = jnp.zeros((4, tm), jnp.int32)
    ri_ref[...] = jnp.concatenate([i1, i2, r1.astype(jnp.int32), r2.astype(jnp.int32), zi], axis=0)
    rw_ref[...] = jnp.concatenate([w1, w2, jnp.zeros((6, tm), F32)], axis=0)


def _attn_route(xf, mixed, kmem, vmem, w_out, norm_xa_w, wq, wo, norm_moe_w, w_route_t, b_route, sut,
                batch, seq):
    n, d = xf.shape
    tm = TOKEN_TILE
    nt = seq // tm
    m = kmem.shape[1]
    tok = lambda b, t: (b * nt + t, 0)
    lane_tok = lambda b, t: (0, b * nt + t)
    const2 = lambda b, t: (0, 0)
    return pl.pallas_call(
        _attn_route_kernel,
        grid=(batch, nt),
        in_specs=[
            pl.BlockSpec((tm, d), tok),
            pl.BlockSpec((tm, d), tok),
            pl.BlockSpec((1, m, d), lambda b, t: (b, 0, 0)),
            pl.BlockSpec((1, m, d), lambda b, t: (b, 0, 0)),
            pl.BlockSpec((d, d), const2),
            pl.BlockSpec((1, d), const2),
            pl.BlockSpec((d, d), const2),
            pl.BlockSpec((d, d), const2),
            pl.BlockSpec((1, d), const2),
            pl.BlockSpec((ROUTE_ROWS, d), const2),
            pl.BlockSpec((ROUTE_ROWS, 1), const2),
            pl.BlockSpec((tm, tm), const2),
        ],
        out_specs=[
            pl.BlockSpec((tm, d), tok),
            pl.BlockSpec((tm, d // 2), tok),
            pl.BlockSpec((8, tm), lane_tok),
            pl.BlockSpec((8, tm), lane_tok),
            pl.BlockSpec((N_EXPERTS, 128), const2),
        ],
        out_shape=[
            jax.ShapeDtypeStruct((n, d), F32),
            jax.ShapeDtypeStruct((n, d // 2), jnp.uint32),
            jax.ShapeDtypeStruct((8, n), jnp.int32),
            jax.ShapeDtypeStruct((8, n), F32),
            jax.ShapeDtypeStruct((N_EXPERTS, 128), F32),
        ],
        scratch_shapes=[pltpu.VMEM((N_EXPERTS, 128), F32)],
        compiler_params=pltpu.CompilerParams(
            dimension_semantics=("arbitrary", "arbitrary"), vmem_limit_bytes=VMEM_LIMIT),
        name="attn_route",
    )(xf, mixed, kmem, vmem, w_out, norm_xa_w, wq, wo, norm_moe_w, w_route_t, b_route, sut)


def _row_copy(src, s, dst, d, sem):
    return pltpu.make_async_copy(src.at[pl.ds(s, 1)], dst.at[pl.ds(d, 1)], sem)


def _dispatch_kernel(zpos_ref, dest_ref, h_ref, xs_ref, idx_ref, idx_sem, row_sem, zero_ref, zero_sem):
    i = pl.program_id(0)
    nsteps = pl.num_programs(0)
    td = h_ref.shape[0]
    bm = zero_ref.shape[0]
    slot = i % 2

    def idx_copy(step, sl):
        off = pl.multiple_of(sl * (2 * td), 2 * td)
        return pltpu.make_async_copy(dest_ref.at[step], idx_ref.at[pl.ds(off, 2 * td)], idx_sem.at[sl])

    @pl.when(i == 0)
    def _():
        zero_ref[...] = jnp.zeros_like(zero_ref)

        def zero_copy(e):
            return pltpu.make_async_copy(zero_ref, xs_ref.at[pl.ds(pl.multiple_of(zpos_ref[e], bm), bm)], zero_sem)

        def tail_copy(b):
            return pltpu.make_async_copy(zero_ref, xs_ref.at[pl.ds(pl.multiple_of(b * bm, bm), bm)], zero_sem)

        nused = zpos_ref[N_EXPERTS]
        nblk = xs_ref.shape[0] // bm
        for e in range(N_EXPERTS):
            pl.when(zpos_ref[e] >= 0)(lambda e=e: zero_copy(e).start())
        lax.fori_loop(nused, nblk, lambda b, c: (tail_copy(b).start(), c)[1], 0)
        for e in range(N_EXPERTS):
            pl.when(zpos_ref[e] >= 0)(lambda e=e: zero_copy(e).wait())
        lax.fori_loop(nused, nblk, lambda b, c: (tail_copy(b).wait(), c)[1], 0)
        idx_copy(0, 0).start()

    idx_copy(i, slot).wait()

    @pl.when(i + 1 < nsteps)
    def _():
        idx_copy(i + 1, 1 - slot).start()

    base = slot * (2 * td)

    def issue(t, carry):
        _row_copy(h_ref, t, xs_ref, idx_ref[base + t], row_sem).start()
        _row_copy(h_ref, t, xs_ref, idx_ref[base + td + t], row_sem).start()
        return carry

    lax.fori_loop(0, td, issue, 0, unroll=8)
    for _ in range(TOP_K):
        pltpu.make_async_copy(h_ref, xs_ref.at[pl.ds(0, td)], row_sem).wait()


def _dispatch(h3p, dest_tiles, zpos, cap):
    n, w = h3p.shape
    nt, td2 = dest_tiles.shape
    td = td2 // 2
    grid_spec = pltpu.PrefetchScalarGridSpec(
        num_scalar_prefetch=1,
        grid=(nt,),
        in_specs=[
            pl.BlockSpec(memory_space=pl.ANY),
            pl.BlockSpec((td, w), lambda i, zp: (i, 0)),
        ],
        out_specs=pl.BlockSpec(memory_space=pl.ANY),
        scratch_shapes=[
            pltpu.SMEM((2 * td2,), jnp.int32),
            pltpu.SemaphoreType.DMA((2,)),
            pltpu.SemaphoreType.DMA,
            pltpu.VMEM((MOE_ROWS, w), jnp.uint32),
            pltpu.SemaphoreType.DMA,
        ],
    )
    return pl.pallas_call(
        _dispatch_kernel,
        grid_spec=grid_spec,
        out_shape=jax.ShapeDtypeStruct((cap, w), jnp.uint32),
        compiler_params=pltpu.CompilerParams(
            dimension_semantics=("arbitrary",), vmem_limit_bytes=VMEM_LIMIT),
        name="dispatch",
    )(zpos, dest_tiles, h3p)


def _expert_kernel(blk_e_ref, nused_ref, xs_ref, wg_ref, wu_ref, wd_ref, ys_ref, wg_b, wu_b, wd_b):
    i = pl.program_id(0)
    prev = blk_e_ref[jnp.maximum(i - 1, 0)]
    fresh = (i == 0) | (blk_e_ref[i] != prev)
    half = wd_b.shape[1] // 2

    @pl.when(fresh)
    def _():
        wg_b[...] = wg_ref[0].astype(BF16)
        wu_b[...] = wu_ref[0].astype(BF16)
        wd_b[...] = wd_ref[0].astype(BF16)

    @pl.when(i < nused_ref[0])
    def _():
        lo, hi = _unpack_rows(xs_ref[...])
        xb = jnp.concatenate([lo.astype(BF16), hi.astype(BF16)], axis=1)
        hid = (_silu(_dot(xb, wg_b[...])) * _dot(xb, wu_b[...])).astype(BF16)
        y = _dot(hid, wd_b[...])
        ys_ref[...] = _pack_rows(y[:, :half], y[:, half:])

    @pl.when(i >= nused_ref[0])
    def _():
        ys_ref[...] = jnp.zeros_like(ys_ref)


def _experts(xs, blk_e, nused, w_gate, w_up, w_down):
    cap, w = xs.shape
    _, d, de = w_gate.shape
    bm = MOE_ROWS
    used = lambda i, be, nu: (jnp.minimum(i, nu[0] - 1), 0)
    grid_spec = pltpu.PrefetchScalarGridSpec(
        num_scalar_prefetch=2,
        grid=(cap // bm,),
        in_specs=[
            pl.BlockSpec((bm, w), used),
            pl.BlockSpec((1, d, de), lambda i, be, nu: (be[i], 0, 0)),
            pl.BlockSpec((1, d, de), lambda i, be, nu: (be[i], 0, 0)),
            pl.BlockSpec((1, de, d), lambda i, be, nu: (be[i], 0, 0)),
        ],
        out_specs=pl.BlockSpec((bm, w), lambda i, be, nu: (i, 0)),
        scratch_shapes=[
            pltpu.VMEM((d, de), BF16),
            pltpu.VMEM((d, de), BF16),
            pltpu.VMEM((de, d), BF16),
        ],
    )
    return pl.pallas_call(
        _expert_kernel,
        grid_spec=grid_spec,
        out_shape=jax.ShapeDtypeStruct((cap, w), jnp.uint32),
        compiler_params=pltpu.CompilerParams(
            dimension_semantics=("arbitrary",), vmem_limit_bytes=VMEM_LIMIT),
        name="experts",
    )(blk_e, nused, xs, w_gate, w_up, w_down)


def _combine_kernel(dest_ref, ys_ref, x2_ref, rw_ref, eye_ref, nw_ref, o_ref, idx_ref, idx_sem, ybuf, ysem):
    i = pl.program_id(0)
    nsteps = pl.num_programs(0)
    tc, d = x2_ref.shape
    half = d // 2
    n_idx = 2 * tc

    def idx_copy(step):
        sl = step % 3
        off = pl.multiple_of(sl * n_idx, n_idx)
        return pltpu.make_async_copy(dest_ref.at[step], idx_ref.at[pl.ds(off, n_idx)], idx_sem.at[sl])

    def gather(step):
        base = (step % 3) * n_idx
        buf = ybuf.at[step % 2]
        sem = ysem.at[step % 2]

        def issue(t, carry):
            _row_copy(ys_ref, idx_ref[base + t], buf, t, sem).start()
            _row_copy(ys_ref, idx_ref[base + tc + t], buf, tc + t, sem).start()
            return carry

        lax.fori_loop(0, tc, issue, 0, unroll=8)

    @pl.when(i == 0)
    def _():
        idx_copy(0).start()
        idx_copy(0).wait()
        gather(0)

        @pl.when(nsteps > 1)
        def _():
            idx_copy(1).start()

    @pl.when(i + 1 < nsteps)
    def _():
        idx_copy(i + 1).wait()

        @pl.when(i + 2 < nsteps)
        def _():
            idx_copy(i + 2).start()

        gather(i + 1)

    slot = i % 2
    pltpu.make_async_copy(ys_ref.at[pl.ds(0, n_idx)], ybuf.at[slot], ysem.at[slot]).wait()
    wcol = _dot_nt(eye_ref[...], rw_ref[...], precision=HIGHEST)
    lo1, hi1 = _unpack_rows(ybuf[slot, 0:tc])
    lo2, hi2 = _unpack_rows(ybuf[slot, tc:n_idx])
    w1 = wcol[:, 0:1]
    w2 = wcol[:, 1:2]
    z_lo = x2_ref[:, :half] + (lo1 * w1 + lo2 * w2)
    z_hi = x2_ref[:, half:] + (hi1 * w1 + hi2 * w2)
    ms = (jnp.sum(z_lo * z_lo, axis=-1, keepdims=True) + jnp.sum(z_hi * z_hi, axis=-1, keepdims=True)) / d
    scale = lax.rsqrt(ms + EPS)
    o_ref[:, :half] = z_lo * scale * nw_ref[:, :half]
    o_ref[:, half:] = z_hi * scale * nw_ref[:, half:]


def _combine(x2, ys, dest_tiles, rw, eye, norm_w):
    n, d = x2.shape
    nt, n_idx = dest_tiles.shape
    tc = n_idx // 2
    w = ys.shape[1]
    return pl.pallas_call(
        _combine_kernel,
        grid=(nt,),
        in_specs=[
            pl.BlockSpec(memory_space=pl.ANY),
            pl.BlockSpec(memory_space=pl.ANY),
            pl.BlockSpec((tc, d), lambda i: (i, 0)),
            pl.BlockSpec((8, tc), lambda i: (0, i)),
            pl.BlockSpec((tc, tc), lambda i: (0, 0)),
            pl.BlockSpec((1, d), lambda i: (0, 0)),
        ],
        out_specs=pl.BlockSpec((tc, d), lambda i: (i, 0)),
        out_shape=jax.ShapeDtypeStruct((n, d), F32),
        scratch_shapes=[
            pltpu.SMEM((3 * n_idx,), jnp.int32),
            pltpu.SemaphoreType.DMA((3,)),
            pltpu.VMEM((2, n_idx, w), jnp.uint32),
            pltpu.SemaphoreType.DMA((2,)),
        ],
        compiler_params=pltpu.CompilerParams(
            dimension_semantics=("arbitrary",), vmem_limit_bytes=VMEM_LIMIT),
        name="combine",
    )(dest_tiles, ys, x2, rw, eye, norm_w)


def _rope_column_order():
    half = RET_DK // 2
    first = [h * RET_DK + j for h in range(RET_HEADS) for j in range(half)]
    second = [h * RET_DK + half + j for h in range(RET_HEADS) for j in range(half)]
    return np.array(first + second, dtype=np.int32)


def _layer(xf, mem, batch, seq, norm_mix_w, w_in, ret_norm_w, ml_conv_w, ml_conv_b, ml_gate_b, ml_norm_w,
           w_out, norm_xa_w, norm_mem_w, xa_wq, xa_wkv, xa_wo, norm_moe_w, moe_w_group, moe_b_group,
           moe_w_router, moe_b_router, moe_w_gate, moe_w_up, moe_w_down, final_norm_w):
    n, d = xf.shape
    perm = _rope_column_order()
    cols = np.concatenate([perm, RET_QK + perm, np.arange(2 * RET_QK, MAIN_WIDTH)])
    w_main = w_in[:, cols].astype(BF16)
    w_if = w_in[:, MAIN_WIDTH:].astype(BF16)
    proj, g, gt = _inproj(xf, norm_mix_w.reshape(1, d), w_main, w_if, w_if.T)

    tabs = {k_: jnp.asarray(v_) for k_, v_ in _mixer_tables(seq).items()}
    mixed = _mixer(proj, g, gt, tabs, ret_norm_w.reshape(1, RET_V), ml_norm_w.reshape(1, ML_V), ml_conv_w,
                   ml_conv_b.reshape(1, 2 * ML_QK), ml_gate_b, batch, seq)

    kmem, vmem = _memkv(mem, norm_mem_w.reshape(1, d), xa_wkv.astype(BF16))

    w_route_t = jnp.concatenate(
        [moe_w_router.T, moe_w_group.T, jnp.zeros((ROUTE_ROWS - N_EXPERTS - N_GROUPS, d), F32)], axis=0)
    b_route = jnp.concatenate(
        [moe_b_router, moe_b_group, jnp.zeros((ROUTE_ROWS - N_EXPERTS - N_GROUPS,), F32)]).reshape(ROUTE_ROWS, 1)
    tm = TOKEN_TILE
    sut = jnp.asarray(np.triu(np.ones((tm, tm), np.float32), 1), dtype=BF16)
    x2, h3, ri, rw, cnt = _attn_route(xf, mixed, kmem, vmem, w_out.astype(BF16), norm_xa_w.reshape(1, d),
                                      xa_wq.astype(BF16), xa_wo.astype(BF16), norm_moe_w.reshape(1, d),
                                      w_route_t, b_route, sut, batch, seq)

    bm = MOE_ROWS
    counts = cnt[:, 0].astype(jnp.int32)
    padded = (counts + bm - 1) // bm * bm
    pends = jnp.cumsum(padded)
    pstarts = pends - padded
    expert = ri[0:TOP_K]
    onehot = expert[None] == jnp.arange(N_EXPERTS, dtype=jnp.int32)[:, None, None]
    dest = jnp.sum(jnp.where(onehot, pstarts[:, None, None], 0), axis=0) + ri[TOP_K:2 * TOP_K]
    cap = n * TOP_K + N_EXPERTS * bm
    nblk = cap // bm
    blk_start = jnp.arange(nblk, dtype=jnp.int32) * bm
    blk_e = jnp.minimum(jnp.sum(blk_start[:, None] >= pends[None, :], axis=1), N_EXPERTS - 1).astype(jnp.int32)
    nused = (pends[-1] // bm).astype(jnp.int32).reshape(1)
    zpos = jnp.where(padded > counts, pends - bm, -1).astype(jnp.int32)
    zpos = jnp.concatenate([zpos, nused])

    def tiles(rows):
        return dest.reshape(TOP_K, n // rows, rows).transpose(1, 0, 2).reshape(n // rows, TOP_K * rows)

    xs = _dispatch(h3, tiles(DISPATCH_TILE), zpos, cap)
    ys = _experts(xs, blk_e, nused, moe_w_gate, moe_w_up, moe_w_down)
    eye = jnp.asarray(np.eye(tm, dtype=np.float32))
    return _combine(x2, ys, tiles(tm), rw, eye, final_norm_w.reshape(1, d))


def kernel(x, mem, norm_mix_w, w_in, ret_norm_w, ml_conv_w, ml_conv_b, ml_gate_b, ml_norm_w, w_out, norm_xa_w, norm_mem_w, xa_wq, xa_wkv, xa_wo, norm_moe_w, moe_w_group, moe_b_group, moe_w_router, moe_b_router, moe_w_gate, moe_w_up, moe_w_down, norm_final_w):
    batch, seq, d = x.shape
    depth = w_in.shape[0]
    assert depth == 1, "the final norm is fused into the last layer's combine kernel"
    l = 0
    out = _layer(x.reshape(batch * seq, d), mem, batch, seq, norm_mix_w[l], w_in[l], ret_norm_w[l], ml_conv_w[l],
                 ml_conv_b[l], ml_gate_b[l], ml_norm_w[l], w_out[l], norm_xa_w[l], norm_mem_w[l], xa_wq[l],
                 xa_wkv[l], xa_wo[l], norm_moe_w[l], moe_w_group[l], moe_b_group[l], moe_w_router[l],
                 moe_b_router[l], moe_w_gate[l], moe_w_up[l], moe_w_down[l], norm_final_w)
    return out.reshape(batch, seq, d)
```

```python
import functools

import numpy as np
import jax
import jax.numpy as jnp
from jax import lax
from jax.experimental import pallas as pl
from jax.experimental.pallas import tpu as pltpu

F32 = jnp.float32
BF16 = jnp.bfloat16
HIGHEST = lax.Precision.HIGHEST

CHUNK = 128
RET_HEADS = 4
RET_DK = 64
RET_DV = 128
ML_HEADS = 4
ML_DK = 128
ML_DV = 128
CONV_W = 4
XA_HEADS = 4
N_GROUPS = 4
EXP_PER_GROUP = 8
N_EXPERTS = N_GROUPS * EXP_PER_GROUP
TOP_K = 2
ROPE_BASE = 10000.0
EPS = 1e-6

RET_QK = RET_HEADS * RET_DK
RET_V = RET_HEADS * RET_DV
ML_QK = ML_HEADS * ML_DK
ML_V = ML_HEADS * ML_DV
OFF_RQ = 0
OFF_RK = OFF_RQ + RET_QK
OFF_RV = OFF_RK + RET_QK
OFF_RG = OFF_RV + RET_V
OFF_MQK = OFF_RG + RET_V
OFF_MV = OFF_MQK + 2 * ML_QK
OFF_MO = OFF_MV + ML_V
MAIN_WIDTH = OFF_MO + ML_V
N_GATES = 2 * ML_HEADS

ROUTE_ROWS = 40
TOKEN_TILE = 512
MOE_ROWS = 512
DISPATCH_TILE = 1024
MIXER_BATCHES = 2
VMEM_LIMIT = 56 * 1024 * 1024


def _dot(a, b):
    return jnp.dot(a, b, preferred_element_type=F32)


def _dot_nt(a, b, precision=None):
    return lax.dot_general(a, b, (((1,), (1,)), ((), ())), preferred_element_type=F32, precision=precision)


def _dot_tn(a, b):
    return lax.dot_general(a, b, (((0,), (0,)), ((), ())), preferred_element_type=F32)


def _rms(x, w):
    return x * lax.rsqrt(jnp.mean(x * x, axis=-1, keepdims=True) + EPS) * w


def _sigmoid(x):
    return 1.0 / (1.0 + jnp.exp(-x))


def _silu(x):
    return x * _sigmoid(x)


def _log_sigmoid(x):
    return jnp.minimum(x, 0.0) - jnp.log1p(jnp.exp(-jnp.abs(x)))


def _head_norm(h):
    mu = jnp.mean(h, axis=-1, keepdims=True)
    d = h - mu
    var = jnp.mean(d * d, axis=-1, keepdims=True)
    return d * lax.rsqrt(var + EPS)


def _pack_rows(lo, hi):
    def rne(t):
        b = lax.bitcast_convert_type(t, jnp.uint32)
        return b + jnp.uint32(0x7FFF) + ((b >> 16) & jnp.uint32(1))
    return (rne(lo) >> 16) | (rne(hi) & jnp.uint32(0xFFFF0000))


def _unpack_rows(u):
    lo = lax.bitcast_convert_type(u << 16, F32)
    hi = lax.bitcast_convert_type(u & jnp.uint32(0xFFFF0000), F32)
    return lo, hi


def _inproj_kernel(x_ref, nw_ref, w_ref, wif_ref, wift_ref, proj_ref, g_ref, gt_ref):
    h = _rms(x_ref[...], nw_ref[...]).astype(BF16)
    for j in range(0, MAIN_WIDTH, 512):
        proj_ref[:, j:j + 512] = _dot(h, w_ref[:, j:j + 512]).astype(BF16)
    g_ref[...] = _dot(h, wif_ref[...])
    gt_ref[...] = _dot_nt(wift_ref[...], h)


def _inproj(xf, norm_w, w_main, w_if, w_ift):
    n, d = xf.shape
    tm = TOKEN_TILE
    return pl.pallas_call(
        _inproj_kernel,
        grid=(n // tm,),
        in_specs=[
            pl.BlockSpec((tm, d), lambda i: (i, 0)),
            pl.BlockSpec((1, d), lambda i: (0, 0)),
            pl.BlockSpec((d, MAIN_WIDTH), lambda i: (0, 0)),
            pl.BlockSpec((d, N_GATES), lambda i: (0, 0)),
            pl.BlockSpec((N_GATES, d), lambda i: (0, 0)),
        ],
        out_specs=[
            pl.BlockSpec((tm, MAIN_WIDTH), lambda i: (i, 0)),
            pl.BlockSpec((tm, N_GATES), lambda i: (i, 0)),
            pl.BlockSpec((N_GATES, tm), lambda i: (0, i)),
        ],
        out_shape=[
            jax.ShapeDtypeStruct((n, MAIN_WIDTH), BF16),
            jax.ShapeDtypeStruct((n, N_GATES), F32),
            jax.ShapeDtypeStruct((N_GATES, n), F32),
        ],
        compiler_params=pltpu.CompilerParams(
            dimension_semantics=("arbitrary",), vmem_limit_bytes=VMEM_LIMIT),
        name="inproj",
    )(xf, norm_w, w_main, w_if, w_ift)


def _mixer_kernel(*refs):
    @pl.when(pl.program_id(1) == 0)
    def _():
        for state_ref in refs[-5:]:
            state_ref[...] = jnp.zeros_like(state_ref)

    for bi in range(refs[0].shape[0]):
        _mixer_one(bi, *refs)


def _mixer_one(bi, proj_ref, g_ref, gt_ref, cos_ref, sin_ref, qdec_ref, kdec_ref, dmat_ref,
               bmask_ref, cdec_ref, hmask_ref, tril_ref, triu_ref, shift_ref, retw_ref, mlw_ref,
               convw_ref, convb_ref, gbc_ref, gbr_ref,
               out_ref, r_ref, c_ref, n_ref, m_ref, xbuf_ref):
    L = CHUNK
    proj_ref, g_ref, gt_ref, out_ref = proj_ref.at[bi], g_ref.at[bi], gt_ref.at[bi], out_ref.at[bi]
    r_ref, c_ref, n_ref, m_ref, xbuf_ref = r_ref.at[bi], c_ref.at[bi], n_ref.at[bi], m_ref.at[bi], xbuf_ref.at[bi]
    chunk = pl.program_id(1)

    cos = cos_ref[...]
    sin = sin_ref[...]
    half = RET_QK // 2

    def rope(t):
        t1, t2 = t[:, :half], t[:, half:]
        return jnp.concatenate([t1 * cos - t2 * sin, t1 * sin + t2 * cos], axis=1)

    q = rope(proj_ref[:, OFF_RQ:OFF_RQ + RET_QK].astype(F32))
    k = rope(proj_ref[:, OFF_RK:OFF_RK + RET_QK].astype(F32)) * (RET_DK ** -0.5)
    v = proj_ref[:, OFF_RV:OFF_RV + RET_V]
    k_b = k.astype(BF16)
    r_prev = r_ref[...]
    cross = _dot((q * qdec_ref[...]).astype(BF16), r_prev.astype(BF16))
    kv = _dot_tn((k * kdec_ref[...]).astype(BF16), v) * bmask_ref[...]
    r_ref[...] = cdec_ref[...] * r_prev + kv
    ret_heads = []
    for h in range(RET_HEADS):
        qh = (q * hmask_ref[h:h + 1, :]).astype(BF16)
        sc = _dot_nt(qh, k_b) * dmat_ref[h]
        inner = _dot(sc.astype(BF16), v[:, h * RET_DV:(h + 1) * RET_DV])
        ret_heads.append(_head_norm(inner + cross[:, h * RET_DV:(h + 1) * RET_DV]))
    ret = jnp.concatenate(ret_heads, axis=1) * retw_ref[...]
    ret = ret * _silu(proj_ref[:, OFF_RG:OFF_RG + RET_V].astype(F32))
    out_ref[:, 0:RET_V] = ret.astype(BF16)

    parity = chunk % 2
    x_b = proj_ref[:, OFF_MQK:OFF_MQK + 2 * ML_QK]
    xbuf_ref[pl.ds(pl.multiple_of(parity * L, L), L), :] = x_b
    shifted = _dot(shift_ref[parity], xbuf_ref[...])
    acc = x_b.astype(F32) * convw_ref[CONV_W - 1:CONV_W, :] + convb_ref[...]
    for s in range(1, CONV_W):
        acc = acc + shifted[(s - 1) * L:s * L] * convw_ref[CONV_W - 1 - s:CONV_W - s, :]
    qk = _silu(acc)
    mq = qk[:, :ML_QK]
    mk = qk[:, ML_QK:] * (ML_DK ** -0.5)
    mv = proj_ref[:, OFF_MV:OFF_MV + ML_V]

    gc = g_ref[...] + gbc_ref[...]
    gr = gt_ref[...] + gbr_ref[...]
    b_c = jnp.dot(tril_ref[...], _log_sigmoid(gc), preferred_element_type=F32, precision=HIGHEST)
    b_r = jnp.dot(_log_sigmoid(gr), triu_ref[...], preferred_element_type=F32, precision=HIGHEST)
    causal = (lax.broadcasted_iota(jnp.int32, (L, L), 0) >= lax.broadcasted_iota(jnp.int32, (L, L), 1))
    ml_heads = []
    for h in range(ML_HEADS):
        bc = b_c[:, ML_HEADS + h:ML_HEADS + h + 1]
        br = b_r[ML_HEADS + h:ML_HEADS + h + 1, :]
        igc = gc[:, h:h + 1]
        igr = gr[h:h + 1, :]
        btot = br[:, L - 1:L]
        log_d = jnp.where(causal, bc - br + igr, -jnp.inf)
        m_intra = jnp.max(log_d, axis=1, keepdims=True)
        m_loc = jnp.max(btot - br + igr, axis=1, keepdims=True)
        wa = jnp.exp(btot - bc + igc - m_loc)
        qh = mq[:, h * ML_DK:(h + 1) * ML_DK]
        kh = mk[:, h * ML_DK:(h + 1) * ML_DK]
        vh = mv[:, h * ML_DV:(h + 1) * ML_DV]
        kw = kh * wa
        kv_loc = _dot_tn(kw.astype(BF16), vh)
        n_loc = jnp.sum(kw, axis=0, keepdims=True)
        c_prev = c_ref[h]
        n_prev = n_ref[h][0:1, :]
        m_prev = m_ref[h][0:1, 0:1]
        m_inter = bc + m_prev
        m_t = jnp.maximum(m_intra, m_inter)
        qh_b = qh.astype(BF16)
        s_mat = _dot_nt(qh_b, kh.astype(BF16)) * jnp.exp(log_d - m_t)
        inter = jnp.exp(m_inter - m_t)
        num = _dot(s_mat.astype(BF16), vh) + inter * _dot(qh_b, c_prev.astype(BF16))
        den = jnp.sum(s_mat, axis=1, keepdims=True) + inter * jnp.sum(qh * n_prev, axis=1, keepdims=True)
        hh = num / jnp.maximum(jnp.abs(den), jnp.exp(-m_t))
        ml_heads.append(_head_norm(hh))
        m_new = jnp.maximum(btot + m_prev, m_loc)
        s_old = jnp.exp(btot + m_prev - m_new)
        s_loc = jnp.exp(m_loc - m_new)
        c_ref[h] = s_old * c_prev + s_loc * kv_loc
        n_ref[h] = jnp.broadcast_to(s_old * n_prev + s_loc * n_loc, (8, ML_DK))
        m_ref[h] = jnp.broadcast_to(m_new, (8, 128))
    ml = jnp.concatenate(ml_heads, axis=1) * mlw_ref[...]
    ml = ml * _sigmoid(proj_ref[:, OFF_MO:OFF_MO + ML_V].astype(F32))
    out_ref[:, RET_V:RET_V + ML_V] = ml.astype(BF16)


def _mixer_tables(seq):
    L = CHUNK
    half = RET_DK // 2
    inv = ROPE_BASE ** (-np.arange(half, dtype=np.float64) / half)
    ang = np.arange(seq, dtype=np.float64)[:, None] * inv[None, :].astype(np.float32).astype(np.float64)
    cos = np.tile(np.cos(ang), (1, RET_HEADS)).astype(np.float32)
    sin = np.tile(np.sin(ang), (1, RET_HEADS)).astype(np.float32)
    log_g = np.log1p(-np.exp2(-5.0 - np.arange(RET_HEADS, dtype=np.float64)))
    n = np.arange(L, dtype=np.float64)
    lane_head = (np.arange(RET_QK) % (RET_QK // 2)) // half
    qdec = np.exp((n + 1)[:, None] * log_g[lane_head][None, :]).astype(np.float32)
    kdec = np.exp((L - 1 - n)[:, None] * log_g[lane_head][None, :]).astype(np.float32)
    diff = n[:, None] - n[None, :]
    dmat = np.where(diff >= 0, np.exp(log_g[:, None, None] * np.maximum(diff, 0.0)[None]), 0.0).astype(np.float32)
    col_head = np.arange(RET_V) // RET_DV
    bmask = (lane_head[:, None] == col_head[None, :]).astype(np.float32)
    cdec = np.exp(L * log_g[col_head])[None, :].astype(np.float32)
    hmask = (lane_head[None, :] == np.arange(RET_HEADS)[:, None]).astype(np.float32)
    hmask = np.concatenate([hmask, np.zeros((8 - RET_HEADS, RET_QK), np.float32)], axis=0)
    tril = np.tril(np.ones((L, L), np.float32))
    shift = np.zeros((2, (CONV_W - 1) * L, 2 * L), np.float32)
    for p in range(2):
        for s in range(1, CONV_W):
            for t in range(L):
                col = p * L + t - s if t >= s else (1 - p) * L + L + t - s
                shift[p, (s - 1) * L + t, col] = 1.0
    return dict(cos=cos, sin=sin, qdec=qdec, kdec=kdec, dmat=dmat, bmask=bmask, cdec=cdec, hmask=hmask,
                tril=tril, triu=np.ascontiguousarray(tril.T), shift=shift)


def _mixer(proj, g, gt, tabs, ret_norm_w, ml_norm_w, conv_w, conv_b, gate_b, batch, seq):
    L = CHUNK
    nc = seq // L
    n = batch * seq
    nb = MIXER_BATCHES if batch % MIXER_BATCHES == 0 else 1
    proj = proj.reshape(batch, seq, MAIN_WIDTH)
    g = g.reshape(batch, seq, N_GATES)
    gt = gt.reshape(N_GATES, batch, seq).transpose(1, 0, 2)
    const2 = lambda b, c: (0, 0)
    const3 = lambda b, c: (0, 0, 0)
    tok = lambda b, c: (b, c, 0)
    in_specs = [
        pl.BlockSpec((nb, L, MAIN_WIDTH), tok),
        pl.BlockSpec((nb, L, N_GATES), tok),
        pl.BlockSpec((nb, N_GATES, L), lambda b, c: (b, 0, c)),
        pl.BlockSpec((L, RET_QK // 2), lambda b, c: (c, 0)),
        pl.BlockSpec((L, RET_QK // 2), lambda b, c: (c, 0)),
        pl.BlockSpec((L, RET_QK), const2),
        pl.BlockSpec((L, RET_QK), const2),
        pl.BlockSpec((RET_HEADS, L, L), const3),
        pl.BlockSpec((RET_QK, RET_V), const2),
        pl.BlockSpec((1, RET_V), const2),
        pl.BlockSpec((8, RET_QK), const2),
        pl.BlockSpec((L, L), const2),
        pl.BlockSpec((L, L), const2),
        pl.BlockSpec((2, (CONV_W - 1) * L, 2 * L), const3),
        pl.BlockSpec((1, RET_V), const2),
        pl.BlockSpec((1, ML_V), const2),
        pl.BlockSpec((CONV_W, 2 * ML_QK), const2),
        pl.BlockSpec((1, 2 * ML_QK), const2),
        pl.BlockSpec((1, N_GATES), const2),
        pl.BlockSpec((N_GATES, 1), const2),
    ]
    return pl.pallas_call(
        _mixer_kernel,
        grid=(batch // nb, nc),
        in_specs=in_specs,
        out_specs=pl.BlockSpec((nb, L, RET_V + ML_V), tok),
        out_shape=jax.ShapeDtypeStruct((batch, seq, RET_V + ML_V), BF16),
        scratch_shapes=[
            pltpu.VMEM((nb, RET_QK, RET_V), F32),
            pltpu.VMEM((nb, ML_HEADS, ML_DK, ML_DV), F32),
            pltpu.VMEM((nb, ML_HEADS, 8, ML_DK), F32),
            pltpu.VMEM((nb, ML_HEADS, 8, 128), F32),
            pltpu.VMEM((nb, 2 * L, 2 * ML_QK), BF16),
        ],
        compiler_params=pltpu.CompilerParams(
            dimension_semantics=("arbitrary", "arbitrary"), vmem_limit_bytes=VMEM_LIMIT),
        name="mixer",
    )(proj, g, gt, tabs["cos"], tabs["sin"], tabs["qdec"], tabs["kdec"], tabs["dmat"], tabs["bmask"],
      tabs["cdec"], tabs["hmask"], tabs["tril"], tabs["triu"], tabs["shift"].astype(BF16), ret_norm_w, ml_norm_w,
      conv_w, conv_b, gate_b.reshape(1, N_GATES), gate_b.reshape(N_GATES, 1)).reshape(n, RET_V + ML_V)


def _memkv_kernel(mem_ref, nw_ref, wkv_ref, k_ref, v_ref):
    d = mem_ref.shape[-1]
    mn = _rms(mem_ref[0], nw_ref[...]).astype(BF16)
    k_ref[0] = _dot(mn, wkv_ref[:, :d]).astype(BF16)
    v_ref[0] = _dot(mn, wkv_ref[:, d:]).astype(BF16)


def _memkv(mem, norm_w, wkv):
    b, m, d = mem.shape
    return pl.pallas_call(
        _memkv_kernel,
        grid=(b,),
        in_specs=[
            pl.BlockSpec((1, m, d), lambda i: (i, 0, 0)),
            pl.BlockSpec((1, d), lambda i: (0, 0)),
            pl.BlockSpec((d, 2 * d), lambda i: (0, 0)),
        ],
        out_specs=[pl.BlockSpec((1, m, d), lambda i: (i, 0, 0))] * 2,
        out_shape=[jax.ShapeDtypeStruct((b, m, d), BF16)] * 2,
        compiler_params=pltpu.CompilerParams(
            dimension_semantics=("arbitrary",), vmem_limit_bytes=VMEM_LIMIT),
        name="memkv",
    )(mem, norm_w, wkv)


def _attn_route_kernel(x_ref, mix_ref, k_ref, v_ref, wout_ref, nxa_ref, wq_ref, wo_ref, nmoe_ref,
                       wr_ref, br_ref, sut_ref,
                       x2_ref, h3_ref, ri_ref, rw_ref, cnt_ref, carry_ref):
    tm, d = x_ref.shape
    dh = d // XA_HEADS

    @pl.when((pl.program_id(0) == 0) & (pl.program_id(1) == 0))
    def _():
        carry_ref[...] = jnp.zeros_like(carry_ref)

    x1 = x_ref[...] + _dot(mix_ref[...], wout_ref[...])
    h2 = _rms(x1, nxa_ref[...]).astype(BF16)
    q = _dot(h2, wq_ref[...]).astype(BF16)
    heads = []
    for h in range(XA_HEADS):
        logits = _dot_nt(q[:, h * dh:(h + 1) * dh], k_ref[0, :, h * dh:(h + 1) * dh]) * (dh ** -0.5)
        mx = jnp.max(logits, axis=-1, keepdims=True)
        e = jnp.exp(logits - mx)
        p = (e / jnp.sum(e, axis=-1, keepdims=True)).astype(BF16)
        heads.append(_dot(p, v_ref[0, :, h * dh:(h + 1) * dh]).astype(BF16))
    x2 = x1 + _dot(jnp.concatenate(heads, axis=1), wo_ref[...])
    x2_ref[...] = x2
    h3 = _rms(x2, nmoe_ref[...])
    h3_ref[...] = _pack_rows(h3[:, :d // 2], h3[:, d // 2:])

    lt = _dot_nt(wr_ref[...], h3, precision=HIGHEST) + br_ref[...]
    gl = lt[N_EXPERTS:N_EXPERTS + N_GROUPS]
    gmax = jnp.max(gl, axis=0, keepdims=True)
    g_w = 1.0 / jnp.sum(jnp.exp(gl - gmax), axis=0, keepdims=True)
    giota = lax.broadcasted_iota(jnp.int32, gl.shape, 0)
    g_sel = jnp.min(jnp.where(gl == gmax, giota, N_GROUPS), axis=0, keepdims=True)
    el = lt[0:N_EXPERTS]
    eiota = lax.broadcasted_iota(jnp.int32, el.shape, 0)
    in_grp = (eiota // EXP_PER_GROUP) == g_sel
    elm = jnp.where(in_grp, el, -jnp.inf)
    m1 = jnp.max(elm, axis=0, keepdims=True)
    esum = jnp.sum(jnp.where(in_grp, jnp.exp(el - m1), 0.0), axis=0, keepdims=True)
    i1 = jnp.min(jnp.where(elm == m1, eiota, N_EXPERTS), axis=0, keepdims=True)
    elm2 = jnp.where(eiota == i1, -jnp.inf, elm)
    m2 = jnp.max(elm2, axis=0, keepdims=True)
    i2 = jnp.min(jnp.where(elm2 == m2, eiota, N_EXPERTS), axis=0, keepdims=True)
    p1 = 1.0 / esum
    p2 = jnp.exp(m2 - m1) / esum
    psum = p1 + p2
    w1 = g_w * (p1 / psum)
    w2 = g_w * (p2 / psum)

    oh1 = (eiota == i1).astype(F32)
    oh2 = (eiota == i2).astype(F32)
    cnt = oh1 + oh2
    base = carry_ref[:, 0:1] + _dot(cnt.astype(BF16), sut_ref[...])
    r1 = jnp.sum(oh1 * base, axis=0, keepdims=True)
    r2 = jnp.sum(oh2 * base, axis=0, keepdims=True)
    new_carry = carry_ref[...] + jnp.sum(cnt, axis=1, keepdims=True)
    carry_ref[...] = new_carry
    cnt_ref[...] = new_carry

    zi = jnp.zeros((4, tm), jnp.int32)
    ri_ref[...] = jnp.concatenate([i1, i2, r1.astype(jnp.int32), r2.astype(jnp.int32), zi], axis=0)
    rw_ref[...] = jnp.concatenate([w1, w2, jnp.zeros((6, tm), F32)], axis=0)


def _attn_route(xf, mixed, kmem, vmem, w_out, norm_xa_w, wq, wo, norm_moe_w, w_route_t, b_route, sut,
                batch, seq):
    n, d = xf.shape
    tm = TOKEN_TILE
    nt = seq // tm
    m = kmem.shape[1]
    tok = lambda b, t: (b * nt + t, 0)
    lane_tok = lambda b, t: (0, b * nt + t)
    const2 = lambda b, t: (0, 0)
    return pl.pallas_call(
        _attn_route_kernel,
        grid=(batch, nt),
        in_specs=[
            pl.BlockSpec((tm, d), tok),
            pl.BlockSpec((tm, d), tok),
            pl.BlockSpec((1, m, d), lambda b, t: (b, 0, 0)),
            pl.BlockSpec((1, m, d), lambda b, t: (b, 0, 0)),
            pl.BlockSpec((d, d), const2),
            pl.BlockSpec((1, d), const2),
            pl.BlockSpec((d, d), const2),
            pl.BlockSpec((d, d), const2),
            pl.BlockSpec((1, d), const2),
            pl.BlockSpec((ROUTE_ROWS, d), const2),
            pl.BlockSpec((ROUTE_ROWS, 1), const2),
            pl.BlockSpec((tm, tm), const2),
        ],
        out_specs=[
            pl.BlockSpec((tm, d), tok),
            pl.BlockSpec((tm, d // 2), tok),
            pl.BlockSpec((8, tm), lane_tok),
            pl.BlockSpec((8, tm), lane_tok),
            pl.BlockSpec((N_EXPERTS, 128), const2),
        ],
        out_shape=[
            jax.ShapeDtypeStruct((n, d), F32),
            jax.ShapeDtypeStruct((n, d // 2), jnp.uint32),
            jax.ShapeDtypeStruct((8, n), jnp.int32),
            jax.ShapeDtypeStruct((8, n), F32),
            jax.ShapeDtypeStruct((N_EXPERTS, 128), F32),
        ],
        scratch_shapes=[pltpu.VMEM((N_EXPERTS, 128), F32)],
        compiler_params=pltpu.CompilerParams(
            dimension_semantics=("arbitrary", "arbitrary"), vmem_limit_bytes=VMEM_LIMIT),
        name="attn_route",
    )(xf, mixed, kmem, vmem, w_out, norm_xa_w, wq, wo, norm_moe_w, w_route_t, b_route, sut)


def _dispatch_kernel(zpos_ref, dest_ref, h_ref, xs_ref, idx_ref, idx_sem, row_sem, zero_ref, zero_sem):
    i = pl.program_id(0)
    nsteps = pl.num_programs(0)
    td = h_ref.shape[0]
    bm = zero_ref.shape[0]
    slot = i % 2

    def idx_copy(step, sl):
        off = pl.multiple_of(sl * (2 * td), 2 * td)
        return pltpu.make_async_copy(dest_ref.at[step], idx_ref.at[pl.ds(off, 2 * td)], idx_sem.at[sl])

    @pl.when(i == 0)
    def _():
        zero_ref[...] = jnp.zeros_like(zero_ref)

        def zero_copy(e):
            return pltpu.make_async_copy(zero_ref, xs_ref.at[pl.ds(pl.multiple_of(zpos_ref[e], bm), bm), 0], zero_sem)

        def tail_copy(b):
            return pltpu.make_async_copy(zero_ref, xs_ref.at[pl.ds(pl.multiple_of(b * bm, bm), bm), 0], zero_sem)

        nused = zpos_ref[N_EXPERTS]
        nblk = xs_ref.shape[0] // bm
        for e in range(N_EXPERTS):
            pl.when(zpos_ref[e] >= 0)(lambda e=e: zero_copy(e).start())
        lax.fori_loop(nused, nblk, lambda b, c: (tail_copy(b).start(), c)[1], 0)
        for e in range(N_EXPERTS):
            pl.when(zpos_ref[e] >= 0)(lambda e=e: zero_copy(e).wait())
        lax.fori_loop(nused, nblk, lambda b, c: (tail_copy(b).wait(), c)[1], 0)
        idx_copy(0, 0).start()

    idx_copy(i, slot).wait()

    @pl.when(i + 1 < nsteps)
    def _():
        idx_copy(i + 1, 1 - slot).start()

    base = slot * (2 * td)

    def issue(t, carry):
        pltpu.make_async_copy(h_ref.at[pl.ds(t, 1)], xs_ref.at[idx_ref[base + t]], row_sem).start()
        pltpu.make_async_copy(h_ref.at[pl.ds(t, 1)], xs_ref.at[idx_ref[base + td + t]], row_sem).start()
        return carry

    lax.fori_loop(0, td, issue, 0, unroll=8)
    for _ in range(TOP_K):
        pltpu.make_async_copy(xs_ref.at[pl.ds(0, td)], xs_ref.at[pl.ds(0, td)], row_sem).wait()


def _dispatch(h3p, dest_tiles, zpos, cap):
    n, w = h3p.shape
    nt, td2 = dest_tiles.shape
    td = td2 // 2
    grid_spec = pltpu.PrefetchScalarGridSpec(
        num_scalar_prefetch=1,
        grid=(nt,),
        in_specs=[
            pl.BlockSpec(memory_space=pl.ANY),
            pl.BlockSpec((td, w), lambda i, zp: (i, 0)),
        ],
        out_specs=pl.BlockSpec(memory_space=pl.ANY),
        scratch_shapes=[
            pltpu.SMEM((2 * td2,), jnp.int32),
            pltpu.SemaphoreType.DMA((2,)),
            pltpu.SemaphoreType.DMA,
            pltpu.VMEM((MOE_ROWS, w), jnp.uint32),
            pltpu.SemaphoreType.DMA,
        ],
    )
    return pl.pallas_call(
        _dispatch_kernel,
        grid_spec=grid_spec,
        out_shape=jax.ShapeDtypeStruct((cap, 1, w), jnp.uint32),
        compiler_params=pltpu.CompilerParams(
            dimension_semantics=("arbitrary",), vmem_limit_bytes=VMEM_LIMIT),
        name="dispatch",
    )(zpos, dest_tiles, h3p)


def _expert_kernel(blk_e_ref, nused_ref, xs_ref, wg_ref, wu_ref, wd_ref, ys_ref, wg_b, wu_b, wd_b,
                   xbuf, ybuf, zbuf, in_sem, out_sem, zero_sem):
    i = pl.program_id(0)
    nsteps = pl.num_programs(0)
    nused = nused_ref[0]
    bm = xbuf.shape[1]
    slot = i % 2
    prev = blk_e_ref[jnp.maximum(i - 1, 0)]
    fresh = (i == 0) | (blk_e_ref[i] != prev)
    half = wd_b.shape[1] // 2

    def rows(ref, step):
        return ref.at[pl.ds(pl.multiple_of(step * bm, bm), bm), 0]

    def in_copy(step, sl):
        return pltpu.make_async_copy(rows(xs_ref, step), xbuf.at[sl], in_sem.at[sl])

    def out_copy(step, sl):
        return pltpu.make_async_copy(ybuf.at[sl], rows(ys_ref, step), out_sem.at[sl])

    def zero_copy(step):
        return pltpu.make_async_copy(zbuf, rows(ys_ref, step), zero_sem)

    @pl.when(i == 0)
    def _():
        zbuf[...] = jnp.zeros_like(zbuf)
        in_copy(0, 0).start()

    @pl.when(i + 1 < nused)
    def _():
        in_copy(i + 1, 1 - slot).start()

    @pl.when(fresh)
    def _():
        wg_b[...] = wg_ref[0].astype(BF16)
        wu_b[...] = wu_ref[0].astype(BF16)
        wd_b[...] = wd_ref[0].astype(BF16)

    @pl.when(i < nused)
    def _():
        in_copy(i, slot).wait()
        pl.when(i >= 2)(lambda: out_copy(i - 2, slot).wait())
        lo, hi = _unpack_rows(xbuf[slot])
        xb = jnp.concatenate([lo.astype(BF16), hi.astype(BF16)], axis=1)
        hid = (_silu(_dot(xb, wg_b[...])) * _dot(xb, wu_b[...])).astype(BF16)
        y = _dot(hid, wd_b[...])
        ybuf[slot] = _pack_rows(y[:, :half], y[:, half:])
        out_copy(i, slot).start()

    pl.when(i >= nused)(lambda: zero_copy(i).start())

    @pl.when(i == nsteps - 1)
    def _():
        pl.when(nused >= 2)(lambda: out_copy(nused - 2, nused % 2).wait())
        pl.when(nused >= 1)(lambda: out_copy(nused - 1, (nused - 1) % 2).wait())
        lax.fori_loop(nused, nsteps, lambda b, c: (zero_copy(b).wait(), c)[1], 0)


def _experts(xs, blk_e, nused, w_gate, w_up, w_down):
    cap, _, w = xs.shape
    _, d, de = w_gate.shape
    bm = MOE_ROWS
    grid_spec = pltpu.PrefetchScalarGridSpec(
        num_scalar_prefetch=2,
        grid=(cap // bm,),
        in_specs=[
            pl.BlockSpec(memory_space=pl.ANY),
            pl.BlockSpec((1, d, de), lambda i, be, nu: (be[i], 0, 0)),
            pl.BlockSpec((1, d, de), lambda i, be, nu: (be[i], 0, 0)),
            pl.BlockSpec((1, de, d), lambda i, be, nu: (be[i], 0, 0)),
        ],
        out_specs=pl.BlockSpec(memory_space=pl.ANY),
        scratch_shapes=[
            pltpu.VMEM((d, de), BF16),
            pltpu.VMEM((d, de), BF16),
            pltpu.VMEM((de, d), BF16),
            pltpu.VMEM((2, bm, w), jnp.uint32),
            pltpu.VMEM((2, bm, w), jnp.uint32),
            pltpu.VMEM((bm, w), jnp.uint32),
            pltpu.SemaphoreType.DMA((2,)),
            pltpu.SemaphoreType.DMA((2,)),
            pltpu.SemaphoreType.DMA,
        ],
    )
    return pl.pallas_call(
        _expert_kernel,
        grid_spec=grid_spec,
        out_shape=jax.ShapeDtypeStruct((cap, 1, w), jnp.uint32),
        compiler_params=pltpu.CompilerParams(
            dimension_semantics=("arbitrary",), vmem_limit_bytes=VMEM_LIMIT),
        name="experts",
    )(blk_e, nused, xs, w_gate, w_up, w_down)


def _combine_kernel(dest_ref, ys_ref, x2_ref, rw_ref, eye_ref, nw_ref, o_ref, idx_ref, idx_sem, ybuf, ysem):
    i = pl.program_id(0)
    nsteps = pl.num_programs(0)
    tc, d = x2_ref.shape
    half = d // 2
    n_idx = 2 * tc

    def idx_copy(step):
        sl = step % 3
        off = pl.multiple_of(sl * n_idx, n_idx)
        return pltpu.make_async_copy(dest_ref.at[step], idx_ref.at[pl.ds(off, n_idx)], idx_sem.at[sl])

    def gather(step):
        base = (step % 3) * n_idx
        buf = ybuf.at[step % 2]
        sem = ysem.at[step % 2]

        def issue(t, carry):
            pltpu.make_async_copy(ys_ref.at[idx_ref[base + t]], buf.at[pl.ds(t, 1)], sem).start()
            pltpu.make_async_copy(ys_ref.at[idx_ref[base + tc + t]], buf.at[pl.ds(tc + t, 1)], sem).start()
            return carry

        lax.fori_loop(0, tc, issue, 0, unroll=8)

    @pl.when(i == 0)
    def _():
        idx_copy(0).start()
        idx_copy(0).wait()
        gather(0)

        @pl.when(nsteps > 1)
        def _():
            idx_copy(1).start()

    @pl.when(i + 1 < nsteps)
    def _():
        idx_copy(i + 1).wait()

        @pl.when(i + 2 < nsteps)
        def _():
            idx_copy(i + 2).start()

        gather(i + 1)

    slot = i % 2
    pltpu.make_async_copy(ybuf.at[slot], ybuf.at[slot], ysem.at[slot]).wait()
    wcol = _dot_nt(eye_ref[...], rw_ref[...], precision=HIGHEST)
    lo1, hi1 = _unpack_rows(ybuf[slot, 0:tc])
    lo2, hi2 = _unpack_rows(ybuf[slot, tc:n_idx])
    w1 = wcol[:, 0:1]
    w2 = wcol[:, 1:2]
    z_lo = x2_ref[:, :half] + (lo1 * w1 + lo2 * w2)
    z_hi = x2_ref[:, half:] + (hi1 * w1 + hi2 * w2)
    ms = (jnp.sum(z_lo * z_lo, axis=-1, keepdims=True) + jnp.sum(z_hi * z_hi, axis=-1, keepdims=True)) / d
    scale = lax.rsqrt(ms + EPS)
    o_ref[:, :half] = z_lo * scale * nw_ref[:, :half]
    o_ref[:, half:] = z_hi * scale * nw_ref[:, half:]


def _combine(x2, ys, dest_tiles, rw, eye, norm_w):
    n, d = x2.shape
    nt, n_idx = dest_tiles.shape
    tc = n_idx // 2
    w = ys.shape[-1]
    return pl.pallas_call(
        _combine_kernel,
        grid=(nt,),
        in_specs=[
            pl.BlockSpec(memory_space=pl.ANY),
            pl.BlockSpec(memory_space=pl.ANY),
            pl.BlockSpec((tc, d), lambda i: (i, 0)),
            pl.BlockSpec((8, tc), lambda i: (0, i)),
            pl.BlockSpec((tc, tc), lambda i: (0, 0)),
            pl.BlockSpec((1, d), lambda i: (0, 0)),
        ],
        out_specs=pl.BlockSpec((tc, d), lambda i: (i, 0)),
        out_shape=jax.ShapeDtypeStruct((n, d), F32),
        scratch_shapes=[
            pltpu.SMEM((3 * n_idx,), jnp.int32),
            pltpu.SemaphoreType.DMA((3,)),
            pltpu.VMEM((2, n_idx, w), jnp.uint32),
            pltpu.SemaphoreType.DMA((2,)),
        ],
        compiler_params=pltpu.CompilerParams(
            dimension_semantics=("arbitrary",), vmem_limit_bytes=VMEM_LIMIT),
        name="combine",
    )(dest_tiles, ys, x2, rw, eye, norm_w)


def _rope_column_order():
    half = RET_DK // 2
    first = [h * RET_DK + j for h in range(RET_HEADS) for j in range(half)]
    second = [h * RET_DK + half + j for h in range(RET_HEADS) for j in range(half)]
    return np.array(first + second, dtype=np.int32)


def _layer(xf, mem, batch, seq, norm_mix_w, w_in, ret_norm_w, ml_conv_w, ml_conv_b, ml_gate_b, ml_norm_w,
           w_out, norm_xa_w, norm_mem_w, xa_wq, xa_wkv, xa_wo, norm_moe_w, moe_w_group, moe_b_group,
           moe_w_router, moe_b_router, moe_w_gate, moe_w_up, moe_w_down, final_norm_w):
    n, d = xf.shape
    perm = _rope_column_order()
    cols = np.concatenate([perm, RET_QK + perm, np.arange(2 * RET_QK, MAIN_WIDTH)])
    w_main = w_in[:, cols].astype(BF16)
    w_if = w_in[:, MAIN_WIDTH:].astype(BF16)
    proj, g, gt = _inproj(xf, norm_mix_w.reshape(1, d), w_main, w_if, w_if.T)

    tabs = {k_: jnp.asarray(v_) for k_, v_ in _mixer_tables(seq).items()}
    mixed = _mixer(proj, g, gt, tabs, ret_norm_w.reshape(1, RET_V), ml_norm_w.reshape(1, ML_V), ml_conv_w,
                   ml_conv_b.reshape(1, 2 * ML_QK), ml_gate_b, batch, seq)

    kmem, vmem = _memkv(mem, norm_mem_w.reshape(1, d), xa_wkv.astype(BF16))

    w_route_t = jnp.concatenate(
        [moe_w_router.T, moe_w_group.T, jnp.zeros((ROUTE_ROWS - N_EXPERTS - N_GROUPS, d), F32)], axis=0)
    b_route = jnp.concatenate(
        [moe_b_router, moe_b_group, jnp.zeros((ROUTE_ROWS - N_EXPERTS - N_GROUPS,), F32)]).reshape(ROUTE_ROWS, 1)
    tm = TOKEN_TILE
    sut = jnp.asarray(np.triu(np.ones((tm, tm), np.float32), 1), dtype=BF16)
    x2, h3, ri, rw, cnt = _attn_route(xf, mixed, kmem, vmem, w_out.astype(BF16), norm_xa_w.reshape(1, d),
                                      xa_wq.astype(BF16), xa_wo.astype(BF16), norm_moe_w.reshape(1, d),
                                      w_route_t, b_route, sut, batch, seq)

    bm = MOE_ROWS
    counts = cnt[:, 0].astype(jnp.int32)
    padded = (counts + bm - 1) // bm * bm
    pends = jnp.cumsum(padded)
    pstarts = pends - padded
    expert = ri[0:TOP_K]
    onehot = expert[None] == jnp.arange(N_EXPERTS, dtype=jnp.int32)[:, None, None]
    dest = jnp.sum(jnp.where(onehot, pstarts[:, None, None], 0), axis=0) + ri[TOP_K:2 * TOP_K]
    cap = n * TOP_K + N_EXPERTS * bm
    nblk = cap // bm
    blk_start = jnp.arange(nblk, dtype=jnp.int32) * bm
    blk_e = jnp.minimum(jnp.sum(blk_start[:, None] >= pends[None, :], axis=1), N_EXPERTS - 1).astype(jnp.int32)
    nused = (pends[-1] // bm).astype(jnp.int32).reshape(1)
    zpos = jnp.where(padded > counts, pends - bm, -1).astype(jnp.int32)
    zpos = jnp.concatenate([zpos, nused])

    def tiles(rows):
        return dest.reshape(TOP_K, n // rows, rows).transpose(1, 0, 2).reshape(n // rows, TOP_K * rows)

    xs = _dispatch(h3, tiles(DISPATCH_TILE), zpos, cap)
    ys = _experts(xs, blk_e, nused, moe_w_gate, moe_w_up, moe_w_down)
    eye = jnp.asarray(np.eye(tm, dtype=np.float32))
    return _combine(x2, ys, tiles(tm), rw, eye, final_norm_w.reshape(1, d))


def kernel(x, mem, norm_mix_w, w_in, ret_norm_w, ml_conv_w, ml_conv_b, ml_gate_b, ml_norm_w, w_out, norm_xa_w, norm_mem_w, xa_wq, xa_wkv, xa_wo, norm_moe_w, moe_w_group, moe_b_group, moe_w_router, moe_b_router, moe_w_gate, moe_w_up, moe_w_down, norm_final_w):
    batch, seq, d = x.shape
    depth = w_in.shape[0]
    assert depth == 1, "the final norm is fused into the last layer's combine kernel"
    l = 0
    out = _layer(x.reshape(batch * seq, d), mem, batch, seq, norm_mix_w[l], w_in[l], ret_norm_w[l], ml_conv_w[l],
                 ml_conv_b[l], ml_gate_b[l], ml_norm_w[l], w_out[l], norm_xa_w[l], norm_mem_w[l], xa_wq[l],
                 xa_wkv[l], xa_wo[l], norm_moe_w[l], moe_w_group[l], moe_b_group[l], moe_w_router[l],
                 moe_b_router[l], moe_w_gate[l], moe_w_up[l], moe_w_down[l], norm_final_w)
    return out.reshape(batch, seq, d)
```

```python
import functools

import numpy as np
import jax
import jax.numpy as jnp
from jax import lax
from jax.experimental import pallas as pl
from jax.experimental.pallas import tpu as pltpu

F32 = jnp.float32
BF16 = jnp.bfloat16
HIGHEST = lax.Precision.HIGHEST

CHUNK = 128
RET_HEADS = 4
RET_DK = 64
RET_DV = 128
ML_HEADS = 4
ML_DK = 128
ML_DV = 128
CONV_W = 4
XA_HEADS = 4
N_GROUPS = 4
EXP_PER_GROUP = 8
N_EXPERTS = N_GROUPS * EXP_PER_GROUP
TOP_K = 2
ROPE_BASE = 10000.0
EPS = 1e-6

RET_QK = RET_HEADS * RET_DK
RET_V = RET_HEADS * RET_DV
ML_QK = ML_HEADS * ML_DK
ML_V = ML_HEADS * ML_DV
OFF_RQ = 0
OFF_RK = OFF_RQ + RET_QK
OFF_RV = OFF_RK + RET_QK
OFF_RG = OFF_RV + RET_V
OFF_MQK = OFF_RG + RET_V
OFF_MV = OFF_MQK + 2 * ML_QK
OFF_MO = OFF_MV + ML_V
MAIN_WIDTH = OFF_MO + ML_V
N_GATES = 2 * ML_HEADS

ROUTE_ROWS = 40
TOKEN_TILE = 512
MOE_ROWS = 512
DISPATCH_TILE = 1024
MIXER_BATCHES = 2
VMEM_LIMIT = 56 * 1024 * 1024


def _dot(a, b):
    return jnp.dot(a, b, preferred_element_type=F32)


def _dot_nt(a, b, precision=None):
    return lax.dot_general(a, b, (((1,), (1,)), ((), ())), preferred_element_type=F32, precision=precision)


def _dot_tn(a, b):
    return lax.dot_general(a, b, (((0,), (0,)), ((), ())), preferred_element_type=F32)


def _rms(x, w):
    return x * lax.rsqrt(jnp.mean(x * x, axis=-1, keepdims=True) + EPS) * w


def _sigmoid(x):
    return 1.0 / (1.0 + jnp.exp(-x))


def _silu(x):
    return x * _sigmoid(x)


def _log_sigmoid(x):
    return jnp.minimum(x, 0.0) - jnp.log1p(jnp.exp(-jnp.abs(x)))


def _head_norm(h):
    mu = jnp.mean(h, axis=-1, keepdims=True)
    d = h - mu
    var = jnp.mean(d * d, axis=-1, keepdims=True)
    return d * lax.rsqrt(var + EPS)


def _pack_rows(lo, hi):
    def rne(t):
        b = lax.bitcast_convert_type(t, jnp.uint32)
        return b + jnp.uint32(0x7FFF) + ((b >> 16) & jnp.uint32(1))
    return (rne(lo) >> 16) | (rne(hi) & jnp.uint32(0xFFFF0000))


def _unpack_rows(u):
    lo = lax.bitcast_convert_type(u << 16, F32)
    hi = lax.bitcast_convert_type(u & jnp.uint32(0xFFFF0000), F32)
    return lo, hi


def _inproj_kernel(x_ref, nw_ref, w_ref, wif_ref, wift_ref, proj_ref, g_ref, gt_ref):
    h = _rms(x_ref[...], nw_ref[...]).astype(BF16)
    for j in range(0, MAIN_WIDTH, 512):
        proj_ref[:, j:j + 512] = _dot(h, w_ref[:, j:j + 512]).astype(BF16)
    g_ref[...] = _dot(h, wif_ref[...])
    gt_ref[...] = _dot_nt(wift_ref[...], h)


def _inproj(xf, norm_w, w_main, w_if, w_ift):
    n, d = xf.shape
    tm = TOKEN_TILE
    return pl.pallas_call(
        _inproj_kernel,
        grid=(n // tm,),
        in_specs=[
            pl.BlockSpec((tm, d), lambda i: (i, 0)),
            pl.BlockSpec((1, d), lambda i: (0, 0)),
            pl.BlockSpec((d, MAIN_WIDTH), lambda i: (0, 0)),
            pl.BlockSpec((d, N_GATES), lambda i: (0, 0)),
            pl.BlockSpec((N_GATES, d), lambda i: (0, 0)),
        ],
        out_specs=[
            pl.BlockSpec((tm, MAIN_WIDTH), lambda i: (i, 0)),
            pl.BlockSpec((tm, N_GATES), lambda i: (i, 0)),
            pl.BlockSpec((N_GATES, tm), lambda i: (0, i)),
        ],
        out_shape=[
            jax.ShapeDtypeStruct((n, MAIN_WIDTH), BF16),
            jax.ShapeDtypeStruct((n, N_GATES), F32),
            jax.ShapeDtypeStruct((N_GATES, n), F32),
        ],
        compiler_params=pltpu.CompilerParams(
            dimension_semantics=("arbitrary",), vmem_limit_bytes=VMEM_LIMIT),
        name="inproj",
    )(xf, norm_w, w_main, w_if, w_ift)


def _mixer_kernel(*refs):
    @pl.when(pl.program_id(1) == 0)
    def _():
        for state_ref in refs[-5:]:
            state_ref[...] = jnp.zeros_like(state_ref)

    for bi in range(refs[0].shape[0]):
        _mixer_one(bi, *refs)


def _mixer_one(bi, proj_ref, g_ref, gt_ref, cos_ref, sin_ref, qdec_ref, kdec_ref, dmat_ref,
               bmask_ref, cdec_ref, hmask_ref, tril_ref, triu_ref, shift_ref, retw_ref, mlw_ref,
               convw_ref, convb_ref, gbc_ref, gbr_ref,
               out_ref, r_ref, c_ref, n_ref, m_ref, xbuf_ref):
    L = CHUNK
    proj_ref, g_ref, gt_ref, out_ref = proj_ref.at[bi], g_ref.at[bi], gt_ref.at[bi], out_ref.at[bi]
    r_ref, c_ref, n_ref, m_ref, xbuf_ref = r_ref.at[bi], c_ref.at[bi], n_ref.at[bi], m_ref.at[bi], xbuf_ref.at[bi]
    chunk = pl.program_id(1)

    cos = cos_ref[...]
    sin = sin_ref[...]
    half = RET_QK // 2

    def rope(t):
        t1, t2 = t[:, :half], t[:, half:]
        return jnp.concatenate([t1 * cos - t2 * sin, t1 * sin + t2 * cos], axis=1)

    q = rope(proj_ref[:, OFF_RQ:OFF_RQ + RET_QK].astype(F32))
    k = rope(proj_ref[:, OFF_RK:OFF_RK + RET_QK].astype(F32)) * (RET_DK ** -0.5)
    v = proj_ref[:, OFF_RV:OFF_RV + RET_V]
    k_b = k.astype(BF16)
    r_prev = r_ref[...]
    cross = _dot((q * qdec_ref[...]).astype(BF16), r_prev.astype(BF16))
    kv = _dot_tn((k * kdec_ref[...]).astype(BF16), v) * bmask_ref[...]
    r_ref[...] = cdec_ref[...] * r_prev + kv
    ret_heads = []
    for h in range(RET_HEADS):
        qh = (q * hmask_ref[h:h + 1, :]).astype(BF16)
        sc = _dot_nt(qh, k_b) * dmat_ref[h]
        inner = _dot(sc.astype(BF16), v[:, h * RET_DV:(h + 1) * RET_DV])
        ret_heads.append(_head_norm(inner + cross[:, h * RET_DV:(h + 1) * RET_DV]))
    ret = jnp.concatenate(ret_heads, axis=1) * retw_ref[...]
    ret = ret * _silu(proj_ref[:, OFF_RG:OFF_RG + RET_V].astype(F32))
    out_ref[:, 0:RET_V] = ret.astype(BF16)

    parity = chunk % 2
    x_b = proj_ref[:, OFF_MQK:OFF_MQK + 2 * ML_QK]
    xbuf_ref[pl.ds(pl.multiple_of(parity * L, L), L), :] = x_b
    shifted = _dot(shift_ref[parity], xbuf_ref[...])
    acc = x_b.astype(F32) * convw_ref[CONV_W - 1:CONV_W, :] + convb_ref[...]
    for s in range(1, CONV_W):
        acc = acc + shifted[(s - 1) * L:s * L] * convw_ref[CONV_W - 1 - s:CONV_W - s, :]
    qk = _silu(acc)
    mq = qk[:, :ML_QK]
    mk = qk[:, ML_QK:] * (ML_DK ** -0.5)
    mv = proj_ref[:, OFF_MV:OFF_MV + ML_V]

    gc = g_ref[...] + gbc_ref[...]
    gr = gt_ref[...] + gbr_ref[...]
    b_c = jnp.dot(tril_ref[...], _log_sigmoid(gc), preferred_element_type=F32, precision=HIGHEST)
    b_r = jnp.dot(_log_sigmoid(gr), triu_ref[...], preferred_element_type=F32, precision=HIGHEST)
    causal = (lax.broadcasted_iota(jnp.int32, (L, L), 0) >= lax.broadcasted_iota(jnp.int32, (L, L), 1))
    ml_heads = []
    for h in range(ML_HEADS):
        bc = b_c[:, ML_HEADS + h:ML_HEADS + h + 1]
        br = b_r[ML_HEADS + h:ML_HEADS + h + 1, :]
        igc = gc[:, h:h + 1]
        igr = gr[h:h + 1, :]
        btot = br[:, L - 1:L]
        log_d = jnp.where(causal, bc - br + igr, -jnp.inf)
        m_intra = jnp.max(log_d, axis=1, keepdims=True)
        m_loc = jnp.max(btot - br + igr, axis=1, keepdims=True)
        wa = jnp.exp(btot - bc + igc - m_loc)
        qh = mq[:, h * ML_DK:(h + 1) * ML_DK]
        kh = mk[:, h * ML_DK:(h + 1) * ML_DK]
        vh = mv[:, h * ML_DV:(h + 1) * ML_DV]
        kw = kh * wa
        kv_loc = _dot_tn(kw.astype(BF16), vh)
        n_loc = jnp.sum(kw, axis=0, keepdims=True)
        c_prev = c_ref[h]
        n_prev = n_ref[h][0:1, :]
        m_prev = m_ref[h][0:1, 0:1]
        m_inter = bc + m_prev
        m_t = jnp.maximum(m_intra, m_inter)
        qh_b = qh.astype(BF16)
        s_mat = _dot_nt(qh_b, kh.astype(BF16)) * jnp.exp(log_d - m_t)
        inter = jnp.exp(m_inter - m_t)
        num = _dot(s_mat.astype(BF16), vh) + inter * _dot(qh_b, c_prev.astype(BF16))
        den = jnp.sum(s_mat, axis=1, keepdims=True) + inter * jnp.sum(qh * n_prev, axis=1, keepdims=True)
        hh = num / jnp.maximum(jnp.abs(den), jnp.exp(-m_t))
        ml_heads.append(_head_norm(hh))
        m_new = jnp.maximum(btot + m_prev, m_loc)
        s_old = jnp.exp(btot + m_prev - m_new)
        s_loc = jnp.exp(m_loc - m_new)
        c_ref[h] = s_old * c_prev + s_loc * kv_loc
        n_ref[h] = jnp.broadcast_to(s_old * n_prev + s_loc * n_loc, (8, ML_DK))
        m_ref[h] = jnp.broadcast_to(m_new, (8, 128))
    ml = jnp.concatenate(ml_heads, axis=1) * mlw_ref[...]
    ml = ml * _sigmoid(proj_ref[:, OFF_MO:OFF_MO + ML_V].astype(F32))
    out_ref[:, RET_V:RET_V + ML_V] = ml.astype(BF16)


def _mixer_tables(seq):
    L = CHUNK
    half = RET_DK // 2
    inv = ROPE_BASE ** (-np.arange(half, dtype=np.float64) / half)
    ang = np.arange(seq, dtype=np.float64)[:, None] * inv[None, :].astype(np.float32).astype(np.float64)
    cos = np.tile(np.cos(ang), (1, RET_HEADS)).astype(np.float32)
    sin = np.tile(np.sin(ang), (1, RET_HEADS)).astype(np.float32)
    log_g = np.log1p(-np.exp2(-5.0 - np.arange(RET_HEADS, dtype=np.float64)))
    n = np.arange(L, dtype=np.float64)
    lane_head = (np.arange(RET_QK) % (RET_QK // 2)) // half
    qdec = np.exp((n + 1)[:, None] * log_g[lane_head][None, :]).astype(np.float32)
    kdec = np.exp((L - 1 - n)[:, None] * log_g[lane_head][None, :]).astype(np.float32)
    diff = n[:, None] - n[None, :]
    dmat = np.where(diff >= 0, np.exp(log_g[:, None, None] * np.maximum(diff, 0.0)[None]), 0.0).astype(np.float32)
    col_head = np.arange(RET_V) // RET_DV
    bmask = (lane_head[:, None] == col_head[None, :]).astype(np.float32)
    cdec = np.exp(L * log_g[col_head])[None, :].astype(np.float32)
    hmask = (lane_head[None, :] == np.arange(RET_HEADS)[:, None]).astype(np.float32)
    hmask = np.concatenate([hmask, np.zeros((8 - RET_HEADS, RET_QK), np.float32)], axis=0)
    tril = np.tril(np.ones((L, L), np.float32))
    shift = np.zeros((2, (CONV_W - 1) * L, 2 * L), np.float32)
    for p in range(2):
        for s in range(1, CONV_W):
            for t in range(L):
                col = p * L + t - s if t >= s else (1 - p) * L + L + t - s
                shift[p, (s - 1) * L + t, col] = 1.0
    return dict(cos=cos, sin=sin, qdec=qdec, kdec=kdec, dmat=dmat, bmask=bmask, cdec=cdec, hmask=hmask,
                tril=tril, triu=np.ascontiguousarray(tril.T), shift=shift)


def _mixer(proj, g, gt, tabs, ret_norm_w, ml_norm_w, conv_w, conv_b, gate_b, batch, seq):
    L = CHUNK
    nc = seq // L
    n = batch * seq
    nb = MIXER_BATCHES if batch % MIXER_BATCHES == 0 else 1
    proj = proj.reshape(batch, seq, MAIN_WIDTH)
    g = g.reshape(batch, seq, N_GATES)
    gt = gt.reshape(N_GATES, batch, seq).transpose(1, 0, 2)
    const2 = lambda b, c: (0, 0)
    const3 = lambda b, c: (0, 0, 0)
    tok = lambda b, c: (b, c, 0)
    in_specs = [
        pl.BlockSpec((nb, L, MAIN_WIDTH), tok),
        pl.BlockSpec((nb, L, N_GATES), tok),
        pl.BlockSpec((nb, N_GATES, L), lambda b, c: (b, 0, c)),
        pl.BlockSpec((L, RET_QK // 2), lambda b, c: (c, 0)),
        pl.BlockSpec((L, RET_QK // 2), lambda b, c: (c, 0)),
        pl.BlockSpec((L, RET_QK), const2),
        pl.BlockSpec((L, RET_QK), const2),
        pl.BlockSpec((RET_HEADS, L, L), const3),
        pl.BlockSpec((RET_QK, RET_V), const2),
        pl.BlockSpec((1, RET_V), const2),
        pl.BlockSpec((8, RET_QK), const2),
        pl.BlockSpec((L, L), const2),
        pl.BlockSpec((L, L), const2),
        pl.BlockSpec((2, (CONV_W - 1) * L, 2 * L), const3),
        pl.BlockSpec((1, RET_V), const2),
        pl.BlockSpec((1, ML_V), const2),
        pl.BlockSpec((CONV_W, 2 * ML_QK), const2),
        pl.BlockSpec((1, 2 * ML_QK), const2),
        pl.BlockSpec((1, N_GATES), const2),
        pl.BlockSpec((N_GATES, 1), const2),
    ]
    return pl.pallas_call(
        _mixer_kernel,
        grid=(batch // nb, nc),
        in_specs=in_specs,
        out_specs=pl.BlockSpec((nb, L, RET_V + ML_V), tok),
        out_shape=jax.ShapeDtypeStruct((batch, seq, RET_V + ML_V), BF16),
        scratch_shapes=[
            pltpu.VMEM((nb, RET_QK, RET_V), F32),
            pltpu.VMEM((nb, ML_HEADS, ML_DK, ML_DV), F32),
            pltpu.VMEM((nb, ML_HEADS, 8, ML_DK), F32),
            pltpu.VMEM((nb, ML_HEADS, 8, 128), F32),
            pltpu.VMEM((nb, 2 * L, 2 * ML_QK), BF16),
        ],
        compiler_params=pltpu.CompilerParams(
            dimension_semantics=("arbitrary", "arbitrary"), vmem_limit_bytes=VMEM_LIMIT),
        name="mixer",
    )(proj, g, gt, tabs["cos"], tabs["sin"], tabs["qdec"], tabs["kdec"], tabs["dmat"], tabs["bmask"],
      tabs["cdec"], tabs["hmask"], tabs["tril"], tabs["triu"], tabs["shift"].astype(BF16), ret_norm_w, ml_norm_w,
      conv_w, conv_b, gate_b.reshape(1, N_GATES), gate_b.reshape(N_GATES, 1)).reshape(n, RET_V + ML_V)


def _memkv_kernel(mem_ref, nw_ref, wkv_ref, k_ref, v_ref):
    d = mem_ref.shape[-1]
    mn = _rms(mem_ref[0], nw_ref[...]).astype(BF16)
    k_ref[0] = _dot(mn, wkv_ref[:, :d]).astype(BF16)
    v_ref[0] = _dot(mn, wkv_ref[:, d:]).astype(BF16)


def _memkv(mem, norm_w, wkv):
    b, m, d = mem.shape
    return pl.pallas_call(
        _memkv_kernel,
        grid=(b,),
        in_specs=[
            pl.BlockSpec((1, m, d), lambda i: (i, 0, 0)),
            pl.BlockSpec((1, d), lambda i: (0, 0)),
            pl.BlockSpec((d, 2 * d), lambda i: (0, 0)),
        ],
        out_specs=[pl.BlockSpec((1, m, d), lambda i: (i, 0, 0))] * 2,
        out_shape=[jax.ShapeDtypeStruct((b, m, d), BF16)] * 2,
        compiler_params=pltpu.CompilerParams(
            dimension_semantics=("arbitrary",), vmem_limit_bytes=VMEM_LIMIT),
        name="memkv",
    )(mem, norm_w, wkv)


def _attn_route_kernel(x_ref, mix_ref, k_ref, v_ref, wout_ref, nxa_ref, wq_ref, wo_ref, nmoe_ref,
                       wr_ref, br_ref, sut_ref,
                       x2_ref, h3_ref, ri_ref, rw_ref, cnt_ref, carry_ref):
    tm, d = x_ref.shape
    dh = d // XA_HEADS

    @pl.when((pl.program_id(0) == 0) & (pl.program_id(1) == 0))
    def _():
        carry_ref[...] = jnp.zeros_like(carry_ref)

    x1 = x_ref[...] + _dot(mix_ref[...], wout_ref[...])
    h2 = _rms(x1, nxa_ref[...]).astype(BF16)
    q = _dot(h2, wq_ref[...]).astype(BF16)
    heads = []
    for h in range(XA_HEADS):
        logits = _dot_nt(q[:, h * dh:(h + 1) * dh], k_ref[0, :, h * dh:(h + 1) * dh]) * (dh ** -0.5)
        mx = jnp.max(logits, axis=-1, keepdims=True)
        e = jnp.exp(logits - mx)
        p = (e / jnp.sum(e, axis=-1, keepdims=True)).astype(BF16)
        heads.append(_dot(p, v_ref[0, :, h * dh:(h + 1) * dh]).astype(BF16))
    x2 = x1 + _dot(jnp.concatenate(heads, axis=1), wo_ref[...])
    x2_ref[...] = x2
    h3 = _rms(x2, nmoe_ref[...])
    h3_ref[...] = _pack_rows(h3[:, :d // 2], h3[:, d // 2:])

    lt = _dot_nt(wr_ref[...], h3, precision=HIGHEST) + br_ref[...]
    gl = lt[N_EXPERTS:N_EXPERTS + N_GROUPS]
    gmax = jnp.max(gl, axis=0, keepdims=True)
    g_w = 1.0 / jnp.sum(jnp.exp(gl - gmax), axis=0, keepdims=True)
    giota = lax.broadcasted_iota(jnp.int32, gl.shape, 0)
    g_sel = jnp.min(jnp.where(gl == gmax, giota, N_GROUPS), axis=0, keepdims=True)
    el = lt[0:N_EXPERTS]
    eiota = lax.broadcasted_iota(jnp.int32, el.shape, 0)
    in_grp = (eiota // EXP_PER_GROUP) == g_sel
    elm = jnp.where(in_grp, el, -jnp.inf)
    m1 = jnp.max(elm, axis=0, keepdims=True)
    esum = jnp.sum(jnp.where(in_grp, jnp.exp(el - m1), 0.0), axis=0, keepdims=True)
    i1 = jnp.min(jnp.where(elm == m1, eiota, N_EXPERTS), axis=0, keepdims=True)
    elm2 = jnp.where(eiota == i1, -jnp.inf, elm)
    m2 = jnp.max(elm2, axis=0, keepdims=True)
    i2 = jnp.min(jnp.where(elm2 == m2, eiota, N_EXPERTS), axis=0, keepdims=True)
    p1 = 1.0 / esum
    p2 = jnp.exp(m2 - m1) / esum
    psum = p1 + p2
    w1 = g_w * (p1 / psum)
    w2 = g_w * (p2 / psum)

    oh1 = (eiota == i1).astype(F32)
    oh2 = (eiota == i2).astype(F32)
    cnt = oh1 + oh2
    base = carry_ref[:, 0:1] + _dot(cnt.astype(BF16), sut_ref[...])
    r1 = jnp.sum(oh1 * base, axis=0, keepdims=True)
    r2 = jnp.sum(oh2 * base, axis=0, keepdims=True)
    new_carry = carry_ref[...] + jnp.sum(cnt, axis=1, keepdims=True)
    carry_ref[...] = new_carry
    cnt_ref[...] = new_carry

    zi = jnp.zeros((4, tm), jnp.int32)
    ri_ref[...] = jnp.concatenate([i1, i2, r1.astype(jnp.int32), r2.astype(jnp.int32), zi], axis=0)
    rw_ref[...] = jnp.concatenate([w1, w2, jnp.zeros((6, tm), F32)], axis=0)


def _attn_route(xf, mixed, kmem, vmem, w_out, norm_xa_w, wq, wo, norm_moe_w, w_route_t, b_route, sut,
                batch, seq):
    n, d = xf.shape
    tm = TOKEN_TILE
    nt = seq // tm
    m = kmem.shape[1]
    tok = lambda b, t: (b * nt + t, 0)
    lane_tok = lambda b, t: (0, b * nt + t)
    const2 = lambda b, t: (0, 0)
    return pl.pallas_call(
        _attn_route_kernel,
        grid=(batch, nt),
        in_specs=[
            pl.BlockSpec((tm, d), tok),
            pl.BlockSpec((tm, d), tok),
            pl.BlockSpec((1, m, d), lambda b, t: (b, 0, 0)),
            pl.BlockSpec((1, m, d), lambda b, t: (b, 0, 0)),
            pl.BlockSpec((d, d), const2),
            pl.BlockSpec((1, d), const2),
            pl.BlockSpec((d, d), const2),
            pl.BlockSpec((d, d), const2),
            pl.BlockSpec((1, d), const2),
            pl.BlockSpec((ROUTE_ROWS, d), const2),
            pl.BlockSpec((ROUTE_ROWS, 1), const2),
            pl.BlockSpec((tm, tm), const2),
        ],
        out_specs=[
            pl.BlockSpec((tm, d), tok),
            pl.BlockSpec((tm, d // 2), tok),
            pl.BlockSpec((8, tm), lane_tok),
            pl.BlockSpec((8, tm), lane_tok),
            pl.BlockSpec((N_EXPERTS, 128), const2),
        ],
        out_shape=[
            jax.ShapeDtypeStruct((n, d), F32),
            jax.ShapeDtypeStruct((n, d // 2), jnp.uint32),
            jax.ShapeDtypeStruct((8, n), jnp.int32),
            jax.ShapeDtypeStruct((8, n), F32),
            jax.ShapeDtypeStruct((N_EXPERTS, 128), F32),
        ],
        scratch_shapes=[pltpu.VMEM((N_EXPERTS, 128), F32)],
        compiler_params=pltpu.CompilerParams(
            dimension_semantics=("arbitrary", "arbitrary"), vmem_limit_bytes=VMEM_LIMIT),
        name="attn_route",
    )(xf, mixed, kmem, vmem, w_out, norm_xa_w, wq, wo, norm_moe_w, w_route_t, b_route, sut)


def _dispatch_kernel(zpos_ref, dest_ref, h_ref, xs_ref, idx_ref, idx_sem, row_sem, zero_ref, zero_sem):
    i = pl.program_id(0)
    nsteps = pl.num_programs(0)
    td = h_ref.shape[0]
    bm = zero_ref.shape[0]
    slot = i % 2

    def idx_copy(step, sl):
        off = pl.multiple_of(sl * (2 * td), 2 * td)
        return pltpu.make_async_copy(dest_ref.at[step], idx_ref.at[pl.ds(off, 2 * td)], idx_sem.at[sl])

    @pl.when(i == 0)
    def _():
        zero_ref[...] = jnp.zeros_like(zero_ref)

        def zero_copy(e):
            return pltpu.make_async_copy(zero_ref, xs_ref.at[pl.ds(pl.multiple_of(zpos_ref[e], bm), bm), 0], zero_sem)

        def tail_copy(b):
            return pltpu.make_async_copy(zero_ref, xs_ref.at[pl.ds(pl.multiple_of(b * bm, bm), bm), 0], zero_sem)

        nused = zpos_ref[N_EXPERTS]
        nblk = xs_ref.shape[0] // bm
        for e in range(N_EXPERTS):
            pl.when(zpos_ref[e] >= 0)(lambda e=e: zero_copy(e).start())
        lax.fori_loop(nused, nblk, lambda b, c: (tail_copy(b).start(), c)[1], 0)
        for e in range(N_EXPERTS):
            pl.when(zpos_ref[e] >= 0)(lambda e=e: zero_copy(e).wait())
        lax.fori_loop(nused, nblk, lambda b, c: (tail_copy(b).wait(), c)[1], 0)
        idx_copy(0, 0).start()

    idx_copy(i, slot).wait()

    @pl.when(i + 1 < nsteps)
    def _():
        idx_copy(i + 1, 1 - slot).start()

    base = slot * (2 * td)

    for t in range(td):
        for k in range(TOP_K):
            pltpu.make_async_copy(h_ref.at[pl.ds(t, 1)], xs_ref.at[idx_ref[base + k * td + t]], row_sem).start()
    for _ in range(TOP_K):
        pltpu.make_async_copy(xs_ref.at[pl.ds(0, td)], xs_ref.at[pl.ds(0, td)], row_sem).wait()


def _dispatch(h3p, dest_tiles, zpos, cap):
    n, w = h3p.shape
    nt, td2 = dest_tiles.shape
    td = td2 // 2
    grid_spec = pltpu.PrefetchScalarGridSpec(
        num_scalar_prefetch=1,
        grid=(nt,),
        in_specs=[
            pl.BlockSpec(memory_space=pl.ANY),
            pl.BlockSpec((td, w), lambda i, zp: (i, 0)),
        ],
        out_specs=pl.BlockSpec(memory_space=pl.ANY),
        scratch_shapes=[
            pltpu.SMEM((2 * td2,), jnp.int32),
            pltpu.SemaphoreType.DMA((2,)),
            pltpu.SemaphoreType.DMA,
            pltpu.VMEM((MOE_ROWS, w), jnp.uint32),
            pltpu.SemaphoreType.DMA,
        ],
    )
    return pl.pallas_call(
        _dispatch_kernel,
        grid_spec=grid_spec,
        out_shape=jax.ShapeDtypeStruct((cap, 1, w), jnp.uint32),
        compiler_params=pltpu.CompilerParams(
            dimension_semantics=("arbitrary",), vmem_limit_bytes=VMEM_LIMIT),
        name="dispatch",
    )(zpos, dest_tiles, h3p)


def _expert_kernel(blk_e_ref, nused_ref, xs_ref, wg_ref, wu_ref, wd_ref, ys_ref, wg_b, wu_b, wd_b,
                   xbuf, ybuf, zbuf, in_sem, out_sem, zero_sem):
    i = pl.program_id(0)
    nsteps = pl.num_programs(0)
    nused = nused_ref[0]
    bm = xbuf.shape[1]
    slot = i % 2
    prev = blk_e_ref[jnp.maximum(i - 1, 0)]
    fresh = (i == 0) | (blk_e_ref[i] != prev)
    half = wd_b.shape[1] // 2

    def rows(ref, step):
        return ref.at[pl.ds(pl.multiple_of(step * bm, bm), bm), 0]

    def in_copy(step, sl):
        return pltpu.make_async_copy(rows(xs_ref, step), xbuf.at[sl], in_sem.at[sl])

    def out_copy(step, sl):
        return pltpu.make_async_copy(ybuf.at[sl], rows(ys_ref, step), out_sem.at[sl])

    def zero_copy(step):
        return pltpu.make_async_copy(zbuf, rows(ys_ref, step), zero_sem)

    @pl.when(i == 0)
    def _():
        zbuf[...] = jnp.zeros_like(zbuf)
        in_copy(0, 0).start()

    @pl.when(i + 1 < nused)
    def _():
        in_copy(i + 1, 1 - slot).start()

    @pl.when(fresh)
    def _():
        wg_b[...] = wg_ref[0].astype(BF16)
        wu_b[...] = wu_ref[0].astype(BF16)
        wd_b[...] = wd_ref[0].astype(BF16)

    @pl.when(i < nused)
    def _():
        in_copy(i, slot).wait()
        pl.when(i >= 2)(lambda: out_copy(i - 2, slot).wait())
        lo, hi = _unpack_rows(xbuf[slot])
        xb = jnp.concatenate([lo.astype(BF16), hi.astype(BF16)], axis=1)
        hid = (_silu(_dot(xb, wg_b[...])) * _dot(xb, wu_b[...])).astype(BF16)
        y = _dot(hid, wd_b[...])
        ybuf[slot] = _pack_rows(y[:, :half], y[:, half:])
        out_copy(i, slot).start()

    pl.when(i >= nused)(lambda: zero_copy(i).start())

    @pl.when(i == nsteps - 1)
    def _():
        pl.when(nused >= 2)(lambda: out_copy(nused - 2, nused % 2).wait())
        pl.when(nused >= 1)(lambda: out_copy(nused - 1, (nused - 1) % 2).wait())
        lax.fori_loop(nused, nsteps, lambda b, c: (zero_copy(b).wait(), c)[1], 0)


def _experts(xs, blk_e, nused, w_gate, w_up, w_down):
    cap, _, w = xs.shape
    _, d, de = w_gate.shape
    bm = MOE_ROWS
    grid_spec = pltpu.PrefetchScalarGridSpec(
        num_scalar_prefetch=2,
        grid=(cap // bm,),
        in_specs=[
            pl.BlockSpec(memory_space=pl.ANY),
            pl.BlockSpec((1, d, de), lambda i, be, nu: (be[i], 0, 0)),
            pl.BlockSpec((1, d, de), lambda i, be, nu: (be[i], 0, 0)),
            pl.BlockSpec((1, de, d), lambda i, be, nu: (be[i], 0, 0)),
        ],
        out_specs=pl.BlockSpec(memory_space=pl.ANY),
        scratch_shapes=[
            pltpu.VMEM((d, de), BF16),
            pltpu.VMEM((d, de), BF16),
            pltpu.VMEM((de, d), BF16),
            pltpu.VMEM((2, bm, w), jnp.uint32),
            pltpu.VMEM((2, bm, w), jnp.uint32),
            pltpu.VMEM((bm, w), jnp.uint32),
            pltpu.SemaphoreType.DMA((2,)),
            pltpu.SemaphoreType.DMA((2,)),
            pltpu.SemaphoreType.DMA,
        ],
    )
    return pl.pallas_call(
        _expert_kernel,
        grid_spec=grid_spec,
        out_shape=jax.ShapeDtypeStruct((cap, 1, w), jnp.uint32),
        compiler_params=pltpu.CompilerParams(
            dimension_semantics=("arbitrary",), vmem_limit_bytes=VMEM_LIMIT),
        name="experts",
    )(blk_e, nused, xs, w_gate, w_up, w_down)


def _combine_kernel(dest_ref, ys_ref, x2_ref, rw_ref, eye_ref, nw_ref, o_ref, idx_ref, idx_sem, ybuf, ysem):
    i = pl.program_id(0)
    nsteps = pl.num_programs(0)
    tc, d = x2_ref.shape
    half = d // 2
    n_idx = 2 * tc

    def idx_copy(step):
        sl = step % 3
        off = pl.multiple_of(sl * n_idx, n_idx)
        return pltpu.make_async_copy(dest_ref.at[step], idx_ref.at[pl.ds(off, n_idx)], idx_sem.at[sl])

    def gather(step):
        base = (step % 3) * n_idx
        buf = ybuf.at[step % 2]
        sem = ysem.at[step % 2]

        for t in range(n_idx):
            pltpu.make_async_copy(ys_ref.at[idx_ref[base + t]], buf.at[pl.ds(t, 1)], sem).start()

    @pl.when(i == 0)
    def _():
        idx_copy(0).start()
        idx_copy(0).wait()
        gather(0)

        @pl.when(nsteps > 1)
        def _():
            idx_copy(1).start()

    @pl.when(i + 1 < nsteps)
    def _():
        idx_copy(i + 1).wait()

        @pl.when(i + 2 < nsteps)
        def _():
            idx_copy(i + 2).start()

        gather(i + 1)

    slot = i % 2
    pltpu.make_async_copy(ybuf.at[slot], ybuf.at[slot], ysem.at[slot]).wait()
    wcol = _dot_nt(eye_ref[...], rw_ref[...], precision=HIGHEST)
    lo1, hi1 = _unpack_rows(ybuf[slot, 0:tc])
    lo2, hi2 = _unpack_rows(ybuf[slot, tc:n_idx])
    w1 = wcol[:, 0:1]
    w2 = wcol[:, 1:2]
    z_lo = x2_ref[:, :half] + (lo1 * w1 + lo2 * w2)
    z_hi = x2_ref[:, half:] + (hi1 * w1 + hi2 * w2)
    ms = (jnp.sum(z_lo * z_lo, axis=-1, keepdims=True) + jnp.sum(z_hi * z_hi, axis=-1, keepdims=True)) / d
    scale = lax.rsqrt(ms + EPS)
    o_ref[:, :half] = z_lo * scale * nw_ref[:, :half]
    o_ref[:, half:] = z_hi * scale * nw_ref[:, half:]


def _combine(x2, ys, dest_tiles, rw, eye, norm_w):
    n, d = x2.shape
    nt, n_idx = dest_tiles.shape
    tc = n_idx // 2
    w = ys.shape[-1]
    return pl.pallas_call(
        _combine_kernel,
        grid=(nt,),
        in_specs=[
            pl.BlockSpec(memory_space=pl.ANY),
            pl.BlockSpec(memory_space=pl.ANY),
            pl.BlockSpec((tc, d), lambda i: (i, 0)),
            pl.BlockSpec((8, tc), lambda i: (0, i)),
            pl.BlockSpec((tc, tc), lambda i: (0, 0)),
            pl.BlockSpec((1, d), lambda i: (0, 0)),
        ],
        out_specs=pl.BlockSpec((tc, d), lambda i: (i, 0)),
        out_shape=jax.ShapeDtypeStruct((n, d), F32),
        scratch_shapes=[
            pltpu.SMEM((3 * n_idx,), jnp.int32),
            pltpu.SemaphoreType.DMA((3,)),
            pltpu.VMEM((2, n_idx, w), jnp.uint32),
            pltpu.SemaphoreType.DMA((2,)),
        ],
        compiler_params=pltpu.CompilerParams(
            dimension_semantics=("arbitrary",), vmem_limit_bytes=VMEM_LIMIT),
        name="combine",
    )(dest_tiles, ys, x2, rw, eye, norm_w)


def _rope_column_order():
    half = RET_DK // 2
    first = [h * RET_DK + j for h in range(RET_HEADS) for j in range(half)]
    second = [h * RET_DK + half + j for h in range(RET_HEADS) for j in range(half)]
    return np.array(first + second, dtype=np.int32)


def _layer(xf, mem, batch, seq, norm_mix_w, w_in, ret_norm_w, ml_conv_w, ml_conv_b, ml_gate_b, ml_norm_w,
           w_out, norm_xa_w, norm_mem_w, xa_wq, xa_wkv, xa_wo, norm_moe_w, moe_w_group, moe_b_group,
           moe_w_router, moe_b_router, moe_w_gate, moe_w_up, moe_w_down, final_norm_w):
    n, d = xf.shape
    perm = _rope_column_order()
    cols = np.concatenate([perm, RET_QK + perm, np.arange(2 * RET_QK, MAIN_WIDTH)])
    w_main = w_in[:, cols].astype(BF16)
    w_if = w_in[:, MAIN_WIDTH:].astype(BF16)
    proj, g, gt = _inproj(xf, norm_mix_w.reshape(1, d), w_main, w_if, w_if.T)

    tabs = {k_: jnp.asarray(v_) for k_, v_ in _mixer_tables(seq).items()}
    mixed = _mixer(proj, g, gt, tabs, ret_norm_w.reshape(1, RET_V), ml_norm_w.reshape(1, ML_V), ml_conv_w,
                   ml_conv_b.reshape(1, 2 * ML_QK), ml_gate_b, batch, seq)

    kmem, vmem = _memkv(mem, norm_mem_w.reshape(1, d), xa_wkv.astype(BF16))

    w_route_t = jnp.concatenate(
        [moe_w_router.T, moe_w_group.T, jnp.zeros((ROUTE_ROWS - N_EXPERTS - N_GROUPS, d), F32)], axis=0)
    b_route = jnp.concatenate(
        [moe_b_router, moe_b_group, jnp.zeros((ROUTE_ROWS - N_EXPERTS - N_GROUPS,), F32)]).reshape(ROUTE_ROWS, 1)
    tm = TOKEN_TILE
    sut = jnp.asarray(np.triu(np.ones((tm, tm), np.float32), 1), dtype=BF16)
    x2, h3, ri, rw, cnt = _attn_route(xf, mixed, kmem, vmem, w_out.astype(BF16), norm_xa_w.reshape(1, d),
                                      xa_wq.astype(BF16), xa_wo.astype(BF16), norm_moe_w.reshape(1, d),
                                      w_route_t, b_route, sut, batch, seq)

    bm = MOE_ROWS
    counts = cnt[:, 0].astype(jnp.int32)
    padded = (counts + bm - 1) // bm * bm
    pends = jnp.cumsum(padded)
    pstarts = pends - padded
    expert = ri[0:TOP_K]
    onehot = expert[None] == jnp.arange(N_EXPERTS, dtype=jnp.int32)[:, None, None]
    dest = jnp.sum(jnp.where(onehot, pstarts[:, None, None], 0), axis=0) + ri[TOP_K:2 * TOP_K]
    cap = n * TOP_K + N_EXPERTS * bm
    nblk = cap // bm
    blk_start = jnp.arange(nblk, dtype=jnp.int32) * bm
    blk_e = jnp.minimum(jnp.sum(blk_start[:, None] >= pends[None, :], axis=1), N_EXPERTS - 1).astype(jnp.int32)
    nused = (pends[-1] // bm).astype(jnp.int32).reshape(1)
    zpos = jnp.where(padded > counts, pends - bm, -1).astype(jnp.int32)
    zpos = jnp.concatenate([zpos, nused])

    def tiles(rows):
        return dest.reshape(TOP_K, n // rows, rows).transpose(1, 0, 2).reshape(n // rows, TOP_K * rows)

    xs = _dispatch(h3, tiles(DISPATCH_TILE), zpos, cap)
    ys = _experts(xs, blk_e, nused, moe_w_gate, moe_w_up, moe_w_down)
    eye = jnp.asarray(np.eye(tm, dtype=np.float32))
    return _combine(x2, ys, tiles(tm), rw, eye, final_norm_w.reshape(1, d))


def kernel(x, mem, norm_mix_w, w_in, ret_norm_w, ml_conv_w, ml_conv_b, ml_gate_b, ml_norm_w, w_out, norm_xa_w, norm_mem_w, xa_wq, xa_wkv, xa_wo, norm_moe_w, moe_w_group, moe_b_group, moe_w_router, moe_b_router, moe_w_gate, moe_w_up, moe_w_down, norm_final_w):
    batch, seq, d = x.shape
    depth = w_in.shape[0]
    assert depth == 1, "the final norm is fused into the last layer's combine kernel"
    l = 0
    out = _layer(x.reshape(batch * seq, d), mem, batch, seq, norm_mix_w[l], w_in[l], ret_norm_w[l], ml_conv_w[l],
                 ml_conv_b[l], ml_gate_b[l], ml_norm_w[l], w_out[l], norm_xa_w[l], norm_mem_w[l], xa_wq[l],
                 xa_wkv[l], xa_wo[l], norm_moe_w[l], moe_w_group[l], moe_b_group[l], moe_w_router[l],
                 moe_b_router[l], moe_w_gate[l], moe_w_up[l], moe_w_down[l], norm_final_w)
    return out.reshape(batch, seq, d)
```

```python
import functools

import numpy as np
import jax
import jax.numpy as jnp
from jax import lax
from jax.experimental import pallas as pl
from jax.experimental.pallas import tpu as pltpu

F32 = jnp.float32
BF16 = jnp.bfloat16
HIGHEST = lax.Precision.HIGHEST

CHUNK = 128
RET_HEADS = 4
RET_DK = 64
RET_DV = 128
ML_HEADS = 4
ML_DK = 128
ML_DV = 128
CONV_W = 4
XA_HEADS = 4
N_GROUPS = 4
EXP_PER_GROUP = 8
N_EXPERTS = N_GROUPS * EXP_PER_GROUP
TOP_K = 2
ROPE_BASE = 10000.0
EPS = 1e-6

RET_QK = RET_HEADS * RET_DK
RET_V = RET_HEADS * RET_DV
ML_QK = ML_HEADS * ML_DK
ML_V = ML_HEADS * ML_DV
OFF_RQ = 0
OFF_RK = OFF_RQ + RET_QK
OFF_RV = OFF_RK + RET_QK
OFF_RG = OFF_RV + RET_V
OFF_MQK = OFF_RG + RET_V
OFF_MV = OFF_MQK + 2 * ML_QK
OFF_MO = OFF_MV + ML_V
MAIN_WIDTH = OFF_MO + ML_V
N_GATES = 2 * ML_HEADS

ROUTE_ROWS = 40
TOKEN_TILE = 512
MOE_ROWS = 512
DISPATCH_TILE = 1024
MIXER_BATCHES = 2
ATTN_GROUPS = 2
VMEM_LIMIT = 56 * 1024 * 1024


def _dot(a, b):
    return jnp.dot(a, b, preferred_element_type=F32)


def _dot_nt(a, b, precision=None):
    return lax.dot_general(a, b, (((1,), (1,)), ((), ())), preferred_element_type=F32, precision=precision)


def _dot_tn(a, b):
    return lax.dot_general(a, b, (((0,), (0,)), ((), ())), preferred_element_type=F32)


def _rms(x, w):
    return x * lax.rsqrt(jnp.mean(x * x, axis=-1, keepdims=True) + EPS) * w


def _sigmoid(x):
    return 1.0 / (1.0 + jnp.exp(-x))


def _silu(x):
    return x * _sigmoid(x)


def _log_sigmoid(x):
    return jnp.minimum(x, 0.0) - jnp.log1p(jnp.exp(-jnp.abs(x)))


def _head_norm(h):
    mu = jnp.mean(h, axis=-1, keepdims=True)
    d = h - mu
    var = jnp.mean(d * d, axis=-1, keepdims=True)
    return d * lax.rsqrt(var + EPS)


def _pack_rows(lo, hi):
    def bits(t):
        return lax.bitcast_convert_type(t.astype(BF16), jnp.uint16).astype(jnp.uint32)
    return bits(lo) | (bits(hi) << 16)


def _unpack_rows(u):
    lo = lax.bitcast_convert_type(u << 16, F32)
    hi = lax.bitcast_convert_type(u & jnp.uint32(0xFFFF0000), F32)
    return lo, hi


def _inproj_kernel(x_ref, nw_ref, w_ref, wif_ref, wift_ref, proj_ref, g_ref, gt_ref):
    h = _rms(x_ref[...], nw_ref[...]).astype(BF16)
    for j in range(0, MAIN_WIDTH, 512):
        proj_ref[:, j:j + 512] = _dot(h, w_ref[:, j:j + 512]).astype(BF16)
    g_ref[...] = _dot(h, wif_ref[...])
    gt_ref[...] = _dot_nt(wift_ref[...], h)


def _inproj(xf, norm_w, w_main, w_if, w_ift):
    n, d = xf.shape
    tm = TOKEN_TILE
    return pl.pallas_call(
        _inproj_kernel,
        grid=(n // tm,),
        in_specs=[
            pl.BlockSpec((tm, d), lambda i: (i, 0)),
            pl.BlockSpec((1, d), lambda i: (0, 0)),
            pl.BlockSpec((d, MAIN_WIDTH), lambda i: (0, 0)),
            pl.BlockSpec((d, N_GATES), lambda i: (0, 0)),
            pl.BlockSpec((N_GATES, d), lambda i: (0, 0)),
        ],
        out_specs=[
            pl.BlockSpec((tm, MAIN_WIDTH), lambda i: (i, 0)),
            pl.BlockSpec((tm, N_GATES), lambda i: (i, 0)),
            pl.BlockSpec((N_GATES, tm), lambda i: (0, i)),
        ],
        out_shape=[
            jax.ShapeDtypeStruct((n, MAIN_WIDTH), BF16),
            jax.ShapeDtypeStruct((n, N_GATES), F32),
            jax.ShapeDtypeStruct((N_GATES, n), F32),
        ],
        compiler_params=pltpu.CompilerParams(
            dimension_semantics=("arbitrary",), vmem_limit_bytes=VMEM_LIMIT),
        name="inproj",
    )(xf, norm_w, w_main, w_if, w_ift)


def _mixer_kernel(*refs):
    @pl.when(pl.program_id(1) == 0)
    def _():
        for state_ref in refs[-5:]:
            state_ref[...] = jnp.zeros_like(state_ref)

    for bi in range(refs[0].shape[0]):
        _mixer_one(bi, *refs)


def _mixer_one(bi, proj_ref, g_ref, gt_ref, cos_ref, sin_ref, qdec_ref, kdec_ref, dmat_ref,
               bmask_ref, cdec_ref, hmask_ref, tril_ref, triu_ref, shift_ref, retw_ref, mlw_ref,
               convw_ref, convb_ref, gbc_ref, gbr_ref,
               out_ref, r_ref, c_ref, n_ref, m_ref, xbuf_ref):
    L = CHUNK
    proj_ref, g_ref, gt_ref, out_ref = proj_ref.at[bi], g_ref.at[bi], gt_ref.at[bi], out_ref.at[bi]
    r_ref, c_ref, n_ref, m_ref, xbuf_ref = r_ref.at[bi], c_ref.at[bi], n_ref.at[bi], m_ref.at[bi], xbuf_ref.at[bi]
    chunk = pl.program_id(1)

    cos = cos_ref[...]
    sin = sin_ref[...]
    half = RET_QK // 2

    def rope(t):
        t1, t2 = t[:, :half], t[:, half:]
        return jnp.concatenate([t1 * cos - t2 * sin, t1 * sin + t2 * cos], axis=1)

    q = rope(proj_ref[:, OFF_RQ:OFF_RQ + RET_QK].astype(F32))
    k = rope(proj_ref[:, OFF_RK:OFF_RK + RET_QK].astype(F32)) * (RET_DK ** -0.5)
    v = proj_ref[:, OFF_RV:OFF_RV + RET_V]
    k_b = k.astype(BF16)
    r_prev = r_ref[...]
    cross = _dot((q * qdec_ref[...]).astype(BF16), r_prev.astype(BF16))
    kv = _dot_tn((k * kdec_ref[...]).astype(BF16), v) * bmask_ref[...]
    r_ref[...] = cdec_ref[...] * r_prev + kv
    ret_heads = []
    for h in range(RET_HEADS):
        qh = (q * hmask_ref[h:h + 1, :]).astype(BF16)
        sc = _dot_nt(qh, k_b) * dmat_ref[h]
        inner = _dot(sc.astype(BF16), v[:, h * RET_DV:(h + 1) * RET_DV])
        ret_heads.append(_head_norm(inner + cross[:, h * RET_DV:(h + 1) * RET_DV]))
    ret = jnp.concatenate(ret_heads, axis=1) * retw_ref[...]
    ret = ret * _silu(proj_ref[:, OFF_RG:OFF_RG + RET_V].astype(F32))
    out_ref[:, 0:RET_V] = ret.astype(BF16)

    parity = chunk % 2
    x_b = proj_ref[:, OFF_MQK:OFF_MQK + 2 * ML_QK]
    xbuf_ref[pl.ds(pl.multiple_of(parity * L, L), L), :] = x_b
    shifted = _dot(shift_ref[parity], xbuf_ref[...])
    acc = x_b.astype(F32) * convw_ref[CONV_W - 1:CONV_W, :] + convb_ref[...]
    for s in range(1, CONV_W):
        acc = acc + shifted[(s - 1) * L:s * L] * convw_ref[CONV_W - 1 - s:CONV_W - s, :]
    qk = _silu(acc)
    mq = qk[:, :ML_QK]
    mk = qk[:, ML_QK:] * (ML_DK ** -0.5)
    mv = proj_ref[:, OFF_MV:OFF_MV + ML_V]

    gc = g_ref[...] + gbc_ref[...]
    gr = gt_ref[...] + gbr_ref[...]
    b_c = jnp.dot(tril_ref[...], _log_sigmoid(gc), preferred_element_type=F32, precision=HIGHEST)
    b_r = jnp.dot(_log_sigmoid(gr), triu_ref[...], preferred_element_type=F32, precision=HIGHEST)
    causal = (lax.broadcasted_iota(jnp.int32, (L, L), 0) >= lax.broadcasted_iota(jnp.int32, (L, L), 1))
    ml_heads = []
    for h in range(ML_HEADS):
        bc = b_c[:, ML_HEADS + h:ML_HEADS + h + 1]
        br = b_r[ML_HEADS + h:ML_HEADS + h + 1, :]
        igc = gc[:, h:h + 1]
        igr = gr[h:h + 1, :]
        btot = br[:, L - 1:L]
        log_d = jnp.where(causal, bc - br + igr, -jnp.inf)
        m_intra = jnp.max(log_d, axis=1, keepdims=True)
        m_loc = jnp.max(btot - br + igr, axis=1, keepdims=True)
        wa = jnp.exp(btot - bc + igc - m_loc)
        qh = mq[:, h * ML_DK:(h + 1) * ML_DK]
        kh = mk[:, h * ML_DK:(h + 1) * ML_DK]
        vh = mv[:, h * ML_DV:(h + 1) * ML_DV]
        kw = kh * wa
        kv_loc = _dot_tn(kw.astype(BF16), vh)
        n_loc = jnp.sum(kw, axis=0, keepdims=True)
        c_prev = c_ref[h]
        n_prev = n_ref[h][0:1, :]
        m_prev = m_ref[h][0:1, 0:1]
        m_inter = bc + m_prev
        m_t = jnp.maximum(m_intra, m_inter)
        qh_b = qh.astype(BF16)
        s_mat = _dot_nt(qh_b, kh.astype(BF16)) * jnp.exp(log_d - m_t)
        inter = jnp.exp(m_inter - m_t)
        num = _dot(s_mat.astype(BF16), vh) + inter * _dot(qh_b, c_prev.astype(BF16))
        den = jnp.sum(s_mat, axis=1, keepdims=True) + inter * jnp.sum(qh * n_prev, axis=1, keepdims=True)
        hh = num / jnp.maximum(jnp.abs(den), jnp.exp(-m_t))
        ml_heads.append(_head_norm(hh))
        m_new = jnp.maximum(btot + m_prev, m_loc)
        s_old = jnp.exp(btot + m_prev - m_new)
        s_loc = jnp.exp(m_loc - m_new)
        c_ref[h] = s_old * c_prev + s_loc * kv_loc
        n_ref[h] = jnp.broadcast_to(s_old * n_prev + s_loc * n_loc, (8, ML_DK))
        m_ref[h] = jnp.broadcast_to(m_new, (8, 128))
    ml = jnp.concatenate(ml_heads, axis=1) * mlw_ref[...]
    ml = ml * _sigmoid(proj_ref[:, OFF_MO:OFF_MO + ML_V].astype(F32))
    out_ref[:, RET_V:RET_V + ML_V] = ml.astype(BF16)


def _mixer_tables(seq):
    L = CHUNK
    half = RET_DK // 2
    inv = ROPE_BASE ** (-np.arange(half, dtype=np.float64) / half)
    ang = np.arange(seq, dtype=np.float64)[:, None] * inv[None, :].astype(np.float32).astype(np.float64)
    cos = np.tile(np.cos(ang), (1, RET_HEADS)).astype(np.float32)
    sin = np.tile(np.sin(ang), (1, RET_HEADS)).astype(np.float32)
    log_g = np.log1p(-np.exp2(-5.0 - np.arange(RET_HEADS, dtype=np.float64)))
    n = np.arange(L, dtype=np.float64)
    lane_head = (np.arange(RET_QK) % (RET_QK // 2)) // half
    qdec = np.exp((n + 1)[:, None] * log_g[lane_head][None, :]).astype(np.float32)
    kdec = np.exp((L - 1 - n)[:, None] * log_g[lane_head][None, :]).astype(np.float32)
    diff = n[:, None] - n[None, :]
    dmat = np.where(diff >= 0, np.exp(log_g[:, None, None] * np.maximum(diff, 0.0)[None]), 0.0).astype(np.float32)
    col_head = np.arange(RET_V) // RET_DV
    bmask = (lane_head[:, None] == col_head[None, :]).astype(np.float32)
    cdec = np.exp(L * log_g[col_head])[None, :].astype(np.float32)
    hmask = (lane_head[None, :] == np.arange(RET_HEADS)[:, None]).astype(np.float32)
    hmask = np.concatenate([hmask, np.zeros((8 - RET_HEADS, RET_QK), np.float32)], axis=0)
    tril = np.tril(np.ones((L, L), np.float32))
    shift = np.zeros((2, (CONV_W - 1) * L, 2 * L), np.float32)
    for p in range(2):
        for s in range(1, CONV_W):
            for t in range(L):
                col = p * L + t - s if t >= s else (1 - p) * L + L + t - s
                shift[p, (s - 1) * L + t, col] = 1.0
    return dict(cos=cos, sin=sin, qdec=qdec, kdec=kdec, dmat=dmat, bmask=bmask, cdec=cdec, hmask=hmask,
                tril=tril, triu=np.ascontiguousarray(tril.T), shift=shift)


def _mixer(proj, g, gt, tabs, ret_norm_w, ml_norm_w, conv_w, conv_b, gate_b, batch, seq):
    L = CHUNK
    nc = seq // L
    n = batch * seq
    nb = MIXER_BATCHES if batch % MIXER_BATCHES == 0 else 1
    proj = proj.reshape(batch, seq, MAIN_WIDTH)
    g = g.reshape(batch, seq, N_GATES)
    gt = gt.reshape(N_GATES, batch, seq).transpose(1, 0, 2)
    const2 = lambda b, c: (0, 0)
    const3 = lambda b, c: (0, 0, 0)
    tok = lambda b, c: (b, c, 0)
    in_specs = [
        pl.BlockSpec((nb, L, MAIN_WIDTH), tok),
        pl.BlockSpec((nb, L, N_GATES), tok),
        pl.BlockSpec((nb, N_GATES, L), lambda b, c: (b, 0, c)),
        pl.BlockSpec((L, RET_QK // 2), lambda b, c: (c, 0)),
        pl.BlockSpec((L, RET_QK // 2), lambda b, c: (c, 0)),
        pl.BlockSpec((L, RET_QK), const2),
        pl.BlockSpec((L, RET_QK), const2),
        pl.BlockSpec((RET_HEADS, L, L), const3),
        pl.BlockSpec((RET_QK, RET_V), const2),
        pl.BlockSpec((1, RET_V), const2),
        pl.BlockSpec((8, RET_QK), const2),
        pl.BlockSpec((L, L), const2),
        pl.BlockSpec((L, L), const2),
        pl.BlockSpec((2, (CONV_W - 1) * L, 2 * L), const3),
        pl.BlockSpec((1, RET_V), const2),
        pl.BlockSpec((1, ML_V), const2),
        pl.BlockSpec((CONV_W, 2 * ML_QK), const2),
        pl.BlockSpec((1, 2 * ML_QK), const2),
        pl.BlockSpec((1, N_GATES), const2),
        pl.BlockSpec((N_GATES, 1), const2),
    ]
    return pl.pallas_call(
        _mixer_kernel,
        grid=(batch // nb, nc),
        in_specs=in_specs,
        out_specs=pl.BlockSpec((nb, L, RET_V + ML_V), tok),
        out_shape=jax.ShapeDtypeStruct((batch, seq, RET_V + ML_V), BF16),
        scratch_shapes=[
            pltpu.VMEM((nb, RET_QK, RET_V), F32),
            pltpu.VMEM((nb, ML_HEADS, ML_DK, ML_DV), F32),
            pltpu.VMEM((nb, ML_HEADS, 8, ML_DK), F32),
            pltpu.VMEM((nb, ML_HEADS, 8, 128), F32),
            pltpu.VMEM((nb, 2 * L, 2 * ML_QK), BF16),
        ],
        compiler_params=pltpu.CompilerParams(
            dimension_semantics=("arbitrary", "arbitrary"), vmem_limit_bytes=VMEM_LIMIT),
        name="mixer",
    )(proj, g, gt, tabs["cos"], tabs["sin"], tabs["qdec"], tabs["kdec"], tabs["dmat"], tabs["bmask"],
      tabs["cdec"], tabs["hmask"], tabs["tril"], tabs["triu"], tabs["shift"].astype(BF16), ret_norm_w, ml_norm_w,
      conv_w, conv_b, gate_b.reshape(1, N_GATES), gate_b.reshape(N_GATES, 1)).reshape(n, RET_V + ML_V)


def _memkv_kernel(mem_ref, nw_ref, wkv_ref, k_ref, v_ref):
    d = mem_ref.shape[-1]
    mn = _rms(mem_ref[0], nw_ref[...]).astype(BF16)
    k_ref[0] = _dot(mn, wkv_ref[:, :d]).astype(BF16)
    v_ref[0] = _dot(mn, wkv_ref[:, d:]).astype(BF16)


def _memkv(mem, norm_w, wkv):
    b, m, d = mem.shape
    return pl.pallas_call(
        _memkv_kernel,
        grid=(b,),
        in_specs=[
            pl.BlockSpec((1, m, d), lambda i: (i, 0, 0)),
            pl.BlockSpec((1, d), lambda i: (0, 0)),
            pl.BlockSpec((d, 2 * d), lambda i: (0, 0)),
        ],
        out_specs=[pl.BlockSpec((1, m, d), lambda i: (i, 0, 0))] * 2,
        out_shape=[jax.ShapeDtypeStruct((b, m, d), BF16)] * 2,
        compiler_params=pltpu.CompilerParams(
            dimension_semantics=("arbitrary",), vmem_limit_bytes=VMEM_LIMIT),
        name="memkv",
    )(mem, norm_w, wkv)


def _attn_route_kernel(x_ref, mix_ref, k_ref, v_ref, wout_ref, nxa_ref, wq_ref, wo_ref, nmoe_ref,
                       wr_ref, wrlo_ref, br_ref, sut_ref,
                       x2_ref, h3_ref, ri_ref, rw_ref, cnt_ref, carry_ref):
    tm, d = x_ref.shape
    dh = d // XA_HEADS

    @pl.when((pl.program_id(0) == 0) & (pl.program_id(1) == 0))
    def _():
        carry_ref[...] = jnp.zeros_like(carry_ref)

    groups = [slice(g * (tm // ATTN_GROUPS), (g + 1) * (tm // ATTN_GROUPS)) for g in range(ATTN_GROUPS)]
    x1 = [x_ref[s, :] + _dot(mix_ref[s, :], wout_ref[...]) for s in groups]
    h2 = [_rms(t, nxa_ref[...]).astype(BF16) for t in x1]
    q = [_dot(t, wq_ref[...]).astype(BF16) for t in h2]
    o = []
    for qg in q:
        heads = []
        for h in range(XA_HEADS):
            logits = _dot_nt(qg[:, h * dh:(h + 1) * dh], k_ref[0, :, h * dh:(h + 1) * dh]) * (dh ** -0.5)
            mx = jnp.max(logits, axis=-1, keepdims=True)
            e = jnp.exp(logits - mx)
            p = (e / jnp.sum(e, axis=-1, keepdims=True)).astype(BF16)
            heads.append(_dot(p, v_ref[0, :, h * dh:(h + 1) * dh]).astype(BF16))
        o.append(jnp.concatenate(heads, axis=1))
    x2 = [a + _dot(b, wo_ref[...]) for a, b in zip(x1, o)]
    for s, t in zip(groups, x2):
        x2_ref[s, :] = t
    h3 = [_rms(t, nmoe_ref[...]) for t in x2]
    for s, t in zip(groups, h3):
        h3_ref[s, :] = _pack_rows(t[:, :d // 2], t[:, d // 2:])

    lts = []
    for t in h3:
        t_hi = t.astype(BF16)
        t_lo = (t - t_hi.astype(F32)).astype(BF16)
        lts.append(_dot_nt(wr_ref[...], t_hi) + (_dot_nt(wr_ref[...], t_lo) + _dot_nt(wrlo_ref[...], t_hi)))
    lt = jnp.concatenate(lts, axis=1) + br_ref[...]
    gl = lt[N_EXPERTS:N_EXPERTS + N_GROUPS]
    gmax = jnp.max(gl, axis=0, keepdims=True)
    g_w = 1.0 / jnp.sum(jnp.exp(gl - gmax), axis=0, keepdims=True)
    giota = lax.broadcasted_iota(jnp.int32, gl.shape, 0)
    g_sel = jnp.min(jnp.where(gl == gmax, giota, N_GROUPS), axis=0, keepdims=True)
    el = lt[0:N_EXPERTS]
    eiota = lax.broadcasted_iota(jnp.int32, el.shape, 0)
    in_grp = (eiota // EXP_PER_GROUP) == g_sel
    elm = jnp.where(in_grp, el, -jnp.inf)
    m1 = jnp.max(elm, axis=0, keepdims=True)
    esum = jnp.sum(jnp.where(in_grp, jnp.exp(el - m1), 0.0), axis=0, keepdims=True)
    i1 = jnp.min(jnp.where(elm == m1, eiota, N_EXPERTS), axis=0, keepdims=True)
    elm2 = jnp.where(eiota == i1, -jnp.inf, elm)
    m2 = jnp.max(elm2, axis=0, keepdims=True)
    i2 = jnp.min(jnp.where(elm2 == m2, eiota, N_EXPERTS), axis=0, keepdims=True)
    p1 = 1.0 / esum
    p2 = jnp.exp(m2 - m1) / esum
    psum = p1 + p2
    w1 = g_w * (p1 / psum)
    w2 = g_w * (p2 / psum)

    oh1 = (eiota == i1).astype(F32)
    oh2 = (eiota == i2).astype(F32)
    cnt = oh1 + oh2
    base = carry_ref[:, 0:1] + _dot(cnt.astype(BF16), sut_ref[...])
    r1 = jnp.sum(oh1 * base, axis=0, keepdims=True)
    r2 = jnp.sum(oh2 * base, axis=0, keepdims=True)
    new_carry = carry_ref[...] + jnp.sum(cnt, axis=1, keepdims=True)
    carry_ref[...] = new_carry
    cnt_ref[...] = new_carry

    zi = jnp.zeros((4, tm), jnp.int32)
    ri_ref[...] = jnp.concatenate([i1, i2, r1.astype(jnp.int32), r2.astype(jnp.int32), zi], axis=0)
    rw_ref[...] = jnp.concatenate([w1, w2, jnp.zeros((6, tm), F32)], axis=0)


def _attn_route(xf, mixed, kmem, vmem, w_out, norm_xa_w, wq, wo, norm_moe_w, w_route_t, b_route, sut,
                batch, seq):
    n, d = xf.shape
    w_route_hi = w_route_t.astype(BF16)
    w_route_lo = (w_route_t - w_route_hi.astype(F32)).astype(BF16)
    tm = TOKEN_TILE
    nt = seq // tm
    m = kmem.shape[1]
    tok = lambda b, t: (b * nt + t, 0)
    lane_tok = lambda b, t: (0, b * nt + t)
    const2 = lambda b, t: (0, 0)
    return pl.pallas_call(
        _attn_route_kernel,
        grid=(batch, nt),
        in_specs=[
            pl.BlockSpec((tm, d), tok),
            pl.BlockSpec((tm, d), tok),
            pl.BlockSpec((1, m, d), lambda b, t: (b, 0, 0)),
            pl.BlockSpec((1, m, d), lambda b, t: (b, 0, 0)),
            pl.BlockSpec((d, d), const2),
            pl.BlockSpec((1, d), const2),
            pl.BlockSpec((d, d), const2),
            pl.BlockSpec((d, d), const2),
            pl.BlockSpec((1, d), const2),
            pl.BlockSpec((ROUTE_ROWS, d), const2),
            pl.BlockSpec((ROUTE_ROWS, d), const2),
            pl.BlockSpec((ROUTE_ROWS, 1), const2),
            pl.BlockSpec((tm, tm), const2),
        ],
        out_specs=[
            pl.BlockSpec((tm, d), tok),
            pl.BlockSpec((tm, d // 2), tok),
            pl.BlockSpec((8, tm), lane_tok),
            pl.BlockSpec((8, tm), lane_tok),
            pl.BlockSpec((N_EXPERTS, 128), const2),
        ],
        out_shape=[
            jax.ShapeDtypeStruct((n, d), F32),
            jax.ShapeDtypeStruct((n, d // 2), jnp.uint32),
            jax.ShapeDtypeStruct((8, n), jnp.int32),
            jax.ShapeDtypeStruct((8, n), F32),
            jax.ShapeDtypeStruct((N_EXPERTS, 128), F32),
        ],
        scratch_shapes=[pltpu.VMEM((N_EXPERTS, 128), F32)],
        compiler_params=pltpu.CompilerParams(
            dimension_semantics=("arbitrary", "arbitrary"), vmem_limit_bytes=VMEM_LIMIT),
        name="attn_route",
    )(xf, mixed, kmem, vmem, w_out, norm_xa_w, wq, wo, norm_moe_w, w_route_hi, w_route_lo, b_route, sut)


def _dispatch_kernel(zpos_ref, dest_ref, h_ref, xs_ref, idx_ref, idx_sem, row_sem, zero_ref, zero_sem):
    i = pl.program_id(0)
    nsteps = pl.num_programs(0)
    td = h_ref.shape[0]
    bm = zero_ref.shape[0]
    slot = i % 2

    def idx_copy(step, sl):
        off = pl.multiple_of(sl * (2 * td), 2 * td)
        return pltpu.make_async_copy(dest_ref.at[step], idx_ref.at[pl.ds(off, 2 * td)], idx_sem.at[sl])

    @pl.when(i == 0)
    def _():
        zero_ref[...] = jnp.zeros_like(zero_ref)

        def zero_copy(e):
            return pltpu.make_async_copy(zero_ref, xs_ref.at[pl.ds(pl.multiple_of(zpos_ref[e], bm), bm), 0], zero_sem)

        def tail_copy(b):
            return pltpu.make_async_copy(zero_ref, xs_ref.at[pl.ds(pl.multiple_of(b * bm, bm), bm), 0], zero_sem)

        nused = zpos_ref[N_EXPERTS]
        nblk = xs_ref.shape[0] // bm
        for e in range(N_EXPERTS):
            pl.when(zpos_ref[e] >= 0)(lambda e=e: zero_copy(e).start())
        lax.fori_loop(nused, nblk, lambda b, c: (tail_copy(b).start(), c)[1], 0)
        for e in range(N_EXPERTS):
            pl.when(zpos_ref[e] >= 0)(lambda e=e: zero_copy(e).wait())
        lax.fori_loop(nused, nblk, lambda b, c: (tail_copy(b).wait(), c)[1], 0)
        idx_copy(0, 0).start()

    idx_copy(i, slot).wait()

    @pl.when(i + 1 < nsteps)
    def _():
        idx_copy(i + 1, 1 - slot).start()

    base = slot * (2 * td)

    for t in range(td):
        for k in range(TOP_K):
            pltpu.make_async_copy(h_ref.at[pl.ds(t, 1)], xs_ref.at[idx_ref[base + k * td + t]], row_sem).start()
    for _ in range(TOP_K):
        pltpu.make_async_copy(xs_ref.at[pl.ds(0, td)], xs_ref.at[pl.ds(0, td)], row_sem).wait()


def _dispatch(h3p, dest_tiles, zpos, cap):
    n, w = h3p.shape
    nt, td2 = dest_tiles.shape
    td = td2 // 2
    grid_spec = pltpu.PrefetchScalarGridSpec(
        num_scalar_prefetch=1,
        grid=(nt,),
        in_specs=[
            pl.BlockSpec(memory_space=pl.ANY),
            pl.BlockSpec((td, w), lambda i, zp: (i, 0)),
        ],
        out_specs=pl.BlockSpec(memory_space=pl.ANY),
        scratch_shapes=[
            pltpu.SMEM((2 * td2,), jnp.int32),
            pltpu.SemaphoreType.DMA((2,)),
            pltpu.SemaphoreType.DMA,
            pltpu.VMEM((MOE_ROWS, w), jnp.uint32),
            pltpu.SemaphoreType.DMA,
        ],
    )
    return pl.pallas_call(
        _dispatch_kernel,
        grid_spec=grid_spec,
        out_shape=jax.ShapeDtypeStruct((cap, 1, w), jnp.uint32),
        compiler_params=pltpu.CompilerParams(
            dimension_semantics=("arbitrary",), vmem_limit_bytes=VMEM_LIMIT),
        name="dispatch",
    )(zpos, dest_tiles, h3p)


def _expert_kernel(blk_e_ref, nused_ref, xs_ref, wg_ref, wu_ref, wd_ref, ys_ref, wg_b, wu_b, wd_b,
                   xbuf, ybuf, zbuf, in_sem, out_sem, zero_sem):
    i = pl.program_id(0)
    nsteps = pl.num_programs(0)
    nused = nused_ref[0]
    bm = xbuf.shape[1]
    slot = i % 2
    prev = blk_e_ref[jnp.maximum(i - 1, 0)]
    fresh = (i == 0) | (blk_e_ref[i] != prev)
    half = wd_b.shape[1] // 2

    def rows(ref, step):
        return ref.at[pl.ds(pl.multiple_of(step * bm, bm), bm), 0]

    def in_copy(step, sl):
        return pltpu.make_async_copy(rows(xs_ref, step), xbuf.at[sl], in_sem.at[sl])

    def out_copy(step, sl):
        return pltpu.make_async_copy(ybuf.at[sl], rows(ys_ref, step), out_sem.at[sl])

    def zero_copy(step):
        return pltpu.make_async_copy(zbuf, rows(ys_ref, step), zero_sem)

    @pl.when(i == 0)
    def _():
        zbuf[...] = jnp.zeros_like(zbuf)
        in_copy(0, 0).start()

    @pl.when(i + 1 < nused)
    def _():
        in_copy(i + 1, 1 - slot).start()

    @pl.when(fresh)
    def _():
        wg_b[...] = wg_ref[0].astype(BF16)
        wu_b[...] = wu_ref[0].astype(BF16)
        wd_b[...] = wd_ref[0].astype(BF16)

    @pl.when(i < nused)
    def _():
        in_copy(i, slot).wait()
        pl.when(i >= 2)(lambda: out_copy(i - 2, slot).wait())
        lo, hi = _unpack_rows(xbuf[slot])
        xb = jnp.concatenate([lo.astype(BF16), hi.astype(BF16)], axis=1)
        hid = (_silu(_dot(xb, wg_b[...])) * _dot(xb, wu_b[...])).astype(BF16)
        y = _dot(hid, wd_b[...])
        ybuf[slot] = _pack_rows(y[:, :half], y[:, half:])
        out_copy(i, slot).start()

    pl.when(i >= nused)(lambda: zero_copy(i).start())

    @pl.when(i == nsteps - 1)
    def _():
        pl.when(nused >= 2)(lambda: out_copy(nused - 2, nused % 2).wait())
        pl.when(nused >= 1)(lambda: out_copy(nused - 1, (nused - 1) % 2).wait())
        lax.fori_loop(nused, nsteps, lambda b, c: (zero_copy(b).wait(), c)[1], 0)


def _experts(xs, blk_e, nused, w_gate, w_up, w_down):
    cap, _, w = xs.shape
    _, d, de = w_gate.shape
    bm = MOE_ROWS
    grid_spec = pltpu.PrefetchScalarGridSpec(
        num_scalar_prefetch=2,
        grid=(cap // bm,),
        in_specs=[
            pl.BlockSpec(memory_space=pl.ANY),
            pl.BlockSpec((1, d, de), lambda i, be, nu: (be[i], 0, 0)),
            pl.BlockSpec((1, d, de), lambda i, be, nu: (be[i], 0, 0)),
            pl.BlockSpec((1, de, d), lambda i, be, nu: (be[i], 0, 0)),
        ],
        out_specs=pl.BlockSpec(memory_space=pl.ANY),
        scratch_shapes=[
            pltpu.VMEM((d, de), BF16),
            pltpu.VMEM((d, de), BF16),
            pltpu.VMEM((de, d), BF16),
            pltpu.VMEM((2, bm, w), jnp.uint32),
            pltpu.VMEM((2, bm, w), jnp.uint32),
            pltpu.VMEM((bm, w), jnp.uint32),
            pltpu.SemaphoreType.DMA((2,)),
            pltpu.SemaphoreType.DMA((2,)),
            pltpu.SemaphoreType.DMA,
        ],
    )
    return pl.pallas_call(
        _expert_kernel,
        grid_spec=grid_spec,
        out_shape=jax.ShapeDtypeStruct((cap, 1, w), jnp.uint32),
        compiler_params=pltpu.CompilerParams(
            dimension_semantics=("arbitrary",), vmem_limit_bytes=VMEM_LIMIT),
        name="experts",
    )(blk_e, nused, xs, w_gate, w_up, w_down)


def _combine_kernel(dest_ref, ys_ref, x2_ref, rw_ref, eye_ref, nw_ref, o_ref, idx_ref, idx_sem, ybuf, ysem):
    i = pl.program_id(0)
    nsteps = pl.num_programs(0)
    tc, d = x2_ref.shape
    half = d // 2
    n_idx = 2 * tc

    def idx_copy(step):
        sl = step % 3
        off = pl.multiple_of(sl * n_idx, n_idx)
        return pltpu.make_async_copy(dest_ref.at[step], idx_ref.at[pl.ds(off, n_idx)], idx_sem.at[sl])

    def gather(step):
        base = (step % 3) * n_idx
        buf = ybuf.at[step % 2]
        sem = ysem.at[step % 2]

        for t in range(n_idx):
            pltpu.make_async_copy(ys_ref.at[idx_ref[base + t]], buf.at[pl.ds(t, 1)], sem).start()

    @pl.when(i == 0)
    def _():
        idx_copy(0).start()
        idx_copy(0).wait()
        gather(0)

        @pl.when(nsteps > 1)
        def _():
            idx_copy(1).start()

    @pl.when(i + 1 < nsteps)
    def _():
        idx_copy(i + 1).wait()

        @pl.when(i + 2 < nsteps)
        def _():
            idx_copy(i + 2).start()

        gather(i + 1)

    slot = i % 2
    pltpu.make_async_copy(ybuf.at[slot], ybuf.at[slot], ysem.at[slot]).wait()
    wcol = _dot_nt(eye_ref[...], rw_ref[...], precision=HIGHEST)
    lo1, hi1 = _unpack_rows(ybuf[slot, 0:tc])
    lo2, hi2 = _unpack_rows(ybuf[slot, tc:n_idx])
    w1 = wcol[:, 0:1]
    w2 = wcol[:, 1:2]
    z_lo = x2_ref[:, :half] + (lo1 * w1 + lo2 * w2)
    z_hi = x2_ref[:, half:] + (hi1 * w1 + hi2 * w2)
    ms = (jnp.sum(z_lo * z_lo, axis=-1, keepdims=True) + jnp.sum(z_hi * z_hi, axis=-1, keepdims=True)) / d
    scale = lax.rsqrt(ms + EPS)
    o_ref[:, :half] = z_lo * scale * nw_ref[:, :half]
    o_ref[:, half:] = z_hi * scale * nw_ref[:, half:]


def _combine(x2, ys, dest_tiles, rw, eye, norm_w):
    n, d = x2.shape
    nt, n_idx = dest_tiles.shape
    tc = n_idx // 2
    w = ys.shape[-1]
    return pl.pallas_call(
        _combine_kernel,
        grid=(nt,),
        in_specs=[
            pl.BlockSpec(memory_space=pl.ANY),
            pl.BlockSpec(memory_space=pl.ANY),
            pl.BlockSpec((tc, d), lambda i: (i, 0)),
            pl.BlockSpec((8, tc), lambda i: (0, i)),
            pl.BlockSpec((tc, tc), lambda i: (0, 0)),
            pl.BlockSpec((1, d), lambda i: (0, 0)),
        ],
        out_specs=pl.BlockSpec((tc, d), lambda i: (i, 0)),
        out_shape=jax.ShapeDtypeStruct((n, d), F32),
        scratch_shapes=[
            pltpu.SMEM((3 * n_idx,), jnp.int32),
            pltpu.SemaphoreType.DMA((3,)),
            pltpu.VMEM((2, n_idx, w), jnp.uint32),
            pltpu.SemaphoreType.DMA((2,)),
        ],
        compiler_params=pltpu.CompilerParams(
            dimension_semantics=("arbitrary",), vmem_limit_bytes=VMEM_LIMIT),
        name="combine",
    )(dest_tiles, ys, x2, rw, eye, norm_w)


def _rope_column_order():
    half = RET_DK // 2
    first = [h * RET_DK + j for h in range(RET_HEADS) for j in range(half)]
    second = [h * RET_DK + half + j for h in range(RET_HEADS) for j in range(half)]
    return np.array(first + second, dtype=np.int32)


def _layer(xf, mem, batch, seq, norm_mix_w, w_in, ret_norm_w, ml_conv_w, ml_conv_b, ml_gate_b, ml_norm_w,
           w_out, norm_xa_w, norm_mem_w, xa_wq, xa_wkv, xa_wo, norm_moe_w, moe_w_group, moe_b_group,
           moe_w_router, moe_b_router, moe_w_gate, moe_w_up, moe_w_down, final_norm_w):
    n, d = xf.shape
    perm = _rope_column_order()
    cols = np.concatenate([perm, RET_QK + perm, np.arange(2 * RET_QK, MAIN_WIDTH)])
    w_main = w_in[:, cols].astype(BF16)
    w_if = w_in[:, MAIN_WIDTH:].astype(BF16)
    proj, g, gt = _inproj(xf, norm_mix_w.reshape(1, d), w_main, w_if, w_if.T)

    tabs = {k_: jnp.asarray(v_) for k_, v_ in _mixer_tables(seq).items()}
    mixed = _mixer(proj, g, gt, tabs, ret_norm_w.reshape(1, RET_V), ml_norm_w.reshape(1, ML_V), ml_conv_w,
                   ml_conv_b.reshape(1, 2 * ML_QK), ml_gate_b, batch, seq)

    kmem, vmem = _memkv(mem, norm_mem_w.reshape(1, d), xa_wkv.astype(BF16))

    w_route_t = jnp.concatenate(
        [moe_w_router.T, moe_w_group.T, jnp.zeros((ROUTE_ROWS - N_EXPERTS - N_GROUPS, d), F32)], axis=0)
    b_route = jnp.concatenate(
        [moe_b_router, moe_b_group, jnp.zeros((ROUTE_ROWS - N_EXPERTS - N_GROUPS,), F32)]).reshape(ROUTE_ROWS, 1)
    tm = TOKEN_TILE
    sut = jnp.asarray(np.triu(np.ones((tm, tm), np.float32), 1), dtype=BF16)
    x2, h3, ri, rw, cnt = _attn_route(xf, mixed, kmem, vmem, w_out.astype(BF16), norm_xa_w.reshape(1, d),
                                      xa_wq.astype(BF16), xa_wo.astype(BF16), norm_moe_w.reshape(1, d),
                                      w_route_t, b_route, sut, batch, seq)

    bm = MOE_ROWS
    counts = cnt[:, 0].astype(jnp.int32)
    padded = (counts + bm - 1) // bm * bm
    pends = jnp.cumsum(padded)
    pstarts = pends - padded
    expert = ri[0:TOP_K]
    onehot = expert[None] == jnp.arange(N_EXPERTS, dtype=jnp.int32)[:, None, None]
    dest = jnp.sum(jnp.where(onehot, pstarts[:, None, None], 0), axis=0) + ri[TOP_K:2 * TOP_K]
    cap = n * TOP_K + N_EXPERTS * bm
    nblk = cap // bm
    blk_start = jnp.arange(nblk, dtype=jnp.int32) * bm
    blk_e = jnp.minimum(jnp.sum(blk_start[:, None] >= pends[None, :], axis=1), N_EXPERTS - 1).astype(jnp.int32)
    nused = (pends[-1] // bm).astype(jnp.int32).reshape(1)
    zpos = jnp.where(padded > counts, pends - bm, -1).astype(jnp.int32)
    zpos = jnp.concatenate([zpos, nused])

    def tiles(rows):
        return dest.reshape(TOP_K, n // rows, rows).transpose(1, 0, 2).reshape(n // rows, TOP_K * rows)

    xs = _dispatch(h3, tiles(DISPATCH_TILE), zpos, cap)
    ys = _experts(xs, blk_e, nused, moe_w_gate, moe_w_up, moe_w_down)
    eye = jnp.asarray(np.eye(tm, dtype=np.float32))
    return _combine(x2, ys, tiles(tm), rw, eye, final_norm_w.reshape(1, d))


def kernel(x, mem, norm_mix_w, w_in, ret_norm_w, ml_conv_w, ml_conv_b, ml_gate_b, ml_norm_w, w_out, norm_xa_w, norm_mem_w, xa_wq, xa_wkv, xa_wo, norm_moe_w, moe_w_group, moe_b_group, moe_w_router, moe_b_router, moe_w_gate, moe_w_up, moe_w_down, norm_final_w):
    batch, seq, d = x.shape
    depth = w_in.shape[0]
    assert depth == 1, "the final norm is fused into the last layer's combine kernel"
    l = 0
    out = _layer(x.reshape(batch * seq, d), mem, batch, seq, norm_mix_w[l], w_in[l], ret_norm_w[l], ml_conv_w[l],
                 ml_conv_b[l], ml_gate_b[l], ml_norm_w[l], w_out[l], norm_xa_w[l], norm_mem_w[l], xa_wq[l],
                 xa_wkv[l], xa_wo[l], norm_moe_w[l], moe_w_group[l], moe_b_group[l], moe_w_router[l],
                 moe_b_router[l], moe_w_gate[l], moe_w_up[l], moe_w_down[l], norm_final_w)
    return out.reshape(batch, seq, d)
```

```python
import functools

import numpy as np
import jax
import jax.numpy as jnp
from jax import lax
from jax.experimental import pallas as pl
from jax.experimental.pallas import tpu as pltpu

F32 = jnp.float32
BF16 = jnp.bfloat16
HIGHEST = lax.Precision.HIGHEST

CHUNK = 128
RET_HEADS = 4
RET_DK = 64
RET_DV = 128
ML_HEADS = 4
ML_DK = 128
ML_DV = 128
CONV_W = 4
XA_HEADS = 4
N_GROUPS = 4
EXP_PER_GROUP = 8
N_EXPERTS = N_GROUPS * EXP_PER_GROUP
TOP_K = 2
ROPE_BASE = 10000.0
EPS = 1e-6

RET_QK = RET_HEADS * RET_DK
RET_V = RET_HEADS * RET_DV
ML_QK = ML_HEADS * ML_DK
ML_V = ML_HEADS * ML_DV
OFF_RQ = 0
OFF_RK = OFF_RQ + RET_QK
OFF_RV = OFF_RK + RET_QK
OFF_RG = OFF_RV + RET_V
OFF_MQK = OFF_RG + RET_V
OFF_MV = OFF_MQK + 2 * ML_QK
OFF_MO = OFF_MV + ML_V
MAIN_WIDTH = OFF_MO + ML_V
N_GATES = 2 * ML_HEADS

ROUTE_ROWS = 40
TOKEN_TILE = 512
MOE_ROWS = 512
DISPATCH_TILE = 1024
MIXER_BATCHES = 2
ATTN_GROUPS = 2
VMEM_LIMIT = 56 * 1024 * 1024


def _dot(a, b):
    return jnp.dot(a, b, preferred_element_type=F32)


def _dot_nt(a, b, precision=None):
    return lax.dot_general(a, b, (((1,), (1,)), ((), ())), preferred_element_type=F32, precision=precision)


def _dot_tn(a, b):
    return lax.dot_general(a, b, (((0,), (0,)), ((), ())), preferred_element_type=F32)


def _rms(x, w):
    return x * lax.rsqrt(jnp.mean(x * x, axis=-1, keepdims=True) + EPS) * w


def _sigmoid(x):
    return 1.0 / (1.0 + jnp.exp(-x))


def _silu(x):
    return x * _sigmoid(x)


def _log_sigmoid(x):
    return jnp.minimum(x, 0.0) - jnp.log1p(jnp.exp(-jnp.abs(x)))


def _head_norms(ts, mean_w):
    mu = [_dot(t.astype(BF16), mean_w) for t in ts]
    dl = [t - m for t, m in zip(ts, mu)]
    var = [_dot((d * d).astype(BF16), mean_w) for d in dl]
    return [d * lax.rsqrt(v + EPS) for d, v in zip(dl, var)]


def _pack_rows(lo, hi):
    def bits(t):
        return lax.bitcast_convert_type(t.astype(BF16), jnp.uint16).astype(jnp.uint32)
    return bits(lo) | (bits(hi) << 16)


def _unpack_rows(u):
    lo = lax.bitcast_convert_type(u << 16, F32)
    hi = lax.bitcast_convert_type(u & jnp.uint32(0xFFFF0000), F32)
    return lo, hi


def _inproj_kernel(tiles_per_seq, x_ref, nw_ref, w_ref, wif_ref, wift_ref, cos_ref, sin_ref, convw_ref, convb_ref,
                   proj_ref, g_ref, gt_ref, carry_ref):
    tm = x_ref.shape[0]

    @pl.when(pl.program_id(0) == 0)
    def _():
        carry_ref[...] = jnp.zeros_like(carry_ref)

    h = _rms(x_ref[...], nw_ref[...]).astype(BF16)

    def mm(off, width):
        return _dot(h, w_ref[:, off:off + width])

    def rotary(qk):
        cos = cos_ref[...]
        sin = sin_ref[...]
        half = RET_QK // 2
        for off, scale in ((OFF_RQ, None), (OFF_RK, RET_DK ** -0.5)):
            t1 = qk[:, off - OFF_RQ:off - OFF_RQ + half]
            t2 = qk[:, off - OFF_RQ + half:off - OFF_RQ + 2 * half]
            r1 = t1 * cos - t2 * sin
            r2 = t1 * sin + t2 * cos
            if scale is not None:
                r1, r2 = r1 * scale, r2 * scale
            proj_ref[:, off:off + half] = r1.astype(BF16)
            proj_ref[:, off + half:off + 2 * half] = r2.astype(BF16)

    def conv_silu(part, scale, cur):
        c0 = part * ML_QK
        first = (pl.program_id(0) % tiles_per_seq) == 0
        row8 = lax.broadcasted_iota(jnp.int32, (8, ML_QK), 0)
        prev = jnp.where(first, 0.0, carry_ref[:, c0:c0 + ML_QK])
        acc = cur * convw_ref[CONV_W - 1:CONV_W, c0:c0 + ML_QK] + convb_ref[:, c0:c0 + ML_QK]
        for s in range(1, CONV_W):
            rolled = pltpu.roll(cur, s, 0)
            head8 = jnp.where(row8 < s, pltpu.roll(prev, s, 0), rolled[0:8])
            shifted = jnp.concatenate([head8, rolled[8:]], axis=0)
            acc = acc + shifted * convw_ref[CONV_W - 1 - s:CONV_W - s, c0:c0 + ML_QK]
        carry_ref[:, c0:c0 + ML_QK] = cur[tm - 8:tm]
        act = _silu(acc) if scale is None else _silu(acc) * scale
        proj_ref[:, OFF_MQK + c0:OFF_MQK + c0 + ML_QK] = act.astype(BF16)

    def store(off, width, fn=None):
        def ep(t):
            proj_ref[:, off:off + width] = (t if fn is None else fn(t)).astype(BF16)
        return ep

    rotary(mm(OFF_RQ, 2 * RET_QK))
    store(OFF_RV, RET_V)(mm(OFF_RV, RET_V))
    store(OFF_RG, RET_V, _silu)(mm(OFF_RG, RET_V))
    conv_silu(0, None, mm(OFF_MQK, ML_QK))
    conv_silu(1, ML_DK ** -0.5, mm(OFF_MQK + ML_QK, ML_QK))
    store(OFF_MV, ML_V)(mm(OFF_MV, ML_V))
    store(OFF_MO, ML_V, _sigmoid)(mm(OFF_MO, ML_V))
    g_ref[...] = _dot(h, wif_ref[...])
    gt_ref[...] = _dot_nt(wift_ref[...], h)


def _inproj(xf, norm_w, w_main, w_if, w_ift, cos, sin, conv_w, conv_b, seq):
    n, d = xf.shape
    tm = TOKEN_TILE
    tiles_per_seq = seq // tm
    return pl.pallas_call(
        functools.partial(_inproj_kernel, tiles_per_seq),
        grid=(n // tm,),
        in_specs=[
            pl.BlockSpec((tm, d), lambda i: (i, 0)),
            pl.BlockSpec((1, d), lambda i: (0, 0)),
            pl.BlockSpec((d, MAIN_WIDTH), lambda i: (0, 0)),
            pl.BlockSpec((d, N_GATES), lambda i: (0, 0)),
            pl.BlockSpec((N_GATES, d), lambda i: (0, 0)),
            pl.BlockSpec((tm, RET_QK // 2), lambda i: (i % tiles_per_seq, 0)),
            pl.BlockSpec((tm, RET_QK // 2), lambda i: (i % tiles_per_seq, 0)),
            pl.BlockSpec((CONV_W, 2 * ML_QK), lambda i: (0, 0)),
            pl.BlockSpec((1, 2 * ML_QK), lambda i: (0, 0)),
        ],
        out_specs=[
            pl.BlockSpec((tm, MAIN_WIDTH), lambda i: (i, 0)),
            pl.BlockSpec((tm, N_GATES), lambda i: (i, 0)),
            pl.BlockSpec((N_GATES, tm), lambda i: (0, i)),
        ],
        out_shape=[
            jax.ShapeDtypeStruct((n, MAIN_WIDTH), BF16),
            jax.ShapeDtypeStruct((n, N_GATES), F32),
            jax.ShapeDtypeStruct((N_GATES, n), F32),
        ],
        scratch_shapes=[pltpu.VMEM((8, 2 * ML_QK), F32)],
        compiler_params=pltpu.CompilerParams(
            dimension_semantics=("arbitrary",), vmem_limit_bytes=VMEM_LIMIT),
        name="inproj",
    )(xf, norm_w, w_main, w_if, w_ift, cos, sin, conv_w, conv_b)


def _mixer_kernel(*refs):
    @pl.when(pl.program_id(1) == 0)
    def _():
        for state_ref in refs[-4:]:
            state_ref[...] = jnp.zeros_like(state_ref)

    for bi in range(refs[0].shape[0]):
        _mixer_one(bi, *refs)


def _mixer_one(bi, proj_ref, g_ref, gt_ref, qdec_ref, kdec_ref, dmat_ref,
               bmask_ref, cdec_ref, hmask_ref, tril_ref, triu_ref, ones_ref, retw_ref, mlw_ref, gbc_ref, gbr_ref,
               out_ref, r_ref, c_ref, n_ref, m_ref):
    L = CHUNK
    proj_ref, g_ref, gt_ref, out_ref = proj_ref.at[bi], g_ref.at[bi], gt_ref.at[bi], out_ref.at[bi]
    r_ref, c_ref, n_ref, m_ref = r_ref.at[bi], c_ref.at[bi], n_ref.at[bi], m_ref.at[bi]
    mean_w = ones_ref[...]


    q = proj_ref[:, OFF_RQ:OFF_RQ + RET_QK].astype(F32)
    k_b = proj_ref[:, OFF_RK:OFF_RK + RET_QK]
    k = k_b.astype(F32)
    v = proj_ref[:, OFF_RV:OFF_RV + RET_V]
    r_prev = r_ref[...]
    cross = _dot((q * qdec_ref[...]).astype(BF16), r_prev.astype(BF16))
    kv = _dot_tn((k * kdec_ref[...]).astype(BF16), v) * bmask_ref[...]
    r_ref[...] = cdec_ref[...] * r_prev + kv
    RH = range(RET_HEADS)
    sc = [_dot_nt((q * hmask_ref[h:h + 1, :]).astype(BF16), k_b) for h in RH]
    sc = [(sc[h] * dmat_ref[h]).astype(BF16) for h in RH]
    tot = [_dot(sc[h], v[:, h * RET_DV:(h + 1) * RET_DV]) + cross[:, h * RET_DV:(h + 1) * RET_DV] for h in RH]
    ret = jnp.concatenate(_head_norms(tot, mean_w), axis=1) * retw_ref[...]
    ret = ret * proj_ref[:, OFF_RG:OFF_RG + RET_V].astype(F32)
    out_ref[:, 0:RET_V] = ret.astype(BF16)

    mq = proj_ref[:, OFF_MQK:OFF_MQK + ML_QK]
    mk = proj_ref[:, OFF_MQK + ML_QK:OFF_MQK + 2 * ML_QK]
    mv = proj_ref[:, OFF_MV:OFF_MV + ML_V]

    gc = g_ref[...] + gbc_ref[...]
    gr = gt_ref[...] + gbr_ref[...]
    b_c = jnp.dot(tril_ref[...], _log_sigmoid(gc), preferred_element_type=F32, precision=HIGHEST)
    b_r = jnp.dot(_log_sigmoid(gr), triu_ref[...], preferred_element_type=F32, precision=HIGHEST)
    causal = (lax.broadcasted_iota(jnp.int32, (L, L), 0) >= lax.broadcasted_iota(jnp.int32, (L, L), 1))
    MH = range(ML_HEADS)
    bc = [b_c[:, ML_HEADS + h:ML_HEADS + h + 1] for h in MH]
    br = [b_r[ML_HEADS + h:ML_HEADS + h + 1, :] for h in MH]
    igc = [gc[:, h:h + 1] for h in MH]
    igr = [gr[h:h + 1, :] for h in MH]
    btot = [br[h][:, L - 1:L] for h in MH]
    qh_b = [mq[:, h * ML_DK:(h + 1) * ML_DK] for h in MH]
    kh_b = [mk[:, h * ML_DK:(h + 1) * ML_DK] for h in MH]
    vh = [mv[:, h * ML_DV:(h + 1) * ML_DV] for h in MH]
    c_prev = [c_ref[h] for h in MH]
    n_prev = [n_ref[h][0:1, :] for h in MH]
    m_prev = [m_ref[h][0:1, 0:1] for h in MH]
    s_raw = [_dot_nt(qh_b[h], kh_b[h]) for h in MH]
    qc = [_dot(qh_b[h], c_prev[h].astype(BF16)) for h in MH]
    log_d = [jnp.where(causal, bc[h] - br[h] + igr[h], -jnp.inf) for h in MH]
    m_intra = [jnp.max(log_d[h], axis=1, keepdims=True) for h in MH]
    m_loc = [jnp.max(btot[h] - br[h] + igr[h], axis=1, keepdims=True) for h in MH]
    kw = [kh_b[h].astype(F32) * jnp.exp(btot[h] - bc[h] + igc[h] - m_loc[h]) for h in MH]
    kv_loc = [_dot_tn(kw[h].astype(BF16), vh[h]) for h in MH]
    n_loc = [jnp.sum(kw[h], axis=0, keepdims=True) for h in MH]
    m_inter = [bc[h] + m_prev[h] for h in MH]
    m_t = [jnp.maximum(m_intra[h], m_inter[h]) for h in MH]
    s_mat = [s_raw[h] * jnp.exp(log_d[h] - m_t[h]) for h in MH]
    inter = [jnp.exp(m_inter[h] - m_t[h]) for h in MH]
    num = [_dot(s_mat[h].astype(BF16), vh[h]) + inter[h] * qc[h] for h in MH]
    den = [jnp.sum(s_mat[h], axis=1, keepdims=True)
           + inter[h] * jnp.sum(qh_b[h].astype(F32) * n_prev[h], axis=1, keepdims=True) for h in MH]
    hh = [num[h] / jnp.maximum(jnp.abs(den[h]), jnp.exp(-m_t[h])) for h in MH]
    for h in MH:
        m_new = jnp.maximum(btot[h] + m_prev[h], m_loc[h])
        s_old = jnp.exp(btot[h] + m_prev[h] - m_new)
        s_loc = jnp.exp(m_loc[h] - m_new)
        c_ref[h] = s_old * c_prev[h] + s_loc * kv_loc[h]
        n_ref[h] = jnp.broadcast_to(s_old * n_prev[h] + s_loc * n_loc[h], (8, ML_DK))
        m_ref[h] = jnp.broadcast_to(m_new, (8, 128))
    ml = jnp.concatenate(_head_norms(hh, mean_w), axis=1) * mlw_ref[...]
    ml = ml * proj_ref[:, OFF_MO:OFF_MO + ML_V].astype(F32)
    out_ref[:, RET_V:RET_V + ML_V] = ml.astype(BF16)


def _mixer_tables(seq):
    L = CHUNK
    half = RET_DK // 2
    inv = ROPE_BASE ** (-np.arange(half, dtype=np.float64) / half)
    ang = np.arange(seq, dtype=np.float64)[:, None] * inv[None, :].astype(np.float32).astype(np.float64)
    cos = np.tile(np.cos(ang), (1, RET_HEADS)).astype(np.float32)
    sin = np.tile(np.sin(ang), (1, RET_HEADS)).astype(np.float32)
    log_g = np.log1p(-np.exp2(-5.0 - np.arange(RET_HEADS, dtype=np.float64)))
    n = np.arange(L, dtype=np.float64)
    lane_head = (np.arange(RET_QK) % (RET_QK // 2)) // half
    qdec = np.exp((n + 1)[:, None] * log_g[lane_head][None, :]).astype(np.float32)
    kdec = np.exp((L - 1 - n)[:, None] * log_g[lane_head][None, :]).astype(np.float32)
    diff = n[:, None] - n[None, :]
    dmat = np.where(diff >= 0, np.exp(log_g[:, None, None] * np.maximum(diff, 0.0)[None]), 0.0).astype(np.float32)
    col_head = np.arange(RET_V) // RET_DV
    bmask = (lane_head[:, None] == col_head[None, :]).astype(np.float32)
    cdec = np.exp(L * log_g[col_head])[None, :].astype(np.float32)
    hmask = (lane_head[None, :] == np.arange(RET_HEADS)[:, None]).astype(np.float32)
    hmask = np.concatenate([hmask, np.zeros((8 - RET_HEADS, RET_QK), np.float32)], axis=0)
    tril = np.tril(np.ones((L, L), np.float32))
    ones = np.full((RET_DV, RET_DV), 1.0 / RET_DV, np.float32)
    return dict(cos=cos, sin=sin, qdec=qdec, kdec=kdec, dmat=dmat, bmask=bmask, cdec=cdec, hmask=hmask,
                tril=tril, triu=np.ascontiguousarray(tril.T), ones=ones)


def _mixer(proj, g, gt, tabs, ret_norm_w, ml_norm_w, gate_b, batch, seq):
    L = CHUNK
    nc = seq // L
    n = batch * seq
    nb = MIXER_BATCHES if batch % MIXER_BATCHES == 0 else 1
    proj = proj.reshape(batch, seq, MAIN_WIDTH)
    g = g.reshape(batch, seq, N_GATES)
    gt = gt.reshape(N_GATES, batch, seq).transpose(1, 0, 2)
    const2 = lambda b, c: (0, 0)
    const3 = lambda b, c: (0, 0, 0)
    tok = lambda b, c: (b, c, 0)
    in_specs = [
        pl.BlockSpec((nb, L, MAIN_WIDTH), tok),
        pl.BlockSpec((nb, L, N_GATES), tok),
        pl.BlockSpec((nb, N_GATES, L), lambda b, c: (b, 0, c)),
        pl.BlockSpec((L, RET_QK), const2),
        pl.BlockSpec((L, RET_QK), const2),
        pl.BlockSpec((RET_HEADS, L, L), const3),
        pl.BlockSpec((RET_QK, RET_V), const2),
        pl.BlockSpec((1, RET_V), const2),
        pl.BlockSpec((8, RET_QK), const2),
        pl.BlockSpec((L, L), const2),
        pl.BlockSpec((L, L), const2),
        pl.BlockSpec((RET_DV, RET_DV), const2),
        pl.BlockSpec((1, RET_V), const2),
        pl.BlockSpec((1, ML_V), const2),
        pl.BlockSpec((1, N_GATES), const2),
        pl.BlockSpec((N_GATES, 1), const2),
    ]
    return pl.pallas_call(
        _mixer_kernel,
        grid=(batch // nb, nc),
        in_specs=in_specs,
        out_specs=pl.BlockSpec((nb, L, RET_V + ML_V), tok),
        out_shape=jax.ShapeDtypeStruct((batch, seq, RET_V + ML_V), BF16),
        scratch_shapes=[
            pltpu.VMEM((nb, RET_QK, RET_V), F32),
            pltpu.VMEM((nb, ML_HEADS, ML_DK, ML_DV), F32),
            pltpu.VMEM((nb, ML_HEADS, 8, ML_DK), F32),
            pltpu.VMEM((nb, ML_HEADS, 8, 128), F32),
        ],
        compiler_params=pltpu.CompilerParams(
            dimension_semantics=("arbitrary", "arbitrary"), vmem_limit_bytes=VMEM_LIMIT),
        name="mixer",
    )(proj, g, gt, tabs["qdec"], tabs["kdec"], tabs["dmat"], tabs["bmask"],
      tabs["cdec"], tabs["hmask"], tabs["tril"], tabs["triu"], tabs["ones"].astype(BF16), ret_norm_w, ml_norm_w,
      gate_b.reshape(1, N_GATES), gate_b.reshape(N_GATES, 1)).reshape(n, RET_V + ML_V)


def _memkv_kernel(mem_ref, nw_ref, wkv_ref, k_ref, v_ref):
    d = mem_ref.shape[-1]
    mn = _rms(mem_ref[0], nw_ref[...]).astype(BF16)
    k_ref[0] = _dot(mn, wkv_ref[:, :d]).astype(BF16)
    v_ref[0] = _dot(mn, wkv_ref[:, d:]).astype(BF16)


def _memkv(mem, norm_w, wkv):
    b, m, d = mem.shape
    return pl.pallas_call(
        _memkv_kernel,
        grid=(b,),
        in_specs=[
            pl.BlockSpec((1, m, d), lambda i: (i, 0, 0)),
            pl.BlockSpec((1, d), lambda i: (0, 0)),
            pl.BlockSpec((d, 2 * d), lambda i: (0, 0)),
        ],
        out_specs=[pl.BlockSpec((1, m, d), lambda i: (i, 0, 0))] * 2,
        out_shape=[jax.ShapeDtypeStruct((b, m, d), BF16)] * 2,
        compiler_params=pltpu.CompilerParams(
            dimension_semantics=("arbitrary",), vmem_limit_bytes=VMEM_LIMIT),
        name="memkv",
    )(mem, norm_w, wkv)


def _attn_route_kernel(x_ref, mix_ref, k_ref, v_ref, wout_ref, nxa_ref, wq_ref, wo_ref, nmoe_ref,
                       wr_ref, wrlo_ref, br_ref, sut_ref,
                       x2_ref, h3_ref, ri_ref, rw_ref, cnt_ref, carry_ref):
    tm, d = x_ref.shape
    dh = d // XA_HEADS

    @pl.when((pl.program_id(0) == 0) & (pl.program_id(1) == 0))
    def _():
        carry_ref[...] = jnp.zeros_like(carry_ref)

    groups = [slice(g * (tm // ATTN_GROUPS), (g + 1) * (tm // ATTN_GROUPS)) for g in range(ATTN_GROUPS)]
    x1 = [x_ref[s, :] + _dot(mix_ref[s, :], wout_ref[...]) for s in groups]
    h2 = [_rms(t, nxa_ref[...]).astype(BF16) for t in x1]
    q = [_dot(t, wq_ref[...]).astype(BF16) for t in h2]
    o = []
    for qg in q:
        heads = []
        for h in range(XA_HEADS):
            logits = _dot_nt(qg[:, h * dh:(h + 1) * dh], k_ref[0, :, h * dh:(h + 1) * dh]) * (dh ** -0.5)
            mx = jnp.max(logits, axis=-1, keepdims=True)
            e = jnp.exp(logits - mx)
            p = (e / jnp.sum(e, axis=-1, keepdims=True)).astype(BF16)
            heads.append(_dot(p, v_ref[0, :, h * dh:(h + 1) * dh]).astype(BF16))
        o.append(jnp.concatenate(heads, axis=1))
    x2 = [a + _dot(b, wo_ref[...]) for a, b in zip(x1, o)]
    for s, t in zip(groups, x2):
        x2_ref[s, :] = t
    h3 = [_rms(t, nmoe_ref[...]) for t in x2]
    for s, t in zip(groups, h3):
        h3_ref[s, :] = _pack_rows(t[:, :d // 2], t[:, d // 2:])

    lts = []
    for t in h3:
        t_hi = t.astype(BF16)
        t_lo = (t - t_hi.astype(F32)).astype(BF16)
        lts.append(_dot_nt(wr_ref[...], t_hi) + (_dot_nt(wr_ref[...], t_lo) + _dot_nt(wrlo_ref[...], t_hi)))
    lt = jnp.concatenate(lts, axis=1) + br_ref[...]
    gl = lt[N_EXPERTS:N_EXPERTS + N_GROUPS]
    gmax = jnp.max(gl, axis=0, keepdims=True)
    g_w = 1.0 / jnp.sum(jnp.exp(gl - gmax), axis=0, keepdims=True)
    giota = lax.broadcasted_iota(jnp.int32, gl.shape, 0)
    g_sel = jnp.min(jnp.where(gl == gmax, giota, N_GROUPS), axis=0, keepdims=True)
    el = lt[0:N_EXPERTS]
    eiota = lax.broadcasted_iota(jnp.int32, el.shape, 0)
    in_grp = (eiota // EXP_PER_GROUP) == g_sel
    elm = jnp.where(in_grp, el, -jnp.inf)
    m1 = jnp.max(elm, axis=0, keepdims=True)
    esum = jnp.sum(jnp.where(in_grp, jnp.exp(el - m1), 0.0), axis=0, keepdims=True)
    i1 = jnp.min(jnp.where(elm == m1, eiota, N_EXPERTS), axis=0, keepdims=True)
    elm2 = jnp.where(eiota == i1, -jnp.inf, elm)
    m2 = jnp.max(elm2, axis=0, keepdims=True)
    i2 = jnp.min(jnp.where(elm2 == m2, eiota, N_EXPERTS), axis=0, keepdims=True)
    p1 = 1.0 / esum
    p2 = jnp.exp(m2 - m1) / esum
    psum = p1 + p2
    w1 = g_w * (p1 / psum)
    w2 = g_w * (p2 / psum)

    oh1 = (eiota == i1).astype(F32)
    oh2 = (eiota == i2).astype(F32)
    cnt = oh1 + oh2
    base = carry_ref[:, 0:1] + _dot(cnt.astype(BF16), sut_ref[...])
    r1 = jnp.sum(oh1 * base, axis=0, keepdims=True)
    r2 = jnp.sum(oh2 * base, axis=0, keepdims=True)
    new_carry = carry_ref[...] + jnp.sum(cnt, axis=1, keepdims=True)
    carry_ref[...] = new_carry
    cnt_ref[...] = new_carry

    zi = jnp.zeros((4, tm), jnp.int32)
    ri_ref[...] = jnp.concatenate([i1, i2, r1.astype(jnp.int32), r2.astype(jnp.int32), zi], axis=0)
    rw_ref[...] = jnp.concatenate([w1, w2, jnp.zeros((6, tm), F32)], axis=0)


def _attn_route(xf, mixed, kmem, vmem, w_out, norm_xa_w, wq, wo, norm_moe_w, w_route_t, b_route, sut,
                batch, seq):
    n, d = xf.shape
    w_route_hi = w_route_t.astype(BF16)
    w_route_lo = (w_route_t - w_route_hi.astype(F32)).astype(BF16)
    tm = TOKEN_TILE
    nt = seq // tm
    m = kmem.shape[1]
    tok = lambda b, t: (b * nt + t, 0)
    lane_tok = lambda b, t: (0, b * nt + t)
    const2 = lambda b, t: (0, 0)
    return pl.pallas_call(
        _attn_route_kernel,
        grid=(batch, nt),
        in_specs=[
            pl.BlockSpec((tm, d), tok),
            pl.BlockSpec((tm, d), tok),
            pl.BlockSpec((1, m, d), lambda b, t: (b, 0, 0)),
            pl.BlockSpec((1, m, d), lambda b, t: (b, 0, 0)),
            pl.BlockSpec((d, d), const2),
            pl.BlockSpec((1, d), const2),
            pl.BlockSpec((d, d), const2),
            pl.BlockSpec((d, d), const2),
            pl.BlockSpec((1, d), const2),
            pl.BlockSpec((ROUTE_ROWS, d), const2),
            pl.BlockSpec((ROUTE_ROWS, d), const2),
            pl.BlockSpec((ROUTE_ROWS, 1), const2),
            pl.BlockSpec((tm, tm), const2),
        ],
        out_specs=[
            pl.BlockSpec((tm, d), tok),
            pl.BlockSpec((tm, d // 2), tok),
            pl.BlockSpec((8, tm), lane_tok),
            pl.BlockSpec((8, tm), lane_tok),
            pl.BlockSpec((N_EXPERTS, 128), const2),
        ],
        out_shape=[
            jax.ShapeDtypeStruct((n, d), F32),
            jax.ShapeDtypeStruct((n, d // 2), jnp.uint32),
            jax.ShapeDtypeStruct((8, n), jnp.int32),
            jax.ShapeDtypeStruct((8, n), F32),
            jax.ShapeDtypeStruct((N_EXPERTS, 128), F32),
        ],
        scratch_shapes=[pltpu.VMEM((N_EXPERTS, 128), F32)],
        compiler_params=pltpu.CompilerParams(
            dimension_semantics=("arbitrary", "arbitrary"), vmem_limit_bytes=VMEM_LIMIT),
        name="attn_route",
    )(xf, mixed, kmem, vmem, w_out, norm_xa_w, wq, wo, norm_moe_w, w_route_hi, w_route_lo, b_route, sut)


def _dispatch_kernel(zpos_ref, dest_ref, h_ref, xs_ref, idx_ref, idx_sem, row_sem, zero_ref, zero_sem):
    i = pl.program_id(0)
    nsteps = pl.num_programs(0)
    td = h_ref.shape[0]
    bm = zero_ref.shape[0]
    slot = i % 2

    def idx_copy(step, sl):
        off = pl.multiple_of(sl * (2 * td), 2 * td)
        return pltpu.make_async_copy(dest_ref.at[step], idx_ref.at[pl.ds(off, 2 * td)], idx_sem.at[sl])

    @pl.when(i == 0)
    def _():
        zero_ref[...] = jnp.zeros_like(zero_ref)

        def zero_copy(e):
            return pltpu.make_async_copy(zero_ref, xs_ref.at[pl.ds(pl.multiple_of(zpos_ref[e], bm), bm), 0], zero_sem)

        def tail_copy(b):
            return pltpu.make_async_copy(zero_ref, xs_ref.at[pl.ds(pl.multiple_of(b * bm, bm), bm), 0], zero_sem)

        nused = zpos_ref[N_EXPERTS]
        nblk = xs_ref.shape[0] // bm
        for e in range(N_EXPERTS):
            pl.when(zpos_ref[e] >= 0)(lambda e=e: zero_copy(e).start())
        lax.fori_loop(nused, nblk, lambda b, c: (tail_copy(b).start(), c)[1], 0)
        for e in range(N_EXPERTS):
            pl.when(zpos_ref[e] >= 0)(lambda e=e: zero_copy(e).wait())
        lax.fori_loop(nused, nblk, lambda b, c: (tail_copy(b).wait(), c)[1], 0)
        idx_copy(0, 0).start()

    idx_copy(i, slot).wait()

    @pl.when(i + 1 < nsteps)
    def _():
        idx_copy(i + 1, 1 - slot).start()

    base = slot * (2 * td)

    for t in range(td):
        for k in range(TOP_K):
            pltpu.make_async_copy(h_ref.at[pl.ds(t, 1)], xs_ref.at[idx_ref[base + k * td + t]], row_sem).start()
    for _ in range(TOP_K):
        pltpu.make_async_copy(xs_ref.at[pl.ds(0, td)], xs_ref.at[pl.ds(0, td)], row_sem).wait()


def _dispatch(h3p, dest_tiles, zpos, cap):
    n, w = h3p.shape
    nt, td2 = dest_tiles.shape
    td = td2 // 2
    grid_spec = pltpu.PrefetchScalarGridSpec(
        num_scalar_prefetch=1,
        grid=(nt,),
        in_specs=[
            pl.BlockSpec(memory_space=pl.ANY),
            pl.BlockSpec((td, w), lambda i, zp: (i, 0)),
        ],
        out_specs=pl.BlockSpec(memory_space=pl.ANY),
        scratch_shapes=[
            pltpu.SMEM((2 * td2,), jnp.int32),
            pltpu.SemaphoreType.DMA((2,)),
            pltpu.SemaphoreType.DMA,
            pltpu.VMEM((MOE_ROWS, w), jnp.uint32),
            pltpu.SemaphoreType.DMA,
        ],
    )
    return pl.pallas_call(
        _dispatch_kernel,
        grid_spec=grid_spec,
        out_shape=jax.ShapeDtypeStruct((cap, 1, w), jnp.uint32),
        compiler_params=pltpu.CompilerParams(
            dimension_semantics=("arbitrary",), vmem_limit_bytes=VMEM_LIMIT),
        name="dispatch",
    )(zpos, dest_tiles, h3p)


def _expert_kernel(blk_e_ref, nused_ref, xs_ref, wg_ref, wu_ref, wd_ref, ys_ref, wg_b, wu_b, wd_b,
                   xbuf, ybuf, zbuf, in_sem, out_sem, zero_sem):
    i = pl.program_id(0)
    nsteps = pl.num_programs(0)
    nused = nused_ref[0]
    bm = xbuf.shape[1]
    slot = i % 2
    prev = blk_e_ref[jnp.maximum(i - 1, 0)]
    fresh = (i == 0) | (blk_e_ref[i] != prev)
    half = wd_b.shape[1] // 2

    def rows(ref, step):
        return ref.at[pl.ds(pl.multiple_of(step * bm, bm), bm), 0]

    def in_copy(step, sl):
        return pltpu.make_async_copy(rows(xs_ref, step), xbuf.at[sl], in_sem.at[sl])

    def out_copy(step, sl):
        return pltpu.make_async_copy(ybuf.at[sl], rows(ys_ref, step), out_sem.at[sl])

    def zero_copy(step):
        return pltpu.make_async_copy(zbuf, rows(ys_ref, step), zero_sem)

    @pl.when(i == 0)
    def _():
        zbuf[...] = jnp.zeros_like(zbuf)
        in_copy(0, 0).start()

    @pl.when(i + 1 < nused)
    def _():
        in_copy(i + 1, 1 - slot).start()

    @pl.when(fresh)
    def _():
        wg_b[...] = wg_ref[0].astype(BF16)
        wu_b[...] = wu_ref[0].astype(BF16)
        wd_b[...] = wd_ref[0].astype(BF16)

    @pl.when(i < nused)
    def _():
        in_copy(i, slot).wait()
        pl.when(i >= 2)(lambda: out_copy(i - 2, slot).wait())
        lo, hi = _unpack_rows(xbuf[slot])
        xb = jnp.concatenate([lo.astype(BF16), hi.astype(BF16)], axis=1)
        hid = (_silu(_dot(xb, wg_b[...])) * _dot(xb, wu_b[...])).astype(BF16)
        y = _dot(hid, wd_b[...])
        ybuf[slot] = _pack_rows(y[:, :half], y[:, half:])
        out_copy(i, slot).start()

    pl.when(i >= nused)(lambda: zero_copy(i).start())

    @pl.when(i == nsteps - 1)
    def _():
        pl.when(nused >= 2)(lambda: out_copy(nused - 2, nused % 2).wait())
        pl.when(nused >= 1)(lambda: out_copy(nused - 1, (nused - 1) % 2).wait())
        lax.fori_loop(nused, nsteps, lambda b, c: (zero_copy(b).wait(), c)[1], 0)


def _experts(xs, blk_e, nused, w_gate, w_up, w_down):
    cap, _, w = xs.shape
    _, d, de = w_gate.shape
    bm = MOE_ROWS
    grid_spec = pltpu.PrefetchScalarGridSpec(
        num_scalar_prefetch=2,
        grid=(cap // bm,),
        in_specs=[
            pl.BlockSpec(memory_space=pl.ANY),
            pl.BlockSpec((1, d, de), lambda i, be, nu: (be[i], 0, 0)),
            pl.BlockSpec((1, d, de), lambda i, be, nu: (be[i], 0, 0)),
            pl.BlockSpec((1, de, d), lambda i, be, nu: (be[i], 0, 0)),
        ],
        out_specs=pl.BlockSpec(memory_space=pl.ANY),
        scratch_shapes=[
            pltpu.VMEM((d, de), BF16),
            pltpu.VMEM((d, de), BF16),
            pltpu.VMEM((de, d), BF16),
            pltpu.VMEM((2, bm, w), jnp.uint32),
            pltpu.VMEM((2, bm, w), jnp.uint32),
            pltpu.VMEM((bm, w), jnp.uint32),
            pltpu.SemaphoreType.DMA((2,)),
            pltpu.SemaphoreType.DMA((2,)),
            pltpu.SemaphoreType.DMA,
        ],
    )
    return pl.pallas_call(
        _expert_kernel,
        grid_spec=grid_spec,
        out_shape=jax.ShapeDtypeStruct((cap, 1, w), jnp.uint32),
        compiler_params=pltpu.CompilerParams(
            dimension_semantics=("arbitrary",), vmem_limit_bytes=VMEM_LIMIT),
        name="experts",
    )(blk_e, nused, xs, w_gate, w_up, w_down)


def _combine_kernel(dest_ref, ys_ref, x2_ref, rw_ref, eye_ref, nw_ref, o_ref, idx_ref, idx_sem, ybuf, ysem):
    i = pl.program_id(0)
    nsteps = pl.num_programs(0)
    tc, d = x2_ref.shape
    half = d // 2
    n_idx = 2 * tc

    def idx_copy(step):
        sl = step % 3
        off = pl.multiple_of(sl * n_idx, n_idx)
        return pltpu.make_async_copy(dest_ref.at[step], idx_ref.at[pl.ds(off, n_idx)], idx_sem.at[sl])

    def gather(step):
        base = (step % 3) * n_idx
        buf = ybuf.at[step % 2]
        sem = ysem.at[step % 2]

        for t in range(n_idx):
            pltpu.make_async_copy(ys_ref.at[idx_ref[base + t]], buf.at[pl.ds(t, 1)], sem).start()

    @pl.when(i == 0)
    def _():
        idx_copy(0).start()
        idx_copy(0).wait()
        gather(0)

        @pl.when(nsteps > 1)
        def _():
            idx_copy(1).start()

    @pl.when(i + 1 < nsteps)
    def _():
        idx_copy(i + 1).wait()

        @pl.when(i + 2 < nsteps)
        def _():
            idx_copy(i + 2).start()

        gather(i + 1)

    slot = i % 2
    pltpu.make_async_copy(ybuf.at[slot], ybuf.at[slot], ysem.at[slot]).wait()
    wcol = _dot_nt(eye_ref[...], rw_ref[...], precision=HIGHEST)
    lo1, hi1 = _unpack_rows(ybuf[slot, 0:tc])
    lo2, hi2 = _unpack_rows(ybuf[slot, tc:n_idx])
    w1 = wcol[:, 0:1]
    w2 = wcol[:, 1:2]
    z_lo = x2_ref[:, :half] + (lo1 * w1 + lo2 * w2)
    z_hi = x2_ref[:, half:] + (hi1 * w1 + hi2 * w2)
    ms = (jnp.sum(z_lo * z_lo, axis=-1, keepdims=True) + jnp.sum(z_hi * z_hi, axis=-1, keepdims=True)) / d
    scale = lax.rsqrt(ms + EPS)
    o_ref[:, :half] = z_lo * scale * nw_ref[:, :half]
    o_ref[:, half:] = z_hi * scale * nw_ref[:, half:]


def _combine(x2, ys, dest_tiles, rw, eye, norm_w):
    n, d = x2.shape
    nt, n_idx = dest_tiles.shape
    tc = n_idx // 2
    w = ys.shape[-1]
    return pl.pallas_call(
        _combine_kernel,
        grid=(nt,),
        in_specs=[
            pl.BlockSpec(memory_space=pl.ANY),
            pl.BlockSpec(memory_space=pl.ANY),
            pl.BlockSpec((tc, d), lambda i: (i, 0)),
            pl.BlockSpec((8, tc), lambda i: (0, i)),
            pl.BlockSpec((tc, tc), lambda i: (0, 0)),
            pl.BlockSpec((1, d), lambda i: (0, 0)),
        ],
        out_specs=pl.BlockSpec((tc, d), lambda i: (i, 0)),
        out_shape=jax.ShapeDtypeStruct((n, d), F32),
        scratch_shapes=[
            pltpu.SMEM((3 * n_idx,), jnp.int32),
            pltpu.SemaphoreType.DMA((3,)),
            pltpu.VMEM((2, n_idx, w), jnp.uint32),
            pltpu.SemaphoreType.DMA((2,)),
        ],
        compiler_params=pltpu.CompilerParams(
            dimension_semantics=("arbitrary",), vmem_limit_bytes=VMEM_LIMIT),
        name="combine",
    )(dest_tiles, ys, x2, rw, eye, norm_w)


def _rope_column_order():
    half = RET_DK // 2
    first = [h * RET_DK + j for h in range(RET_HEADS) for j in range(half)]
    second = [h * RET_DK + half + j for h in range(RET_HEADS) for j in range(half)]
    return np.array(first + second, dtype=np.int32)


def _layer(xf, mem, batch, seq, norm_mix_w, w_in, ret_norm_w, ml_conv_w, ml_conv_b, ml_gate_b, ml_norm_w,
           w_out, norm_xa_w, norm_mem_w, xa_wq, xa_wkv, xa_wo, norm_moe_w, moe_w_group, moe_b_group,
           moe_w_router, moe_b_router, moe_w_gate, moe_w_up, moe_w_down, final_norm_w):
    n, d = xf.shape
    perm = _rope_column_order()
    cols = np.concatenate([perm, RET_QK + perm, np.arange(2 * RET_QK, MAIN_WIDTH)])
    w_main = w_in[:, cols].astype(BF16)
    w_if = w_in[:, MAIN_WIDTH:].astype(BF16)
    tabs = {k_: jnp.asarray(v_) for k_, v_ in _mixer_tables(seq).items()}
    proj, g, gt = _inproj(xf, norm_mix_w.reshape(1, d), w_main, w_if, w_if.T, tabs["cos"], tabs["sin"], ml_conv_w,
                          ml_conv_b.reshape(1, 2 * ML_QK), seq)
    mixed = _mixer(proj, g, gt, tabs, ret_norm_w.reshape(1, RET_V), ml_norm_w.reshape(1, ML_V), ml_gate_b,
                   batch, seq)

    kmem, vmem = _memkv(mem, norm_mem_w.reshape(1, d), xa_wkv.astype(BF16))

    w_route_t = jnp.concatenate(
        [moe_w_router.T, moe_w_group.T, jnp.zeros((ROUTE_ROWS - N_EXPERTS - N_GROUPS, d), F32)], axis=0)
    b_route = jnp.concatenate(
        [moe_b_router, moe_b_group, jnp.zeros((ROUTE_ROWS - N_EXPERTS - N_GROUPS,), F32)]).reshape(ROUTE_ROWS, 1)
    tm = TOKEN_TILE
    sut = jnp.asarray(np.triu(np.ones((tm, tm), np.float32), 1), dtype=BF16)
    x2, h3, ri, rw, cnt = _attn_route(xf, mixed, kmem, vmem, w_out.astype(BF16), norm_xa_w.reshape(1, d),
                                      xa_wq.astype(BF16), xa_wo.astype(BF16), norm_moe_w.reshape(1, d),
                                      w_route_t, b_route, sut, batch, seq)

    bm = MOE_ROWS
    counts = cnt[:, 0].astype(jnp.int32)
    padded = (counts + bm - 1) // bm * bm
    pends = jnp.cumsum(padded)
    pstarts = pends - padded
    expert = ri[0:TOP_K]
    onehot = expert[None] == jnp.arange(N_EXPERTS, dtype=jnp.int32)[:, None, None]
    dest = jnp.sum(jnp.where(onehot, pstarts[:, None, None], 0), axis=0) + ri[TOP_K:2 * TOP_K]
    cap = n * TOP_K + N_EXPERTS * bm
    nblk = cap // bm
    blk_start = jnp.arange(nblk, dtype=jnp.int32) * bm
    blk_e = jnp.minimum(jnp.sum(blk_start[:, None] >= pends[None, :], axis=1), N_EXPERTS - 1).astype(jnp.int32)
    nused = (pends[-1] // bm).astype(jnp.int32).reshape(1)
    zpos = jnp.where(padded > counts, pends - bm, -1).astype(jnp.int32)
    zpos = jnp.concatenate([zpos, nused])

    def tiles(rows):
        return dest.reshape(TOP_K, n // rows, rows).transpose(1, 0, 2).reshape(n // rows, TOP_K * rows)

    xs = _dispatch(h3, tiles(DISPATCH_TILE), zpos, cap)
    ys = _experts(xs, blk_e, nused, moe_w_gate, moe_w_up, moe_w_down)
    eye = jnp.asarray(np.eye(tm, dtype=np.float32))
    return _combine(x2, ys, tiles(tm), rw, eye, final_norm_w.reshape(1, d))


def kernel(x, mem, norm_mix_w, w_in, ret_norm_w, ml_conv_w, ml_conv_b, ml_gate_b, ml_norm_w, w_out, norm_xa_w, norm_mem_w, xa_wq, xa_wkv, xa_wo, norm_moe_w, moe_w_group, moe_b_group, moe_w_router, moe_b_router, moe_w_gate, moe_w_up, moe_w_down, norm_final_w):
    batch, seq, d = x.shape
    depth = w_in.shape[0]
    assert depth == 1, "the final norm is fused into the last layer's combine kernel"
    l = 0
    out = _layer(x.reshape(batch * seq, d), mem, batch, seq, norm_mix_w[l], w_in[l], ret_norm_w[l], ml_conv_w[l],
                 ml_conv_b[l], ml_gate_b[l], ml_norm_w[l], w_out[l], norm_xa_w[l], norm_mem_w[l], xa_wq[l],
                 xa_wkv[l], xa_wo[l], norm_moe_w[l], moe_w_group[l], moe_b_group[l], moe_w_router[l],
                 moe_b_router[l], moe_w_gate[l], moe_w_up[l], moe_w_down[l], norm_final_w)
    return out.reshape(batch, seq, d)
```

```python
import functools

import numpy as np
import jax
import jax.numpy as jnp
from jax import lax
from jax.experimental import pallas as pl
from jax.experimental.pallas import tpu as pltpu

F32 = jnp.float32
BF16 = jnp.bfloat16
HIGHEST = lax.Precision.HIGHEST

CHUNK = 128
RET_HEADS = 4
RET_DK = 64
RET_DV = 128
ML_HEADS = 4
ML_DK = 128
ML_DV = 128
CONV_W = 4
XA_HEADS = 4
N_GROUPS = 4
EXP_PER_GROUP = 8
N_EXPERTS = N_GROUPS * EXP_PER_GROUP
TOP_K = 2
ROPE_BASE = 10000.0
EPS = 1e-6

RET_QK = RET_HEADS * RET_DK
RET_V = RET_HEADS * RET_DV
ML_QK = ML_HEADS * ML_DK
ML_V = ML_HEADS * ML_DV
OFF_RQ = 0
OFF_RK = OFF_RQ + RET_QK
OFF_RV = OFF_RK + RET_QK
OFF_RG = OFF_RV + RET_V
OFF_MQK = OFF_RG + RET_V
OFF_MV = OFF_MQK + 2 * ML_QK
OFF_MO = OFF_MV + ML_V
MAIN_WIDTH = OFF_MO + ML_V
N_GATES = 2 * ML_HEADS

ROUTE_ROWS = 40
TOKEN_TILE = 512
MOE_ROWS = 512
DISPATCH_TILE = 1024
MIXER_BATCHES = 2
MOE_PARTS = 1
ATTN_GROUPS = 2
VMEM_LIMIT = 56 * 1024 * 1024


def _dot(a, b):
    return jnp.dot(a, b, preferred_element_type=F32)


def _dot_nt(a, b, precision=None):
    return lax.dot_general(a, b, (((1,), (1,)), ((), ())), preferred_element_type=F32, precision=precision)


def _dot_tn(a, b):
    return lax.dot_general(a, b, (((0,), (0,)), ((), ())), preferred_element_type=F32)


def _rms(x, w):
    return x * lax.rsqrt(jnp.mean(x * x, axis=-1, keepdims=True) + EPS) * w


def _sigmoid(x):
    return 1.0 / (1.0 + jnp.exp(-x))


def _silu(x):
    return x * _sigmoid(x)


def _log_sigmoid(x):
    return jnp.minimum(x, 0.0) - jnp.log1p(jnp.exp(-jnp.abs(x)))


def _head_norms(ts, mean_w):
    mu = [_dot(t.astype(BF16), mean_w) for t in ts]
    dl = [t - m for t, m in zip(ts, mu)]
    var = [_dot((d * d).astype(BF16), mean_w) for d in dl]
    return [d * lax.rsqrt(v + EPS) for d, v in zip(dl, var)]


def _pack_rows(lo, hi):
    def bits(t):
        return lax.bitcast_convert_type(t.astype(BF16), jnp.uint16).astype(jnp.uint32)
    return bits(lo) | (bits(hi) << 16)


def _unpack_rows(u):
    lo = lax.bitcast_convert_type(u << 16, F32)
    hi = lax.bitcast_convert_type(u & jnp.uint32(0xFFFF0000), F32)
    return lo, hi


def _inproj_kernel(tiles_per_seq, x_ref, nw_ref, w_ref, wif_ref, wift_ref, cos_ref, sin_ref, convw_ref, convb_ref,
                   proj_ref, g_ref, gt_ref, carry_ref):
    tm = x_ref.shape[0]

    @pl.when(pl.program_id(0) == 0)
    def _():
        carry_ref[...] = jnp.zeros_like(carry_ref)

    h = _rms(x_ref[...], nw_ref[...]).astype(BF16)

    def mm(off, width):
        return _dot(h, w_ref[:, off:off + width])

    def rotary(qk):
        cos = cos_ref[...]
        sin = sin_ref[...]
        half = RET_QK // 2
        for off, scale in ((OFF_RQ, None), (OFF_RK, RET_DK ** -0.5)):
            t1 = qk[:, off - OFF_RQ:off - OFF_RQ + half]
            t2 = qk[:, off - OFF_RQ + half:off - OFF_RQ + 2 * half]
            r1 = t1 * cos - t2 * sin
            r2 = t1 * sin + t2 * cos
            if scale is not None:
                r1, r2 = r1 * scale, r2 * scale
            proj_ref[:, off:off + half] = r1.astype(BF16)
            proj_ref[:, off + half:off + 2 * half] = r2.astype(BF16)

    def conv_silu(part, scale, cur):
        c0 = part * ML_QK
        first = (pl.program_id(0) % tiles_per_seq) == 0
        row8 = lax.broadcasted_iota(jnp.int32, (8, ML_QK), 0)
        prev = jnp.where(first, 0.0, carry_ref[:, c0:c0 + ML_QK])
        acc = cur * convw_ref[CONV_W - 1:CONV_W, c0:c0 + ML_QK] + convb_ref[:, c0:c0 + ML_QK]
        for s in range(1, CONV_W):
            rolled = pltpu.roll(cur, s, 0)
            head8 = jnp.where(row8 < s, pltpu.roll(prev, s, 0), rolled[0:8])
            shifted = jnp.concatenate([head8, rolled[8:]], axis=0)
            acc = acc + shifted * convw_ref[CONV_W - 1 - s:CONV_W - s, c0:c0 + ML_QK]
        carry_ref[:, c0:c0 + ML_QK] = cur[tm - 8:tm]
        act = _silu(acc) if scale is None else _silu(acc) * scale
        proj_ref[:, OFF_MQK + c0:OFF_MQK + c0 + ML_QK] = act.astype(BF16)

    def store(off, width, fn=None):
        def ep(t):
            proj_ref[:, off:off + width] = (t if fn is None else fn(t)).astype(BF16)
        return ep

    rotary(mm(OFF_RQ, 2 * RET_QK))
    store(OFF_RV, RET_V)(mm(OFF_RV, RET_V))
    store(OFF_RG, RET_V, _silu)(mm(OFF_RG, RET_V))
    conv_silu(0, None, mm(OFF_MQK, ML_QK))
    conv_silu(1, ML_DK ** -0.5, mm(OFF_MQK + ML_QK, ML_QK))
    store(OFF_MV, ML_V)(mm(OFF_MV, ML_V))
    store(OFF_MO, ML_V, _sigmoid)(mm(OFF_MO, ML_V))
    g_ref[...] = _dot(h, wif_ref[...])
    gt_ref[...] = _dot_nt(wift_ref[...], h)


def _inproj(xf, norm_w, w_main, w_if, w_ift, cos, sin, conv_w, conv_b, seq):
    n, d = xf.shape
    tm = TOKEN_TILE
    tiles_per_seq = seq // tm
    return pl.pallas_call(
        functools.partial(_inproj_kernel, tiles_per_seq),
        grid=(n // tm,),
        in_specs=[
            pl.BlockSpec((tm, d), lambda i: (i, 0)),
            pl.BlockSpec((1, d), lambda i: (0, 0)),
            pl.BlockSpec((d, MAIN_WIDTH), lambda i: (0, 0)),
            pl.BlockSpec((d, N_GATES), lambda i: (0, 0)),
            pl.BlockSpec((N_GATES, d), lambda i: (0, 0)),
            pl.BlockSpec((tm, RET_QK // 2), lambda i: (i % tiles_per_seq, 0)),
            pl.BlockSpec((tm, RET_QK // 2), lambda i: (i % tiles_per_seq, 0)),
            pl.BlockSpec((CONV_W, 2 * ML_QK), lambda i: (0, 0)),
            pl.BlockSpec((1, 2 * ML_QK), lambda i: (0, 0)),
        ],
        out_specs=[
            pl.BlockSpec((tm, MAIN_WIDTH), lambda i: (i, 0)),
            pl.BlockSpec((tm, N_GATES), lambda i: (i, 0)),
            pl.BlockSpec((N_GATES, tm), lambda i: (0, i)),
        ],
        out_shape=[
            jax.ShapeDtypeStruct((n, MAIN_WIDTH), BF16),
            jax.ShapeDtypeStruct((n, N_GATES), F32),
            jax.ShapeDtypeStruct((N_GATES, n), F32),
        ],
        scratch_shapes=[pltpu.VMEM((8, 2 * ML_QK), F32)],
        compiler_params=pltpu.CompilerParams(
            dimension_semantics=("arbitrary",), vmem_limit_bytes=VMEM_LIMIT),
        name="inproj",
    )(xf, norm_w, w_main, w_if, w_ift, cos, sin, conv_w, conv_b)


def _mixer_kernel(*refs):
    @pl.when(pl.program_id(1) == 0)
    def _():
        for state_ref in refs[-4:]:
            state_ref[...] = jnp.zeros_like(state_ref)

    for bi in range(refs[0].shape[0]):
        _mixer_one(bi, *refs)


def _mixer_one(bi, proj_ref, g_ref, gt_ref, qdec_ref, kdec_ref, dmat_ref,
               bmask_ref, cdec_ref, hmask_ref, tril_ref, triu_ref, ones_ref, retw_ref, mlw_ref, gbc_ref, gbr_ref,
               out_ref, r_ref, c_ref, n_ref, m_ref):
    L = CHUNK
    proj_ref, g_ref, gt_ref, out_ref = proj_ref.at[bi], g_ref.at[bi], gt_ref.at[bi], out_ref.at[bi]
    r_ref, c_ref, n_ref, m_ref = r_ref.at[bi], c_ref.at[bi], n_ref.at[bi], m_ref.at[bi]
    mean_w = ones_ref[...]


    q = proj_ref[:, OFF_RQ:OFF_RQ + RET_QK].astype(F32)
    k_b = proj_ref[:, OFF_RK:OFF_RK + RET_QK]
    k = k_b.astype(F32)
    v = proj_ref[:, OFF_RV:OFF_RV + RET_V]
    r_prev = r_ref[...]
    cross = _dot((q * qdec_ref[...]).astype(BF16), r_prev.astype(BF16))
    kv = _dot_tn((k * kdec_ref[...]).astype(BF16), v) * bmask_ref[...]
    r_ref[...] = cdec_ref[...] * r_prev + kv
    RH = range(RET_HEADS)
    sc = [_dot_nt((q * hmask_ref[h:h + 1, :]).astype(BF16), k_b) for h in RH]
    sc = [(sc[h] * dmat_ref[h]).astype(BF16) for h in RH]
    tot = [_dot(sc[h], v[:, h * RET_DV:(h + 1) * RET_DV]) + cross[:, h * RET_DV:(h + 1) * RET_DV] for h in RH]
    ret = jnp.concatenate(_head_norms(tot, mean_w), axis=1) * retw_ref[...]
    ret = ret * proj_ref[:, OFF_RG:OFF_RG + RET_V].astype(F32)
    out_ref[:, 0:RET_V] = ret.astype(BF16)

    mq = proj_ref[:, OFF_MQK:OFF_MQK + ML_QK]
    mk = proj_ref[:, OFF_MQK + ML_QK:OFF_MQK + 2 * ML_QK]
    mv = proj_ref[:, OFF_MV:OFF_MV + ML_V]

    gc = g_ref[...] + gbc_ref[...]
    gr = gt_ref[...] + gbr_ref[...]
    b_c = jnp.dot(tril_ref[...], _log_sigmoid(gc), preferred_element_type=F32, precision=HIGHEST)
    b_r = jnp.dot(_log_sigmoid(gr), triu_ref[...], preferred_element_type=F32, precision=HIGHEST)
    causal = (lax.broadcasted_iota(jnp.int32, (L, L), 0) >= lax.broadcasted_iota(jnp.int32, (L, L), 1))
    MH = range(ML_HEADS)
    bc = [b_c[:, ML_HEADS + h:ML_HEADS + h + 1] for h in MH]
    br = [b_r[ML_HEADS + h:ML_HEADS + h + 1, :] for h in MH]
    igc = [gc[:, h:h + 1] for h in MH]
    igr = [gr[h:h + 1, :] for h in MH]
    btot = [br[h][:, L - 1:L] for h in MH]
    qh_b = [mq[:, h * ML_DK:(h + 1) * ML_DK] for h in MH]
    kh_b = [mk[:, h * ML_DK:(h + 1) * ML_DK] for h in MH]
    vh = [mv[:, h * ML_DV:(h + 1) * ML_DV] for h in MH]
    c_prev = [c_ref[h] for h in MH]
    n_prev = [n_ref[h][0:1, :] for h in MH]
    m_prev = [m_ref[h][0:1, 0:1] for h in MH]
    s_raw = [_dot_nt(qh_b[h], kh_b[h]) for h in MH]
    qc = [_dot(qh_b[h], c_prev[h].astype(BF16)) for h in MH]
    log_d = [jnp.where(causal, bc[h] - br[h] + igr[h], -jnp.inf) for h in MH]
    m_intra = [jnp.max(log_d[h], axis=1, keepdims=True) for h in MH]
    m_loc = [jnp.max(btot[h] - br[h] + igr[h], axis=1, keepdims=True) for h in MH]
    kw = [kh_b[h].astype(F32) * jnp.exp(btot[h] - bc[h] + igc[h] - m_loc[h]) for h in MH]
    kv_loc = [_dot_tn(kw[h].astype(BF16), vh[h]) for h in MH]
    n_loc = [jnp.sum(kw[h], axis=0, keepdims=True) for h in MH]
    m_inter = [bc[h] + m_prev[h] for h in MH]
    m_t = [jnp.maximum(m_intra[h], m_inter[h]) for h in MH]
    s_mat = [s_raw[h] * jnp.exp(log_d[h] - m_t[h]) for h in MH]
    inter = [jnp.exp(m_inter[h] - m_t[h]) for h in MH]
    num = [_dot(s_mat[h].astype(BF16), vh[h]) + inter[h] * qc[h] for h in MH]
    den = [jnp.sum(s_mat[h], axis=1, keepdims=True)
           + inter[h] * jnp.sum(qh_b[h].astype(F32) * n_prev[h], axis=1, keepdims=True) for h in MH]
    hh = [num[h] / jnp.maximum(jnp.abs(den[h]), jnp.exp(-m_t[h])) for h in MH]
    for h in MH:
        m_new = jnp.maximum(btot[h] + m_prev[h], m_loc[h])
        s_old = jnp.exp(btot[h] + m_prev[h] - m_new)
        s_loc = jnp.exp(m_loc[h] - m_new)
        c_ref[h] = s_old * c_prev[h] + s_loc * kv_loc[h]
        n_ref[h] = jnp.broadcast_to(s_old * n_prev[h] + s_loc * n_loc[h], (8, ML_DK))
        m_ref[h] = jnp.broadcast_to(m_new, (8, 128))
    ml = jnp.concatenate(_head_norms(hh, mean_w), axis=1) * mlw_ref[...]
    ml = ml * proj_ref[:, OFF_MO:OFF_MO + ML_V].astype(F32)
    out_ref[:, RET_V:RET_V + ML_V] = ml.astype(BF16)


def _mixer_tables(seq):
    L = CHUNK
    half = RET_DK // 2
    inv = ROPE_BASE ** (-np.arange(half, dtype=np.float64) / half)
    ang = np.arange(seq, dtype=np.float64)[:, None] * inv[None, :].astype(np.float32).astype(np.float64)
    cos = np.tile(np.cos(ang), (1, RET_HEADS)).astype(np.float32)
    sin = np.tile(np.sin(ang), (1, RET_HEADS)).astype(np.float32)
    log_g = np.log1p(-np.exp2(-5.0 - np.arange(RET_HEADS, dtype=np.float64)))
    n = np.arange(L, dtype=np.float64)
    lane_head = (np.arange(RET_QK) % (RET_QK // 2)) // half
    qdec = np.exp((n + 1)[:, None] * log_g[lane_head][None, :]).astype(np.float32)
    kdec = np.exp((L - 1 - n)[:, None] * log_g[lane_head][None, :]).astype(np.float32)
    diff = n[:, None] - n[None, :]
    dmat = np.where(diff >= 0, np.exp(log_g[:, None, None] * np.maximum(diff, 0.0)[None]), 0.0).astype(np.float32)
    col_head = np.arange(RET_V) // RET_DV
    bmask = (lane_head[:, None] == col_head[None, :]).astype(np.float32)
    cdec = np.exp(L * log_g[col_head])[None, :].astype(np.float32)
    hmask = (lane_head[None, :] == np.arange(RET_HEADS)[:, None]).astype(np.float32)
    hmask = np.concatenate([hmask, np.zeros((8 - RET_HEADS, RET_QK), np.float32)], axis=0)
    tril = np.tril(np.ones((L, L), np.float32))
    ones = np.full((RET_DV, RET_DV), 1.0 / RET_DV, np.float32)
    return dict(cos=cos, sin=sin, qdec=qdec, kdec=kdec, dmat=dmat, bmask=bmask, cdec=cdec, hmask=hmask,
                tril=tril, triu=np.ascontiguousarray(tril.T), ones=ones)


def _mixer(proj, g, gt, tabs, ret_norm_w, ml_norm_w, gate_b, batch, seq):
    L = CHUNK
    nc = seq // L
    n = batch * seq
    nb = MIXER_BATCHES if batch % MIXER_BATCHES == 0 else 1
    proj = proj.reshape(batch, seq, MAIN_WIDTH)
    g = g.reshape(batch, seq, N_GATES)
    gt = gt.reshape(N_GATES, batch, seq).transpose(1, 0, 2)
    const2 = lambda b, c: (0, 0)
    const3 = lambda b, c: (0, 0, 0)
    tok = lambda b, c: (b, c, 0)
    in_specs = [
        pl.BlockSpec((nb, L, MAIN_WIDTH), tok),
        pl.BlockSpec((nb, L, N_GATES), tok),
        pl.BlockSpec((nb, N_GATES, L), lambda b, c: (b, 0, c)),
        pl.BlockSpec((L, RET_QK), const2),
        pl.BlockSpec((L, RET_QK), const2),
        pl.BlockSpec((RET_HEADS, L, L), const3),
        pl.BlockSpec((RET_QK, RET_V), const2),
        pl.BlockSpec((1, RET_V), const2),
        pl.BlockSpec((8, RET_QK), const2),
        pl.BlockSpec((L, L), const2),
        pl.BlockSpec((L, L), const2),
        pl.BlockSpec((RET_DV, RET_DV), const2),
        pl.BlockSpec((1, RET_V), const2),
        pl.BlockSpec((1, ML_V), const2),
        pl.BlockSpec((1, N_GATES), const2),
        pl.BlockSpec((N_GATES, 1), const2),
    ]
    return pl.pallas_call(
        _mixer_kernel,
        grid=(batch // nb, nc),
        in_specs=in_specs,
        out_specs=pl.BlockSpec((nb, L, RET_V + ML_V), tok),
        out_shape=jax.ShapeDtypeStruct((batch, seq, RET_V + ML_V), BF16),
        scratch_shapes=[
            pltpu.VMEM((nb, RET_QK, RET_V), F32),
            pltpu.VMEM((nb, ML_HEADS, ML_DK, ML_DV), F32),
            pltpu.VMEM((nb, ML_HEADS, 8, ML_DK), F32),
            pltpu.VMEM((nb, ML_HEADS, 8, 128), F32),
        ],
        compiler_params=pltpu.CompilerParams(
            dimension_semantics=("arbitrary", "arbitrary"), vmem_limit_bytes=VMEM_LIMIT),
        name="mixer",
    )(proj, g, gt, tabs["qdec"], tabs["kdec"], tabs["dmat"], tabs["bmask"],
      tabs["cdec"], tabs["hmask"], tabs["tril"], tabs["triu"], tabs["ones"].astype(BF16), ret_norm_w, ml_norm_w,
      gate_b.reshape(1, N_GATES), gate_b.reshape(N_GATES, 1)).reshape(n, RET_V + ML_V)


def _memkv_kernel(mem_ref, nw_ref, wkv_ref, k_ref, v_ref):
    d = mem_ref.shape[-1]
    mn = _rms(mem_ref[0], nw_ref[...]).astype(BF16)
    k_ref[0] = _dot(mn, wkv_ref[:, :d]).astype(BF16)
    v_ref[0] = _dot(mn, wkv_ref[:, d:]).astype(BF16)


def _memkv(mem, norm_w, wkv):
    b, m, d = mem.shape
    return pl.pallas_call(
        _memkv_kernel,
        grid=(b,),
        in_specs=[
            pl.BlockSpec((1, m, d), lambda i: (i, 0, 0)),
            pl.BlockSpec((1, d), lambda i: (0, 0)),
            pl.BlockSpec((d, 2 * d), lambda i: (0, 0)),
        ],
        out_specs=[pl.BlockSpec((1, m, d), lambda i: (i, 0, 0))] * 2,
        out_shape=[jax.ShapeDtypeStruct((b, m, d), BF16)] * 2,
        compiler_params=pltpu.CompilerParams(
            dimension_semantics=("arbitrary",), vmem_limit_bytes=VMEM_LIMIT),
        name="memkv",
    )(mem, norm_w, wkv)


def _attn_route_kernel(x_ref, mix_ref, k_ref, v_ref, wout_ref, nxa_ref, wq_ref, wo_ref, nmoe_ref,
                       wr_ref, wrlo_ref, br_ref, sut_ref,
                       x2_ref, h3_ref, ri_ref, rw_ref, cnt_ref, carry_ref):
    tm, d = x_ref.shape
    dh = d // XA_HEADS

    @pl.when((pl.program_id(0) % (pl.num_programs(0) // MOE_PARTS) == 0) & (pl.program_id(1) == 0))
    def _():
        carry_ref[...] = jnp.zeros_like(carry_ref)

    groups = [slice(g * (tm // ATTN_GROUPS), (g + 1) * (tm // ATTN_GROUPS)) for g in range(ATTN_GROUPS)]
    x1 = [x_ref[s, :] + _dot(mix_ref[s, :], wout_ref[...]) for s in groups]
    h2 = [_rms(t, nxa_ref[...]).astype(BF16) for t in x1]
    q = [_dot(t, wq_ref[...]).astype(BF16) for t in h2]
    o = []
    for qg in q:
        heads = []
        for h in range(XA_HEADS):
            logits = _dot_nt(qg[:, h * dh:(h + 1) * dh], k_ref[0, :, h * dh:(h + 1) * dh]) * (dh ** -0.5)
            mx = jnp.max(logits, axis=-1, keepdims=True)
            e = jnp.exp(logits - mx)
            p = (e / jnp.sum(e, axis=-1, keepdims=True)).astype(BF16)
            heads.append(_dot(p, v_ref[0, :, h * dh:(h + 1) * dh]).astype(BF16))
        o.append(jnp.concatenate(heads, axis=1))
    x2 = [a + _dot(b, wo_ref[...]) for a, b in zip(x1, o)]
    for s, t in zip(groups, x2):
        x2_ref[s, :] = t
    h3 = [_rms(t, nmoe_ref[...]) for t in x2]
    for s, t in zip(groups, h3):
        h3_ref[s, 0, :] = _pack_rows(t[:, :d // 2], t[:, d // 2:])

    lts = []
    for t in h3:
        t_hi = t.astype(BF16)
        t_lo = (t - t_hi.astype(F32)).astype(BF16)
        lts.append(_dot_nt(wr_ref[...], t_hi) + (_dot_nt(wr_ref[...], t_lo) + _dot_nt(wrlo_ref[...], t_hi)))
    lt = jnp.concatenate(lts, axis=1) + br_ref[...]
    gl = lt[N_EXPERTS:N_EXPERTS + N_GROUPS]
    gmax = jnp.max(gl, axis=0, keepdims=True)
    g_w = 1.0 / jnp.sum(jnp.exp(gl - gmax), axis=0, keepdims=True)
    giota = lax.broadcasted_iota(jnp.int32, gl.shape, 0)
    g_sel = jnp.min(jnp.where(gl == gmax, giota, N_GROUPS), axis=0, keepdims=True)
    el = lt[0:N_EXPERTS]
    eiota = lax.broadcasted_iota(jnp.int32, el.shape, 0)
    in_grp = (eiota // EXP_PER_GROUP) == g_sel
    elm = jnp.where(in_grp, el, -jnp.inf)
    m1 = jnp.max(elm, axis=0, keepdims=True)
    esum = jnp.sum(jnp.where(in_grp, jnp.exp(el - m1), 0.0), axis=0, keepdims=True)
    i1 = jnp.min(jnp.where(elm == m1, eiota, N_EXPERTS), axis=0, keepdims=True)
    elm2 = jnp.where(eiota == i1, -jnp.inf, elm)
    m2 = jnp.max(elm2, axis=0, keepdims=True)
    i2 = jnp.min(jnp.where(elm2 == m2, eiota, N_EXPERTS), axis=0, keepdims=True)
    p1 = 1.0 / esum
    p2 = jnp.exp(m2 - m1) / esum
    psum = p1 + p2
    w1 = g_w * (p1 / psum)
    w2 = g_w * (p2 / psum)

    oh1 = (eiota == i1).astype(F32)
    oh2 = (eiota == i2).astype(F32)
    cnt = oh1 + oh2
    base = carry_ref[:, 0:1] + _dot(cnt.astype(BF16), sut_ref[...])
    r1 = jnp.sum(oh1 * base, axis=0, keepdims=True)
    r2 = jnp.sum(oh2 * base, axis=0, keepdims=True)
    new_carry = carry_ref[...] + jnp.sum(cnt, axis=1, keepdims=True)
    carry_ref[...] = new_carry
    cnt_ref[...] = new_carry

    zi = jnp.zeros((4, tm), jnp.int32)
    ri_ref[...] = jnp.concatenate([i1, i2, r1.astype(jnp.int32), r2.astype(jnp.int32), zi], axis=0)
    rw_ref[...] = jnp.concatenate([w1, w2, jnp.zeros((6, tm), F32)], axis=0)


def _attn_route(xf, mixed, kmem, vmem, w_out, norm_xa_w, wq, wo, norm_moe_w, w_route_t, b_route, sut,
                batch, seq):
    n, d = xf.shape
    w_route_hi = w_route_t.astype(BF16)
    w_route_lo = (w_route_t - w_route_hi.astype(F32)).astype(BF16)
    tm = TOKEN_TILE
    nt = seq // tm
    m = kmem.shape[1]
    tok = lambda b, t: (b * nt + t, 0)
    lane_tok = lambda b, t: (0, b * nt + t)
    const2 = lambda b, t: (0, 0)
    return pl.pallas_call(
        _attn_route_kernel,
        grid=(batch, nt),
        in_specs=[
            pl.BlockSpec((tm, d), tok),
            pl.BlockSpec((tm, d), tok),
            pl.BlockSpec((1, m, d), lambda b, t: (b, 0, 0)),
            pl.BlockSpec((1, m, d), lambda b, t: (b, 0, 0)),
            pl.BlockSpec((d, d), const2),
            pl.BlockSpec((1, d), const2),
            pl.BlockSpec((d, d), const2),
            pl.BlockSpec((d, d), const2),
            pl.BlockSpec((1, d), const2),
            pl.BlockSpec((ROUTE_ROWS, d), const2),
            pl.BlockSpec((ROUTE_ROWS, d), const2),
            pl.BlockSpec((ROUTE_ROWS, 1), const2),
            pl.BlockSpec((tm, tm), const2),
        ],
        out_specs=[
            pl.BlockSpec((tm, d), tok),
            pl.BlockSpec((tm, 1, d // 2), lambda b, t: (b * nt + t, 0, 0)),
            pl.BlockSpec((8, tm), lane_tok),
            pl.BlockSpec((8, tm), lane_tok),
            pl.BlockSpec((N_EXPERTS, 128), lambda b, t: (b // (batch // MOE_PARTS), 0)),
        ],
        out_shape=[
            jax.ShapeDtypeStruct((n, d), F32),
            jax.ShapeDtypeStruct((n, 1, d // 2), jnp.uint32),
            jax.ShapeDtypeStruct((8, n), jnp.int32),
            jax.ShapeDtypeStruct((8, n), F32),
            jax.ShapeDtypeStruct((MOE_PARTS * N_EXPERTS, 128), F32),
        ],
        scratch_shapes=[pltpu.VMEM((N_EXPERTS, 128), F32)],
        compiler_params=pltpu.CompilerParams(
            dimension_semantics=("arbitrary", "arbitrary"), vmem_limit_bytes=VMEM_LIMIT),
        name="attn_route",
    )(xf, mixed, kmem, vmem, w_out, norm_xa_w, wq, wo, norm_moe_w, w_route_hi, w_route_lo, b_route, sut)


def _dispatch_kernel(zpos_ref, dest_ref, h_ref, xs_ref, idx_ref, idx_sem, row_sem, zero_ref, zero_sem):
    i = pl.program_id(0)
    nsteps = pl.num_programs(0)
    td = h_ref.shape[0]
    bm = zero_ref.shape[0]
    slot = i % 2

    def idx_copy(step, sl):
        off = pl.multiple_of(sl * (2 * td), 2 * td)
        return pltpu.make_async_copy(dest_ref.at[step], idx_ref.at[pl.ds(off, 2 * td)], idx_sem.at[sl])

    @pl.when(i == 0)
    def _():
        zero_ref[...] = jnp.zeros_like(zero_ref)

        def zero_copy(e):
            return pltpu.make_async_copy(zero_ref, xs_ref.at[pl.ds(pl.multiple_of(zpos_ref[e], bm), bm), 0], zero_sem)

        def tail_copy(b):
            return pltpu.make_async_copy(zero_ref, xs_ref.at[pl.ds(pl.multiple_of(b * bm, bm), bm), 0], zero_sem)

        nused = zpos_ref[N_EXPERTS]
        nblk = xs_ref.shape[0] // bm
        for e in range(N_EXPERTS):
            pl.when(zpos_ref[e] >= 0)(lambda e=e: zero_copy(e).start())
        lax.fori_loop(nused, nblk, lambda b, c: (tail_copy(b).start(), c)[1], 0)
        for e in range(N_EXPERTS):
            pl.when(zpos_ref[e] >= 0)(lambda e=e: zero_copy(e).wait())
        lax.fori_loop(nused, nblk, lambda b, c: (tail_copy(b).wait(), c)[1], 0)
        idx_copy(0, 0).start()

    idx_copy(i, slot).wait()

    @pl.when(i + 1 < nsteps)
    def _():
        idx_copy(i + 1, 1 - slot).start()

    base = slot * (2 * td)

    for t in range(td):
        for k in range(TOP_K):
            pltpu.make_async_copy(h_ref.at[t], xs_ref.at[idx_ref[base + k * td + t]], row_sem).start(priority=t % 2)
    for _ in range(TOP_K):
        pltpu.make_async_copy(xs_ref.at[pl.ds(0, td)], xs_ref.at[pl.ds(0, td)], row_sem).wait()


def _dispatch(h3p, dest_tiles, zpos, cap, tile0):
    w = h3p.shape[-1]
    nt, td2 = dest_tiles.shape
    td = td2 // 2
    grid_spec = pltpu.PrefetchScalarGridSpec(
        num_scalar_prefetch=1,
        grid=(nt,),
        in_specs=[
            pl.BlockSpec(memory_space=pl.ANY),
            pl.BlockSpec((td, 1, w), lambda i, zp: (i + tile0, 0, 0)),
        ],
        out_specs=pl.BlockSpec(memory_space=pl.ANY),
        scratch_shapes=[
            pltpu.SMEM((2 * td2,), jnp.int32),
            pltpu.SemaphoreType.DMA((2,)),
            pltpu.SemaphoreType.DMA,
            pltpu.VMEM((MOE_ROWS, w), jnp.uint32),
            pltpu.SemaphoreType.DMA,
        ],
    )
    return pl.pallas_call(
        _dispatch_kernel,
        grid_spec=grid_spec,
        out_shape=jax.ShapeDtypeStruct((cap, 1, w), jnp.uint32),
        compiler_params=pltpu.CompilerParams(
            dimension_semantics=("arbitrary",), vmem_limit_bytes=VMEM_LIMIT),
        name="dispatch",
    )(zpos, dest_tiles, h3p)


def _expert_kernel(side, *refs):
    if side is None:
        (blk_e_ref, nused_ref, xs_ref, wg_ref, wu_ref, wd_ref, ys_ref, wg_b, wu_b, wd_b,
         xbuf, ybuf, zbuf, in_sem, out_sem, zero_sem) = refs
    else:
        (blk_e_ref, nused_ref, zpos_ref, xs_ref, wg_ref, wu_ref, wd_ref, sdest_ref, h3_ref, ys_ref, xs2_ref,
         wg_b, wu_b, wd_b, xbuf, ybuf, zbuf, in_sem, out_sem, zero_sem, sidx_ref, sidx_sem, side_sem) = refs
        side_tok0, ts, n_side = side
    i = pl.program_id(0)
    nsteps = pl.num_programs(0)
    nused = nused_ref[0]
    bm = xbuf.shape[1]
    slot = i % 2
    prev = blk_e_ref[jnp.maximum(i - 1, 0)]
    fresh = (i == 0) | (blk_e_ref[i] != prev)
    half = wd_b.shape[1] // 2

    def side_tile(step):
        return jnp.minimum(step, n_side - 1)

    def sidx_copy(step):
        sl = step % 2
        off = pl.multiple_of(sl * (2 * ts), 2 * ts)
        return pltpu.make_async_copy(sdest_ref.at[side_tile(step)], sidx_ref.at[pl.ds(off, 2 * ts)], sidx_sem.at[sl])

    def side_rows_wait(step):
        for _ in range(TOP_K):
            pltpu.make_async_copy(xs2_ref.at[pl.ds(0, ts)], xs2_ref.at[pl.ds(0, ts)], side_sem.at[step % 2]).wait()

    def side_issue(step):
        base = (step % 2) * (2 * ts)
        tok = side_tok0 + side_tile(step) * ts
        for t in range(ts):
            for k in range(TOP_K):
                pltpu.make_async_copy(h3_ref.at[tok + t], xs2_ref.at[sidx_ref[base + k * ts + t]],
                                      side_sem.at[step % 2]).start()

    def side_drain():
        sidx_copy(nused).wait()
        side_rows_wait(nused - 1)

    def rows(ref, step):
        return ref.at[pl.ds(pl.multiple_of(step * bm, bm), bm), 0]

    def in_copy(step, sl):
        return pltpu.make_async_copy(rows(xs_ref, step), xbuf.at[sl], in_sem.at[sl])

    def out_copy(step, sl):
        return pltpu.make_async_copy(ybuf.at[sl], rows(ys_ref, step), out_sem.at[sl])

    def zero_copy(step):
        return pltpu.make_async_copy(zbuf, rows(ys_ref, step), zero_sem)

    @pl.when(i == 0)
    def _():
        zbuf[...] = jnp.zeros_like(zbuf)
        in_copy(0, 0).start()
        if side is not None:
            def zfill(start):
                return pltpu.make_async_copy(zbuf, xs2_ref.at[pl.ds(pl.multiple_of(start, bm), bm), 0], zero_sem)

            n2 = zpos_ref[N_EXPERTS]
            nblk2 = xs2_ref.shape[0] // bm
            for e in range(N_EXPERTS):
                pl.when(zpos_ref[e] >= 0)(lambda e=e: zfill(zpos_ref[e]).start())
            lax.fori_loop(n2, nblk2, lambda b, c: (zfill(b * bm).start(), c)[1], 0)
            for e in range(N_EXPERTS):
                pl.when(zpos_ref[e] >= 0)(lambda e=e: zfill(zpos_ref[e]).wait())
            lax.fori_loop(n2, nblk2, lambda b, c: (zfill(b * bm).wait(), c)[1], 0)
            sidx_copy(0).start()

    @pl.when(i + 1 < nused)
    def _():
        in_copy(i + 1, 1 - slot).start()

    @pl.when(fresh)
    def _():
        wg_b[...] = wg_ref[0].astype(BF16)
        wu_b[...] = wu_ref[0].astype(BF16)
        wd_b[...] = wd_ref[0].astype(BF16)

    @pl.when(i < nused)
    def _():
        in_copy(i, slot).wait()
        pl.when(i >= 2)(lambda: out_copy(i - 2, slot).wait())
        if side is not None:
            pl.when(i >= 1)(lambda: side_rows_wait(i - 1))
            sidx_copy(i).wait()
            sidx_copy(i + 1).start()
            side_issue(i)
        lo, hi = _unpack_rows(xbuf[slot])
        xb = jnp.concatenate([lo.astype(BF16), hi.astype(BF16)], axis=1)
        hid = (_silu(_dot(xb, wg_b[...])) * _dot(xb, wu_b[...])).astype(BF16)
        y = _dot(hid, wd_b[...])
        ybuf[slot] = _pack_rows(y[:, :half], y[:, half:])
        out_copy(i, slot).start()

    pl.when(i >= nused)(lambda: zero_copy(i).start())
    if side is not None:
        pl.when((i == nused) & (nused >= 1))(side_drain)

    @pl.when(i == nsteps - 1)
    def _():
        pl.when(nused >= 2)(lambda: out_copy(nused - 2, nused % 2).wait())
        pl.when(nused >= 1)(lambda: out_copy(nused - 1, (nused - 1) % 2).wait())
        lax.fori_loop(nused, nsteps, lambda b, c: (zero_copy(b).wait(), c)[1], 0)
        if side is not None:
            pl.when(nused == nsteps)(side_drain)


def _experts(xs, blk_e, nused, w_gate, w_up, w_down, side=None):
    cap, _, w = xs.shape
    _, d, de = w_gate.shape
    bm = MOE_ROWS
    n_pre = 2 if side is None else 3
    wspec = lambda blk: pl.BlockSpec(blk, lambda i, be, *_: (be[i], 0, 0))
    in_specs = [pl.BlockSpec(memory_space=pl.ANY), wspec((1, d, de)), wspec((1, d, de)), wspec((1, de, d))]
    out_specs = pl.BlockSpec(memory_space=pl.ANY)
    out_shape = jax.ShapeDtypeStruct((cap, 1, w), jnp.uint32)
    scratch = [
        pltpu.VMEM((d, de), BF16),
        pltpu.VMEM((d, de), BF16),
        pltpu.VMEM((de, d), BF16),
        pltpu.VMEM((2, bm, w), jnp.uint32),
        pltpu.VMEM((2, bm, w), jnp.uint32),
        pltpu.VMEM((bm, w), jnp.uint32),
        pltpu.SemaphoreType.DMA((2,)),
        pltpu.SemaphoreType.DMA((2,)),
        pltpu.SemaphoreType.DMA,
    ]
    args = [blk_e, nused]
    static = None
    if side is not None:
        h3p, dest_tiles, zpos, tok0, cap2 = side
        n_side, ts2 = dest_tiles.shape
        assert ts2 == bm and n_side * bm <= cap - N_EXPERTS * bm
        static = (tok0, ts2 // TOP_K, n_side)
        in_specs += [pl.BlockSpec(memory_space=pl.ANY), pl.BlockSpec(memory_space=pl.ANY)]
        out_specs = [out_specs, pl.BlockSpec(memory_space=pl.ANY)]
        out_shape = [out_shape, jax.ShapeDtypeStruct((cap2, 1, w), jnp.uint32)]
        scratch += [pltpu.SMEM((2 * ts2,), jnp.int32), pltpu.SemaphoreType.DMA((2,)), pltpu.SemaphoreType.DMA((2,))]
        args.append(zpos)
    args += [xs, w_gate, w_up, w_down]
    if side is not None:
        args += [dest_tiles, h3p]
    grid_spec = pltpu.PrefetchScalarGridSpec(
        num_scalar_prefetch=n_pre, grid=(cap // bm,), in_specs=in_specs, out_specs=out_specs,
        scratch_shapes=scratch)
    return pl.pallas_call(
        functools.partial(_expert_kernel, static),
        grid_spec=grid_spec,
        out_shape=out_shape,
        compiler_params=pltpu.CompilerParams(
            dimension_semantics=("arbitrary",), vmem_limit_bytes=VMEM_LIMIT),
        name="experts" if side is None else "experts_dispatch",
    )(*args)


def _combine_kernel(n_parts, dest_ref, *refs):
    ys_refs = refs[:n_parts]
    x2_ref, rw_ref, eye_ref, nw_ref, o_ref, idx_ref, idx_sem, ybuf, ysem = refs[n_parts:]
    i = pl.program_id(0)
    nsteps = pl.num_programs(0)
    tc, d = x2_ref.shape
    half = d // 2
    n_idx = 2 * tc
    tiles_per_part = nsteps // n_parts

    def idx_copy(step):
        sl = step % 3
        off = pl.multiple_of(sl * n_idx, n_idx)
        return pltpu.make_async_copy(dest_ref.at[step], idx_ref.at[pl.ds(off, n_idx)], idx_sem.at[sl])

    def gather(step):
        base = (step % 3) * n_idx
        buf = ybuf.at[step % 2]
        sem = ysem.at[step % 2]

        def issue(ys_ref):
            for t in range(n_idx):
                pltpu.make_async_copy(ys_ref.at[idx_ref[base + t]], buf.at[pl.ds(t, 1)], sem).start(priority=t % 2)

        for p, ys_ref in enumerate(ys_refs):
            pl.when(step // tiles_per_part == p)(functools.partial(issue, ys_ref))

    @pl.when(i == 0)
    def _():
        idx_copy(0).start()
        idx_copy(0).wait()
        gather(0)

        @pl.when(nsteps > 1)
        def _():
            idx_copy(1).start()

    @pl.when(i + 1 < nsteps)
    def _():
        idx_copy(i + 1).wait()

        @pl.when(i + 2 < nsteps)
        def _():
            idx_copy(i + 2).start()

        gather(i + 1)

    slot = i % 2
    pltpu.make_async_copy(ybuf.at[slot], ybuf.at[slot], ysem.at[slot]).wait()
    wcol = _dot_nt(eye_ref[...], rw_ref[...], precision=HIGHEST)
    lo1, hi1 = _unpack_rows(ybuf[slot, 0:tc])
    lo2, hi2 = _unpack_rows(ybuf[slot, tc:n_idx])
    w1 = wcol[:, 0:1]
    w2 = wcol[:, 1:2]
    z_lo = x2_ref[:, :half] + (lo1 * w1 + lo2 * w2)
    z_hi = x2_ref[:, half:] + (hi1 * w1 + hi2 * w2)
    ms = (jnp.sum(z_lo * z_lo, axis=-1, keepdims=True) + jnp.sum(z_hi * z_hi, axis=-1, keepdims=True)) / d
    scale = lax.rsqrt(ms + EPS)
    o_ref[:, :half] = z_lo * scale * nw_ref[:, :half]
    o_ref[:, half:] = z_hi * scale * nw_ref[:, half:]


def _combine(x2, ys_parts, dest_tiles, rw, eye, norm_w):
    n, d = x2.shape
    nt, n_idx = dest_tiles.shape
    tc = n_idx // 2
    w = ys_parts[0].shape[-1]
    assert nt % len(ys_parts) == 0
    return pl.pallas_call(
        functools.partial(_combine_kernel, len(ys_parts)),
        grid=(nt,),
        in_specs=[pl.BlockSpec(memory_space=pl.ANY)] * (1 + len(ys_parts)) + [
            pl.BlockSpec((tc, d), lambda i: (i, 0)),
            pl.BlockSpec((8, tc), lambda i: (0, i)),
            pl.BlockSpec((tc, tc), lambda i: (0, 0)),
            pl.BlockSpec((1, d), lambda i: (0, 0)),
        ],
        out_specs=pl.BlockSpec((tc, d), lambda i: (i, 0)),
        out_shape=jax.ShapeDtypeStruct((n, d), F32),
        scratch_shapes=[
            pltpu.SMEM((3 * n_idx,), jnp.int32),
            pltpu.SemaphoreType.DMA((3,)),
            pltpu.VMEM((2, n_idx, w), jnp.uint32),
            pltpu.SemaphoreType.DMA((2,)),
        ],
        compiler_params=pltpu.CompilerParams(
            dimension_semantics=("arbitrary",), vmem_limit_bytes=VMEM_LIMIT),
        name="combine",
    )(dest_tiles, *ys_parts, x2, rw, eye, norm_w)


def _rope_column_order():
    half = RET_DK // 2
    first = [h * RET_DK + j for h in range(RET_HEADS) for j in range(half)]
    second = [h * RET_DK + half + j for h in range(RET_HEADS) for j in range(half)]
    return np.array(first + second, dtype=np.int32)


def _layer(xf, mem, batch, seq, norm_mix_w, w_in, ret_norm_w, ml_conv_w, ml_conv_b, ml_gate_b, ml_norm_w,
           w_out, norm_xa_w, norm_mem_w, xa_wq, xa_wkv, xa_wo, norm_moe_w, moe_w_group, moe_b_group,
           moe_w_router, moe_b_router, moe_w_gate, moe_w_up, moe_w_down, final_norm_w):
    n, d = xf.shape
    perm = _rope_column_order()
    cols = np.concatenate([perm, RET_QK + perm, np.arange(2 * RET_QK, MAIN_WIDTH)])
    w_main = w_in[:, cols].astype(BF16)
    w_if = w_in[:, MAIN_WIDTH:].astype(BF16)
    tabs = {k_: jnp.asarray(v_) for k_, v_ in _mixer_tables(seq).items()}
    proj, g, gt = _inproj(xf, norm_mix_w.reshape(1, d), w_main, w_if, w_if.T, tabs["cos"], tabs["sin"], ml_conv_w,
                          ml_conv_b.reshape(1, 2 * ML_QK), seq)
    mixed = _mixer(proj, g, gt, tabs, ret_norm_w.reshape(1, RET_V), ml_norm_w.reshape(1, ML_V), ml_gate_b,
                   batch, seq)

    kmem, vmem = _memkv(mem, norm_mem_w.reshape(1, d), xa_wkv.astype(BF16))

    w_route_t = jnp.concatenate(
        [moe_w_router.T, moe_w_group.T, jnp.zeros((ROUTE_ROWS - N_EXPERTS - N_GROUPS, d), F32)], axis=0)
    b_route = jnp.concatenate(
        [moe_b_router, moe_b_group, jnp.zeros((ROUTE_ROWS - N_EXPERTS - N_GROUPS,), F32)]).reshape(ROUTE_ROWS, 1)
    tm = TOKEN_TILE
    sut = jnp.asarray(np.triu(np.ones((tm, tm), np.float32), 1), dtype=BF16)
    x2, h3, ri, rw, cnt = _attn_route(xf, mixed, kmem, vmem, w_out.astype(BF16), norm_xa_w.reshape(1, d),
                                      xa_wq.astype(BF16), xa_wo.astype(BF16), norm_moe_w.reshape(1, d),
                                      w_route_t, b_route, sut, batch, seq)

    bm = MOE_ROWS
    n_p = n // MOE_PARTS
    cap = n_p * TOP_K + N_EXPERTS * bm
    nblk = cap // bm
    blk_start = jnp.arange(nblk, dtype=jnp.int32) * bm
    parts = []
    for p in range(MOE_PARTS):
        counts = cnt[p * N_EXPERTS:(p + 1) * N_EXPERTS, 0].astype(jnp.int32)
        padded = (counts + bm - 1) // bm * bm
        pends = jnp.cumsum(padded)
        pstarts = pends - padded
        expert = ri[0:TOP_K, p * n_p:(p + 1) * n_p]
        onehot = expert[None] == jnp.arange(N_EXPERTS, dtype=jnp.int32)[:, None, None]
        dest = (jnp.sum(jnp.where(onehot, pstarts[:, None, None], 0), axis=0)
                + ri[TOP_K:2 * TOP_K, p * n_p:(p + 1) * n_p])
        blk_e = jnp.minimum(jnp.sum(blk_start[:, None] >= pends[None, :], axis=1), N_EXPERTS - 1).astype(jnp.int32)
        nused = (pends[-1] // bm).astype(jnp.int32).reshape(1)
        zpos = jnp.where(padded > counts, pends - bm, -1).astype(jnp.int32)
        parts.append(dict(dest=dest, blk_e=blk_e, nused=nused, zpos=jnp.concatenate([zpos, nused])))

    def tiles(dest, rows):
        m = dest.shape[1]
        return dest.reshape(TOP_K, m // rows, rows).transpose(1, 0, 2).reshape(m // rows, TOP_K * rows)

    xs = _dispatch(h3, tiles(parts[0]["dest"], DISPATCH_TILE), parts[0]["zpos"], cap, 0)
    ys_parts = []
    for p in range(MOE_PARTS):
        side = None
        if p + 1 < MOE_PARTS:
            nxt = parts[p + 1]
            side = (h3, tiles(nxt["dest"], bm // TOP_K), nxt["zpos"], (p + 1) * n_p, cap)
        res = _experts(xs, parts[p]["blk_e"], parts[p]["nused"], moe_w_gate, moe_w_up, moe_w_down, side)
        ys, xs = res if side is not None else (res, None)
        ys_parts.append(ys)
    eye = jnp.asarray(np.eye(tm, dtype=np.float32))
    dest_all = jnp.concatenate([pt["dest"] for pt in parts], axis=1)
    return _combine(x2, ys_parts, tiles(dest_all, tm), rw, eye, final_norm_w.reshape(1, d))


def kernel(x, mem, norm_mix_w, w_in, ret_norm_w, ml_conv_w, ml_conv_b, ml_gate_b, ml_norm_w, w_out, norm_xa_w, norm_mem_w, xa_wq, xa_wkv, xa_wo, norm_moe_w, moe_w_group, moe_b_group, moe_w_router, moe_b_router, moe_w_gate, moe_w_up, moe_w_down, norm_final_w):
    batch, seq, d = x.shape
    depth = w_in.shape[0]
    assert depth == 1, "the final norm is fused into the last layer's combine kernel"
    l = 0
    out = _layer(x.reshape(batch * seq, d), mem, batch, seq, norm_mix_w[l], w_in[l], ret_norm_w[l], ml_conv_w[l],
                 ml_conv_b[l], ml_gate_b[l], ml_norm_w[l], w_out[l], norm_xa_w[l], norm_mem_w[l], xa_wq[l],
                 xa_wkv[l], xa_wo[l], norm_moe_w[l], moe_w_group[l], moe_b_group[l], moe_w_router[l],
                 moe_b_router[l], moe_w_gate[l], moe_w_up[l], moe_w_down[l], norm_final_w)
    return out.reshape(batch, seq, d)
```

```python
import functools

import numpy as np
import jax
import jax.numpy as jnp
from jax import lax
from jax.experimental import pallas as pl
from jax.experimental.pallas import tpu as pltpu

F32 = jnp.float32
BF16 = jnp.bfloat16
HIGHEST = lax.Precision.HIGHEST

CHUNK = 128
RET_HEADS = 4
RET_DK = 64
RET_DV = 128
ML_HEADS = 4
ML_DK = 128
ML_DV = 128
CONV_W = 4
XA_HEADS = 4
N_GROUPS = 4
EXP_PER_GROUP = 8
N_EXPERTS = N_GROUPS * EXP_PER_GROUP
TOP_K = 2
ROPE_BASE = 10000.0
EPS = 1e-6

RET_QK = RET_HEADS * RET_DK
RET_V = RET_HEADS * RET_DV
ML_QK = ML_HEADS * ML_DK
ML_V = ML_HEADS * ML_DV
OFF_RQ = 0
OFF_RK = OFF_RQ + RET_QK
OFF_RV = OFF_RK + RET_QK
OFF_RG = OFF_RV + RET_V
OFF_MQK = OFF_RG + RET_V
OFF_MV = OFF_MQK + 2 * ML_QK
OFF_MO = OFF_MV + ML_V
MAIN_WIDTH = OFF_MO + ML_V
N_GATES = 2 * ML_HEADS

ROUTE_ROWS = 40
TOKEN_TILE = 512
MOE_ROWS = 512
DISPATCH_TILE = 1024
MIXER_BATCHES = 4
ATTN_GROUPS = 2
VMEM_LIMIT = 56 * 1024 * 1024


def _dot(a, b):
    return jnp.dot(a, b, preferred_element_type=F32)


def _dot_nt(a, b, precision=None):
    return lax.dot_general(a, b, (((1,), (1,)), ((), ())), preferred_element_type=F32, precision=precision)


def _dot_tn(a, b):
    return lax.dot_general(a, b, (((0,), (0,)), ((), ())), preferred_element_type=F32)


def _rms(x, w):
    return x * lax.rsqrt(jnp.mean(x * x, axis=-1, keepdims=True) + EPS) * w


def _sigmoid(x):
    return 1.0 / (1.0 + jnp.exp(-x))


def _silu(x):
    return x * _sigmoid(x)


def _log_sigmoid(x):
    return jnp.minimum(x, 0.0) - jnp.log1p(jnp.exp(-jnp.abs(x)))


def _head_norms(ts, mean_w):
    mu = [_dot(t.astype(BF16), mean_w) for t in ts]
    dl = [t - m for t, m in zip(ts, mu)]
    var = [_dot((d * d).astype(BF16), mean_w) for d in dl]
    return [d * lax.rsqrt(v + EPS) for d, v in zip(dl, var)]


def _pack_rows(lo, hi):
    def bits(t):
        return lax.bitcast_convert_type(t.astype(BF16), jnp.uint16).astype(jnp.uint32)
    return bits(lo) | (bits(hi) << 16)


def _unpack_rows(u):
    lo = lax.bitcast_convert_type(u << 16, F32)
    hi = lax.bitcast_convert_type(u & jnp.uint32(0xFFFF0000), F32)
    return lo, hi


def _inproj_kernel(tiles_per_seq, x_ref, nw_ref, w_ref, wif_ref, wift_ref, cos_ref, sin_ref, convw_ref, convb_ref,
                   proj_ref, g_ref, gt_ref, carry_ref):
    tm = x_ref.shape[0]

    @pl.when(pl.program_id(0) == 0)
    def _():
        carry_ref[...] = jnp.zeros_like(carry_ref)

    h = _rms(x_ref[...], nw_ref[...]).astype(BF16)

    def mm(off, width):
        return _dot(h, w_ref[:, off:off + width])

    def rotary(qk):
        cos = cos_ref[...]
        sin = sin_ref[...]
        half = RET_QK // 2
        for off, scale in ((OFF_RQ, None), (OFF_RK, RET_DK ** -0.5)):
            t1 = qk[:, off - OFF_RQ:off - OFF_RQ + half]
            t2 = qk[:, off - OFF_RQ + half:off - OFF_RQ + 2 * half]
            r1 = t1 * cos - t2 * sin
            r2 = t1 * sin + t2 * cos
            if scale is not None:
                r1, r2 = r1 * scale, r2 * scale
            proj_ref[:, off:off + half] = r1.astype(BF16)
            proj_ref[:, off + half:off + 2 * half] = r2.astype(BF16)

    def conv_silu(part, scale, cur):
        c0 = part * ML_QK
        first = (pl.program_id(0) % tiles_per_seq) == 0
        row8 = lax.broadcasted_iota(jnp.int32, (8, ML_QK), 0)
        prev = jnp.where(first, 0.0, carry_ref[:, c0:c0 + ML_QK])
        acc = cur * convw_ref[CONV_W - 1:CONV_W, c0:c0 + ML_QK] + convb_ref[:, c0:c0 + ML_QK]
        for s in range(1, CONV_W):
            rolled = pltpu.roll(cur, s, 0)
            head8 = jnp.where(row8 < s, pltpu.roll(prev, s, 0), rolled[0:8])
            shifted = jnp.concatenate([head8, rolled[8:]], axis=0)
            acc = acc + shifted * convw_ref[CONV_W - 1 - s:CONV_W - s, c0:c0 + ML_QK]
        carry_ref[:, c0:c0 + ML_QK] = cur[tm - 8:tm]
        act = _silu(acc) if scale is None else _silu(acc) * scale
        proj_ref[:, OFF_MQK + c0:OFF_MQK + c0 + ML_QK] = act.astype(BF16)

    def store(off, width, fn=None):
        def ep(t):
            proj_ref[:, off:off + width] = (t if fn is None else fn(t)).astype(BF16)
        return ep

    rotary(mm(OFF_RQ, 2 * RET_QK))
    store(OFF_RV, RET_V)(mm(OFF_RV, RET_V))
    store(OFF_RG, RET_V, _silu)(mm(OFF_RG, RET_V))
    conv_silu(0, None, mm(OFF_MQK, ML_QK))
    conv_silu(1, ML_DK ** -0.5, mm(OFF_MQK + ML_QK, ML_QK))
    store(OFF_MV, ML_V)(mm(OFF_MV, ML_V))
    store(OFF_MO, ML_V, _sigmoid)(mm(OFF_MO, ML_V))
    g_ref[...] = _dot(h, wif_ref[...])
    gt_ref[...] = _dot_nt(wift_ref[...], h)


def _inproj(xf, norm_w, w_main, w_if, w_ift, cos, sin, conv_w, conv_b, seq):
    n, d = xf.shape
    tm = TOKEN_TILE
    tiles_per_seq = seq // tm
    return pl.pallas_call(
        functools.partial(_inproj_kernel, tiles_per_seq),
        grid=(n // tm,),
        in_specs=[
            pl.BlockSpec((tm, d), lambda i: (i, 0)),
            pl.BlockSpec((1, d), lambda i: (0, 0)),
            pl.BlockSpec((d, MAIN_WIDTH), lambda i: (0, 0)),
            pl.BlockSpec((d, N_GATES), lambda i: (0, 0)),
            pl.BlockSpec((N_GATES, d), lambda i: (0, 0)),
            pl.BlockSpec((tm, RET_QK // 2), lambda i: (i % tiles_per_seq, 0)),
            pl.BlockSpec((tm, RET_QK // 2), lambda i: (i % tiles_per_seq, 0)),
            pl.BlockSpec((CONV_W, 2 * ML_QK), lambda i: (0, 0)),
            pl.BlockSpec((1, 2 * ML_QK), lambda i: (0, 0)),
        ],
        out_specs=[
            pl.BlockSpec((tm, MAIN_WIDTH), lambda i: (i, 0)),
            pl.BlockSpec((tm, N_GATES), lambda i: (i, 0)),
            pl.BlockSpec((N_GATES, tm), lambda i: (0, i)),
        ],
        out_shape=[
            jax.ShapeDtypeStruct((n, MAIN_WIDTH), BF16),
            jax.ShapeDtypeStruct((n, N_GATES), F32),
            jax.ShapeDtypeStruct((N_GATES, n), F32),
        ],
        scratch_shapes=[pltpu.VMEM((8, 2 * ML_QK), F32)],
        compiler_params=pltpu.CompilerParams(
            dimension_semantics=("arbitrary",), vmem_limit_bytes=VMEM_LIMIT),
        name="inproj",
    )(xf, norm_w, w_main, w_if, w_ift, cos, sin, conv_w, conv_b)


def _mixer_kernel(*refs):
    @pl.when(pl.program_id(1) == 0)
    def _():
        for state_ref in refs[-4:]:
            state_ref[...] = jnp.zeros_like(state_ref)

    for bi in range(refs[0].shape[0]):
        _mixer_one(bi, *refs)


def _mixer_one(bi, proj_ref, g_ref, gt_ref, qdec_ref, kdec_ref, dmat_ref,
               bmask_ref, cdec_ref, hmask_ref, tril_ref, triu_ref, ones_ref, retw_ref, mlw_ref, gbc_ref, gbr_ref,
               out_ref, r_ref, c_ref, n_ref, m_ref):
    L = CHUNK
    proj_ref, g_ref, gt_ref, out_ref = proj_ref.at[bi], g_ref.at[bi], gt_ref.at[bi], out_ref.at[bi]
    r_ref, c_ref, n_ref, m_ref = r_ref.at[bi], c_ref.at[bi], n_ref.at[bi], m_ref.at[bi]
    mean_w = ones_ref[...]


    q = proj_ref[:, OFF_RQ:OFF_RQ + RET_QK].astype(F32)
    k_b = proj_ref[:, OFF_RK:OFF_RK + RET_QK]
    k = k_b.astype(F32)
    v = proj_ref[:, OFF_RV:OFF_RV + RET_V]
    r_prev = r_ref[...]
    cross = _dot((q * qdec_ref[...]).astype(BF16), r_prev.astype(BF16))
    kv = _dot_tn((k * kdec_ref[...]).astype(BF16), v) * bmask_ref[...]
    r_ref[...] = cdec_ref[...] * r_prev + kv
    RH = range(RET_HEADS)
    sc = [_dot_nt((q * hmask_ref[h:h + 1, :]).astype(BF16), k_b) for h in RH]
    sc = [(sc[h] * dmat_ref[h]).astype(BF16) for h in RH]
    tot = [_dot(sc[h], v[:, h * RET_DV:(h + 1) * RET_DV]) + cross[:, h * RET_DV:(h + 1) * RET_DV] for h in RH]
    ret = jnp.concatenate(_head_norms(tot, mean_w), axis=1) * retw_ref[...]
    ret = ret * proj_ref[:, OFF_RG:OFF_RG + RET_V].astype(F32)
    out_ref[:, 0:RET_V] = ret.astype(BF16)

    mq = proj_ref[:, OFF_MQK:OFF_MQK + ML_QK]
    mk = proj_ref[:, OFF_MQK + ML_QK:OFF_MQK + 2 * ML_QK]
    mv = proj_ref[:, OFF_MV:OFF_MV + ML_V]

    gc = g_ref[...] + gbc_ref[...]
    gr = gt_ref[...] + gbr_ref[...]
    b_c = jnp.dot(tril_ref[...], _log_sigmoid(gc), preferred_element_type=F32, precision=HIGHEST)
    b_r = jnp.dot(_log_sigmoid(gr), triu_ref[...], preferred_element_type=F32, precision=HIGHEST)
    causal = (lax.broadcasted_iota(jnp.int32, (L, L), 0) >= lax.broadcasted_iota(jnp.int32, (L, L), 1))
    MH = range(ML_HEADS)
    bc = [b_c[:, ML_HEADS + h:ML_HEADS + h + 1] for h in MH]
    br = [b_r[ML_HEADS + h:ML_HEADS + h + 1, :] for h in MH]
    igc = [gc[:, h:h + 1] for h in MH]
    igr = [gr[h:h + 1, :] for h in MH]
    btot = [br[h][:, L - 1:L] for h in MH]
    qh_b = [mq[:, h * ML_DK:(h + 1) * ML_DK] for h in MH]
    kh_b = [mk[:, h * ML_DK:(h + 1) * ML_DK] for h in MH]
    vh = [mv[:, h * ML_DV:(h + 1) * ML_DV] for h in MH]
    c_prev = [c_ref[h] for h in MH]
    n_prev = [n_ref[h][0:1, :] for h in MH]
    m_prev = [m_ref[h][0:1, 0:1] for h in MH]
    s_raw = [_dot_nt(qh_b[h], kh_b[h]) for h in MH]
    qc = [_dot(qh_b[h], c_prev[h].astype(BF16)) for h in MH]
    log_d = [jnp.where(causal, bc[h] - br[h] + igr[h], -jnp.inf) for h in MH]
    m_intra = [jnp.max(log_d[h], axis=1, keepdims=True) for h in MH]
    m_loc = [jnp.max(btot[h] - br[h] + igr[h], axis=1, keepdims=True) for h in MH]
    kw = [kh_b[h].astype(F32) * jnp.exp(btot[h] - bc[h] + igc[h] - m_loc[h]) for h in MH]
    kv_loc = [_dot_tn(kw[h].astype(BF16), vh[h]) for h in MH]
    n_loc = [jnp.sum(kw[h], axis=0, keepdims=True) for h in MH]
    m_inter = [bc[h] + m_prev[h] for h in MH]
    m_t = [jnp.maximum(m_intra[h], m_inter[h]) for h in MH]
    s_mat = [s_raw[h] * jnp.exp(log_d[h] - m_t[h]) for h in MH]
    inter = [jnp.exp(m_inter[h] - m_t[h]) for h in MH]
    num = [_dot(s_mat[h].astype(BF16), vh[h]) + inter[h] * qc[h] for h in MH]
    den = [jnp.sum(s_mat[h], axis=1, keepdims=True)
           + inter[h] * jnp.sum(qh_b[h].astype(F32) * n_prev[h], axis=1, keepdims=True) for h in MH]
    hh = [num[h] / jnp.maximum(jnp.abs(den[h]), jnp.exp(-m_t[h])) for h in MH]
    for h in MH:
        m_new = jnp.maximum(btot[h] + m_prev[h], m_loc[h])
        s_old = jnp.exp(btot[h] + m_prev[h] - m_new)
        s_loc = jnp.exp(m_loc[h] - m_new)
        c_ref[h] = s_old * c_prev[h] + s_loc * kv_loc[h]
        n_ref[h] = jnp.broadcast_to(s_old * n_prev[h] + s_loc * n_loc[h], (8, ML_DK))
        m_ref[h] = jnp.broadcast_to(m_new, (8, 128))
    ml = jnp.concatenate(_head_norms(hh, mean_w), axis=1) * mlw_ref[...]
    ml = ml * proj_ref[:, OFF_MO:OFF_MO + ML_V].astype(F32)
    out_ref[:, RET_V:RET_V + ML_V] = ml.astype(BF16)


def _mixer_tables(seq):
    L = CHUNK
    half = RET_DK // 2
    inv = ROPE_BASE ** (-np.arange(half, dtype=np.float64) / half)
    ang = np.arange(seq, dtype=np.float64)[:, None] * inv[None, :].astype(np.float32).astype(np.float64)
    cos = np.tile(np.cos(ang), (1, RET_HEADS)).astype(np.float32)
    sin = np.tile(np.sin(ang), (1, RET_HEADS)).astype(np.float32)
    log_g = np.log1p(-np.exp2(-5.0 - np.arange(RET_HEADS, dtype=np.float64)))
    n = np.arange(L, dtype=np.float64)
    lane_head = (np.arange(RET_QK) % (RET_QK // 2)) // half
    qdec = np.exp((n + 1)[:, None] * log_g[lane_head][None, :]).astype(np.float32)
    kdec = np.exp((L - 1 - n)[:, None] * log_g[lane_head][None, :]).astype(np.float32)
    diff = n[:, None] - n[None, :]
    dmat = np.where(diff >= 0, np.exp(log_g[:, None, None] * np.maximum(diff, 0.0)[None]), 0.0).astype(np.float32)
    col_head = np.arange(RET_V) // RET_DV
    bmask = (lane_head[:, None] == col_head[None, :]).astype(np.float32)
    cdec = np.exp(L * log_g[col_head])[None, :].astype(np.float32)
    hmask = (lane_head[None, :] == np.arange(RET_HEADS)[:, None]).astype(np.float32)
    hmask = np.concatenate([hmask, np.zeros((8 - RET_HEADS, RET_QK), np.float32)], axis=0)
    tril = np.tril(np.ones((L, L), np.float32))
    ones = np.full((RET_DV, RET_DV), 1.0 / RET_DV, np.float32)
    return dict(cos=cos, sin=sin, qdec=qdec, kdec=kdec, dmat=dmat, bmask=bmask, cdec=cdec, hmask=hmask,
                tril=tril, triu=np.ascontiguousarray(tril.T), ones=ones)


def _mixer(proj, g, gt, tabs, ret_norm_w, ml_norm_w, gate_b, batch, seq):
    L = CHUNK
    nc = seq // L
    n = batch * seq
    nb = MIXER_BATCHES if batch % MIXER_BATCHES == 0 else 1
    proj = proj.reshape(batch, seq, MAIN_WIDTH)
    g = g.reshape(batch, seq, N_GATES)
    gt = gt.reshape(N_GATES, batch, seq).transpose(1, 0, 2)
    const2 = lambda b, c: (0, 0)
    const3 = lambda b, c: (0, 0, 0)
    tok = lambda b, c: (b, c, 0)
    in_specs = [
        pl.BlockSpec((nb, L, MAIN_WIDTH), tok),
        pl.BlockSpec((nb, L, N_GATES), tok),
        pl.BlockSpec((nb, N_GATES, L), lambda b, c: (b, 0, c)),
        pl.BlockSpec((L, RET_QK), const2),
        pl.BlockSpec((L, RET_QK), const2),
        pl.BlockSpec((RET_HEADS, L, L), const3),
        pl.BlockSpec((RET_QK, RET_V), const2),
        pl.BlockSpec((1, RET_V), const2),
        pl.BlockSpec((8, RET_QK), const2),
        pl.BlockSpec((L, L), const2),
        pl.BlockSpec((L, L), const2),
        pl.BlockSpec((RET_DV, RET_DV), const2),
        pl.BlockSpec((1, RET_V), const2),
        pl.BlockSpec((1, ML_V), const2),
        pl.BlockSpec((1, N_GATES), const2),
        pl.BlockSpec((N_GATES, 1), const2),
    ]
    return pl.pallas_call(
        _mixer_kernel,
        grid=(batch // nb, nc),
        in_specs=in_specs,
        out_specs=pl.BlockSpec((nb, L, RET_V + ML_V), tok),
        out_shape=jax.ShapeDtypeStruct((batch, seq, RET_V + ML_V), BF16),
        scratch_shapes=[
            pltpu.VMEM((nb, RET_QK, RET_V), F32),
            pltpu.VMEM((nb, ML_HEADS, ML_DK, ML_DV), F32),
            pltpu.VMEM((nb, ML_HEADS, 8, ML_DK), F32),
            pltpu.VMEM((nb, ML_HEADS, 8, 128), F32),
        ],
        compiler_params=pltpu.CompilerParams(
            dimension_semantics=("arbitrary", "arbitrary"), vmem_limit_bytes=VMEM_LIMIT),
        name="mixer",
    )(proj, g, gt, tabs["qdec"], tabs["kdec"], tabs["dmat"], tabs["bmask"],
      tabs["cdec"], tabs["hmask"], tabs["tril"], tabs["triu"], tabs["ones"].astype(BF16), ret_norm_w, ml_norm_w,
      gate_b.reshape(1, N_GATES), gate_b.reshape(N_GATES, 1)).reshape(n, RET_V + ML_V)


def _memkv_kernel(mem_ref, nw_ref, wkv_ref, k_ref, v_ref):
    d = mem_ref.shape[-1]
    mn = _rms(mem_ref[0], nw_ref[...]).astype(BF16)
    k_ref[0] = _dot(mn, wkv_ref[:, :d]).astype(BF16)
    v_ref[0] = _dot(mn, wkv_ref[:, d:]).astype(BF16)


def _memkv(mem, norm_w, wkv):
    b, m, d = mem.shape
    return pl.pallas_call(
        _memkv_kernel,
        grid=(b,),
        in_specs=[
            pl.BlockSpec((1, m, d), lambda i: (i, 0, 0)),
            pl.BlockSpec((1, d), lambda i: (0, 0)),
            pl.BlockSpec((d, 2 * d), lambda i: (0, 0)),
        ],
        out_specs=[pl.BlockSpec((1, m, d), lambda i: (i, 0, 0))] * 2,
        out_shape=[jax.ShapeDtypeStruct((b, m, d), BF16)] * 2,
        compiler_params=pltpu.CompilerParams(
            dimension_semantics=("arbitrary",), vmem_limit_bytes=VMEM_LIMIT),
        name="memkv",
    )(mem, norm_w, wkv)


def _attn_route_kernel(x_ref, mix_ref, k_ref, v_ref, wout_ref, nxa_ref, wq_ref, wo_ref, nmoe_ref,
                       wr_ref, wrlo_ref, br_ref, sut_ref,
                       x2_ref, h3_ref, ri_ref, rw_ref, cnt_ref, carry_ref):
    tm, d = x_ref.shape
    dh = d // XA_HEADS

    @pl.when((pl.program_id(0) == 0) & (pl.program_id(1) == 0))
    def _():
        carry_ref[...] = jnp.zeros_like(carry_ref)

    groups = [slice(g * (tm // ATTN_GROUPS), (g + 1) * (tm // ATTN_GROUPS)) for g in range(ATTN_GROUPS)]
    x1 = [x_ref[s, :] + _dot(mix_ref[s, :], wout_ref[...]) for s in groups]
    h2 = [_rms(t, nxa_ref[...]).astype(BF16) for t in x1]
    q = [_dot(t, wq_ref[...]).astype(BF16) for t in h2]
    o = []
    for qg in q:
        heads = []
        for h in range(XA_HEADS):
            logits = _dot_nt(qg[:, h * dh:(h + 1) * dh], k_ref[0, :, h * dh:(h + 1) * dh]) * (dh ** -0.5)
            mx = jnp.max(logits, axis=-1, keepdims=True)
            e = jnp.exp(logits - mx)
            p = (e / jnp.sum(e, axis=-1, keepdims=True)).astype(BF16)
            heads.append(_dot(p, v_ref[0, :, h * dh:(h + 1) * dh]).astype(BF16))
        o.append(jnp.concatenate(heads, axis=1))
    x2 = [a + _dot(b, wo_ref[...]) for a, b in zip(x1, o)]
    for s, t in zip(groups, x2):
        x2_ref[s, :] = t
    h3 = [_rms(t, nmoe_ref[...]) for t in x2]
    for s, t in zip(groups, h3):
        h3_ref[s, 0, :] = _pack_rows(t[:, :d // 2], t[:, d // 2:])

    lts = []
    for t in h3:
        t_hi = t.astype(BF16)
        t_lo = (t - t_hi.astype(F32)).astype(BF16)
        lts.append(_dot_nt(wr_ref[...], t_hi) + (_dot_nt(wr_ref[...], t_lo) + _dot_nt(wrlo_ref[...], t_hi)))
    lt = jnp.concatenate(lts, axis=1) + br_ref[...]
    gl = lt[N_EXPERTS:N_EXPERTS + N_GROUPS]
    gmax = jnp.max(gl, axis=0, keepdims=True)
    g_w = 1.0 / jnp.sum(jnp.exp(gl - gmax), axis=0, keepdims=True)
    giota = lax.broadcasted_iota(jnp.int32, gl.shape, 0)
    g_sel = jnp.min(jnp.where(gl == gmax, giota, N_GROUPS), axis=0, keepdims=True)
    el = lt[0:N_EXPERTS]
    eiota = lax.broadcasted_iota(jnp.int32, el.shape, 0)
    in_grp = (eiota // EXP_PER_GROUP) == g_sel
    elm = jnp.where(in_grp, el, -jnp.inf)
    m1 = jnp.max(elm, axis=0, keepdims=True)
    esum = jnp.sum(jnp.where(in_grp, jnp.exp(el - m1), 0.0), axis=0, keepdims=True)
    i1 = jnp.min(jnp.where(elm == m1, eiota, N_EXPERTS), axis=0, keepdims=True)
    elm2 = jnp.where(eiota == i1, -jnp.inf, elm)
    m2 = jnp.max(elm2, axis=0, keepdims=True)
    i2 = jnp.min(jnp.where(elm2 == m2, eiota, N_EXPERTS), axis=0, keepdims=True)
    p1 = 1.0 / esum
    p2 = jnp.exp(m2 - m1) / esum
    psum = p1 + p2
    w1 = g_w * (p1 / psum)
    w2 = g_w * (p2 / psum)

    oh1 = (eiota == i1).astype(F32)
    oh2 = (eiota == i2).astype(F32)
    cnt = oh1 + oh2
    base = carry_ref[:, 0:1] + _dot(cnt.astype(BF16), sut_ref[...])
    r1 = jnp.sum(oh1 * base, axis=0, keepdims=True)
    r2 = jnp.sum(oh2 * base, axis=0, keepdims=True)
    new_carry = carry_ref[...] + jnp.sum(cnt, axis=1, keepdims=True)
    carry_ref[...] = new_carry
    cnt_ref[...] = new_carry

    zi = jnp.zeros((4, tm), jnp.int32)
    ri_ref[...] = jnp.concatenate([i1, i2, r1.astype(jnp.int32), r2.astype(jnp.int32), zi], axis=0)
    rw_ref[...] = jnp.concatenate([w1, w2, jnp.zeros((6, tm), F32)], axis=0)


def _attn_route(xf, mixed, kmem, vmem, w_out, norm_xa_w, wq, wo, norm_moe_w, w_route_t, b_route, sut,
                batch, seq):
    n, d = xf.shape
    w_route_hi = w_route_t.astype(BF16)
    w_route_lo = (w_route_t - w_route_hi.astype(F32)).astype(BF16)
    tm = TOKEN_TILE
    nt = seq // tm
    m = kmem.shape[1]
    tok = lambda b, t: (b * nt + t, 0)
    lane_tok = lambda b, t: (0, b * nt + t)
    const2 = lambda b, t: (0, 0)
    return pl.pallas_call(
        _attn_route_kernel,
        grid=(batch, nt),
        in_specs=[
            pl.BlockSpec((tm, d), tok),
            pl.BlockSpec((tm, d), tok),
            pl.BlockSpec((1, m, d), lambda b, t: (b, 0, 0)),
            pl.BlockSpec((1, m, d), lambda b, t: (b, 0, 0)),
            pl.BlockSpec((d, d), const2),
            pl.BlockSpec((1, d), const2),
            pl.BlockSpec((d, d), const2),
            pl.BlockSpec((d, d), const2),
            pl.BlockSpec((1, d), const2),
            pl.BlockSpec((ROUTE_ROWS, d), const2),
            pl.BlockSpec((ROUTE_ROWS, d), const2),
            pl.BlockSpec((ROUTE_ROWS, 1), const2),
            pl.BlockSpec((tm, tm), const2),
        ],
        out_specs=[
            pl.BlockSpec((tm, d), tok),
            pl.BlockSpec((tm, 1, d // 2), lambda b, t: (b * nt + t, 0, 0)),
            pl.BlockSpec((8, tm), lane_tok),
            pl.BlockSpec((8, tm), lane_tok),
            pl.BlockSpec((N_EXPERTS, 128), const2),
        ],
        out_shape=[
            jax.ShapeDtypeStruct((n, d), F32),
            jax.ShapeDtypeStruct((n, 1, d // 2), jnp.uint32),
            jax.ShapeDtypeStruct((8, n), jnp.int32),
            jax.ShapeDtypeStruct((8, n), F32),
            jax.ShapeDtypeStruct((N_EXPERTS, 128), F32),
        ],
        scratch_shapes=[pltpu.VMEM((N_EXPERTS, 128), F32)],
        compiler_params=pltpu.CompilerParams(
            dimension_semantics=("arbitrary", "arbitrary"), vmem_limit_bytes=VMEM_LIMIT),
        name="attn_route",
    )(xf, mixed, kmem, vmem, w_out, norm_xa_w, wq, wo, norm_moe_w, w_route_hi, w_route_lo, b_route, sut)


def _dispatch_kernel(zpos_ref, dest_ref, h_ref, xs_ref, idx_ref, idx_sem, row_sem, zero_ref, zero_sem):
    i = pl.program_id(0)
    nsteps = pl.num_programs(0)
    td = h_ref.shape[0]
    bm = zero_ref.shape[0]
    slot = i % 2

    def idx_copy(step, sl):
        off = pl.multiple_of(sl * (2 * td), 2 * td)
        return pltpu.make_async_copy(dest_ref.at[step], idx_ref.at[pl.ds(off, 2 * td)], idx_sem.at[sl])

    @pl.when(i == 0)
    def _():
        zero_ref[...] = jnp.zeros_like(zero_ref)

        def zero_copy(e):
            return pltpu.make_async_copy(zero_ref, xs_ref.at[pl.ds(pl.multiple_of(zpos_ref[e], bm), bm), 0], zero_sem)

        def tail_copy(b):
            return pltpu.make_async_copy(zero_ref, xs_ref.at[pl.ds(pl.multiple_of(b * bm, bm), bm), 0], zero_sem)

        nused = zpos_ref[N_EXPERTS]
        nblk = xs_ref.shape[0] // bm
        for e in range(N_EXPERTS):
            pl.when(zpos_ref[e] >= 0)(lambda e=e: zero_copy(e).start())
        lax.fori_loop(nused, nblk, lambda b, c: (tail_copy(b).start(), c)[1], 0)
        for e in range(N_EXPERTS):
            pl.when(zpos_ref[e] >= 0)(lambda e=e: zero_copy(e).wait())
        lax.fori_loop(nused, nblk, lambda b, c: (tail_copy(b).wait(), c)[1], 0)
        idx_copy(0, 0).start()

    idx_copy(i, slot).wait()

    @pl.when(i + 1 < nsteps)
    def _():
        idx_copy(i + 1, 1 - slot).start()

    base = slot * (2 * td)

    for t in range(td):
        for k in range(TOP_K):
            pltpu.make_async_copy(h_ref.at[t], xs_ref.at[idx_ref[base + k * td + t]], row_sem).start(priority=t % 2)
    for _ in range(TOP_K):
        pltpu.make_async_copy(xs_ref.at[pl.ds(0, td)], xs_ref.at[pl.ds(0, td)], row_sem).wait()


def _dispatch(h3p, dest_tiles, zpos, cap):
    w = h3p.shape[-1]
    nt, td2 = dest_tiles.shape
    td = td2 // 2
    grid_spec = pltpu.PrefetchScalarGridSpec(
        num_scalar_prefetch=1,
        grid=(nt,),
        in_specs=[
            pl.BlockSpec(memory_space=pl.ANY),
            pl.BlockSpec((td, 1, w), lambda i, zp: (i, 0, 0)),
        ],
        out_specs=pl.BlockSpec(memory_space=pl.ANY),
        scratch_shapes=[
            pltpu.SMEM((2 * td2,), jnp.int32),
            pltpu.SemaphoreType.DMA((2,)),
            pltpu.SemaphoreType.DMA,
            pltpu.VMEM((MOE_ROWS, w), jnp.uint32),
            pltpu.SemaphoreType.DMA,
        ],
    )
    return pl.pallas_call(
        _dispatch_kernel,
        grid_spec=grid_spec,
        out_shape=jax.ShapeDtypeStruct((cap, 1, w), jnp.uint32),
        compiler_params=pltpu.CompilerParams(
            dimension_semantics=("arbitrary",), vmem_limit_bytes=VMEM_LIMIT),
        name="dispatch",
    )(zpos, dest_tiles, h3p)


def _expert_kernel(blk_e_ref, nused_ref, xs_ref, wg_ref, wu_ref, wd_ref, ys_ref, wg_b, wu_b, wd_b,
                   xbuf, ybuf, zbuf, in_sem, out_sem, zero_sem):
    i = pl.program_id(0)
    nsteps = pl.num_programs(0)
    nused = nused_ref[0]
    bm = xbuf.shape[1]
    slot = i % 2
    prev = blk_e_ref[jnp.maximum(i - 1, 0)]
    fresh = (i == 0) | (blk_e_ref[i] != prev)
    half = wd_b.shape[1] // 2

    def rows(ref, step):
        return ref.at[pl.ds(pl.multiple_of(step * bm, bm), bm), 0]

    def in_copy(step, sl):
        return pltpu.make_async_copy(rows(xs_ref, step), xbuf.at[sl], in_sem.at[sl])

    def out_copy(step, sl):
        return pltpu.make_async_copy(ybuf.at[sl], rows(ys_ref, step), out_sem.at[sl])

    def zero_copy(step):
        return pltpu.make_async_copy(zbuf, rows(ys_ref, step), zero_sem)

    @pl.when(i == 0)
    def _():
        zbuf[...] = jnp.zeros_like(zbuf)
        in_copy(0, 0).start()

    @pl.when(i + 1 < nused)
    def _():
        in_copy(i + 1, 1 - slot).start()

    @pl.when(fresh)
    def _():
        wg_b[...] = wg_ref[0].astype(BF16)
        wu_b[...] = wu_ref[0].astype(BF16)
        wd_b[...] = wd_ref[0].astype(BF16)

    @pl.when(i < nused)
    def _():
        in_copy(i, slot).wait()
        pl.when(i >= 2)(lambda: out_copy(i - 2, slot).wait())
        lo, hi = _unpack_rows(xbuf[slot])
        xb = jnp.concatenate([lo.astype(BF16), hi.astype(BF16)], axis=1)
        hid = (_silu(_dot(xb, wg_b[...])) * _dot(xb, wu_b[...])).astype(BF16)
        y = _dot(hid, wd_b[...])
        ybuf[slot] = _pack_rows(y[:, :half], y[:, half:])
        out_copy(i, slot).start()

    pl.when(i >= nused)(lambda: zero_copy(i).start())

    @pl.when(i == nsteps - 1)
    def _():
        pl.when(nused >= 2)(lambda: out_copy(nused - 2, nused % 2).wait())
        pl.when(nused >= 1)(lambda: out_copy(nused - 1, (nused - 1) % 2).wait())
        lax.fori_loop(nused, nsteps, lambda b, c: (zero_copy(b).wait(), c)[1], 0)


def _experts(xs, blk_e, nused, w_gate, w_up, w_down):
    cap, _, w = xs.shape
    _, d, de = w_gate.shape
    bm = MOE_ROWS
    wspec = lambda blk: pl.BlockSpec(blk, lambda i, be, nu: (be[i], 0, 0))
    grid_spec = pltpu.PrefetchScalarGridSpec(
        num_scalar_prefetch=2,
        grid=(cap // bm,),
        in_specs=[pl.BlockSpec(memory_space=pl.ANY), wspec((1, d, de)), wspec((1, d, de)), wspec((1, de, d))],
        out_specs=pl.BlockSpec(memory_space=pl.ANY),
        scratch_shapes=[
            pltpu.VMEM((d, de), BF16),
            pltpu.VMEM((d, de), BF16),
            pltpu.VMEM((de, d), BF16),
            pltpu.VMEM((2, bm, w), jnp.uint32),
            pltpu.VMEM((2, bm, w), jnp.uint32),
            pltpu.VMEM((bm, w), jnp.uint32),
            pltpu.SemaphoreType.DMA((2,)),
            pltpu.SemaphoreType.DMA((2,)),
            pltpu.SemaphoreType.DMA,
        ],
    )
    return pl.pallas_call(
        _expert_kernel,
        grid_spec=grid_spec,
        out_shape=jax.ShapeDtypeStruct((cap, 1, w), jnp.uint32),
        compiler_params=pltpu.CompilerParams(
            dimension_semantics=("arbitrary",), vmem_limit_bytes=VMEM_LIMIT),
        name="experts",
    )(blk_e, nused, xs, w_gate, w_up, w_down)


def _combine_kernel(dest_ref, ys_ref, x2_ref, rw_ref, eye_ref, nw_ref, o_ref, idx_ref, idx_sem, ybuf, ysem):
    i = pl.program_id(0)
    nsteps = pl.num_programs(0)
    tc, d = x2_ref.shape
    half = d // 2
    n_idx = 2 * tc

    def idx_copy(step):
        sl = step % 3
        off = pl.multiple_of(sl * n_idx, n_idx)
        return pltpu.make_async_copy(dest_ref.at[step], idx_ref.at[pl.ds(off, n_idx)], idx_sem.at[sl])

    def gather(step):
        base = (step % 3) * n_idx
        buf = ybuf.at[step % 2]
        sem = ysem.at[step % 2]

        for t in range(n_idx):
            pltpu.make_async_copy(ys_ref.at[idx_ref[base + t]], buf.at[pl.ds(t, 1)], sem).start(priority=t % 2)

    @pl.when(i == 0)
    def _():
        idx_copy(0).start()
        idx_copy(0).wait()
        gather(0)

        @pl.when(nsteps > 1)
        def _():
            idx_copy(1).start()

    @pl.when(i + 1 < nsteps)
    def _():
        idx_copy(i + 1).wait()

        @pl.when(i + 2 < nsteps)
        def _():
            idx_copy(i + 2).start()

        gather(i + 1)

    slot = i % 2
    pltpu.make_async_copy(ybuf.at[slot], ybuf.at[slot], ysem.at[slot]).wait()
    rw = rw_ref[...]
    rw_a = rw.astype(BF16)
    rw_b = (rw - rw_a.astype(F32)).astype(BF16)
    rw_c = (rw - rw_a.astype(F32) - rw_b.astype(F32)).astype(BF16)
    eye = eye_ref[...]
    wcol = _dot_nt(eye, rw_a) + (_dot_nt(eye, rw_b) + _dot_nt(eye, rw_c))
    lo1, hi1 = _unpack_rows(ybuf[slot, 0:tc])
    lo2, hi2 = _unpack_rows(ybuf[slot, tc:n_idx])
    w1 = wcol[:, 0:1]
    w2 = wcol[:, 1:2]
    z_lo = x2_ref[:, :half] + (lo1 * w1 + lo2 * w2)
    z_hi = x2_ref[:, half:] + (hi1 * w1 + hi2 * w2)
    ms = (jnp.sum(z_lo * z_lo, axis=-1, keepdims=True) + jnp.sum(z_hi * z_hi, axis=-1, keepdims=True)) / d
    scale = lax.rsqrt(ms + EPS)
    o_ref[:, :half] = z_lo * scale * nw_ref[:, :half]
    o_ref[:, half:] = z_hi * scale * nw_ref[:, half:]


def _combine(x2, ys, dest_tiles, rw, eye, norm_w):
    n, d = x2.shape
    nt, n_idx = dest_tiles.shape
    tc = n_idx // 2
    w = ys.shape[-1]
    return pl.pallas_call(
        _combine_kernel,
        grid=(nt,),
        in_specs=[pl.BlockSpec(memory_space=pl.ANY)] * 2 + [
            pl.BlockSpec((tc, d), lambda i: (i, 0)),
            pl.BlockSpec((8, tc), lambda i: (0, i)),
            pl.BlockSpec((tc, tc), lambda i: (0, 0)),
            pl.BlockSpec((1, d), lambda i: (0, 0)),
        ],
        out_specs=pl.BlockSpec((tc, d), lambda i: (i, 0)),
        out_shape=jax.ShapeDtypeStruct((n, d), F32),
        scratch_shapes=[
            pltpu.SMEM((3 * n_idx,), jnp.int32),
            pltpu.SemaphoreType.DMA((3,)),
            pltpu.VMEM((2, n_idx, w), jnp.uint32),
            pltpu.SemaphoreType.DMA((2,)),
        ],
        compiler_params=pltpu.CompilerParams(
            dimension_semantics=("arbitrary",), vmem_limit_bytes=VMEM_LIMIT),
        name="combine",
    )(dest_tiles, ys, x2, rw, eye, norm_w)


def _rope_column_order():
    half = RET_DK // 2
    first = [h * RET_DK + j for h in range(RET_HEADS) for j in range(half)]
    second = [h * RET_DK + half + j for h in range(RET_HEADS) for j in range(half)]
    return np.array(first + second, dtype=np.int32)


def _layer(xf, mem, batch, seq, norm_mix_w, w_in, ret_norm_w, ml_conv_w, ml_conv_b, ml_gate_b, ml_norm_w,
           w_out, norm_xa_w, norm_mem_w, xa_wq, xa_wkv, xa_wo, norm_moe_w, moe_w_group, moe_b_group,
           moe_w_router, moe_b_router, moe_w_gate, moe_w_up, moe_w_down, final_norm_w):
    n, d = xf.shape
    perm = _rope_column_order()
    cols = np.concatenate([perm, RET_QK + perm, np.arange(2 * RET_QK, MAIN_WIDTH)])
    w_main = w_in[:, cols].astype(BF16)
    w_if = w_in[:, MAIN_WIDTH:].astype(BF16)
    tabs = {k_: jnp.asarray(v_) for k_, v_ in _mixer_tables(seq).items()}
    proj, g, gt = _inproj(xf, norm_mix_w.reshape(1, d), w_main, w_if, w_if.T, tabs["cos"], tabs["sin"], ml_conv_w,
                          ml_conv_b.reshape(1, 2 * ML_QK), seq)
    mixed = _mixer(proj, g, gt, tabs, ret_norm_w.reshape(1, RET_V), ml_norm_w.reshape(1, ML_V), ml_gate_b,
                   batch, seq)

    kmem, vmem = _memkv(mem, norm_mem_w.reshape(1, d), xa_wkv.astype(BF16))

    w_route_t = jnp.concatenate(
        [moe_w_router.T, moe_w_group.T, jnp.zeros((ROUTE_ROWS - N_EXPERTS - N_GROUPS, d), F32)], axis=0)
    b_route = jnp.concatenate(
        [moe_b_router, moe_b_group, jnp.zeros((ROUTE_ROWS - N_EXPERTS - N_GROUPS,), F32)]).reshape(ROUTE_ROWS, 1)
    tm = TOKEN_TILE
    sut = jnp.asarray(np.triu(np.ones((tm, tm), np.float32), 1), dtype=BF16)
    x2, h3, ri, rw, cnt = _attn_route(xf, mixed, kmem, vmem, w_out.astype(BF16), norm_xa_w.reshape(1, d),
                                      xa_wq.astype(BF16), xa_wo.astype(BF16), norm_moe_w.reshape(1, d),
                                      w_route_t, b_route, sut, batch, seq)

    bm = MOE_ROWS
    counts = cnt[:, 0].astype(jnp.int32)
    padded = (counts + bm - 1) // bm * bm
    pends = jnp.cumsum(padded)
    pstarts = pends - padded
    expert = ri[0:TOP_K]
    onehot = expert[None] == jnp.arange(N_EXPERTS, dtype=jnp.int32)[:, None, None]
    dest = jnp.sum(jnp.where(onehot, pstarts[:, None, None], 0), axis=0) + ri[TOP_K:2 * TOP_K]
    cap = n * TOP_K + N_EXPERTS * bm
    nblk = cap // bm
    blk_start = jnp.arange(nblk, dtype=jnp.int32) * bm
    blk_e = jnp.minimum(jnp.sum(blk_start[:, None] >= pends[None, :], axis=1), N_EXPERTS - 1).astype(jnp.int32)
    nused = (pends[-1] // bm).astype(jnp.int32).reshape(1)
    zpos = jnp.where(padded > counts, pends - bm, -1).astype(jnp.int32)
    zpos = jnp.concatenate([zpos, nused])

    def tiles(rows):
        return dest.reshape(TOP_K, n // rows, rows).transpose(1, 0, 2).reshape(n // rows, TOP_K * rows)

    xs = _dispatch(h3, tiles(DISPATCH_TILE), zpos, cap)
    ys = _experts(xs, blk_e, nused, moe_w_gate, moe_w_up, moe_w_down)
    eye = jnp.asarray(np.eye(tm, dtype=np.float32), dtype=BF16)
    return _combine(x2, ys, tiles(tm), rw, eye, final_norm_w.reshape(1, d))


def kernel(x, mem, norm_mix_w, w_in, ret_norm_w, ml_conv_w, ml_conv_b, ml_gate_b, ml_norm_w, w_out, norm_xa_w, norm_mem_w, xa_wq, xa_wkv, xa_wo, norm_moe_w, moe_w_group, moe_b_group, moe_w_router, moe_b_router, moe_w_gate, moe_w_up, moe_w_down, norm_final_w):
    batch, seq, d = x.shape
    depth = w_in.shape[0]
    assert depth == 1, "the final norm is fused into the last layer's combine kernel"
    l = 0
    out = _layer(x.reshape(batch * seq, d), mem, batch, seq, norm_mix_w[l], w_in[l], ret_norm_w[l], ml_conv_w[l],
                 ml_conv_b[l], ml_gate_b[l], ml_norm_w[l], w_out[l], norm_xa_w[l], norm_mem_w[l], xa_wq[l],
                 xa_wkv[l], xa_wo[l], norm_moe_w[l], moe_w_group[l], moe_b_group[l], moe_w_router[l],
                 moe_b_router[l], moe_w_gate[l], moe_w_up[l], moe_w_down[l], norm_final_w)
    return out.reshape(batch, seq, d)
```

```python
import functools

import numpy as np
import jax
import jax.numpy as jnp
from jax import lax
from jax.experimental import pallas as pl
from jax.experimental.pallas import tpu as pltpu

F32 = jnp.float32
BF16 = jnp.bfloat16
HIGHEST = lax.Precision.HIGHEST

CHUNK = 128
RET_HEADS = 4
RET_DK = 64
RET_DV = 128
ML_HEADS = 4
ML_DK = 128
ML_DV = 128
CONV_W = 4
XA_HEADS = 4
N_GROUPS = 4
EXP_PER_GROUP = 8
N_EXPERTS = N_GROUPS * EXP_PER_GROUP
TOP_K = 2
ROPE_BASE = 10000.0
EPS = 1e-6

RET_QK = RET_HEADS * RET_DK
RET_V = RET_HEADS * RET_DV
ML_QK = ML_HEADS * ML_DK
ML_V = ML_HEADS * ML_DV
OFF_RQ = 0
OFF_RK = OFF_RQ + RET_QK
OFF_RV = OFF_RK + RET_QK
OFF_RG = OFF_RV + RET_V
OFF_MQK = OFF_RG + RET_V
OFF_MV = OFF_MQK + 2 * ML_QK
OFF_MO = OFF_MV + ML_V
MAIN_WIDTH = OFF_MO + ML_V
N_GATES = 2 * ML_HEADS

ROUTE_ROWS = 40
TOKEN_TILE = 1024
MOE_ROWS = 512
DISPATCH_TILE = 1024
COMBINE_TILE = 512
MIXER_BATCHES = 4
ATTN_GROUPS = 2
VMEM_LIMIT = 56 * 1024 * 1024


def _dot(a, b):
    return jnp.dot(a, b, preferred_element_type=F32)


def _dot_nt(a, b, precision=None):
    return lax.dot_general(a, b, (((1,), (1,)), ((), ())), preferred_element_type=F32, precision=precision)


def _dot_tn(a, b):
    return lax.dot_general(a, b, (((0,), (0,)), ((), ())), preferred_element_type=F32)


def _rms(x, w):
    return x * lax.rsqrt(jnp.mean(x * x, axis=-1, keepdims=True) + EPS) * w


def _sigmoid(x):
    return 1.0 / (1.0 + jnp.exp(-x))


def _silu(x):
    return x * _sigmoid(x)


def _log_sigmoid(x):
    return jnp.minimum(x, 0.0) - jnp.log1p(jnp.exp(-jnp.abs(x)))


def _head_norms(ts, mean_w):
    mu = [_dot(t.astype(BF16), mean_w) for t in ts]
    dl = [t - m for t, m in zip(ts, mu)]
    var = [_dot((d * d).astype(BF16), mean_w) for d in dl]
    return [d * lax.rsqrt(v + EPS) for d, v in zip(dl, var)]


def _pack_rows(lo, hi):
    def bits(t):
        return lax.bitcast_convert_type(t.astype(BF16), jnp.uint16).astype(jnp.uint32)
    return bits(lo) | (bits(hi) << 16)


def _unpack_rows(u):
    lo = lax.bitcast_convert_type(u << 16, F32)
    hi = lax.bitcast_convert_type(u & jnp.uint32(0xFFFF0000), F32)
    return lo, hi


def _inproj_kernel(tiles_per_seq, x_ref, nw_ref, w_ref, wif_ref, wift_ref, cos_ref, sin_ref, convw_ref, convb_ref,
                   proj_ref, g_ref, gt_ref, carry_ref):
    tm = x_ref.shape[0]

    @pl.when(pl.program_id(0) == 0)
    def _():
        carry_ref[...] = jnp.zeros_like(carry_ref)

    h = _rms(x_ref[...], nw_ref[...]).astype(BF16)

    def mm(off, width):
        return _dot(h, w_ref[:, off:off + width])

    def rotary(qk):
        cos = cos_ref[...]
        sin = sin_ref[...]
        half = RET_QK // 2
        for off, scale in ((OFF_RQ, None), (OFF_RK, RET_DK ** -0.5)):
            t1 = qk[:, off - OFF_RQ:off - OFF_RQ + half]
            t2 = qk[:, off - OFF_RQ + half:off - OFF_RQ + 2 * half]
            r1 = t1 * cos - t2 * sin
            r2 = t1 * sin + t2 * cos
            if scale is not None:
                r1, r2 = r1 * scale, r2 * scale
            proj_ref[:, off:off + half] = r1.astype(BF16)
            proj_ref[:, off + half:off + 2 * half] = r2.astype(BF16)

    def conv_silu(part, scale, cur):
        c0 = part * ML_QK
        first = (pl.program_id(0) % tiles_per_seq) == 0
        row8 = lax.broadcasted_iota(jnp.int32, (8, ML_QK), 0)
        prev = jnp.where(first, 0.0, carry_ref[:, c0:c0 + ML_QK])
        acc = cur * convw_ref[CONV_W - 1:CONV_W, c0:c0 + ML_QK] + convb_ref[:, c0:c0 + ML_QK]
        for s in range(1, CONV_W):
            rolled = pltpu.roll(cur, s, 0)
            head8 = jnp.where(row8 < s, pltpu.roll(prev, s, 0), rolled[0:8])
            shifted = jnp.concatenate([head8, rolled[8:]], axis=0)
            acc = acc + shifted * convw_ref[CONV_W - 1 - s:CONV_W - s, c0:c0 + ML_QK]
        carry_ref[:, c0:c0 + ML_QK] = cur[tm - 8:tm]
        act = _silu(acc) if scale is None else _silu(acc) * scale
        proj_ref[:, OFF_MQK + c0:OFF_MQK + c0 + ML_QK] = act.astype(BF16)

    def store(off, width, fn=None):
        def ep(t):
            proj_ref[:, off:off + width] = (t if fn is None else fn(t)).astype(BF16)
        return ep

    rotary(mm(OFF_RQ, 2 * RET_QK))
    store(OFF_RV, RET_V)(mm(OFF_RV, RET_V))
    store(OFF_RG, RET_V, _silu)(mm(OFF_RG, RET_V))
    conv_silu(0, None, mm(OFF_MQK, ML_QK))
    conv_silu(1, ML_DK ** -0.5, mm(OFF_MQK + ML_QK, ML_QK))
    store(OFF_MV, ML_V)(mm(OFF_MV, ML_V))
    store(OFF_MO, ML_V, _sigmoid)(mm(OFF_MO, ML_V))
    g_ref[...] = _dot(h, wif_ref[...])
    gt_ref[...] = _dot_nt(wift_ref[...], h)


def _inproj(xf, norm_w, w_main, w_if, w_ift, cos, sin, conv_w, conv_b, seq):
    n, d = xf.shape
    tm = TOKEN_TILE
    tiles_per_seq = seq // tm
    return pl.pallas_call(
        functools.partial(_inproj_kernel, tiles_per_seq),
        grid=(n // tm,),
        in_specs=[
            pl.BlockSpec((tm, d), lambda i: (i, 0)),
            pl.BlockSpec((1, d), lambda i: (0, 0)),
            pl.BlockSpec((d, MAIN_WIDTH), lambda i: (0, 0)),
            pl.BlockSpec((d, N_GATES), lambda i: (0, 0)),
            pl.BlockSpec((N_GATES, d), lambda i: (0, 0)),
            pl.BlockSpec((tm, RET_QK // 2), lambda i: (i % tiles_per_seq, 0)),
            pl.BlockSpec((tm, RET_QK // 2), lambda i: (i % tiles_per_seq, 0)),
            pl.BlockSpec((CONV_W, 2 * ML_QK), lambda i: (0, 0)),
            pl.BlockSpec((1, 2 * ML_QK), lambda i: (0, 0)),
        ],
        out_specs=[
            pl.BlockSpec((tm, MAIN_WIDTH), lambda i: (i, 0)),
            pl.BlockSpec((tm, N_GATES), lambda i: (i, 0)),
            pl.BlockSpec((N_GATES, tm), lambda i: (0, i)),
        ],
        out_shape=[
            jax.ShapeDtypeStruct((n, MAIN_WIDTH), BF16),
            jax.ShapeDtypeStruct((n, N_GATES), F32),
            jax.ShapeDtypeStruct((N_GATES, n), F32),
        ],
        scratch_shapes=[pltpu.VMEM((8, 2 * ML_QK), F32)],
        compiler_params=pltpu.CompilerParams(
            dimension_semantics=("arbitrary",), vmem_limit_bytes=VMEM_LIMIT),
        name="inproj",
    )(xf, norm_w, w_main, w_if, w_ift, cos, sin, conv_w, conv_b)


def _mixer_kernel(*refs):
    @pl.when(pl.program_id(1) == 0)
    def _():
        for state_ref in refs[-4:]:
            state_ref[...] = jnp.zeros_like(state_ref)

    for bi in range(refs[0].shape[0]):
        _mixer_one(bi, *refs)


def _mixer_one(bi, proj_ref, g_ref, gt_ref, qdec_ref, kdec_ref, dmat_ref,
               bmask_ref, cdec_ref, hmask_ref, tril_ref, triu_ref, ones_ref, retw_ref, mlw_ref, gbc_ref, gbr_ref,
               out_ref, r_ref, c_ref, n_ref, m_ref):
    L = CHUNK
    proj_ref, g_ref, gt_ref, out_ref = proj_ref.at[bi], g_ref.at[bi], gt_ref.at[bi], out_ref.at[bi]
    r_ref, c_ref, n_ref, m_ref = r_ref.at[bi], c_ref.at[bi], n_ref.at[bi], m_ref.at[bi]
    mean_w = ones_ref[...]


    q = proj_ref[:, OFF_RQ:OFF_RQ + RET_QK].astype(F32)
    k_b = proj_ref[:, OFF_RK:OFF_RK + RET_QK]
    k = k_b.astype(F32)
    v = proj_ref[:, OFF_RV:OFF_RV + RET_V]
    r_prev = r_ref[...]
    cross = _dot((q * qdec_ref[...]).astype(BF16), r_prev.astype(BF16))
    kv = _dot_tn((k * kdec_ref[...]).astype(BF16), v) * bmask_ref[...]
    r_ref[...] = cdec_ref[...] * r_prev + kv
    RH = range(RET_HEADS)
    sc = [_dot_nt((q * hmask_ref[h:h + 1, :]).astype(BF16), k_b) for h in RH]
    sc = [(sc[h] * dmat_ref[h]).astype(BF16) for h in RH]
    tot = [_dot(sc[h], v[:, h * RET_DV:(h + 1) * RET_DV]) + cross[:, h * RET_DV:(h + 1) * RET_DV] for h in RH]
    ret = jnp.concatenate(_head_norms(tot, mean_w), axis=1) * retw_ref[...]
    ret = ret * proj_ref[:, OFF_RG:OFF_RG + RET_V].astype(F32)
    out_ref[:, 0:RET_V] = ret.astype(BF16)

    mq = proj_ref[:, OFF_MQK:OFF_MQK + ML_QK]
    mk = proj_ref[:, OFF_MQK + ML_QK:OFF_MQK + 2 * ML_QK]
    mv = proj_ref[:, OFF_MV:OFF_MV + ML_V]

    gc = g_ref[...] + gbc_ref[...]
    gr = gt_ref[...] + gbr_ref[...]
    b_c = jnp.dot(tril_ref[...], _log_sigmoid(gc), preferred_element_type=F32, precision=HIGHEST)
    b_r = jnp.dot(_log_sigmoid(gr), triu_ref[...], preferred_element_type=F32, precision=HIGHEST)
    causal = (lax.broadcasted_iota(jnp.int32, (L, L), 0) >= lax.broadcasted_iota(jnp.int32, (L, L), 1))
    MH = range(ML_HEADS)
    bc = [b_c[:, ML_HEADS + h:ML_HEADS + h + 1] for h in MH]
    br = [b_r[ML_HEADS + h:ML_HEADS + h + 1, :] for h in MH]
    igc = [gc[:, h:h + 1] for h in MH]
    igr = [gr[h:h + 1, :] for h in MH]
    btot = [br[h][:, L - 1:L] for h in MH]
    qh_b = [mq[:, h * ML_DK:(h + 1) * ML_DK] for h in MH]
    kh_b = [mk[:, h * ML_DK:(h + 1) * ML_DK] for h in MH]
    vh = [mv[:, h * ML_DV:(h + 1) * ML_DV] for h in MH]
    c_prev = [c_ref[h] for h in MH]
    n_prev = [n_ref[h][0:1, :] for h in MH]
    m_prev = [m_ref[h][0:1, 0:1] for h in MH]
    s_raw = [_dot_nt(qh_b[h], kh_b[h]) for h in MH]
    qc = [_dot(qh_b[h], c_prev[h].astype(BF16)) for h in MH]
    log_d = [jnp.where(causal, bc[h] - br[h] + igr[h], -jnp.inf) for h in MH]
    m_intra = [jnp.max(log_d[h], axis=1, keepdims=True) for h in MH]
    m_loc = [jnp.max(btot[h] - br[h] + igr[h], axis=1, keepdims=True) for h in MH]
    kw = [kh_b[h].astype(F32) * jnp.exp(btot[h] - bc[h] + igc[h] - m_loc[h]) for h in MH]
    kv_loc = [_dot_tn(kw[h].astype(BF16), vh[h]) for h in MH]
    n_loc = [jnp.sum(kw[h], axis=0, keepdims=True) for h in MH]
    m_inter = [bc[h] + m_prev[h] for h in MH]
    m_t = [jnp.maximum(m_intra[h], m_inter[h]) for h in MH]
    s_mat = [s_raw[h] * jnp.exp(log_d[h] - m_t[h]) for h in MH]
    inter = [jnp.exp(m_inter[h] - m_t[h]) for h in MH]
    num = [_dot(s_mat[h].astype(BF16), vh[h]) + inter[h] * qc[h] for h in MH]
    den = [jnp.sum(s_mat[h], axis=1, keepdims=True)
           + inter[h] * jnp.sum(qh_b[h].astype(F32) * n_prev[h], axis=1, keepdims=True) for h in MH]
    hh = [num[h] / jnp.maximum(jnp.abs(den[h]), jnp.exp(-m_t[h])) for h in MH]
    for h in MH:
        m_new = jnp.maximum(btot[h] + m_prev[h], m_loc[h])
        s_old = jnp.exp(btot[h] + m_prev[h] - m_new)
        s_loc = jnp.exp(m_loc[h] - m_new)
        c_ref[h] = s_old * c_prev[h] + s_loc * kv_loc[h]
        n_ref[h] = jnp.broadcast_to(s_old * n_prev[h] + s_loc * n_loc[h], (8, ML_DK))
        m_ref[h] = jnp.broadcast_to(m_new, (8, 128))
    ml = jnp.concatenate(_head_norms(hh, mean_w), axis=1) * mlw_ref[...]
    ml = ml * proj_ref[:, OFF_MO:OFF_MO + ML_V].astype(F32)
    out_ref[:, RET_V:RET_V + ML_V] = ml.astype(BF16)


def _mixer_tables(seq):
    L = CHUNK
    half = RET_DK // 2
    inv = ROPE_BASE ** (-np.arange(half, dtype=np.float64) / half)
    ang = np.arange(seq, dtype=np.float64)[:, None] * inv[None, :].astype(np.float32).astype(np.float64)
    cos = np.tile(np.cos(ang), (1, RET_HEADS)).astype(np.float32)
    sin = np.tile(np.sin(ang), (1, RET_HEADS)).astype(np.float32)
    log_g = np.log1p(-np.exp2(-5.0 - np.arange(RET_HEADS, dtype=np.float64)))
    n = np.arange(L, dtype=np.float64)
    lane_head = (np.arange(RET_QK) % (RET_QK // 2)) // half
    qdec = np.exp((n + 1)[:, None] * log_g[lane_head][None, :]).astype(np.float32)
    kdec = np.exp((L - 1 - n)[:, None] * log_g[lane_head][None, :]).astype(np.float32)
    diff = n[:, None] - n[None, :]
    dmat = np.where(diff >= 0, np.exp(log_g[:, None, None] * np.maximum(diff, 0.0)[None]), 0.0).astype(np.float32)
    col_head = np.arange(RET_V) // RET_DV
    bmask = (lane_head[:, None] == col_head[None, :]).astype(np.float32)
    cdec = np.exp(L * log_g[col_head])[None, :].astype(np.float32)
    hmask = (lane_head[None, :] == np.arange(RET_HEADS)[:, None]).astype(np.float32)
    hmask = np.concatenate([hmask, np.zeros((8 - RET_HEADS, RET_QK), np.float32)], axis=0)
    tril = np.tril(np.ones((L, L), np.float32))
    ones = np.full((RET_DV, RET_DV), 1.0 / RET_DV, np.float32)
    return dict(cos=cos, sin=sin, qdec=qdec, kdec=kdec, dmat=dmat, bmask=bmask, cdec=cdec, hmask=hmask,
                tril=tril, triu=np.ascontiguousarray(tril.T), ones=ones)


def _mixer(proj, g, gt, tabs, ret_norm_w, ml_norm_w, gate_b, batch, seq):
    L = CHUNK
    nc = seq // L
    n = batch * seq
    nb = MIXER_BATCHES if batch % MIXER_BATCHES == 0 else 1
    proj = proj.reshape(batch, seq, MAIN_WIDTH)
    g = g.reshape(batch, seq, N_GATES)
    gt = gt.reshape(N_GATES, batch, seq).transpose(1, 0, 2)
    const2 = lambda b, c: (0, 0)
    const3 = lambda b, c: (0, 0, 0)
    tok = lambda b, c: (b, c, 0)
    in_specs = [
        pl.BlockSpec((nb, L, MAIN_WIDTH), tok),
        pl.BlockSpec((nb, L, N_GATES), tok),
        pl.BlockSpec((nb, N_GATES, L), lambda b, c: (b, 0, c)),
        pl.BlockSpec((L, RET_QK), const2),
        pl.BlockSpec((L, RET_QK), const2),
        pl.BlockSpec((RET_HEADS, L, L), const3),
        pl.BlockSpec((RET_QK, RET_V), const2),
        pl.BlockSpec((1, RET_V), const2),
        pl.BlockSpec((8, RET_QK), const2),
        pl.BlockSpec((L, L), const2),
        pl.BlockSpec((L, L), const2),
        pl.BlockSpec((RET_DV, RET_DV), const2),
        pl.BlockSpec((1, RET_V), const2),
        pl.BlockSpec((1, ML_V), const2),
        pl.BlockSpec((1, N_GATES), const2),
        pl.BlockSpec((N_GATES, 1), const2),
    ]
    return pl.pallas_call(
        _mixer_kernel,
        grid=(batch // nb, nc),
        in_specs=in_specs,
        out_specs=pl.BlockSpec((nb, L, RET_V + ML_V), tok),
        out_shape=jax.ShapeDtypeStruct((batch, seq, RET_V + ML_V), BF16),
        scratch_shapes=[
            pltpu.VMEM((nb, RET_QK, RET_V), F32),
            pltpu.VMEM((nb, ML_HEADS, ML_DK, ML_DV), F32),
            pltpu.VMEM((nb, ML_HEADS, 8, ML_DK), F32),
            pltpu.VMEM((nb, ML_HEADS, 8, 128), F32),
        ],
        compiler_params=pltpu.CompilerParams(
            dimension_semantics=("arbitrary", "arbitrary"), vmem_limit_bytes=VMEM_LIMIT),
        name="mixer",
    )(proj, g, gt, tabs["qdec"], tabs["kdec"], tabs["dmat"], tabs["bmask"],
      tabs["cdec"], tabs["hmask"], tabs["tril"], tabs["triu"], tabs["ones"].astype(BF16), ret_norm_w, ml_norm_w,
      gate_b.reshape(1, N_GATES), gate_b.reshape(N_GATES, 1)).reshape(n, RET_V + ML_V)


def _memkv_kernel(mem_ref, nw_ref, wkv_ref, k_ref, v_ref):
    d = mem_ref.shape[-1]
    mn = _rms(mem_ref[0], nw_ref[...]).astype(BF16)
    k_ref[0] = _dot(mn, wkv_ref[:, :d]).astype(BF16)
    v_ref[0] = _dot(mn, wkv_ref[:, d:]).astype(BF16)


def _memkv(mem, norm_w, wkv):
    b, m, d = mem.shape
    return pl.pallas_call(
        _memkv_kernel,
        grid=(b,),
        in_specs=[
            pl.BlockSpec((1, m, d), lambda i: (i, 0, 0)),
            pl.BlockSpec((1, d), lambda i: (0, 0)),
            pl.BlockSpec((d, 2 * d), lambda i: (0, 0)),
        ],
        out_specs=[pl.BlockSpec((1, m, d), lambda i: (i, 0, 0))] * 2,
        out_shape=[jax.ShapeDtypeStruct((b, m, d), BF16)] * 2,
        compiler_params=pltpu.CompilerParams(
            dimension_semantics=("arbitrary",), vmem_limit_bytes=VMEM_LIMIT),
        name="memkv",
    )(mem, norm_w, wkv)


def _attn_route_kernel(x_ref, mix_ref, k_ref, v_ref, wout_ref, nxa_ref, wq_ref, wo_ref, nmoe_ref,
                       wr_ref, wrlo_ref, br_ref, sut_ref,
                       x2_ref, h3_ref, ri_ref, rw_ref, cnt_ref, carry_ref):
    tm, d = x_ref.shape
    dh = d // XA_HEADS

    @pl.when((pl.program_id(0) == 0) & (pl.program_id(1) == 0))
    def _():
        carry_ref[...] = jnp.zeros_like(carry_ref)

    groups = [slice(g * (tm // ATTN_GROUPS), (g + 1) * (tm // ATTN_GROUPS)) for g in range(ATTN_GROUPS)]
    x1 = [x_ref[s, :] + _dot(mix_ref[s, :], wout_ref[...]) for s in groups]
    h2 = [_rms(t, nxa_ref[...]).astype(BF16) for t in x1]
    q = [_dot(t, wq_ref[...]).astype(BF16) for t in h2]
    o = []
    for qg in q:
        heads = []
        for h in range(XA_HEADS):
            logits = _dot_nt(qg[:, h * dh:(h + 1) * dh], k_ref[0, :, h * dh:(h + 1) * dh]) * (dh ** -0.5)
            mx = jnp.max(logits, axis=-1, keepdims=True)
            e = jnp.exp(logits - mx)
            p = (e / jnp.sum(e, axis=-1, keepdims=True)).astype(BF16)
            heads.append(_dot(p, v_ref[0, :, h * dh:(h + 1) * dh]).astype(BF16))
        o.append(jnp.concatenate(heads, axis=1))
    x2 = [a + _dot(b, wo_ref[...]) for a, b in zip(x1, o)]
    for s, t in zip(groups, x2):
        x2_ref[s, :] = t
    h3 = [_rms(t, nmoe_ref[...]) for t in x2]
    for s, t in zip(groups, h3):
        h3_ref[s, 0, :] = _pack_rows(t[:, :d // 2], t[:, d // 2:])

    lts = []
    for t in h3:
        t_hi = t.astype(BF16)
        t_lo = (t - t_hi.astype(F32)).astype(BF16)
        lts.append(_dot_nt(wr_ref[...], t_hi) + (_dot_nt(wr_ref[...], t_lo) + _dot_nt(wrlo_ref[...], t_hi)))
    lt = jnp.concatenate(lts, axis=1) + br_ref[...]
    gl = lt[N_EXPERTS:N_EXPERTS + N_GROUPS]
    gmax = jnp.max(gl, axis=0, keepdims=True)
    g_w = 1.0 / jnp.sum(jnp.exp(gl - gmax), axis=0, keepdims=True)
    giota = lax.broadcasted_iota(jnp.int32, gl.shape, 0)
    g_sel = jnp.min(jnp.where(gl == gmax, giota, N_GROUPS), axis=0, keepdims=True)
    el = lt[0:N_EXPERTS]
    eiota = lax.broadcasted_iota(jnp.int32, el.shape, 0)
    in_grp = (eiota // EXP_PER_GROUP) == g_sel
    elm = jnp.where(in_grp, el, -jnp.inf)
    m1 = jnp.max(elm, axis=0, keepdims=True)
    esum = jnp.sum(jnp.where(in_grp, jnp.exp(el - m1), 0.0), axis=0, keepdims=True)
    i1 = jnp.min(jnp.where(elm == m1, eiota, N_EXPERTS), axis=0, keepdims=True)
    elm2 = jnp.where(eiota == i1, -jnp.inf, elm)
    m2 = jnp.max(elm2, axis=0, keepdims=True)
    i2 = jnp.min(jnp.where(elm2 == m2, eiota, N_EXPERTS), axis=0, keepdims=True)
    p1 = 1.0 / esum
    p2 = jnp.exp(m2 - m1) / esum
    psum = p1 + p2
    w1 = g_w * (p1 / psum)
    w2 = g_w * (p2 / psum)

    oh1 = (eiota == i1).astype(F32)
    oh2 = (eiota == i2).astype(F32)
    cnt = oh1 + oh2
    base = carry_ref[:, 0:1] + _dot(cnt.astype(BF16), sut_ref[...])
    r1 = jnp.sum(oh1 * base, axis=0, keepdims=True)
    r2 = jnp.sum(oh2 * base, axis=0, keepdims=True)
    new_carry = carry_ref[...] + jnp.sum(cnt, axis=1, keepdims=True)
    carry_ref[...] = new_carry
    cnt_ref[...] = new_carry

    zi = jnp.zeros((4, tm), jnp.int32)
    ri_ref[...] = jnp.concatenate([i1, i2, r1.astype(jnp.int32), r2.astype(jnp.int32), zi], axis=0)
    rw_ref[...] = jnp.concatenate([w1, w2, jnp.zeros((6, tm), F32)], axis=0)


def _attn_route(xf, mixed, kmem, vmem, w_out, norm_xa_w, wq, wo, norm_moe_w, w_route_t, b_route, sut,
                batch, seq):
    n, d = xf.shape
    w_route_hi = w_route_t.astype(BF16)
    w_route_lo = (w_route_t - w_route_hi.astype(F32)).astype(BF16)
    tm = TOKEN_TILE
    nt = seq // tm
    m = kmem.shape[1]
    tok = lambda b, t: (b * nt + t, 0)
    lane_tok = lambda b, t: (0, b * nt + t)
    const2 = lambda b, t: (0, 0)
    return pl.pallas_call(
        _attn_route_kernel,
        grid=(batch, nt),
        in_specs=[
            pl.BlockSpec((tm, d), tok),
            pl.BlockSpec((tm, d), tok),
            pl.BlockSpec((1, m, d), lambda b, t: (b, 0, 0)),
            pl.BlockSpec((1, m, d), lambda b, t: (b, 0, 0)),
            pl.BlockSpec((d, d), const2),
            pl.BlockSpec((1, d), const2),
            pl.BlockSpec((d, d), const2),
            pl.BlockSpec((d, d), const2),
            pl.BlockSpec((1, d), const2),
            pl.BlockSpec((ROUTE_ROWS, d), const2),
            pl.BlockSpec((ROUTE_ROWS, d), const2),
            pl.BlockSpec((ROUTE_ROWS, 1), const2),
            pl.BlockSpec((tm, tm), const2),
        ],
        out_specs=[
            pl.BlockSpec((tm, d), tok),
            pl.BlockSpec((tm, 1, d // 2), lambda b, t: (b * nt + t, 0, 0)),
            pl.BlockSpec((8, tm), lane_tok),
            pl.BlockSpec((8, tm), lane_tok),
            pl.BlockSpec((N_EXPERTS, 128), const2),
        ],
        out_shape=[
            jax.ShapeDtypeStruct((n, d), F32),
            jax.ShapeDtypeStruct((n, 1, d // 2), jnp.uint32),
            jax.ShapeDtypeStruct((8, n), jnp.int32),
            jax.ShapeDtypeStruct((8, n), F32),
            jax.ShapeDtypeStruct((N_EXPERTS, 128), F32),
        ],
        scratch_shapes=[pltpu.VMEM((N_EXPERTS, 128), F32)],
        compiler_params=pltpu.CompilerParams(
            dimension_semantics=("arbitrary", "arbitrary"), vmem_limit_bytes=VMEM_LIMIT),
        name="attn_route",
    )(xf, mixed, kmem, vmem, w_out, norm_xa_w, wq, wo, norm_moe_w, w_route_hi, w_route_lo, b_route, sut)


def _dispatch_kernel(zpos_ref, dest_ref, h_ref, xs_ref, idx_ref, idx_sem, row_sem, zero_ref, zero_sem):
    i = pl.program_id(0)
    nsteps = pl.num_programs(0)
    td = h_ref.shape[0]
    bm = zero_ref.shape[0]
    slot = i % 2

    def idx_copy(step, sl):
        off = pl.multiple_of(sl * (2 * td), 2 * td)
        return pltpu.make_async_copy(dest_ref.at[step], idx_ref.at[pl.ds(off, 2 * td)], idx_sem.at[sl])

    @pl.when(i == 0)
    def _():
        zero_ref[...] = jnp.zeros_like(zero_ref)

        def zero_copy(e):
            return pltpu.make_async_copy(zero_ref, xs_ref.at[pl.ds(pl.multiple_of(zpos_ref[e], bm), bm), 0], zero_sem)

        def tail_copy(b):
            return pltpu.make_async_copy(zero_ref, xs_ref.at[pl.ds(pl.multiple_of(b * bm, bm), bm), 0], zero_sem)

        nused = zpos_ref[N_EXPERTS]
        nblk = xs_ref.shape[0] // bm
        for e in range(N_EXPERTS):
            pl.when(zpos_ref[e] >= 0)(lambda e=e: zero_copy(e).start())
        lax.fori_loop(nused, nblk, lambda b, c: (tail_copy(b).start(), c)[1], 0)
        for e in range(N_EXPERTS):
            pl.when(zpos_ref[e] >= 0)(lambda e=e: zero_copy(e).wait())
        lax.fori_loop(nused, nblk, lambda b, c: (tail_copy(b).wait(), c)[1], 0)
        idx_copy(0, 0).start()

    idx_copy(i, slot).wait()

    @pl.when(i + 1 < nsteps)
    def _():
        idx_copy(i + 1, 1 - slot).start()

    base = slot * (2 * td)

    for t in range(td):
        for k in range(TOP_K):
            pltpu.make_async_copy(h_ref.at[t], xs_ref.at[idx_ref[base + k * td + t]], row_sem).start(priority=t % 2)
    for _ in range(TOP_K):
        pltpu.make_async_copy(xs_ref.at[pl.ds(0, td)], xs_ref.at[pl.ds(0, td)], row_sem).wait()


def _dispatch(h3p, dest_tiles, zpos, cap):
    w = h3p.shape[-1]
    nt, td2 = dest_tiles.shape
    td = td2 // 2
    grid_spec = pltpu.PrefetchScalarGridSpec(
        num_scalar_prefetch=1,
        grid=(nt,),
        in_specs=[
            pl.BlockSpec(memory_space=pl.ANY),
            pl.BlockSpec((td, 1, w), lambda i, zp: (i, 0, 0)),
        ],
        out_specs=pl.BlockSpec(memory_space=pl.ANY),
        scratch_shapes=[
            pltpu.SMEM((2 * td2,), jnp.int32),
            pltpu.SemaphoreType.DMA((2,)),
            pltpu.SemaphoreType.DMA,
            pltpu.VMEM((MOE_ROWS, w), jnp.uint32),
            pltpu.SemaphoreType.DMA,
        ],
    )
    return pl.pallas_call(
        _dispatch_kernel,
        grid_spec=grid_spec,
        out_shape=jax.ShapeDtypeStruct((cap, 1, w), jnp.uint32),
        compiler_params=pltpu.CompilerParams(
            dimension_semantics=("arbitrary",), vmem_limit_bytes=VMEM_LIMIT),
        name="dispatch",
    )(zpos, dest_tiles, h3p)


def _expert_kernel(blk_e_ref, nused_ref, xs_ref, wg_ref, wu_ref, wd_ref, ys_ref, wg_b, wu_b, wd_b,
                   xbuf, ybuf, zbuf, in_sem, out_sem, zero_sem):
    i = pl.program_id(0)
    nsteps = pl.num_programs(0)
    nused = nused_ref[0]
    bm = xbuf.shape[1]
    slot = i % 2
    prev = blk_e_ref[jnp.maximum(i - 1, 0)]
    fresh = (i == 0) | (blk_e_ref[i] != prev)
    half = wd_b.shape[1] // 2

    def rows(ref, step):
        return ref.at[pl.ds(pl.multiple_of(step * bm, bm), bm), 0]

    def in_copy(step, sl):
        return pltpu.make_async_copy(rows(xs_ref, step), xbuf.at[sl], in_sem.at[sl])

    def out_copy(step, sl):
        return pltpu.make_async_copy(ybuf.at[sl], rows(ys_ref, step), out_sem.at[sl])

    def zero_copy(step):
        return pltpu.make_async_copy(zbuf, rows(ys_ref, step), zero_sem)

    @pl.when(i == 0)
    def _():
        zbuf[...] = jnp.zeros_like(zbuf)
        in_copy(0, 0).start()

    @pl.when(i + 1 < nused)
    def _():
        in_copy(i + 1, 1 - slot).start()

    @pl.when(fresh)
    def _():
        wg_b[...] = wg_ref[0].astype(BF16)
        wu_b[...] = wu_ref[0].astype(BF16)
        wd_b[...] = wd_ref[0].astype(BF16)

    @pl.when(i < nused)
    def _():
        in_copy(i, slot).wait()
        pl.when(i >= 2)(lambda: out_copy(i - 2, slot).wait())
        lo, hi = _unpack_rows(xbuf[slot])
        xb = jnp.concatenate([lo.astype(BF16), hi.astype(BF16)], axis=1)
        hid = (_silu(_dot(xb, wg_b[...])) * _dot(xb, wu_b[...])).astype(BF16)
        y = _dot(hid, wd_b[...])
        ybuf[slot] = _pack_rows(y[:, :half], y[:, half:])
        out_copy(i, slot).start()

    pl.when(i >= nused)(lambda: zero_copy(i).start())

    @pl.when(i == nsteps - 1)
    def _():
        pl.when(nused >= 2)(lambda: out_copy(nused - 2, nused % 2).wait())
        pl.when(nused >= 1)(lambda: out_copy(nused - 1, (nused - 1) % 2).wait())
        lax.fori_loop(nused, nsteps, lambda b, c: (zero_copy(b).wait(), c)[1], 0)


def _experts(xs, blk_e, nused, w_gate, w_up, w_down):
    cap, _, w = xs.shape
    _, d, de = w_gate.shape
    bm = MOE_ROWS
    wspec = lambda blk: pl.BlockSpec(blk, lambda i, be, nu: (be[i], 0, 0))
    grid_spec = pltpu.PrefetchScalarGridSpec(
        num_scalar_prefetch=2,
        grid=(cap // bm,),
        in_specs=[pl.BlockSpec(memory_space=pl.ANY), wspec((1, d, de)), wspec((1, d, de)), wspec((1, de, d))],
        out_specs=pl.BlockSpec(memory_space=pl.ANY),
        scratch_shapes=[
            pltpu.VMEM((d, de), BF16),
            pltpu.VMEM((d, de), BF16),
            pltpu.VMEM((de, d), BF16),
            pltpu.VMEM((2, bm, w), jnp.uint32),
            pltpu.VMEM((2, bm, w), jnp.uint32),
            pltpu.VMEM((bm, w), jnp.uint32),
            pltpu.SemaphoreType.DMA((2,)),
            pltpu.SemaphoreType.DMA((2,)),
            pltpu.SemaphoreType.DMA,
        ],
    )
    return pl.pallas_call(
        _expert_kernel,
        grid_spec=grid_spec,
        out_shape=jax.ShapeDtypeStruct((cap, 1, w), jnp.uint32),
        compiler_params=pltpu.CompilerParams(
            dimension_semantics=("arbitrary",), vmem_limit_bytes=VMEM_LIMIT),
        name="experts",
    )(blk_e, nused, xs, w_gate, w_up, w_down)


def _combine_kernel(dest_ref, ys_ref, x2_ref, rw_ref, eye_ref, nw_ref, o_ref, idx_ref, idx_sem, ybuf, ysem):
    i = pl.program_id(0)
    nsteps = pl.num_programs(0)
    tc, d = x2_ref.shape
    half = d // 2
    n_idx = 2 * tc

    def idx_copy(step):
        sl = step % 3
        off = pl.multiple_of(sl * n_idx, n_idx)
        return pltpu.make_async_copy(dest_ref.at[step], idx_ref.at[pl.ds(off, n_idx)], idx_sem.at[sl])

    def gather(step):
        base = (step % 3) * n_idx
        buf = ybuf.at[step % 2]
        sem = ysem.at[step % 2]

        for t in range(n_idx):
            pltpu.make_async_copy(ys_ref.at[idx_ref[base + t]], buf.at[pl.ds(t, 1)], sem).start(priority=t % 2)

    @pl.when(i == 0)
    def _():
        idx_copy(0).start()
        idx_copy(0).wait()
        gather(0)

        @pl.when(nsteps > 1)
        def _():
            idx_copy(1).start()

    @pl.when(i + 1 < nsteps)
    def _():
        idx_copy(i + 1).wait()

        @pl.when(i + 2 < nsteps)
        def _():
            idx_copy(i + 2).start()

        gather(i + 1)

    slot = i % 2
    pltpu.make_async_copy(ybuf.at[slot], ybuf.at[slot], ysem.at[slot]).wait()
    rw = rw_ref[...]
    rw_a = rw.astype(BF16)
    rw_b = (rw - rw_a.astype(F32)).astype(BF16)
    rw_c = (rw - rw_a.astype(F32) - rw_b.astype(F32)).astype(BF16)
    eye = eye_ref[...]
    wcol = _dot_nt(eye, rw_a) + (_dot_nt(eye, rw_b) + _dot_nt(eye, rw_c))
    lo1, hi1 = _unpack_rows(ybuf[slot, 0:tc])
    lo2, hi2 = _unpack_rows(ybuf[slot, tc:n_idx])
    w1 = wcol[:, 0:1]
    w2 = wcol[:, 1:2]
    z_lo = x2_ref[:, :half] + (lo1 * w1 + lo2 * w2)
    z_hi = x2_ref[:, half:] + (hi1 * w1 + hi2 * w2)
    ms = (jnp.sum(z_lo * z_lo, axis=-1, keepdims=True) + jnp.sum(z_hi * z_hi, axis=-1, keepdims=True)) / d
    scale = lax.rsqrt(ms + EPS)
    o_ref[:, :half] = z_lo * scale * nw_ref[:, :half]
    o_ref[:, half:] = z_hi * scale * nw_ref[:, half:]


def _combine(x2, ys, dest_tiles, rw, eye, norm_w):
    n, d = x2.shape
    nt, n_idx = dest_tiles.shape
    tc = n_idx // 2
    w = ys.shape[-1]
    return pl.pallas_call(
        _combine_kernel,
        grid=(nt,),
        in_specs=[pl.BlockSpec(memory_space=pl.ANY)] * 2 + [
            pl.BlockSpec((tc, d), lambda i: (i, 0)),
            pl.BlockSpec((8, tc), lambda i: (0, i)),
            pl.BlockSpec((tc, tc), lambda i: (0, 0)),
            pl.BlockSpec((1, d), lambda i: (0, 0)),
        ],
        out_specs=pl.BlockSpec((tc, d), lambda i: (i, 0)),
        out_shape=jax.ShapeDtypeStruct((n, d), F32),
        scratch_shapes=[
            pltpu.SMEM((3 * n_idx,), jnp.int32),
            pltpu.SemaphoreType.DMA((3,)),
            pltpu.VMEM((2, n_idx, w), jnp.uint32),
            pltpu.SemaphoreType.DMA((2,)),
        ],
        compiler_params=pltpu.CompilerParams(
            dimension_semantics=("arbitrary",), vmem_limit_bytes=VMEM_LIMIT),
        name="combine",
    )(dest_tiles, ys, x2, rw, eye, norm_w)


def _rope_column_order():
    half = RET_DK // 2
    first = [h * RET_DK + j for h in range(RET_HEADS) for j in range(half)]
    second = [h * RET_DK + half + j for h in range(RET_HEADS) for j in range(half)]
    return np.array(first + second, dtype=np.int32)


def _layer(xf, mem, batch, seq, norm_mix_w, w_in, ret_norm_w, ml_conv_w, ml_conv_b, ml_gate_b, ml_norm_w,
           w_out, norm_xa_w, norm_mem_w, xa_wq, xa_wkv, xa_wo, norm_moe_w, moe_w_group, moe_b_group,
           moe_w_router, moe_b_router, moe_w_gate, moe_w_up, moe_w_down, final_norm_w):
    n, d = xf.shape
    perm = _rope_column_order()
    cols = np.concatenate([perm, RET_QK + perm, np.arange(2 * RET_QK, MAIN_WIDTH)])
    w_main = w_in[:, cols].astype(BF16)
    w_if = w_in[:, MAIN_WIDTH:].astype(BF16)
    tabs = {k_: jnp.asarray(v_) for k_, v_ in _mixer_tables(seq).items()}
    proj, g, gt = _inproj(xf, norm_mix_w.reshape(1, d), w_main, w_if, w_if.T, tabs["cos"], tabs["sin"], ml_conv_w,
                          ml_conv_b.reshape(1, 2 * ML_QK), seq)
    mixed = _mixer(proj, g, gt, tabs, ret_norm_w.reshape(1, RET_V), ml_norm_w.reshape(1, ML_V), ml_gate_b,
                   batch, seq)

    kmem, vmem = _memkv(mem, norm_mem_w.reshape(1, d), xa_wkv.astype(BF16))

    w_route_t = jnp.concatenate(
        [moe_w_router.T, moe_w_group.T, jnp.zeros((ROUTE_ROWS - N_EXPERTS - N_GROUPS, d), F32)], axis=0)
    b_route = jnp.concatenate(
        [moe_b_router, moe_b_group, jnp.zeros((ROUTE_ROWS - N_EXPERTS - N_GROUPS,), F32)]).reshape(ROUTE_ROWS, 1)
    tm = TOKEN_TILE
    sut = jnp.asarray(np.triu(np.ones((tm, tm), np.float32), 1), dtype=BF16)
    x2, h3, ri, rw, cnt = _attn_route(xf, mixed, kmem, vmem, w_out.astype(BF16), norm_xa_w.reshape(1, d),
                                      xa_wq.astype(BF16), xa_wo.astype(BF16), norm_moe_w.reshape(1, d),
                                      w_route_t, b_route, sut, batch, seq)

    bm = MOE_ROWS
    counts = cnt[:, 0].astype(jnp.int32)
    padded = (counts + bm - 1) // bm * bm
    pends = jnp.cumsum(padded)
    pstarts = pends - padded
    expert = ri[0:TOP_K]
    onehot = expert[None] == jnp.arange(N_EXPERTS, dtype=jnp.int32)[:, None, None]
    dest = jnp.sum(jnp.where(onehot, pstarts[:, None, None], 0), axis=0) + ri[TOP_K:2 * TOP_K]
    cap = n * TOP_K + N_EXPERTS * bm
    nblk = cap // bm
    blk_start = jnp.arange(nblk, dtype=jnp.int32) * bm
    blk_e = jnp.minimum(jnp.sum(blk_start[:, None] >= pends[None, :], axis=1), N_EXPERTS - 1).astype(jnp.int32)
    nused = (pends[-1] // bm).astype(jnp.int32).reshape(1)
    zpos = jnp.where(padded > counts, pends - bm, -1).astype(jnp.int32)
    zpos = jnp.concatenate([zpos, nused])

    def tiles(rows):
        return dest.reshape(TOP_K, n // rows, rows).transpose(1, 0, 2).reshape(n // rows, TOP_K * rows)

    xs = _dispatch(h3, tiles(DISPATCH_TILE), zpos, cap)
    ys = _experts(xs, blk_e, nused, moe_w_gate, moe_w_up, moe_w_down)
    eye = jnp.asarray(np.eye(COMBINE_TILE, dtype=np.float32), dtype=BF16)
    return _combine(x2, ys, tiles(COMBINE_TILE), rw, eye, final_norm_w.reshape(1, d))


def kernel(x, mem, norm_mix_w, w_in, ret_norm_w, ml_conv_w, ml_conv_b, ml_gate_b, ml_norm_w, w_out, norm_xa_w, norm_mem_w, xa_wq, xa_wkv, xa_wo, norm_moe_w, moe_w_group, moe_b_group, moe_w_router, moe_b_router, moe_w_gate, moe_w_up, moe_w_down, norm_final_w):
    batch, seq, d = x.shape
    depth = w_in.shape[0]
    assert depth == 1, "the final norm is fused into the last layer's combine kernel"
    l = 0
    out = _layer(x.reshape(batch * seq, d), mem, batch, seq, norm_mix_w[l], w_in[l], ret_norm_w[l], ml_conv_w[l],
                 ml_conv_b[l], ml_gate_b[l], ml_norm_w[l], w_out[l], norm_xa_w[l], norm_mem_w[l], xa_wq[l],
                 xa_wkv[l], xa_wo[l], norm_moe_w[l], moe_w_group[l], moe_b_group[l], moe_w_router[l],
                 moe_b_router[l], moe_w_gate[l], moe_w_up[l], moe_w_down[l], norm_final_w)
    return out.reshape(batch, seq, d)
```

```python
import functools

import numpy as np
import jax
import jax.numpy as jnp
from jax import lax
from jax.experimental import pallas as pl
from jax.experimental.pallas import tpu as pltpu

F32 = jnp.float32
BF16 = jnp.bfloat16
HIGHEST = lax.Precision.HIGHEST

CHUNK = 128
RET_HEADS = 4
RET_DK = 64
RET_DV = 128
ML_HEADS = 4
ML_DK = 128
ML_DV = 128
CONV_W = 4
XA_HEADS = 4
N_GROUPS = 4
EXP_PER_GROUP = 8
N_EXPERTS = N_GROUPS * EXP_PER_GROUP
TOP_K = 2
ROPE_BASE = 10000.0
EPS = 1e-6

RET_QK = RET_HEADS * RET_DK
RET_V = RET_HEADS * RET_DV
ML_QK = ML_HEADS * ML_DK
ML_V = ML_HEADS * ML_DV
OFF_RQ = 0
OFF_RK = OFF_RQ + RET_QK
OFF_RV = OFF_RK + RET_QK
OFF_RG = OFF_RV + RET_V
OFF_MQK = OFF_RG + RET_V
OFF_MV = OFF_MQK + 2 * ML_QK
OFF_MO = OFF_MV + ML_V
MAIN_WIDTH = OFF_MO + ML_V
N_GATES = 2 * ML_HEADS

ROUTE_ROWS = 40
TOKEN_TILE = 1024
MOE_ROWS = 512
DISPATCH_TILE = 1024
COMBINE_TILE = 512
MIXER_BATCHES = 4
ATTN_GROUPS = 2
VMEM_LIMIT = 56 * 1024 * 1024


def _dot(a, b):
    return jnp.dot(a, b, preferred_element_type=F32)


def _dot_nt(a, b, precision=None):
    return lax.dot_general(a, b, (((1,), (1,)), ((), ())), preferred_element_type=F32, precision=precision)


def _dot_tn(a, b):
    return lax.dot_general(a, b, (((0,), (0,)), ((), ())), preferred_element_type=F32)


def _rms(x, w):
    return x * lax.rsqrt(jnp.mean(x * x, axis=-1, keepdims=True) + EPS) * w


def _sigmoid(x):
    return 1.0 / (1.0 + jnp.exp(-x))


def _silu(x):
    return x * _sigmoid(x)


def _log_sigmoid(x):
    return jnp.minimum(x, 0.0) - jnp.log1p(jnp.exp(-jnp.abs(x)))


def _head_norms(ts, mean_w):
    mu = [_dot(t.astype(BF16), mean_w) for t in ts]
    dl = [t - m for t, m in zip(ts, mu)]
    var = [_dot((d * d).astype(BF16), mean_w) for d in dl]
    return [d * lax.rsqrt(v + EPS) for d, v in zip(dl, var)]


def _pack_rows(lo, hi):
    def bits(t):
        return lax.bitcast_convert_type(t.astype(BF16), jnp.uint16).astype(jnp.uint32)
    return bits(lo) | (bits(hi) << 16)


def _unpack_rows(u):
    lo = lax.bitcast_convert_type(u << 16, F32)
    hi = lax.bitcast_convert_type(u & jnp.uint32(0xFFFF0000), F32)
    return lo, hi


def _inproj_kernel(tiles_per_seq, x_ref, nw_ref, w_ref, wif_ref, wift_ref, cos_ref, sin_ref, convw_ref, convb_ref,
                   proj_ref, g_ref, gt_ref, carry_ref):
    tm = x_ref.shape[0]

    @pl.when(pl.program_id(0) == 0)
    def _():
        carry_ref[...] = jnp.zeros_like(carry_ref)

    h = _rms(x_ref[...], nw_ref[...]).astype(BF16)

    def mm(off, width):
        return _dot(h, w_ref[:, off:off + width])

    def rotary(qk):
        cos = cos_ref[...]
        sin = sin_ref[...]
        half = RET_QK // 2
        for off, scale in ((OFF_RQ, None), (OFF_RK, RET_DK ** -0.5)):
            t1 = qk[:, off - OFF_RQ:off - OFF_RQ + half]
            t2 = qk[:, off - OFF_RQ + half:off - OFF_RQ + 2 * half]
            r1 = t1 * cos - t2 * sin
            r2 = t1 * sin + t2 * cos
            if scale is not None:
                r1, r2 = r1 * scale, r2 * scale
            proj_ref[:, off:off + half] = r1.astype(BF16)
            proj_ref[:, off + half:off + 2 * half] = r2.astype(BF16)

    def conv_silu(part, scale, cur):
        c0 = part * ML_QK
        first = (pl.program_id(0) % tiles_per_seq) == 0
        row8 = lax.broadcasted_iota(jnp.int32, (8, ML_QK), 0)
        prev = jnp.where(first, 0.0, carry_ref[:, c0:c0 + ML_QK])
        acc = cur * convw_ref[CONV_W - 1:CONV_W, c0:c0 + ML_QK] + convb_ref[:, c0:c0 + ML_QK]
        for s in range(1, CONV_W):
            rolled = pltpu.roll(cur, s, 0)
            head8 = jnp.where(row8 < s, pltpu.roll(prev, s, 0), rolled[0:8])
            shifted = jnp.concatenate([head8, rolled[8:]], axis=0)
            acc = acc + shifted * convw_ref[CONV_W - 1 - s:CONV_W - s, c0:c0 + ML_QK]
        carry_ref[:, c0:c0 + ML_QK] = cur[tm - 8:tm]
        act = _silu(acc) if scale is None else _silu(acc) * scale
        proj_ref[:, OFF_MQK + c0:OFF_MQK + c0 + ML_QK] = act.astype(BF16)

    def store(off, width, fn=None):
        def ep(t):
            proj_ref[:, off:off + width] = (t if fn is None else fn(t)).astype(BF16)
        return ep

    rotary(mm(OFF_RQ, 2 * RET_QK))
    store(OFF_RV, RET_V)(mm(OFF_RV, RET_V))
    store(OFF_RG, RET_V, _silu)(mm(OFF_RG, RET_V))
    conv_silu(0, None, mm(OFF_MQK, ML_QK))
    conv_silu(1, ML_DK ** -0.5, mm(OFF_MQK + ML_QK, ML_QK))
    store(OFF_MV, ML_V)(mm(OFF_MV, ML_V))
    store(OFF_MO, ML_V, _sigmoid)(mm(OFF_MO, ML_V))
    g_ref[...] = _dot(h, wif_ref[...])
    gt_ref[...] = _dot_nt(wift_ref[...], h)


def _inproj(xf, norm_w, w_main, w_if, w_ift, cos, sin, conv_w, conv_b, seq):
    n, d = xf.shape
    tm = TOKEN_TILE
    tiles_per_seq = seq // tm
    return pl.pallas_call(
        functools.partial(_inproj_kernel, tiles_per_seq),
        grid=(n // tm,),
        in_specs=[
            pl.BlockSpec((tm, d), lambda i: (i, 0)),
            pl.BlockSpec((1, d), lambda i: (0, 0)),
            pl.BlockSpec((d, MAIN_WIDTH), lambda i: (0, 0)),
            pl.BlockSpec((d, N_GATES), lambda i: (0, 0)),
            pl.BlockSpec((N_GATES, d), lambda i: (0, 0)),
            pl.BlockSpec((tm, RET_QK // 2), lambda i: (i % tiles_per_seq, 0)),
            pl.BlockSpec((tm, RET_QK // 2), lambda i: (i % tiles_per_seq, 0)),
            pl.BlockSpec((CONV_W, 2 * ML_QK), lambda i: (0, 0)),
            pl.BlockSpec((1, 2 * ML_QK), lambda i: (0, 0)),
        ],
        out_specs=[
            pl.BlockSpec((tm, MAIN_WIDTH), lambda i: (i, 0)),
            pl.BlockSpec((tm, N_GATES), lambda i: (i, 0)),
            pl.BlockSpec((N_GATES, tm), lambda i: (0, i)),
        ],
        out_shape=[
            jax.ShapeDtypeStruct((n, MAIN_WIDTH), BF16),
            jax.ShapeDtypeStruct((n, N_GATES), F32),
            jax.ShapeDtypeStruct((N_GATES, n), F32),
        ],
        scratch_shapes=[pltpu.VMEM((8, 2 * ML_QK), F32)],
        compiler_params=pltpu.CompilerParams(
            dimension_semantics=("arbitrary",), vmem_limit_bytes=VMEM_LIMIT),
        name="inproj",
    )(xf, norm_w, w_main, w_if, w_ift, cos, sin, conv_w, conv_b)


def _mixer_kernel(*refs):
    @pl.when(pl.program_id(1) == 0)
    def _():
        for state_ref in refs[-4:]:
            state_ref[...] = jnp.zeros_like(state_ref)

    for bi in range(refs[0].shape[0]):
        _mixer_one(bi, *refs)


def _mixer_one(bi, proj_ref, g_ref, gt_ref, qdec_ref, kdec_ref, dmat_ref,
               bmask_ref, cdec_ref, hmask_ref, tril_ref, triu_ref, ones_ref, retw_ref, mlw_ref, gbc_ref, gbr_ref,
               out_ref, r_ref, c_ref, n_ref, m_ref):
    L = CHUNK
    proj_ref, g_ref, gt_ref, out_ref = proj_ref.at[bi], g_ref.at[bi], gt_ref.at[bi], out_ref.at[bi]
    r_ref, c_ref, n_ref, m_ref = r_ref.at[bi], c_ref.at[bi], n_ref.at[bi], m_ref.at[bi]
    mean_w = ones_ref[...]

    q = proj_ref[:, OFF_RQ:OFF_RQ + RET_QK].astype(F32)
    k_b = proj_ref[:, OFF_RK:OFF_RK + RET_QK]
    k = k_b.astype(F32)
    v = proj_ref[:, OFF_RV:OFF_RV + RET_V]
    r_prev = r_ref[...]
    cross = _dot((q * qdec_ref[...]).astype(BF16), r_prev.astype(BF16))
    kv = _dot_tn((k * kdec_ref[...]).astype(BF16), v) * bmask_ref[...]
    r_ref[...] = cdec_ref[...] * r_prev + kv
    RH = range(RET_HEADS)
    sc = [_dot_nt((q * hmask_ref[h:h + 1, :]).astype(BF16), k_b) for h in RH]
    sc = [(sc[h] * dmat_ref[h]).astype(BF16) for h in RH]
    tot = [_dot(sc[h], v[:, h * RET_DV:(h + 1) * RET_DV]) + cross[:, h * RET_DV:(h + 1) * RET_DV] for h in RH]
    ret = jnp.concatenate(_head_norms(tot, mean_w), axis=1) * retw_ref[...]
    ret = ret * proj_ref[:, OFF_RG:OFF_RG + RET_V].astype(F32)
    out_ref[:, 0:RET_V] = ret.astype(BF16)

    mq = proj_ref[:, OFF_MQK:OFF_MQK + ML_QK]
    mk = proj_ref[:, OFF_MQK + ML_QK:OFF_MQK + 2 * ML_QK]
    mv = proj_ref[:, OFF_MV:OFF_MV + ML_V]

    gc = g_ref[...] + gbc_ref[...]
    gr = gt_ref[...] + gbr_ref[...]
    b_c = jnp.dot(tril_ref[...], _log_sigmoid(gc), preferred_element_type=F32, precision=HIGHEST)
    b_r = jnp.dot(_log_sigmoid(gr), triu_ref[...], preferred_element_type=F32, precision=HIGHEST)
    causal = (lax.broadcasted_iota(jnp.int32, (L, L), 0) >= lax.broadcasted_iota(jnp.int32, (L, L), 1))
    MH = range(ML_HEADS)
    bc = [b_c[:, ML_HEADS + h:ML_HEADS + h + 1] for h in MH]
    br = [b_r[ML_HEADS + h:ML_HEADS + h + 1, :] for h in MH]
    igc = [gc[:, h:h + 1] for h in MH]
    igr = [gr[h:h + 1, :] for h in MH]
    btot = [br[h][:, L - 1:L] for h in MH]
    qh_b = [mq[:, h * ML_DK:(h + 1) * ML_DK] for h in MH]
    kh_b = [mk[:, h * ML_DK:(h + 1) * ML_DK] for h in MH]
    vh = [mv[:, h * ML_DV:(h + 1) * ML_DV] for h in MH]
    c_prev = [c_ref[h] for h in MH]
    n_prev = [n_ref[h][0:1, :] for h in MH]
    m_prev = [m_ref[h][0:1, 0:1] for h in MH]
    s_raw = [_dot_nt(qh_b[h], kh_b[h]) for h in MH]
    qc = [_dot(qh_b[h], c_prev[h].astype(BF16)) for h in MH]
    log_d = [jnp.where(causal, bc[h] - br[h] + igr[h], -jnp.inf) for h in MH]
    m_intra = [jnp.max(log_d[h], axis=1, keepdims=True) for h in MH]
    m_loc = [jnp.max(btot[h] - br[h] + igr[h], axis=1, keepdims=True) for h in MH]
    kw = [kh_b[h].astype(F32) * jnp.exp(btot[h] - bc[h] + igc[h] - m_loc[h]) for h in MH]
    kv_loc = [_dot_tn(kw[h].astype(BF16), vh[h]) for h in MH]
    n_loc = [jnp.sum(kw[h], axis=0, keepdims=True) for h in MH]
    m_inter = [bc[h] + m_prev[h] for h in MH]
    m_t = [jnp.maximum(m_intra[h], m_inter[h]) for h in MH]
    s_mat = [s_raw[h] * jnp.exp(log_d[h] - m_t[h]) for h in MH]
    inter = [jnp.exp(m_inter[h] - m_t[h]) for h in MH]
    num = [_dot(s_mat[h].astype(BF16), vh[h]) + inter[h] * qc[h] for h in MH]
    den = [jnp.sum(s_mat[h], axis=1, keepdims=True)
           + inter[h] * jnp.sum(qh_b[h].astype(F32) * n_prev[h], axis=1, keepdims=True) for h in MH]
    hh = [num[h] / jnp.maximum(jnp.abs(den[h]), jnp.exp(-m_t[h])) for h in MH]
    for h in MH:
        m_new = jnp.maximum(btot[h] + m_prev[h], m_loc[h])
        s_old = jnp.exp(btot[h] + m_prev[h] - m_new)
        s_loc = jnp.exp(m_loc[h] - m_new)
        c_ref[h] = s_old * c_prev[h] + s_loc * kv_loc[h]
        n_ref[h] = jnp.broadcast_to(s_old * n_prev[h] + s_loc * n_loc[h], (8, ML_DK))
        m_ref[h] = jnp.broadcast_to(m_new, (8, 128))
    ml = jnp.concatenate(_head_norms(hh, mean_w), axis=1) * mlw_ref[...]
    ml = ml * proj_ref[:, OFF_MO:OFF_MO + ML_V].astype(F32)
    out_ref[:, RET_V:RET_V + ML_V] = ml.astype(BF16)


def _mixer_tables(seq):
    L = CHUNK
    half = RET_DK // 2
    inv = ROPE_BASE ** (-np.arange(half, dtype=np.float64) / half)
    ang = np.arange(seq, dtype=np.float64)[:, None] * inv[None, :].astype(np.float32).astype(np.float64)
    cos = np.tile(np.cos(ang), (1, RET_HEADS)).astype(np.float32)
    sin = np.tile(np.sin(ang), (1, RET_HEADS)).astype(np.float32)
    log_g = np.log1p(-np.exp2(-5.0 - np.arange(RET_HEADS, dtype=np.float64)))
    n = np.arange(L, dtype=np.float64)
    lane_head = (np.arange(RET_QK) % (RET_QK // 2)) // half
    qdec = np.exp((n + 1)[:, None] * log_g[lane_head][None, :]).astype(np.float32)
    kdec = np.exp((L - 1 - n)[:, None] * log_g[lane_head][None, :]).astype(np.float32)
    diff = n[:, None] - n[None, :]
    dmat = np.where(diff >= 0, np.exp(log_g[:, None, None] * np.maximum(diff, 0.0)[None]), 0.0).astype(np.float32)
    col_head = np.arange(RET_V) // RET_DV
    bmask = (lane_head[:, None] == col_head[None, :]).astype(np.float32)
    cdec = np.exp(L * log_g[col_head])[None, :].astype(np.float32)
    hmask = (lane_head[None, :] == np.arange(RET_HEADS)[:, None]).astype(np.float32)
    hmask = np.concatenate([hmask, np.zeros((8 - RET_HEADS, RET_QK), np.float32)], axis=0)
    tril = np.tril(np.ones((L, L), np.float32))
    ones = np.full((RET_DV, RET_DV), 1.0 / RET_DV, np.float32)
    return dict(cos=cos, sin=sin, qdec=qdec, kdec=kdec, dmat=dmat, bmask=bmask, cdec=cdec, hmask=hmask,
                tril=tril, triu=np.ascontiguousarray(tril.T), ones=ones)


def _mixer(proj, g, gt, tabs, ret_norm_w, ml_norm_w, gate_b, batch, seq):
    L = CHUNK
    nc = seq // L
    n = batch * seq
    nb = MIXER_BATCHES if batch % MIXER_BATCHES == 0 else 1
    proj = proj.reshape(batch, seq, MAIN_WIDTH)
    g = g.reshape(batch, seq, N_GATES)
    gt = gt.reshape(N_GATES, batch, seq).transpose(1, 0, 2)
    const2 = lambda b, c: (0, 0)
    const3 = lambda b, c: (0, 0, 0)
    tok = lambda b, c: (b, c, 0)
    in_specs = [
        pl.BlockSpec((nb, L, MAIN_WIDTH), tok),
        pl.BlockSpec((nb, L, N_GATES), tok),
        pl.BlockSpec((nb, N_GATES, L), lambda b, c: (b, 0, c)),
        pl.BlockSpec((L, RET_QK), const2),
        pl.BlockSpec((L, RET_QK), const2),
        pl.BlockSpec((RET_HEADS, L, L), const3),
        pl.BlockSpec((RET_QK, RET_V), const2),
        pl.BlockSpec((1, RET_V), const2),
        pl.BlockSpec((8, RET_QK), const2),
        pl.BlockSpec((L, L), const2),
        pl.BlockSpec((L, L), const2),
        pl.BlockSpec((RET_DV, RET_DV), const2),
        pl.BlockSpec((1, RET_V), const2),
        pl.BlockSpec((1, ML_V), const2),
        pl.BlockSpec((1, N_GATES), const2),
        pl.BlockSpec((N_GATES, 1), const2),
    ]
    return pl.pallas_call(
        _mixer_kernel,
        grid=(batch // nb, nc),
        in_specs=in_specs,
        out_specs=pl.BlockSpec((nb, L, RET_V + ML_V), tok),
        out_shape=jax.ShapeDtypeStruct((batch, seq, RET_V + ML_V), BF16),
        scratch_shapes=[
            pltpu.VMEM((nb, RET_QK, RET_V), F32),
            pltpu.VMEM((nb, ML_HEADS, ML_DK, ML_DV), F32),
            pltpu.VMEM((nb, ML_HEADS, 8, ML_DK), F32),
            pltpu.VMEM((nb, ML_HEADS, 8, 128), F32),
        ],
        compiler_params=pltpu.CompilerParams(
            dimension_semantics=("arbitrary", "arbitrary"), vmem_limit_bytes=VMEM_LIMIT),
        name="mixer",
    )(proj, g, gt, tabs["qdec"], tabs["kdec"], tabs["dmat"], tabs["bmask"],
      tabs["cdec"], tabs["hmask"], tabs["tril"], tabs["triu"], tabs["ones"].astype(BF16), ret_norm_w, ml_norm_w,
      gate_b.reshape(1, N_GATES), gate_b.reshape(N_GATES, 1)).reshape(n, RET_V + ML_V)


def _memkv_kernel(mem_ref, nw_ref, wkv_ref, k_ref, v_ref):
    d = mem_ref.shape[-1]
    mn = _rms(mem_ref[0], nw_ref[...]).astype(BF16)
    k_ref[0] = _dot(mn, wkv_ref[:, :d]).astype(BF16)
    v_ref[0] = _dot(mn, wkv_ref[:, d:]).astype(BF16)


def _memkv(mem, norm_w, wkv):
    b, m, d = mem.shape
    return pl.pallas_call(
        _memkv_kernel,
        grid=(b,),
        in_specs=[
            pl.BlockSpec((1, m, d), lambda i: (i, 0, 0)),
            pl.BlockSpec((1, d), lambda i: (0, 0)),
            pl.BlockSpec((d, 2 * d), lambda i: (0, 0)),
        ],
        out_specs=[pl.BlockSpec((1, m, d), lambda i: (i, 0, 0))] * 2,
        out_shape=[jax.ShapeDtypeStruct((b, m, d), BF16)] * 2,
        compiler_params=pltpu.CompilerParams(
            dimension_semantics=("arbitrary",), vmem_limit_bytes=VMEM_LIMIT),
        name="memkv",
    )(mem, norm_w, wkv)


def _attn_route_kernel(x_ref, mix_ref, k_ref, v_ref, wout_ref, nxa_ref, wq_ref, wo_ref, nmoe_ref,
                       wr_ref, wrlo_ref, br_ref, sut_ref,
                       x2_ref, h3_ref, ri_ref, rw_ref, cnt_ref, carry_ref):
    tm, d = x_ref.shape
    dh = d // XA_HEADS

    @pl.when((pl.program_id(0) == 0) & (pl.program_id(1) == 0))
    def _():
        carry_ref[...] = jnp.zeros_like(carry_ref)

    groups = [slice(g * (tm // ATTN_GROUPS), (g + 1) * (tm // ATTN_GROUPS)) for g in range(ATTN_GROUPS)]
    x1 = [x_ref[s, :] + _dot(mix_ref[s, :], wout_ref[...]) for s in groups]
    h2 = [_rms(t, nxa_ref[...]).astype(BF16) for t in x1]
    q = [_dot(t, wq_ref[...]).astype(BF16) for t in h2]
    o = []
    for qg in q:
        heads = []
        for h in range(XA_HEADS):
            logits = _dot_nt(qg[:, h * dh:(h + 1) * dh], k_ref[0, :, h * dh:(h + 1) * dh]) * (dh ** -0.5)
            mx = jnp.max(logits, axis=-1, keepdims=True)
            e = jnp.exp(logits - mx)
            p = (e / jnp.sum(e, axis=-1, keepdims=True)).astype(BF16)
            heads.append(_dot(p, v_ref[0, :, h * dh:(h + 1) * dh]).astype(BF16))
        o.append(jnp.concatenate(heads, axis=1))
    x2 = [a + _dot(b, wo_ref[...]) for a, b in zip(x1, o)]
    for s, t in zip(groups, x2):
        x2_ref[s, :] = t
    h3 = [_rms(t, nmoe_ref[...]) for t in x2]
    for s, t in zip(groups, h3):
        h3_ref[s, :] = _pack_rows(t[:, :d // 2], t[:, d // 2:])

    lts = []
    for t in h3:
        t_hi = t.astype(BF16)
        t_lo = (t - t_hi.astype(F32)).astype(BF16)
        lts.append(_dot_nt(wr_ref[...], t_hi) + (_dot_nt(wr_ref[...], t_lo) + _dot_nt(wrlo_ref[...], t_hi)))
    lt = jnp.concatenate(lts, axis=1) + br_ref[...]
    gl = lt[N_EXPERTS:N_EXPERTS + N_GROUPS]
    gmax = jnp.max(gl, axis=0, keepdims=True)
    g_w = 1.0 / jnp.sum(jnp.exp(gl - gmax), axis=0, keepdims=True)
    giota = lax.broadcasted_iota(jnp.int32, gl.shape, 0)
    g_sel = jnp.min(jnp.where(gl == gmax, giota, N_GROUPS), axis=0, keepdims=True)
    el = lt[0:N_EXPERTS]
    eiota = lax.broadcasted_iota(jnp.int32, el.shape, 0)
    in_grp = (eiota // EXP_PER_GROUP) == g_sel
    elm = jnp.where(in_grp, el, -jnp.inf)
    m1 = jnp.max(elm, axis=0, keepdims=True)
    esum = jnp.sum(jnp.where(in_grp, jnp.exp(el - m1), 0.0), axis=0, keepdims=True)
    i1 = jnp.min(jnp.where(elm == m1, eiota, N_EXPERTS), axis=0, keepdims=True)
    elm2 = jnp.where(eiota == i1, -jnp.inf, elm)
    m2 = jnp.max(elm2, axis=0, keepdims=True)
    i2 = jnp.min(jnp.where(elm2 == m2, eiota, N_EXPERTS), axis=0, keepdims=True)
    p1 = 1.0 / esum
    p2 = jnp.exp(m2 - m1) / esum
    psum = p1 + p2
    w1 = g_w * (p1 / psum)
    w2 = g_w * (p2 / psum)

    oh1 = (eiota == i1).astype(F32)
    oh2 = (eiota == i2).astype(F32)
    cnt = oh1 + oh2
    base = carry_ref[:, 0:1] + _dot(cnt.astype(BF16), sut_ref[...])
    r1 = jnp.sum(oh1 * base, axis=0, keepdims=True)
    r2 = jnp.sum(oh2 * base, axis=0, keepdims=True)
    new_carry = carry_ref[...] + jnp.sum(cnt, axis=1, keepdims=True)
    carry_ref[...] = new_carry
    cnt_ref[...] = new_carry

    zi = jnp.zeros((4, tm), jnp.int32)
    ri_ref[...] = jnp.concatenate([i1, i2, r1.astype(jnp.int32), r2.astype(jnp.int32), zi], axis=0)
    rw_ref[...] = jnp.concatenate([w1, w2, jnp.zeros((6, tm), F32)], axis=0)


def _attn_route(xf, mixed, kmem, vmem, w_out, norm_xa_w, wq, wo, norm_moe_w, w_route_t, b_route, sut,
                batch, seq):
    n, d = xf.shape
    w_route_hi = w_route_t.astype(BF16)
    w_route_lo = (w_route_t - w_route_hi.astype(F32)).astype(BF16)
    tm = TOKEN_TILE
    nt = seq // tm
    m = kmem.shape[1]
    tok = lambda b, t: (b * nt + t, 0)
    lane_tok = lambda b, t: (0, b * nt + t)
    const2 = lambda b, t: (0, 0)
    return pl.pallas_call(
        _attn_route_kernel,
        grid=(batch, nt),
        in_specs=[
            pl.BlockSpec((tm, d), tok),
            pl.BlockSpec((tm, d), tok),
            pl.BlockSpec((1, m, d), lambda b, t: (b, 0, 0)),
            pl.BlockSpec((1, m, d), lambda b, t: (b, 0, 0)),
            pl.BlockSpec((d, d), const2),
            pl.BlockSpec((1, d), const2),
            pl.BlockSpec((d, d), const2),
            pl.BlockSpec((d, d), const2),
            pl.BlockSpec((1, d), const2),
            pl.BlockSpec((ROUTE_ROWS, d), const2),
            pl.BlockSpec((ROUTE_ROWS, d), const2),
            pl.BlockSpec((ROUTE_ROWS, 1), const2),
            pl.BlockSpec((tm, tm), const2),
        ],
        out_specs=[
            pl.BlockSpec((tm, d), tok),
            pl.BlockSpec((tm, d // 2), tok),
            pl.BlockSpec((8, tm), lane_tok),
            pl.BlockSpec((8, tm), lane_tok),
            pl.BlockSpec((N_EXPERTS, 128), const2),
        ],
        out_shape=[
            jax.ShapeDtypeStruct((n, d), F32),
            jax.ShapeDtypeStruct((n, d // 2), jnp.uint32),
            jax.ShapeDtypeStruct((8, n), jnp.int32),
            jax.ShapeDtypeStruct((8, n), F32),
            jax.ShapeDtypeStruct((N_EXPERTS, 128), F32),
        ],
        scratch_shapes=[pltpu.VMEM((N_EXPERTS, 128), F32)],
        compiler_params=pltpu.CompilerParams(
            dimension_semantics=("arbitrary", "arbitrary"), vmem_limit_bytes=VMEM_LIMIT),
        name="attn_route",
    )(xf, mixed, kmem, vmem, w_out, norm_xa_w, wq, wo, norm_moe_w, w_route_hi, w_route_lo, b_route, sut)


def _dispatch_kernel(zpos_ref, dest_ref, h_ref, xs_ref, idx_ref, idx_sem, row_sem, zero_ref, zero_sem):
    i = pl.program_id(0)
    nsteps = pl.num_programs(0)
    td = h_ref.shape[0]
    bm = zero_ref.shape[0]
    slot = i % 2

    def idx_copy(step, sl):
        off = pl.multiple_of(sl * (2 * td), 2 * td)
        return pltpu.make_async_copy(dest_ref.at[step], idx_ref.at[pl.ds(off, 2 * td)], idx_sem.at[sl])

    @pl.when(i == 0)
    def _():
        zero_ref[...] = jnp.zeros_like(zero_ref)

        def zero_copy(e):
            return pltpu.make_async_copy(zero_ref, xs_ref.at[pl.ds(pl.multiple_of(zpos_ref[e], bm), bm), 0], zero_sem)

        def tail_copy(b):
            return pltpu.make_async_copy(zero_ref, xs_ref.at[pl.ds(pl.multiple_of(b * bm, bm), bm), 0], zero_sem)

        nused = zpos_ref[N_EXPERTS]
        nblk = xs_ref.shape[0] // bm
        for e in range(N_EXPERTS):
            pl.when(zpos_ref[e] >= 0)(lambda e=e: zero_copy(e).start())
        lax.fori_loop(nused, nblk, lambda b, c: (tail_copy(b).start(), c)[1], 0)
        for e in range(N_EXPERTS):
            pl.when(zpos_ref[e] >= 0)(lambda e=e: zero_copy(e).wait())
        lax.fori_loop(nused, nblk, lambda b, c: (tail_copy(b).wait(), c)[1], 0)
        idx_copy(0, 0).start()

    idx_copy(i, slot).wait()

    @pl.when(i + 1 < nsteps)
    def _():
        idx_copy(i + 1, 1 - slot).start()

    base = slot * (2 * td)

    for t in range(td):
        for k in range(TOP_K):
            pltpu.make_async_copy(h_ref.at[pl.ds(t, 1)], xs_ref.at[idx_ref[base + k * td + t]],
                                  row_sem).start(priority=t % 2)
    for _ in range(TOP_K):
        pltpu.make_async_copy(xs_ref.at[pl.ds(0, td)], xs_ref.at[pl.ds(0, td)], row_sem).wait()


def _dispatch(h3p, dest_tiles, zpos, cap):
    w = h3p.shape[-1]
    nt, td2 = dest_tiles.shape
    td = td2 // 2
    grid_spec = pltpu.PrefetchScalarGridSpec(
        num_scalar_prefetch=1,
        grid=(nt,),
        in_specs=[
            pl.BlockSpec(memory_space=pl.ANY),
            pl.BlockSpec((td, w), lambda i, zp: (i, 0)),
        ],
        out_specs=pl.BlockSpec(memory_space=pl.ANY),
        scratch_shapes=[
            pltpu.SMEM((2 * td2,), jnp.int32),
            pltpu.SemaphoreType.DMA((2,)),
            pltpu.SemaphoreType.DMA,
            pltpu.VMEM((MOE_ROWS, w), jnp.uint32),
            pltpu.SemaphoreType.DMA,
        ],
    )
    return pl.pallas_call(
        _dispatch_kernel,
        grid_spec=grid_spec,
        out_shape=jax.ShapeDtypeStruct((cap, 1, w), jnp.uint32),
        compiler_params=pltpu.CompilerParams(
            dimension_semantics=("arbitrary",), vmem_limit_bytes=VMEM_LIMIT),
        name="dispatch",
    )(zpos, dest_tiles, h3p)


def _expert_kernel(blk_e_ref, nused_ref, xs_ref, wg_ref, wu_ref, wd_ref, ys_ref, wg_b, wu_b, wd_b,
                   xbuf, ybuf, zbuf, in_sem, out_sem, zero_sem):
    i = pl.program_id(0)
    nsteps = pl.num_programs(0)
    nused = nused_ref[0]
    bm = xbuf.shape[1]
    slot = i % 2
    prev = blk_e_ref[jnp.maximum(i - 1, 0)]
    fresh = (i == 0) | (blk_e_ref[i] != prev)
    half = wd_b.shape[1] // 2

    def rows(ref, step):
        return ref.at[pl.ds(pl.multiple_of(step * bm, bm), bm), 0]

    def in_copy(step, sl):
        return pltpu.make_async_copy(rows(xs_ref, step), xbuf.at[sl], in_sem.at[sl])

    def out_copy(step, sl):
        return pltpu.make_async_copy(ybuf.at[sl], rows(ys_ref, step), out_sem.at[sl])

    def zero_copy(step):
        return pltpu.make_async_copy(zbuf, rows(ys_ref, step), zero_sem)

    @pl.when(i == 0)
    def _():
        zbuf[...] = jnp.zeros_like(zbuf)
        in_copy(0, 0).start()

    @pl.when(i + 1 < nused)
    def _():
        in_copy(i + 1, 1 - slot).start()

    @pl.when(fresh)
    def _():
        wg_b[...] = wg_ref[0].astype(BF16)
        wu_b[...] = wu_ref[0].astype(BF16)
        wd_b[...] = wd_ref[0].astype(BF16)

    @pl.when(i < nused)
    def _():
        in_copy(i, slot).wait()
        pl.when(i >= 2)(lambda: out_copy(i - 2, slot).wait())
        lo, hi = _unpack_rows(xbuf[slot])
        xb = jnp.concatenate([lo.astype(BF16), hi.astype(BF16)], axis=1)
        hid = (_silu(_dot(xb, wg_b[...])) * _dot(xb, wu_b[...])).astype(BF16)
        y = _dot(hid, wd_b[...])
        ybuf[slot] = _pack_rows(y[:, :half], y[:, half:])
        out_copy(i, slot).start()

    pl.when(i >= nused)(lambda: zero_copy(i).start())

    @pl.when(i == nsteps - 1)
    def _():
        pl.when(nused >= 2)(lambda: out_copy(nused - 2, nused % 2).wait())
        pl.when(nused >= 1)(lambda: out_copy(nused - 1, (nused - 1) % 2).wait())
        lax.fori_loop(nused, nsteps, lambda b, c: (zero_copy(b).wait(), c)[1], 0)


def _experts(xs, blk_e, nused, w_gate, w_up, w_down):
    cap, _, w = xs.shape
    _, d, de = w_gate.shape
    bm = MOE_ROWS
    wspec = lambda blk: pl.BlockSpec(blk, lambda i, be, nu: (be[i], 0, 0))
    grid_spec = pltpu.PrefetchScalarGridSpec(
        num_scalar_prefetch=2,
        grid=(cap // bm,),
        in_specs=[pl.BlockSpec(memory_space=pl.ANY), wspec((1, d, de)), wspec((1, d, de)), wspec((1, de, d))],
        out_specs=pl.BlockSpec(memory_space=pl.ANY),
        scratch_shapes=[
            pltpu.VMEM((d, de), BF16),
            pltpu.VMEM((d, de), BF16),
            pltpu.VMEM((de, d), BF16),
            pltpu.VMEM((2, bm, w), jnp.uint32),
            pltpu.VMEM((2, bm, w), jnp.uint32),
            pltpu.VMEM((bm, w), jnp.uint32),
            pltpu.SemaphoreType.DMA((2,)),
            pltpu.SemaphoreType.DMA((2,)),
            pltpu.SemaphoreType.DMA,
        ],
    )
    return pl.pallas_call(
        _expert_kernel,
        grid_spec=grid_spec,
        out_shape=jax.ShapeDtypeStruct((cap, 1, w), jnp.uint32),
        compiler_params=pltpu.CompilerParams(
            dimension_semantics=("arbitrary",), vmem_limit_bytes=VMEM_LIMIT),
        name="experts",
    )(blk_e, nused, xs, w_gate, w_up, w_down)


def _combine_kernel(dest_ref, ys_ref, x2_ref, rw_ref, eye_ref, nw_ref, o_ref, idx_ref, idx_sem, ybuf, ysem):
    i = pl.program_id(0)
    nsteps = pl.num_programs(0)
    tc, d = x2_ref.shape
    half = d // 2
    n_idx = 2 * tc

    def idx_copy(step):
        sl = step % 3
        off = pl.multiple_of(sl * n_idx, n_idx)
        return pltpu.make_async_copy(dest_ref.at[step], idx_ref.at[pl.ds(off, n_idx)], idx_sem.at[sl])

    def gather(step):
        base = (step % 3) * n_idx
        buf = ybuf.at[step % 2]
        sem = ysem.at[step % 2]

        for t in range(n_idx):
            pltpu.make_async_copy(ys_ref.at[idx_ref[base + t]], buf.at[pl.ds(t, 1)], sem).start(priority=t % 2)

    @pl.when(i == 0)
    def _():
        idx_copy(0).start()
        idx_copy(0).wait()
        gather(0)

        @pl.when(nsteps > 1)
        def _():
            idx_copy(1).start()

    @pl.when(i + 1 < nsteps)
    def _():
        idx_copy(i + 1).wait()

        @pl.when(i + 2 < nsteps)
        def _():
            idx_copy(i + 2).start()

        gather(i + 1)

    slot = i % 2
    pltpu.make_async_copy(ybuf.at[slot], ybuf.at[slot], ysem.at[slot]).wait()
    rw = rw_ref[...]
    rw_a = rw.astype(BF16)
    rw_b = (rw - rw_a.astype(F32)).astype(BF16)
    rw_c = (rw - rw_a.astype(F32) - rw_b.astype(F32)).astype(BF16)
    eye = eye_ref[...]
    wcol = _dot_nt(eye, rw_a) + (_dot_nt(eye, rw_b) + _dot_nt(eye, rw_c))
    lo1, hi1 = _unpack_rows(ybuf[slot, 0:tc])
    lo2, hi2 = _unpack_rows(ybuf[slot, tc:n_idx])
    w1 = wcol[:, 0:1]
    w2 = wcol[:, 1:2]
    z_lo = x2_ref[:, :half] + (lo1 * w1 + lo2 * w2)
    z_hi = x2_ref[:, half:] + (hi1 * w1 + hi2 * w2)
    ms = (jnp.sum(z_lo * z_lo, axis=-1, keepdims=True) + jnp.sum(z_hi * z_hi, axis=-1, keepdims=True)) / d
    scale = lax.rsqrt(ms + EPS)
    o_ref[:, :half] = z_lo * scale * nw_ref[:, :half]
    o_ref[:, half:] = z_hi * scale * nw_ref[:, half:]


def _combine(x2, ys, dest_tiles, rw, eye, norm_w):
    n, d = x2.shape
    nt, n_idx = dest_tiles.shape
    tc = n_idx // 2
    w = ys.shape[-1]
    return pl.pallas_call(
        _combine_kernel,
        grid=(nt,),
        in_specs=[pl.BlockSpec(memory_space=pl.ANY)] * 2 + [
            pl.BlockSpec((tc, d), lambda i: (i, 0)),
            pl.BlockSpec((8, tc), lambda i: (0, i)),
            pl.BlockSpec((tc, tc), lambda i: (0, 0)),
            pl.BlockSpec((1, d), lambda i: (0, 0)),
        ],
        out_specs=pl.BlockSpec((tc, d), lambda i: (i, 0)),
        out_shape=jax.ShapeDtypeStruct((n, d), F32),
        scratch_shapes=[
            pltpu.SMEM((3 * n_idx,), jnp.int32),
            pltpu.SemaphoreType.DMA((3,)),
            pltpu.VMEM((2, n_idx, w), jnp.uint32),
            pltpu.SemaphoreType.DMA((2,)),
        ],
        compiler_params=pltpu.CompilerParams(
            dimension_semantics=("arbitrary",), vmem_limit_bytes=VMEM_LIMIT),
        name="combine",
    )(dest_tiles, ys, x2, rw, eye, norm_w)


def _layer(xf, mem, batch, seq, norm_mix_w, w_in, ret_norm_w, ml_conv_w, ml_conv_b, ml_gate_b, ml_norm_w,
           w_out, norm_xa_w, norm_mem_w, xa_wq, xa_wkv, xa_wo, norm_moe_w, moe_w_group, moe_b_group,
           moe_w_router, moe_b_router, moe_w_gate, moe_w_up, moe_w_down, final_norm_w):
    n, d = xf.shape
    def halves_first(w):
        return w.reshape(d, RET_HEADS, 2, RET_DK // 2).transpose(0, 2, 1, 3).reshape(d, RET_QK)

    w_main = jnp.concatenate([halves_first(w_in[:, OFF_RQ:OFF_RQ + RET_QK]),
                              halves_first(w_in[:, OFF_RK:OFF_RK + RET_QK]),
                              w_in[:, OFF_RV:MAIN_WIDTH]], axis=1).astype(BF16)
    w_if = w_in[:, MAIN_WIDTH:].astype(BF16)
    tabs = {k_: jnp.asarray(v_) for k_, v_ in _mixer_tables(seq).items()}
    proj, g, gt = _inproj(xf, norm_mix_w.reshape(1, d), w_main, w_if, w_if.T, tabs["cos"], tabs["sin"], ml_conv_w,
                          ml_conv_b.reshape(1, 2 * ML_QK), seq)
    mixed = _mixer(proj, g, gt, tabs, ret_norm_w.reshape(1, RET_V), ml_norm_w.reshape(1, ML_V), ml_gate_b,
                   batch, seq)

    kmem, vmem = _memkv(mem, norm_mem_w.reshape(1, d), xa_wkv.astype(BF16))

    w_route_t = jnp.concatenate(
        [moe_w_router.T, moe_w_group.T, jnp.zeros((ROUTE_ROWS - N_EXPERTS - N_GROUPS, d), F32)], axis=0)
    b_route = jnp.concatenate(
        [moe_b_router, moe_b_group, jnp.zeros((ROUTE_ROWS - N_EXPERTS - N_GROUPS,), F32)]).reshape(ROUTE_ROWS, 1)
    tm = TOKEN_TILE
    sut = jnp.asarray(np.triu(np.ones((tm, tm), np.float32), 1), dtype=BF16)
    x2, h3, ri, rw, cnt = _attn_route(xf, mixed, kmem, vmem, w_out.astype(BF16), norm_xa_w.reshape(1, d),
                                      xa_wq.astype(BF16), xa_wo.astype(BF16), norm_moe_w.reshape(1, d),
                                      w_route_t, b_route, sut, batch, seq)

    bm = MOE_ROWS
    counts = cnt[:, 0].astype(jnp.int32)
    padded = (counts + bm - 1) // bm * bm
    pends = jnp.cumsum(padded)
    pstarts = pends - padded
    expert = ri[0:TOP_K]
    onehot = expert[None] == jnp.arange(N_EXPERTS, dtype=jnp.int32)[:, None, None]
    dest = jnp.sum(jnp.where(onehot, pstarts[:, None, None], 0), axis=0) + ri[TOP_K:2 * TOP_K]
    cap = n * TOP_K + N_EXPERTS * bm
    nblk = cap // bm
    blk_start = jnp.arange(nblk, dtype=jnp.int32) * bm
    blk_e = jnp.minimum(jnp.sum(blk_start[:, None] >= pends[None, :], axis=1), N_EXPERTS - 1).astype(jnp.int32)
    nused = (pends[-1] // bm).astype(jnp.int32).reshape(1)
    zpos = jnp.where(padded > counts, pends - bm, -1).astype(jnp.int32)
    zpos = jnp.concatenate([zpos, nused])

    def tiles(rows):
        return dest.reshape(TOP_K, n // rows, rows).transpose(1, 0, 2).reshape(n // rows, TOP_K * rows)

    xs = _dispatch(h3, tiles(DISPATCH_TILE), zpos, cap)
    ys = _experts(xs, blk_e, nused, moe_w_gate, moe_w_up, moe_w_down)
    eye = jnp.asarray(np.eye(COMBINE_TILE, dtype=np.float32), dtype=BF16)
    return _combine(x2, ys, tiles(COMBINE_TILE), rw, eye, final_norm_w.reshape(1, d))


def kernel(x, mem, norm_mix_w, w_in, ret_norm_w, ml_conv_w, ml_conv_b, ml_gate_b, ml_norm_w, w_out, norm_xa_w, norm_mem_w, xa_wq, xa_wkv, xa_wo, norm_moe_w, moe_w_group, moe_b_group, moe_w_router, moe_b_router, moe_w_gate, moe_w_up, moe_w_down, norm_final_w):
    batch, seq, d = x.shape
    depth = w_in.shape[0]
    assert depth == 1, "the final norm is fused into the last layer's combine kernel"
    l = 0
    out = _layer(x.reshape(batch * seq, d), mem, batch, seq, norm_mix_w[l], w_in[l], ret_norm_w[l], ml_conv_w[l],
                 ml_conv_b[l], ml_gate_b[l], ml_norm_w[l], w_out[l], norm_xa_w[l], norm_mem_w[l], xa_wq[l],
                 xa_wkv[l], xa_wo[l], norm_moe_w[l], moe_w_group[l], moe_b_group[l], moe_w_router[l],
                 moe_b_router[l], moe_w_gate[l], moe_w_up[l], moe_w_down[l], norm_final_w)
    return out.reshape(batch, seq, d)
```

```python
import functools

import numpy as np
import jax
import jax.numpy as jnp
from jax import lax
from jax.experimental import pallas as pl
from jax.experimental.pallas import tpu as pltpu

F32 = jnp.float32
BF16 = jnp.bfloat16

CHUNK = 128
RET_HEADS = 4
RET_DK = 64
RET_DV = 128
ML_HEADS = 4
ML_DK = 128
ML_DV = 128
CONV_W = 4
XA_HEADS = 4
N_GROUPS = 4
EXP_PER_GROUP = 8
N_EXPERTS = N_GROUPS * EXP_PER_GROUP
TOP_K = 2
ROPE_BASE = 10000.0
EPS = 1e-6

RET_QK = RET_HEADS * RET_DK
RET_V = RET_HEADS * RET_DV
ML_QK = ML_HEADS * ML_DK
ML_V = ML_HEADS * ML_DV
OFF_RQ = 0
OFF_RK = OFF_RQ + RET_QK
OFF_RV = OFF_RK + RET_QK
OFF_RG = OFF_RV + RET_V
OFF_MQK = OFF_RG + RET_V
OFF_MV = OFF_MQK + 2 * ML_QK
OFF_MO = OFF_MV + ML_V
MAIN_WIDTH = OFF_MO + ML_V
N_GATES = 2 * ML_HEADS

ROUTE_ROWS = 40
TOKEN_TILE = 1024
MOE_ROWS = 512
DISPATCH_TILE = 1024
COMBINE_TILE = 1024
COMBINE_EYE = 512
MIXER_BATCHES = 4
ATTN_GROUPS = 2
VMEM_LIMIT = 56 * 1024 * 1024


def _dot(a, b):
    return jnp.dot(a, b, preferred_element_type=F32)


def _dot_nt(a, b):
    return lax.dot_general(a, b, (((1,), (1,)), ((), ())), preferred_element_type=F32)


def _dot_tn(a, b):
    return lax.dot_general(a, b, (((0,), (0,)), ((), ())), preferred_element_type=F32)


def _rms(x, w):
    return x * lax.rsqrt(jnp.mean(x * x, axis=-1, keepdims=True) + EPS) * w


def _sigmoid(x):
    return 1.0 / (1.0 + jnp.exp(-x))


def _silu(x):
    return x * _sigmoid(x)


def _log_sigmoid(x):
    return jnp.minimum(x, 0.0) - jnp.log1p(jnp.exp(-jnp.abs(x)))


def _head_norms(ts, mean_w):
    mu = [_dot(t.astype(BF16), mean_w) for t in ts]
    dl = [t - m for t, m in zip(ts, mu)]
    var = [_dot((d * d).astype(BF16), mean_w) for d in dl]
    return [d * lax.rsqrt(v + EPS) for d, v in zip(dl, var)]


def _pack_rows(lo, hi):
    def bits(t):
        return lax.bitcast_convert_type(t.astype(BF16), jnp.uint16).astype(jnp.uint32)
    return bits(lo) | (bits(hi) << 16)


def _unpack_rows(u):
    lo = lax.bitcast_convert_type(u << 16, F32)
    hi = lax.bitcast_convert_type(u & jnp.uint32(0xFFFF0000), F32)
    return lo, hi


def _inproj_kernel(tiles_per_seq, x_ref, nw_ref, w_ref, wif_ref, wift_ref, cos_ref, sin_ref, convw_ref, convb_ref,
                   proj_ref, g_ref, gt_ref, carry_ref):
    tm = x_ref.shape[0]

    @pl.when(pl.program_id(0) == 0)
    def _():
        carry_ref[...] = jnp.zeros_like(carry_ref)

    h = _rms(x_ref[...], nw_ref[...]).astype(BF16)

    def mm(off, width):
        return _dot(h, w_ref[:, off:off + width])

    def rotary(qk):
        cos = cos_ref[...]
        sin = sin_ref[...]
        half = RET_QK // 2
        for off, scale in ((OFF_RQ, None), (OFF_RK, RET_DK ** -0.5)):
            t1 = qk[:, off - OFF_RQ:off - OFF_RQ + half]
            t2 = qk[:, off - OFF_RQ + half:off - OFF_RQ + 2 * half]
            r1 = t1 * cos - t2 * sin
            r2 = t1 * sin + t2 * cos
            if scale is not None:
                r1, r2 = r1 * scale, r2 * scale
            proj_ref[:, off:off + half] = r1.astype(BF16)
            proj_ref[:, off + half:off + 2 * half] = r2.astype(BF16)

    def conv_silu(part, scale, cur):
        c0 = part * ML_QK
        first = (pl.program_id(0) % tiles_per_seq) == 0
        row8 = lax.broadcasted_iota(jnp.int32, (8, ML_QK), 0)
        prev = jnp.where(first, 0.0, carry_ref[:, c0:c0 + ML_QK])
        acc = cur * convw_ref[CONV_W - 1:CONV_W, c0:c0 + ML_QK] + convb_ref[:, c0:c0 + ML_QK]
        for s in range(1, CONV_W):
            rolled = pltpu.roll(cur, s, 0)
            head8 = jnp.where(row8 < s, pltpu.roll(prev, s, 0), rolled[0:8])
            shifted = jnp.concatenate([head8, rolled[8:]], axis=0)
            acc = acc + shifted * convw_ref[CONV_W - 1 - s:CONV_W - s, c0:c0 + ML_QK]
        carry_ref[:, c0:c0 + ML_QK] = cur[tm - 8:tm]
        act = _silu(acc) if scale is None else _silu(acc) * scale
        proj_ref[:, OFF_MQK + c0:OFF_MQK + c0 + ML_QK] = act.astype(BF16)

    def store(off, width, fn=None):
        def ep(t):
            proj_ref[:, off:off + width] = (t if fn is None else fn(t)).astype(BF16)
        return ep

    rotary(mm(OFF_RQ, 2 * RET_QK))
    store(OFF_RV, RET_V)(mm(OFF_RV, RET_V))
    store(OFF_RG, RET_V, _silu)(mm(OFF_RG, RET_V))
    conv_silu(0, None, mm(OFF_MQK, ML_QK))
    conv_silu(1, ML_DK ** -0.5, mm(OFF_MQK + ML_QK, ML_QK))
    store(OFF_MV, ML_V)(mm(OFF_MV, ML_V))
    store(OFF_MO, ML_V, _sigmoid)(mm(OFF_MO, ML_V))
    g_ref[...] = _dot(h, wif_ref[...])
    gt_ref[...] = _dot_nt(wift_ref[...], h)


def _inproj(xf, norm_w, w_main, w_if, w_ift, cos, sin, conv_w, conv_b, seq):
    n, d = xf.shape
    tm = TOKEN_TILE
    tiles_per_seq = seq // tm
    return pl.pallas_call(
        functools.partial(_inproj_kernel, tiles_per_seq),
        grid=(n // tm,),
        in_specs=[
            pl.BlockSpec((tm, d), lambda i: (i, 0)),
            pl.BlockSpec((1, d), lambda i: (0, 0)),
            pl.BlockSpec((d, MAIN_WIDTH), lambda i: (0, 0)),
            pl.BlockSpec((d, N_GATES), lambda i: (0, 0)),
            pl.BlockSpec((N_GATES, d), lambda i: (0, 0)),
            pl.BlockSpec((tm, RET_QK // 2), lambda i: (i % tiles_per_seq, 0)),
            pl.BlockSpec((tm, RET_QK // 2), lambda i: (i % tiles_per_seq, 0)),
            pl.BlockSpec((CONV_W, 2 * ML_QK), lambda i: (0, 0)),
            pl.BlockSpec((1, 2 * ML_QK), lambda i: (0, 0)),
        ],
        out_specs=[
            pl.BlockSpec((tm, MAIN_WIDTH), lambda i: (i, 0)),
            pl.BlockSpec((tm, N_GATES), lambda i: (i, 0)),
            pl.BlockSpec((N_GATES, tm), lambda i: (0, i)),
        ],
        out_shape=[
            jax.ShapeDtypeStruct((n, MAIN_WIDTH), BF16),
            jax.ShapeDtypeStruct((n, N_GATES), F32),
            jax.ShapeDtypeStruct((N_GATES, n), F32),
        ],
        scratch_shapes=[pltpu.VMEM((8, 2 * ML_QK), F32)],
        compiler_params=pltpu.CompilerParams(
            dimension_semantics=("arbitrary",), vmem_limit_bytes=VMEM_LIMIT),
        name="inproj",
    )(xf, norm_w, w_main, w_if, w_ift, cos, sin, conv_w, conv_b)


def _mixer_kernel(*refs):
    @pl.when(pl.program_id(1) == 0)
    def _():
        for state_ref in refs[-4:]:
            state_ref[...] = jnp.zeros_like(state_ref)

    for bi in range(refs[0].shape[0]):
        _mixer_one(bi, *refs)


def _mixer_one(bi, proj_ref, g_ref, gt_ref, qdec_ref, kdec_ref, dmat_ref,
               bmask_ref, cdec_ref, hmask_ref, tril_ref, triu_ref, ones_ref, retw_ref, mlw_ref, gbc_ref, gbr_ref,
               out_ref, r_ref, c_ref, n_ref, m_ref):
    L = CHUNK
    proj_ref, g_ref, gt_ref, out_ref = proj_ref.at[bi], g_ref.at[bi], gt_ref.at[bi], out_ref.at[bi]
    r_ref, c_ref, n_ref, m_ref = r_ref.at[bi], c_ref.at[bi], n_ref.at[bi], m_ref.at[bi]
    mean_w = ones_ref[...]

    q = proj_ref[:, OFF_RQ:OFF_RQ + RET_QK].astype(F32)
    k_b = proj_ref[:, OFF_RK:OFF_RK + RET_QK]
    k = k_b.astype(F32)
    v = proj_ref[:, OFF_RV:OFF_RV + RET_V]
    r_prev = r_ref[...]
    cross = _dot((q * qdec_ref[...]).astype(BF16), r_prev.astype(BF16))
    kv = _dot_tn((k * kdec_ref[...]).astype(BF16), v) * bmask_ref[...]
    r_ref[...] = cdec_ref[...] * r_prev + kv
    RH = range(RET_HEADS)
    sc = [_dot_nt((q * hmask_ref[h:h + 1, :]).astype(BF16), k_b) for h in RH]
    sc = [(sc[h] * dmat_ref[h]).astype(BF16) for h in RH]
    tot = [_dot(sc[h], v[:, h * RET_DV:(h + 1) * RET_DV]) + cross[:, h * RET_DV:(h + 1) * RET_DV] for h in RH]
    ret = jnp.concatenate(_head_norms(tot, mean_w), axis=1) * retw_ref[...]
    ret = ret * proj_ref[:, OFF_RG:OFF_RG + RET_V].astype(F32)
    out_ref[:, 0:RET_V] = ret.astype(BF16)

    mq = proj_ref[:, OFF_MQK:OFF_MQK + ML_QK]
    mk = proj_ref[:, OFF_MQK + ML_QK:OFF_MQK + 2 * ML_QK]
    mv = proj_ref[:, OFF_MV:OFF_MV + ML_V]

    gc = g_ref[...] + gbc_ref[...]
    gr = gt_ref[...] + gbr_ref[...]
    lf_c = _log_sigmoid(gc)
    lf_r = _log_sigmoid(gr)
    lf_c_hi = lf_c.astype(BF16)
    lf_r_hi = lf_r.astype(BF16)
    b_c = (_dot(tril_ref[...], lf_c_hi)
           + _dot(tril_ref[...], (lf_c - lf_c_hi.astype(F32)).astype(BF16)))
    b_r = (_dot(lf_r_hi, triu_ref[...])
           + _dot((lf_r - lf_r_hi.astype(F32)).astype(BF16), triu_ref[...]))
    causal = (lax.broadcasted_iota(jnp.int32, (L, L), 0) >= lax.broadcasted_iota(jnp.int32, (L, L), 1))
    MH = range(ML_HEADS)
    bc = [b_c[:, ML_HEADS + h:ML_HEADS + h + 1] for h in MH]
    br = [b_r[ML_HEADS + h:ML_HEADS + h + 1, :] for h in MH]
    igc = [gc[:, h:h + 1] for h in MH]
    igr = [gr[h:h + 1, :] for h in MH]
    btot = [br[h][:, L - 1:L] for h in MH]
    qh_b = [mq[:, h * ML_DK:(h + 1) * ML_DK] for h in MH]
    kh_b = [mk[:, h * ML_DK:(h + 1) * ML_DK] for h in MH]
    vh = [mv[:, h * ML_DV:(h + 1) * ML_DV] for h in MH]
    c_prev = [c_ref[h] for h in MH]
    n_prev = [n_ref[h][0:1, :] for h in MH]
    m_prev = [m_ref[h][0:1, 0:1] for h in MH]
    s_raw = [_dot_nt(qh_b[h], kh_b[h]) for h in MH]
    qc = [_dot(qh_b[h], c_prev[h].astype(BF16)) for h in MH]
    log_d = [jnp.where(causal, bc[h] - br[h] + igr[h], -jnp.inf) for h in MH]
    m_intra = [jnp.max(log_d[h], axis=1, keepdims=True) for h in MH]
    m_loc = [jnp.max(btot[h] - br[h] + igr[h], axis=1, keepdims=True) for h in MH]
    kw = [kh_b[h].astype(F32) * jnp.exp(btot[h] - bc[h] + igc[h] - m_loc[h]) for h in MH]
    kv_loc = [_dot_tn(kw[h].astype(BF16), vh[h]) for h in MH]
    n_loc = [jnp.sum(kw[h], axis=0, keepdims=True) for h in MH]
    m_inter = [bc[h] + m_prev[h] for h in MH]
    m_t = [jnp.maximum(m_intra[h], m_inter[h]) for h in MH]
    s_mat = [s_raw[h] * jnp.exp(log_d[h] - m_t[h]) for h in MH]
    inter = [jnp.exp(m_inter[h] - m_t[h]) for h in MH]
    num = [_dot(s_mat[h].astype(BF16), vh[h]) + inter[h] * qc[h] for h in MH]
    den = [jnp.sum(s_mat[h], axis=1, keepdims=True)
           + inter[h] * jnp.sum(qh_b[h].astype(F32) * n_prev[h], axis=1, keepdims=True) for h in MH]
    hh = [num[h] / jnp.maximum(jnp.abs(den[h]), jnp.exp(-m_t[h])) for h in MH]
    for h in MH:
        m_new = jnp.maximum(btot[h] + m_prev[h], m_loc[h])
        s_old = jnp.exp(btot[h] + m_prev[h] - m_new)
        s_loc = jnp.exp(m_loc[h] - m_new)
        c_ref[h] = s_old * c_prev[h] + s_loc * kv_loc[h]
        n_ref[h] = jnp.broadcast_to(s_old * n_prev[h] + s_loc * n_loc[h], (8, ML_DK))
        m_ref[h] = jnp.broadcast_to(m_new, (8, 128))
    ml = jnp.concatenate(_head_norms(hh, mean_w), axis=1) * mlw_ref[...]
    ml = ml * proj_ref[:, OFF_MO:OFF_MO + ML_V].astype(F32)
    out_ref[:, RET_V:RET_V + ML_V] = ml.astype(BF16)


def _mixer_tables(seq):
    L = CHUNK
    half = RET_DK // 2
    inv = ROPE_BASE ** (-np.arange(half, dtype=np.float64) / half)
    ang = np.arange(seq, dtype=np.float64)[:, None] * inv[None, :].astype(np.float32).astype(np.float64)
    cos = np.tile(np.cos(ang), (1, RET_HEADS)).astype(np.float32)
    sin = np.tile(np.sin(ang), (1, RET_HEADS)).astype(np.float32)
    log_g = np.log1p(-np.exp2(-5.0 - np.arange(RET_HEADS, dtype=np.float64)))
    n = np.arange(L, dtype=np.float64)
    lane_head = (np.arange(RET_QK) % (RET_QK // 2)) // half
    qdec = np.exp((n + 1)[:, None] * log_g[lane_head][None, :]).astype(np.float32)
    kdec = np.exp((L - 1 - n)[:, None] * log_g[lane_head][None, :]).astype(np.float32)
    diff = n[:, None] - n[None, :]
    dmat = np.where(diff >= 0, np.exp(log_g[:, None, None] * np.maximum(diff, 0.0)[None]), 0.0).astype(np.float32)
    col_head = np.arange(RET_V) // RET_DV
    bmask = (lane_head[:, None] == col_head[None, :]).astype(np.float32)
    cdec = np.exp(L * log_g[col_head])[None, :].astype(np.float32)
    hmask = (lane_head[None, :] == np.arange(RET_HEADS)[:, None]).astype(np.float32)
    hmask = np.concatenate([hmask, np.zeros((8 - RET_HEADS, RET_QK), np.float32)], axis=0)
    tril = np.tril(np.ones((L, L), np.float32))
    ones = np.full((RET_DV, RET_DV), 1.0 / RET_DV, np.float32)
    return dict(cos=cos, sin=sin, qdec=qdec, kdec=kdec, dmat=dmat, bmask=bmask, cdec=cdec, hmask=hmask,
                tril=tril, triu=np.ascontiguousarray(tril.T), ones=ones)


def _mixer(proj, g, gt, tabs, ret_norm_w, ml_norm_w, gate_b, batch, seq):
    L = CHUNK
    nc = seq // L
    n = batch * seq
    nb = MIXER_BATCHES if batch % MIXER_BATCHES == 0 else 1
    proj = proj.reshape(batch, seq, MAIN_WIDTH)
    g = g.reshape(batch, seq, N_GATES)
    gt = gt.reshape(N_GATES, batch, seq).transpose(1, 0, 2)
    const2 = lambda b, c: (0, 0)
    const3 = lambda b, c: (0, 0, 0)
    tok = lambda b, c: (b, c, 0)
    in_specs = [
        pl.BlockSpec((nb, L, MAIN_WIDTH), tok),
        pl.BlockSpec((nb, L, N_GATES), tok),
        pl.BlockSpec((nb, N_GATES, L), lambda b, c: (b, 0, c)),
        pl.BlockSpec((L, RET_QK), const2),
        pl.BlockSpec((L, RET_QK), const2),
        pl.BlockSpec((RET_HEADS, L, L), const3),
        pl.BlockSpec((RET_QK, RET_V), const2),
        pl.BlockSpec((1, RET_V), const2),
        pl.BlockSpec((8, RET_QK), const2),
        pl.BlockSpec((L, L), const2),
        pl.BlockSpec((L, L), const2),
        pl.BlockSpec((RET_DV, RET_DV), const2),
        pl.BlockSpec((1, RET_V), const2),
        pl.BlockSpec((1, ML_V), const2),
        pl.BlockSpec((1, N_GATES), const2),
        pl.BlockSpec((N_GATES, 1), const2),
    ]
    return pl.pallas_call(
        _mixer_kernel,
        grid=(batch // nb, nc),
        in_specs=in_specs,
        out_specs=pl.BlockSpec((nb, L, RET_V + ML_V), tok),
        out_shape=jax.ShapeDtypeStruct((batch, seq, RET_V + ML_V), BF16),
        scratch_shapes=[
            pltpu.VMEM((nb, RET_QK, RET_V), F32),
            pltpu.VMEM((nb, ML_HEADS, ML_DK, ML_DV), F32),
            pltpu.VMEM((nb, ML_HEADS, 8, ML_DK), F32),
            pltpu.VMEM((nb, ML_HEADS, 8, 128), F32),
        ],
        compiler_params=pltpu.CompilerParams(
            dimension_semantics=("arbitrary", "arbitrary"), vmem_limit_bytes=VMEM_LIMIT),
        name="mixer",
    )(proj, g, gt, tabs["qdec"], tabs["kdec"], tabs["dmat"], tabs["bmask"],
      tabs["cdec"], tabs["hmask"], tabs["tril"].astype(BF16), tabs["triu"].astype(BF16), tabs["ones"].astype(BF16),
      ret_norm_w, ml_norm_w,
      gate_b.reshape(1, N_GATES), gate_b.reshape(N_GATES, 1)).reshape(n, RET_V + ML_V)


def _memkv_kernel(mem_ref, nw_ref, wkv_ref, k_ref, v_ref):
    d = mem_ref.shape[-1]
    mn = _rms(mem_ref[0], nw_ref[...]).astype(BF16)
    k_ref[0] = _dot(mn, wkv_ref[:, :d]).astype(BF16)
    v_ref[0] = _dot(mn, wkv_ref[:, d:]).astype(BF16)


def _memkv(mem, norm_w, wkv):
    b, m, d = mem.shape
    return pl.pallas_call(
        _memkv_kernel,
        grid=(b,),
        in_specs=[
            pl.BlockSpec((1, m, d), lambda i: (i, 0, 0)),
            pl.BlockSpec((1, d), lambda i: (0, 0)),
            pl.BlockSpec((d, 2 * d), lambda i: (0, 0)),
        ],
        out_specs=[pl.BlockSpec((1, m, d), lambda i: (i, 0, 0))] * 2,
        out_shape=[jax.ShapeDtypeStruct((b, m, d), BF16)] * 2,
        compiler_params=pltpu.CompilerParams(
            dimension_semantics=("arbitrary",), vmem_limit_bytes=VMEM_LIMIT),
        name="memkv",
    )(mem, norm_w, wkv)


def _attn_route_kernel(x_ref, mix_ref, k_ref, v_ref, wout_ref, nxa_ref, wq_ref, wo_ref, nmoe_ref,
                       wr_ref, wrlo_ref, br_ref, sut_ref,
                       x2_ref, h3_ref, ri_ref, rw_ref, cnt_ref, carry_ref):
    tm, d = x_ref.shape
    dh = d // XA_HEADS

    @pl.when((pl.program_id(0) == 0) & (pl.program_id(1) == 0))
    def _():
        carry_ref[...] = jnp.zeros_like(carry_ref)

    groups = [slice(g * (tm // ATTN_GROUPS), (g + 1) * (tm // ATTN_GROUPS)) for g in range(ATTN_GROUPS)]
    x1 = [x_ref[s, :] + _dot(mix_ref[s, :], wout_ref[...]) for s in groups]
    h2 = [_rms(t, nxa_ref[...]).astype(BF16) for t in x1]
    q = [_dot(t, wq_ref[...]).astype(BF16) for t in h2]
    o = []
    for qg in q:
        heads = []
        for h in range(XA_HEADS):
            logits = _dot_nt(qg[:, h * dh:(h + 1) * dh], k_ref[0, :, h * dh:(h + 1) * dh]) * (dh ** -0.5)
            mx = jnp.max(logits, axis=-1, keepdims=True)
            e = jnp.exp(logits - mx)
            p = (e / jnp.sum(e, axis=-1, keepdims=True)).astype(BF16)
            heads.append(_dot(p, v_ref[0, :, h * dh:(h + 1) * dh]).astype(BF16))
        o.append(jnp.concatenate(heads, axis=1))
    x2 = [a + _dot(b, wo_ref[...]) for a, b in zip(x1, o)]
    for s, t in zip(groups, x2):
        x2_ref[s, :] = t
    h3 = [_rms(t, nmoe_ref[...]) for t in x2]
    for s, t in zip(groups, h3):
        h3_ref[s, 0, :] = _pack_rows(t[:, :d // 2], t[:, d // 2:])

    lts = []
    for t in h3:
        t_hi = t.astype(BF16)
        t_lo = (t - t_hi.astype(F32)).astype(BF16)
        lts.append(_dot_nt(wr_ref[...], t_hi) + (_dot_nt(wr_ref[...], t_lo) + _dot_nt(wrlo_ref[...], t_hi)))
    lt = jnp.concatenate(lts, axis=1) + br_ref[...]
    gl = lt[N_EXPERTS:N_EXPERTS + N_GROUPS]
    gmax = jnp.max(gl, axis=0, keepdims=True)
    g_w = 1.0 / jnp.sum(jnp.exp(gl - gmax), axis=0, keepdims=True)
    giota = lax.broadcasted_iota(jnp.int32, gl.shape, 0)
    g_sel = jnp.min(jnp.where(gl == gmax, giota, N_GROUPS), axis=0, keepdims=True)
    el = lt[0:N_EXPERTS]
    eiota = lax.broadcasted_iota(jnp.int32, el.shape, 0)
    in_grp = (eiota // EXP_PER_GROUP) == g_sel
    elm = jnp.where(in_grp, el, -jnp.inf)
    m1 = jnp.max(elm, axis=0, keepdims=True)
    esum = jnp.sum(jnp.where(in_grp, jnp.exp(el - m1), 0.0), axis=0, keepdims=True)
    i1 = jnp.min(jnp.where(elm == m1, eiota, N_EXPERTS), axis=0, keepdims=True)
    elm2 = jnp.where(eiota == i1, -jnp.inf, elm)
    m2 = jnp.max(elm2, axis=0, keepdims=True)
    i2 = jnp.min(jnp.where(elm2 == m2, eiota, N_EXPERTS), axis=0, keepdims=True)
    p1 = 1.0 / esum
    p2 = jnp.exp(m2 - m1) / esum
    psum = p1 + p2
    w1 = g_w * (p1 / psum)
    w2 = g_w * (p2 / psum)

    oh1 = (eiota == i1).astype(F32)
    oh2 = (eiota == i2).astype(F32)
    cnt = oh1 + oh2
    base = carry_ref[:, 0:1] + _dot(cnt.astype(BF16), sut_ref[...])
    r1 = jnp.sum(oh1 * base, axis=0, keepdims=True)
    r2 = jnp.sum(oh2 * base, axis=0, keepdims=True)
    new_carry = carry_ref[...] + jnp.sum(cnt, axis=1, keepdims=True)
    carry_ref[...] = new_carry
    cnt_ref[...] = new_carry

    zi = jnp.zeros((4, tm), jnp.int32)
    ri_ref[...] = jnp.concatenate([i1, i2, r1.astype(jnp.int32), r2.astype(jnp.int32), zi], axis=0)
    rw_ref[...] = jnp.concatenate([w1, w2, jnp.zeros((6, tm), F32)], axis=0)


def _attn_route(xf, mixed, kmem, vmem, w_out, norm_xa_w, wq, wo, norm_moe_w, w_route_t, b_route, sut,
                batch, seq):
    n, d = xf.shape
    w_route_hi = w_route_t.astype(BF16)
    w_route_lo = (w_route_t - w_route_hi.astype(F32)).astype(BF16)
    tm = TOKEN_TILE
    nt = seq // tm
    m = kmem.shape[1]
    tok = lambda b, t: (b * nt + t, 0)
    lane_tok = lambda b, t: (0, b * nt + t)
    const2 = lambda b, t: (0, 0)
    return pl.pallas_call(
        _attn_route_kernel,
        grid=(batch, nt),
        in_specs=[
            pl.BlockSpec((tm, d), tok),
            pl.BlockSpec((tm, d), tok),
            pl.BlockSpec((1, m, d), lambda b, t: (b, 0, 0)),
            pl.BlockSpec((1, m, d), lambda b, t: (b, 0, 0)),
            pl.BlockSpec((d, d), const2),
            pl.BlockSpec((1, d), const2),
            pl.BlockSpec((d, d), const2),
            pl.BlockSpec((d, d), const2),
            pl.BlockSpec((1, d), const2),
            pl.BlockSpec((ROUTE_ROWS, d), const2),
            pl.BlockSpec((ROUTE_ROWS, d), const2),
            pl.BlockSpec((ROUTE_ROWS, 1), const2),
            pl.BlockSpec((tm, tm), const2),
        ],
        out_specs=[
            pl.BlockSpec((tm, d), tok),
            pl.BlockSpec((tm, 1, d // 2), lambda b, t: (b * nt + t, 0, 0)),
            pl.BlockSpec((8, tm), lane_tok),
            pl.BlockSpec((8, tm), lane_tok),
            pl.BlockSpec((N_EXPERTS, 128), const2),
        ],
        out_shape=[
            jax.ShapeDtypeStruct((n, d), F32),
            jax.ShapeDtypeStruct((n, 1, d // 2), jnp.uint32),
            jax.ShapeDtypeStruct((8, n), jnp.int32),
            jax.ShapeDtypeStruct((8, n), F32),
            jax.ShapeDtypeStruct((N_EXPERTS, 128), F32),
        ],
        scratch_shapes=[pltpu.VMEM((N_EXPERTS, 128), F32)],
        compiler_params=pltpu.CompilerParams(
            dimension_semantics=("arbitrary", "arbitrary"), vmem_limit_bytes=VMEM_LIMIT),
        name="attn_route",
    )(xf, mixed, kmem, vmem, w_out, norm_xa_w, wq, wo, norm_moe_w, w_route_hi, w_route_lo, b_route, sut)


def _dispatch_kernel(zpos_ref, dest_ref, h_ref, xs_ref, idx_ref, idx_sem, row_sem, zero_ref, zero_sem):
    i = pl.program_id(0)
    nsteps = pl.num_programs(0)
    td = h_ref.shape[0]
    bm = zero_ref.shape[0]
    slot = i % 2

    def idx_copy(step, sl):
        off = pl.multiple_of(sl * (2 * td), 2 * td)
        return pltpu.make_async_copy(dest_ref.at[step], idx_ref.at[pl.ds(off, 2 * td)], idx_sem.at[sl])

    @pl.when(i == 0)
    def _():
        zero_ref[...] = jnp.zeros_like(zero_ref)

        def zero_copy(e):
            return pltpu.make_async_copy(zero_ref, xs_ref.at[pl.ds(pl.multiple_of(zpos_ref[e], bm), bm), 0], zero_sem)

        def tail_copy(b):
            return pltpu.make_async_copy(zero_ref, xs_ref.at[pl.ds(pl.multiple_of(b * bm, bm), bm), 0], zero_sem)

        nused = zpos_ref[N_EXPERTS]
        nblk = xs_ref.shape[0] // bm
        for e in range(N_EXPERTS):
            pl.when(zpos_ref[e] >= 0)(lambda e=e: zero_copy(e).start())
        lax.fori_loop(nused, nblk, lambda b, c: (tail_copy(b).start(), c)[1], 0)
        for e in range(N_EXPERTS):
            pl.when(zpos_ref[e] >= 0)(lambda e=e: zero_copy(e).wait())
        lax.fori_loop(nused, nblk, lambda b, c: (tail_copy(b).wait(), c)[1], 0)
        idx_copy(0, 0).start()

    idx_copy(i, slot).wait()

    @pl.when(i + 1 < nsteps)
    def _():
        idx_copy(i + 1, 1 - slot).start()

    base = slot * (2 * td)

    for t in range(td):
        for k in range(TOP_K):
            pltpu.make_async_copy(h_ref.at[t], xs_ref.at[idx_ref[base + k * td + t]], row_sem).start(priority=t % 2)
    for _ in range(TOP_K):
        pltpu.make_async_copy(xs_ref.at[pl.ds(0, td)], xs_ref.at[pl.ds(0, td)], row_sem).wait()


def _dispatch(h3p, dest_tiles, zpos, cap):
    w = h3p.shape[-1]
    nt, td2 = dest_tiles.shape
    td = td2 // 2
    grid_spec = pltpu.PrefetchScalarGridSpec(
        num_scalar_prefetch=1,
        grid=(nt,),
        in_specs=[
            pl.BlockSpec(memory_space=pl.ANY),
            pl.BlockSpec((td, 1, w), lambda i, zp: (i, 0, 0)),
        ],
        out_specs=pl.BlockSpec(memory_space=pl.ANY),
        scratch_shapes=[
            pltpu.SMEM((2 * td2,), jnp.int32),
            pltpu.SemaphoreType.DMA((2,)),
            pltpu.SemaphoreType.DMA,
            pltpu.VMEM((MOE_ROWS, w), jnp.uint32),
            pltpu.SemaphoreType.DMA,
        ],
    )
    return pl.pallas_call(
        _dispatch_kernel,
        grid_spec=grid_spec,
        out_shape=jax.ShapeDtypeStruct((cap, 1, w), jnp.uint32),
        compiler_params=pltpu.CompilerParams(
            dimension_semantics=("arbitrary",), vmem_limit_bytes=VMEM_LIMIT),
        name="dispatch",
    )(zpos, dest_tiles, h3p)


def _expert_kernel(blk_e_ref, nused_ref, xs_ref, wg_ref, wu_ref, wd_ref, ys_ref, wg_b, wu_b, wd_b,
                   xbuf, ybuf, zbuf, in_sem, out_sem, zero_sem):
    i = pl.program_id(0)
    nsteps = pl.num_programs(0)
    nused = nused_ref[0]
    bm = xbuf.shape[1]
    slot = i % 2
    prev = blk_e_ref[jnp.maximum(i - 1, 0)]
    fresh = (i == 0) | (blk_e_ref[i] != prev)
    half = wd_b.shape[1] // 2

    def rows(ref, step):
        return ref.at[pl.ds(pl.multiple_of(step * bm, bm), bm), 0]

    def in_copy(step, sl):
        return pltpu.make_async_copy(rows(xs_ref, step), xbuf.at[sl], in_sem.at[sl])

    def out_copy(step, sl):
        return pltpu.make_async_copy(ybuf.at[sl], rows(ys_ref, step), out_sem.at[sl])

    def zero_copy(step):
        return pltpu.make_async_copy(zbuf, rows(ys_ref, step), zero_sem)

    @pl.when(i == 0)
    def _():
        zbuf[...] = jnp.zeros_like(zbuf)
        in_copy(0, 0).start()

    @pl.when(i + 1 < nused)
    def _():
        in_copy(i + 1, 1 - slot).start()

    @pl.when(fresh)
    def _():
        wg_b[...] = wg_ref[0].astype(BF16)
        wu_b[...] = wu_ref[0].astype(BF16)
        wd_b[...] = wd_ref[0].astype(BF16)

    @pl.when(i < nused)
    def _():
        in_copy(i, slot).wait()
        pl.when(i >= 2)(lambda: out_copy(i - 2, slot).wait())
        lo, hi = _unpack_rows(xbuf[slot])
        xb = jnp.concatenate([lo.astype(BF16), hi.astype(BF16)], axis=1)
        hid = (_silu(_dot(xb, wg_b[...])) * _dot(xb, wu_b[...])).astype(BF16)
        y = _dot(hid, wd_b[...])
        ybuf[slot] = _pack_rows(y[:, :half], y[:, half:])
        out_copy(i, slot).start()

    pl.when(i >= nused)(lambda: zero_copy(i).start())

    @pl.when(i == nsteps - 1)
    def _():
        pl.when(nused >= 2)(lambda: out_copy(nused - 2, nused % 2).wait())
        pl.when(nused >= 1)(lambda: out_copy(nused - 1, (nused - 1) % 2).wait())
        lax.fori_loop(nused, nsteps, lambda b, c: (zero_copy(b).wait(), c)[1], 0)


def _experts(xs, blk_e, nused, w_gate, w_up, w_down):
    cap, _, w = xs.shape
    _, d, de = w_gate.shape
    bm = MOE_ROWS
    wspec = lambda blk: pl.BlockSpec(blk, lambda i, be, nu: (be[i], 0, 0))
    grid_spec = pltpu.PrefetchScalarGridSpec(
        num_scalar_prefetch=2,
        grid=(cap // bm,),
        in_specs=[pl.BlockSpec(memory_space=pl.ANY), wspec((1, d, de)), wspec((1, d, de)), wspec((1, de, d))],
        out_specs=pl.BlockSpec(memory_space=pl.ANY),
        scratch_shapes=[
            pltpu.VMEM((d, de), BF16),
            pltpu.VMEM((d, de), BF16),
            pltpu.VMEM((de, d), BF16),
            pltpu.VMEM((2, bm, w), jnp.uint32),
            pltpu.VMEM((2, bm, w), jnp.uint32),
            pltpu.VMEM((bm, w), jnp.uint32),
            pltpu.SemaphoreType.DMA((2,)),
            pltpu.SemaphoreType.DMA((2,)),
            pltpu.SemaphoreType.DMA,
        ],
    )
    return pl.pallas_call(
        _expert_kernel,
        grid_spec=grid_spec,
        out_shape=jax.ShapeDtypeStruct((cap, 1, w), jnp.uint32),
        compiler_params=pltpu.CompilerParams(
            dimension_semantics=("arbitrary",), vmem_limit_bytes=VMEM_LIMIT),
        name="experts",
    )(blk_e, nused, xs, w_gate, w_up, w_down)


def _combine_kernel(dest_ref, ys_ref, x2_ref, rw_ref, eye_ref, nw_ref, o_ref, idx_ref, idx_sem, ybuf, ysem):
    i = pl.program_id(0)
    nsteps = pl.num_programs(0)
    tc, d = x2_ref.shape
    half = d // 2
    n_idx = 2 * tc

    def idx_copy(step):
        sl = step % 3
        off = pl.multiple_of(sl * n_idx, n_idx)
        return pltpu.make_async_copy(dest_ref.at[step], idx_ref.at[pl.ds(off, n_idx)], idx_sem.at[sl])

    def gather(step):
        base = (step % 3) * n_idx
        buf = ybuf.at[step % 2]
        sem = ysem.at[step % 2]

        for t in range(n_idx):
            pltpu.make_async_copy(ys_ref.at[idx_ref[base + t]], buf.at[pl.ds(t, 1)], sem).start(priority=t % 2)

    @pl.when(i == 0)
    def _():
        idx_copy(0).start()
        idx_copy(0).wait()
        gather(0)

        @pl.when(nsteps > 1)
        def _():
            idx_copy(1).start()

    @pl.when(i + 1 < nsteps)
    def _():
        idx_copy(i + 1).wait()

        @pl.when(i + 2 < nsteps)
        def _():
            idx_copy(i + 2).start()

        gather(i + 1)

    slot = i % 2
    pltpu.make_async_copy(ybuf.at[slot], ybuf.at[slot], ysem.at[slot]).wait()
    eye = eye_ref[...]
    ew = eye.shape[0]

    def to_columns(r):
        r_a = r.astype(BF16)
        r_b = (r - r_a.astype(F32)).astype(BF16)
        r_c = (r - r_a.astype(F32) - r_b.astype(F32)).astype(BF16)
        return _dot_nt(eye, r_a) + (_dot_nt(eye, r_b) + _dot_nt(eye, r_c))

    wcol = jnp.concatenate([to_columns(rw_ref[:, j:j + ew]) for j in range(0, tc, ew)], axis=0)
    lo1, hi1 = _unpack_rows(ybuf[slot, 0:tc])
    lo2, hi2 = _unpack_rows(ybuf[slot, tc:n_idx])
    w1 = wcol[:, 0:1]
    w2 = wcol[:, 1:2]
    z_lo = x2_ref[:, :half] + (lo1 * w1 + lo2 * w2)
    z_hi = x2_ref[:, half:] + (hi1 * w1 + hi2 * w2)
    ms = (jnp.sum(z_lo * z_lo, axis=-1, keepdims=True) + jnp.sum(z_hi * z_hi, axis=-1, keepdims=True)) / d
    scale = lax.rsqrt(ms + EPS)
    o_ref[:, :half] = z_lo * scale * nw_ref[:, :half]
    o_ref[:, half:] = z_hi * scale * nw_ref[:, half:]


def _combine(x2, ys, dest_tiles, rw, eye, norm_w):
    n, d = x2.shape
    nt, n_idx = dest_tiles.shape
    tc = n_idx // 2
    w = ys.shape[-1]
    return pl.pallas_call(
        _combine_kernel,
        grid=(nt,),
        in_specs=[pl.BlockSpec(memory_space=pl.ANY)] * 2 + [
            pl.BlockSpec((tc, d), lambda i: (i, 0)),
            pl.BlockSpec((8, tc), lambda i: (0, i)),
            pl.BlockSpec(eye.shape, lambda i: (0, 0)),
            pl.BlockSpec((1, d), lambda i: (0, 0)),
        ],
        out_specs=pl.BlockSpec((tc, d), lambda i: (i, 0)),
        out_shape=jax.ShapeDtypeStruct((n, d), F32),
        scratch_shapes=[
            pltpu.SMEM((3 * n_idx,), jnp.int32),
            pltpu.SemaphoreType.DMA((3,)),
            pltpu.VMEM((2, n_idx, w), jnp.uint32),
            pltpu.SemaphoreType.DMA((2,)),
        ],
        compiler_params=pltpu.CompilerParams(
            dimension_semantics=("arbitrary",), vmem_limit_bytes=VMEM_LIMIT),
        name="combine",
    )(dest_tiles, ys, x2, rw, eye, norm_w)


def _layer(xf, mem, batch, seq, norm_mix_w, w_in, ret_norm_w, ml_conv_w, ml_conv_b, ml_gate_b, ml_norm_w,
           w_out, norm_xa_w, norm_mem_w, xa_wq, xa_wkv, xa_wo, norm_moe_w, moe_w_group, moe_b_group,
           moe_w_router, moe_b_router, moe_w_gate, moe_w_up, moe_w_down, final_norm_w):
    n, d = xf.shape
    def halves_first(w):
        return w.reshape(d, RET_HEADS, 2, RET_DK // 2).transpose(0, 2, 1, 3).reshape(d, RET_QK)

    w_main = jnp.concatenate([halves_first(w_in[:, OFF_RQ:OFF_RQ + RET_QK]),
                              halves_first(w_in[:, OFF_RK:OFF_RK + RET_QK]),
                              w_in[:, OFF_RV:MAIN_WIDTH]], axis=1).astype(BF16)
    w_if = w_in[:, MAIN_WIDTH:].astype(BF16)
    tabs = {k_: jnp.asarray(v_) for k_, v_ in _mixer_tables(seq).items()}
    proj, g, gt = _inproj(xf, norm_mix_w.reshape(1, d), w_main, w_if, w_if.T, tabs["cos"], tabs["sin"], ml_conv_w,
                          ml_conv_b.reshape(1, 2 * ML_QK), seq)
    mixed = _mixer(proj, g, gt, tabs, ret_norm_w.reshape(1, RET_V), ml_norm_w.reshape(1, ML_V), ml_gate_b,
                   batch, seq)

    kmem, vmem = _memkv(mem, norm_mem_w.reshape(1, d), xa_wkv.astype(BF16))

    w_route_t = jnp.concatenate(
        [moe_w_router.T, moe_w_group.T, jnp.zeros((ROUTE_ROWS - N_EXPERTS - N_GROUPS, d), F32)], axis=0)
    b_route = jnp.concatenate(
        [moe_b_router, moe_b_group, jnp.zeros((ROUTE_ROWS - N_EXPERTS - N_GROUPS,), F32)]).reshape(ROUTE_ROWS, 1)
    tm = TOKEN_TILE
    sut = jnp.asarray(np.triu(np.ones((tm, tm), np.float32), 1), dtype=BF16)
    x2, h3, ri, rw, cnt = _attn_route(xf, mixed, kmem, vmem, w_out.astype(BF16), norm_xa_w.reshape(1, d),
                                      xa_wq.astype(BF16), xa_wo.astype(BF16), norm_moe_w.reshape(1, d),
                                      w_route_t, b_route, sut, batch, seq)

    bm = MOE_ROWS
    counts = cnt[:, 0].astype(jnp.int32)
    padded = (counts + bm - 1) // bm * bm
    pends = jnp.cumsum(padded)
    pstarts = pends - padded
    expert = ri[0:TOP_K]
    onehot = expert[None] == jnp.arange(N_EXPERTS, dtype=jnp.int32)[:, None, None]
    dest = jnp.sum(jnp.where(onehot, pstarts[:, None, None], 0), axis=0) + ri[TOP_K:2 * TOP_K]
    cap = n * TOP_K + N_EXPERTS * bm
    nblk = cap // bm
    blk_start = jnp.arange(nblk, dtype=jnp.int32) * bm
    blk_e = jnp.minimum(jnp.sum(blk_start[:, None] >= pends[None, :], axis=1), N_EXPERTS - 1).astype(jnp.int32)
    nused = (pends[-1] // bm).astype(jnp.int32).reshape(1)
    zpos = jnp.where(padded > counts, pends - bm, -1).astype(jnp.int32)
    zpos = jnp.concatenate([zpos, nused])

    def tiles(rows):
        return dest.reshape(TOP_K, n // rows, rows).transpose(1, 0, 2).reshape(n // rows, TOP_K * rows)

    xs = _dispatch(h3, tiles(DISPATCH_TILE), zpos, cap)
    ys = _experts(xs, blk_e, nused, moe_w_gate, moe_w_up, moe_w_down)
    eye = jnp.asarray(np.eye(COMBINE_EYE, dtype=np.float32), dtype=BF16)
    return _combine(x2, ys, tiles(COMBINE_TILE), rw, eye, final_norm_w.reshape(1, d))


def kernel(x, mem, norm_mix_w, w_in, ret_norm_w, ml_conv_w, ml_conv_b, ml_gate_b, ml_norm_w, w_out, norm_xa_w, norm_mem_w, xa_wq, xa_wkv, xa_wo, norm_moe_w, moe_w_group, moe_b_group, moe_w_router, moe_b_router, moe_w_gate, moe_w_up, moe_w_down, norm_final_w):
    batch, seq, d = x.shape
    depth = w_in.shape[0]
    assert depth == 1, "the final norm is fused into the last layer's combine kernel"
    l = 0
    out = _layer(x.reshape(batch * seq, d), mem, batch, seq, norm_mix_w[l], w_in[l], ret_norm_w[l], ml_conv_w[l],
                 ml_conv_b[l], ml_gate_b[l], ml_norm_w[l], w_out[l], norm_xa_w[l], norm_mem_w[l], xa_wq[l],
                 xa_wkv[l], xa_wo[l], norm_moe_w[l], moe_w_group[l], moe_b_group[l], moe_w_router[l],
                 moe_b_router[l], moe_w_gate[l], moe_w_up[l], moe_w_down[l], norm_final_w)
    return out.reshape(batch, seq, d)
```

```python
import functools

import numpy as np
import jax
import jax.numpy as jnp
from jax import lax
from jax.experimental import pallas as pl
from jax.experimental.pallas import tpu as pltpu

F32 = jnp.float32
BF16 = jnp.bfloat16

CHUNK = 128
RET_HEADS = 4
RET_DK = 64
RET_DV = 128
ML_HEADS = 4
ML_DK = 128
ML_DV = 128
CONV_W = 4
XA_HEADS = 4
N_GROUPS = 4
EXP_PER_GROUP = 8
N_EXPERTS = N_GROUPS * EXP_PER_GROUP
TOP_K = 2
ROPE_BASE = 10000.0
EPS = 1e-6

RET_QK = RET_HEADS * RET_DK
RET_V = RET_HEADS * RET_DV
ML_QK = ML_HEADS * ML_DK
ML_V = ML_HEADS * ML_DV
OFF_RQ = 0
OFF_RK = OFF_RQ + RET_QK
OFF_RV = OFF_RK + RET_QK
OFF_RG = OFF_RV + RET_V
OFF_MQK = OFF_RG + RET_V
OFF_MV = OFF_MQK + 2 * ML_QK
OFF_MO = OFF_MV + ML_V
MAIN_WIDTH = OFF_MO + ML_V
N_GATES = 2 * ML_HEADS

ROUTE_ROWS = 40
TOKEN_TILE = 1024
MOE_ROWS = 512
DISPATCH_TILE = 1024
COMBINE_TILE = 512
COMBINE_EYE = 512
MIXER_BATCHES = 4
ATTN_GROUPS = 2
VMEM_LIMIT = 56 * 1024 * 1024


def _dot(a, b):
    return jnp.dot(a, b, preferred_element_type=F32)


def _dot_nt(a, b):
    return lax.dot_general(a, b, (((1,), (1,)), ((), ())), preferred_element_type=F32)


def _dot_tn(a, b):
    return lax.dot_general(a, b, (((0,), (0,)), ((), ())), preferred_element_type=F32)


def _rms(x, w):
    return x * lax.rsqrt(jnp.mean(x * x, axis=-1, keepdims=True) + EPS) * w


def _sigmoid(x):
    return 1.0 / (1.0 + jnp.exp(-x))


def _silu(x):
    return x * _sigmoid(x)


def _log_sigmoid(x):
    return jnp.minimum(x, 0.0) - jnp.log1p(jnp.exp(-jnp.abs(x)))


def _head_norms(ts, mean_w):
    mu = [_dot(t.astype(BF16), mean_w) for t in ts]
    dl = [t - m for t, m in zip(ts, mu)]
    var = [_dot((d * d).astype(BF16), mean_w) for d in dl]
    return [d * lax.rsqrt(v + EPS) for d, v in zip(dl, var)]


def _pack_rows(lo, hi):
    def bits(t):
        return lax.bitcast_convert_type(t.astype(BF16), jnp.uint16).astype(jnp.uint32)
    return bits(lo) | (bits(hi) << 16)


def _unpack_rows(u):
    lo = lax.bitcast_convert_type(u << 16, F32)
    hi = lax.bitcast_convert_type(u & jnp.uint32(0xFFFF0000), F32)
    return lo, hi


def _inproj_kernel(tiles_per_seq, x_ref, nw_ref, w_ref, wif_ref, wift_ref, cos_ref, sin_ref, convw_ref, convb_ref,
                   proj_ref, g_ref, gt_ref, carry_ref):
    tm = x_ref.shape[0]

    @pl.when(pl.program_id(0) == 0)
    def _():
        carry_ref[...] = jnp.zeros_like(carry_ref)

    h = _rms(x_ref[...], nw_ref[...]).astype(BF16)

    def mm(off, width):
        return _dot(h, w_ref[:, off:off + width])

    def rotary(qk):
        cos = cos_ref[...]
        sin = sin_ref[...]
        half = RET_QK // 2
        for off, scale in ((OFF_RQ, None), (OFF_RK, RET_DK ** -0.5)):
            t1 = qk[:, off - OFF_RQ:off - OFF_RQ + half]
            t2 = qk[:, off - OFF_RQ + half:off - OFF_RQ + 2 * half]
            r1 = t1 * cos - t2 * sin
            r2 = t1 * sin + t2 * cos
            if scale is not None:
                r1, r2 = r1 * scale, r2 * scale
            proj_ref[:, off:off + half] = r1.astype(BF16)
            proj_ref[:, off + half:off + 2 * half] = r2.astype(BF16)

    def conv_silu(part, scale, cur):
        c0 = part * ML_QK
        first = (pl.program_id(0) % tiles_per_seq) == 0
        row8 = lax.broadcasted_iota(jnp.int32, (8, ML_QK), 0)
        prev = jnp.where(first, 0.0, carry_ref[:, c0:c0 + ML_QK])
        acc = cur * convw_ref[CONV_W - 1:CONV_W, c0:c0 + ML_QK] + convb_ref[:, c0:c0 + ML_QK]
        for s in range(1, CONV_W):
            rolled = pltpu.roll(cur, s, 0)
            head8 = jnp.where(row8 < s, pltpu.roll(prev, s, 0), rolled[0:8])
            shifted = jnp.concatenate([head8, rolled[8:]], axis=0)
            acc = acc + shifted * convw_ref[CONV_W - 1 - s:CONV_W - s, c0:c0 + ML_QK]
        carry_ref[:, c0:c0 + ML_QK] = cur[tm - 8:tm]
        act = _silu(acc) if scale is None else _silu(acc) * scale
        proj_ref[:, OFF_MQK + c0:OFF_MQK + c0 + ML_QK] = act.astype(BF16)

    def store(off, width, fn=None):
        def ep(t):
            proj_ref[:, off:off + width] = (t if fn is None else fn(t)).astype(BF16)
        return ep

    rotary(mm(OFF_RQ, 2 * RET_QK))
    store(OFF_RV, RET_V)(mm(OFF_RV, RET_V))
    store(OFF_RG, RET_V, _silu)(mm(OFF_RG, RET_V))
    conv_silu(0, None, mm(OFF_MQK, ML_QK))
    conv_silu(1, ML_DK ** -0.5, mm(OFF_MQK + ML_QK, ML_QK))
    store(OFF_MV, ML_V)(mm(OFF_MV, ML_V))
    store(OFF_MO, ML_V, _sigmoid)(mm(OFF_MO, ML_V))
    g_ref[...] = _dot(h, wif_ref[...])
    gt_ref[...] = _dot_nt(wift_ref[...], h)


def _inproj(xf, norm_w, w_main, w_if, w_ift, cos, sin, conv_w, conv_b, seq):
    n, d = xf.shape
    tm = TOKEN_TILE
    tiles_per_seq = seq // tm
    return pl.pallas_call(
        functools.partial(_inproj_kernel, tiles_per_seq),
        grid=(n // tm,),
        in_specs=[
            pl.BlockSpec((tm, d), lambda i: (i, 0)),
            pl.BlockSpec((1, d), lambda i: (0, 0)),
            pl.BlockSpec((d, MAIN_WIDTH), lambda i: (0, 0)),
            pl.BlockSpec((d, N_GATES), lambda i: (0, 0)),
            pl.BlockSpec((N_GATES, d), lambda i: (0, 0)),
            pl.BlockSpec((tm, RET_QK // 2), lambda i: (i % tiles_per_seq, 0)),
            pl.BlockSpec((tm, RET_QK // 2), lambda i: (i % tiles_per_seq, 0)),
            pl.BlockSpec((CONV_W, 2 * ML_QK), lambda i: (0, 0)),
            pl.BlockSpec((1, 2 * ML_QK), lambda i: (0, 0)),
        ],
        out_specs=[
            pl.BlockSpec((tm, MAIN_WIDTH), lambda i: (i, 0)),
            pl.BlockSpec((tm, N_GATES), lambda i: (i, 0)),
            pl.BlockSpec((N_GATES, tm), lambda i: (0, i)),
        ],
        out_shape=[
            jax.ShapeDtypeStruct((n, MAIN_WIDTH), BF16),
            jax.ShapeDtypeStruct((n, N_GATES), F32),
            jax.ShapeDtypeStruct((N_GATES, n), F32),
        ],
        scratch_shapes=[pltpu.VMEM((8, 2 * ML_QK), F32)],
        compiler_params=pltpu.CompilerParams(
            dimension_semantics=("arbitrary",), vmem_limit_bytes=VMEM_LIMIT),
        name="inproj",
    )(xf, norm_w, w_main, w_if, w_ift, cos, sin, conv_w, conv_b)


def _mixer_kernel(*refs):
    @pl.when(pl.program_id(1) == 0)
    def _():
        for state_ref in refs[-4:]:
            state_ref[...] = jnp.zeros_like(state_ref)

    for bi in range(refs[0].shape[0]):
        _mixer_one(bi, *refs)


def _mixer_one(bi, proj_ref, g_ref, gt_ref, qdec_ref, kdec_ref, dmat_ref,
               bmask_ref, cdec_ref, hmask_ref, tril_ref, triu_ref, ones_ref, retw_ref, mlw_ref, gbc_ref, gbr_ref,
               out_ref, r_ref, c_ref, n_ref, m_ref):
    L = CHUNK
    proj_ref, g_ref, gt_ref, out_ref = proj_ref.at[bi], g_ref.at[bi], gt_ref.at[bi], out_ref.at[bi]
    r_ref, c_ref, n_ref, m_ref = r_ref.at[bi], c_ref.at[bi], n_ref.at[bi], m_ref.at[bi]
    mean_w = ones_ref[...]

    q = proj_ref[:, OFF_RQ:OFF_RQ + RET_QK].astype(F32)
    k_b = proj_ref[:, OFF_RK:OFF_RK + RET_QK]
    k = k_b.astype(F32)
    v = proj_ref[:, OFF_RV:OFF_RV + RET_V]
    r_prev = r_ref[...]
    cross = _dot((q * qdec_ref[...]).astype(BF16), r_prev.astype(BF16))
    kv = _dot_tn((k * kdec_ref[...]).astype(BF16), v) * bmask_ref[...]
    r_ref[...] = cdec_ref[...] * r_prev + kv
    RH = range(RET_HEADS)
    sc = [_dot_nt((q * hmask_ref[h:h + 1, :]).astype(BF16), k_b) for h in RH]
    sc = [(sc[h] * dmat_ref[h]).astype(BF16) for h in RH]
    tot = [_dot(sc[h], v[:, h * RET_DV:(h + 1) * RET_DV]) + cross[:, h * RET_DV:(h + 1) * RET_DV] for h in RH]
    ret = jnp.concatenate(_head_norms(tot, mean_w), axis=1) * retw_ref[...]
    ret = ret * proj_ref[:, OFF_RG:OFF_RG + RET_V].astype(F32)
    out_ref[:, 0:RET_V] = ret.astype(BF16)

    mq = proj_ref[:, OFF_MQK:OFF_MQK + ML_QK]
    mk = proj_ref[:, OFF_MQK + ML_QK:OFF_MQK + 2 * ML_QK]
    mv = proj_ref[:, OFF_MV:OFF_MV + ML_V]

    gc = g_ref[...] + gbc_ref[...]
    gr = gt_ref[...] + gbr_ref[...]
    lf_c = _log_sigmoid(gc)
    lf_r = _log_sigmoid(gr)
    lf_c_hi = lf_c.astype(BF16)
    lf_r_hi = lf_r.astype(BF16)
    b_c = (_dot(tril_ref[...], lf_c_hi)
           + _dot(tril_ref[...], (lf_c - lf_c_hi.astype(F32)).astype(BF16)))
    b_r = (_dot(lf_r_hi, triu_ref[...])
           + _dot((lf_r - lf_r_hi.astype(F32)).astype(BF16), triu_ref[...]))
    causal = (lax.broadcasted_iota(jnp.int32, (L, L), 0) >= lax.broadcasted_iota(jnp.int32, (L, L), 1))
    MH = range(ML_HEADS)
    bc = [b_c[:, ML_HEADS + h:ML_HEADS + h + 1] for h in MH]
    br = [b_r[ML_HEADS + h:ML_HEADS + h + 1, :] for h in MH]
    igc = [gc[:, h:h + 1] for h in MH]
    igr = [gr[h:h + 1, :] for h in MH]
    btot = [br[h][:, L - 1:L] for h in MH]
    qh_b = [mq[:, h * ML_DK:(h + 1) * ML_DK] for h in MH]
    kh_b = [mk[:, h * ML_DK:(h + 1) * ML_DK] for h in MH]
    vh = [mv[:, h * ML_DV:(h + 1) * ML_DV] for h in MH]
    c_prev = [c_ref[h] for h in MH]
    n_prev = [n_ref[h][0:1, :] for h in MH]
    m_prev = [m_ref[h][0:1, 0:1] for h in MH]
    s_raw = [_dot_nt(qh_b[h], kh_b[h]) for h in MH]
    qc = [_dot(qh_b[h], c_prev[h].astype(BF16)) for h in MH]
    log_d = [jnp.where(causal, bc[h] - br[h] + igr[h], -jnp.inf) for h in MH]
    m_intra = [jnp.max(log_d[h], axis=1, keepdims=True) for h in MH]
    m_loc = [jnp.max(btot[h] - br[h] + igr[h], axis=1, keepdims=True) for h in MH]
    kw = [kh_b[h].astype(F32) * jnp.exp(btot[h] - bc[h] + igc[h] - m_loc[h]) for h in MH]
    kv_loc = [_dot_tn(kw[h].astype(BF16), vh[h]) for h in MH]
    n_loc = [jnp.sum(kw[h], axis=0, keepdims=True) for h in MH]
    m_inter = [bc[h] + m_prev[h] for h in MH]
    m_t = [jnp.maximum(m_intra[h], m_inter[h]) for h in MH]
    s_mat = [s_raw[h] * jnp.exp(log_d[h] - m_t[h]) for h in MH]
    inter = [jnp.exp(m_inter[h] - m_t[h]) for h in MH]
    num = [_dot(s_mat[h].astype(BF16), vh[h]) + inter[h] * qc[h] for h in MH]
    den = [jnp.sum(s_mat[h], axis=1, keepdims=True)
           + inter[h] * jnp.sum(qh_b[h].astype(F32) * n_prev[h], axis=1, keepdims=True) for h in MH]
    hh = [num[h] / jnp.maximum(jnp.abs(den[h]), jnp.exp(-m_t[h])) for h in MH]
    for h in MH:
        m_new = jnp.maximum(btot[h] + m_prev[h], m_loc[h])
        s_old = jnp.exp(btot[h] + m_prev[h] - m_new)
        s_loc = jnp.exp(m_loc[h] - m_new)
        c_ref[h] = s_old * c_prev[h] + s_loc * kv_loc[h]
        n_ref[h] = jnp.broadcast_to(s_old * n_prev[h] + s_loc * n_loc[h], (8, ML_DK))
        m_ref[h] = jnp.broadcast_to(m_new, (8, 128))
    ml = jnp.concatenate(_head_norms(hh, mean_w), axis=1) * mlw_ref[...]
    ml = ml * proj_ref[:, OFF_MO:OFF_MO + ML_V].astype(F32)
    out_ref[:, RET_V:RET_V + ML_V] = ml.astype(BF16)


def _mixer_tables(seq):
    L = CHUNK
    half = RET_DK // 2
    inv = ROPE_BASE ** (-np.arange(half, dtype=np.float64) / half)
    ang = np.arange(seq, dtype=np.float64)[:, None] * inv[None, :].astype(np.float32).astype(np.float64)
    cos = np.tile(np.cos(ang), (1, RET_HEADS)).astype(np.float32)
    sin = np.tile(np.sin(ang), (1, RET_HEADS)).astype(np.float32)
    log_g = np.log1p(-np.exp2(-5.0 - np.arange(RET_HEADS, dtype=np.float64)))
    n = np.arange(L, dtype=np.float64)
    lane_head = (np.arange(RET_QK) % (RET_QK // 2)) // half
    qdec = np.exp((n + 1)[:, None] * log_g[lane_head][None, :]).astype(np.float32)
    kdec = np.exp((L - 1 - n)[:, None] * log_g[lane_head][None, :]).astype(np.float32)
    diff = n[:, None] - n[None, :]
    dmat = np.where(diff >= 0, np.exp(log_g[:, None, None] * np.maximum(diff, 0.0)[None]), 0.0).astype(np.float32)
    col_head = np.arange(RET_V) // RET_DV
    bmask = (lane_head[:, None] == col_head[None, :]).astype(np.float32)
    cdec = np.exp(L * log_g[col_head])[None, :].astype(np.float32)
    hmask = (lane_head[None, :] == np.arange(RET_HEADS)[:, None]).astype(np.float32)
    hmask = np.concatenate([hmask, np.zeros((8 - RET_HEADS, RET_QK), np.float32)], axis=0)
    tril = np.tril(np.ones((L, L), np.float32))
    ones = np.full((RET_DV, RET_DV), 1.0 / RET_DV, np.float32)
    return dict(cos=cos, sin=sin, qdec=qdec, kdec=kdec, dmat=dmat, bmask=bmask, cdec=cdec, hmask=hmask,
                tril=tril, triu=np.ascontiguousarray(tril.T), ones=ones)


def _mixer(proj, g, gt, tabs, ret_norm_w, ml_norm_w, gate_b, batch, seq):
    L = CHUNK
    nc = seq // L
    n = batch * seq
    nb = MIXER_BATCHES if batch % MIXER_BATCHES == 0 else 1
    proj = proj.reshape(batch, seq, MAIN_WIDTH)
    g = g.reshape(batch, seq, N_GATES)
    gt = gt.reshape(N_GATES, batch, seq).transpose(1, 0, 2)
    const2 = lambda b, c: (0, 0)
    const3 = lambda b, c: (0, 0, 0)
    tok = lambda b, c: (b, c, 0)
    in_specs = [
        pl.BlockSpec((nb, L, MAIN_WIDTH), tok),
        pl.BlockSpec((nb, L, N_GATES), tok),
        pl.BlockSpec((nb, N_GATES, L), lambda b, c: (b, 0, c)),
        pl.BlockSpec((L, RET_QK), const2),
        pl.BlockSpec((L, RET_QK), const2),
        pl.BlockSpec((RET_HEADS, L, L), const3),
        pl.BlockSpec((RET_QK, RET_V), const2),
        pl.BlockSpec((1, RET_V), const2),
        pl.BlockSpec((8, RET_QK), const2),
        pl.BlockSpec((L, L), const2),
        pl.BlockSpec((L, L), const2),
        pl.BlockSpec((RET_DV, RET_DV), const2),
        pl.BlockSpec((1, RET_V), const2),
        pl.BlockSpec((1, ML_V), const2),
        pl.BlockSpec((1, N_GATES), const2),
        pl.BlockSpec((N_GATES, 1), const2),
    ]
    return pl.pallas_call(
        _mixer_kernel,
        grid=(batch // nb, nc),
        in_specs=in_specs,
        out_specs=pl.BlockSpec((nb, L, RET_V + ML_V), tok),
        out_shape=jax.ShapeDtypeStruct((batch, seq, RET_V + ML_V), BF16),
        scratch_shapes=[
            pltpu.VMEM((nb, RET_QK, RET_V), F32),
            pltpu.VMEM((nb, ML_HEADS, ML_DK, ML_DV), F32),
            pltpu.VMEM((nb, ML_HEADS, 8, ML_DK), F32),
            pltpu.VMEM((nb, ML_HEADS, 8, 128), F32),
        ],
        compiler_params=pltpu.CompilerParams(
            dimension_semantics=("arbitrary", "arbitrary"), vmem_limit_bytes=VMEM_LIMIT),
        name="mixer",
    )(proj, g, gt, tabs["qdec"], tabs["kdec"], tabs["dmat"], tabs["bmask"],
      tabs["cdec"], tabs["hmask"], tabs["tril"].astype(BF16), tabs["triu"].astype(BF16), tabs["ones"].astype(BF16),
      ret_norm_w, ml_norm_w,
      gate_b.reshape(1, N_GATES), gate_b.reshape(N_GATES, 1)).reshape(n, RET_V + ML_V)


def _memkv_kernel(mem_ref, nw_ref, wkv_ref, k_ref, v_ref):
    d = mem_ref.shape[-1]
    mn = _rms(mem_ref[0], nw_ref[...]).astype(BF16)
    k_ref[0] = _dot(mn, wkv_ref[:, :d]).astype(BF16)
    v_ref[0] = _dot(mn, wkv_ref[:, d:]).astype(BF16)


def _memkv(mem, norm_w, wkv):
    b, m, d = mem.shape
    return pl.pallas_call(
        _memkv_kernel,
        grid=(b,),
        in_specs=[
            pl.BlockSpec((1, m, d), lambda i: (i, 0, 0)),
            pl.BlockSpec((1, d), lambda i: (0, 0)),
            pl.BlockSpec((d, 2 * d), lambda i: (0, 0)),
        ],
        out_specs=[pl.BlockSpec((1, m, d), lambda i: (i, 0, 0))] * 2,
        out_shape=[jax.ShapeDtypeStruct((b, m, d), BF16)] * 2,
        compiler_params=pltpu.CompilerParams(
            dimension_semantics=("arbitrary",), vmem_limit_bytes=VMEM_LIMIT),
        name="memkv",
    )(mem, norm_w, wkv)


def _attn_route_kernel(x_ref, mix_ref, k_ref, v_ref, wout_ref, nxa_ref, wq_ref, wo_ref, nmoe_ref,
                       wr_ref, wrlo_ref, br_ref, sut_ref,
                       x2_ref, h3_ref, ri_ref, rw_ref, cnt_ref, carry_ref):
    tm, d = x_ref.shape
    dh = d // XA_HEADS

    @pl.when((pl.program_id(0) == 0) & (pl.program_id(1) == 0))
    def _():
        carry_ref[...] = jnp.zeros_like(carry_ref)

    groups = [slice(g * (tm // ATTN_GROUPS), (g + 1) * (tm // ATTN_GROUPS)) for g in range(ATTN_GROUPS)]
    x1 = [x_ref[s, :] + _dot(mix_ref[s, :], wout_ref[...]) for s in groups]
    h2 = [_rms(t, nxa_ref[...]).astype(BF16) for t in x1]
    q = [_dot(t, wq_ref[...]).astype(BF16) for t in h2]
    o = []
    for qg in q:
        heads = []
        for h in range(XA_HEADS):
            logits = _dot_nt(qg[:, h * dh:(h + 1) * dh], k_ref[0, :, h * dh:(h + 1) * dh]) * (dh ** -0.5)
            mx = jnp.max(logits, axis=-1, keepdims=True)
            e = jnp.exp(logits - mx)
            p = (e / jnp.sum(e, axis=-1, keepdims=True)).astype(BF16)
            heads.append(_dot(p, v_ref[0, :, h * dh:(h + 1) * dh]).astype(BF16))
        o.append(jnp.concatenate(heads, axis=1))
    x2 = [a + _dot(b, wo_ref[...]) for a, b in zip(x1, o)]
    for s, t in zip(groups, x2):
        x2_ref[s, :] = t
    h3 = [_rms(t, nmoe_ref[...]) for t in x2]
    for s, t in zip(groups, h3):
        h3_ref[s, 0, :] = _pack_rows(t[:, :d // 2], t[:, d // 2:])

    lts = []
    for t in h3:
        t_hi = t.astype(BF16)
        t_lo = (t - t_hi.astype(F32)).astype(BF16)
        lts.append(_dot_nt(wr_ref[...], t_hi) + (_dot_nt(wr_ref[...], t_lo) + _dot_nt(wrlo_ref[...], t_hi)))
    lt = jnp.concatenate(lts, axis=1) + br_ref[...]
    gl = lt[N_EXPERTS:N_EXPERTS + N_GROUPS]
    gmax = jnp.max(gl, axis=0, keepdims=True)
    g_w = 1.0 / jnp.sum(jnp.exp(gl - gmax), axis=0, keepdims=True)
    giota = lax.broadcasted_iota(jnp.int32, gl.shape, 0)
    g_sel = jnp.min(jnp.where(gl == gmax, giota, N_GROUPS), axis=0, keepdims=True)
    el = lt[0:N_EXPERTS]
    eiota = lax.broadcasted_iota(jnp.int32, el.shape, 0)
    in_grp = (eiota // EXP_PER_GROUP) == g_sel
    elm = jnp.where(in_grp, el, -jnp.inf)
    m1 = jnp.max(elm, axis=0, keepdims=True)
    esum = jnp.sum(jnp.where(in_grp, jnp.exp(el - m1), 0.0), axis=0, keepdims=True)
    i1 = jnp.min(jnp.where(elm == m1, eiota, N_EXPERTS), axis=0, keepdims=True)
    elm2 = jnp.where(eiota == i1, -jnp.inf, elm)
    m2 = jnp.max(elm2, axis=0, keepdims=True)
    i2 = jnp.min(jnp.where(elm2 == m2, eiota, N_EXPERTS), axis=0, keepdims=True)
    p1 = 1.0 / esum
    p2 = jnp.exp(m2 - m1) / esum
    psum = p1 + p2
    w1 = g_w * (p1 / psum)
    w2 = g_w * (p2 / psum)

    oh1 = (eiota == i1).astype(F32)
    oh2 = (eiota == i2).astype(F32)
    cnt = oh1 + oh2
    base = carry_ref[:, 0:1] + _dot(cnt.astype(BF16), sut_ref[...])
    r1 = jnp.sum(oh1 * base, axis=0, keepdims=True)
    r2 = jnp.sum(oh2 * base, axis=0, keepdims=True)
    new_carry = carry_ref[...] + jnp.sum(cnt, axis=1, keepdims=True)
    carry_ref[...] = new_carry
    cnt_ref[...] = new_carry

    zi = jnp.zeros((4, tm), jnp.int32)
    ri_ref[...] = jnp.concatenate([i1, i2, r1.astype(jnp.int32), r2.astype(jnp.int32), zi], axis=0)
    rw_ref[...] = jnp.concatenate([w1, w2, jnp.zeros((6, tm), F32)], axis=0)


def _attn_route(xf, mixed, kmem, vmem, w_out, norm_xa_w, wq, wo, norm_moe_w, w_route_t, b_route, sut,
                batch, seq):
    n, d = xf.shape
    w_route_hi = w_route_t.astype(BF16)
    w_route_lo = (w_route_t - w_route_hi.astype(F32)).astype(BF16)
    tm = TOKEN_TILE
    nt = seq // tm
    m = kmem.shape[1]
    tok = lambda b, t: (b * nt + t, 0)
    lane_tok = lambda b, t: (0, b * nt + t)
    const2 = lambda b, t: (0, 0)
    return pl.pallas_call(
        _attn_route_kernel,
        grid=(batch, nt),
        in_specs=[
            pl.BlockSpec((tm, d), tok),
            pl.BlockSpec((tm, d), tok),
            pl.BlockSpec((1, m, d), lambda b, t: (b, 0, 0)),
            pl.BlockSpec((1, m, d), lambda b, t: (b, 0, 0)),
            pl.BlockSpec((d, d), const2),
            pl.BlockSpec((1, d), const2),
            pl.BlockSpec((d, d), const2),
            pl.BlockSpec((d, d), const2),
            pl.BlockSpec((1, d), const2),
            pl.BlockSpec((ROUTE_ROWS, d), const2),
            pl.BlockSpec((ROUTE_ROWS, d), const2),
            pl.BlockSpec((ROUTE_ROWS, 1), const2),
            pl.BlockSpec((tm, tm), const2),
        ],
        out_specs=[
            pl.BlockSpec((tm, d), tok),
            pl.BlockSpec((tm, 1, d // 2), lambda b, t: (b * nt + t, 0, 0)),
            pl.BlockSpec((8, tm), lane_tok),
            pl.BlockSpec((8, tm), lane_tok),
            pl.BlockSpec((N_EXPERTS, 128), const2),
        ],
        out_shape=[
            jax.ShapeDtypeStruct((n, d), F32),
            jax.ShapeDtypeStruct((n, 1, d // 2), jnp.uint32),
            jax.ShapeDtypeStruct((8, n), jnp.int32),
            jax.ShapeDtypeStruct((8, n), F32),
            jax.ShapeDtypeStruct((N_EXPERTS, 128), F32),
        ],
        scratch_shapes=[pltpu.VMEM((N_EXPERTS, 128), F32)],
        compiler_params=pltpu.CompilerParams(
            dimension_semantics=("arbitrary", "arbitrary"), vmem_limit_bytes=VMEM_LIMIT),
        name="attn_route",
    )(xf, mixed, kmem, vmem, w_out, norm_xa_w, wq, wo, norm_moe_w, w_route_hi, w_route_lo, b_route, sut)


def _dispatch_kernel(zpos_ref, dest_ref, h_ref, xs_ref, idx_ref, idx_sem, row_sem, zero_ref, zero_sem):
    i = pl.program_id(0)
    nsteps = pl.num_programs(0)
    td = h_ref.shape[0]
    bm = zero_ref.shape[0]
    slot = i % 2

    def idx_copy(step, sl):
        off = pl.multiple_of(sl * (2 * td), 2 * td)
        return pltpu.make_async_copy(dest_ref.at[step], idx_ref.at[pl.ds(off, 2 * td)], idx_sem.at[sl])

    @pl.when(i == 0)
    def _():
        zero_ref[...] = jnp.zeros_like(zero_ref)

        def zero_copy(e):
            return pltpu.make_async_copy(zero_ref, xs_ref.at[pl.ds(pl.multiple_of(zpos_ref[e], bm), bm), 0], zero_sem)

        def tail_copy(b):
            return pltpu.make_async_copy(zero_ref, xs_ref.at[pl.ds(pl.multiple_of(b * bm, bm), bm), 0], zero_sem)

        nused = zpos_ref[N_EXPERTS]
        nblk = xs_ref.shape[0] // bm
        for e in range(N_EXPERTS):
            pl.when(zpos_ref[e] >= 0)(lambda e=e: zero_copy(e).start())
        lax.fori_loop(nused, nblk, lambda b, c: (tail_copy(b).start(), c)[1], 0)
        for e in range(N_EXPERTS):
            pl.when(zpos_ref[e] >= 0)(lambda e=e: zero_copy(e).wait())
        lax.fori_loop(nused, nblk, lambda b, c: (tail_copy(b).wait(), c)[1], 0)
        idx_copy(0, 0).start()

    idx_copy(i, slot).wait()

    @pl.when(i + 1 < nsteps)
    def _():
        idx_copy(i + 1, 1 - slot).start()

    base = slot * (2 * td)

    for t in range(td):
        for k in range(TOP_K):
            pltpu.make_async_copy(h_ref.at[t], xs_ref.at[idx_ref[base + k * td + t]], row_sem).start(priority=t % 2)
    for _ in range(TOP_K):
        pltpu.make_async_copy(xs_ref.at[pl.ds(0, td)], xs_ref.at[pl.ds(0, td)], row_sem).wait()


def _dispatch(h3p, dest_tiles, zpos, cap):
    w = h3p.shape[-1]
    nt, td2 = dest_tiles.shape
    td = td2 // 2
    grid_spec = pltpu.PrefetchScalarGridSpec(
        num_scalar_prefetch=1,
        grid=(nt,),
        in_specs=[
            pl.BlockSpec(memory_space=pl.ANY),
            pl.BlockSpec((td, 1, w), lambda i, zp: (i, 0, 0)),
        ],
        out_specs=pl.BlockSpec(memory_space=pl.ANY),
        scratch_shapes=[
            pltpu.SMEM((2 * td2,), jnp.int32),
            pltpu.SemaphoreType.DMA((2,)),
            pltpu.SemaphoreType.DMA,
            pltpu.VMEM((MOE_ROWS, w), jnp.uint32),
            pltpu.SemaphoreType.DMA,
        ],
    )
    return pl.pallas_call(
        _dispatch_kernel,
        grid_spec=grid_spec,
        out_shape=jax.ShapeDtypeStruct((cap, 1, w), jnp.uint32),
        compiler_params=pltpu.CompilerParams(
            dimension_semantics=("arbitrary",), vmem_limit_bytes=VMEM_LIMIT),
        name="dispatch",
    )(zpos, dest_tiles, h3p)


def _expert_kernel(blk_e_ref, nused_ref, xs_ref, wg_ref, wu_ref, wd_ref, ys_ref, wg_b, wu_b, wd_b,
                   xbuf, ybuf, zbuf, in_sem, out_sem, zero_sem):
    i = pl.program_id(0)
    nsteps = pl.num_programs(0)
    nused = nused_ref[0]
    bm = xbuf.shape[1]
    slot = i % 2
    prev = blk_e_ref[jnp.maximum(i - 1, 0)]
    fresh = (i == 0) | (blk_e_ref[i] != prev)
    half = wd_b.shape[1] // 2

    def rows(ref, step):
        return ref.at[pl.ds(pl.multiple_of(step * bm, bm), bm), 0]

    def in_copy(step, sl):
        return pltpu.make_async_copy(rows(xs_ref, step), xbuf.at[sl], in_sem.at[sl])

    def out_copy(step, sl):
        return pltpu.make_async_copy(ybuf.at[sl], rows(ys_ref, step), out_sem.at[sl])

    def zero_copy(step):
        return pltpu.make_async_copy(zbuf, rows(ys_ref, step), zero_sem)

    @pl.when(i == 0)
    def _():
        zbuf[...] = jnp.zeros_like(zbuf)
        in_copy(0, 0).start()

    @pl.when(i + 1 < nused)
    def _():
        in_copy(i + 1, 1 - slot).start()

    @pl.when(fresh)
    def _():
        wg_b[...] = wg_ref[0].astype(BF16)
        wu_b[...] = wu_ref[0].astype(BF16)
        wd_b[...] = wd_ref[0].astype(BF16)

    @pl.when(i < nused)
    def _():
        in_copy(i, slot).wait()
        pl.when(i >= 2)(lambda: out_copy(i - 2, slot).wait())
        lo, hi = _unpack_rows(xbuf[slot])
        xb = jnp.concatenate([lo.astype(BF16), hi.astype(BF16)], axis=1)
        hid = (_silu(_dot(xb, wg_b[...])) * _dot(xb, wu_b[...])).astype(BF16)
        y = _dot(hid, wd_b[...])
        ybuf[slot] = _pack_rows(y[:, :half], y[:, half:])
        out_copy(i, slot).start()

    pl.when(i >= nused)(lambda: zero_copy(i).start())

    @pl.when(i == nsteps - 1)
    def _():
        pl.when(nused >= 2)(lambda: out_copy(nused - 2, nused % 2).wait())
        pl.when(nused >= 1)(lambda: out_copy(nused - 1, (nused - 1) % 2).wait())
        lax.fori_loop(nused, nsteps, lambda b, c: (zero_copy(b).wait(), c)[1], 0)


def _experts(xs, blk_e, nused, w_gate, w_up, w_down):
    cap, _, w = xs.shape
    _, d, de = w_gate.shape
    bm = MOE_ROWS
    wspec = lambda blk: pl.BlockSpec(blk, lambda i, be, nu: (be[i], 0, 0))
    grid_spec = pltpu.PrefetchScalarGridSpec(
        num_scalar_prefetch=2,
        grid=(cap // bm,),
        in_specs=[pl.BlockSpec(memory_space=pl.ANY), wspec((1, d, de)), wspec((1, d, de)), wspec((1, de, d))],
        out_specs=pl.BlockSpec(memory_space=pl.ANY),
        scratch_shapes=[
            pltpu.VMEM((d, de), BF16),
            pltpu.VMEM((d, de), BF16),
            pltpu.VMEM((de, d), BF16),
            pltpu.VMEM((2, bm, w), jnp.uint32),
            pltpu.VMEM((2, bm, w), jnp.uint32),
            pltpu.VMEM((bm, w), jnp.uint32),
            pltpu.SemaphoreType.DMA((2,)),
            pltpu.SemaphoreType.DMA((2,)),
            pltpu.SemaphoreType.DMA,
        ],
    )
    return pl.pallas_call(
        _expert_kernel,
        grid_spec=grid_spec,
        out_shape=jax.ShapeDtypeStruct((cap, 1, w), jnp.uint32),
        compiler_params=pltpu.CompilerParams(
            dimension_semantics=("arbitrary",), vmem_limit_bytes=VMEM_LIMIT),
        name="experts",
    )(blk_e, nused, xs, w_gate, w_up, w_down)


def _combine_kernel(dest_ref, ys_ref, x2_ref, rw_ref, eye_ref, nw_ref, o_ref, idx_ref, idx_sem, ybuf, ysem):
    i = pl.program_id(0)
    nsteps = pl.num_programs(0)
    tc, d = x2_ref.shape
    half = d // 2
    n_idx = 2 * tc

    def idx_copy(step):
        sl = step % 3
        off = pl.multiple_of(sl * n_idx, n_idx)
        return pltpu.make_async_copy(dest_ref.at[step], idx_ref.at[pl.ds(off, n_idx)], idx_sem.at[sl])

    def gather(step):
        base = (step % 3) * n_idx
        buf = ybuf.at[step % 2]
        sem = ysem.at[step % 2]

        for t in range(n_idx):
            pltpu.make_async_copy(ys_ref.at[idx_ref[base + t]], buf.at[pl.ds(t, 1)], sem).start(priority=t % 2)

    @pl.when(i == 0)
    def _():
        idx_copy(0).start()
        idx_copy(0).wait()
        gather(0)

        @pl.when(nsteps > 1)
        def _():
            idx_copy(1).start()

    @pl.when(i + 1 < nsteps)
    def _():
        idx_copy(i + 1).wait()

        @pl.when(i + 2 < nsteps)
        def _():
            idx_copy(i + 2).start()

        gather(i + 1)

    slot = i % 2
    pltpu.make_async_copy(ybuf.at[slot], ybuf.at[slot], ysem.at[slot]).wait()
    eye = eye_ref[...]
    ew = eye.shape[0]

    def to_columns(r):
        r_a = r.astype(BF16)
        r_b = (r - r_a.astype(F32)).astype(BF16)
        r_c = (r - r_a.astype(F32) - r_b.astype(F32)).astype(BF16)
        return _dot_nt(eye, r_a) + (_dot_nt(eye, r_b) + _dot_nt(eye, r_c))

    wcol = jnp.concatenate([to_columns(rw_ref[:, j:j + ew]) for j in range(0, tc, ew)], axis=0)
    lo1, hi1 = _unpack_rows(ybuf[slot, 0:tc])
    lo2, hi2 = _unpack_rows(ybuf[slot, tc:n_idx])
    w1 = wcol[:, 0:1]
    w2 = wcol[:, 1:2]
    z_lo = x2_ref[:, :half] + (lo1 * w1 + lo2 * w2)
    z_hi = x2_ref[:, half:] + (hi1 * w1 + hi2 * w2)
    ms = (jnp.sum(z_lo * z_lo, axis=-1, keepdims=True) + jnp.sum(z_hi * z_hi, axis=-1, keepdims=True)) / d
    scale = lax.rsqrt(ms + EPS)
    o_ref[:, :half] = z_lo * scale * nw_ref[:, :half]
    o_ref[:, half:] = z_hi * scale * nw_ref[:, half:]


def _combine(x2, ys, dest_tiles, rw, eye, norm_w):
    n, d = x2.shape
    nt, n_idx = dest_tiles.shape
    tc = n_idx // 2
    w = ys.shape[-1]
    return pl.pallas_call(
        _combine_kernel,
        grid=(nt,),
        in_specs=[pl.BlockSpec(memory_space=pl.ANY)] * 2 + [
            pl.BlockSpec((tc, d), lambda i: (i, 0)),
            pl.BlockSpec((8, tc), lambda i: (0, i)),
            pl.BlockSpec(eye.shape, lambda i: (0, 0)),
            pl.BlockSpec((1, d), lambda i: (0, 0)),
        ],
        out_specs=pl.BlockSpec((tc, d), lambda i: (i, 0)),
        out_shape=jax.ShapeDtypeStruct((n, d), F32),
        scratch_shapes=[
            pltpu.SMEM((3 * n_idx,), jnp.int32),
            pltpu.SemaphoreType.DMA((3,)),
            pltpu.VMEM((2, n_idx, w), jnp.uint32),
            pltpu.SemaphoreType.DMA((2,)),
        ],
        compiler_params=pltpu.CompilerParams(
            dimension_semantics=("arbitrary",), vmem_limit_bytes=VMEM_LIMIT),
        name="combine",
    )(dest_tiles, ys, x2, rw, eye, norm_w)


def _layer(xf, mem, batch, seq, norm_mix_w, w_in, ret_norm_w, ml_conv_w, ml_conv_b, ml_gate_b, ml_norm_w,
           w_out, norm_xa_w, norm_mem_w, xa_wq, xa_wkv, xa_wo, norm_moe_w, moe_w_group, moe_b_group,
           moe_w_router, moe_b_router, moe_w_gate, moe_w_up, moe_w_down, final_norm_w):
    n, d = xf.shape
    def halves_first(w):
        return w.reshape(d, RET_HEADS, 2, RET_DK // 2).transpose(0, 2, 1, 3).reshape(d, RET_QK)

    w_main = jnp.concatenate([halves_first(w_in[:, OFF_RQ:OFF_RQ + RET_QK]),
                              halves_first(w_in[:, OFF_RK:OFF_RK + RET_QK]),
                              w_in[:, OFF_RV:MAIN_WIDTH]], axis=1).astype(BF16)
    w_if = w_in[:, MAIN_WIDTH:].astype(BF16)
    tabs = {k_: jnp.asarray(v_) for k_, v_ in _mixer_tables(seq).items()}
    proj, g, gt = _inproj(xf, norm_mix_w.reshape(1, d), w_main, w_if, w_if.T, tabs["cos"], tabs["sin"], ml_conv_w,
                          ml_conv_b.reshape(1, 2 * ML_QK), seq)
    mixed = _mixer(proj, g, gt, tabs, ret_norm_w.reshape(1, RET_V), ml_norm_w.reshape(1, ML_V), ml_gate_b,
                   batch, seq)

    kmem, vmem = _memkv(mem, norm_mem_w.reshape(1, d), xa_wkv.astype(BF16))

    w_route_t = jnp.concatenate(
        [moe_w_router.T, moe_w_group.T, jnp.zeros((ROUTE_ROWS - N_EXPERTS - N_GROUPS, d), F32)], axis=0)
    b_route = jnp.concatenate(
        [moe_b_router, moe_b_group, jnp.zeros((ROUTE_ROWS - N_EXPERTS - N_GROUPS,), F32)]).reshape(ROUTE_ROWS, 1)
    tm = TOKEN_TILE
    sut = jnp.asarray(np.triu(np.ones((tm, tm), np.float32), 1), dtype=BF16)
    x2, h3, ri, rw, cnt = _attn_route(xf, mixed, kmem, vmem, w_out.astype(BF16), norm_xa_w.reshape(1, d),
                                      xa_wq.astype(BF16), xa_wo.astype(BF16), norm_moe_w.reshape(1, d),
                                      w_route_t, b_route, sut, batch, seq)

    bm = MOE_ROWS
    counts = cnt[:, 0].astype(jnp.int32)
    padded = (counts + bm - 1) // bm * bm
    pends = jnp.cumsum(padded)
    pstarts = pends - padded
    expert = ri[0:TOP_K]
    onehot = expert[None] == jnp.arange(N_EXPERTS, dtype=jnp.int32)[:, None, None]
    dest = jnp.sum(jnp.where(onehot, pstarts[:, None, None], 0), axis=0) + ri[TOP_K:2 * TOP_K]
    cap = n * TOP_K + N_EXPERTS * bm
    nblk = cap // bm
    blk_start = jnp.arange(nblk, dtype=jnp.int32) * bm
    blk_e = jnp.minimum(jnp.sum(blk_start[:, None] >= pends[None, :], axis=1), N_EXPERTS - 1).astype(jnp.int32)
    nused = (pends[-1] // bm).astype(jnp.int32).reshape(1)
    zpos = jnp.where(padded > counts, pends - bm, -1).astype(jnp.int32)
    zpos = jnp.concatenate([zpos, nused])

    def tiles(rows):
        return dest.reshape(TOP_K, n // rows, rows).transpose(1, 0, 2).reshape(n // rows, TOP_K * rows)

    xs = _dispatch(h3, tiles(DISPATCH_TILE), zpos, cap)
    ys = _experts(xs, blk_e, nused, moe_w_gate, moe_w_up, moe_w_down)
    eye = jnp.asarray(np.eye(COMBINE_EYE, dtype=np.float32), dtype=BF16)
    return _combine(x2, ys, tiles(COMBINE_TILE), rw, eye, final_norm_w.reshape(1, d))


def kernel(x, mem, norm_mix_w, w_in, ret_norm_w, ml_conv_w, ml_conv_b, ml_gate_b, ml_norm_w, w_out, norm_xa_w, norm_mem_w, xa_wq, xa_wkv, xa_wo, norm_moe_w, moe_w_group, moe_b_group, moe_w_router, moe_b_router, moe_w_gate, moe_w_up, moe_w_down, norm_final_w):
    batch, seq, d = x.shape
    depth = w_in.shape[0]
    assert depth == 1, "the final norm is fused into the last layer's combine kernel"
    l = 0
    out = _layer(x.reshape(batch * seq, d), mem, batch, seq, norm_mix_w[l], w_in[l], ret_norm_w[l], ml_conv_w[l],
                 ml_conv_b[l], ml_gate_b[l], ml_norm_w[l], w_out[l], norm_xa_w[l], norm_mem_w[l], xa_wq[l],
                 xa_wkv[l], xa_wo[l], norm_moe_w[l], moe_w_group[l], moe_b_group[l], moe_w_router[l],
                 moe_b_router[l], moe_w_gate[l], moe_w_up[l], moe_w_down[l], norm_final_w)
    return out.reshape(batch, seq, d)
```

```python
import functools

import numpy as np
import jax
import jax.numpy as jnp
from jax import lax
from jax.experimental import pallas as pl
from jax.experimental.pallas import tpu as pltpu

F32 = jnp.float32
BF16 = jnp.bfloat16

CHUNK = 128
RET_HEADS = 4
RET_DK = 64
RET_DV = 128
ML_HEADS = 4
ML_DK = 128
ML_DV = 128
CONV_W = 4
XA_HEADS = 4
N_GROUPS = 4
EXP_PER_GROUP = 8
N_EXPERTS = N_GROUPS * EXP_PER_GROUP
TOP_K = 2
ROPE_BASE = 10000.0
EPS = 1e-6

RET_QK = RET_HEADS * RET_DK
RET_V = RET_HEADS * RET_DV
ML_QK = ML_HEADS * ML_DK
ML_V = ML_HEADS * ML_DV
OFF_RQ = 0
OFF_RK = OFF_RQ + RET_QK
OFF_RV = OFF_RK + RET_QK
OFF_RG = OFF_RV + RET_V
OFF_MQK = OFF_RG + RET_V
OFF_MV = OFF_MQK + 2 * ML_QK
OFF_MO = OFF_MV + ML_V
MAIN_WIDTH = OFF_MO + ML_V
N_GATES = 2 * ML_HEADS

ROUTE_ROWS = 40
TOKEN_TILE = 1024
MOE_ROWS = 512
DISPATCH_TILE = 1024
COMBINE_TILE = 512
CONV_ROWS = 128
COMBINE_EYE = 512
MIXER_BATCHES = 4
ATTN_GROUPS = 2
VMEM_LIMIT = 56 * 1024 * 1024


def _dot(a, b):
    return jnp.dot(a, b, preferred_element_type=F32)


def _dot_nt(a, b):
    return lax.dot_general(a, b, (((1,), (1,)), ((), ())), preferred_element_type=F32)


def _dot_tn(a, b):
    return lax.dot_general(a, b, (((0,), (0,)), ((), ())), preferred_element_type=F32)


def _rms(x, w):
    return x * lax.rsqrt(jnp.mean(x * x, axis=-1, keepdims=True) + EPS) * w


def _sigmoid(x):
    return 1.0 / (1.0 + jnp.exp(-x))


def _silu(x):
    return x * _sigmoid(x)


def _log_sigmoid(x):
    return jnp.minimum(x, 0.0) - jnp.log1p(jnp.exp(-jnp.abs(x)))


def _head_norms(ts, mean_w):
    mu = [_dot(t.astype(BF16), mean_w) for t in ts]
    dl = [t - m for t, m in zip(ts, mu)]
    var = [_dot((d * d).astype(BF16), mean_w) for d in dl]
    return [d * lax.rsqrt(v + EPS) for d, v in zip(dl, var)]


def _pack_rows(lo, hi):
    def bits(t):
        return lax.bitcast_convert_type(t.astype(BF16), jnp.uint16).astype(jnp.uint32)
    return bits(lo) | (bits(hi) << 16)


def _unpack_rows(u):
    lo = lax.bitcast_convert_type(u << 16, F32)
    hi = lax.bitcast_convert_type(u & jnp.uint32(0xFFFF0000), F32)
    return lo, hi


def _inproj_kernel(tiles_per_seq, x_ref, nw_ref, w_ref, wif_ref, wift_ref, cos_ref, sin_ref, convw_ref, convb_ref,
                   proj_ref, g_ref, gt_ref, carry_ref):
    tm = x_ref.shape[0]

    @pl.when(pl.program_id(0) == 0)
    def _():
        carry_ref[...] = jnp.zeros_like(carry_ref)

    h = _rms(x_ref[...], nw_ref[...]).astype(BF16)

    def mm(off, width):
        return _dot(h, w_ref[:, off:off + width])

    def rotary(qk):
        cos = cos_ref[...]
        sin = sin_ref[...]
        half = RET_QK // 2
        for off, scale in ((OFF_RQ, None), (OFF_RK, RET_DK ** -0.5)):
            t1 = qk[:, off - OFF_RQ:off - OFF_RQ + half]
            t2 = qk[:, off - OFF_RQ + half:off - OFF_RQ + 2 * half]
            r1 = t1 * cos - t2 * sin
            r2 = t1 * sin + t2 * cos
            if scale is not None:
                r1, r2 = r1 * scale, r2 * scale
            proj_ref[:, off:off + half] = r1.astype(BF16)
            proj_ref[:, off + half:off + 2 * half] = r2.astype(BF16)

    def conv_silu(part, scale):
        c0 = part * ML_QK
        first = (pl.program_id(0) % tiles_per_seq) == 0
        row8 = lax.broadcasted_iota(jnp.int32, (8, ML_QK), 0)
        prev = jnp.where(first, 0.0, carry_ref[:, c0:c0 + ML_QK])
        for r0 in range(0, tm, CONV_ROWS):
            cur = _dot(h[r0:r0 + CONV_ROWS], w_ref[:, OFF_MQK + c0:OFF_MQK + c0 + ML_QK])
            acc = cur * convw_ref[CONV_W - 1:CONV_W, c0:c0 + ML_QK] + convb_ref[:, c0:c0 + ML_QK]
            for s in range(1, CONV_W):
                rolled = pltpu.roll(cur, s, 0)
                head8 = jnp.where(row8 < s, pltpu.roll(prev, s, 0), rolled[0:8])
                shifted = jnp.concatenate([head8, rolled[8:]], axis=0)
                acc = acc + shifted * convw_ref[CONV_W - 1 - s:CONV_W - s, c0:c0 + ML_QK]
            prev = cur[CONV_ROWS - 8:CONV_ROWS]
            act = _silu(acc) if scale is None else _silu(acc) * scale
            proj_ref[r0:r0 + CONV_ROWS, OFF_MQK + c0:OFF_MQK + c0 + ML_QK] = act.astype(BF16)
        carry_ref[:, c0:c0 + ML_QK] = prev

    def store(off, width, fn=None):
        def ep(t):
            proj_ref[:, off:off + width] = (t if fn is None else fn(t)).astype(BF16)
        return ep

    rotary(mm(OFF_RQ, 2 * RET_QK))
    store(OFF_RV, RET_V)(mm(OFF_RV, RET_V))
    store(OFF_RG, RET_V, _silu)(mm(OFF_RG, RET_V))
    conv_silu(0, None)
    conv_silu(1, ML_DK ** -0.5)
    store(OFF_MV, ML_V)(mm(OFF_MV, ML_V))
    store(OFF_MO, ML_V, _sigmoid)(mm(OFF_MO, ML_V))
    g_ref[...] = _dot(h, wif_ref[...])
    gt_ref[...] = _dot_nt(wift_ref[...], h)


def _inproj(xf, norm_w, w_main, w_if, w_ift, cos, sin, conv_w, conv_b, seq):
    n, d = xf.shape
    tm = TOKEN_TILE
    tiles_per_seq = seq // tm
    return pl.pallas_call(
        functools.partial(_inproj_kernel, tiles_per_seq),
        grid=(n // tm,),
        in_specs=[
            pl.BlockSpec((tm, d), lambda i: (i, 0)),
            pl.BlockSpec((1, d), lambda i: (0, 0)),
            pl.BlockSpec((d, MAIN_WIDTH), lambda i: (0, 0)),
            pl.BlockSpec((d, N_GATES), lambda i: (0, 0)),
            pl.BlockSpec((N_GATES, d), lambda i: (0, 0)),
            pl.BlockSpec((tm, RET_QK // 2), lambda i: (i % tiles_per_seq, 0)),
            pl.BlockSpec((tm, RET_QK // 2), lambda i: (i % tiles_per_seq, 0)),
            pl.BlockSpec((CONV_W, 2 * ML_QK), lambda i: (0, 0)),
            pl.BlockSpec((1, 2 * ML_QK), lambda i: (0, 0)),
        ],
        out_specs=[
            pl.BlockSpec((tm, MAIN_WIDTH), lambda i: (i, 0)),
            pl.BlockSpec((tm, N_GATES), lambda i: (i, 0)),
            pl.BlockSpec((N_GATES, tm), lambda i: (0, i)),
        ],
        out_shape=[
            jax.ShapeDtypeStruct((n, MAIN_WIDTH), BF16),
            jax.ShapeDtypeStruct((n, N_GATES), F32),
            jax.ShapeDtypeStruct((N_GATES, n), F32),
        ],
        scratch_shapes=[pltpu.VMEM((8, 2 * ML_QK), F32)],
        compiler_params=pltpu.CompilerParams(
            dimension_semantics=("arbitrary",), vmem_limit_bytes=VMEM_LIMIT),
        name="inproj",
    )(xf, norm_w, w_main, w_if, w_ift, cos, sin, conv_w, conv_b)


def _mixer_kernel(*refs):
    @pl.when(pl.program_id(1) == 0)
    def _():
        for state_ref in refs[-4:]:
            state_ref[...] = jnp.zeros_like(state_ref)

    for bi in range(refs[0].shape[0]):
        _mixer_one(bi, *refs)


def _mixer_one(bi, proj_ref, g_ref, gt_ref, qdec_ref, kdec_ref, dmat_ref,
               bmask_ref, cdec_ref, hmask_ref, tril_ref, triu_ref, ones_ref, retw_ref, mlw_ref, gbc_ref, gbr_ref,
               out_ref, r_ref, c_ref, n_ref, m_ref):
    L = CHUNK
    proj_ref, g_ref, gt_ref, out_ref = proj_ref.at[bi], g_ref.at[bi], gt_ref.at[bi], out_ref.at[bi]
    r_ref, c_ref, n_ref, m_ref = r_ref.at[bi], c_ref.at[bi], n_ref.at[bi], m_ref.at[bi]
    mean_w = ones_ref[...]

    q = proj_ref[:, OFF_RQ:OFF_RQ + RET_QK].astype(F32)
    k_b = proj_ref[:, OFF_RK:OFF_RK + RET_QK]
    k = k_b.astype(F32)
    v = proj_ref[:, OFF_RV:OFF_RV + RET_V]
    r_prev = r_ref[...]
    cross = _dot((q * qdec_ref[...]).astype(BF16), r_prev.astype(BF16))
    kv = _dot_tn((k * kdec_ref[...]).astype(BF16), v) * bmask_ref[...]
    r_ref[...] = cdec_ref[...] * r_prev + kv
    RH = range(RET_HEADS)
    sc = [_dot_nt((q * hmask_ref[h:h + 1, :]).astype(BF16), k_b) for h in RH]
    sc = [(sc[h] * dmat_ref[h]).astype(BF16) for h in RH]
    tot = [_dot(sc[h], v[:, h * RET_DV:(h + 1) * RET_DV]) + cross[:, h * RET_DV:(h + 1) * RET_DV] for h in RH]
    ret = jnp.concatenate(_head_norms(tot, mean_w), axis=1) * retw_ref[...]
    ret = ret * proj_ref[:, OFF_RG:OFF_RG + RET_V].astype(F32)
    out_ref[:, 0:RET_V] = ret.astype(BF16)

    mq = proj_ref[:, OFF_MQK:OFF_MQK + ML_QK]
    mk = proj_ref[:, OFF_MQK + ML_QK:OFF_MQK + 2 * ML_QK]
    mv = proj_ref[:, OFF_MV:OFF_MV + ML_V]

    gc = g_ref[...] + gbc_ref[...]
    gr = gt_ref[...] + gbr_ref[...]
    lf_c = _log_sigmoid(gc)
    lf_r = _log_sigmoid(gr)
    lf_c_hi = lf_c.astype(BF16)
    lf_r_hi = lf_r.astype(BF16)
    b_c = (_dot(tril_ref[...], lf_c_hi)
           + _dot(tril_ref[...], (lf_c - lf_c_hi.astype(F32)).astype(BF16)))
    b_r = (_dot(lf_r_hi, triu_ref[...])
           + _dot((lf_r - lf_r_hi.astype(F32)).astype(BF16), triu_ref[...]))
    causal = (lax.broadcasted_iota(jnp.int32, (L, L), 0) >= lax.broadcasted_iota(jnp.int32, (L, L), 1))
    MH = range(ML_HEADS)
    bc = [b_c[:, ML_HEADS + h:ML_HEADS + h + 1] for h in MH]
    br = [b_r[ML_HEADS + h:ML_HEADS + h + 1, :] for h in MH]
    igc = [gc[:, h:h + 1] for h in MH]
    igr = [gr[h:h + 1, :] for h in MH]
    btot = [br[h][:, L - 1:L] for h in MH]
    qh_b = [mq[:, h * ML_DK:(h + 1) * ML_DK] for h in MH]
    kh_b = [mk[:, h * ML_DK:(h + 1) * ML_DK] for h in MH]
    vh = [mv[:, h * ML_DV:(h + 1) * ML_DV] for h in MH]
    c_prev = [c_ref[h] for h in MH]
    n_prev = [n_ref[h][0:1, :] for h in MH]
    m_prev = [m_ref[h][0:1, 0:1] for h in MH]
    s_raw = [_dot_nt(qh_b[h], kh_b[h]) for h in MH]
    qc = [_dot(qh_b[h], c_prev[h].astype(BF16)) for h in MH]
    log_d = [jnp.where(causal, bc[h] - br[h] + igr[h], -jnp.inf) for h in MH]
    m_intra = [jnp.max(log_d[h], axis=1, keepdims=True) for h in MH]
    m_loc = [jnp.max(btot[h] - br[h] + igr[h], axis=1, keepdims=True) for h in MH]
    kw = [kh_b[h].astype(F32) * jnp.exp(btot[h] - bc[h] + igc[h] - m_loc[h]) for h in MH]
    kv_loc = [_dot_tn(kw[h].astype(BF16), vh[h]) for h in MH]
    n_loc = [jnp.sum(kw[h], axis=0, keepdims=True) for h in MH]
    m_inter = [bc[h] + m_prev[h] for h in MH]
    m_t = [jnp.maximum(m_intra[h], m_inter[h]) for h in MH]
    s_mat = [s_raw[h] * jnp.exp(log_d[h] - m_t[h]) for h in MH]
    inter = [jnp.exp(m_inter[h] - m_t[h]) for h in MH]
    num = [_dot(s_mat[h].astype(BF16), vh[h]) + inter[h] * qc[h] for h in MH]
    den = [jnp.sum(s_mat[h], axis=1, keepdims=True)
           + inter[h] * jnp.sum(qh_b[h].astype(F32) * n_prev[h], axis=1, keepdims=True) for h in MH]
    hh = [num[h] / jnp.maximum(jnp.abs(den[h]), jnp.exp(-m_t[h])) for h in MH]
    for h in MH:
        m_new = jnp.maximum(btot[h] + m_prev[h], m_loc[h])
        s_old = jnp.exp(btot[h] + m_prev[h] - m_new)
        s_loc = jnp.exp(m_loc[h] - m_new)
        c_ref[h] = s_old * c_prev[h] + s_loc * kv_loc[h]
        n_ref[h] = jnp.broadcast_to(s_old * n_prev[h] + s_loc * n_loc[h], (8, ML_DK))
        m_ref[h] = jnp.broadcast_to(m_new, (8, 128))
    ml = jnp.concatenate(_head_norms(hh, mean_w), axis=1) * mlw_ref[...]
    ml = ml * proj_ref[:, OFF_MO:OFF_MO + ML_V].astype(F32)
    out_ref[:, RET_V:RET_V + ML_V] = ml.astype(BF16)


def _mixer_tables(seq):
    L = CHUNK
    half = RET_DK // 2
    inv = ROPE_BASE ** (-np.arange(half, dtype=np.float64) / half)
    ang = np.arange(seq, dtype=np.float64)[:, None] * inv[None, :].astype(np.float32).astype(np.float64)
    cos = np.tile(np.cos(ang), (1, RET_HEADS)).astype(np.float32)
    sin = np.tile(np.sin(ang), (1, RET_HEADS)).astype(np.float32)
    log_g = np.log1p(-np.exp2(-5.0 - np.arange(RET_HEADS, dtype=np.float64)))
    n = np.arange(L, dtype=np.float64)
    lane_head = (np.arange(RET_QK) % (RET_QK // 2)) // half
    qdec = np.exp((n + 1)[:, None] * log_g[lane_head][None, :]).astype(np.float32)
    kdec = np.exp((L - 1 - n)[:, None] * log_g[lane_head][None, :]).astype(np.float32)
    diff = n[:, None] - n[None, :]
    dmat = np.where(diff >= 0, np.exp(log_g[:, None, None] * np.maximum(diff, 0.0)[None]), 0.0).astype(np.float32)
    col_head = np.arange(RET_V) // RET_DV
    bmask = (lane_head[:, None] == col_head[None, :]).astype(np.float32)
    cdec = np.exp(L * log_g[col_head])[None, :].astype(np.float32)
    hmask = (lane_head[None, :] == np.arange(RET_HEADS)[:, None]).astype(np.float32)
    hmask = np.concatenate([hmask, np.zeros((8 - RET_HEADS, RET_QK), np.float32)], axis=0)
    tril = np.tril(np.ones((L, L), np.float32))
    ones = np.full((RET_DV, RET_DV), 1.0 / RET_DV, np.float32)
    return dict(cos=cos, sin=sin, qdec=qdec, kdec=kdec, dmat=dmat, bmask=bmask, cdec=cdec, hmask=hmask,
                tril=tril, triu=np.ascontiguousarray(tril.T), ones=ones)


def _mixer(proj, g, gt, tabs, ret_norm_w, ml_norm_w, gate_b, batch, seq):
    L = CHUNK
    nc = seq // L
    n = batch * seq
    nb = MIXER_BATCHES if batch % MIXER_BATCHES == 0 else 1
    proj = proj.reshape(batch, seq, MAIN_WIDTH)
    g = g.reshape(batch, seq, N_GATES)
    gt = gt.reshape(N_GATES, batch, seq).transpose(1, 0, 2)
    const2 = lambda b, c: (0, 0)
    const3 = lambda b, c: (0, 0, 0)
    tok = lambda b, c: (b, c, 0)
    in_specs = [
        pl.BlockSpec((nb, L, MAIN_WIDTH), tok),
        pl.BlockSpec((nb, L, N_GATES), tok),
        pl.BlockSpec((nb, N_GATES, L), lambda b, c: (b, 0, c)),
        pl.BlockSpec((L, RET_QK), const2),
        pl.BlockSpec((L, RET_QK), const2),
        pl.BlockSpec((RET_HEADS, L, L), const3),
        pl.BlockSpec((RET_QK, RET_V), const2),
        pl.BlockSpec((1, RET_V), const2),
        pl.BlockSpec((8, RET_QK), const2),
        pl.BlockSpec((L, L), const2),
        pl.BlockSpec((L, L), const2),
        pl.BlockSpec((RET_DV, RET_DV), const2),
        pl.BlockSpec((1, RET_V), const2),
        pl.BlockSpec((1, ML_V), const2),
        pl.BlockSpec((1, N_GATES), const2),
        pl.BlockSpec((N_GATES, 1), const2),
    ]
    return pl.pallas_call(
        _mixer_kernel,
        grid=(batch // nb, nc),
        in_specs=in_specs,
        out_specs=pl.BlockSpec((nb, L, RET_V + ML_V), tok),
        out_shape=jax.ShapeDtypeStruct((batch, seq, RET_V + ML_V), BF16),
        scratch_shapes=[
            pltpu.VMEM((nb, RET_QK, RET_V), F32),
            pltpu.VMEM((nb, ML_HEADS, ML_DK, ML_DV), F32),
            pltpu.VMEM((nb, ML_HEADS, 8, ML_DK), F32),
            pltpu.VMEM((nb, ML_HEADS, 8, 128), F32),
        ],
        compiler_params=pltpu.CompilerParams(
            dimension_semantics=("arbitrary", "arbitrary"), vmem_limit_bytes=VMEM_LIMIT),
        name="mixer",
    )(proj, g, gt, tabs["qdec"], tabs["kdec"], tabs["dmat"], tabs["bmask"],
      tabs["cdec"], tabs["hmask"], tabs["tril"].astype(BF16), tabs["triu"].astype(BF16), tabs["ones"].astype(BF16),
      ret_norm_w, ml_norm_w,
      gate_b.reshape(1, N_GATES), gate_b.reshape(N_GATES, 1)).reshape(n, RET_V + ML_V)


def _memkv_kernel(mem_ref, nw_ref, wkv_ref, k_ref, v_ref):
    d = mem_ref.shape[-1]
    mn = _rms(mem_ref[0], nw_ref[...]).astype(BF16)
    k_ref[0] = _dot(mn, wkv_ref[:, :d]).astype(BF16)
    v_ref[0] = _dot(mn, wkv_ref[:, d:]).astype(BF16)


def _memkv(mem, norm_w, wkv):
    b, m, d = mem.shape
    return pl.pallas_call(
        _memkv_kernel,
        grid=(b,),
        in_specs=[
            pl.BlockSpec((1, m, d), lambda i: (i, 0, 0)),
            pl.BlockSpec((1, d), lambda i: (0, 0)),
            pl.BlockSpec((d, 2 * d), lambda i: (0, 0)),
        ],
        out_specs=[pl.BlockSpec((1, m, d), lambda i: (i, 0, 0))] * 2,
        out_shape=[jax.ShapeDtypeStruct((b, m, d), BF16)] * 2,
        compiler_params=pltpu.CompilerParams(
            dimension_semantics=("arbitrary",), vmem_limit_bytes=VMEM_LIMIT),
        name="memkv",
    )(mem, norm_w, wkv)


def _attn_route_kernel(x_ref, mix_ref, k_ref, v_ref, wout_ref, nxa_ref, wq_ref, wo_ref, nmoe_ref,
                       wr_ref, wrlo_ref, br_ref, sut_ref,
                       x2_ref, h3_ref, ri_ref, rw_ref, cnt_ref, carry_ref):
    tm, d = x_ref.shape
    dh = d // XA_HEADS

    @pl.when((pl.program_id(0) == 0) & (pl.program_id(1) == 0))
    def _():
        carry_ref[...] = jnp.zeros_like(carry_ref)

    groups = [slice(g * (tm // ATTN_GROUPS), (g + 1) * (tm // ATTN_GROUPS)) for g in range(ATTN_GROUPS)]
    x1 = [x_ref[s, :] + _dot(mix_ref[s, :], wout_ref[...]) for s in groups]
    h2 = [_rms(t, nxa_ref[...]).astype(BF16) for t in x1]
    q = [_dot(t, wq_ref[...]).astype(BF16) for t in h2]
    o = []
    for qg in q:
        heads = []
        for h in range(XA_HEADS):
            logits = _dot_nt(qg[:, h * dh:(h + 1) * dh], k_ref[0, :, h * dh:(h + 1) * dh]) * (dh ** -0.5)
            mx = jnp.max(logits, axis=-1, keepdims=True)
            e = jnp.exp(logits - mx)
            p = (e / jnp.sum(e, axis=-1, keepdims=True)).astype(BF16)
            heads.append(_dot(p, v_ref[0, :, h * dh:(h + 1) * dh]).astype(BF16))
        o.append(jnp.concatenate(heads, axis=1))
    x2 = [a + _dot(b, wo_ref[...]) for a, b in zip(x1, o)]
    for s, t in zip(groups, x2):
        x2_ref[s, :] = t
    h3 = [_rms(t, nmoe_ref[...]) for t in x2]
    for s, t in zip(groups, h3):
        h3_ref[s, 0, :] = _pack_rows(t[:, :d // 2], t[:, d // 2:])

    lts = []
    for t in h3:
        t_hi = t.astype(BF16)
        t_lo = (t - t_hi.astype(F32)).astype(BF16)
        lts.append(_dot_nt(wr_ref[...], t_hi) + (_dot_nt(wr_ref[...], t_lo) + _dot_nt(wrlo_ref[...], t_hi)))
    lt = jnp.concatenate(lts, axis=1) + br_ref[...]
    gl = lt[N_EXPERTS:N_EXPERTS + N_GROUPS]
    gmax = jnp.max(gl, axis=0, keepdims=True)
    g_w = 1.0 / jnp.sum(jnp.exp(gl - gmax), axis=0, keepdims=True)
    giota = lax.broadcasted_iota(jnp.int32, gl.shape, 0)
    g_sel = jnp.min(jnp.where(gl == gmax, giota, N_GROUPS), axis=0, keepdims=True)
    el = lt[0:N_EXPERTS]
    eiota = lax.broadcasted_iota(jnp.int32, el.shape, 0)
    in_grp = (eiota // EXP_PER_GROUP) == g_sel
    elm = jnp.where(in_grp, el, -jnp.inf)
    m1 = jnp.max(elm, axis=0, keepdims=True)
    esum = jnp.sum(jnp.where(in_grp, jnp.exp(el - m1), 0.0), axis=0, keepdims=True)
    i1 = jnp.min(jnp.where(elm == m1, eiota, N_EXPERTS), axis=0, keepdims=True)
    elm2 = jnp.where(eiota == i1, -jnp.inf, elm)
    m2 = jnp.max(elm2, axis=0, keepdims=True)
    i2 = jnp.min(jnp.where(elm2 == m2, eiota, N_EXPERTS), axis=0, keepdims=True)
    p1 = 1.0 / esum
    p2 = jnp.exp(m2 - m1) / esum
    psum = p1 + p2
    w1 = g_w * (p1 / psum)
    w2 = g_w * (p2 / psum)

    oh1 = (eiota == i1).astype(F32)
    oh2 = (eiota == i2).astype(F32)
    cnt = oh1 + oh2
    base = carry_ref[:, 0:1] + _dot(cnt.astype(BF16), sut_ref[...])
    r1 = jnp.sum(oh1 * base, axis=0, keepdims=True)
    r2 = jnp.sum(oh2 * base, axis=0, keepdims=True)
    new_carry = carry_ref[...] + jnp.sum(cnt, axis=1, keepdims=True)
    carry_ref[...] = new_carry
    cnt_ref[...] = new_carry

    zi = jnp.zeros((4, tm), jnp.int32)
    ri_ref[...] = jnp.concatenate([i1, i2, r1.astype(jnp.int32), r2.astype(jnp.int32), zi], axis=0)
    rw_ref[...] = jnp.concatenate([w1, w2, jnp.zeros((6, tm), F32)], axis=0)


def _attn_route(xf, mixed, kmem, vmem, w_out, norm_xa_w, wq, wo, norm_moe_w, w_route_t, b_route, sut,
                batch, seq):
    n, d = xf.shape
    w_route_hi = w_route_t.astype(BF16)
    w_route_lo = (w_route_t - w_route_hi.astype(F32)).astype(BF16)
    tm = TOKEN_TILE
    nt = seq // tm
    m = kmem.shape[1]
    tok = lambda b, t: (b * nt + t, 0)
    lane_tok = lambda b, t: (0, b * nt + t)
    const2 = lambda b, t: (0, 0)
    return pl.pallas_call(
        _attn_route_kernel,
        grid=(batch, nt),
        in_specs=[
            pl.BlockSpec((tm, d), tok),
            pl.BlockSpec((tm, d), tok),
            pl.BlockSpec((1, m, d), lambda b, t: (b, 0, 0)),
            pl.BlockSpec((1, m, d), lambda b, t: (b, 0, 0)),
            pl.BlockSpec((d, d), const2),
            pl.BlockSpec((1, d), const2),
            pl.BlockSpec((d, d), const2),
            pl.BlockSpec((d, d), const2),
            pl.BlockSpec((1, d), const2),
            pl.BlockSpec((ROUTE_ROWS, d), const2),
            pl.BlockSpec((ROUTE_ROWS, d), const2),
            pl.BlockSpec((ROUTE_ROWS, 1), const2),
            pl.BlockSpec((tm, tm), const2),
        ],
        out_specs=[
            pl.BlockSpec((tm, d), tok),
            pl.BlockSpec((tm, 1, d // 2), lambda b, t: (b * nt + t, 0, 0)),
            pl.BlockSpec((8, tm), lane_tok),
            pl.BlockSpec((8, tm), lane_tok),
            pl.BlockSpec((N_EXPERTS, 128), const2),
        ],
        out_shape=[
            jax.ShapeDtypeStruct((n, d), F32),
            jax.ShapeDtypeStruct((n, 1, d // 2), jnp.uint32),
            jax.ShapeDtypeStruct((8, n), jnp.int32),
            jax.ShapeDtypeStruct((8, n), F32),
            jax.ShapeDtypeStruct((N_EXPERTS, 128), F32),
        ],
        scratch_shapes=[pltpu.VMEM((N_EXPERTS, 128), F32)],
        compiler_params=pltpu.CompilerParams(
            dimension_semantics=("arbitrary", "arbitrary"), vmem_limit_bytes=VMEM_LIMIT),
        name="attn_route",
    )(xf, mixed, kmem, vmem, w_out, norm_xa_w, wq, wo, norm_moe_w, w_route_hi, w_route_lo, b_route, sut)


def _dispatch_kernel(zpos_ref, dest_ref, h_ref, xs_ref, idx_ref, idx_sem, row_sem, zero_ref, zero_sem):
    i = pl.program_id(0)
    nsteps = pl.num_programs(0)
    td = h_ref.shape[0]
    bm = zero_ref.shape[0]
    slot = i % 2

    def idx_copy(step, sl):
        off = pl.multiple_of(sl * (2 * td), 2 * td)
        return pltpu.make_async_copy(dest_ref.at[step], idx_ref.at[pl.ds(off, 2 * td)], idx_sem.at[sl])

    @pl.when(i == 0)
    def _():
        zero_ref[...] = jnp.zeros_like(zero_ref)

        def zero_copy(e):
            return pltpu.make_async_copy(zero_ref, xs_ref.at[pl.ds(pl.multiple_of(zpos_ref[e], bm), bm), 0], zero_sem)

        def tail_copy(b):
            return pltpu.make_async_copy(zero_ref, xs_ref.at[pl.ds(pl.multiple_of(b * bm, bm), bm), 0], zero_sem)

        nused = zpos_ref[N_EXPERTS]
        nblk = xs_ref.shape[0] // bm
        for e in range(N_EXPERTS):
            pl.when(zpos_ref[e] >= 0)(lambda e=e: zero_copy(e).start())
        lax.fori_loop(nused, nblk, lambda b, c: (tail_copy(b).start(), c)[1], 0)
        for e in range(N_EXPERTS):
            pl.when(zpos_ref[e] >= 0)(lambda e=e: zero_copy(e).wait())
        lax.fori_loop(nused, nblk, lambda b, c: (tail_copy(b).wait(), c)[1], 0)
        idx_copy(0, 0).start()

    idx_copy(i, slot).wait()

    @pl.when(i + 1 < nsteps)
    def _():
        idx_copy(i + 1, 1 - slot).start()

    base = slot * (2 * td)

    for t in range(td):
        for k in range(TOP_K):
            pltpu.make_async_copy(h_ref.at[t], xs_ref.at[idx_ref[base + k * td + t]], row_sem).start(priority=t % 2)
    for _ in range(TOP_K):
        pltpu.make_async_copy(xs_ref.at[pl.ds(0, td)], xs_ref.at[pl.ds(0, td)], row_sem).wait()


def _dispatch(h3p, dest_tiles, zpos, cap):
    w = h3p.shape[-1]
    nt, td2 = dest_tiles.shape
    td = td2 // 2
    grid_spec = pltpu.PrefetchScalarGridSpec(
        num_scalar_prefetch=1,
        grid=(nt,),
        in_specs=[
            pl.BlockSpec(memory_space=pl.ANY),
            pl.BlockSpec((td, 1, w), lambda i, zp: (i, 0, 0)),
        ],
        out_specs=pl.BlockSpec(memory_space=pl.ANY),
        scratch_shapes=[
            pltpu.SMEM((2 * td2,), jnp.int32),
            pltpu.SemaphoreType.DMA((2,)),
            pltpu.SemaphoreType.DMA,
            pltpu.VMEM((MOE_ROWS, w), jnp.uint32),
            pltpu.SemaphoreType.DMA,
        ],
    )
    return pl.pallas_call(
        _dispatch_kernel,
        grid_spec=grid_spec,
        out_shape=jax.ShapeDtypeStruct((cap, 1, w), jnp.uint32),
        compiler_params=pltpu.CompilerParams(
            dimension_semantics=("arbitrary",), vmem_limit_bytes=VMEM_LIMIT),
        name="dispatch",
    )(zpos, dest_tiles, h3p)


def _expert_kernel(blk_e_ref, nused_ref, xs_ref, wg_ref, wu_ref, wd_ref, ys_ref, wg_b, wu_b, wd_b,
                   xbuf, ybuf, zbuf, in_sem, out_sem, zero_sem):
    i = pl.program_id(0)
    nsteps = pl.num_programs(0)
    nused = nused_ref[0]
    bm = xbuf.shape[1]
    slot = i % 2
    prev = blk_e_ref[jnp.maximum(i - 1, 0)]
    fresh = (i == 0) | (blk_e_ref[i] != prev)
    half = wd_b.shape[1] // 2

    def rows(ref, step):
        return ref.at[pl.ds(pl.multiple_of(step * bm, bm), bm), 0]

    def in_copy(step, sl):
        return pltpu.make_async_copy(rows(xs_ref, step), xbuf.at[sl], in_sem.at[sl])

    def out_copy(step, sl):
        return pltpu.make_async_copy(ybuf.at[sl], rows(ys_ref, step), out_sem.at[sl])

    def zero_copy(step):
        return pltpu.make_async_copy(zbuf, rows(ys_ref, step), zero_sem)

    @pl.when(i == 0)
    def _():
        zbuf[...] = jnp.zeros_like(zbuf)
        in_copy(0, 0).start()

    @pl.when(i + 1 < nused)
    def _():
        in_copy(i + 1, 1 - slot).start()

    @pl.when(fresh)
    def _():
        wg_b[...] = wg_ref[0].astype(BF16)
        wu_b[...] = wu_ref[0].astype(BF16)
        wd_b[...] = wd_ref[0].astype(BF16)

    @pl.when(i < nused)
    def _():
        in_copy(i, slot).wait()
        pl.when(i >= 2)(lambda: out_copy(i - 2, slot).wait())
        lo, hi = _unpack_rows(xbuf[slot])
        xb = jnp.concatenate([lo.astype(BF16), hi.astype(BF16)], axis=1)
        hid = (_silu(_dot(xb, wg_b[...])) * _dot(xb, wu_b[...])).astype(BF16)
        y = _dot(hid, wd_b[...])
        ybuf[slot] = _pack_rows(y[:, :half], y[:, half:])
        out_copy(i, slot).start()

    pl.when(i >= nused)(lambda: zero_copy(i).start())

    @pl.when(i == nsteps - 1)
    def _():
        pl.when(nused >= 2)(lambda: out_copy(nused - 2, nused % 2).wait())
        pl.when(nused >= 1)(lambda: out_copy(nused - 1, (nused - 1) % 2).wait())
        lax.fori_loop(nused, nsteps, lambda b, c: (zero_copy(b).wait(), c)[1], 0)


def _experts(xs, blk_e, nused, w_gate, w_up, w_down):
    cap, _, w = xs.shape
    _, d, de = w_gate.shape
    bm = MOE_ROWS
    wspec = lambda blk: pl.BlockSpec(blk, lambda i, be, nu: (be[i], 0, 0))
    grid_spec = pltpu.PrefetchScalarGridSpec(
        num_scalar_prefetch=2,
        grid=(cap // bm,),
        in_specs=[pl.BlockSpec(memory_space=pl.ANY), wspec((1, d, de)), wspec((1, d, de)), wspec((1, de, d))],
        out_specs=pl.BlockSpec(memory_space=pl.ANY),
        scratch_shapes=[
            pltpu.VMEM((d, de), BF16),
            pltpu.VMEM((d, de), BF16),
            pltpu.VMEM((de, d), BF16),
            pltpu.VMEM((2, bm, w), jnp.uint32),
            pltpu.VMEM((2, bm, w), jnp.uint32),
            pltpu.VMEM((bm, w), jnp.uint32),
            pltpu.SemaphoreType.DMA((2,)),
            pltpu.SemaphoreType.DMA((2,)),
            pltpu.SemaphoreType.DMA,
        ],
    )
    return pl.pallas_call(
        _expert_kernel,
        grid_spec=grid_spec,
        out_shape=jax.ShapeDtypeStruct((cap, 1, w), jnp.uint32),
        compiler_params=pltpu.CompilerParams(
            dimension_semantics=("arbitrary",), vmem_limit_bytes=VMEM_LIMIT),
        name="experts",
    )(blk_e, nused, xs, w_gate, w_up, w_down)


def _combine_kernel(dest_ref, ys_ref, x2_ref, rw_ref, eye_ref, nw_ref, o_ref, idx_ref, idx_sem, ybuf, ysem):
    i = pl.program_id(0)
    nsteps = pl.num_programs(0)
    tc, d = x2_ref.shape
    half = d // 2
    n_idx = 2 * tc

    def idx_copy(step):
        sl = step % 3
        off = pl.multiple_of(sl * n_idx, n_idx)
        return pltpu.make_async_copy(dest_ref.at[step], idx_ref.at[pl.ds(off, n_idx)], idx_sem.at[sl])

    def gather(step):
        base = (step % 3) * n_idx
        buf = ybuf.at[step % 2]
        sem = ysem.at[step % 2]

        for t in range(n_idx):
            pltpu.make_async_copy(ys_ref.at[idx_ref[base + t]], buf.at[pl.ds(t, 1)], sem).start(priority=t % 2)

    @pl.when(i == 0)
    def _():
        idx_copy(0).start()
        idx_copy(0).wait()
        gather(0)

        @pl.when(nsteps > 1)
        def _():
            idx_copy(1).start()

    @pl.when(i + 1 < nsteps)
    def _():
        idx_copy(i + 1).wait()

        @pl.when(i + 2 < nsteps)
        def _():
            idx_copy(i + 2).start()

        gather(i + 1)

    slot = i % 2
    pltpu.make_async_copy(ybuf.at[slot], ybuf.at[slot], ysem.at[slot]).wait()
    eye = eye_ref[...]
    ew = eye.shape[0]

    def to_columns(r):
        r_a = r.astype(BF16)
        r_b = (r - r_a.astype(F32)).astype(BF16)
        r_c = (r - r_a.astype(F32) - r_b.astype(F32)).astype(BF16)
        return _dot_nt(eye, r_a) + (_dot_nt(eye, r_b) + _dot_nt(eye, r_c))

    wcol = jnp.concatenate([to_columns(rw_ref[:, j:j + ew]) for j in range(0, tc, ew)], axis=0)
    lo1, hi1 = _unpack_rows(ybuf[slot, 0:tc])
    lo2, hi2 = _unpack_rows(ybuf[slot, tc:n_idx])
    w1 = wcol[:, 0:1]
    w2 = wcol[:, 1:2]
    z_lo = x2_ref[:, :half] + (lo1 * w1 + lo2 * w2)
    z_hi = x2_ref[:, half:] + (hi1 * w1 + hi2 * w2)
    ms = (jnp.sum(z_lo * z_lo, axis=-1, keepdims=True) + jnp.sum(z_hi * z_hi, axis=-1, keepdims=True)) / d
    scale = lax.rsqrt(ms + EPS)
    o_ref[:, :half] = z_lo * scale * nw_ref[:, :half]
    o_ref[:, half:] = z_hi * scale * nw_ref[:, half:]


def _combine(x2, ys, dest_tiles, rw, eye, norm_w):
    n, d = x2.shape
    nt, n_idx = dest_tiles.shape
    tc = n_idx // 2
    w = ys.shape[-1]
    return pl.pallas_call(
        _combine_kernel,
        grid=(nt,),
        in_specs=[pl.BlockSpec(memory_space=pl.ANY)] * 2 + [
            pl.BlockSpec((tc, d), lambda i: (i, 0)),
            pl.BlockSpec((8, tc), lambda i: (0, i)),
            pl.BlockSpec(eye.shape, lambda i: (0, 0)),
            pl.BlockSpec((1, d), lambda i: (0, 0)),
        ],
        out_specs=pl.BlockSpec((tc, d), lambda i: (i, 0)),
        out_shape=jax.ShapeDtypeStruct((n, d), F32),
        scratch_shapes=[
            pltpu.SMEM((3 * n_idx,), jnp.int32),
            pltpu.SemaphoreType.DMA((3,)),
            pltpu.VMEM((2, n_idx, w), jnp.uint32),
            pltpu.SemaphoreType.DMA((2,)),
        ],
        compiler_params=pltpu.CompilerParams(
            dimension_semantics=("arbitrary",), vmem_limit_bytes=VMEM_LIMIT),
        name="combine",
    )(dest_tiles, ys, x2, rw, eye, norm_w)


def _layer(xf, mem, batch, seq, norm_mix_w, w_in, ret_norm_w, ml_conv_w, ml_conv_b, ml_gate_b, ml_norm_w,
           w_out, norm_xa_w, norm_mem_w, xa_wq, xa_wkv, xa_wo, norm_moe_w, moe_w_group, moe_b_group,
           moe_w_router, moe_b_router, moe_w_gate, moe_w_up, moe_w_down, final_norm_w):
    n, d = xf.shape
    def halves_first(w):
        return w.reshape(d, RET_HEADS, 2, RET_DK // 2).transpose(0, 2, 1, 3).reshape(d, RET_QK)

    w_main = jnp.concatenate([halves_first(w_in[:, OFF_RQ:OFF_RQ + RET_QK]),
                              halves_first(w_in[:, OFF_RK:OFF_RK + RET_QK]),
                              w_in[:, OFF_RV:MAIN_WIDTH]], axis=1).astype(BF16)
    w_if = w_in[:, MAIN_WIDTH:].astype(BF16)
    tabs = {k_: jnp.asarray(v_) for k_, v_ in _mixer_tables(seq).items()}
    proj, g, gt = _inproj(xf, norm_mix_w.reshape(1, d), w_main, w_if, w_if.T, tabs["cos"], tabs["sin"], ml_conv_w,
                          ml_conv_b.reshape(1, 2 * ML_QK), seq)
    mixed = _mixer(proj, g, gt, tabs, ret_norm_w.reshape(1, RET_V), ml_norm_w.reshape(1, ML_V), ml_gate_b,
                   batch, seq)

    kmem, vmem = _memkv(mem, norm_mem_w.reshape(1, d), xa_wkv.astype(BF16))

    w_route_t = jnp.concatenate(
        [moe_w_router.T, moe_w_group.T, jnp.zeros((ROUTE_ROWS - N_EXPERTS - N_GROUPS, d), F32)], axis=0)
    b_route = jnp.concatenate(
        [moe_b_router, moe_b_group, jnp.zeros((ROUTE_ROWS - N_EXPERTS - N_GROUPS,), F32)]).reshape(ROUTE_ROWS, 1)
    tm = TOKEN_TILE
    sut = jnp.asarray(np.triu(np.ones((tm, tm), np.float32), 1), dtype=BF16)
    x2, h3, ri, rw, cnt = _attn_route(xf, mixed, kmem, vmem, w_out.astype(BF16), norm_xa_w.reshape(1, d),
                                      xa_wq.astype(BF16), xa_wo.astype(BF16), norm_moe_w.reshape(1, d),
                                      w_route_t, b_route, sut, batch, seq)

    bm = MOE_ROWS
    counts = cnt[:, 0].astype(jnp.int32)
    padded = (counts + bm - 1) // bm * bm
    pends = jnp.cumsum(padded)
    pstarts = pends - padded
    expert = ri[0:TOP_K]
    onehot = expert[None] == jnp.arange(N_EXPERTS, dtype=jnp.int32)[:, None, None]
    dest = jnp.sum(jnp.where(onehot, pstarts[:, None, None], 0), axis=0) + ri[TOP_K:2 * TOP_K]
    cap = n * TOP_K + N_EXPERTS * bm
    nblk = cap // bm
    blk_start = jnp.arange(nblk, dtype=jnp.int32) * bm
    blk_e = jnp.minimum(jnp.sum(blk_start[:, None] >= pends[None, :], axis=1), N_EXPERTS - 1).astype(jnp.int32)
    nused = (pends[-1] // bm).astype(jnp.int32).reshape(1)
    zpos = jnp.where(padded > counts, pends - bm, -1).astype(jnp.int32)
    zpos = jnp.concatenate([zpos, nused])

    def tiles(rows):
        return dest.reshape(TOP_K, n // rows, rows).transpose(1, 0, 2).reshape(n // rows, TOP_K * rows)

    xs = _dispatch(h3, tiles(DISPATCH_TILE), zpos, cap)
    ys = _experts(xs, blk_e, nused, moe_w_gate, moe_w_up, moe_w_down)
    eye = jnp.asarray(np.eye(COMBINE_EYE, dtype=np.float32), dtype=BF16)
    return _combine(x2, ys, tiles(COMBINE_TILE), rw, eye, final_norm_w.reshape(1, d))


def kernel(x, mem, norm_mix_w, w_in, ret_norm_w, ml_conv_w, ml_conv_b, ml_gate_b, ml_norm_w, w_out, norm_xa_w, norm_mem_w, xa_wq, xa_wkv, xa_wo, norm_moe_w, moe_w_group, moe_b_group, moe_w_router, moe_b_router, moe_w_gate, moe_w_up, moe_w_down, norm_final_w):
    batch, seq, d = x.shape
    depth = w_in.shape[0]
    assert depth == 1, "the final norm is fused into the last layer's combine kernel"
    l = 0
    out = _layer(x.reshape(batch * seq, d), mem, batch, seq, norm_mix_w[l], w_in[l], ret_norm_w[l], ml_conv_w[l],
                 ml_conv_b[l], ml_gate_b[l], ml_norm_w[l], w_out[l], norm_xa_w[l], norm_mem_w[l], xa_wq[l],
                 xa_wkv[l], xa_wo[l], norm_moe_w[l], moe_w_group[l], moe_b_group[l], moe_w_router[l],
                 moe_b_router[l], moe_w_gate[l], moe_w_up[l], moe_w_down[l], norm_final_w)
    return out.reshape(batch, seq, d)
```

```python
import functools

import numpy as np
import jax
import jax.numpy as jnp
from jax import lax
from jax.experimental import pallas as pl
from jax.experimental.pallas import tpu as pltpu

F32 = jnp.float32
BF16 = jnp.bfloat16

CHUNK = 128
RET_HEADS = 4
RET_DK = 64
RET_DV = 128
ML_HEADS = 4
ML_DK = 128
ML_DV = 128
CONV_W = 4
XA_HEADS = 4
N_GROUPS = 4
EXP_PER_GROUP = 8
N_EXPERTS = N_GROUPS * EXP_PER_GROUP
TOP_K = 2
ROPE_BASE = 10000.0
EPS = 1e-6

RET_QK = RET_HEADS * RET_DK
RET_V = RET_HEADS * RET_DV
ML_QK = ML_HEADS * ML_DK
ML_V = ML_HEADS * ML_DV
OFF_RQ = 0
OFF_RK = OFF_RQ + RET_QK
OFF_RV = OFF_RK + RET_QK
OFF_RG = OFF_RV + RET_V
OFF_MQK = OFF_RG + RET_V
OFF_MV = OFF_MQK + 2 * ML_QK
OFF_MO = OFF_MV + ML_V
MAIN_WIDTH = OFF_MO + ML_V
N_GATES = 2 * ML_HEADS

ROUTE_ROWS = 40
TOKEN_TILE = 1024
MOE_ROWS = 512
DISPATCH_TILE = 1024
COMBINE_TILE = 512
CONV_ROWS = 128
COMBINE_EYE = 512
MIXER_BATCHES = 8
ATTN_GROUPS = 2
VMEM_LIMIT = 56 * 1024 * 1024


def _dot(a, b):
    return jnp.dot(a, b, preferred_element_type=F32)


def _dot_nt(a, b):
    return lax.dot_general(a, b, (((1,), (1,)), ((), ())), preferred_element_type=F32)


def _dot_tn(a, b):
    return lax.dot_general(a, b, (((0,), (0,)), ((), ())), preferred_element_type=F32)


def _rms(x, w):
    return x * lax.rsqrt(jnp.mean(x * x, axis=-1, keepdims=True) + EPS) * w


def _sigmoid(x):
    return 1.0 / (1.0 + jnp.exp(-x))


def _silu(x):
    return x * _sigmoid(x)


def _log_sigmoid(x):
    return jnp.minimum(x, 0.0) - jnp.log1p(jnp.exp(-jnp.abs(x)))


def _head_norms(ts, mean_w):
    mu = [_dot(t.astype(BF16), mean_w) for t in ts]
    dl = [t - m for t, m in zip(ts, mu)]
    var = [_dot((d * d).astype(BF16), mean_w) for d in dl]
    return [d * lax.rsqrt(v + EPS) for d, v in zip(dl, var)]


def _pack_rows(lo, hi):
    def bits(t):
        return lax.bitcast_convert_type(t.astype(BF16), jnp.uint16).astype(jnp.uint32)
    return bits(lo) | (bits(hi) << 16)


def _unpack_rows(u):
    lo = lax.bitcast_convert_type(u << 16, F32)
    hi = lax.bitcast_convert_type(u & jnp.uint32(0xFFFF0000), F32)
    return lo, hi


def _inproj_kernel(tiles_per_seq, x_ref, nw_ref, w_ref, wif_ref, wift_ref, cos_ref, sin_ref, convw_ref, convb_ref,
                   proj_ref, g_ref, gt_ref, carry_ref):
    tm = x_ref.shape[0]

    @pl.when(pl.program_id(0) == 0)
    def _():
        carry_ref[...] = jnp.zeros_like(carry_ref)

    h = _rms(x_ref[...], nw_ref[...]).astype(BF16)

    def mm(off, width):
        return _dot(h, w_ref[:, off:off + width])

    def rotary(qk):
        cos = cos_ref[...]
        sin = sin_ref[...]
        half = RET_QK // 2
        for off, scale in ((OFF_RQ, None), (OFF_RK, RET_DK ** -0.5)):
            t1 = qk[:, off - OFF_RQ:off - OFF_RQ + half]
            t2 = qk[:, off - OFF_RQ + half:off - OFF_RQ + 2 * half]
            r1 = t1 * cos - t2 * sin
            r2 = t1 * sin + t2 * cos
            if scale is not None:
                r1, r2 = r1 * scale, r2 * scale
            proj_ref[:, off:off + half] = r1.astype(BF16)
            proj_ref[:, off + half:off + 2 * half] = r2.astype(BF16)

    def conv_silu(part, scale):
        c0 = part * ML_QK
        first = (pl.program_id(0) % tiles_per_seq) == 0
        row8 = lax.broadcasted_iota(jnp.int32, (8, ML_QK), 0)
        prev = jnp.where(first, 0.0, carry_ref[:, c0:c0 + ML_QK])
        for r0 in range(0, tm, CONV_ROWS):
            cur = _dot(h[r0:r0 + CONV_ROWS], w_ref[:, OFF_MQK + c0:OFF_MQK + c0 + ML_QK])
            acc = cur * convw_ref[CONV_W - 1:CONV_W, c0:c0 + ML_QK] + convb_ref[:, c0:c0 + ML_QK]
            for s in range(1, CONV_W):
                rolled = pltpu.roll(cur, s, 0)
                head8 = jnp.where(row8 < s, pltpu.roll(prev, s, 0), rolled[0:8])
                shifted = jnp.concatenate([head8, rolled[8:]], axis=0)
                acc = acc + shifted * convw_ref[CONV_W - 1 - s:CONV_W - s, c0:c0 + ML_QK]
            prev = cur[CONV_ROWS - 8:CONV_ROWS]
            act = _silu(acc) if scale is None else _silu(acc) * scale
            proj_ref[r0:r0 + CONV_ROWS, OFF_MQK + c0:OFF_MQK + c0 + ML_QK] = act.astype(BF16)
        carry_ref[:, c0:c0 + ML_QK] = prev

    def store(off, width, fn=None):
        def ep(t):
            proj_ref[:, off:off + width] = (t if fn is None else fn(t)).astype(BF16)
        return ep

    rotary(mm(OFF_RQ, 2 * RET_QK))
    store(OFF_RV, RET_V)(mm(OFF_RV, RET_V))
    store(OFF_RG, RET_V, _silu)(mm(OFF_RG, RET_V))
    conv_silu(0, None)
    conv_silu(1, ML_DK ** -0.5)
    store(OFF_MV, ML_V)(mm(OFF_MV, ML_V))
    store(OFF_MO, ML_V, _sigmoid)(mm(OFF_MO, ML_V))
    g_ref[...] = _dot(h, wif_ref[...])
    gt_ref[...] = _dot_nt(wift_ref[...], h)


def _inproj(xf, norm_w, w_main, w_if, w_ift, cos, sin, conv_w, conv_b, seq):
    n, d = xf.shape
    tm = TOKEN_TILE
    tiles_per_seq = seq // tm
    return pl.pallas_call(
        functools.partial(_inproj_kernel, tiles_per_seq),
        grid=(n // tm,),
        in_specs=[
            pl.BlockSpec((tm, d), lambda i: (i, 0)),
            pl.BlockSpec((1, d), lambda i: (0, 0)),
            pl.BlockSpec((d, MAIN_WIDTH), lambda i: (0, 0)),
            pl.BlockSpec((d, N_GATES), lambda i: (0, 0)),
            pl.BlockSpec((N_GATES, d), lambda i: (0, 0)),
            pl.BlockSpec((tm, RET_QK // 2), lambda i: (i % tiles_per_seq, 0)),
            pl.BlockSpec((tm, RET_QK // 2), lambda i: (i % tiles_per_seq, 0)),
            pl.BlockSpec((CONV_W, 2 * ML_QK), lambda i: (0, 0)),
            pl.BlockSpec((1, 2 * ML_QK), lambda i: (0, 0)),
        ],
        out_specs=[
            pl.BlockSpec((tm, MAIN_WIDTH), lambda i: (i, 0)),
            pl.BlockSpec((tm, N_GATES), lambda i: (i, 0)),
            pl.BlockSpec((N_GATES, tm), lambda i: (0, i)),
        ],
        out_shape=[
            jax.ShapeDtypeStruct((n, MAIN_WIDTH), BF16),
            jax.ShapeDtypeStruct((n, N_GATES), F32),
            jax.ShapeDtypeStruct((N_GATES, n), F32),
        ],
        scratch_shapes=[pltpu.VMEM((8, 2 * ML_QK), F32)],
        compiler_params=pltpu.CompilerParams(
            dimension_semantics=("arbitrary",), vmem_limit_bytes=VMEM_LIMIT),
        name="inproj",
    )(xf, norm_w, w_main, w_if, w_ift, cos, sin, conv_w, conv_b)


def _mixer_kernel(*refs):
    @pl.when(pl.program_id(1) == 0)
    def _():
        for state_ref in refs[-4:]:
            state_ref[...] = jnp.zeros_like(state_ref)

    for bi in range(refs[0].shape[0]):
        _mixer_one(bi, *refs)


def _mixer_one(bi, proj_ref, g_ref, gt_ref, qdec_ref, kdec_ref, dmat_ref,
               bmask_ref, cdec_ref, hmask_ref, tril_ref, triu_ref, ones_ref, retw_ref, mlw_ref, gbc_ref, gbr_ref,
               out_ref, r_ref, c_ref, n_ref, m_ref):
    L = CHUNK
    proj_ref, g_ref, gt_ref, out_ref = proj_ref.at[bi], g_ref.at[bi], gt_ref.at[bi], out_ref.at[bi]
    r_ref, c_ref, n_ref, m_ref = r_ref.at[bi], c_ref.at[bi], n_ref.at[bi], m_ref.at[bi]
    mean_w = ones_ref[...]

    q = proj_ref[:, OFF_RQ:OFF_RQ + RET_QK].astype(F32)
    k_b = proj_ref[:, OFF_RK:OFF_RK + RET_QK]
    k = k_b.astype(F32)
    v = proj_ref[:, OFF_RV:OFF_RV + RET_V]
    r_prev = r_ref[...]
    cross = _dot((q * qdec_ref[...]).astype(BF16), r_prev.astype(BF16))
    kv = _dot_tn((k * kdec_ref[...]).astype(BF16), v) * bmask_ref[...]
    r_ref[...] = cdec_ref[...] * r_prev + kv
    RH = range(RET_HEADS)
    sc = [_dot_nt((q * hmask_ref[h:h + 1, :]).astype(BF16), k_b) for h in RH]
    sc = [(sc[h] * dmat_ref[h]).astype(BF16) for h in RH]
    tot = [_dot(sc[h], v[:, h * RET_DV:(h + 1) * RET_DV]) + cross[:, h * RET_DV:(h + 1) * RET_DV] for h in RH]
    ret = jnp.concatenate(_head_norms(tot, mean_w), axis=1) * retw_ref[...]
    ret = ret * proj_ref[:, OFF_RG:OFF_RG + RET_V].astype(F32)
    out_ref[:, 0:RET_V] = ret.astype(BF16)

    mq = proj_ref[:, OFF_MQK:OFF_MQK + ML_QK]
    mk = proj_ref[:, OFF_MQK + ML_QK:OFF_MQK + 2 * ML_QK]
    mv = proj_ref[:, OFF_MV:OFF_MV + ML_V]

    gc = g_ref[...] + gbc_ref[...]
    gr = gt_ref[...] + gbr_ref[...]
    lf_c = _log_sigmoid(gc)
    lf_r = _log_sigmoid(gr)
    lf_c_hi = lf_c.astype(BF16)
    lf_r_hi = lf_r.astype(BF16)
    b_c = (_dot(tril_ref[...], lf_c_hi)
           + _dot(tril_ref[...], (lf_c - lf_c_hi.astype(F32)).astype(BF16)))
    b_r = (_dot(lf_r_hi, triu_ref[...])
           + _dot((lf_r - lf_r_hi.astype(F32)).astype(BF16), triu_ref[...]))
    causal = (lax.broadcasted_iota(jnp.int32, (L, L), 0) >= lax.broadcasted_iota(jnp.int32, (L, L), 1))
    MH = range(ML_HEADS)
    bc = [b_c[:, ML_HEADS + h:ML_HEADS + h + 1] for h in MH]
    br = [b_r[ML_HEADS + h:ML_HEADS + h + 1, :] for h in MH]
    igc = [gc[:, h:h + 1] for h in MH]
    igr = [gr[h:h + 1, :] for h in MH]
    btot = [br[h][:, L - 1:L] for h in MH]
    qh_b = [mq[:, h * ML_DK:(h + 1) * ML_DK] for h in MH]
    kh_b = [mk[:, h * ML_DK:(h + 1) * ML_DK] for h in MH]
    vh = [mv[:, h * ML_DV:(h + 1) * ML_DV] for h in MH]
    c_prev = [c_ref[h] for h in MH]
    n_prev = [n_ref[h][0:1, :] for h in MH]
    m_prev = [m_ref[h][0:1, 0:1] for h in MH]
    s_raw = [_dot_nt(qh_b[h], kh_b[h]) for h in MH]
    qc = [_dot(qh_b[h], c_prev[h].astype(BF16)) for h in MH]
    log_d = [jnp.where(causal, bc[h] - br[h] + igr[h], -jnp.inf) for h in MH]
    m_intra = [jnp.max(log_d[h], axis=1, keepdims=True) for h in MH]
    m_loc = [jnp.max(btot[h] - br[h] + igr[h], axis=1, keepdims=True) for h in MH]
    kw = [kh_b[h].astype(F32) * jnp.exp(btot[h] - bc[h] + igc[h] - m_loc[h]) for h in MH]
    kv_loc = [_dot_tn(kw[h].astype(BF16), vh[h]) for h in MH]
    n_loc = [jnp.sum(kw[h], axis=0, keepdims=True) for h in MH]
    m_inter = [bc[h] + m_prev[h] for h in MH]
    m_t = [jnp.maximum(m_intra[h], m_inter[h]) for h in MH]
    s_mat = [s_raw[h] * jnp.exp(log_d[h] - m_t[h]) for h in MH]
    inter = [jnp.exp(m_inter[h] - m_t[h]) for h in MH]
    num = [_dot(s_mat[h].astype(BF16), vh[h]) + inter[h] * qc[h] for h in MH]
    den = [jnp.sum(s_mat[h], axis=1, keepdims=True)
           + inter[h] * jnp.sum(qh_b[h].astype(F32) * n_prev[h], axis=1, keepdims=True) for h in MH]
    hh = [num[h] / jnp.maximum(jnp.abs(den[h]), jnp.exp(-m_t[h])) for h in MH]
    for h in MH:
        m_new = jnp.maximum(btot[h] + m_prev[h], m_loc[h])
        s_old = jnp.exp(btot[h] + m_prev[h] - m_new)
        s_loc = jnp.exp(m_loc[h] - m_new)
        c_ref[h] = s_old * c_prev[h] + s_loc * kv_loc[h]
        n_ref[h] = jnp.broadcast_to(s_old * n_prev[h] + s_loc * n_loc[h], (8, ML_DK))
        m_ref[h] = jnp.broadcast_to(m_new, (8, 128))
    ml = jnp.concatenate(_head_norms(hh, mean_w), axis=1) * mlw_ref[...]
    ml = ml * proj_ref[:, OFF_MO:OFF_MO + ML_V].astype(F32)
    out_ref[:, RET_V:RET_V + ML_V] = ml.astype(BF16)


def _mixer_tables(seq):
    L = CHUNK
    half = RET_DK // 2
    inv = ROPE_BASE ** (-np.arange(half, dtype=np.float64) / half)
    ang = np.arange(seq, dtype=np.float64)[:, None] * inv[None, :].astype(np.float32).astype(np.float64)
    cos = np.tile(np.cos(ang), (1, RET_HEADS)).astype(np.float32)
    sin = np.tile(np.sin(ang), (1, RET_HEADS)).astype(np.float32)
    log_g = np.log1p(-np.exp2(-5.0 - np.arange(RET_HEADS, dtype=np.float64)))
    n = np.arange(L, dtype=np.float64)
    lane_head = (np.arange(RET_QK) % (RET_QK // 2)) // half
    qdec = np.exp((n + 1)[:, None] * log_g[lane_head][None, :]).astype(np.float32)
    kdec = np.exp((L - 1 - n)[:, None] * log_g[lane_head][None, :]).astype(np.float32)
    diff = n[:, None] - n[None, :]
    dmat = np.where(diff >= 0, np.exp(log_g[:, None, None] * np.maximum(diff, 0.0)[None]), 0.0).astype(np.float32)
    col_head = np.arange(RET_V) // RET_DV
    bmask = (lane_head[:, None] == col_head[None, :]).astype(np.float32)
    cdec = np.exp(L * log_g[col_head])[None, :].astype(np.float32)
    hmask = (lane_head[None, :] == np.arange(RET_HEADS)[:, None]).astype(np.float32)
    hmask = np.concatenate([hmask, np.zeros((8 - RET_HEADS, RET_QK), np.float32)], axis=0)
    tril = np.tril(np.ones((L, L), np.float32))
    ones = np.full((RET_DV, RET_DV), 1.0 / RET_DV, np.float32)
    return dict(cos=cos, sin=sin, qdec=qdec, kdec=kdec, dmat=dmat, bmask=bmask, cdec=cdec, hmask=hmask,
                tril=tril, triu=np.ascontiguousarray(tril.T), ones=ones)


def _mixer(proj, g, gt, tabs, ret_norm_w, ml_norm_w, gate_b, batch, seq):
    L = CHUNK
    nc = seq // L
    n = batch * seq
    nb = MIXER_BATCHES if batch % MIXER_BATCHES == 0 else 1
    proj = proj.reshape(batch, seq, MAIN_WIDTH)
    g = g.reshape(batch, seq, N_GATES)
    gt = gt.reshape(N_GATES, batch, seq).transpose(1, 0, 2)
    const2 = lambda b, c: (0, 0)
    const3 = lambda b, c: (0, 0, 0)
    tok = lambda b, c: (b, c, 0)
    in_specs = [
        pl.BlockSpec((nb, L, MAIN_WIDTH), tok),
        pl.BlockSpec((nb, L, N_GATES), tok),
        pl.BlockSpec((nb, N_GATES, L), lambda b, c: (b, 0, c)),
        pl.BlockSpec((L, RET_QK), const2),
        pl.BlockSpec((L, RET_QK), const2),
        pl.BlockSpec((RET_HEADS, L, L), const3),
        pl.BlockSpec((RET_QK, RET_V), const2),
        pl.BlockSpec((1, RET_V), const2),
        pl.BlockSpec((8, RET_QK), const2),
        pl.BlockSpec((L, L), const2),
        pl.BlockSpec((L, L), const2),
        pl.BlockSpec((RET_DV, RET_DV), const2),
        pl.BlockSpec((1, RET_V), const2),
        pl.BlockSpec((1, ML_V), const2),
        pl.BlockSpec((1, N_GATES), const2),
        pl.BlockSpec((N_GATES, 1), const2),
    ]
    return pl.pallas_call(
        _mixer_kernel,
        grid=(batch // nb, nc),
        in_specs=in_specs,
        out_specs=pl.BlockSpec((nb, L, RET_V + ML_V), tok),
        out_shape=jax.ShapeDtypeStruct((batch, seq, RET_V + ML_V), BF16),
        scratch_shapes=[
            pltpu.VMEM((nb, RET_QK, RET_V), F32),
            pltpu.VMEM((nb, ML_HEADS, ML_DK, ML_DV), F32),
            pltpu.VMEM((nb, ML_HEADS, 8, ML_DK), F32),
            pltpu.VMEM((nb, ML_HEADS, 8, 128), F32),
        ],
        compiler_params=pltpu.CompilerParams(
            dimension_semantics=("arbitrary", "arbitrary"), vmem_limit_bytes=VMEM_LIMIT),
        name="mixer",
    )(proj, g, gt, tabs["qdec"], tabs["kdec"], tabs["dmat"], tabs["bmask"],
      tabs["cdec"], tabs["hmask"], tabs["tril"].astype(BF16), tabs["triu"].astype(BF16), tabs["ones"].astype(BF16),
      ret_norm_w, ml_norm_w,
      gate_b.reshape(1, N_GATES), gate_b.reshape(N_GATES, 1)).reshape(n, RET_V + ML_V)


def _memkv_kernel(mem_ref, nw_ref, wkv_ref, k_ref, v_ref):
    d = mem_ref.shape[-1]
    mn = _rms(mem_ref[0], nw_ref[...]).astype(BF16)
    k_ref[0] = _dot(mn, wkv_ref[:, :d]).astype(BF16)
    v_ref[0] = _dot(mn, wkv_ref[:, d:]).astype(BF16)


def _memkv(mem, norm_w, wkv):
    b, m, d = mem.shape
    return pl.pallas_call(
        _memkv_kernel,
        grid=(b,),
        in_specs=[
            pl.BlockSpec((1, m, d), lambda i: (i, 0, 0)),
            pl.BlockSpec((1, d), lambda i: (0, 0)),
            pl.BlockSpec((d, 2 * d), lambda i: (0, 0)),
        ],
        out_specs=[pl.BlockSpec((1, m, d), lambda i: (i, 0, 0))] * 2,
        out_shape=[jax.ShapeDtypeStruct((b, m, d), BF16)] * 2,
        compiler_params=pltpu.CompilerParams(
            dimension_semantics=("arbitrary",), vmem_limit_bytes=VMEM_LIMIT),
        name="memkv",
    )(mem, norm_w, wkv)


def _attn_route_kernel(x_ref, mix_ref, k_ref, v_ref, wout_ref, nxa_ref, wq_ref, wo_ref, nmoe_ref,
                       wr_ref, wrlo_ref, br_ref, sut_ref,
                       x2_ref, h3_ref, ri_ref, rw_ref, cnt_ref, carry_ref):
    tm, d = x_ref.shape
    dh = d // XA_HEADS

    @pl.when((pl.program_id(0) == 0) & (pl.program_id(1) == 0))
    def _():
        carry_ref[...] = jnp.zeros_like(carry_ref)

    groups = [slice(g * (tm // ATTN_GROUPS), (g + 1) * (tm // ATTN_GROUPS)) for g in range(ATTN_GROUPS)]
    x1 = [x_ref[s, :] + _dot(mix_ref[s, :], wout_ref[...]) for s in groups]
    h2 = [_rms(t, nxa_ref[...]).astype(BF16) for t in x1]
    q = [_dot(t, wq_ref[...]).astype(BF16) for t in h2]
    o = []
    for qg in q:
        heads = []
        for h in range(XA_HEADS):
            logits = _dot_nt(qg[:, h * dh:(h + 1) * dh], k_ref[0, :, h * dh:(h + 1) * dh]) * (dh ** -0.5)
            mx = jnp.max(logits, axis=-1, keepdims=True)
            e = jnp.exp(logits - mx)
            p = (e / jnp.sum(e, axis=-1, keepdims=True)).astype(BF16)
            heads.append(_dot(p, v_ref[0, :, h * dh:(h + 1) * dh]).astype(BF16))
        o.append(jnp.concatenate(heads, axis=1))
    x2 = [a + _dot(b, wo_ref[...]) for a, b in zip(x1, o)]
    for s, t in zip(groups, x2):
        x2_ref[s, :] = t
    h3 = [_rms(t, nmoe_ref[...]) for t in x2]
    for s, t in zip(groups, h3):
        h3_ref[s, 0, :] = _pack_rows(t[:, :d // 2], t[:, d // 2:])

    lts = []
    for t in h3:
        t_hi = t.astype(BF16)
        t_lo = (t - t_hi.astype(F32)).astype(BF16)
        lts.append(_dot_nt(wr_ref[...], t_hi) + (_dot_nt(wr_ref[...], t_lo) + _dot_nt(wrlo_ref[...], t_hi)))
    lt = jnp.concatenate(lts, axis=1) + br_ref[...]
    gl = lt[N_EXPERTS:N_EXPERTS + N_GROUPS]
    gmax = jnp.max(gl, axis=0, keepdims=True)
    g_w = 1.0 / jnp.sum(jnp.exp(gl - gmax), axis=0, keepdims=True)
    giota = lax.broadcasted_iota(jnp.int32, gl.shape, 0)
    g_sel = jnp.min(jnp.where(gl == gmax, giota, N_GROUPS), axis=0, keepdims=True)
    el = lt[0:N_EXPERTS]
    eiota = lax.broadcasted_iota(jnp.int32, el.shape, 0)
    in_grp = (eiota // EXP_PER_GROUP) == g_sel
    elm = jnp.where(in_grp, el, -jnp.inf)
    m1 = jnp.max(elm, axis=0, keepdims=True)
    esum = jnp.sum(jnp.where(in_grp, jnp.exp(el - m1), 0.0), axis=0, keepdims=True)
    i1 = jnp.min(jnp.where(elm == m1, eiota, N_EXPERTS), axis=0, keepdims=True)
    elm2 = jnp.where(eiota == i1, -jnp.inf, elm)
    m2 = jnp.max(elm2, axis=0, keepdims=True)
    i2 = jnp.min(jnp.where(elm2 == m2, eiota, N_EXPERTS), axis=0, keepdims=True)
    p1 = 1.0 / esum
    p2 = jnp.exp(m2 - m1) / esum
    psum = p1 + p2
    w1 = g_w * (p1 / psum)
    w2 = g_w * (p2 / psum)

    oh1 = (eiota == i1).astype(F32)
    oh2 = (eiota == i2).astype(F32)
    cnt = oh1 + oh2
    base = carry_ref[:, 0:1] + _dot(cnt.astype(BF16), sut_ref[...])
    r1 = jnp.sum(oh1 * base, axis=0, keepdims=True)
    r2 = jnp.sum(oh2 * base, axis=0, keepdims=True)
    new_carry = carry_ref[...] + jnp.sum(cnt, axis=1, keepdims=True)
    carry_ref[...] = new_carry
    cnt_ref[...] = new_carry

    zi = jnp.zeros((4, tm), jnp.int32)
    ri_ref[...] = jnp.concatenate([i1, i2, r1.astype(jnp.int32), r2.astype(jnp.int32), zi], axis=0)
    rw_ref[...] = jnp.concatenate([w1, w2, jnp.zeros((6, tm), F32)], axis=0)


def _attn_route(xf, mixed, kmem, vmem, w_out, norm_xa_w, wq, wo, norm_moe_w, w_route_t, b_route, sut,
                batch, seq):
    n, d = xf.shape
    w_route_hi = w_route_t.astype(BF16)
    w_route_lo = (w_route_t - w_route_hi.astype(F32)).astype(BF16)
    tm = TOKEN_TILE
    nt = seq // tm
    m = kmem.shape[1]
    tok = lambda b, t: (b * nt + t, 0)
    lane_tok = lambda b, t: (0, b * nt + t)
    const2 = lambda b, t: (0, 0)
    return pl.pallas_call(
        _attn_route_kernel,
        grid=(batch, nt),
        in_specs=[
            pl.BlockSpec((tm, d), tok),
            pl.BlockSpec((tm, d), tok),
            pl.BlockSpec((1, m, d), lambda b, t: (b, 0, 0)),
            pl.BlockSpec((1, m, d), lambda b, t: (b, 0, 0)),
            pl.BlockSpec((d, d), const2),
            pl.BlockSpec((1, d), const2),
            pl.BlockSpec((d, d), const2),
            pl.BlockSpec((d, d), const2),
            pl.BlockSpec((1, d), const2),
            pl.BlockSpec((ROUTE_ROWS, d), const2),
            pl.BlockSpec((ROUTE_ROWS, d), const2),
            pl.BlockSpec((ROUTE_ROWS, 1), const2),
            pl.BlockSpec((tm, tm), const2),
        ],
        out_specs=[
            pl.BlockSpec((tm, d), tok),
            pl.BlockSpec((tm, 1, d // 2), lambda b, t: (b * nt + t, 0, 0)),
            pl.BlockSpec((8, tm), lane_tok),
            pl.BlockSpec((8, tm), lane_tok),
            pl.BlockSpec((N_EXPERTS, 128), const2),
        ],
        out_shape=[
            jax.ShapeDtypeStruct((n, d), F32),
            jax.ShapeDtypeStruct((n, 1, d // 2), jnp.uint32),
            jax.ShapeDtypeStruct((8, n), jnp.int32),
            jax.ShapeDtypeStruct((8, n), F32),
            jax.ShapeDtypeStruct((N_EXPERTS, 128), F32),
        ],
        scratch_shapes=[pltpu.VMEM((N_EXPERTS, 128), F32)],
        compiler_params=pltpu.CompilerParams(
            dimension_semantics=("arbitrary", "arbitrary"), vmem_limit_bytes=VMEM_LIMIT),
        name="attn_route",
    )(xf, mixed, kmem, vmem, w_out, norm_xa_w, wq, wo, norm_moe_w, w_route_hi, w_route_lo, b_route, sut)


def _dispatch_kernel(zpos_ref, dest_ref, h_ref, xs_ref, idx_ref, idx_sem, row_sem, zero_ref, zero_sem):
    i = pl.program_id(0)
    nsteps = pl.num_programs(0)
    td = h_ref.shape[0]
    bm = zero_ref.shape[0]
    slot = i % 2

    def idx_copy(step, sl):
        off = pl.multiple_of(sl * (2 * td), 2 * td)
        return pltpu.make_async_copy(dest_ref.at[step], idx_ref.at[pl.ds(off, 2 * td)], idx_sem.at[sl])

    @pl.when(i == 0)
    def _():
        zero_ref[...] = jnp.zeros_like(zero_ref)

        def zero_copy(e):
            return pltpu.make_async_copy(zero_ref, xs_ref.at[pl.ds(pl.multiple_of(zpos_ref[e], bm), bm), 0], zero_sem)

        def tail_copy(b):
            return pltpu.make_async_copy(zero_ref, xs_ref.at[pl.ds(pl.multiple_of(b * bm, bm), bm), 0], zero_sem)

        nused = zpos_ref[N_EXPERTS]
        nblk = xs_ref.shape[0] // bm
        for e in range(N_EXPERTS):
            pl.when(zpos_ref[e] >= 0)(lambda e=e: zero_copy(e).start())
        lax.fori_loop(nused, nblk, lambda b, c: (tail_copy(b).start(), c)[1], 0)
        for e in range(N_EXPERTS):
            pl.when(zpos_ref[e] >= 0)(lambda e=e: zero_copy(e).wait())
        lax.fori_loop(nused, nblk, lambda b, c: (tail_copy(b).wait(), c)[1], 0)
        idx_copy(0, 0).start()

    idx_copy(i, slot).wait()

    @pl.when(i + 1 < nsteps)
    def _():
        idx_copy(i + 1, 1 - slot).start()

    base = slot * (2 * td)

    for t in range(td):
        for k in range(TOP_K):
            pltpu.make_async_copy(h_ref.at[t], xs_ref.at[idx_ref[base + k * td + t]], row_sem).start(priority=t % 2)
    for _ in range(TOP_K):
        pltpu.make_async_copy(xs_ref.at[pl.ds(0, td)], xs_ref.at[pl.ds(0, td)], row_sem).wait()


def _dispatch(h3p, dest_tiles, zpos, cap):
    w = h3p.shape[-1]
    nt, td2 = dest_tiles.shape
    td = td2 // 2
    grid_spec = pltpu.PrefetchScalarGridSpec(
        num_scalar_prefetch=1,
        grid=(nt,),
        in_specs=[
            pl.BlockSpec(memory_space=pl.ANY),
            pl.BlockSpec((td, 1, w), lambda i, zp: (i, 0, 0)),
        ],
        out_specs=pl.BlockSpec(memory_space=pl.ANY),
        scratch_shapes=[
            pltpu.SMEM((2 * td2,), jnp.int32),
            pltpu.SemaphoreType.DMA((2,)),
            pltpu.SemaphoreType.DMA,
            pltpu.VMEM((MOE_ROWS, w), jnp.uint32),
            pltpu.SemaphoreType.DMA,
        ],
    )
    return pl.pallas_call(
        _dispatch_kernel,
        grid_spec=grid_spec,
        out_shape=jax.ShapeDtypeStruct((cap, 1, w), jnp.uint32),
        compiler_params=pltpu.CompilerParams(
            dimension_semantics=("arbitrary",), vmem_limit_bytes=VMEM_LIMIT),
        name="dispatch",
    )(zpos, dest_tiles, h3p)


def _expert_kernel(blk_e_ref, nused_ref, xs_ref, wg_ref, wu_ref, wd_ref, ys_ref, wg_b, wu_b, wd_b,
                   xbuf, ybuf, zbuf, in_sem, out_sem, zero_sem):
    i = pl.program_id(0)
    nsteps = pl.num_programs(0)
    nused = nused_ref[0]
    bm = xbuf.shape[1]
    slot = i % 2
    prev = blk_e_ref[jnp.maximum(i - 1, 0)]
    fresh = (i == 0) | (blk_e_ref[i] != prev)
    half = wd_b.shape[1] // 2

    def rows(ref, step):
        return ref.at[pl.ds(pl.multiple_of(step * bm, bm), bm), 0]

    def in_copy(step, sl):
        return pltpu.make_async_copy(rows(xs_ref, step), xbuf.at[sl], in_sem.at[sl])

    def out_copy(step, sl):
        return pltpu.make_async_copy(ybuf.at[sl], rows(ys_ref, step), out_sem.at[sl])

    def zero_copy(step):
        return pltpu.make_async_copy(zbuf, rows(ys_ref, step), zero_sem)

    @pl.when(i == 0)
    def _():
        zbuf[...] = jnp.zeros_like(zbuf)
        in_copy(0, 0).start()

    @pl.when(i + 1 < nused)
    def _():
        in_copy(i + 1, 1 - slot).start()

    @pl.when(fresh)
    def _():
        wg_b[...] = wg_ref[0].astype(BF16)
        wu_b[...] = wu_ref[0].astype(BF16)
        wd_b[...] = wd_ref[0].astype(BF16)

    @pl.when(i < nused)
    def _():
        in_copy(i, slot).wait()
        pl.when(i >= 2)(lambda: out_copy(i - 2, slot).wait())
        lo, hi = _unpack_rows(xbuf[slot])
        xb = jnp.concatenate([lo.astype(BF16), hi.astype(BF16)], axis=1)
        hid = (_silu(_dot(xb, wg_b[...])) * _dot(xb, wu_b[...])).astype(BF16)
        y = _dot(hid, wd_b[...])
        ybuf[slot] = _pack_rows(y[:, :half], y[:, half:])
        out_copy(i, slot).start()

    pl.when(i >= nused)(lambda: zero_copy(i).start())

    @pl.when(i == nsteps - 1)
    def _():
        pl.when(nused >= 2)(lambda: out_copy(nused - 2, nused % 2).wait())
        pl.when(nused >= 1)(lambda: out_copy(nused - 1, (nused - 1) % 2).wait())
        lax.fori_loop(nused, nsteps, lambda b, c: (zero_copy(b).wait(), c)[1], 0)


def _experts(xs, blk_e, nused, w_gate, w_up, w_down):
    cap, _, w = xs.shape
    _, d, de = w_gate.shape
    bm = MOE_ROWS
    wspec = lambda blk: pl.BlockSpec(blk, lambda i, be, nu: (be[i], 0, 0))
    grid_spec = pltpu.PrefetchScalarGridSpec(
        num_scalar_prefetch=2,
        grid=(cap // bm,),
        in_specs=[pl.BlockSpec(memory_space=pl.ANY), wspec((1, d, de)), wspec((1, d, de)), wspec((1, de, d))],
        out_specs=pl.BlockSpec(memory_space=pl.ANY),
        scratch_shapes=[
            pltpu.VMEM((d, de), BF16),
            pltpu.VMEM((d, de), BF16),
            pltpu.VMEM((de, d), BF16),
            pltpu.VMEM((2, bm, w), jnp.uint32),
            pltpu.VMEM((2, bm, w), jnp.uint32),
            pltpu.VMEM((bm, w), jnp.uint32),
            pltpu.SemaphoreType.DMA((2,)),
            pltpu.SemaphoreType.DMA((2,)),
            pltpu.SemaphoreType.DMA,
        ],
    )
    return pl.pallas_call(
        _expert_kernel,
        grid_spec=grid_spec,
        out_shape=jax.ShapeDtypeStruct((cap, 1, w), jnp.uint32),
        compiler_params=pltpu.CompilerParams(
            dimension_semantics=("arbitrary",), vmem_limit_bytes=VMEM_LIMIT),
        name="experts",
    )(blk_e, nused, xs, w_gate, w_up, w_down)


def _combine_kernel(dest_ref, ys_ref, x2_ref, rw_ref, eye_ref, nw_ref, o_ref, idx_ref, idx_sem, ybuf, ysem):
    i = pl.program_id(0)
    nsteps = pl.num_programs(0)
    tc, d = x2_ref.shape
    half = d // 2
    n_idx = 2 * tc

    def idx_copy(step):
        sl = step % 3
        off = pl.multiple_of(sl * n_idx, n_idx)
        return pltpu.make_async_copy(dest_ref.at[step], idx_ref.at[pl.ds(off, n_idx)], idx_sem.at[sl])

    def gather(step):
        base = (step % 3) * n_idx
        buf = ybuf.at[step % 2]
        sem = ysem.at[step % 2]

        for t in range(n_idx):
            pltpu.make_async_copy(ys_ref.at[idx_ref[base + t]], buf.at[pl.ds(t, 1)], sem).start(priority=t % 2)

    @pl.when(i == 0)
    def _():
        idx_copy(0).start()
        idx_copy(0).wait()
        gather(0)

        @pl.when(nsteps > 1)
        def _():
            idx_copy(1).start()

    @pl.when(i + 1 < nsteps)
    def _():
        idx_copy(i + 1).wait()

        @pl.when(i + 2 < nsteps)
        def _():
            idx_copy(i + 2).start()

        gather(i + 1)

    slot = i % 2
    pltpu.make_async_copy(ybuf.at[slot], ybuf.at[slot], ysem.at[slot]).wait()
    eye = eye_ref[...]
    ew = eye.shape[0]

    def to_columns(r):
        r_a = r.astype(BF16)
        r_b = (r - r_a.astype(F32)).astype(BF16)
        r_c = (r - r_a.astype(F32) - r_b.astype(F32)).astype(BF16)
        return _dot_nt(eye, r_a) + (_dot_nt(eye, r_b) + _dot_nt(eye, r_c))

    wcol = jnp.concatenate([to_columns(rw_ref[:, j:j + ew]) for j in range(0, tc, ew)], axis=0)
    lo1, hi1 = _unpack_rows(ybuf[slot, 0:tc])
    lo2, hi2 = _unpack_rows(ybuf[slot, tc:n_idx])
    w1 = wcol[:, 0:1]
    w2 = wcol[:, 1:2]
    z_lo = x2_ref[:, :half] + (lo1 * w1 + lo2 * w2)
    z_hi = x2_ref[:, half:] + (hi1 * w1 + hi2 * w2)
    ms = (jnp.sum(z_lo * z_lo, axis=-1, keepdims=True) + jnp.sum(z_hi * z_hi, axis=-1, keepdims=True)) / d
    scale = lax.rsqrt(ms + EPS)
    o_ref[:, :half] = z_lo * scale * nw_ref[:, :half]
    o_ref[:, half:] = z_hi * scale * nw_ref[:, half:]


def _combine(x2, ys, dest_tiles, rw, eye, norm_w):
    n, d = x2.shape
    nt, n_idx = dest_tiles.shape
    tc = n_idx // 2
    w = ys.shape[-1]
    return pl.pallas_call(
        _combine_kernel,
        grid=(nt,),
        in_specs=[pl.BlockSpec(memory_space=pl.ANY)] * 2 + [
            pl.BlockSpec((tc, d), lambda i: (i, 0)),
            pl.BlockSpec((8, tc), lambda i: (0, i)),
            pl.BlockSpec(eye.shape, lambda i: (0, 0)),
            pl.BlockSpec((1, d), lambda i: (0, 0)),
        ],
        out_specs=pl.BlockSpec((tc, d), lambda i: (i, 0)),
        out_shape=jax.ShapeDtypeStruct((n, d), F32),
        scratch_shapes=[
            pltpu.SMEM((3 * n_idx,), jnp.int32),
            pltpu.SemaphoreType.DMA((3,)),
            pltpu.VMEM((2, n_idx, w), jnp.uint32),
            pltpu.SemaphoreType.DMA((2,)),
        ],
        compiler_params=pltpu.CompilerParams(
            dimension_semantics=("arbitrary",), vmem_limit_bytes=VMEM_LIMIT),
        name="combine",
    )(dest_tiles, ys, x2, rw, eye, norm_w)


def _layer(xf, mem, batch, seq, norm_mix_w, w_in, ret_norm_w, ml_conv_w, ml_conv_b, ml_gate_b, ml_norm_w,
           w_out, norm_xa_w, norm_mem_w, xa_wq, xa_wkv, xa_wo, norm_moe_w, moe_w_group, moe_b_group,
           moe_w_router, moe_b_router, moe_w_gate, moe_w_up, moe_w_down, final_norm_w):
    n, d = xf.shape
    def halves_first(w):
        return w.reshape(d, RET_HEADS, 2, RET_DK // 2).transpose(0, 2, 1, 3).reshape(d, RET_QK)

    w_main = jnp.concatenate([halves_first(w_in[:, OFF_RQ:OFF_RQ + RET_QK]),
                              halves_first(w_in[:, OFF_RK:OFF_RK + RET_QK]),
                              w_in[:, OFF_RV:MAIN_WIDTH]], axis=1).astype(BF16)
    w_if = w_in[:, MAIN_WIDTH:].astype(BF16)
    tabs = {k_: jnp.asarray(v_) for k_, v_ in _mixer_tables(seq).items()}
    proj, g, gt = _inproj(xf, norm_mix_w.reshape(1, d), w_main, w_if, w_if.T, tabs["cos"], tabs["sin"], ml_conv_w,
                          ml_conv_b.reshape(1, 2 * ML_QK), seq)
    mixed = _mixer(proj, g, gt, tabs, ret_norm_w.reshape(1, RET_V), ml_norm_w.reshape(1, ML_V), ml_gate_b,
                   batch, seq)

    kmem, vmem = _memkv(mem, norm_mem_w.reshape(1, d), xa_wkv.astype(BF16))

    w_route_t = jnp.concatenate(
        [moe_w_router.T, moe_w_group.T, jnp.zeros((ROUTE_ROWS - N_EXPERTS - N_GROUPS, d), F32)], axis=0)
    b_route = jnp.concatenate(
        [moe_b_router, moe_b_group, jnp.zeros((ROUTE_ROWS - N_EXPERTS - N_GROUPS,), F32)]).reshape(ROUTE_ROWS, 1)
    tm = TOKEN_TILE
    sut = jnp.asarray(np.triu(np.ones((tm, tm), np.float32), 1), dtype=BF16)
    x2, h3, ri, rw, cnt = _attn_route(xf, mixed, kmem, vmem, w_out.astype(BF16), norm_xa_w.reshape(1, d),
                                      xa_wq.astype(BF16), xa_wo.astype(BF16), norm_moe_w.reshape(1, d),
                                      w_route_t, b_route, sut, batch, seq)

    bm = MOE_ROWS
    counts = cnt[:, 0].astype(jnp.int32)
    padded = (counts + bm - 1) // bm * bm
    pends = jnp.cumsum(padded)
    pstarts = pends - padded
    expert = ri[0:TOP_K]
    onehot = expert[None] == jnp.arange(N_EXPERTS, dtype=jnp.int32)[:, None, None]
    dest = jnp.sum(jnp.where(onehot, pstarts[:, None, None], 0), axis=0) + ri[TOP_K:2 * TOP_K]
    cap = n * TOP_K + N_EXPERTS * bm
    nblk = cap // bm
    blk_start = jnp.arange(nblk, dtype=jnp.int32) * bm
    blk_e = jnp.minimum(jnp.sum(blk_start[:, None] >= pends[None, :], axis=1), N_EXPERTS - 1).astype(jnp.int32)
    nused = (pends[-1] // bm).astype(jnp.int32).reshape(1)
    zpos = jnp.where(padded > counts, pends - bm, -1).astype(jnp.int32)
    zpos = jnp.concatenate([zpos, nused])

    def tiles(rows):
        return dest.reshape(TOP_K, n // rows, rows).transpose(1, 0, 2).reshape(n // rows, TOP_K * rows)

    xs = _dispatch(h3, tiles(DISPATCH_TILE), zpos, cap)
    ys = _experts(xs, blk_e, nused, moe_w_gate, moe_w_up, moe_w_down)
    eye = jnp.asarray(np.eye(COMBINE_EYE, dtype=np.float32), dtype=BF16)
    return _combine(x2, ys, tiles(COMBINE_TILE), rw, eye, final_norm_w.reshape(1, d))


def kernel(x, mem, norm_mix_w, w_in, ret_norm_w, ml_conv_w, ml_conv_b, ml_gate_b, ml_norm_w, w_out, norm_xa_w, norm_mem_w, xa_wq, xa_wkv, xa_wo, norm_moe_w, moe_w_group, moe_b_group, moe_w_router, moe_b_router, moe_w_gate, moe_w_up, moe_w_down, norm_final_w):
    batch, seq, d = x.shape
    depth = w_in.shape[0]
    assert depth == 1, "the final norm is fused into the last layer's combine kernel"
    l = 0
    out = _layer(x.reshape(batch * seq, d), mem, batch, seq, norm_mix_w[l], w_in[l], ret_norm_w[l], ml_conv_w[l],
                 ml_conv_b[l], ml_gate_b[l], ml_norm_w[l], w_out[l], norm_xa_w[l], norm_mem_w[l], xa_wq[l],
                 xa_wkv[l], xa_wo[l], norm_moe_w[l], moe_w_group[l], moe_b_group[l], moe_w_router[l],
                 moe_b_router[l], moe_w_gate[l], moe_w_up[l], moe_w_down[l], norm_final_w)
    return out.reshape(batch, seq, d)
```

```python
import functools

import numpy as np
import jax
import jax.numpy as jnp
from jax import lax
from jax.experimental import pallas as pl
from jax.experimental.pallas import tpu as pltpu

F32 = jnp.float32
BF16 = jnp.bfloat16

CHUNK = 128
RET_HEADS = 4
RET_DK = 64
RET_DV = 128
ML_HEADS = 4
ML_DK = 128
ML_DV = 128
CONV_W = 4
XA_HEADS = 4
N_GROUPS = 4
EXP_PER_GROUP = 8
N_EXPERTS = N_GROUPS * EXP_PER_GROUP
TOP_K = 2
ROPE_BASE = 10000.0
EPS = 1e-6

RET_QK = RET_HEADS * RET_DK
RET_V = RET_HEADS * RET_DV
ML_QK = ML_HEADS * ML_DK
ML_V = ML_HEADS * ML_DV
OFF_RQ = 0
OFF_RK = OFF_RQ + RET_QK
OFF_RV = OFF_RK + RET_QK
OFF_RG = OFF_RV + RET_V
OFF_MQK = OFF_RG + RET_V
OFF_MV = OFF_MQK + 2 * ML_QK
OFF_MO = OFF_MV + ML_V
MAIN_WIDTH = OFF_MO + ML_V
N_GATES = 2 * ML_HEADS

ROUTE_ROWS = 40
TOKEN_TILE = 1024
MOE_ROWS = 512
DISPATCH_TILE = 1024
COMBINE_TILE = 512
CONV_ROWS = 128
COMBINE_EYE = 512
MIXER_BATCHES = 8
ATTN_GROUPS = 2
VMEM_LIMIT = 56 * 1024 * 1024


def _dot(a, b):
    return jnp.dot(a, b, preferred_element_type=F32)


def _dot_nt(a, b):
    return lax.dot_general(a, b, (((1,), (1,)), ((), ())), preferred_element_type=F32)


def _dot_tn(a, b):
    return lax.dot_general(a, b, (((0,), (0,)), ((), ())), preferred_element_type=F32)


def _rms(x, w):
    return x * lax.rsqrt(jnp.mean(x * x, axis=-1, keepdims=True) + EPS) * w


def _sigmoid(x):
    return 1.0 / (1.0 + jnp.exp(-x))


def _silu(x):
    return x * _sigmoid(x)


def _log_sigmoid(x):
    return jnp.minimum(x, 0.0) - jnp.log1p(jnp.exp(-jnp.abs(x)))


def _head_norms(ts, mean_w):
    mu = [_dot(t.astype(BF16), mean_w) for t in ts]
    dl = [t - m for t, m in zip(ts, mu)]
    var = [_dot((d * d).astype(BF16), mean_w) for d in dl]
    return [d * lax.rsqrt(v + EPS) for d, v in zip(dl, var)]


def _pack_rows(lo, hi):
    def bits(t):
        return lax.bitcast_convert_type(t.astype(BF16), jnp.uint16).astype(jnp.uint32)
    return bits(lo) | (bits(hi) << 16)


def _unpack_rows(u):
    lo = lax.bitcast_convert_type(u << 16, F32)
    hi = lax.bitcast_convert_type(u & jnp.uint32(0xFFFF0000), F32)
    return lo, hi


def _inproj_kernel(tiles_per_seq, x_ref, nw_ref, w_ref, wif_ref, wift_ref, cos_ref, sin_ref, convw_ref, convb_ref,
                   proj_ref, g_ref, gt_ref, carry_ref):
    tm = x_ref.shape[0]

    @pl.when(pl.program_id(0) == 0)
    def _():
        carry_ref[...] = jnp.zeros_like(carry_ref)

    h = _rms(x_ref[...], nw_ref[...]).astype(BF16)

    def mm(off, width):
        return _dot(h, w_ref[:, off:off + width])

    def rotary(qk):
        cos = cos_ref[...]
        sin = sin_ref[...]
        half = RET_QK // 2
        for off, scale in ((OFF_RQ, None), (OFF_RK, RET_DK ** -0.5)):
            t1 = qk[:, off - OFF_RQ:off - OFF_RQ + half]
            t2 = qk[:, off - OFF_RQ + half:off - OFF_RQ + 2 * half]
            r1 = t1 * cos - t2 * sin
            r2 = t1 * sin + t2 * cos
            if scale is not None:
                r1, r2 = r1 * scale, r2 * scale
            proj_ref[:, off:off + half] = r1.astype(BF16)
            proj_ref[:, off + half:off + 2 * half] = r2.astype(BF16)

    def conv_silu(part, scale):
        c0 = part * ML_QK
        first = (pl.program_id(0) % tiles_per_seq) == 0
        row8 = lax.broadcasted_iota(jnp.int32, (8, ML_QK), 0)
        prev = jnp.where(first, 0.0, carry_ref[:, c0:c0 + ML_QK])
        for r0 in range(0, tm, CONV_ROWS):
            cur = _dot(h[r0:r0 + CONV_ROWS], w_ref[:, OFF_MQK + c0:OFF_MQK + c0 + ML_QK])
            acc = cur * convw_ref[CONV_W - 1:CONV_W, c0:c0 + ML_QK] + convb_ref[:, c0:c0 + ML_QK]
            for s in range(1, CONV_W):
                rolled = pltpu.roll(cur, s, 0)
                head8 = jnp.where(row8 < s, pltpu.roll(prev, s, 0), rolled[0:8])
                shifted = jnp.concatenate([head8, rolled[8:]], axis=0)
                acc = acc + shifted * convw_ref[CONV_W - 1 - s:CONV_W - s, c0:c0 + ML_QK]
            prev = cur[CONV_ROWS - 8:CONV_ROWS]
            act = _silu(acc) if scale is None else _silu(acc) * scale
            proj_ref[r0:r0 + CONV_ROWS, OFF_MQK + c0:OFF_MQK + c0 + ML_QK] = act.astype(BF16)
        carry_ref[:, c0:c0 + ML_QK] = prev

    def store(off, width, fn=None):
        def ep(t):
            proj_ref[:, off:off + width] = (t if fn is None else fn(t)).astype(BF16)
        return ep

    rotary(mm(OFF_RQ, 2 * RET_QK))
    store(OFF_RV, RET_V)(mm(OFF_RV, RET_V))
    store(OFF_RG, RET_V, _silu)(mm(OFF_RG, RET_V))
    conv_silu(0, None)
    conv_silu(1, ML_DK ** -0.5)
    store(OFF_MV, ML_V)(mm(OFF_MV, ML_V))
    store(OFF_MO, ML_V, _sigmoid)(mm(OFF_MO, ML_V))
    g_ref[...] = _dot(h, wif_ref[...])
    gt_ref[...] = _dot_nt(wift_ref[...], h)


def _inproj(xf, norm_w, w_main, w_if, w_ift, cos, sin, conv_w, conv_b, seq):
    n, d = xf.shape
    tm = TOKEN_TILE
    tiles_per_seq = seq // tm
    return pl.pallas_call(
        functools.partial(_inproj_kernel, tiles_per_seq),
        grid=(n // tm,),
        in_specs=[
            pl.BlockSpec((tm, d), lambda i: (i, 0)),
            pl.BlockSpec((1, d), lambda i: (0, 0)),
            pl.BlockSpec((d, MAIN_WIDTH), lambda i: (0, 0)),
            pl.BlockSpec((d, N_GATES), lambda i: (0, 0)),
            pl.BlockSpec((N_GATES, d), lambda i: (0, 0)),
            pl.BlockSpec((tm, RET_QK // 2), lambda i: (i % tiles_per_seq, 0)),
            pl.BlockSpec((tm, RET_QK // 2), lambda i: (i % tiles_per_seq, 0)),
            pl.BlockSpec((CONV_W, 2 * ML_QK), lambda i: (0, 0)),
            pl.BlockSpec((1, 2 * ML_QK), lambda i: (0, 0)),
        ],
        out_specs=[
            pl.BlockSpec((tm, MAIN_WIDTH), lambda i: (i, 0)),
            pl.BlockSpec((tm, N_GATES), lambda i: (i, 0)),
            pl.BlockSpec((N_GATES, tm), lambda i: (0, i)),
        ],
        out_shape=[
            jax.ShapeDtypeStruct((n, MAIN_WIDTH), BF16),
            jax.ShapeDtypeStruct((n, N_GATES), F32),
            jax.ShapeDtypeStruct((N_GATES, n), F32),
        ],
        scratch_shapes=[pltpu.VMEM((8, 2 * ML_QK), F32)],
        compiler_params=pltpu.CompilerParams(
            dimension_semantics=("arbitrary",), vmem_limit_bytes=VMEM_LIMIT),
        name="inproj",
    )(xf, norm_w, w_main, w_if, w_ift, cos, sin, conv_w, conv_b)


def _mixer_kernel(*refs):
    @pl.when(pl.program_id(1) == 0)
    def _():
        for state_ref in refs[-4:]:
            state_ref[...] = jnp.zeros_like(state_ref)

    for bi in range(refs[0].shape[0]):
        _mixer_one(bi, *refs)


def _mixer_one(bi, proj_ref, g_ref, gt_ref, qdec_ref, kdec_ref, dmat_ref,
               bmask_ref, cdec_ref, hmask_ref, tril_ref, triu_ref, ones_ref, retw_ref, mlw_ref, gbc_ref, gbr_ref,
               out_ref, r_ref, c_ref, n_ref, m_ref):
    L = CHUNK
    proj_ref, g_ref, gt_ref, out_ref = proj_ref.at[bi], g_ref.at[bi], gt_ref.at[bi], out_ref.at[bi]
    r_ref, c_ref, n_ref, m_ref = r_ref.at[bi], c_ref.at[bi], n_ref.at[bi], m_ref.at[bi]
    mean_w = ones_ref[...]

    q = proj_ref[:, OFF_RQ:OFF_RQ + RET_QK].astype(F32)
    k_b = proj_ref[:, OFF_RK:OFF_RK + RET_QK]
    k = k_b.astype(F32)
    v = proj_ref[:, OFF_RV:OFF_RV + RET_V]
    r_prev = r_ref[...]
    cross = _dot((q * qdec_ref[...]).astype(BF16), r_prev.astype(BF16))
    kv = _dot_tn((k * kdec_ref[...]).astype(BF16), v) * bmask_ref[...]
    r_ref[...] = cdec_ref[...] * r_prev + kv
    RH = range(RET_HEADS)
    sc = [_dot_nt((q * hmask_ref[h:h + 1, :]).astype(BF16), k_b) for h in RH]
    sc = [(sc[h] * dmat_ref[h]).astype(BF16) for h in RH]
    tot = [_dot(sc[h], v[:, h * RET_DV:(h + 1) * RET_DV]) + cross[:, h * RET_DV:(h + 1) * RET_DV] for h in RH]
    ret = jnp.concatenate(_head_norms(tot, mean_w), axis=1) * retw_ref[...]
    ret = ret * proj_ref[:, OFF_RG:OFF_RG + RET_V].astype(F32)
    out_ref[:, 0:RET_V] = ret.astype(BF16)

    mq = proj_ref[:, OFF_MQK:OFF_MQK + ML_QK]
    mk = proj_ref[:, OFF_MQK + ML_QK:OFF_MQK + 2 * ML_QK]
    mv = proj_ref[:, OFF_MV:OFF_MV + ML_V]

    gc = g_ref[...] + gbc_ref[...]
    gr = gt_ref[...] + gbr_ref[...]
    lf_c = _log_sigmoid(gc)
    lf_r = _log_sigmoid(gr)
    lf_c_hi = lf_c.astype(BF16)
    lf_r_hi = lf_r.astype(BF16)
    b_c = (_dot(tril_ref[...], lf_c_hi)
           + _dot(tril_ref[...], (lf_c - lf_c_hi.astype(F32)).astype(BF16)))
    b_r = (_dot(lf_r_hi, triu_ref[...])
           + _dot((lf_r - lf_r_hi.astype(F32)).astype(BF16), triu_ref[...]))
    causal = (lax.broadcasted_iota(jnp.int32, (L, L), 0) >= lax.broadcasted_iota(jnp.int32, (L, L), 1))
    MH = range(ML_HEADS)
    bc = [b_c[:, ML_HEADS + h:ML_HEADS + h + 1] for h in MH]
    br = [b_r[ML_HEADS + h:ML_HEADS + h + 1, :] for h in MH]
    igc = [gc[:, h:h + 1] for h in MH]
    igr = [gr[h:h + 1, :] for h in MH]
    btot = [br[h][:, L - 1:L] for h in MH]
    qh_b = [mq[:, h * ML_DK:(h + 1) * ML_DK] for h in MH]
    kh_b = [mk[:, h * ML_DK:(h + 1) * ML_DK] for h in MH]
    vh = [mv[:, h * ML_DV:(h + 1) * ML_DV] for h in MH]
    c_prev = [c_ref[h] for h in MH]
    n_prev = [n_ref[h][0:1, :] for h in MH]
    m_prev = [m_ref[h][0:1, 0:1] for h in MH]
    s_raw = [_dot_nt(qh_b[h], kh_b[h]) for h in MH]
    qc = [_dot(qh_b[h], c_prev[h].astype(BF16)) for h in MH]
    log_d = [jnp.where(causal, bc[h] - br[h] + igr[h], -jnp.inf) for h in MH]
    m_intra = [jnp.max(log_d[h], axis=1, keepdims=True) for h in MH]
    m_loc = [jnp.max(btot[h] - br[h] + igr[h], axis=1, keepdims=True) for h in MH]
    kw = [kh_b[h].astype(F32) * jnp.exp(btot[h] - bc[h] + igc[h] - m_loc[h]) for h in MH]
    kv_loc = [_dot_tn(kw[h].astype(BF16), vh[h]) for h in MH]
    n_loc = [jnp.sum(kw[h], axis=0, keepdims=True) for h in MH]
    m_inter = [bc[h] + m_prev[h] for h in MH]
    m_t = [jnp.maximum(m_intra[h], m_inter[h]) for h in MH]
    s_mat = [s_raw[h] * jnp.exp(log_d[h] - m_t[h]) for h in MH]
    inter = [jnp.exp(m_inter[h] - m_t[h]) for h in MH]
    num = [_dot(s_mat[h].astype(BF16), vh[h]) + inter[h] * qc[h] for h in MH]
    den = [jnp.sum(s_mat[h], axis=1, keepdims=True)
           + inter[h] * jnp.sum(qh_b[h].astype(F32) * n_prev[h], axis=1, keepdims=True) for h in MH]
    hh = [num[h] / jnp.maximum(jnp.abs(den[h]), jnp.exp(-m_t[h])) for h in MH]
    for h in MH:
        m_new = jnp.maximum(btot[h] + m_prev[h], m_loc[h])
        s_old = jnp.exp(btot[h] + m_prev[h] - m_new)
        s_loc = jnp.exp(m_loc[h] - m_new)
        c_ref[h] = s_old * c_prev[h] + s_loc * kv_loc[h]
        n_ref[h] = jnp.broadcast_to(s_old * n_prev[h] + s_loc * n_loc[h], (8, ML_DK))
        m_ref[h] = jnp.broadcast_to(m_new, (8, 128))
    ml = jnp.concatenate(_head_norms(hh, mean_w), axis=1) * mlw_ref[...]
    ml = ml * proj_ref[:, OFF_MO:OFF_MO + ML_V].astype(F32)
    out_ref[:, RET_V:RET_V + ML_V] = ml.astype(BF16)


def _mixer_tables(seq):
    L = CHUNK
    half = RET_DK // 2
    inv = ROPE_BASE ** (-np.arange(half, dtype=np.float64) / half)
    ang = np.arange(seq, dtype=np.float64)[:, None] * inv[None, :].astype(np.float32).astype(np.float64)
    cos = np.tile(np.cos(ang), (1, RET_HEADS)).astype(np.float32)
    sin = np.tile(np.sin(ang), (1, RET_HEADS)).astype(np.float32)
    log_g = np.log1p(-np.exp2(-5.0 - np.arange(RET_HEADS, dtype=np.float64)))
    n = np.arange(L, dtype=np.float64)
    lane_head = (np.arange(RET_QK) % (RET_QK // 2)) // half
    qdec = np.exp((n + 1)[:, None] * log_g[lane_head][None, :]).astype(np.float32)
    kdec = np.exp((L - 1 - n)[:, None] * log_g[lane_head][None, :]).astype(np.float32)
    diff = n[:, None] - n[None, :]
    dmat = np.where(diff >= 0, np.exp(log_g[:, None, None] * np.maximum(diff, 0.0)[None]), 0.0).astype(np.float32)
    col_head = np.arange(RET_V) // RET_DV
    bmask = (lane_head[:, None] == col_head[None, :]).astype(np.float32)
    cdec = np.exp(L * log_g[col_head])[None, :].astype(np.float32)
    hmask = (lane_head[None, :] == np.arange(RET_HEADS)[:, None]).astype(np.float32)
    hmask = np.concatenate([hmask, np.zeros((8 - RET_HEADS, RET_QK), np.float32)], axis=0)
    tril = np.tril(np.ones((L, L), np.float32))
    ones = np.full((RET_DV, RET_DV), 1.0 / RET_DV, np.float32)
    return dict(cos=cos, sin=sin, qdec=qdec, kdec=kdec, dmat=dmat, bmask=bmask, cdec=cdec, hmask=hmask,
                tril=tril, triu=np.ascontiguousarray(tril.T), ones=ones)


def _mixer(proj, g, gt, tabs, ret_norm_w, ml_norm_w, gate_b, batch, seq):
    L = CHUNK
    nc = seq // L
    n = batch * seq
    nb = MIXER_BATCHES if batch % MIXER_BATCHES == 0 else 1
    proj = proj.reshape(batch, seq, MAIN_WIDTH)
    g = g.reshape(batch, seq, N_GATES)
    gt = gt.reshape(N_GATES, batch, seq).transpose(1, 0, 2)
    const2 = lambda b, c: (0, 0)
    const3 = lambda b, c: (0, 0, 0)
    tok = lambda b, c: (b, c, 0)
    in_specs = [
        pl.BlockSpec((nb, L, MAIN_WIDTH), tok),
        pl.BlockSpec((nb, L, N_GATES), tok),
        pl.BlockSpec((nb, N_GATES, L), lambda b, c: (b, 0, c)),
        pl.BlockSpec((L, RET_QK), const2),
        pl.BlockSpec((L, RET_QK), const2),
        pl.BlockSpec((RET_HEADS, L, L), const3),
        pl.BlockSpec((RET_QK, RET_V), const2),
        pl.BlockSpec((1, RET_V), const2),
        pl.BlockSpec((8, RET_QK), const2),
        pl.BlockSpec((L, L), const2),
        pl.BlockSpec((L, L), const2),
        pl.BlockSpec((RET_DV, RET_DV), const2),
        pl.BlockSpec((1, RET_V), const2),
        pl.BlockSpec((1, ML_V), const2),
        pl.BlockSpec((1, N_GATES), const2),
        pl.BlockSpec((N_GATES, 1), const2),
    ]
    return pl.pallas_call(
        _mixer_kernel,
        grid=(batch // nb, nc),
        in_specs=in_specs,
        out_specs=pl.BlockSpec((nb, L, RET_V + ML_V), tok),
        out_shape=jax.ShapeDtypeStruct((batch, seq, RET_V + ML_V), BF16),
        scratch_shapes=[
            pltpu.VMEM((nb, RET_QK, RET_V), F32),
            pltpu.VMEM((nb, ML_HEADS, ML_DK, ML_DV), F32),
            pltpu.VMEM((nb, ML_HEADS, 8, ML_DK), F32),
            pltpu.VMEM((nb, ML_HEADS, 8, 128), F32),
        ],
        compiler_params=pltpu.CompilerParams(
            dimension_semantics=("arbitrary", "arbitrary"), vmem_limit_bytes=VMEM_LIMIT),
        name="mixer",
    )(proj, g, gt, tabs["qdec"], tabs["kdec"], tabs["dmat"], tabs["bmask"],
      tabs["cdec"], tabs["hmask"], tabs["tril"].astype(BF16), tabs["triu"].astype(BF16), tabs["ones"].astype(BF16),
      ret_norm_w, ml_norm_w,
      gate_b.reshape(1, N_GATES), gate_b.reshape(N_GATES, 1)).reshape(n, RET_V + ML_V)


def _memkv_kernel(mem_ref, nw_ref, wkv_ref, k_ref, v_ref):
    d = mem_ref.shape[-1]
    mn = _rms(mem_ref[0], nw_ref[...]).astype(BF16)
    k_ref[0] = _dot(mn, wkv_ref[:, :d]).astype(BF16)
    v_ref[0] = _dot(mn, wkv_ref[:, d:]).astype(BF16)


def _memkv(mem, norm_w, wkv):
    b, m, d = mem.shape
    return pl.pallas_call(
        _memkv_kernel,
        grid=(b,),
        in_specs=[
            pl.BlockSpec((1, m, d), lambda i: (i, 0, 0)),
            pl.BlockSpec((1, d), lambda i: (0, 0)),
            pl.BlockSpec((d, 2 * d), lambda i: (0, 0)),
        ],
        out_specs=[pl.BlockSpec((1, m, d), lambda i: (i, 0, 0))] * 2,
        out_shape=[jax.ShapeDtypeStruct((b, m, d), BF16)] * 2,
        compiler_params=pltpu.CompilerParams(
            dimension_semantics=("arbitrary",), vmem_limit_bytes=VMEM_LIMIT),
        name="memkv",
    )(mem, norm_w, wkv)


def _attn_route_kernel(x_ref, mix_ref, k_ref, v_ref, wout_ref, nxa_ref, wq_ref, wo_ref, nmoe_ref,
                       wr_ref, wrlo_ref, br_ref, sut_ref,
                       x2_ref, h3_ref, ri_ref, rw_ref, cnt_ref, carry_ref):
    tm, d = x_ref.shape
    dh = d // XA_HEADS

    @pl.when((pl.program_id(0) == 0) & (pl.program_id(1) == 0))
    def _():
        carry_ref[...] = jnp.zeros_like(carry_ref)

    groups = [slice(g * (tm // ATTN_GROUPS), (g + 1) * (tm // ATTN_GROUPS)) for g in range(ATTN_GROUPS)]
    x1 = [x_ref[s, :] + _dot(mix_ref[s, :], wout_ref[...]) for s in groups]
    h2 = [_rms(t, nxa_ref[...]).astype(BF16) for t in x1]
    q = [_dot(t, wq_ref[...]).astype(BF16) for t in h2]
    o = []
    for qg in q:
        heads = []
        for h in range(XA_HEADS):
            logits = _dot_nt(qg[:, h * dh:(h + 1) * dh], k_ref[0, :, h * dh:(h + 1) * dh]) * (dh ** -0.5)
            mx = jnp.max(logits, axis=-1, keepdims=True)
            e = jnp.exp(logits - mx)
            p = (e / jnp.sum(e, axis=-1, keepdims=True)).astype(BF16)
            heads.append(_dot(p, v_ref[0, :, h * dh:(h + 1) * dh]).astype(BF16))
        o.append(jnp.concatenate(heads, axis=1))
    x2 = [a + _dot(b, wo_ref[...]) for a, b in zip(x1, o)]
    for s, t in zip(groups, x2):
        x2_ref[s, :] = t
    h3 = [_rms(t, nmoe_ref[...]) for t in x2]
    for s, t in zip(groups, h3):
        h3_ref[s, 0, :] = _pack_rows(t[:, :d // 2], t[:, d // 2:])

    lts = []
    for t in h3:
        t_hi = t.astype(BF16)
        t_lo = (t - t_hi.astype(F32)).astype(BF16)
        lts.append(_dot_nt(wr_ref[...], t_hi) + (_dot_nt(wr_ref[...], t_lo) + _dot_nt(wrlo_ref[...], t_hi)))
    lt = jnp.concatenate(lts, axis=1) + br_ref[...]
    gl = lt[N_EXPERTS:N_EXPERTS + N_GROUPS]
    gmax = jnp.max(gl, axis=0, keepdims=True)
    g_w = 1.0 / jnp.sum(jnp.exp(gl - gmax), axis=0, keepdims=True)
    giota = lax.broadcasted_iota(jnp.int32, gl.shape, 0)
    g_sel = jnp.min(jnp.where(gl == gmax, giota, N_GROUPS), axis=0, keepdims=True)
    el = lt[0:N_EXPERTS]
    eiota = lax.broadcasted_iota(jnp.int32, el.shape, 0)
    in_grp = (eiota // EXP_PER_GROUP) == g_sel
    elm = jnp.where(in_grp, el, -jnp.inf)
    m1 = jnp.max(elm, axis=0, keepdims=True)
    esum = jnp.sum(jnp.where(in_grp, jnp.exp(el - m1), 0.0), axis=0, keepdims=True)
    i1 = jnp.min(jnp.where(elm == m1, eiota, N_EXPERTS), axis=0, keepdims=True)
    elm2 = jnp.where(eiota == i1, -jnp.inf, elm)
    m2 = jnp.max(elm2, axis=0, keepdims=True)
    i2 = jnp.min(jnp.where(elm2 == m2, eiota, N_EXPERTS), axis=0, keepdims=True)
    p1 = 1.0 / esum
    p2 = jnp.exp(m2 - m1) / esum
    psum = p1 + p2
    w1 = g_w * (p1 / psum)
    w2 = g_w * (p2 / psum)

    oh1 = (eiota == i1).astype(F32)
    oh2 = (eiota == i2).astype(F32)
    cnt = oh1 + oh2
    base = carry_ref[:, 0:1] + _dot(cnt.astype(BF16), sut_ref[...])
    r1 = jnp.sum(oh1 * base, axis=0, keepdims=True)
    r2 = jnp.sum(oh2 * base, axis=0, keepdims=True)
    new_carry = carry_ref[...] + jnp.sum(cnt, axis=1, keepdims=True)
    carry_ref[...] = new_carry
    cnt_ref[...] = new_carry

    zi = jnp.zeros((4, tm), jnp.int32)
    ri_ref[...] = jnp.concatenate([i1, i2, r1.astype(jnp.int32), r2.astype(jnp.int32), zi], axis=0)
    rw_ref[...] = jnp.concatenate([w1, w2, jnp.zeros((6, tm), F32)], axis=0)


def _attn_route(xf, mixed, kmem, vmem, w_out, norm_xa_w, wq, wo, norm_moe_w, w_route_t, b_route, sut,
                batch, seq):
    n, d = xf.shape
    w_route_hi = w_route_t.astype(BF16)
    w_route_lo = (w_route_t - w_route_hi.astype(F32)).astype(BF16)
    tm = TOKEN_TILE
    nt = seq // tm
    m = kmem.shape[1]
    tok = lambda b, t: (b * nt + t, 0)
    lane_tok = lambda b, t: (0, b * nt + t)
    const2 = lambda b, t: (0, 0)
    return pl.pallas_call(
        _attn_route_kernel,
        grid=(batch, nt),
        in_specs=[
            pl.BlockSpec((tm, d), tok),
            pl.BlockSpec((tm, d), tok),
            pl.BlockSpec((1, m, d), lambda b, t: (b, 0, 0)),
            pl.BlockSpec((1, m, d), lambda b, t: (b, 0, 0)),
            pl.BlockSpec((d, d), const2),
            pl.BlockSpec((1, d), const2),
            pl.BlockSpec((d, d), const2),
            pl.BlockSpec((d, d), const2),
            pl.BlockSpec((1, d), const2),
            pl.BlockSpec((ROUTE_ROWS, d), const2),
            pl.BlockSpec((ROUTE_ROWS, d), const2),
            pl.BlockSpec((ROUTE_ROWS, 1), const2),
            pl.BlockSpec((tm, tm), const2),
        ],
        out_specs=[
            pl.BlockSpec((tm, d), tok),
            pl.BlockSpec((tm, 1, d // 2), lambda b, t: (b * nt + t, 0, 0)),
            pl.BlockSpec((8, tm), lane_tok),
            pl.BlockSpec((8, tm), lane_tok),
            pl.BlockSpec((N_EXPERTS, 128), const2),
        ],
        out_shape=[
            jax.ShapeDtypeStruct((n, d), F32),
            jax.ShapeDtypeStruct((n, 1, d // 2), jnp.uint32),
            jax.ShapeDtypeStruct((8, n), jnp.int32),
            jax.ShapeDtypeStruct((8, n), F32),
            jax.ShapeDtypeStruct((N_EXPERTS, 128), F32),
        ],
        scratch_shapes=[pltpu.VMEM((N_EXPERTS, 128), F32)],
        compiler_params=pltpu.CompilerParams(
            dimension_semantics=("arbitrary", "arbitrary"), vmem_limit_bytes=VMEM_LIMIT),
        name="attn_route",
    )(xf, mixed, kmem, vmem, w_out, norm_xa_w, wq, wo, norm_moe_w, w_route_hi, w_route_lo, b_route, sut)


def _dispatch_kernel(zpos_ref, dest_ref, h_ref, xs_ref, idx_ref, idx_sem, row_sem, zero_ref, zero_sem):
    i = pl.program_id(0)
    nsteps = pl.num_programs(0)
    td = h_ref.shape[0]
    bm = zero_ref.shape[0]
    slot = i % 2

    def idx_copy(step, sl):
        off = pl.multiple_of(sl * (2 * td), 2 * td)
        return pltpu.make_async_copy(dest_ref.at[step], idx_ref.at[pl.ds(off, 2 * td)], idx_sem.at[sl])

    @pl.when(i == 0)
    def _():
        zero_ref[...] = jnp.zeros_like(zero_ref)

        def zero_copy(e):
            return pltpu.make_async_copy(zero_ref, xs_ref.at[pl.ds(pl.multiple_of(zpos_ref[e], bm), bm), 0], zero_sem)

        def tail_copy(b):
            return pltpu.make_async_copy(zero_ref, xs_ref.at[pl.ds(pl.multiple_of(b * bm, bm), bm), 0], zero_sem)

        nused = zpos_ref[N_EXPERTS]
        nblk = xs_ref.shape[0] // bm
        for e in range(N_EXPERTS):
            pl.when(zpos_ref[e] >= 0)(lambda e=e: zero_copy(e).start())
        lax.fori_loop(nused, nblk, lambda b, c: (tail_copy(b).start(), c)[1], 0)
        for e in range(N_EXPERTS):
            pl.when(zpos_ref[e] >= 0)(lambda e=e: zero_copy(e).wait())
        lax.fori_loop(nused, nblk, lambda b, c: (tail_copy(b).wait(), c)[1], 0)
        idx_copy(0, 0).start()

    idx_copy(i, slot).wait()

    @pl.when(i + 1 < nsteps)
    def _():
        idx_copy(i + 1, 1 - slot).start()

    base = slot * (2 * td)

    for t in range(td):
        for k in range(TOP_K):
            pltpu.make_async_copy(h_ref.at[t], xs_ref.at[idx_ref[base + k * td + t]], row_sem).start(priority=t % 2)
    for _ in range(TOP_K):
        pltpu.make_async_copy(xs_ref.at[pl.ds(0, td)], xs_ref.at[pl.ds(0, td)], row_sem).wait()


def _dispatch(h3p, dest_tiles, zpos, cap):
    w = h3p.shape[-1]
    nt, td2 = dest_tiles.shape
    td = td2 // 2
    grid_spec = pltpu.PrefetchScalarGridSpec(
        num_scalar_prefetch=1,
        grid=(nt,),
        in_specs=[
            pl.BlockSpec(memory_space=pl.ANY),
            pl.BlockSpec((td, 1, w), lambda i, zp: (i, 0, 0)),
        ],
        out_specs=pl.BlockSpec(memory_space=pl.ANY),
        scratch_shapes=[
            pltpu.SMEM((2 * td2,), jnp.int32),
            pltpu.SemaphoreType.DMA((2,)),
            pltpu.SemaphoreType.DMA,
            pltpu.VMEM((MOE_ROWS, w), jnp.uint32),
            pltpu.SemaphoreType.DMA,
        ],
    )
    return pl.pallas_call(
        _dispatch_kernel,
        grid_spec=grid_spec,
        out_shape=jax.ShapeDtypeStruct((cap, 1, w), jnp.uint32),
        compiler_params=pltpu.CompilerParams(
            dimension_semantics=("arbitrary",), vmem_limit_bytes=VMEM_LIMIT),
        name="dispatch",
    )(zpos, dest_tiles, h3p)


def _expert_kernel(blk_e_ref, nused_ref, xs_ref, wg_ref, wu_ref, wd_ref, ys_ref, wg_b, wu_b, wd_b,
                   xbuf, ybuf, zbuf, in_sem, out_sem, zero_sem):
    i = pl.program_id(0)
    nsteps = pl.num_programs(0)
    nused = nused_ref[0]
    bm = xbuf.shape[1]
    slot = i % 2
    prev = blk_e_ref[jnp.maximum(i - 1, 0)]
    fresh = (i == 0) | (blk_e_ref[i] != prev)
    half = wd_b.shape[1] // 2

    def rows(ref, step):
        return ref.at[pl.ds(pl.multiple_of(step * bm, bm), bm), 0]

    def in_copy(step, sl):
        return pltpu.make_async_copy(rows(xs_ref, step), xbuf.at[sl], in_sem.at[sl])

    def out_copy(step, sl):
        return pltpu.make_async_copy(ybuf.at[sl], rows(ys_ref, step), out_sem.at[sl])

    def zero_copy(step):
        return pltpu.make_async_copy(zbuf, rows(ys_ref, step), zero_sem)

    @pl.when(i == 0)
    def _():
        zbuf[...] = jnp.zeros_like(zbuf)
        in_copy(0, 0).start()

    @pl.when(i + 1 < nused)
    def _():
        in_copy(i + 1, 1 - slot).start()

    @pl.when(fresh)
    def _():
        wg_b[...] = wg_ref[0].astype(BF16)
        wu_b[...] = wu_ref[0].astype(BF16)
        wd_b[...] = wd_ref[0].astype(BF16)

    @pl.when(i < nused)
    def _():
        in_copy(i, slot).wait()
        pl.when(i >= 2)(lambda: out_copy(i - 2, slot).wait())
        lo, hi = _unpack_rows(xbuf[slot])
        xb = jnp.concatenate([lo.astype(BF16), hi.astype(BF16)], axis=1)
        hid = (_silu(_dot(xb, wg_b[...])) * _dot(xb, wu_b[...])).astype(BF16)
        y = _dot(hid, wd_b[...])
        ybuf[slot] = _pack_rows(y[:, :half], y[:, half:])
        out_copy(i, slot).start(priority=1)

    pl.when(i >= nused)(lambda: zero_copy(i).start())

    @pl.when(i == nsteps - 1)
    def _():
        pl.when(nused >= 2)(lambda: out_copy(nused - 2, nused % 2).wait())
        pl.when(nused >= 1)(lambda: out_copy(nused - 1, (nused - 1) % 2).wait())
        lax.fori_loop(nused, nsteps, lambda b, c: (zero_copy(b).wait(), c)[1], 0)


def _experts(xs, blk_e, nused, w_gate, w_up, w_down):
    cap, _, w = xs.shape
    _, d, de = w_gate.shape
    bm = MOE_ROWS
    wspec = lambda blk: pl.BlockSpec(blk, lambda i, be, nu: (be[i], 0, 0))
    grid_spec = pltpu.PrefetchScalarGridSpec(
        num_scalar_prefetch=2,
        grid=(cap // bm,),
        in_specs=[pl.BlockSpec(memory_space=pl.ANY), wspec((1, d, de)), wspec((1, d, de)), wspec((1, de, d))],
        out_specs=pl.BlockSpec(memory_space=pl.ANY),
        scratch_shapes=[
            pltpu.VMEM((d, de), BF16),
            pltpu.VMEM((d, de), BF16),
            pltpu.VMEM((de, d), BF16),
            pltpu.VMEM((2, bm, w), jnp.uint32),
            pltpu.VMEM((2, bm, w), jnp.uint32),
            pltpu.VMEM((bm, w), jnp.uint32),
            pltpu.SemaphoreType.DMA((2,)),
            pltpu.SemaphoreType.DMA((2,)),
            pltpu.SemaphoreType.DMA,
        ],
    )
    return pl.pallas_call(
        _expert_kernel,
        grid_spec=grid_spec,
        out_shape=jax.ShapeDtypeStruct((cap, 1, w), jnp.uint32),
        compiler_params=pltpu.CompilerParams(
            dimension_semantics=("arbitrary",), vmem_limit_bytes=VMEM_LIMIT),
        name="experts",
    )(blk_e, nused, xs, w_gate, w_up, w_down)


def _combine_kernel(dest_ref, ys_ref, x2_ref, rw_ref, eye_ref, nw_ref, o_ref, idx_ref, idx_sem, ybuf, ysem):
    i = pl.program_id(0)
    nsteps = pl.num_programs(0)
    tc, d = x2_ref.shape
    half = d // 2
    n_idx = 2 * tc

    def idx_copy(step):
        sl = step % 3
        off = pl.multiple_of(sl * n_idx, n_idx)
        return pltpu.make_async_copy(dest_ref.at[step], idx_ref.at[pl.ds(off, n_idx)], idx_sem.at[sl])

    def gather(step):
        base = (step % 3) * n_idx
        buf = ybuf.at[step % 2]
        sem = ysem.at[step % 2]

        for t in range(n_idx):
            pltpu.make_async_copy(ys_ref.at[idx_ref[base + t]], buf.at[pl.ds(t, 1)], sem).start(priority=t % 2)

    @pl.when(i == 0)
    def _():
        idx_copy(0).start()
        idx_copy(0).wait()
        gather(0)

        @pl.when(nsteps > 1)
        def _():
            idx_copy(1).start()

    @pl.when(i + 1 < nsteps)
    def _():
        idx_copy(i + 1).wait()

        @pl.when(i + 2 < nsteps)
        def _():
            idx_copy(i + 2).start()

        gather(i + 1)

    slot = i % 2
    pltpu.make_async_copy(ybuf.at[slot], ybuf.at[slot], ysem.at[slot]).wait()
    eye = eye_ref[...]
    ew = eye.shape[0]

    def to_columns(r):
        r_a = r.astype(BF16)
        r_b = (r - r_a.astype(F32)).astype(BF16)
        r_c = (r - r_a.astype(F32) - r_b.astype(F32)).astype(BF16)
        return _dot_nt(eye, r_a) + (_dot_nt(eye, r_b) + _dot_nt(eye, r_c))

    wcol = jnp.concatenate([to_columns(rw_ref[:, j:j + ew]) for j in range(0, tc, ew)], axis=0)
    lo1, hi1 = _unpack_rows(ybuf[slot, 0:tc])
    lo2, hi2 = _unpack_rows(ybuf[slot, tc:n_idx])
    w1 = wcol[:, 0:1]
    w2 = wcol[:, 1:2]
    z_lo = x2_ref[:, :half] + (lo1 * w1 + lo2 * w2)
    z_hi = x2_ref[:, half:] + (hi1 * w1 + hi2 * w2)
    ms = (jnp.sum(z_lo * z_lo, axis=-1, keepdims=True) + jnp.sum(z_hi * z_hi, axis=-1, keepdims=True)) / d
    scale = lax.rsqrt(ms + EPS)
    o_ref[:, :half] = z_lo * scale * nw_ref[:, :half]
    o_ref[:, half:] = z_hi * scale * nw_ref[:, half:]


def _combine(x2, ys, dest_tiles, rw, eye, norm_w):
    n, d = x2.shape
    nt, n_idx = dest_tiles.shape
    tc = n_idx // 2
    w = ys.shape[-1]
    return pl.pallas_call(
        _combine_kernel,
        grid=(nt,),
        in_specs=[pl.BlockSpec(memory_space=pl.ANY)] * 2 + [
            pl.BlockSpec((tc, d), lambda i: (i, 0)),
            pl.BlockSpec((8, tc), lambda i: (0, i)),
            pl.BlockSpec(eye.shape, lambda i: (0, 0)),
            pl.BlockSpec((1, d), lambda i: (0, 0)),
        ],
        out_specs=pl.BlockSpec((tc, d), lambda i: (i, 0)),
        out_shape=jax.ShapeDtypeStruct((n, d), F32),
        scratch_shapes=[
            pltpu.SMEM((3 * n_idx,), jnp.int32),
            pltpu.SemaphoreType.DMA((3,)),
            pltpu.VMEM((2, n_idx, w), jnp.uint32),
            pltpu.SemaphoreType.DMA((2,)),
        ],
        compiler_params=pltpu.CompilerParams(
            dimension_semantics=("arbitrary",), vmem_limit_bytes=VMEM_LIMIT),
        name="combine",
    )(dest_tiles, ys, x2, rw, eye, norm_w)


def _layer(xf, mem, batch, seq, norm_mix_w, w_in, ret_norm_w, ml_conv_w, ml_conv_b, ml_gate_b, ml_norm_w,
           w_out, norm_xa_w, norm_mem_w, xa_wq, xa_wkv, xa_wo, norm_moe_w, moe_w_group, moe_b_group,
           moe_w_router, moe_b_router, moe_w_gate, moe_w_up, moe_w_down, final_norm_w):
    n, d = xf.shape
    def halves_first(w):
        return w.reshape(d, RET_HEADS, 2, RET_DK // 2).transpose(0, 2, 1, 3).reshape(d, RET_QK)

    w_main = jnp.concatenate([halves_first(w_in[:, OFF_RQ:OFF_RQ + RET_QK]),
                              halves_first(w_in[:, OFF_RK:OFF_RK + RET_QK]),
                              w_in[:, OFF_RV:MAIN_WIDTH]], axis=1).astype(BF16)
    w_if = w_in[:, MAIN_WIDTH:].astype(BF16)
    tabs = {k_: jnp.asarray(v_) for k_, v_ in _mixer_tables(seq).items()}
    proj, g, gt = _inproj(xf, norm_mix_w.reshape(1, d), w_main, w_if, w_if.T, tabs["cos"], tabs["sin"], ml_conv_w,
                          ml_conv_b.reshape(1, 2 * ML_QK), seq)
    mixed = _mixer(proj, g, gt, tabs, ret_norm_w.reshape(1, RET_V), ml_norm_w.reshape(1, ML_V), ml_gate_b,
                   batch, seq)

    kmem, vmem = _memkv(mem, norm_mem_w.reshape(1, d), xa_wkv.astype(BF16))

    w_route_t = jnp.concatenate(
        [moe_w_router.T, moe_w_group.T, jnp.zeros((ROUTE_ROWS - N_EXPERTS - N_GROUPS, d), F32)], axis=0)
    b_route = jnp.concatenate(
        [moe_b_router, moe_b_group, jnp.zeros((ROUTE_ROWS - N_EXPERTS - N_GROUPS,), F32)]).reshape(ROUTE_ROWS, 1)
    tm = TOKEN_TILE
    sut = jnp.asarray(np.triu(np.ones((tm, tm), np.float32), 1), dtype=BF16)
    x2, h3, ri, rw, cnt = _attn_route(xf, mixed, kmem, vmem, w_out.astype(BF16), norm_xa_w.reshape(1, d),
                                      xa_wq.astype(BF16), xa_wo.astype(BF16), norm_moe_w.reshape(1, d),
                                      w_route_t, b_route, sut, batch, seq)

    bm = MOE_ROWS
    counts = cnt[:, 0].astype(jnp.int32)
    padded = (counts + bm - 1) // bm * bm
    pends = jnp.cumsum(padded)
    pstarts = pends - padded
    expert = ri[0:TOP_K]
    onehot = expert[None] == jnp.arange(N_EXPERTS, dtype=jnp.int32)[:, None, None]
    dest = jnp.sum(jnp.where(onehot, pstarts[:, None, None], 0), axis=0) + ri[TOP_K:2 * TOP_K]
    cap = n * TOP_K + N_EXPERTS * bm
    nblk = cap // bm
    blk_start = jnp.arange(nblk, dtype=jnp.int32) * bm
    blk_e = jnp.minimum(jnp.sum(blk_start[:, None] >= pends[None, :], axis=1), N_EXPERTS - 1).astype(jnp.int32)
    nused = (pends[-1] // bm).astype(jnp.int32).reshape(1)
    zpos = jnp.where(padded > counts, pends - bm, -1).astype(jnp.int32)
    zpos = jnp.concatenate([zpos, nused])

    def tiles(rows):
        return dest.reshape(TOP_K, n // rows, rows).transpose(1, 0, 2).reshape(n // rows, TOP_K * rows)

    xs = _dispatch(h3, tiles(DISPATCH_TILE), zpos, cap)
    ys = _experts(xs, blk_e, nused, moe_w_gate, moe_w_up, moe_w_down)
    eye = jnp.asarray(np.eye(COMBINE_EYE, dtype=np.float32), dtype=BF16)
    return _combine(x2, ys, tiles(COMBINE_TILE), rw, eye, final_norm_w.reshape(1, d))


def kernel(x, mem, norm_mix_w, w_in, ret_norm_w, ml_conv_w, ml_conv_b, ml_gate_b, ml_norm_w, w_out, norm_xa_w, norm_mem_w, xa_wq, xa_wkv, xa_wo, norm_moe_w, moe_w_group, moe_b_group, moe_w_router, moe_b_router, moe_w_gate, moe_w_up, moe_w_down, norm_final_w):
    batch, seq, d = x.shape
    depth = w_in.shape[0]
    assert depth == 1, "the final norm is fused into the last layer's combine kernel"
    l = 0
    out = _layer(x.reshape(batch * seq, d), mem, batch, seq, norm_mix_w[l], w_in[l], ret_norm_w[l], ml_conv_w[l],
                 ml_conv_b[l], ml_gate_b[l], ml_norm_w[l], w_out[l], norm_xa_w[l], norm_mem_w[l], xa_wq[l],
                 xa_wkv[l], xa_wo[l], norm_moe_w[l], moe_w_group[l], moe_b_group[l], moe_w_router[l],
                 moe_b_router[l], moe_w_gate[l], moe_w_up[l], moe_w_down[l], norm_final_w)
    return out.reshape(batch, seq, d)
```

```python
import functools

import numpy as np
import jax
import jax.numpy as jnp
from jax import lax
from jax.experimental import pallas as pl
from jax.experimental.pallas import tpu as pltpu

F32 = jnp.float32
BF16 = jnp.bfloat16

CHUNK = 128
RET_HEADS = 4
RET_DK = 64
RET_DV = 128
ML_HEADS = 4
ML_DK = 128
ML_DV = 128
CONV_W = 4
XA_HEADS = 4
N_GROUPS = 4
EXP_PER_GROUP = 8
N_EXPERTS = N_GROUPS * EXP_PER_GROUP
TOP_K = 2
ROPE_BASE = 10000.0
EPS = 1e-6

RET_QK = RET_HEADS * RET_DK
RET_V = RET_HEADS * RET_DV
ML_QK = ML_HEADS * ML_DK
ML_V = ML_HEADS * ML_DV
OFF_RQ = 0
OFF_RK = OFF_RQ + RET_QK
OFF_RV = OFF_RK + RET_QK
OFF_RG = OFF_RV + RET_V
OFF_MQK = OFF_RG + RET_V
OFF_MV = OFF_MQK + 2 * ML_QK
OFF_MO = OFF_MV + ML_V
MAIN_WIDTH = OFF_MO + ML_V
N_GATES = 2 * ML_HEADS

ROUTE_ROWS = 40
TOKEN_TILE = 1024
MOE_ROWS = 512
DISPATCH_TILE = 1024
COMBINE_TILE = 512
CONV_ROWS = 128
COMBINE_EYE = 512
MIXER_BATCHES = 8
ATTN_GROUPS = 2
VMEM_LIMIT = 56 * 1024 * 1024


def _dot(a, b):
    return jnp.dot(a, b, preferred_element_type=F32)


def _dot_nt(a, b):
    return lax.dot_general(a, b, (((1,), (1,)), ((), ())), preferred_element_type=F32)


def _dot_tn(a, b):
    return lax.dot_general(a, b, (((0,), (0,)), ((), ())), preferred_element_type=F32)


def _rms(x, w):
    return x * lax.rsqrt(jnp.mean(x * x, axis=-1, keepdims=True) + EPS) * w


def _sigmoid(x):
    return 1.0 / (1.0 + jnp.exp(-x))


def _silu(x):
    return x * _sigmoid(x)


def _log_sigmoid(x):
    return jnp.minimum(x, 0.0) - jnp.log1p(jnp.exp(-jnp.abs(x)))


def _head_norms(ts, mean_w):
    mu = [_dot(t.astype(BF16), mean_w) for t in ts]
    dl = [t - m for t, m in zip(ts, mu)]
    var = [_dot((d * d).astype(BF16), mean_w) for d in dl]
    return [d * lax.rsqrt(v + EPS) for d, v in zip(dl, var)]


def _pack_rows(lo, hi):
    def bits(t):
        return lax.bitcast_convert_type(t.astype(BF16), jnp.uint16).astype(jnp.uint32)
    return bits(lo) | (bits(hi) << 16)


def _unpack_rows(u):
    lo = lax.bitcast_convert_type(u << 16, F32)
    hi = lax.bitcast_convert_type(u & jnp.uint32(0xFFFF0000), F32)
    return lo, hi


def _inproj_kernel(tiles_per_seq, x_ref, nw_ref, w_ref, wif_ref, wift_ref, cos_ref, sin_ref, convw_ref, convb_ref,
                   proj_ref, g_ref, gt_ref, carry_ref):
    tm = x_ref.shape[0]

    @pl.when(pl.program_id(0) == 0)
    def _():
        carry_ref[...] = jnp.zeros_like(carry_ref)

    h = _rms(x_ref[...], nw_ref[...]).astype(BF16)

    def mm(off, width):
        return _dot(h, w_ref[:, off:off + width])

    def rotary(qk):
        cos = cos_ref[...]
        sin = sin_ref[...]
        half = RET_QK // 2
        for off, scale in ((OFF_RQ, None), (OFF_RK, RET_DK ** -0.5)):
            t1 = qk[:, off - OFF_RQ:off - OFF_RQ + half]
            t2 = qk[:, off - OFF_RQ + half:off - OFF_RQ + 2 * half]
            r1 = t1 * cos - t2 * sin
            r2 = t1 * sin + t2 * cos
            if scale is not None:
                r1, r2 = r1 * scale, r2 * scale
            proj_ref[:, off:off + half] = r1.astype(BF16)
            proj_ref[:, off + half:off + 2 * half] = r2.astype(BF16)

    def conv_silu():
        cw = 2 * ML_QK
        first = (pl.program_id(0) % tiles_per_seq) == 0
        row8 = lax.broadcasted_iota(jnp.int32, (8, cw), 0)
        prev = jnp.where(first, 0.0, carry_ref[...])
        for r0 in range(0, tm, CONV_ROWS):
            cur = _dot(h[r0:r0 + CONV_ROWS], w_ref[:, OFF_MQK:OFF_MQK + cw])
            acc = cur * convw_ref[CONV_W - 1:CONV_W, :] + convb_ref[...]
            for s in range(1, CONV_W):
                rolled = pltpu.roll(cur, s, 0)
                head8 = jnp.where(row8 < s, pltpu.roll(prev, s, 0), rolled[0:8])
                shifted = jnp.concatenate([head8, rolled[8:]], axis=0)
                acc = acc + shifted * convw_ref[CONV_W - 1 - s:CONV_W - s, :]
            prev = cur[CONV_ROWS - 8:CONV_ROWS]
            act = _silu(acc)
            proj_ref[r0:r0 + CONV_ROWS, OFF_MQK:OFF_MQK + ML_QK] = act[:, :ML_QK].astype(BF16)
            proj_ref[r0:r0 + CONV_ROWS, OFF_MQK + ML_QK:OFF_MQK + cw] = (act[:, ML_QK:] * (ML_DK ** -0.5)).astype(BF16)
        carry_ref[...] = prev

    def store(off, width, fn=None):
        def ep(t):
            proj_ref[:, off:off + width] = (t if fn is None else fn(t)).astype(BF16)
        return ep

    rotary(mm(OFF_RQ, 2 * RET_QK))
    store(OFF_RV, RET_V)(mm(OFF_RV, RET_V))
    store(OFF_RG, RET_V, _silu)(mm(OFF_RG, RET_V))
    conv_silu()
    store(OFF_MV, ML_V)(mm(OFF_MV, ML_V))
    store(OFF_MO, ML_V, _sigmoid)(mm(OFF_MO, ML_V))
    g_ref[...] = _dot(h, wif_ref[...])
    gt_ref[...] = _dot_nt(wift_ref[...], h)


def _inproj(xf, norm_w, w_main, w_if, w_ift, cos, sin, conv_w, conv_b, seq):
    n, d = xf.shape
    tm = TOKEN_TILE
    tiles_per_seq = seq // tm
    return pl.pallas_call(
        functools.partial(_inproj_kernel, tiles_per_seq),
        grid=(n // tm,),
        in_specs=[
            pl.BlockSpec((tm, d), lambda i: (i, 0)),
            pl.BlockSpec((1, d), lambda i: (0, 0)),
            pl.BlockSpec((d, MAIN_WIDTH), lambda i: (0, 0)),
            pl.BlockSpec((d, N_GATES), lambda i: (0, 0)),
            pl.BlockSpec((N_GATES, d), lambda i: (0, 0)),
            pl.BlockSpec((tm, RET_QK // 2), lambda i: (i % tiles_per_seq, 0)),
            pl.BlockSpec((tm, RET_QK // 2), lambda i: (i % tiles_per_seq, 0)),
            pl.BlockSpec((CONV_W, 2 * ML_QK), lambda i: (0, 0)),
            pl.BlockSpec((1, 2 * ML_QK), lambda i: (0, 0)),
        ],
        out_specs=[
            pl.BlockSpec((tm, MAIN_WIDTH), lambda i: (i, 0)),
            pl.BlockSpec((tm, N_GATES), lambda i: (i, 0)),
            pl.BlockSpec((N_GATES, tm), lambda i: (0, i)),
        ],
        out_shape=[
            jax.ShapeDtypeStruct((n, MAIN_WIDTH), BF16),
            jax.ShapeDtypeStruct((n, N_GATES), F32),
            jax.ShapeDtypeStruct((N_GATES, n), F32),
        ],
        scratch_shapes=[pltpu.VMEM((8, 2 * ML_QK), F32)],
        compiler_params=pltpu.CompilerParams(
            dimension_semantics=("arbitrary",), vmem_limit_bytes=VMEM_LIMIT),
        name="inproj",
    )(xf, norm_w, w_main, w_if, w_ift, cos, sin, conv_w, conv_b)


def _mixer_kernel(*refs):
    @pl.when(pl.program_id(1) == 0)
    def _():
        for state_ref in refs[-4:]:
            state_ref[...] = jnp.zeros_like(state_ref)

    for bi in range(refs[0].shape[0]):
        _mixer_one(bi, *refs)


def _mixer_one(bi, proj_ref, g_ref, gt_ref, qdec_ref, kdec_ref, dmat_ref,
               bmask_ref, cdec_ref, hmask_ref, tril_ref, triu_ref, ones_ref, retw_ref, mlw_ref, gbc_ref, gbr_ref,
               out_ref, r_ref, c_ref, n_ref, m_ref):
    L = CHUNK
    proj_ref, g_ref, gt_ref, out_ref = proj_ref.at[bi], g_ref.at[bi], gt_ref.at[bi], out_ref.at[bi]
    r_ref, c_ref, n_ref, m_ref = r_ref.at[bi], c_ref.at[bi], n_ref.at[bi], m_ref.at[bi]
    mean_w = ones_ref[...]

    q = proj_ref[:, OFF_RQ:OFF_RQ + RET_QK].astype(F32)
    k_b = proj_ref[:, OFF_RK:OFF_RK + RET_QK]
    k = k_b.astype(F32)
    v = proj_ref[:, OFF_RV:OFF_RV + RET_V]
    r_prev = r_ref[...]
    cross = _dot((q * qdec_ref[...]).astype(BF16), r_prev.astype(BF16))
    kv = _dot_tn((k * kdec_ref[...]).astype(BF16), v) * bmask_ref[...]
    r_ref[...] = cdec_ref[...] * r_prev + kv
    RH = range(RET_HEADS)
    sc = [_dot_nt((q * hmask_ref[h:h + 1, :]).astype(BF16), k_b) for h in RH]
    sc = [(sc[h] * dmat_ref[h]).astype(BF16) for h in RH]
    tot = [_dot(sc[h], v[:, h * RET_DV:(h + 1) * RET_DV]) + cross[:, h * RET_DV:(h + 1) * RET_DV] for h in RH]
    ret = jnp.concatenate(_head_norms(tot, mean_w), axis=1) * retw_ref[...]
    ret = ret * proj_ref[:, OFF_RG:OFF_RG + RET_V].astype(F32)
    out_ref[:, 0:RET_V] = ret.astype(BF16)

    mq = proj_ref[:, OFF_MQK:OFF_MQK + ML_QK]
    mk = proj_ref[:, OFF_MQK + ML_QK:OFF_MQK + 2 * ML_QK]
    mv = proj_ref[:, OFF_MV:OFF_MV + ML_V]

    gc = g_ref[...] + gbc_ref[...]
    gr = gt_ref[...] + gbr_ref[...]
    lf_c = _log_sigmoid(gc)
    lf_r = _log_sigmoid(gr)
    lf_c_hi = lf_c.astype(BF16)
    lf_r_hi = lf_r.astype(BF16)
    b_c = (_dot(tril_ref[...], lf_c_hi)
           + _dot(tril_ref[...], (lf_c - lf_c_hi.astype(F32)).astype(BF16)))
    b_r = (_dot(lf_r_hi, triu_ref[...])
           + _dot((lf_r - lf_r_hi.astype(F32)).astype(BF16), triu_ref[...]))
    causal = (lax.broadcasted_iota(jnp.int32, (L, L), 0) >= lax.broadcasted_iota(jnp.int32, (L, L), 1))
    MH = range(ML_HEADS)
    bc = [b_c[:, ML_HEADS + h:ML_HEADS + h + 1] for h in MH]
    br = [b_r[ML_HEADS + h:ML_HEADS + h + 1, :] for h in MH]
    igc = [gc[:, h:h + 1] for h in MH]
    igr = [gr[h:h + 1, :] for h in MH]
    btot = [br[h][:, L - 1:L] for h in MH]
    qh_b = [mq[:, h * ML_DK:(h + 1) * ML_DK] for h in MH]
    kh_b = [mk[:, h * ML_DK:(h + 1) * ML_DK] for h in MH]
    vh = [mv[:, h * ML_DV:(h + 1) * ML_DV] for h in MH]
    c_prev = [c_ref[h] for h in MH]
    n_prev = [n_ref[h][0:1, :] for h in MH]
    m_prev = [m_ref[h][0:1, 0:1] for h in MH]
    s_raw = [_dot_nt(qh_b[h], kh_b[h]) for h in MH]
    qc = [_dot(qh_b[h], c_prev[h].astype(BF16)) for h in MH]
    log_d = [jnp.where(causal, bc[h] - br[h] + igr[h], -jnp.inf) for h in MH]
    m_intra = [jnp.max(log_d[h], axis=1, keepdims=True) for h in MH]
    m_loc = [jnp.max(btot[h] - br[h] + igr[h], axis=1, keepdims=True) for h in MH]
    kw = [kh_b[h].astype(F32) * jnp.exp(btot[h] - bc[h] + igc[h] - m_loc[h]) for h in MH]
    kv_loc = [_dot_tn(kw[h].astype(BF16), vh[h]) for h in MH]
    n_loc = [jnp.sum(kw[h], axis=0, keepdims=True) for h in MH]
    m_inter = [bc[h] + m_prev[h] for h in MH]
    m_t = [jnp.maximum(m_intra[h], m_inter[h]) for h in MH]
    s_mat = [s_raw[h] * jnp.exp(log_d[h] - m_t[h]) for h in MH]
    inter = [jnp.exp(m_inter[h] - m_t[h]) for h in MH]
    num = [_dot(s_mat[h].astype(BF16), vh[h]) + inter[h] * qc[h] for h in MH]
    den = [jnp.sum(s_mat[h], axis=1, keepdims=True)
           + inter[h] * jnp.sum(qh_b[h].astype(F32) * n_prev[h], axis=1, keepdims=True) for h in MH]
    hh = [num[h] / jnp.maximum(jnp.abs(den[h]), jnp.exp(-m_t[h])) for h in MH]
    for h in MH:
        m_new = jnp.maximum(btot[h] + m_prev[h], m_loc[h])
        s_old = jnp.exp(btot[h] + m_prev[h] - m_new)
        s_loc = jnp.exp(m_loc[h] - m_new)
        c_ref[h] = s_old * c_prev[h] + s_loc * kv_loc[h]
        n_ref[h] = jnp.broadcast_to(s_old * n_prev[h] + s_loc * n_loc[h], (8, ML_DK))
        m_ref[h] = jnp.broadcast_to(m_new, (8, 128))
    ml = jnp.concatenate(_head_norms(hh, mean_w), axis=1) * mlw_ref[...]
    ml = ml * proj_ref[:, OFF_MO:OFF_MO + ML_V].astype(F32)
    out_ref[:, RET_V:RET_V + ML_V] = ml.astype(BF16)


def _mixer_tables(seq):
    L = CHUNK
    half = RET_DK // 2
    inv = ROPE_BASE ** (-np.arange(half, dtype=np.float64) / half)
    ang = np.arange(seq, dtype=np.float64)[:, None] * inv[None, :].astype(np.float32).astype(np.float64)
    cos = np.tile(np.cos(ang), (1, RET_HEADS)).astype(np.float32)
    sin = np.tile(np.sin(ang), (1, RET_HEADS)).astype(np.float32)
    log_g = np.log1p(-np.exp2(-5.0 - np.arange(RET_HEADS, dtype=np.float64)))
    n = np.arange(L, dtype=np.float64)
    lane_head = (np.arange(RET_QK) % (RET_QK // 2)) // half
    qdec = np.exp((n + 1)[:, None] * log_g[lane_head][None, :]).astype(np.float32)
    kdec = np.exp((L - 1 - n)[:, None] * log_g[lane_head][None, :]).astype(np.float32)
    diff = n[:, None] - n[None, :]
    dmat = np.where(diff >= 0, np.exp(log_g[:, None, None] * np.maximum(diff, 0.0)[None]), 0.0).astype(np.float32)
    col_head = np.arange(RET_V) // RET_DV
    bmask = (lane_head[:, None] == col_head[None, :]).astype(np.float32)
    cdec = np.exp(L * log_g[col_head])[None, :].astype(np.float32)
    hmask = (lane_head[None, :] == np.arange(RET_HEADS)[:, None]).astype(np.float32)
    hmask = np.concatenate([hmask, np.zeros((8 - RET_HEADS, RET_QK), np.float32)], axis=0)
    tril = np.tril(np.ones((L, L), np.float32))
    ones = np.full((RET_DV, RET_DV), 1.0 / RET_DV, np.float32)
    return dict(cos=cos, sin=sin, qdec=qdec, kdec=kdec, dmat=dmat, bmask=bmask, cdec=cdec, hmask=hmask,
                tril=tril, triu=np.ascontiguousarray(tril.T), ones=ones)


def _mixer(proj, g, gt, tabs, ret_norm_w, ml_norm_w, gate_b, batch, seq):
    L = CHUNK
    nc = seq // L
    n = batch * seq
    nb = MIXER_BATCHES if batch % MIXER_BATCHES == 0 else 1
    proj = proj.reshape(batch, seq, MAIN_WIDTH)
    g = g.reshape(batch, seq, N_GATES)
    gt = gt.reshape(N_GATES, batch, seq).transpose(1, 0, 2)
    const2 = lambda b, c: (0, 0)
    const3 = lambda b, c: (0, 0, 0)
    tok = lambda b, c: (b, c, 0)
    in_specs = [
        pl.BlockSpec((nb, L, MAIN_WIDTH), tok),
        pl.BlockSpec((nb, L, N_GATES), tok),
        pl.BlockSpec((nb, N_GATES, L), lambda b, c: (b, 0, c)),
        pl.BlockSpec((L, RET_QK), const2),
        pl.BlockSpec((L, RET_QK), const2),
        pl.BlockSpec((RET_HEADS, L, L), const3),
        pl.BlockSpec((RET_QK, RET_V), const2),
        pl.BlockSpec((1, RET_V), const2),
        pl.BlockSpec((8, RET_QK), const2),
        pl.BlockSpec((L, L), const2),
        pl.BlockSpec((L, L), const2),
        pl.BlockSpec((RET_DV, RET_DV), const2),
        pl.BlockSpec((1, RET_V), const2),
        pl.BlockSpec((1, ML_V), const2),
        pl.BlockSpec((1, N_GATES), const2),
        pl.BlockSpec((N_GATES, 1), const2),
    ]
    return pl.pallas_call(
        _mixer_kernel,
        grid=(batch // nb, nc),
        in_specs=in_specs,
        out_specs=pl.BlockSpec((nb, L, RET_V + ML_V), tok),
        out_shape=jax.ShapeDtypeStruct((batch, seq, RET_V + ML_V), BF16),
        scratch_shapes=[
            pltpu.VMEM((nb, RET_QK, RET_V), F32),
            pltpu.VMEM((nb, ML_HEADS, ML_DK, ML_DV), F32),
            pltpu.VMEM((nb, ML_HEADS, 8, ML_DK), F32),
            pltpu.VMEM((nb, ML_HEADS, 8, 128), F32),
        ],
        compiler_params=pltpu.CompilerParams(
            dimension_semantics=("arbitrary", "arbitrary"), vmem_limit_bytes=VMEM_LIMIT),
        name="mixer",
    )(proj, g, gt, tabs["qdec"], tabs["kdec"], tabs["dmat"], tabs["bmask"],
      tabs["cdec"], tabs["hmask"], tabs["tril"].astype(BF16), tabs["triu"].astype(BF16), tabs["ones"].astype(BF16),
      ret_norm_w, ml_norm_w,
      gate_b.reshape(1, N_GATES), gate_b.reshape(N_GATES, 1)).reshape(n, RET_V + ML_V)


def _memkv_kernel(mem_ref, nw_ref, wkv_ref, k_ref, v_ref):
    d = mem_ref.shape[-1]
    mn = _rms(mem_ref[0], nw_ref[...]).astype(BF16)
    k_ref[0] = _dot(mn, wkv_ref[:, :d]).astype(BF16)
    v_ref[0] = _dot(mn, wkv_ref[:, d:]).astype(BF16)


def _memkv(mem, norm_w, wkv):
    b, m, d = mem.shape
    return pl.pallas_call(
        _memkv_kernel,
        grid=(b,),
        in_specs=[
            pl.BlockSpec((1, m, d), lambda i: (i, 0, 0)),
            pl.BlockSpec((1, d), lambda i: (0, 0)),
            pl.BlockSpec((d, 2 * d), lambda i: (0, 0)),
        ],
        out_specs=[pl.BlockSpec((1, m, d), lambda i: (i, 0, 0))] * 2,
        out_shape=[jax.ShapeDtypeStruct((b, m, d), BF16)] * 2,
        compiler_params=pltpu.CompilerParams(
            dimension_semantics=("arbitrary",), vmem_limit_bytes=VMEM_LIMIT),
        name="memkv",
    )(mem, norm_w, wkv)


def _attn_route_kernel(x_ref, mix_ref, k_ref, v_ref, wout_ref, nxa_ref, wq_ref, wo_ref, nmoe_ref,
                       wr_ref, wrlo_ref, br_ref, sut_ref,
                       x2_ref, h3_ref, ri_ref, rw_ref, cnt_ref, carry_ref):
    tm, d = x_ref.shape
    dh = d // XA_HEADS

    @pl.when((pl.program_id(0) == 0) & (pl.program_id(1) == 0))
    def _():
        carry_ref[...] = jnp.zeros_like(carry_ref)

    groups = [slice(g * (tm // ATTN_GROUPS), (g + 1) * (tm // ATTN_GROUPS)) for g in range(ATTN_GROUPS)]
    x1 = [x_ref[s, :] + _dot(mix_ref[s, :], wout_ref[...]) for s in groups]
    h2 = [_rms(t, nxa_ref[...]).astype(BF16) for t in x1]
    q = [_dot(t, wq_ref[...]).astype(BF16) for t in h2]
    o = []
    for qg in q:
        heads = []
        for h in range(XA_HEADS):
            logits = _dot_nt(qg[:, h * dh:(h + 1) * dh], k_ref[0, :, h * dh:(h + 1) * dh]) * (dh ** -0.5)
            mx = jnp.max(logits, axis=-1, keepdims=True)
            e = jnp.exp(logits - mx)
            p = (e / jnp.sum(e, axis=-1, keepdims=True)).astype(BF16)
            heads.append(_dot(p, v_ref[0, :, h * dh:(h + 1) * dh]).astype(BF16))
        o.append(jnp.concatenate(heads, axis=1))
    x2 = [a + _dot(b, wo_ref[...]) for a, b in zip(x1, o)]
    for s, t in zip(groups, x2):
        x2_ref[s, :] = t
    h3 = [_rms(t, nmoe_ref[...]) for t in x2]
    for s, t in zip(groups, h3):
        h3_ref[s, 0, :] = _pack_rows(t[:, :d // 2], t[:, d // 2:])

    lts = []
    for t in h3:
        t_hi = t.astype(BF16)
        t_lo = (t - t_hi.astype(F32)).astype(BF16)
        lts.append(_dot_nt(wr_ref[...], t_hi) + (_dot_nt(wr_ref[...], t_lo) + _dot_nt(wrlo_ref[...], t_hi)))
    lt = jnp.concatenate(lts, axis=1) + br_ref[...]
    gl = lt[N_EXPERTS:N_EXPERTS + N_GROUPS]
    gmax = jnp.max(gl, axis=0, keepdims=True)
    g_w = 1.0 / jnp.sum(jnp.exp(gl - gmax), axis=0, keepdims=True)
    giota = lax.broadcasted_iota(jnp.int32, gl.shape, 0)
    g_sel = jnp.min(jnp.where(gl == gmax, giota, N_GROUPS), axis=0, keepdims=True)
    el = lt[0:N_EXPERTS]
    eiota = lax.broadcasted_iota(jnp.int32, el.shape, 0)
    in_grp = (eiota // EXP_PER_GROUP) == g_sel
    elm = jnp.where(in_grp, el, -jnp.inf)
    m1 = jnp.max(elm, axis=0, keepdims=True)
    esum = jnp.sum(jnp.where(in_grp, jnp.exp(el - m1), 0.0), axis=0, keepdims=True)
    i1 = jnp.min(jnp.where(elm == m1, eiota, N_EXPERTS), axis=0, keepdims=True)
    elm2 = jnp.where(eiota == i1, -jnp.inf, elm)
    m2 = jnp.max(elm2, axis=0, keepdims=True)
    i2 = jnp.min(jnp.where(elm2 == m2, eiota, N_EXPERTS), axis=0, keepdims=True)
    p1 = 1.0 / esum
    p2 = jnp.exp(m2 - m1) / esum
    psum = p1 + p2
    w1 = g_w * (p1 / psum)
    w2 = g_w * (p2 / psum)

    oh1 = (eiota == i1).astype(F32)
    oh2 = (eiota == i2).astype(F32)
    cnt = oh1 + oh2
    base = carry_ref[:, 0:1] + _dot(cnt.astype(BF16), sut_ref[...])
    r1 = jnp.sum(oh1 * base, axis=0, keepdims=True)
    r2 = jnp.sum(oh2 * base, axis=0, keepdims=True)
    new_carry = carry_ref[...] + jnp.sum(cnt, axis=1, keepdims=True)
    carry_ref[...] = new_carry
    cnt_ref[...] = new_carry

    zi = jnp.zeros((4, tm), jnp.int32)
    ri_ref[...] = jnp.concatenate([i1, i2, r1.astype(jnp.int32), r2.astype(jnp.int32), zi], axis=0)
    rw_ref[...] = jnp.concatenate([w1, w2, jnp.zeros((6, tm), F32)], axis=0)


def _attn_route(xf, mixed, kmem, vmem, w_out, norm_xa_w, wq, wo, norm_moe_w, w_route_t, b_route, sut,
                batch, seq):
    n, d = xf.shape
    w_route_hi = w_route_t.astype(BF16)
    w_route_lo = (w_route_t - w_route_hi.astype(F32)).astype(BF16)
    tm = TOKEN_TILE
    nt = seq // tm
    m = kmem.shape[1]
    tok = lambda b, t: (b * nt + t, 0)
    lane_tok = lambda b, t: (0, b * nt + t)
    const2 = lambda b, t: (0, 0)
    return pl.pallas_call(
        _attn_route_kernel,
        grid=(batch, nt),
        in_specs=[
            pl.BlockSpec((tm, d), tok),
            pl.BlockSpec((tm, d), tok),
            pl.BlockSpec((1, m, d), lambda b, t: (b, 0, 0)),
            pl.BlockSpec((1, m, d), lambda b, t: (b, 0, 0)),
            pl.BlockSpec((d, d), const2),
            pl.BlockSpec((1, d), const2),
            pl.BlockSpec((d, d), const2),
            pl.BlockSpec((d, d), const2),
            pl.BlockSpec((1, d), const2),
            pl.BlockSpec((ROUTE_ROWS, d), const2),
            pl.BlockSpec((ROUTE_ROWS, d), const2),
            pl.BlockSpec((ROUTE_ROWS, 1), const2),
            pl.BlockSpec((tm, tm), const2),
        ],
        out_specs=[
            pl.BlockSpec((tm, d), tok),
            pl.BlockSpec((tm, 1, d // 2), lambda b, t: (b * nt + t, 0, 0)),
            pl.BlockSpec((8, tm), lane_tok),
            pl.BlockSpec((8, tm), lane_tok),
            pl.BlockSpec((N_EXPERTS, 128), const2),
        ],
        out_shape=[
            jax.ShapeDtypeStruct((n, d), F32),
            jax.ShapeDtypeStruct((n, 1, d // 2), jnp.uint32),
            jax.ShapeDtypeStruct((8, n), jnp.int32),
            jax.ShapeDtypeStruct((8, n), F32),
            jax.ShapeDtypeStruct((N_EXPERTS, 128), F32),
        ],
        scratch_shapes=[pltpu.VMEM((N_EXPERTS, 128), F32)],
        compiler_params=pltpu.CompilerParams(
            dimension_semantics=("arbitrary", "arbitrary"), vmem_limit_bytes=VMEM_LIMIT),
        name="attn_route",
    )(xf, mixed, kmem, vmem, w_out, norm_xa_w, wq, wo, norm_moe_w, w_route_hi, w_route_lo, b_route, sut)


def _dispatch_kernel(zpos_ref, dest_ref, h_ref, xs_ref, idx_ref, idx_sem, row_sem, zero_ref, zero_sem):
    i = pl.program_id(0)
    nsteps = pl.num_programs(0)
    td = h_ref.shape[0]
    bm = zero_ref.shape[0]
    slot = i % 2

    def idx_copy(step, sl):
        off = pl.multiple_of(sl * (2 * td), 2 * td)
        return pltpu.make_async_copy(dest_ref.at[step], idx_ref.at[pl.ds(off, 2 * td)], idx_sem.at[sl])

    @pl.when(i == 0)
    def _():
        zero_ref[...] = jnp.zeros_like(zero_ref)

        def zero_copy(e):
            return pltpu.make_async_copy(zero_ref, xs_ref.at[pl.ds(pl.multiple_of(zpos_ref[e], bm), bm), 0], zero_sem)

        def tail_copy(b):
            return pltpu.make_async_copy(zero_ref, xs_ref.at[pl.ds(pl.multiple_of(b * bm, bm), bm), 0], zero_sem)

        nused = zpos_ref[N_EXPERTS]
        nblk = xs_ref.shape[0] // bm
        for e in range(N_EXPERTS):
            pl.when(zpos_ref[e] >= 0)(lambda e=e: zero_copy(e).start())
        lax.fori_loop(nused, nblk, lambda b, c: (tail_copy(b).start(), c)[1], 0)
        for e in range(N_EXPERTS):
            pl.when(zpos_ref[e] >= 0)(lambda e=e: zero_copy(e).wait())
        lax.fori_loop(nused, nblk, lambda b, c: (tail_copy(b).wait(), c)[1], 0)
        idx_copy(0, 0).start()

    idx_copy(i, slot).wait()

    @pl.when(i + 1 < nsteps)
    def _():
        idx_copy(i + 1, 1 - slot).start()

    base = slot * (2 * td)

    for t in range(td):
        for k in range(TOP_K):
            pltpu.make_async_copy(h_ref.at[t], xs_ref.at[idx_ref[base + k * td + t]], row_sem).start(priority=t % 2)
    for _ in range(TOP_K):
        pltpu.make_async_copy(xs_ref.at[pl.ds(0, td)], xs_ref.at[pl.ds(0, td)], row_sem).wait()


def _dispatch(h3p, dest_tiles, zpos, cap):
    w = h3p.shape[-1]
    nt, td2 = dest_tiles.shape
    td = td2 // 2
    grid_spec = pltpu.PrefetchScalarGridSpec(
        num_scalar_prefetch=1,
        grid=(nt,),
        in_specs=[
            pl.BlockSpec(memory_space=pl.ANY),
            pl.BlockSpec((td, 1, w), lambda i, zp: (i, 0, 0)),
        ],
        out_specs=pl.BlockSpec(memory_space=pl.ANY),
        scratch_shapes=[
            pltpu.SMEM((2 * td2,), jnp.int32),
            pltpu.SemaphoreType.DMA((2,)),
            pltpu.SemaphoreType.DMA,
            pltpu.VMEM((MOE_ROWS, w), jnp.uint32),
            pltpu.SemaphoreType.DMA,
        ],
    )
    return pl.pallas_call(
        _dispatch_kernel,
        grid_spec=grid_spec,
        out_shape=jax.ShapeDtypeStruct((cap, 1, w), jnp.uint32),
        compiler_params=pltpu.CompilerParams(
            dimension_semantics=("arbitrary",), vmem_limit_bytes=VMEM_LIMIT),
        name="dispatch",
    )(zpos, dest_tiles, h3p)


def _expert_kernel(blk_e_ref, nused_ref, xs_ref, wg_ref, wu_ref, wd_ref, ys_ref, wg_b, wu_b, wd_b,
                   xbuf, ybuf, zbuf, in_sem, out_sem, zero_sem):
    i = pl.program_id(0)
    nsteps = pl.num_programs(0)
    nused = nused_ref[0]
    bm = xbuf.shape[1]
    slot = i % 2
    prev = blk_e_ref[jnp.maximum(i - 1, 0)]
    fresh = (i == 0) | (blk_e_ref[i] != prev)
    half = wd_b.shape[1] // 2

    def rows(ref, step):
        return ref.at[pl.ds(pl.multiple_of(step * bm, bm), bm), 0]

    def in_copy(step, sl):
        return pltpu.make_async_copy(rows(xs_ref, step), xbuf.at[sl], in_sem.at[sl])

    def out_copy(step, sl):
        return pltpu.make_async_copy(ybuf.at[sl], rows(ys_ref, step), out_sem.at[sl])

    def zero_copy(step):
        return pltpu.make_async_copy(zbuf, rows(ys_ref, step), zero_sem)

    @pl.when(i == 0)
    def _():
        zbuf[...] = jnp.zeros_like(zbuf)
        in_copy(0, 0).start()

    @pl.when(i + 1 < nused)
    def _():
        in_copy(i + 1, 1 - slot).start()

    @pl.when(fresh)
    def _():
        wg_b[...] = wg_ref[0].astype(BF16)
        wu_b[...] = wu_ref[0].astype(BF16)
        wd_b[...] = wd_ref[0].astype(BF16)

    @pl.when(i < nused)
    def _():
        in_copy(i, slot).wait()
        pl.when(i >= 2)(lambda: out_copy(i - 2, slot).wait())
        lo, hi = _unpack_rows(xbuf[slot])
        xb = jnp.concatenate([lo.astype(BF16), hi.astype(BF16)], axis=1)
        hid = (_silu(_dot(xb, wg_b[...])) * _dot(xb, wu_b[...])).astype(BF16)
        y = _dot(hid, wd_b[...])
        ybuf[slot] = _pack_rows(y[:, :half], y[:, half:])
        out_copy(i, slot).start(priority=1)

    pl.when(i >= nused)(lambda: zero_copy(i).start())

    @pl.when(i == nsteps - 1)
    def _():
        pl.when(nused >= 2)(lambda: out_copy(nused - 2, nused % 2).wait())
        pl.when(nused >= 1)(lambda: out_copy(nused - 1, (nused - 1) % 2).wait())
        lax.fori_loop(nused, nsteps, lambda b, c: (zero_copy(b).wait(), c)[1], 0)


def _experts(xs, blk_e, nused, w_gate, w_up, w_down):
    cap, _, w = xs.shape
    _, d, de = w_gate.shape
    bm = MOE_ROWS
    wspec = lambda blk: pl.BlockSpec(blk, lambda i, be, nu: (be[i], 0, 0))
    grid_spec = pltpu.PrefetchScalarGridSpec(
        num_scalar_prefetch=2,
        grid=(cap // bm,),
        in_specs=[pl.BlockSpec(memory_space=pl.ANY), wspec((1, d, de)), wspec((1, d, de)), wspec((1, de, d))],
        out_specs=pl.BlockSpec(memory_space=pl.ANY),
        scratch_shapes=[
            pltpu.VMEM((d, de), BF16),
            pltpu.VMEM((d, de), BF16),
            pltpu.VMEM((de, d), BF16),
            pltpu.VMEM((2, bm, w), jnp.uint32),
            pltpu.VMEM((2, bm, w), jnp.uint32),
            pltpu.VMEM((bm, w), jnp.uint32),
            pltpu.SemaphoreType.DMA((2,)),
            pltpu.SemaphoreType.DMA((2,)),
            pltpu.SemaphoreType.DMA,
        ],
    )
    return pl.pallas_call(
        _expert_kernel,
        grid_spec=grid_spec,
        out_shape=jax.ShapeDtypeStruct((cap, 1, w), jnp.uint32),
        compiler_params=pltpu.CompilerParams(
            dimension_semantics=("arbitrary",), vmem_limit_bytes=VMEM_LIMIT),
        name="experts",
    )(blk_e, nused, xs, w_gate, w_up, w_down)


def _combine_kernel(dest_ref, ys_ref, x2_ref, rw_ref, eye_ref, nw_ref, o_ref, idx_ref, idx_sem, ybuf, ysem):
    i = pl.program_id(0)
    nsteps = pl.num_programs(0)
    tc, d = x2_ref.shape
    half = d // 2
    n_idx = 2 * tc

    def idx_copy(step):
        sl = step % 3
        off = pl.multiple_of(sl * n_idx, n_idx)
        return pltpu.make_async_copy(dest_ref.at[step], idx_ref.at[pl.ds(off, n_idx)], idx_sem.at[sl])

    def gather(step):
        base = (step % 3) * n_idx
        buf = ybuf.at[step % 2]
        sem = ysem.at[step % 2]

        for t in range(n_idx):
            pltpu.make_async_copy(ys_ref.at[idx_ref[base + t]], buf.at[pl.ds(t, 1)], sem).start(priority=t % 2)

    @pl.when(i == 0)
    def _():
        idx_copy(0).start()
        idx_copy(0).wait()
        gather(0)

        @pl.when(nsteps > 1)
        def _():
            idx_copy(1).start()

    @pl.when(i + 1 < nsteps)
    def _():
        idx_copy(i + 1).wait()

        @pl.when(i + 2 < nsteps)
        def _():
            idx_copy(i + 2).start()

        gather(i + 1)

    slot = i % 2
    pltpu.make_async_copy(ybuf.at[slot], ybuf.at[slot], ysem.at[slot]).wait()
    eye = eye_ref[...]
    ew = eye.shape[0]

    def to_columns(r):
        r_a = r.astype(BF16)
        r_b = (r - r_a.astype(F32)).astype(BF16)
        r_c = (r - r_a.astype(F32) - r_b.astype(F32)).astype(BF16)
        return _dot_nt(eye, r_a) + (_dot_nt(eye, r_b) + _dot_nt(eye, r_c))

    wcol = jnp.concatenate([to_columns(rw_ref[:, j:j + ew]) for j in range(0, tc, ew)], axis=0)
    lo1, hi1 = _unpack_rows(ybuf[slot, 0:tc])
    lo2, hi2 = _unpack_rows(ybuf[slot, tc:n_idx])
    w1 = wcol[:, 0:1]
    w2 = wcol[:, 1:2]
    z_lo = x2_ref[:, :half] + (lo1 * w1 + lo2 * w2)
    z_hi = x2_ref[:, half:] + (hi1 * w1 + hi2 * w2)
    ms = (jnp.sum(z_lo * z_lo, axis=-1, keepdims=True) + jnp.sum(z_hi * z_hi, axis=-1, keepdims=True)) / d
    scale = lax.rsqrt(ms + EPS)
    o_ref[:, :half] = z_lo * scale * nw_ref[:, :half]
    o_ref[:, half:] = z_hi * scale * nw_ref[:, half:]


def _combine(x2, ys, dest_tiles, rw, eye, norm_w):
    n, d = x2.shape
    nt, n_idx = dest_tiles.shape
    tc = n_idx // 2
    w = ys.shape[-1]
    return pl.pallas_call(
        _combine_kernel,
        grid=(nt,),
        in_specs=[pl.BlockSpec(memory_space=pl.ANY)] * 2 + [
            pl.BlockSpec((tc, d), lambda i: (i, 0)),
            pl.BlockSpec((8, tc), lambda i: (0, i)),
            pl.BlockSpec(eye.shape, lambda i: (0, 0)),
            pl.BlockSpec((1, d), lambda i: (0, 0)),
        ],
        out_specs=pl.BlockSpec((tc, d), lambda i: (i, 0)),
        out_shape=jax.ShapeDtypeStruct((n, d), F32),
        scratch_shapes=[
            pltpu.SMEM((3 * n_idx,), jnp.int32),
            pltpu.SemaphoreType.DMA((3,)),
            pltpu.VMEM((2, n_idx, w), jnp.uint32),
            pltpu.SemaphoreType.DMA((2,)),
        ],
        compiler_params=pltpu.CompilerParams(
            dimension_semantics=("arbitrary",), vmem_limit_bytes=VMEM_LIMIT),
        name="combine",
    )(dest_tiles, ys, x2, rw, eye, norm_w)


def _layer(xf, mem, batch, seq, norm_mix_w, w_in, ret_norm_w, ml_conv_w, ml_conv_b, ml_gate_b, ml_norm_w,
           w_out, norm_xa_w, norm_mem_w, xa_wq, xa_wkv, xa_wo, norm_moe_w, moe_w_group, moe_b_group,
           moe_w_router, moe_b_router, moe_w_gate, moe_w_up, moe_w_down, final_norm_w):
    n, d = xf.shape
    def halves_first(w):
        return w.reshape(d, RET_HEADS, 2, RET_DK // 2).transpose(0, 2, 1, 3).reshape(d, RET_QK)

    w_main = jnp.concatenate([halves_first(w_in[:, OFF_RQ:OFF_RQ + RET_QK]),
                              halves_first(w_in[:, OFF_RK:OFF_RK + RET_QK]),
                              w_in[:, OFF_RV:MAIN_WIDTH]], axis=1).astype(BF16)
    w_if = w_in[:, MAIN_WIDTH:].astype(BF16)
    tabs = {k_: jnp.asarray(v_) for k_, v_ in _mixer_tables(seq).items()}
    proj, g, gt = _inproj(xf, norm_mix_w.reshape(1, d), w_main, w_if, w_if.T, tabs["cos"], tabs["sin"], ml_conv_w,
                          ml_conv_b.reshape(1, 2 * ML_QK), seq)
    mixed = _mixer(proj, g, gt, tabs, ret_norm_w.reshape(1, RET_V), ml_norm_w.reshape(1, ML_V), ml_gate_b,
                   batch, seq)

    kmem, vmem = _memkv(mem, norm_mem_w.reshape(1, d), xa_wkv.astype(BF16))

    w_route_t = jnp.concatenate(
        [moe_w_router.T, moe_w_group.T, jnp.zeros((ROUTE_ROWS - N_EXPERTS - N_GROUPS, d), F32)], axis=0)
    b_route = jnp.concatenate(
        [moe_b_router, moe_b_group, jnp.zeros((ROUTE_ROWS - N_EXPERTS - N_GROUPS,), F32)]).reshape(ROUTE_ROWS, 1)
    tm = TOKEN_TILE
    sut = jnp.asarray(np.triu(np.ones((tm, tm), np.float32), 1), dtype=BF16)
    x2, h3, ri, rw, cnt = _attn_route(xf, mixed, kmem, vmem, w_out.astype(BF16), norm_xa_w.reshape(1, d),
                                      xa_wq.astype(BF16), xa_wo.astype(BF16), norm_moe_w.reshape(1, d),
                                      w_route_t, b_route, sut, batch, seq)

    bm = MOE_ROWS
    counts = cnt[:, 0].astype(jnp.int32)
    padded = (counts + bm - 1) // bm * bm
    pends = jnp.cumsum(padded)
    pstarts = pends - padded
    expert = ri[0:TOP_K]
    onehot = expert[None] == jnp.arange(N_EXPERTS, dtype=jnp.int32)[:, None, None]
    dest = jnp.sum(jnp.where(onehot, pstarts[:, None, None], 0), axis=0) + ri[TOP_K:2 * TOP_K]
    cap = n * TOP_K + N_EXPERTS * bm
    nblk = cap // bm
    blk_start = jnp.arange(nblk, dtype=jnp.int32) * bm
    blk_e = jnp.minimum(jnp.sum(blk_start[:, None] >= pends[None, :], axis=1), N_EXPERTS - 1).astype(jnp.int32)
    nused = (pends[-1] // bm).astype(jnp.int32).reshape(1)
    zpos = jnp.where(padded > counts, pends - bm, -1).astype(jnp.int32)
    zpos = jnp.concatenate([zpos, nused])

    def tiles(rows):
        return dest.reshape(TOP_K, n // rows, rows).transpose(1, 0, 2).reshape(n // rows, TOP_K * rows)

    xs = _dispatch(h3, tiles(DISPATCH_TILE), zpos, cap)
    ys = _experts(xs, blk_e, nused, moe_w_gate, moe_w_up, moe_w_down)
    eye = jnp.asarray(np.eye(COMBINE_EYE, dtype=np.float32), dtype=BF16)
    return _combine(x2, ys, tiles(COMBINE_TILE), rw, eye, final_norm_w.reshape(1, d))


def kernel(x, mem, norm_mix_w, w_in, ret_norm_w, ml_conv_w, ml_conv_b, ml_gate_b, ml_norm_w, w_out, norm_xa_w, norm_mem_w, xa_wq, xa_wkv, xa_wo, norm_moe_w, moe_w_group, moe_b_group, moe_w_router, moe_b_router, moe_w_gate, moe_w_up, moe_w_down, norm_final_w):
    batch, seq, d = x.shape
    depth = w_in.shape[0]
    assert depth == 1, "the final norm is fused into the last layer's combine kernel"
    l = 0
    out = _layer(x.reshape(batch * seq, d), mem, batch, seq, norm_mix_w[l], w_in[l], ret_norm_w[l], ml_conv_w[l],
                 ml_conv_b[l], ml_gate_b[l], ml_norm_w[l], w_out[l], norm_xa_w[l], norm_mem_w[l], xa_wq[l],
                 xa_wkv[l], xa_wo[l], norm_moe_w[l], moe_w_group[l], moe_b_group[l], moe_w_router[l],
                 moe_b_router[l], moe_w_gate[l], moe_w_up[l], moe_w_down[l], norm_final_w)
    return out.reshape(batch, seq, d)
```

```python
import functools

import numpy as np
import jax
import jax.numpy as jnp
from jax import lax
from jax.experimental import pallas as pl
from jax.experimental.pallas import tpu as pltpu

F32 = jnp.float32
BF16 = jnp.bfloat16

CHUNK = 128
RET_HEADS = 4
RET_DK = 64
RET_DV = 128
ML_HEADS = 4
ML_DK = 128
ML_DV = 128
CONV_W = 4
XA_HEADS = 4
N_GROUPS = 4
EXP_PER_GROUP = 8
N_EXPERTS = N_GROUPS * EXP_PER_GROUP
TOP_K = 2
ROPE_BASE = 10000.0
EPS = 1e-6

RET_QK = RET_HEADS * RET_DK
RET_V = RET_HEADS * RET_DV
ML_QK = ML_HEADS * ML_DK
ML_V = ML_HEADS * ML_DV
OFF_RQ = 0
OFF_RK = OFF_RQ + RET_QK
OFF_RV = OFF_RK + RET_QK
OFF_RG = OFF_RV + RET_V
OFF_MQK = OFF_RG + RET_V
OFF_MV = OFF_MQK + 2 * ML_QK
OFF_MO = OFF_MV + ML_V
MAIN_WIDTH = OFF_MO + ML_V
N_GATES = 2 * ML_HEADS

ROUTE_ROWS = 40
TOKEN_TILE = 1024
MOE_ROWS = 512
DISPATCH_TILE = 1024
COMBINE_TILE = 512
CONV_ROWS = 128
COMBINE_EYE = 512
MIXER_BATCHES = 8
ATTN_GROUPS = 2
VMEM_LIMIT = 56 * 1024 * 1024


def _dot(a, b):
    return jnp.dot(a, b, preferred_element_type=F32)


def _dot_nt(a, b):
    return lax.dot_general(a, b, (((1,), (1,)), ((), ())), preferred_element_type=F32)


def _dot_tn(a, b):
    return lax.dot_general(a, b, (((0,), (0,)), ((), ())), preferred_element_type=F32)


def _rms(x, w):
    return x * lax.rsqrt(jnp.mean(x * x, axis=-1, keepdims=True) + EPS) * w


def _sigmoid(x):
    return 1.0 / (1.0 + jnp.exp(-x))


def _silu(x):
    return x * _sigmoid(x)


def _log_sigmoid(x):
    return jnp.minimum(x, 0.0) - jnp.log1p(jnp.exp(-jnp.abs(x)))


def _head_norms(ts, mean_w):
    mu = [_dot(t.astype(BF16), mean_w) for t in ts]
    dl = [t - m for t, m in zip(ts, mu)]
    var = [_dot((d * d).astype(BF16), mean_w) for d in dl]
    return [d * lax.rsqrt(v + EPS) for d, v in zip(dl, var)]


def _pack_rows(lo, hi):
    def bits(t):
        return lax.bitcast_convert_type(t.astype(BF16), jnp.uint16).astype(jnp.uint32)
    return bits(lo) | (bits(hi) << 16)


def _unpack_rows(u):
    lo = lax.bitcast_convert_type(u << 16, F32)
    hi = lax.bitcast_convert_type(u & jnp.uint32(0xFFFF0000), F32)
    return lo, hi


def _inproj_kernel(tiles_per_seq, x_ref, nw_ref, w_ref, wif_ref, wift_ref, cos_ref, sin_ref, convw_ref, convb_ref,
                   proj_ref, g_ref, gt_ref, carry_ref):
    tm = x_ref.shape[0]

    @pl.when(pl.program_id(0) == 0)
    def _():
        carry_ref[...] = jnp.zeros_like(carry_ref)

    h = _rms(x_ref[...], nw_ref[...]).astype(BF16)

    def mm(off, width):
        return _dot(h, w_ref[:, off:off + width])

    def rotary(qk):
        cos = cos_ref[...]
        sin = sin_ref[...]
        half = RET_QK // 2
        for off, scale in ((OFF_RQ, None), (OFF_RK, RET_DK ** -0.5)):
            t1 = qk[:, off - OFF_RQ:off - OFF_RQ + half]
            t2 = qk[:, off - OFF_RQ + half:off - OFF_RQ + 2 * half]
            r1 = t1 * cos - t2 * sin
            r2 = t1 * sin + t2 * cos
            if scale is not None:
                r1, r2 = r1 * scale, r2 * scale
            proj_ref[:, off:off + half] = r1.astype(BF16)
            proj_ref[:, off + half:off + 2 * half] = r2.astype(BF16)

    def conv_silu(part, scale):
        c0 = part * ML_QK
        first = (pl.program_id(0) % tiles_per_seq) == 0
        row8 = lax.broadcasted_iota(jnp.int32, (8, ML_QK), 0)
        prev = jnp.where(first, 0.0, carry_ref[:, c0:c0 + ML_QK])
        for r0 in range(0, tm, CONV_ROWS):
            cur = _dot(h[r0:r0 + CONV_ROWS], w_ref[:, OFF_MQK + c0:OFF_MQK + c0 + ML_QK])
            acc = cur * convw_ref[CONV_W - 1:CONV_W, c0:c0 + ML_QK] + convb_ref[:, c0:c0 + ML_QK]
            for s in range(1, CONV_W):
                rolled = pltpu.roll(cur, s, 0)
                head8 = jnp.where(row8 < s, pltpu.roll(prev, s, 0), rolled[0:8])
                shifted = jnp.concatenate([head8, rolled[8:]], axis=0)
                acc = acc + shifted * convw_ref[CONV_W - 1 - s:CONV_W - s, c0:c0 + ML_QK]
            prev = cur[CONV_ROWS - 8:CONV_ROWS]
            act = _silu(acc) if scale is None else _silu(acc) * scale
            proj_ref[r0:r0 + CONV_ROWS, OFF_MQK + c0:OFF_MQK + c0 + ML_QK] = act.astype(BF16)
        carry_ref[:, c0:c0 + ML_QK] = prev

    def store(off, width, fn=None):
        def ep(t):
            proj_ref[:, off:off + width] = (t if fn is None else fn(t)).astype(BF16)
        return ep

    rotary(mm(OFF_RQ, 2 * RET_QK))
    store(OFF_RV, RET_V)(mm(OFF_RV, RET_V))
    store(OFF_RG, RET_V, _silu)(mm(OFF_RG, RET_V))
    conv_silu(0, None)
    conv_silu(1, ML_DK ** -0.5)
    store(OFF_MV, ML_V)(mm(OFF_MV, ML_V))
    store(OFF_MO, ML_V, _sigmoid)(mm(OFF_MO, ML_V))
    g_ref[...] = _dot(h, wif_ref[...])
    gt_ref[...] = _dot_nt(wift_ref[...], h)


def _inproj(xf, norm_w, w_main, w_if, w_ift, cos, sin, conv_w, conv_b, seq):
    n, d = xf.shape
    tm = TOKEN_TILE
    tiles_per_seq = seq // tm
    return pl.pallas_call(
        functools.partial(_inproj_kernel, tiles_per_seq),
        grid=(n // tm,),
        in_specs=[
            pl.BlockSpec((tm, d), lambda i: (i, 0)),
            pl.BlockSpec((1, d), lambda i: (0, 0)),
            pl.BlockSpec((d, MAIN_WIDTH), lambda i: (0, 0)),
            pl.BlockSpec((d, N_GATES), lambda i: (0, 0)),
            pl.BlockSpec((N_GATES, d), lambda i: (0, 0)),
            pl.BlockSpec((tm, RET_QK // 2), lambda i: (i % tiles_per_seq, 0)),
            pl.BlockSpec((tm, RET_QK // 2), lambda i: (i % tiles_per_seq, 0)),
            pl.BlockSpec((CONV_W, 2 * ML_QK), lambda i: (0, 0)),
            pl.BlockSpec((1, 2 * ML_QK), lambda i: (0, 0)),
        ],
        out_specs=[
            pl.BlockSpec((tm, MAIN_WIDTH), lambda i: (i, 0)),
            pl.BlockSpec((tm, N_GATES), lambda i: (i, 0)),
            pl.BlockSpec((N_GATES, tm), lambda i: (0, i)),
        ],
        out_shape=[
            jax.ShapeDtypeStruct((n, MAIN_WIDTH), BF16),
            jax.ShapeDtypeStruct((n, N_GATES), F32),
            jax.ShapeDtypeStruct((N_GATES, n), F32),
        ],
        scratch_shapes=[pltpu.VMEM((8, 2 * ML_QK), F32)],
        compiler_params=pltpu.CompilerParams(
            dimension_semantics=("arbitrary",), vmem_limit_bytes=VMEM_LIMIT),
        name="inproj",
    )(xf, norm_w, w_main, w_if, w_ift, cos, sin, conv_w, conv_b)


def _mixer_kernel(*refs):
    @pl.when(pl.program_id(1) == 0)
    def _():
        for state_ref in refs[-4:]:
            state_ref[...] = jnp.zeros_like(state_ref)

    for bi in range(refs[0].shape[0]):
        _mixer_one(bi, *refs)


def _mixer_one(bi, proj_ref, g_ref, gt_ref, qdec_ref, kdec_ref, dmat_ref,
               bmask_ref, cdec_ref, hmask_ref, tril_ref, triu_ref, ones_ref, retw_ref, mlw_ref, gbc_ref, gbr_ref,
               out_ref, r_ref, c_ref, n_ref, m_ref):
    L = CHUNK
    proj_ref, g_ref, gt_ref, out_ref = proj_ref.at[bi], g_ref.at[bi], gt_ref.at[bi], out_ref.at[bi]
    r_ref, c_ref, n_ref, m_ref = r_ref.at[bi], c_ref.at[bi], n_ref.at[bi], m_ref.at[bi]
    mean_w = ones_ref[...]

    q = proj_ref[:, OFF_RQ:OFF_RQ + RET_QK].astype(F32)
    k_b = proj_ref[:, OFF_RK:OFF_RK + RET_QK]
    k = k_b.astype(F32)
    v = proj_ref[:, OFF_RV:OFF_RV + RET_V]
    r_prev = r_ref[...]
    cross = _dot((q * qdec_ref[...]).astype(BF16), r_prev.astype(BF16))
    kv = _dot_tn((k * kdec_ref[...]).astype(BF16), v) * bmask_ref[...]
    r_ref[...] = cdec_ref[...] * r_prev + kv
    RH = range(RET_HEADS)
    sc = [_dot_nt((q * hmask_ref[h:h + 1, :]).astype(BF16), k_b) for h in RH]
    sc = [(sc[h] * dmat_ref[h]).astype(BF16) for h in RH]
    tot = [_dot(sc[h], v[:, h * RET_DV:(h + 1) * RET_DV]) + cross[:, h * RET_DV:(h + 1) * RET_DV] for h in RH]
    ret = jnp.concatenate(_head_norms(tot, mean_w), axis=1) * retw_ref[...]
    ret = ret * proj_ref[:, OFF_RG:OFF_RG + RET_V].astype(F32)
    out_ref[:, 0:RET_V] = ret.astype(BF16)

    mq = proj_ref[:, OFF_MQK:OFF_MQK + ML_QK]
    mk = proj_ref[:, OFF_MQK + ML_QK:OFF_MQK + 2 * ML_QK]
    mv = proj_ref[:, OFF_MV:OFF_MV + ML_V]

    gc = g_ref[...] + gbc_ref[...]
    gr = gt_ref[...] + gbr_ref[...]
    lf_c = _log_sigmoid(gc)
    lf_r = _log_sigmoid(gr)
    lf_c_hi = lf_c.astype(BF16)
    lf_r_hi = lf_r.astype(BF16)
    b_c = (_dot(tril_ref[...], lf_c_hi)
           + _dot(tril_ref[...], (lf_c - lf_c_hi.astype(F32)).astype(BF16)))
    b_r = (_dot(lf_r_hi, triu_ref[...])
           + _dot((lf_r - lf_r_hi.astype(F32)).astype(BF16), triu_ref[...]))
    causal = (lax.broadcasted_iota(jnp.int32, (L, L), 0) >= lax.broadcasted_iota(jnp.int32, (L, L), 1))
    MH = range(ML_HEADS)
    bc = [b_c[:, ML_HEADS + h:ML_HEADS + h + 1] for h in MH]
    br = [b_r[ML_HEADS + h:ML_HEADS + h + 1, :] for h in MH]
    igc = [gc[:, h:h + 1] for h in MH]
    igr = [gr[h:h + 1, :] for h in MH]
    btot = [br[h][:, L - 1:L] for h in MH]
    qh_b = [mq[:, h * ML_DK:(h + 1) * ML_DK] for h in MH]
    kh_b = [mk[:, h * ML_DK:(h + 1) * ML_DK] for h in MH]
    vh = [mv[:, h * ML_DV:(h + 1) * ML_DV] for h in MH]
    c_prev = [c_ref[h] for h in MH]
    n_prev = [n_ref[h][0:1, :] for h in MH]
    m_prev = [m_ref[h][0:1, 0:1] for h in MH]
    s_raw = [_dot_nt(qh_b[h], kh_b[h]) for h in MH]
    qc = [_dot(qh_b[h], c_prev[h].astype(BF16)) for h in MH]
    log_d = [jnp.where(causal, bc[h] - br[h] + igr[h], -jnp.inf) for h in MH]
    m_intra = [jnp.max(log_d[h], axis=1, keepdims=True) for h in MH]
    m_loc = [jnp.max(btot[h] - br[h] + igr[h], axis=1, keepdims=True) for h in MH]
    kw = [kh_b[h].astype(F32) * jnp.exp(btot[h] - bc[h] + igc[h] - m_loc[h]) for h in MH]
    kv_loc = [_dot_tn(kw[h].astype(BF16), vh[h]) for h in MH]
    n_loc = [jnp.sum(kw[h], axis=0, keepdims=True) for h in MH]
    m_inter = [bc[h] + m_prev[h] for h in MH]
    m_t = [jnp.maximum(m_intra[h], m_inter[h]) for h in MH]
    s_mat = [s_raw[h] * jnp.exp(log_d[h] - m_t[h]) for h in MH]
    inter = [jnp.exp(m_inter[h] - m_t[h]) for h in MH]
    num = [_dot(s_mat[h].astype(BF16), vh[h]) + inter[h] * qc[h] for h in MH]
    den = [jnp.sum(s_mat[h], axis=1, keepdims=True)
           + inter[h] * jnp.sum(qh_b[h].astype(F32) * n_prev[h], axis=1, keepdims=True) for h in MH]
    hh = [num[h] / jnp.maximum(jnp.abs(den[h]), jnp.exp(-m_t[h])) for h in MH]
    for h in MH:
        m_new = jnp.maximum(btot[h] + m_prev[h], m_loc[h])
        s_old = jnp.exp(btot[h] + m_prev[h] - m_new)
        s_loc = jnp.exp(m_loc[h] - m_new)
        c_ref[h] = s_old * c_prev[h] + s_loc * kv_loc[h]
        n_ref[h] = jnp.broadcast_to(s_old * n_prev[h] + s_loc * n_loc[h], (8, ML_DK))
        m_ref[h] = jnp.broadcast_to(m_new, (8, 128))
    ml = jnp.concatenate(_head_norms(hh, mean_w), axis=1) * mlw_ref[...]
    ml = ml * proj_ref[:, OFF_MO:OFF_MO + ML_V].astype(F32)
    out_ref[:, RET_V:RET_V + ML_V] = ml.astype(BF16)


def _mixer_tables(seq):
    L = CHUNK
    half = RET_DK // 2
    inv = ROPE_BASE ** (-np.arange(half, dtype=np.float64) / half)
    ang = np.arange(seq, dtype=np.float64)[:, None] * inv[None, :].astype(np.float32).astype(np.float64)
    cos = np.tile(np.cos(ang), (1, RET_HEADS)).astype(np.float32)
    sin = np.tile(np.sin(ang), (1, RET_HEADS)).astype(np.float32)
    log_g = np.log1p(-np.exp2(-5.0 - np.arange(RET_HEADS, dtype=np.float64)))
    n = np.arange(L, dtype=np.float64)
    lane_head = (np.arange(RET_QK) % (RET_QK // 2)) // half
    qdec = np.exp((n + 1)[:, None] * log_g[lane_head][None, :]).astype(np.float32)
    kdec = np.exp((L - 1 - n)[:, None] * log_g[lane_head][None, :]).astype(np.float32)
    diff = n[:, None] - n[None, :]
    dmat = np.where(diff >= 0, np.exp(log_g[:, None, None] * np.maximum(diff, 0.0)[None]), 0.0).astype(np.float32)
    col_head = np.arange(RET_V) // RET_DV
    bmask = (lane_head[:, None] == col_head[None, :]).astype(np.float32)
    cdec = np.exp(L * log_g[col_head])[None, :].astype(np.float32)
    hmask = (lane_head[None, :] == np.arange(RET_HEADS)[:, None]).astype(np.float32)
    hmask = np.concatenate([hmask, np.zeros((8 - RET_HEADS, RET_QK), np.float32)], axis=0)
    tril = np.tril(np.ones((L, L), np.float32))
    ones = np.full((RET_DV, RET_DV), 1.0 / RET_DV, np.float32)
    return dict(cos=cos, sin=sin, qdec=qdec, kdec=kdec, dmat=dmat, bmask=bmask, cdec=cdec, hmask=hmask,
                tril=tril, triu=np.ascontiguousarray(tril.T), ones=ones)


def _mixer(proj, g, gt, tabs, ret_norm_w, ml_norm_w, gate_b, batch, seq):
    L = CHUNK
    nc = seq // L
    n = batch * seq
    nb = MIXER_BATCHES if batch % MIXER_BATCHES == 0 else 1
    proj = proj.reshape(batch, seq, MAIN_WIDTH)
    g = g.reshape(batch, seq, N_GATES)
    gt = gt.reshape(N_GATES, batch, seq).transpose(1, 0, 2)
    const2 = lambda b, c: (0, 0)
    const3 = lambda b, c: (0, 0, 0)
    tok = lambda b, c: (b, c, 0)
    in_specs = [
        pl.BlockSpec((nb, L, MAIN_WIDTH), tok),
        pl.BlockSpec((nb, L, N_GATES), tok),
        pl.BlockSpec((nb, N_GATES, L), lambda b, c: (b, 0, c)),
        pl.BlockSpec((L, RET_QK), const2),
        pl.BlockSpec((L, RET_QK), const2),
        pl.BlockSpec((RET_HEADS, L, L), const3),
        pl.BlockSpec((RET_QK, RET_V), const2),
        pl.BlockSpec((1, RET_V), const2),
        pl.BlockSpec((8, RET_QK), const2),
        pl.BlockSpec((L, L), const2),
        pl.BlockSpec((L, L), const2),
        pl.BlockSpec((RET_DV, RET_DV), const2),
        pl.BlockSpec((1, RET_V), const2),
        pl.BlockSpec((1, ML_V), const2),
        pl.BlockSpec((1, N_GATES), const2),
        pl.BlockSpec((N_GATES, 1), const2),
    ]
    return pl.pallas_call(
        _mixer_kernel,
        grid=(batch // nb, nc),
        in_specs=in_specs,
        out_specs=pl.BlockSpec((nb, L, RET_V + ML_V), tok),
        out_shape=jax.ShapeDtypeStruct((batch, seq, RET_V + ML_V), BF16),
        scratch_shapes=[
            pltpu.VMEM((nb, RET_QK, RET_V), F32),
            pltpu.VMEM((nb, ML_HEADS, ML_DK, ML_DV), F32),
            pltpu.VMEM((nb, ML_HEADS, 8, ML_DK), F32),
            pltpu.VMEM((nb, ML_HEADS, 8, 128), F32),
        ],
        compiler_params=pltpu.CompilerParams(
            dimension_semantics=("arbitrary", "arbitrary"), vmem_limit_bytes=VMEM_LIMIT),
        name="mixer",
    )(proj, g, gt, tabs["qdec"], tabs["kdec"], tabs["dmat"], tabs["bmask"],
      tabs["cdec"], tabs["hmask"], tabs["tril"].astype(BF16), tabs["triu"].astype(BF16), tabs["ones"].astype(BF16),
      ret_norm_w, ml_norm_w,
      gate_b.reshape(1, N_GATES), gate_b.reshape(N_GATES, 1)).reshape(n, RET_V + ML_V)


def _memkv_kernel(mem_ref, nw_ref, wkv_ref, k_ref, v_ref):
    d = mem_ref.shape[-1]
    mn = _rms(mem_ref[0], nw_ref[...]).astype(BF16)
    k_ref[0] = _dot(mn, wkv_ref[:, :d]).astype(BF16)
    v_ref[0] = _dot(mn, wkv_ref[:, d:]).astype(BF16)


def _memkv(mem, norm_w, wkv):
    b, m, d = mem.shape
    return pl.pallas_call(
        _memkv_kernel,
        grid=(b,),
        in_specs=[
            pl.BlockSpec((1, m, d), lambda i: (i, 0, 0)),
            pl.BlockSpec((1, d), lambda i: (0, 0)),
            pl.BlockSpec((d, 2 * d), lambda i: (0, 0)),
        ],
        out_specs=[pl.BlockSpec((1, m, d), lambda i: (i, 0, 0))] * 2,
        out_shape=[jax.ShapeDtypeStruct((b, m, d), BF16)] * 2,
        compiler_params=pltpu.CompilerParams(
            dimension_semantics=("arbitrary",), vmem_limit_bytes=VMEM_LIMIT),
        name="memkv",
    )(mem, norm_w, wkv)


def _attn_route_kernel(x_ref, mix_ref, k_ref, v_ref, wout_ref, nxa_ref, wq_ref, wo_ref, nmoe_ref,
                       wr_ref, wrlo_ref, br_ref, sut_ref,
                       x2_ref, h3_ref, ri_ref, rw_ref, cnt_ref, carry_ref):
    tm, d = x_ref.shape
    dh = d // XA_HEADS

    @pl.when((pl.program_id(0) == 0) & (pl.program_id(1) == 0))
    def _():
        carry_ref[...] = jnp.zeros_like(carry_ref)

    groups = [slice(g * (tm // ATTN_GROUPS), (g + 1) * (tm // ATTN_GROUPS)) for g in range(ATTN_GROUPS)]
    x1 = [x_ref[s, :] + _dot(mix_ref[s, :], wout_ref[...]) for s in groups]
    h2 = [_rms(t, nxa_ref[...]).astype(BF16) for t in x1]
    q = [_dot(t, wq_ref[...]).astype(BF16) for t in h2]
    o = []
    for qg in q:
        heads = []
        for h in range(XA_HEADS):
            logits = _dot_nt(qg[:, h * dh:(h + 1) * dh], k_ref[0, :, h * dh:(h + 1) * dh]) * (dh ** -0.5)
            mx = jnp.max(logits, axis=-1, keepdims=True)
            e = jnp.exp(logits - mx)
            p = (e / jnp.sum(e, axis=-1, keepdims=True)).astype(BF16)
            heads.append(_dot(p, v_ref[0, :, h * dh:(h + 1) * dh]).astype(BF16))
        o.append(jnp.concatenate(heads, axis=1))
    x2 = [a + _dot(b, wo_ref[...]) for a, b in zip(x1, o)]
    for s, t in zip(groups, x2):
        x2_ref[s, :] = t
    h3 = [_rms(t, nmoe_ref[...]) for t in x2]
    for s, t in zip(groups, h3):
        h3_ref[s, 0, :] = _pack_rows(t[:, :d // 2], t[:, d // 2:])

    lts = []
    for t in h3:
        t_hi = t.astype(BF16)
        t_lo = (t - t_hi.astype(F32)).astype(BF16)
        lts.append(_dot_nt(wr_ref[...], t_hi) + (_dot_nt(wr_ref[...], t_lo) + _dot_nt(wrlo_ref[...], t_hi)))
    lt = jnp.concatenate(lts, axis=1) + br_ref[...]
    gl = lt[N_EXPERTS:N_EXPERTS + N_GROUPS]
    gmax = jnp.max(gl, axis=0, keepdims=True)
    g_w = 1.0 / jnp.sum(jnp.exp(gl - gmax), axis=0, keepdims=True)
    giota = lax.broadcasted_iota(jnp.int32, gl.shape, 0)
    g_sel = jnp.min(jnp.where(gl == gmax, giota, N_GROUPS), axis=0, keepdims=True)
    el = lt[0:N_EXPERTS]
    eiota = lax.broadcasted_iota(jnp.int32, el.shape, 0)
    in_grp = (eiota // EXP_PER_GROUP) == g_sel
    elm = jnp.where(in_grp, el, -jnp.inf)
    m1 = jnp.max(elm, axis=0, keepdims=True)
    esum = jnp.sum(jnp.where(in_grp, jnp.exp(el - m1), 0.0), axis=0, keepdims=True)
    i1 = jnp.min(jnp.where(elm == m1, eiota, N_EXPERTS), axis=0, keepdims=True)
    elm2 = jnp.where(eiota == i1, -jnp.inf, elm)
    m2 = jnp.max(elm2, axis=0, keepdims=True)
    i2 = jnp.min(jnp.where(elm2 == m2, eiota, N_EXPERTS), axis=0, keepdims=True)
    p1 = 1.0 / esum
    p2 = jnp.exp(m2 - m1) / esum
    psum = p1 + p2
    w1 = g_w * (p1 / psum)
    w2 = g_w * (p2 / psum)

    oh1 = (eiota == i1).astype(F32)
    oh2 = (eiota == i2).astype(F32)
    cnt = oh1 + oh2
    base = carry_ref[:, 0:1] + _dot(cnt.astype(BF16), sut_ref[...])
    r1 = jnp.sum(oh1 * base, axis=0, keepdims=True)
    r2 = jnp.sum(oh2 * base, axis=0, keepdims=True)
    new_carry = carry_ref[...] + jnp.sum(cnt, axis=1, keepdims=True)
    carry_ref[...] = new_carry
    cnt_ref[...] = new_carry

    zi = jnp.zeros((4, tm), jnp.int32)
    ri_ref[...] = jnp.concatenate([i1, i2, r1.astype(jnp.int32), r2.astype(jnp.int32), zi], axis=0)
    rw_ref[...] = jnp.concatenate([w1, w2, jnp.zeros((6, tm), F32)], axis=0)


def _attn_route(xf, mixed, kmem, vmem, w_out, norm_xa_w, wq, wo, norm_moe_w, w_route_t, b_route, sut,
                batch, seq):
    n, d = xf.shape
    w_route_hi = w_route_t.astype(BF16)
    w_route_lo = (w_route_t - w_route_hi.astype(F32)).astype(BF16)
    tm = TOKEN_TILE
    nt = seq // tm
    m = kmem.shape[1]
    tok = lambda b, t: (b * nt + t, 0)
    lane_tok = lambda b, t: (0, b * nt + t)
    const2 = lambda b, t: (0, 0)
    return pl.pallas_call(
        _attn_route_kernel,
        grid=(batch, nt),
        in_specs=[
            pl.BlockSpec((tm, d), tok),
            pl.BlockSpec((tm, d), tok),
            pl.BlockSpec((1, m, d), lambda b, t: (b, 0, 0)),
            pl.BlockSpec((1, m, d), lambda b, t: (b, 0, 0)),
            pl.BlockSpec((d, d), const2),
            pl.BlockSpec((1, d), const2),
            pl.BlockSpec((d, d), const2),
            pl.BlockSpec((d, d), const2),
            pl.BlockSpec((1, d), const2),
            pl.BlockSpec((ROUTE_ROWS, d), const2),
            pl.BlockSpec((ROUTE_ROWS, d), const2),
            pl.BlockSpec((ROUTE_ROWS, 1), const2),
            pl.BlockSpec((tm, tm), const2),
        ],
        out_specs=[
            pl.BlockSpec((tm, d), tok),
            pl.BlockSpec((tm, 1, d // 2), lambda b, t: (b * nt + t, 0, 0)),
            pl.BlockSpec((8, tm), lane_tok),
            pl.BlockSpec((8, tm), lane_tok),
            pl.BlockSpec((N_EXPERTS, 128), const2),
        ],
        out_shape=[
            jax.ShapeDtypeStruct((n, d), F32),
            jax.ShapeDtypeStruct((n, 1, d // 2), jnp.uint32),
            jax.ShapeDtypeStruct((8, n), jnp.int32),
            jax.ShapeDtypeStruct((8, n), F32),
            jax.ShapeDtypeStruct((N_EXPERTS, 128), F32),
        ],
        scratch_shapes=[pltpu.VMEM((N_EXPERTS, 128), F32)],
        compiler_params=pltpu.CompilerParams(
            dimension_semantics=("arbitrary", "arbitrary"), vmem_limit_bytes=VMEM_LIMIT),
        name="attn_route",
    )(xf, mixed, kmem, vmem, w_out, norm_xa_w, wq, wo, norm_moe_w, w_route_hi, w_route_lo, b_route, sut)


def _dispatch_kernel(zpos_ref, dest_ref, h_ref, xs_ref, idx_ref, idx_sem, row_sem, zero_ref, zero_sem):
    i = pl.program_id(0)
    nsteps = pl.num_programs(0)
    td = h_ref.shape[0]
    bm = zero_ref.shape[0]
    slot = i % 2

    def idx_copy(step, sl):
        off = pl.multiple_of(sl * (2 * td), 2 * td)
        return pltpu.make_async_copy(dest_ref.at[step], idx_ref.at[pl.ds(off, 2 * td)], idx_sem.at[sl])

    @pl.when(i == 0)
    def _():
        zero_ref[...] = jnp.zeros_like(zero_ref)

        def zero_copy(e):
            return pltpu.make_async_copy(zero_ref, xs_ref.at[pl.ds(pl.multiple_of(zpos_ref[e], bm), bm), 0], zero_sem)

        def tail_copy(b):
            return pltpu.make_async_copy(zero_ref, xs_ref.at[pl.ds(pl.multiple_of(b * bm, bm), bm), 0], zero_sem)

        nused = zpos_ref[N_EXPERTS]
        nblk = xs_ref.shape[0] // bm
        for e in range(N_EXPERTS):
            pl.when(zpos_ref[e] >= 0)(lambda e=e: zero_copy(e).start())
        lax.fori_loop(nused, nblk, lambda b, c: (tail_copy(b).start(), c)[1], 0)
        for e in range(N_EXPERTS):
            pl.when(zpos_ref[e] >= 0)(lambda e=e: zero_copy(e).wait())
        lax.fori_loop(nused, nblk, lambda b, c: (tail_copy(b).wait(), c)[1], 0)
        idx_copy(0, 0).start()

    idx_copy(i, slot).wait()

    @pl.when(i + 1 < nsteps)
    def _():
        idx_copy(i + 1, 1 - slot).start()

    base = slot * (2 * td)

    for t in range(td):
        for k in range(TOP_K):
            pltpu.make_async_copy(h_ref.at[t], xs_ref.at[idx_ref[base + k * td + t]], row_sem).start(priority=t % 2)
    for _ in range(TOP_K):
        pltpu.make_async_copy(xs_ref.at[pl.ds(0, td)], xs_ref.at[pl.ds(0, td)], row_sem).wait()


def _dispatch(h3p, dest_tiles, zpos, cap):
    w = h3p.shape[-1]
    nt, td2 = dest_tiles.shape
    td = td2 // 2
    grid_spec = pltpu.PrefetchScalarGridSpec(
        num_scalar_prefetch=1,
        grid=(nt,),
        in_specs=[
            pl.BlockSpec(memory_space=pl.ANY),
            pl.BlockSpec((td, 1, w), lambda i, zp: (i, 0, 0)),
        ],
        out_specs=pl.BlockSpec(memory_space=pl.ANY),
        scratch_shapes=[
            pltpu.SMEM((2 * td2,), jnp.int32),
            pltpu.SemaphoreType.DMA((2,)),
            pltpu.SemaphoreType.DMA,
            pltpu.VMEM((MOE_ROWS, w), jnp.uint32),
            pltpu.SemaphoreType.DMA,
        ],
    )
    return pl.pallas_call(
        _dispatch_kernel,
        grid_spec=grid_spec,
        out_shape=jax.ShapeDtypeStruct((cap, 1, w), jnp.uint32),
        compiler_params=pltpu.CompilerParams(
            dimension_semantics=("arbitrary",), vmem_limit_bytes=VMEM_LIMIT),
        name="dispatch",
    )(zpos, dest_tiles, h3p)


def _expert_kernel(blk_e_ref, nused_ref, next_e_ref, xs_ref, wg_ref, wu_ref, wd_ref, ys_ref, wg_b, wu_b, wd_b,
                   wg_s, wu_s, wd_s, xbuf, ybuf, zbuf, in_sem, out_sem, zero_sem, w_sem):
    i = pl.program_id(0)
    nsteps = pl.num_programs(0)
    nused = nused_ref[0]
    bm = xbuf.shape[1]
    slot = i % 2
    prev = blk_e_ref[jnp.maximum(i - 1, 0)]
    fresh = (i < nused) & ((i == 0) | (blk_e_ref[i] != prev))
    half = wd_b.shape[1] // 2

    def w_copies(e):
        return [pltpu.make_async_copy(src.at[e], dst, w_sem.at[j])
                for j, (src, dst) in enumerate(((wg_ref, wg_s), (wu_ref, wu_s), (wd_ref, wd_s)))]

    def rows(ref, step):
        return ref.at[pl.ds(pl.multiple_of(step * bm, bm), bm), 0]

    def in_copy(step, sl):
        return pltpu.make_async_copy(rows(xs_ref, step), xbuf.at[sl], in_sem.at[sl])

    def out_copy(step, sl):
        return pltpu.make_async_copy(ybuf.at[sl], rows(ys_ref, step), out_sem.at[sl])

    def zero_copy(step):
        return pltpu.make_async_copy(zbuf, rows(ys_ref, step), zero_sem)

    @pl.when(i == 0)
    def _():
        zbuf[...] = jnp.zeros_like(zbuf)
        in_copy(0, 0).start()
        for c in w_copies(blk_e_ref[0]):
            c.start()

    @pl.when(i + 1 < nused)
    def _():
        in_copy(i + 1, 1 - slot).start()

    @pl.when(fresh)
    def _():
        for c in w_copies(blk_e_ref[i]):
            c.wait()
        wg_b[...] = wg_s[...].astype(BF16)
        wu_b[...] = wu_s[...].astype(BF16)
        wd_b[...] = wd_s[...].astype(BF16)

        @pl.when(next_e_ref[i] >= 0)
        def _():
            for c in w_copies(next_e_ref[i]):
                c.start()

    @pl.when(i < nused)
    def _():
        in_copy(i, slot).wait()
        pl.when(i >= 2)(lambda: out_copy(i - 2, slot).wait())
        lo, hi = _unpack_rows(xbuf[slot])
        xb = jnp.concatenate([lo.astype(BF16), hi.astype(BF16)], axis=1)
        hid = (_silu(_dot(xb, wg_b[...])) * _dot(xb, wu_b[...])).astype(BF16)
        y = _dot(hid, wd_b[...])
        ybuf[slot] = _pack_rows(y[:, :half], y[:, half:])
        out_copy(i, slot).start(priority=1)

    pl.when(i >= nused)(lambda: zero_copy(i).start())

    @pl.when(i == nsteps - 1)
    def _():
        pl.when(nused >= 2)(lambda: out_copy(nused - 2, nused % 2).wait())
        pl.when(nused >= 1)(lambda: out_copy(nused - 1, (nused - 1) % 2).wait())
        lax.fori_loop(nused, nsteps, lambda b, c: (zero_copy(b).wait(), c)[1], 0)


def _experts(xs, blk_e, nused, next_e, w_gate, w_up, w_down):
    cap, _, w = xs.shape
    _, d, de = w_gate.shape
    bm = MOE_ROWS
    grid_spec = pltpu.PrefetchScalarGridSpec(
        num_scalar_prefetch=3,
        grid=(cap // bm,),
        in_specs=[pl.BlockSpec(memory_space=pl.ANY)] * 4,
        out_specs=pl.BlockSpec(memory_space=pl.ANY),
        scratch_shapes=[
            pltpu.VMEM((d, de), BF16),
            pltpu.VMEM((d, de), BF16),
            pltpu.VMEM((de, d), BF16),
            pltpu.VMEM((d, de), F32),
            pltpu.VMEM((d, de), F32),
            pltpu.VMEM((de, d), F32),
            pltpu.VMEM((2, bm, w), jnp.uint32),
            pltpu.VMEM((2, bm, w), jnp.uint32),
            pltpu.VMEM((bm, w), jnp.uint32),
            pltpu.SemaphoreType.DMA((2,)),
            pltpu.SemaphoreType.DMA((2,)),
            pltpu.SemaphoreType.DMA,
            pltpu.SemaphoreType.DMA((3,)),
        ],
    )
    return pl.pallas_call(
        _expert_kernel,
        grid_spec=grid_spec,
        out_shape=jax.ShapeDtypeStruct((cap, 1, w), jnp.uint32),
        compiler_params=pltpu.CompilerParams(
            dimension_semantics=("arbitrary",), vmem_limit_bytes=VMEM_LIMIT),
        name="experts",
    )(blk_e, nused, next_e, xs, w_gate, w_up, w_down)


def _combine_kernel(dest_ref, ys_ref, x2_ref, rw_ref, eye_ref, nw_ref, o_ref, idx_ref, idx_sem, ybuf, ysem):
    i = pl.program_id(0)
    nsteps = pl.num_programs(0)
    tc, d = x2_ref.shape
    half = d // 2
    n_idx = 2 * tc

    def idx_copy(step):
        sl = step % 3
        off = pl.multiple_of(sl * n_idx, n_idx)
        return pltpu.make_async_copy(dest_ref.at[step], idx_ref.at[pl.ds(off, n_idx)], idx_sem.at[sl])

    def gather(step):
        base = (step % 3) * n_idx
        buf = ybuf.at[step % 2]
        sem = ysem.at[step % 2]

        for t in range(n_idx):
            pltpu.make_async_copy(ys_ref.at[idx_ref[base + t]], buf.at[pl.ds(t, 1)], sem).start(priority=t % 2)

    @pl.when(i == 0)
    def _():
        idx_copy(0).start()
        idx_copy(0).wait()
        gather(0)

        @pl.when(nsteps > 1)
        def _():
            idx_copy(1).start()

    @pl.when(i + 1 < nsteps)
    def _():
        idx_copy(i + 1).wait()

        @pl.when(i + 2 < nsteps)
        def _():
            idx_copy(i + 2).start()

        gather(i + 1)

    slot = i % 2
    pltpu.make_async_copy(ybuf.at[slot], ybuf.at[slot], ysem.at[slot]).wait()
    eye = eye_ref[...]
    ew = eye.shape[0]

    def to_columns(r):
        r_a = r.astype(BF16)
        r_b = (r - r_a.astype(F32)).astype(BF16)
        r_c = (r - r_a.astype(F32) - r_b.astype(F32)).astype(BF16)
        return _dot_nt(eye, r_a) + (_dot_nt(eye, r_b) + _dot_nt(eye, r_c))

    wcol = jnp.concatenate([to_columns(rw_ref[:, j:j + ew]) for j in range(0, tc, ew)], axis=0)
    lo1, hi1 = _unpack_rows(ybuf[slot, 0:tc])
    lo2, hi2 = _unpack_rows(ybuf[slot, tc:n_idx])
    w1 = wcol[:, 0:1]
    w2 = wcol[:, 1:2]
    z_lo = x2_ref[:, :half] + (lo1 * w1 + lo2 * w2)
    z_hi = x2_ref[:, half:] + (hi1 * w1 + hi2 * w2)
    ms = (jnp.sum(z_lo * z_lo, axis=-1, keepdims=True) + jnp.sum(z_hi * z_hi, axis=-1, keepdims=True)) / d
    scale = lax.rsqrt(ms + EPS)
    o_ref[:, :half] = z_lo * scale * nw_ref[:, :half]
    o_ref[:, half:] = z_hi * scale * nw_ref[:, half:]


def _combine(x2, ys, dest_tiles, rw, eye, norm_w):
    n, d = x2.shape
    nt, n_idx = dest_tiles.shape
    tc = n_idx // 2
    w = ys.shape[-1]
    return pl.pallas_call(
        _combine_kernel,
        grid=(nt,),
        in_specs=[pl.BlockSpec(memory_space=pl.ANY)] * 2 + [
            pl.BlockSpec((tc, d), lambda i: (i, 0)),
            pl.BlockSpec((8, tc), lambda i: (0, i)),
            pl.BlockSpec(eye.shape, lambda i: (0, 0)),
            pl.BlockSpec((1, d), lambda i: (0, 0)),
        ],
        out_specs=pl.BlockSpec((tc, d), lambda i: (i, 0)),
        out_shape=jax.ShapeDtypeStruct((n, d), F32),
        scratch_shapes=[
            pltpu.SMEM((3 * n_idx,), jnp.int32),
            pltpu.SemaphoreType.DMA((3,)),
            pltpu.VMEM((2, n_idx, w), jnp.uint32),
            pltpu.SemaphoreType.DMA((2,)),
        ],
        compiler_params=pltpu.CompilerParams(
            dimension_semantics=("arbitrary",), vmem_limit_bytes=VMEM_LIMIT),
        name="combine",
    )(dest_tiles, ys, x2, rw, eye, norm_w)


def _layer(xf, mem, batch, seq, norm_mix_w, w_in, ret_norm_w, ml_conv_w, ml_conv_b, ml_gate_b, ml_norm_w,
           w_out, norm_xa_w, norm_mem_w, xa_wq, xa_wkv, xa_wo, norm_moe_w, moe_w_group, moe_b_group,
           moe_w_router, moe_b_router, moe_w_gate, moe_w_up, moe_w_down, final_norm_w):
    n, d = xf.shape
    def halves_first(w):
        return w.reshape(d, RET_HEADS, 2, RET_DK // 2).transpose(0, 2, 1, 3).reshape(d, RET_QK)

    w_main = jnp.concatenate([halves_first(w_in[:, OFF_RQ:OFF_RQ + RET_QK]),
                              halves_first(w_in[:, OFF_RK:OFF_RK + RET_QK]),
                              w_in[:, OFF_RV:MAIN_WIDTH]], axis=1).astype(BF16)
    w_if = w_in[:, MAIN_WIDTH:].astype(BF16)
    tabs = {k_: jnp.asarray(v_) for k_, v_ in _mixer_tables(seq).items()}
    proj, g, gt = _inproj(xf, norm_mix_w.reshape(1, d), w_main, w_if, w_if.T, tabs["cos"], tabs["sin"], ml_conv_w,
                          ml_conv_b.reshape(1, 2 * ML_QK), seq)
    mixed = _mixer(proj, g, gt, tabs, ret_norm_w.reshape(1, RET_V), ml_norm_w.reshape(1, ML_V), ml_gate_b,
                   batch, seq)

    kmem, vmem = _memkv(mem, norm_mem_w.reshape(1, d), xa_wkv.astype(BF16))

    w_route_t = jnp.concatenate(
        [moe_w_router.T, moe_w_group.T, jnp.zeros((ROUTE_ROWS - N_EXPERTS - N_GROUPS, d), F32)], axis=0)
    b_route = jnp.concatenate(
        [moe_b_router, moe_b_group, jnp.zeros((ROUTE_ROWS - N_EXPERTS - N_GROUPS,), F32)]).reshape(ROUTE_ROWS, 1)
    tm = TOKEN_TILE
    sut = jnp.asarray(np.triu(np.ones((tm, tm), np.float32), 1), dtype=BF16)
    x2, h3, ri, rw, cnt = _attn_route(xf, mixed, kmem, vmem, w_out.astype(BF16), norm_xa_w.reshape(1, d),
                                      xa_wq.astype(BF16), xa_wo.astype(BF16), norm_moe_w.reshape(1, d),
                                      w_route_t, b_route, sut, batch, seq)

    bm = MOE_ROWS
    counts = cnt[:, 0].astype(jnp.int32)
    padded = (counts + bm - 1) // bm * bm
    pends = jnp.cumsum(padded)
    pstarts = pends - padded
    expert = ri[0:TOP_K]
    onehot = expert[None] == jnp.arange(N_EXPERTS, dtype=jnp.int32)[:, None, None]
    dest = jnp.sum(jnp.where(onehot, pstarts[:, None, None], 0), axis=0) + ri[TOP_K:2 * TOP_K]
    cap = n * TOP_K + N_EXPERTS * bm
    nblk = cap // bm
    blk_start = jnp.arange(nblk, dtype=jnp.int32) * bm
    blk_e = jnp.minimum(jnp.sum(blk_start[:, None] >= pends[None, :], axis=1), N_EXPERTS - 1).astype(jnp.int32)
    nused = (pends[-1] // bm).astype(jnp.int32).reshape(1)
    zpos = jnp.where(padded > counts, pends - bm, -1).astype(jnp.int32)
    zpos = jnp.concatenate([zpos, nused])

    def tiles(rows):
        return dest.reshape(TOP_K, n // rows, rows).transpose(1, 0, 2).reshape(n // rows, TOP_K * rows)

    eids = jnp.arange(N_EXPERTS, dtype=jnp.int32)
    later = jnp.where((eids[None, :] > eids[:, None]) & (padded[None, :] > 0), eids[None, :], N_EXPERTS)
    next_of = jnp.min(later, axis=1)
    next_of = jnp.where(next_of == N_EXPERTS, -1, next_of)
    next_e = jnp.sum(jnp.where(blk_e[:, None] == eids[None, :], next_of[None, :], 0), axis=1).astype(jnp.int32)

    xs = _dispatch(h3, tiles(DISPATCH_TILE), zpos, cap)
    ys = _experts(xs, blk_e, nused, next_e, moe_w_gate, moe_w_up, moe_w_down)
    eye = jnp.asarray(np.eye(COMBINE_EYE, dtype=np.float32), dtype=BF16)
    return _combine(x2, ys, tiles(COMBINE_TILE), rw, eye, final_norm_w.reshape(1, d))


def kernel(x, mem, norm_mix_w, w_in, ret_norm_w, ml_conv_w, ml_conv_b, ml_gate_b, ml_norm_w, w_out, norm_xa_w, norm_mem_w, xa_wq, xa_wkv, xa_wo, norm_moe_w, moe_w_group, moe_b_group, moe_w_router, moe_b_router, moe_w_gate, moe_w_up, moe_w_down, norm_final_w):
    batch, seq, d = x.shape
    depth = w_in.shape[0]
    assert depth == 1, "the final norm is fused into the last layer's combine kernel"
    l = 0
    out = _layer(x.reshape(batch * seq, d), mem, batch, seq, norm_mix_w[l], w_in[l], ret_norm_w[l], ml_conv_w[l],
                 ml_conv_b[l], ml_gate_b[l], ml_norm_w[l], w_out[l], norm_xa_w[l], norm_mem_w[l], xa_wq[l],
                 xa_wkv[l], xa_wo[l], norm_moe_w[l], moe_w_group[l], moe_b_group[l], moe_w_router[l],
                 moe_b_router[l], moe_w_gate[l], moe_w_up[l], moe_w_down[l], norm_final_w)
    return out.reshape(batch, seq, d)
```

```python
import functools

import numpy as np
import jax
import jax.numpy as jnp
from jax import lax
from jax.experimental import pallas as pl
from jax.experimental.pallas import tpu as pltpu

F32 = jnp.float32
BF16 = jnp.bfloat16

CHUNK = 128
RET_HEADS = 4
RET_DK = 64
RET_DV = 128
ML_HEADS = 4
ML_DK = 128
ML_DV = 128
CONV_W = 4
XA_HEADS = 4
N_GROUPS = 4
EXP_PER_GROUP = 8
N_EXPERTS = N_GROUPS * EXP_PER_GROUP
TOP_K = 2
ROPE_BASE = 10000.0
EPS = 1e-6

RET_QK = RET_HEADS * RET_DK
RET_V = RET_HEADS * RET_DV
ML_QK = ML_HEADS * ML_DK
ML_V = ML_HEADS * ML_DV
OFF_RQ = 0
OFF_RK = OFF_RQ + RET_QK
OFF_RV = OFF_RK + RET_QK
OFF_RG = OFF_RV + RET_V
OFF_MQK = OFF_RG + RET_V
OFF_MV = OFF_MQK + 2 * ML_QK
OFF_MO = OFF_MV + ML_V
MAIN_WIDTH = OFF_MO + ML_V
N_GATES = 2 * ML_HEADS
GATE_LANES = 128

ROUTE_ROWS = 40
TOKEN_TILE = 1024
MOE_ROWS = 512
DISPATCH_TILE = 1024
COMBINE_TILE = 512
CONV_ROWS = 128
COMBINE_EYE = 512
MIXER_BATCHES = 8
ATTN_GROUPS = 2
VMEM_LIMIT = 56 * 1024 * 1024


def _dot(a, b):
    return jnp.dot(a, b, preferred_element_type=F32)


def _dot_nt(a, b):
    return lax.dot_general(a, b, (((1,), (1,)), ((), ())), preferred_element_type=F32)


def _dot_tn(a, b):
    return lax.dot_general(a, b, (((0,), (0,)), ((), ())), preferred_element_type=F32)


def _rms(x, w):
    return x * lax.rsqrt(jnp.mean(x * x, axis=-1, keepdims=True) + EPS) * w


def _sigmoid(x):
    return 1.0 / (1.0 + jnp.exp(-x))


def _silu(x):
    return x * _sigmoid(x)


def _log_sigmoid(x):
    return jnp.minimum(x, 0.0) - jnp.log1p(jnp.exp(-jnp.abs(x)))


def _head_norms(ts, mean_w):
    mu = [_dot(t.astype(BF16), mean_w) for t in ts]
    dl = [t - m for t, m in zip(ts, mu)]
    var = [_dot((d * d).astype(BF16), mean_w) for d in dl]
    return [d * lax.rsqrt(v + EPS) for d, v in zip(dl, var)]


def _pack_rows(lo, hi):
    def bits(t):
        return lax.bitcast_convert_type(t.astype(BF16), jnp.uint16).astype(jnp.uint32)
    return bits(lo) | (bits(hi) << 16)


def _unpack_rows(u):
    lo = lax.bitcast_convert_type(u << 16, F32)
    hi = lax.bitcast_convert_type(u & jnp.uint32(0xFFFF0000), F32)
    return lo, hi


def _inproj_kernel(tiles_per_seq, x_ref, nw_ref, w_ref, wif_ref, wift_ref, cos_ref, sin_ref, convw_ref, convb_ref,
                   proj_ref, g_ref, gt_ref, carry_ref):
    tm = x_ref.shape[0]

    @pl.when(pl.program_id(0) == 0)
    def _():
        carry_ref[...] = jnp.zeros_like(carry_ref)

    h = _rms(x_ref[...], nw_ref[...]).astype(BF16)

    def mm(off, width):
        return _dot(h, w_ref[:, off:off + width])

    def rotary(qk):
        cos = cos_ref[...]
        sin = sin_ref[...]
        half = RET_QK // 2
        for off, scale in ((OFF_RQ, None), (OFF_RK, RET_DK ** -0.5)):
            t1 = qk[:, off - OFF_RQ:off - OFF_RQ + half]
            t2 = qk[:, off - OFF_RQ + half:off - OFF_RQ + 2 * half]
            r1 = t1 * cos - t2 * sin
            r2 = t1 * sin + t2 * cos
            if scale is not None:
                r1, r2 = r1 * scale, r2 * scale
            proj_ref[:, off:off + half] = r1.astype(BF16)
            proj_ref[:, off + half:off + 2 * half] = r2.astype(BF16)

    def conv_silu(part, scale):
        c0 = part * ML_QK
        first = (pl.program_id(0) % tiles_per_seq) == 0
        row8 = lax.broadcasted_iota(jnp.int32, (8, ML_QK), 0)
        prev = jnp.where(first, 0.0, carry_ref[:, c0:c0 + ML_QK])
        for r0 in range(0, tm, CONV_ROWS):
            cur = _dot(h[r0:r0 + CONV_ROWS], w_ref[:, OFF_MQK + c0:OFF_MQK + c0 + ML_QK])
            acc = cur * convw_ref[CONV_W - 1:CONV_W, c0:c0 + ML_QK] + convb_ref[:, c0:c0 + ML_QK]
            for s in range(1, CONV_W):
                rolled = pltpu.roll(cur, s, 0)
                head8 = jnp.where(row8 < s, pltpu.roll(prev, s, 0), rolled[0:8])
                shifted = jnp.concatenate([head8, rolled[8:]], axis=0)
                acc = acc + shifted * convw_ref[CONV_W - 1 - s:CONV_W - s, c0:c0 + ML_QK]
            prev = cur[CONV_ROWS - 8:CONV_ROWS]
            act = _silu(acc) if scale is None else _silu(acc) * scale
            proj_ref[r0:r0 + CONV_ROWS, OFF_MQK + c0:OFF_MQK + c0 + ML_QK] = act.astype(BF16)
        carry_ref[:, c0:c0 + ML_QK] = prev

    def store(off, width, fn=None):
        def ep(t):
            proj_ref[:, off:off + width] = (t if fn is None else fn(t)).astype(BF16)
        return ep

    rotary(mm(OFF_RQ, 2 * RET_QK))
    store(OFF_RV, RET_V)(mm(OFF_RV, RET_V))
    store(OFF_RG, RET_V, _silu)(mm(OFF_RG, RET_V))
    conv_silu(0, None)
    conv_silu(1, ML_DK ** -0.5)
    store(OFF_MV, ML_V)(mm(OFF_MV, ML_V))
    store(OFF_MO, ML_V, _sigmoid)(mm(OFF_MO, ML_V))
    g_ref[...] = _dot(h, wif_ref[...])
    gt_ref[...] = _dot_nt(wift_ref[...], h)


def _inproj(xf, norm_w, w_main, w_if, w_ift, cos, sin, conv_w, conv_b, seq):
    n, d = xf.shape
    tm = TOKEN_TILE
    tiles_per_seq = seq // tm
    return pl.pallas_call(
        functools.partial(_inproj_kernel, tiles_per_seq),
        grid=(n // tm,),
        in_specs=[
            pl.BlockSpec((tm, d), lambda i: (i, 0)),
            pl.BlockSpec((1, d), lambda i: (0, 0)),
            pl.BlockSpec((d, MAIN_WIDTH), lambda i: (0, 0)),
            pl.BlockSpec((d, GATE_LANES), lambda i: (0, 0)),
            pl.BlockSpec((N_GATES, d), lambda i: (0, 0)),
            pl.BlockSpec((tm, RET_QK // 2), lambda i: (i % tiles_per_seq, 0)),
            pl.BlockSpec((tm, RET_QK // 2), lambda i: (i % tiles_per_seq, 0)),
            pl.BlockSpec((CONV_W, 2 * ML_QK), lambda i: (0, 0)),
            pl.BlockSpec((1, 2 * ML_QK), lambda i: (0, 0)),
        ],
        out_specs=[
            pl.BlockSpec((tm, MAIN_WIDTH), lambda i: (i, 0)),
            pl.BlockSpec((tm, GATE_LANES), lambda i: (i, 0)),
            pl.BlockSpec((N_GATES, tm), lambda i: (0, i)),
        ],
        out_shape=[
            jax.ShapeDtypeStruct((n, MAIN_WIDTH), BF16),
            jax.ShapeDtypeStruct((n, GATE_LANES), F32),
            jax.ShapeDtypeStruct((N_GATES, n), F32),
        ],
        scratch_shapes=[pltpu.VMEM((8, 2 * ML_QK), F32)],
        compiler_params=pltpu.CompilerParams(
            dimension_semantics=("arbitrary",), vmem_limit_bytes=VMEM_LIMIT),
        name="inproj",
    )(xf, norm_w, w_main, w_if, w_ift, cos, sin, conv_w, conv_b)


def _mixer_kernel(*refs):
    @pl.when(pl.program_id(1) == 0)
    def _():
        for state_ref in refs[-4:]:
            state_ref[...] = jnp.zeros_like(state_ref)

    for bi in range(refs[0].shape[0]):
        _mixer_one(bi, *refs)


def _mixer_one(bi, proj_ref, g_ref, gt_ref, qdec_ref, kdec_ref, dmat_ref,
               bmask_ref, cdec_ref, hmask_ref, tril_ref, triu_ref, ones_ref, retw_ref, mlw_ref, gbc_ref, gbr_ref,
               out_ref, r_ref, c_ref, n_ref, m_ref):
    L = CHUNK
    proj_ref, g_ref, gt_ref, out_ref = proj_ref.at[bi], g_ref.at[bi], gt_ref.at[bi], out_ref.at[bi]
    r_ref, c_ref, n_ref, m_ref = r_ref.at[bi], c_ref.at[bi], n_ref.at[bi], m_ref.at[bi]
    mean_w = ones_ref[...]

    q = proj_ref[:, OFF_RQ:OFF_RQ + RET_QK].astype(F32)
    k_b = proj_ref[:, OFF_RK:OFF_RK + RET_QK]
    k = k_b.astype(F32)
    v = proj_ref[:, OFF_RV:OFF_RV + RET_V]
    r_prev = r_ref[...]
    cross = _dot((q * qdec_ref[...]).astype(BF16), r_prev.astype(BF16))
    kv = _dot_tn((k * kdec_ref[...]).astype(BF16), v) * bmask_ref[...]
    r_ref[...] = cdec_ref[...] * r_prev + kv
    RH = range(RET_HEADS)
    sc = [_dot_nt((q * hmask_ref[h:h + 1, :]).astype(BF16), k_b) for h in RH]
    sc = [(sc[h] * dmat_ref[h]).astype(BF16) for h in RH]
    tot = [_dot(sc[h], v[:, h * RET_DV:(h + 1) * RET_DV]) + cross[:, h * RET_DV:(h + 1) * RET_DV] for h in RH]
    ret = jnp.concatenate(_head_norms(tot, mean_w), axis=1) * retw_ref[...]
    ret = ret * proj_ref[:, OFF_RG:OFF_RG + RET_V].astype(F32)
    out_ref[:, 0:RET_V] = ret.astype(BF16)

    mq = proj_ref[:, OFF_MQK:OFF_MQK + ML_QK]
    mk = proj_ref[:, OFF_MQK + ML_QK:OFF_MQK + 2 * ML_QK]
    mv = proj_ref[:, OFF_MV:OFF_MV + ML_V]

    gc = g_ref[:, 0:N_GATES] + gbc_ref[...]
    gr = gt_ref[...] + gbr_ref[...]
    lf_c = _log_sigmoid(gc)
    lf_r = _log_sigmoid(gr)
    lf_c_hi = lf_c.astype(BF16)
    lf_r_hi = lf_r.astype(BF16)
    b_c = (_dot(tril_ref[...], lf_c_hi)
           + _dot(tril_ref[...], (lf_c - lf_c_hi.astype(F32)).astype(BF16)))
    b_r = (_dot(lf_r_hi, triu_ref[...])
           + _dot((lf_r - lf_r_hi.astype(F32)).astype(BF16), triu_ref[...]))
    causal = (lax.broadcasted_iota(jnp.int32, (L, L), 0) >= lax.broadcasted_iota(jnp.int32, (L, L), 1))
    MH = range(ML_HEADS)
    bc = [b_c[:, ML_HEADS + h:ML_HEADS + h + 1] for h in MH]
    br = [b_r[ML_HEADS + h:ML_HEADS + h + 1, :] for h in MH]
    igc = [gc[:, h:h + 1] for h in MH]
    igr = [gr[h:h + 1, :] for h in MH]
    btot = [br[h][:, L - 1:L] for h in MH]
    qh_b = [mq[:, h * ML_DK:(h + 1) * ML_DK] for h in MH]
    kh_b = [mk[:, h * ML_DK:(h + 1) * ML_DK] for h in MH]
    vh = [mv[:, h * ML_DV:(h + 1) * ML_DV] for h in MH]
    c_prev = [c_ref[h] for h in MH]
    n_prev = [n_ref[h][0:1, :] for h in MH]
    m_prev = [m_ref[h][0:1, 0:1] for h in MH]
    s_raw = [_dot_nt(qh_b[h], kh_b[h]) for h in MH]
    qc = [_dot(qh_b[h], c_prev[h].astype(BF16)) for h in MH]
    log_d = [jnp.where(causal, bc[h] - br[h] + igr[h], -jnp.inf) for h in MH]
    m_intra = [jnp.max(log_d[h], axis=1, keepdims=True) for h in MH]
    m_loc = [jnp.max(btot[h] - br[h] + igr[h], axis=1, keepdims=True) for h in MH]
    kw = [kh_b[h].astype(F32) * jnp.exp(btot[h] - bc[h] + igc[h] - m_loc[h]) for h in MH]
    kv_loc = [_dot_tn(kw[h].astype(BF16), vh[h]) for h in MH]
    n_loc = [jnp.sum(kw[h], axis=0, keepdims=True) for h in MH]
    m_inter = [bc[h] + m_prev[h] for h in MH]
    m_t = [jnp.maximum(m_intra[h], m_inter[h]) for h in MH]
    s_mat = [s_raw[h] * jnp.exp(log_d[h] - m_t[h]) for h in MH]
    inter = [jnp.exp(m_inter[h] - m_t[h]) for h in MH]
    num = [_dot(s_mat[h].astype(BF16), vh[h]) + inter[h] * qc[h] for h in MH]
    den = [jnp.sum(s_mat[h], axis=1, keepdims=True)
           + inter[h] * jnp.sum(qh_b[h].astype(F32) * n_prev[h], axis=1, keepdims=True) for h in MH]
    hh = [num[h] / jnp.maximum(jnp.abs(den[h]), jnp.exp(-m_t[h])) for h in MH]
    for h in MH:
        m_new = jnp.maximum(btot[h] + m_prev[h], m_loc[h])
        s_old = jnp.exp(btot[h] + m_prev[h] - m_new)
        s_loc = jnp.exp(m_loc[h] - m_new)
        c_ref[h] = s_old * c_prev[h] + s_loc * kv_loc[h]
        n_ref[h] = jnp.broadcast_to(s_old * n_prev[h] + s_loc * n_loc[h], (8, ML_DK))
        m_ref[h] = jnp.broadcast_to(m_new, (8, 128))
    ml = jnp.concatenate(_head_norms(hh, mean_w), axis=1) * mlw_ref[...]
    ml = ml * proj_ref[:, OFF_MO:OFF_MO + ML_V].astype(F32)
    out_ref[:, RET_V:RET_V + ML_V] = ml.astype(BF16)


def _mixer_tables(seq):
    L = CHUNK
    half = RET_DK // 2
    inv = ROPE_BASE ** (-np.arange(half, dtype=np.float64) / half)
    ang = np.arange(seq, dtype=np.float64)[:, None] * inv[None, :].astype(np.float32).astype(np.float64)
    cos = np.tile(np.cos(ang), (1, RET_HEADS)).astype(np.float32)
    sin = np.tile(np.sin(ang), (1, RET_HEADS)).astype(np.float32)
    log_g = np.log1p(-np.exp2(-5.0 - np.arange(RET_HEADS, dtype=np.float64)))
    n = np.arange(L, dtype=np.float64)
    lane_head = (np.arange(RET_QK) % (RET_QK // 2)) // half
    qdec = np.exp((n + 1)[:, None] * log_g[lane_head][None, :]).astype(np.float32)
    kdec = np.exp((L - 1 - n)[:, None] * log_g[lane_head][None, :]).astype(np.float32)
    diff = n[:, None] - n[None, :]
    dmat = np.where(diff >= 0, np.exp(log_g[:, None, None] * np.maximum(diff, 0.0)[None]), 0.0).astype(np.float32)
    col_head = np.arange(RET_V) // RET_DV
    bmask = (lane_head[:, None] == col_head[None, :]).astype(np.float32)
    cdec = np.exp(L * log_g[col_head])[None, :].astype(np.float32)
    hmask = (lane_head[None, :] == np.arange(RET_HEADS)[:, None]).astype(np.float32)
    hmask = np.concatenate([hmask, np.zeros((8 - RET_HEADS, RET_QK), np.float32)], axis=0)
    tril = np.tril(np.ones((L, L), np.float32))
    ones = np.full((RET_DV, RET_DV), 1.0 / RET_DV, np.float32)
    return dict(cos=cos, sin=sin, qdec=qdec, kdec=kdec, dmat=dmat, bmask=bmask, cdec=cdec, hmask=hmask,
                tril=tril, triu=np.ascontiguousarray(tril.T), ones=ones)


def _mixer(proj, g, gt, tabs, ret_norm_w, ml_norm_w, gate_b, batch, seq):
    L = CHUNK
    nc = seq // L
    n = batch * seq
    nb = MIXER_BATCHES if batch % MIXER_BATCHES == 0 else 1
    proj = proj.reshape(batch, seq, MAIN_WIDTH)
    g = g.reshape(batch, seq, GATE_LANES)
    gt = gt.reshape(N_GATES, batch, seq).transpose(1, 0, 2)
    const2 = lambda b, c: (0, 0)
    const3 = lambda b, c: (0, 0, 0)
    tok = lambda b, c: (b, c, 0)
    in_specs = [
        pl.BlockSpec((nb, L, MAIN_WIDTH), tok),
        pl.BlockSpec((nb, L, GATE_LANES), tok),
        pl.BlockSpec((nb, N_GATES, L), lambda b, c: (b, 0, c)),
        pl.BlockSpec((L, RET_QK), const2),
        pl.BlockSpec((L, RET_QK), const2),
        pl.BlockSpec((RET_HEADS, L, L), const3),
        pl.BlockSpec((RET_QK, RET_V), const2),
        pl.BlockSpec((1, RET_V), const2),
        pl.BlockSpec((8, RET_QK), const2),
        pl.BlockSpec((L, L), const2),
        pl.BlockSpec((L, L), const2),
        pl.BlockSpec((RET_DV, RET_DV), const2),
        pl.BlockSpec((1, RET_V), const2),
        pl.BlockSpec((1, ML_V), const2),
        pl.BlockSpec((1, N_GATES), const2),
        pl.BlockSpec((N_GATES, 1), const2),
    ]
    return pl.pallas_call(
        _mixer_kernel,
        grid=(batch // nb, nc),
        in_specs=in_specs,
        out_specs=pl.BlockSpec((nb, L, RET_V + ML_V), tok),
        out_shape=jax.ShapeDtypeStruct((batch, seq, RET_V + ML_V), BF16),
        scratch_shapes=[
            pltpu.VMEM((nb, RET_QK, RET_V), F32),
            pltpu.VMEM((nb, ML_HEADS, ML_DK, ML_DV), F32),
            pltpu.VMEM((nb, ML_HEADS, 8, ML_DK), F32),
            pltpu.VMEM((nb, ML_HEADS, 8, 128), F32),
        ],
        compiler_params=pltpu.CompilerParams(
            dimension_semantics=("arbitrary", "arbitrary"), vmem_limit_bytes=VMEM_LIMIT),
        name="mixer",
    )(proj, g, gt, tabs["qdec"], tabs["kdec"], tabs["dmat"], tabs["bmask"],
      tabs["cdec"], tabs["hmask"], tabs["tril"].astype(BF16), tabs["triu"].astype(BF16), tabs["ones"].astype(BF16),
      ret_norm_w, ml_norm_w,
      gate_b.reshape(1, N_GATES), gate_b.reshape(N_GATES, 1)).reshape(n, RET_V + ML_V)


def _memkv_kernel(mem_ref, nw_ref, wkv_ref, k_ref, v_ref):
    d = mem_ref.shape[-1]
    mn = _rms(mem_ref[0], nw_ref[...]).astype(BF16)
    k_ref[0] = _dot(mn, wkv_ref[:, :d]).astype(BF16)
    v_ref[0] = _dot(mn, wkv_ref[:, d:]).astype(BF16)


def _memkv(mem, norm_w, wkv):
    b, m, d = mem.shape
    return pl.pallas_call(
        _memkv_kernel,
        grid=(b,),
        in_specs=[
            pl.BlockSpec((1, m, d), lambda i: (i, 0, 0)),
            pl.BlockSpec((1, d), lambda i: (0, 0)),
            pl.BlockSpec((d, 2 * d), lambda i: (0, 0)),
        ],
        out_specs=[pl.BlockSpec((1, m, d), lambda i: (i, 0, 0))] * 2,
        out_shape=[jax.ShapeDtypeStruct((b, m, d), BF16)] * 2,
        compiler_params=pltpu.CompilerParams(
            dimension_semantics=("arbitrary",), vmem_limit_bytes=VMEM_LIMIT),
        name="memkv",
    )(mem, norm_w, wkv)


def _attn_route_kernel(x_ref, mix_ref, k_ref, v_ref, wout_ref, nxa_ref, wq_ref, wo_ref, nmoe_ref,
                       wr_ref, wrlo_ref, br_ref, sut_ref,
                       x2_ref, h3_ref, ri_ref, rw_ref, cnt_ref, carry_ref):
    tm, d = x_ref.shape
    dh = d // XA_HEADS

    @pl.when((pl.program_id(0) == 0) & (pl.program_id(1) == 0))
    def _():
        carry_ref[...] = jnp.zeros_like(carry_ref)

    groups = [slice(g * (tm // ATTN_GROUPS), (g + 1) * (tm // ATTN_GROUPS)) for g in range(ATTN_GROUPS)]
    x1 = [x_ref[s, :] + _dot(mix_ref[s, :], wout_ref[...]) for s in groups]
    h2 = [_rms(t, nxa_ref[...]).astype(BF16) for t in x1]
    q = [_dot(t, wq_ref[...]).astype(BF16) for t in h2]
    o = []
    for qg in q:
        heads = []
        for h in range(XA_HEADS):
            logits = _dot_nt(qg[:, h * dh:(h + 1) * dh], k_ref[0, :, h * dh:(h + 1) * dh]) * (dh ** -0.5)
            mx = jnp.max(logits, axis=-1, keepdims=True)
            e = jnp.exp(logits - mx)
            p = (e / jnp.sum(e, axis=-1, keepdims=True)).astype(BF16)
            heads.append(_dot(p, v_ref[0, :, h * dh:(h + 1) * dh]).astype(BF16))
        o.append(jnp.concatenate(heads, axis=1))
    x2 = [a + _dot(b, wo_ref[...]) for a, b in zip(x1, o)]
    for s, t in zip(groups, x2):
        x2_ref[s, :] = t
    h3 = [_rms(t, nmoe_ref[...]) for t in x2]
    for s, t in zip(groups, h3):
        h3_ref[s, 0, :] = _pack_rows(t[:, :d // 2], t[:, d // 2:])

    lts = []
    for t in h3:
        t_hi = t.astype(BF16)
        t_lo = (t - t_hi.astype(F32)).astype(BF16)
        lts.append(_dot_nt(wr_ref[...], t_hi) + (_dot_nt(wr_ref[...], t_lo) + _dot_nt(wrlo_ref[...], t_hi)))
    lt = jnp.concatenate(lts, axis=1) + br_ref[...]
    gl = lt[N_EXPERTS:N_EXPERTS + N_GROUPS]
    gmax = jnp.max(gl, axis=0, keepdims=True)
    g_w = 1.0 / jnp.sum(jnp.exp(gl - gmax), axis=0, keepdims=True)
    giota = lax.broadcasted_iota(jnp.int32, gl.shape, 0)
    g_sel = jnp.min(jnp.where(gl == gmax, giota, N_GROUPS), axis=0, keepdims=True)
    el = lt[0:N_EXPERTS]
    eiota = lax.broadcasted_iota(jnp.int32, el.shape, 0)
    in_grp = (eiota // EXP_PER_GROUP) == g_sel
    elm = jnp.where(in_grp, el, -jnp.inf)
    m1 = jnp.max(elm, axis=0, keepdims=True)
    esum = jnp.sum(jnp.where(in_grp, jnp.exp(el - m1), 0.0), axis=0, keepdims=True)
    i1 = jnp.min(jnp.where(elm == m1, eiota, N_EXPERTS), axis=0, keepdims=True)
    elm2 = jnp.where(eiota == i1, -jnp.inf, elm)
    m2 = jnp.max(elm2, axis=0, keepdims=True)
    i2 = jnp.min(jnp.where(elm2 == m2, eiota, N_EXPERTS), axis=0, keepdims=True)
    p1 = 1.0 / esum
    p2 = jnp.exp(m2 - m1) / esum
    psum = p1 + p2
    w1 = g_w * (p1 / psum)
    w2 = g_w * (p2 / psum)

    oh1 = (eiota == i1).astype(F32)
    oh2 = (eiota == i2).astype(F32)
    cnt = oh1 + oh2
    base = carry_ref[:, 0:1] + _dot(cnt.astype(BF16), sut_ref[...])
    r1 = jnp.sum(oh1 * base, axis=0, keepdims=True)
    r2 = jnp.sum(oh2 * base, axis=0, keepdims=True)
    new_carry = carry_ref[...] + jnp.sum(cnt, axis=1, keepdims=True)
    carry_ref[...] = new_carry
    cnt_ref[...] = new_carry

    zi = jnp.zeros((4, tm), jnp.int32)
    ri_ref[...] = jnp.concatenate([i1, i2, r1.astype(jnp.int32), r2.astype(jnp.int32), zi], axis=0)
    rw_ref[...] = jnp.concatenate([w1, w2, jnp.zeros((6, tm), F32)], axis=0)


def _attn_route(xf, mixed, kmem, vmem, w_out, norm_xa_w, wq, wo, norm_moe_w, w_route_t, b_route, sut,
                batch, seq):
    n, d = xf.shape
    w_route_hi = w_route_t.astype(BF16)
    w_route_lo = (w_route_t - w_route_hi.astype(F32)).astype(BF16)
    tm = TOKEN_TILE
    nt = seq // tm
    m = kmem.shape[1]
    tok = lambda b, t: (b * nt + t, 0)
    lane_tok = lambda b, t: (0, b * nt + t)
    const2 = lambda b, t: (0, 0)
    return pl.pallas_call(
        _attn_route_kernel,
        grid=(batch, nt),
        in_specs=[
            pl.BlockSpec((tm, d), tok),
            pl.BlockSpec((tm, d), tok),
            pl.BlockSpec((1, m, d), lambda b, t: (b, 0, 0)),
            pl.BlockSpec((1, m, d), lambda b, t: (b, 0, 0)),
            pl.BlockSpec((d, d), const2),
            pl.BlockSpec((1, d), const2),
            pl.BlockSpec((d, d), const2),
            pl.BlockSpec((d, d), const2),
            pl.BlockSpec((1, d), const2),
            pl.BlockSpec((ROUTE_ROWS, d), const2),
            pl.BlockSpec((ROUTE_ROWS, d), const2),
            pl.BlockSpec((ROUTE_ROWS, 1), const2),
            pl.BlockSpec((tm, tm), const2),
        ],
        out_specs=[
            pl.BlockSpec((tm, d), tok),
            pl.BlockSpec((tm, 1, d // 2), lambda b, t: (b * nt + t, 0, 0)),
            pl.BlockSpec((8, tm), lane_tok),
            pl.BlockSpec((8, tm), lane_tok),
            pl.BlockSpec((N_EXPERTS, 128), const2),
        ],
        out_shape=[
            jax.ShapeDtypeStruct((n, d), F32),
            jax.ShapeDtypeStruct((n, 1, d // 2), jnp.uint32),
            jax.ShapeDtypeStruct((8, n), jnp.int32),
            jax.ShapeDtypeStruct((8, n), F32),
            jax.ShapeDtypeStruct((N_EXPERTS, 128), F32),
        ],
        scratch_shapes=[pltpu.VMEM((N_EXPERTS, 128), F32)],
        compiler_params=pltpu.CompilerParams(
            dimension_semantics=("arbitrary", "arbitrary"), vmem_limit_bytes=VMEM_LIMIT),
        name="attn_route",
    )(xf, mixed, kmem, vmem, w_out, norm_xa_w, wq, wo, norm_moe_w, w_route_hi, w_route_lo, b_route, sut)


def _dispatch_kernel(zpos_ref, dest_ref, h_ref, xs_ref, idx_ref, idx_sem, row_sem, zero_ref, zero_sem):
    i = pl.program_id(0)
    nsteps = pl.num_programs(0)
    td = h_ref.shape[0]
    bm = zero_ref.shape[0]
    slot = i % 2

    def idx_copy(step, sl):
        off = pl.multiple_of(sl * (2 * td), 2 * td)
        return pltpu.make_async_copy(dest_ref.at[step], idx_ref.at[pl.ds(off, 2 * td)], idx_sem.at[sl])

    @pl.when(i == 0)
    def _():
        zero_ref[...] = jnp.zeros_like(zero_ref)

        def zero_copy(e):
            return pltpu.make_async_copy(zero_ref, xs_ref.at[pl.ds(pl.multiple_of(zpos_ref[e], bm), bm), 0], zero_sem)

        def tail_copy(b):
            return pltpu.make_async_copy(zero_ref, xs_ref.at[pl.ds(pl.multiple_of(b * bm, bm), bm), 0], zero_sem)

        nused = zpos_ref[N_EXPERTS]
        nblk = xs_ref.shape[0] // bm
        for e in range(N_EXPERTS):
            pl.when(zpos_ref[e] >= 0)(lambda e=e: zero_copy(e).start())
        lax.fori_loop(nused, nblk, lambda b, c: (tail_copy(b).start(), c)[1], 0)
        for e in range(N_EXPERTS):
            pl.when(zpos_ref[e] >= 0)(lambda e=e: zero_copy(e).wait())
        lax.fori_loop(nused, nblk, lambda b, c: (tail_copy(b).wait(), c)[1], 0)
        idx_copy(0, 0).start()

    idx_copy(i, slot).wait()

    @pl.when(i + 1 < nsteps)
    def _():
        idx_copy(i + 1, 1 - slot).start()

    base = slot * (2 * td)

    for t in range(td):
        for k in range(TOP_K):
            pltpu.make_async_copy(h_ref.at[t], xs_ref.at[idx_ref[base + k * td + t]], row_sem).start(priority=t % 2)
    for _ in range(TOP_K):
        pltpu.make_async_copy(xs_ref.at[pl.ds(0, td)], xs_ref.at[pl.ds(0, td)], row_sem).wait()


def _dispatch(h3p, dest_tiles, zpos, cap):
    w = h3p.shape[-1]
    nt, td2 = dest_tiles.shape
    td = td2 // 2
    grid_spec = pltpu.PrefetchScalarGridSpec(
        num_scalar_prefetch=1,
        grid=(nt,),
        in_specs=[
            pl.BlockSpec(memory_space=pl.ANY),
            pl.BlockSpec((td, 1, w), lambda i, zp: (i, 0, 0)),
        ],
        out_specs=pl.BlockSpec(memory_space=pl.ANY),
        scratch_shapes=[
            pltpu.SMEM((2 * td2,), jnp.int32),
            pltpu.SemaphoreType.DMA((2,)),
            pltpu.SemaphoreType.DMA,
            pltpu.VMEM((MOE_ROWS, w), jnp.uint32),
            pltpu.SemaphoreType.DMA,
        ],
    )
    return pl.pallas_call(
        _dispatch_kernel,
        grid_spec=grid_spec,
        out_shape=jax.ShapeDtypeStruct((cap, 1, w), jnp.uint32),
        compiler_params=pltpu.CompilerParams(
            dimension_semantics=("arbitrary",), vmem_limit_bytes=VMEM_LIMIT),
        name="dispatch",
    )(zpos, dest_tiles, h3p)


def _expert_kernel(blk_e_ref, nused_ref, next_e_ref, xs_ref, wg_ref, wu_ref, wd_ref, ys_ref, wg_b, wu_b, wd_b,
                   wg_s, wu_s, wd_s, xbuf, ybuf, zbuf, in_sem, out_sem, zero_sem, w_sem):
    i = pl.program_id(0)
    nsteps = pl.num_programs(0)
    nused = nused_ref[0]
    bm = xbuf.shape[1]
    slot = i % 2
    prev = blk_e_ref[jnp.maximum(i - 1, 0)]
    fresh = (i < nused) & ((i == 0) | (blk_e_ref[i] != prev))
    half = wd_b.shape[1] // 2

    def w_copies(e):
        return [pltpu.make_async_copy(src.at[e], dst, w_sem.at[j])
                for j, (src, dst) in enumerate(((wg_ref, wg_s), (wu_ref, wu_s), (wd_ref, wd_s)))]

    def rows(ref, step):
        return ref.at[pl.ds(pl.multiple_of(step * bm, bm), bm), 0]

    def in_copy(step, sl):
        return pltpu.make_async_copy(rows(xs_ref, step), xbuf.at[sl], in_sem.at[sl])

    def out_copy(step, sl):
        return pltpu.make_async_copy(ybuf.at[sl], rows(ys_ref, step), out_sem.at[sl])

    def zero_copy(step):
        return pltpu.make_async_copy(zbuf, rows(ys_ref, step), zero_sem)

    @pl.when(i == 0)
    def _():
        zbuf[...] = jnp.zeros_like(zbuf)
        in_copy(0, 0).start()
        for c in w_copies(blk_e_ref[0]):
            c.start()

    @pl.when(i + 1 < nused)
    def _():
        in_copy(i + 1, 1 - slot).start()

    @pl.when(fresh)
    def _():
        for c in w_copies(blk_e_ref[i]):
            c.wait()
        wg_b[...] = wg_s[...].astype(BF16)
        wu_b[...] = wu_s[...].astype(BF16)
        wd_b[...] = wd_s[...].astype(BF16)

        @pl.when(next_e_ref[i] >= 0)
        def _():
            for c in w_copies(next_e_ref[i]):
                c.start()

    @pl.when(i < nused)
    def _():
        in_copy(i, slot).wait()
        pl.when(i >= 2)(lambda: out_copy(i - 2, slot).wait())
        lo, hi = _unpack_rows(xbuf[slot])
        xb = jnp.concatenate([lo.astype(BF16), hi.astype(BF16)], axis=1)
        hid = (_silu(_dot(xb, wg_b[...])) * _dot(xb, wu_b[...])).astype(BF16)
        y = _dot(hid, wd_b[...])
        ybuf[slot] = _pack_rows(y[:, :half], y[:, half:])
        out_copy(i, slot).start(priority=1)

    pl.when(i >= nused)(lambda: zero_copy(i).start())

    @pl.when(i == nsteps - 1)
    def _():
        pl.when(nused >= 2)(lambda: out_copy(nused - 2, nused % 2).wait())
        pl.when(nused >= 1)(lambda: out_copy(nused - 1, (nused - 1) % 2).wait())
        lax.fori_loop(nused, nsteps, lambda b, c: (zero_copy(b).wait(), c)[1], 0)


def _experts(xs, blk_e, nused, next_e, w_gate, w_up, w_down):
    cap, _, w = xs.shape
    _, d, de = w_gate.shape
    bm = MOE_ROWS
    grid_spec = pltpu.PrefetchScalarGridSpec(
        num_scalar_prefetch=3,
        grid=(cap // bm,),
        in_specs=[pl.BlockSpec(memory_space=pl.ANY)] * 4,
        out_specs=pl.BlockSpec(memory_space=pl.ANY),
        scratch_shapes=[
            pltpu.VMEM((d, de), BF16),
            pltpu.VMEM((d, de), BF16),
            pltpu.VMEM((de, d), BF16),
            pltpu.VMEM((d, de), F32),
            pltpu.VMEM((d, de), F32),
            pltpu.VMEM((de, d), F32),
            pltpu.VMEM((2, bm, w), jnp.uint32),
            pltpu.VMEM((2, bm, w), jnp.uint32),
            pltpu.VMEM((bm, w), jnp.uint32),
            pltpu.SemaphoreType.DMA((2,)),
            pltpu.SemaphoreType.DMA((2,)),
            pltpu.SemaphoreType.DMA,
            pltpu.SemaphoreType.DMA((3,)),
        ],
    )
    return pl.pallas_call(
        _expert_kernel,
        grid_spec=grid_spec,
        out_shape=jax.ShapeDtypeStruct((cap, 1, w), jnp.uint32),
        compiler_params=pltpu.CompilerParams(
            dimension_semantics=("arbitrary",), vmem_limit_bytes=VMEM_LIMIT),
        name="experts",
    )(blk_e, nused, next_e, xs, w_gate, w_up, w_down)


def _combine_kernel(dest_ref, ys_ref, x2_ref, rw_ref, eye_ref, nw_ref, o_ref, idx_ref, idx_sem, ybuf, ysem):
    i = pl.program_id(0)
    nsteps = pl.num_programs(0)
    tc, d = x2_ref.shape
    half = d // 2
    n_idx = 2 * tc

    def idx_copy(step):
        sl = step % 3
        off = pl.multiple_of(sl * n_idx, n_idx)
        return pltpu.make_async_copy(dest_ref.at[step], idx_ref.at[pl.ds(off, n_idx)], idx_sem.at[sl])

    def gather(step):
        base = (step % 3) * n_idx
        buf = ybuf.at[step % 2]
        sem = ysem.at[step % 2]

        for t in range(n_idx):
            pltpu.make_async_copy(ys_ref.at[idx_ref[base + t]], buf.at[pl.ds(t, 1)], sem).start(priority=t % 2)

    @pl.when(i == 0)
    def _():
        idx_copy(0).start()
        idx_copy(0).wait()
        gather(0)

        @pl.when(nsteps > 1)
        def _():
            idx_copy(1).start()

    @pl.when(i + 1 < nsteps)
    def _():
        idx_copy(i + 1).wait()

        @pl.when(i + 2 < nsteps)
        def _():
            idx_copy(i + 2).start()

        gather(i + 1)

    slot = i % 2
    pltpu.make_async_copy(ybuf.at[slot], ybuf.at[slot], ysem.at[slot]).wait()
    eye = eye_ref[...]
    ew = eye.shape[0]

    def to_columns(r):
        r_a = r.astype(BF16)
        r_b = (r - r_a.astype(F32)).astype(BF16)
        r_c = (r - r_a.astype(F32) - r_b.astype(F32)).astype(BF16)
        return _dot_nt(eye, r_a) + (_dot_nt(eye, r_b) + _dot_nt(eye, r_c))

    wcol = jnp.concatenate([to_columns(rw_ref[:, j:j + ew]) for j in range(0, tc, ew)], axis=0)
    lo1, hi1 = _unpack_rows(ybuf[slot, 0:tc])
    lo2, hi2 = _unpack_rows(ybuf[slot, tc:n_idx])
    w1 = wcol[:, 0:1]
    w2 = wcol[:, 1:2]
    z_lo = x2_ref[:, :half] + (lo1 * w1 + lo2 * w2)
    z_hi = x2_ref[:, half:] + (hi1 * w1 + hi2 * w2)
    ms = (jnp.sum(z_lo * z_lo, axis=-1, keepdims=True) + jnp.sum(z_hi * z_hi, axis=-1, keepdims=True)) / d
    scale = lax.rsqrt(ms + EPS)
    o_ref[:, :half] = z_lo * scale * nw_ref[:, :half]
    o_ref[:, half:] = z_hi * scale * nw_ref[:, half:]


def _combine(x2, ys, dest_tiles, rw, eye, norm_w):
    n, d = x2.shape
    nt, n_idx = dest_tiles.shape
    tc = n_idx // 2
    w = ys.shape[-1]
    return pl.pallas_call(
        _combine_kernel,
        grid=(nt,),
        in_specs=[pl.BlockSpec(memory_space=pl.ANY)] * 2 + [
            pl.BlockSpec((tc, d), lambda i: (i, 0)),
            pl.BlockSpec((8, tc), lambda i: (0, i)),
            pl.BlockSpec(eye.shape, lambda i: (0, 0)),
            pl.BlockSpec((1, d), lambda i: (0, 0)),
        ],
        out_specs=pl.BlockSpec((tc, d), lambda i: (i, 0)),
        out_shape=jax.ShapeDtypeStruct((n, d), F32),
        scratch_shapes=[
            pltpu.SMEM((3 * n_idx,), jnp.int32),
            pltpu.SemaphoreType.DMA((3,)),
            pltpu.VMEM((2, n_idx, w), jnp.uint32),
            pltpu.SemaphoreType.DMA((2,)),
        ],
        compiler_params=pltpu.CompilerParams(
            dimension_semantics=("arbitrary",), vmem_limit_bytes=VMEM_LIMIT),
        name="combine",
    )(dest_tiles, ys, x2, rw, eye, norm_w)


def _layer(xf, mem, batch, seq, norm_mix_w, w_in, ret_norm_w, ml_conv_w, ml_conv_b, ml_gate_b, ml_norm_w,
           w_out, norm_xa_w, norm_mem_w, xa_wq, xa_wkv, xa_wo, norm_moe_w, moe_w_group, moe_b_group,
           moe_w_router, moe_b_router, moe_w_gate, moe_w_up, moe_w_down, final_norm_w):
    n, d = xf.shape
    def halves_first(w):
        return w.reshape(d, RET_HEADS, 2, RET_DK // 2).transpose(0, 2, 1, 3).reshape(d, RET_QK)

    w_main = jnp.concatenate([halves_first(w_in[:, OFF_RQ:OFF_RQ + RET_QK]),
                              halves_first(w_in[:, OFF_RK:OFF_RK + RET_QK]),
                              w_in[:, OFF_RV:MAIN_WIDTH]], axis=1).astype(BF16)
    w_if = w_in[:, MAIN_WIDTH:].astype(BF16)
    tabs = {k_: jnp.asarray(v_) for k_, v_ in _mixer_tables(seq).items()}
    w_if_pad = jnp.pad(w_if, ((0, 0), (0, GATE_LANES - N_GATES)))
    proj, g, gt = _inproj(xf, norm_mix_w.reshape(1, d), w_main, w_if_pad, w_if.T, tabs["cos"], tabs["sin"],
                          ml_conv_w, ml_conv_b.reshape(1, 2 * ML_QK), seq)
    mixed = _mixer(proj, g, gt, tabs, ret_norm_w.reshape(1, RET_V), ml_norm_w.reshape(1, ML_V), ml_gate_b,
                   batch, seq)

    kmem, vmem = _memkv(mem, norm_mem_w.reshape(1, d), xa_wkv.astype(BF16))

    w_route_t = jnp.concatenate(
        [moe_w_router.T, moe_w_group.T, jnp.zeros((ROUTE_ROWS - N_EXPERTS - N_GROUPS, d), F32)], axis=0)
    b_route = jnp.concatenate(
        [moe_b_router, moe_b_group, jnp.zeros((ROUTE_ROWS - N_EXPERTS - N_GROUPS,), F32)]).reshape(ROUTE_ROWS, 1)
    tm = TOKEN_TILE
    sut = jnp.asarray(np.triu(np.ones((tm, tm), np.float32), 1), dtype=BF16)
    x2, h3, ri, rw, cnt = _attn_route(xf, mixed, kmem, vmem, w_out.astype(BF16), norm_xa_w.reshape(1, d),
                                      xa_wq.astype(BF16), xa_wo.astype(BF16), norm_moe_w.reshape(1, d),
                                      w_route_t, b_route, sut, batch, seq)

    bm = MOE_ROWS
    counts = cnt[:, 0].astype(jnp.int32)
    padded = (counts + bm - 1) // bm * bm
    pends = jnp.cumsum(padded)
    pstarts = pends - padded
    expert = ri[0:TOP_K]
    onehot = expert[None] == jnp.arange(N_EXPERTS, dtype=jnp.int32)[:, None, None]
    dest = jnp.sum(jnp.where(onehot, pstarts[:, None, None], 0), axis=0) + ri[TOP_K:2 * TOP_K]
    cap = n * TOP_K + N_EXPERTS * bm
    nblk = cap // bm
    blk_start = jnp.arange(nblk, dtype=jnp.int32) * bm
    blk_e = jnp.minimum(jnp.sum(blk_start[:, None] >= pends[None, :], axis=1), N_EXPERTS - 1).astype(jnp.int32)
    nused = (pends[-1] // bm).astype(jnp.int32).reshape(1)
    zpos = jnp.where(padded > counts, pends - bm, -1).astype(jnp.int32)
    zpos = jnp.concatenate([zpos, nused])

    def tiles(rows):
        return dest.reshape(TOP_K, n // rows, rows).transpose(1, 0, 2).reshape(n // rows, TOP_K * rows)

    eids = jnp.arange(N_EXPERTS, dtype=jnp.int32)
    later = jnp.where((eids[None, :] > eids[:, None]) & (padded[None, :] > 0), eids[None, :], N_EXPERTS)
    next_of = jnp.min(later, axis=1)
    next_of = jnp.where(next_of == N_EXPERTS, -1, next_of)
    next_e = jnp.sum(jnp.where(blk_e[:, None] == eids[None, :], next_of[None, :], 0), axis=1).astype(jnp.int32)

    xs = _dispatch(h3, tiles(DISPATCH_TILE), zpos, cap)
    ys = _experts(xs, blk_e, nused, next_e, moe_w_gate, moe_w_up, moe_w_down)
    eye = jnp.asarray(np.eye(COMBINE_EYE, dtype=np.float32), dtype=BF16)
    return _combine(x2, ys, tiles(COMBINE_TILE), rw, eye, final_norm_w.reshape(1, d))


def kernel(x, mem, norm_mix_w, w_in, ret_norm_w, ml_conv_w, ml_conv_b, ml_gate_b, ml_norm_w, w_out, norm_xa_w, norm_mem_w, xa_wq, xa_wkv, xa_wo, norm_moe_w, moe_w_group, moe_b_group, moe_w_router, moe_b_router, moe_w_gate, moe_w_up, moe_w_down, norm_final_w):
    batch, seq, d = x.shape
    depth = w_in.shape[0]
    assert depth == 1, "the final norm is fused into the last layer's combine kernel"
    l = 0
    out = _layer(x.reshape(batch * seq, d), mem, batch, seq, norm_mix_w[l], w_in[l], ret_norm_w[l], ml_conv_w[l],
                 ml_conv_b[l], ml_gate_b[l], ml_norm_w[l], w_out[l], norm_xa_w[l], norm_mem_w[l], xa_wq[l],
                 xa_wkv[l], xa_wo[l], norm_moe_w[l], moe_w_group[l], moe_b_group[l], moe_w_router[l],
                 moe_b_router[l], moe_w_gate[l], moe_w_up[l], moe_w_down[l], norm_final_w)
    return out.reshape(batch, seq, d)
```

```python
import functools

import numpy as np
import jax
import jax.numpy as jnp
from jax import lax
from jax.experimental import pallas as pl
from jax.experimental.pallas import tpu as pltpu

F32 = jnp.float32
BF16 = jnp.bfloat16

CHUNK = 128
RET_HEADS = 4
RET_DK = 64
RET_DV = 128
ML_HEADS = 4
ML_DK = 128
ML_DV = 128
CONV_W = 4
XA_HEADS = 4
N_GROUPS = 4
EXP_PER_GROUP = 8
N_EXPERTS = N_GROUPS * EXP_PER_GROUP
TOP_K = 2
ROPE_BASE = 10000.0
EPS = 1e-6

RET_QK = RET_HEADS * RET_DK
RET_V = RET_HEADS * RET_DV
ML_QK = ML_HEADS * ML_DK
ML_V = ML_HEADS * ML_DV
OFF_RQ = 0
OFF_RK = OFF_RQ + RET_QK
OFF_RV = OFF_RK + RET_QK
OFF_RG = OFF_RV + RET_V
OFF_MQK = OFF_RG + RET_V
OFF_MV = OFF_MQK + 2 * ML_QK
OFF_MO = OFF_MV + ML_V
MAIN_WIDTH = OFF_MO + ML_V
N_GATES = 2 * ML_HEADS
GATE_LANES = 128

ROUTE_ROWS = 40
TOKEN_TILE = 1024
MOE_ROWS = 512
DISPATCH_TILE = 2048
COMBINE_TILE = 512
CONV_ROWS = 128
COMBINE_EYE = 512
MIXER_BATCHES = 8
ATTN_GROUPS = 2
VMEM_LIMIT = 56 * 1024 * 1024


def _dot(a, b):
    return jnp.dot(a, b, preferred_element_type=F32)


def _dot_nt(a, b):
    return lax.dot_general(a, b, (((1,), (1,)), ((), ())), preferred_element_type=F32)


def _dot_tn(a, b):
    return lax.dot_general(a, b, (((0,), (0,)), ((), ())), preferred_element_type=F32)


def _rms(x, w):
    return x * lax.rsqrt(jnp.mean(x * x, axis=-1, keepdims=True) + EPS) * w


def _sigmoid(x):
    return 1.0 / (1.0 + jnp.exp(-x))


def _silu(x):
    return x * _sigmoid(x)


def _log_sigmoid(x):
    return jnp.minimum(x, 0.0) - jnp.log1p(jnp.exp(-jnp.abs(x)))


def _head_norms(ts, mean_w):
    mu = [_dot(t.astype(BF16), mean_w) for t in ts]
    dl = [t - m for t, m in zip(ts, mu)]
    var = [_dot((d * d).astype(BF16), mean_w) for d in dl]
    return [d * lax.rsqrt(v + EPS) for d, v in zip(dl, var)]


def _pack_rows(lo, hi):
    def bits(t):
        return lax.bitcast_convert_type(t.astype(BF16), jnp.uint16).astype(jnp.uint32)
    return bits(lo) | (bits(hi) << 16)


def _unpack_rows(u):
    lo = lax.bitcast_convert_type(u << 16, F32)
    hi = lax.bitcast_convert_type(u & jnp.uint32(0xFFFF0000), F32)
    return lo, hi


def _inproj_kernel(tiles_per_seq, x_ref, nw_ref, w_ref, wif_ref, wift_ref, cos_ref, sin_ref, convw_ref, convb_ref,
                   proj_ref, g_ref, gt_ref, carry_ref):
    tm = x_ref.shape[0]

    @pl.when(pl.program_id(0) == 0)
    def _():
        carry_ref[...] = jnp.zeros_like(carry_ref)

    h = _rms(x_ref[...], nw_ref[...]).astype(BF16)

    def mm(off, width):
        return _dot(h, w_ref[:, off:off + width])

    def rotary(qk):
        cos = cos_ref[...]
        sin = sin_ref[...]
        half = RET_QK // 2
        for off, scale in ((OFF_RQ, None), (OFF_RK, RET_DK ** -0.5)):
            t1 = qk[:, off - OFF_RQ:off - OFF_RQ + half]
            t2 = qk[:, off - OFF_RQ + half:off - OFF_RQ + 2 * half]
            r1 = t1 * cos - t2 * sin
            r2 = t1 * sin + t2 * cos
            if scale is not None:
                r1, r2 = r1 * scale, r2 * scale
            proj_ref[:, off:off + half] = r1.astype(BF16)
            proj_ref[:, off + half:off + 2 * half] = r2.astype(BF16)

    def conv_silu(part, scale):
        c0 = part * ML_QK
        first = (pl.program_id(0) % tiles_per_seq) == 0
        row8 = lax.broadcasted_iota(jnp.int32, (8, ML_QK), 0)
        prev = jnp.where(first, 0.0, carry_ref[:, c0:c0 + ML_QK])
        for r0 in range(0, tm, CONV_ROWS):
            cur = _dot(h[r0:r0 + CONV_ROWS], w_ref[:, OFF_MQK + c0:OFF_MQK + c0 + ML_QK])
            acc = cur * convw_ref[CONV_W - 1:CONV_W, c0:c0 + ML_QK] + convb_ref[:, c0:c0 + ML_QK]
            for s in range(1, CONV_W):
                rolled = pltpu.roll(cur, s, 0)
                head8 = jnp.where(row8 < s, pltpu.roll(prev, s, 0), rolled[0:8])
                shifted = jnp.concatenate([head8, rolled[8:]], axis=0)
                acc = acc + shifted * convw_ref[CONV_W - 1 - s:CONV_W - s, c0:c0 + ML_QK]
            prev = cur[CONV_ROWS - 8:CONV_ROWS]
            act = _silu(acc) if scale is None else _silu(acc) * scale
            proj_ref[r0:r0 + CONV_ROWS, OFF_MQK + c0:OFF_MQK + c0 + ML_QK] = act.astype(BF16)
        carry_ref[:, c0:c0 + ML_QK] = prev

    def store(off, width, fn=None):
        def ep(t):
            proj_ref[:, off:off + width] = (t if fn is None else fn(t)).astype(BF16)
        return ep

    rotary(mm(OFF_RQ, 2 * RET_QK))
    store(OFF_RV, RET_V)(mm(OFF_RV, RET_V))
    store(OFF_RG, RET_V, _silu)(mm(OFF_RG, RET_V))
    conv_silu(0, None)
    conv_silu(1, ML_DK ** -0.5)
    store(OFF_MV, ML_V)(mm(OFF_MV, ML_V))
    store(OFF_MO, ML_V, _sigmoid)(mm(OFF_MO, ML_V))
    g_ref[...] = _dot(h, wif_ref[...])
    gt_ref[...] = _dot_nt(wift_ref[...], h)


def _inproj(xf, norm_w, w_main, w_if, w_ift, cos, sin, conv_w, conv_b, seq):
    n, d = xf.shape
    tm = TOKEN_TILE
    tiles_per_seq = seq // tm
    return pl.pallas_call(
        functools.partial(_inproj_kernel, tiles_per_seq),
        grid=(n // tm,),
        in_specs=[
            pl.BlockSpec((tm, d), lambda i: (i, 0)),
            pl.BlockSpec((1, d), lambda i: (0, 0)),
            pl.BlockSpec((d, MAIN_WIDTH), lambda i: (0, 0)),
            pl.BlockSpec((d, GATE_LANES), lambda i: (0, 0)),
            pl.BlockSpec((N_GATES, d), lambda i: (0, 0)),
            pl.BlockSpec((tm, RET_QK // 2), lambda i: (i % tiles_per_seq, 0)),
            pl.BlockSpec((tm, RET_QK // 2), lambda i: (i % tiles_per_seq, 0)),
            pl.BlockSpec((CONV_W, 2 * ML_QK), lambda i: (0, 0)),
            pl.BlockSpec((1, 2 * ML_QK), lambda i: (0, 0)),
        ],
        out_specs=[
            pl.BlockSpec((tm, MAIN_WIDTH), lambda i: (i, 0)),
            pl.BlockSpec((tm, GATE_LANES), lambda i: (i, 0)),
            pl.BlockSpec((N_GATES, tm), lambda i: (0, i)),
        ],
        out_shape=[
            jax.ShapeDtypeStruct((n, MAIN_WIDTH), BF16),
            jax.ShapeDtypeStruct((n, GATE_LANES), F32),
            jax.ShapeDtypeStruct((N_GATES, n), F32),
        ],
        scratch_shapes=[pltpu.VMEM((8, 2 * ML_QK), F32)],
        compiler_params=pltpu.CompilerParams(
            dimension_semantics=("arbitrary",), vmem_limit_bytes=VMEM_LIMIT),
        name="inproj",
    )(xf, norm_w, w_main, w_if, w_ift, cos, sin, conv_w, conv_b)


def _mixer_kernel(*refs):
    @pl.when(pl.program_id(1) == 0)
    def _():
        for state_ref in refs[-4:]:
            state_ref[...] = jnp.zeros_like(state_ref)

    for bi in range(refs[0].shape[0]):
        _mixer_one(bi, *refs)


def _mixer_one(bi, proj_ref, g_ref, gt_ref, qdec_ref, kdec_ref, dmat_ref,
               bmask_ref, cdec_ref, hmask_ref, tril_ref, triu_ref, ones_ref, retw_ref, mlw_ref, gbc_ref, gbr_ref,
               out_ref, r_ref, c_ref, n_ref, m_ref):
    L = CHUNK
    proj_ref, g_ref, gt_ref, out_ref = proj_ref.at[bi], g_ref.at[bi], gt_ref.at[bi], out_ref.at[bi]
    r_ref, c_ref, n_ref, m_ref = r_ref.at[bi], c_ref.at[bi], n_ref.at[bi], m_ref.at[bi]
    mean_w = ones_ref[...]

    q = proj_ref[:, OFF_RQ:OFF_RQ + RET_QK].astype(F32)
    k_b = proj_ref[:, OFF_RK:OFF_RK + RET_QK]
    k = k_b.astype(F32)
    v = proj_ref[:, OFF_RV:OFF_RV + RET_V]
    r_prev = r_ref[...]
    cross = _dot((q * qdec_ref[...]).astype(BF16), r_prev.astype(BF16))
    kv = _dot_tn((k * kdec_ref[...]).astype(BF16), v) * bmask_ref[...]
    r_ref[...] = cdec_ref[...] * r_prev + kv
    RH = range(RET_HEADS)
    sc = [_dot_nt((q * hmask_ref[h:h + 1, :]).astype(BF16), k_b) for h in RH]
    sc = [(sc[h] * dmat_ref[h]).astype(BF16) for h in RH]
    tot = [_dot(sc[h], v[:, h * RET_DV:(h + 1) * RET_DV]) + cross[:, h * RET_DV:(h + 1) * RET_DV] for h in RH]
    ret = jnp.concatenate(_head_norms(tot, mean_w), axis=1) * retw_ref[...]
    ret = ret * proj_ref[:, OFF_RG:OFF_RG + RET_V].astype(F32)
    out_ref[:, 0:RET_V] = ret.astype(BF16)

    mq = proj_ref[:, OFF_MQK:OFF_MQK + ML_QK]
    mk = proj_ref[:, OFF_MQK + ML_QK:OFF_MQK + 2 * ML_QK]
    mv = proj_ref[:, OFF_MV:OFF_MV + ML_V]

    gc = g_ref[:, 0:N_GATES] + gbc_ref[...]
    gr = gt_ref[...] + gbr_ref[...]
    lf_c = _log_sigmoid(gc)
    lf_r = _log_sigmoid(gr)
    lf_c_hi = lf_c.astype(BF16)
    lf_r_hi = lf_r.astype(BF16)
    b_c = (_dot(tril_ref[...], lf_c_hi)
           + _dot(tril_ref[...], (lf_c - lf_c_hi.astype(F32)).astype(BF16)))
    b_r = (_dot(lf_r_hi, triu_ref[...])
           + _dot((lf_r - lf_r_hi.astype(F32)).astype(BF16), triu_ref[...]))
    causal = (lax.broadcasted_iota(jnp.int32, (L, L), 0) >= lax.broadcasted_iota(jnp.int32, (L, L), 1))
    MH = range(ML_HEADS)
    bc = [b_c[:, ML_HEADS + h:ML_HEADS + h + 1] for h in MH]
    br = [b_r[ML_HEADS + h:ML_HEADS + h + 1, :] for h in MH]
    igc = [gc[:, h:h + 1] for h in MH]
    igr = [gr[h:h + 1, :] for h in MH]
    btot = [br[h][:, L - 1:L] for h in MH]
    qh_b = [mq[:, h * ML_DK:(h + 1) * ML_DK] for h in MH]
    kh_b = [mk[:, h * ML_DK:(h + 1) * ML_DK] for h in MH]
    vh = [mv[:, h * ML_DV:(h + 1) * ML_DV] for h in MH]
    c_prev = [c_ref[h] for h in MH]
    n_prev = [n_ref[h][0:1, :] for h in MH]
    m_prev = [m_ref[h][0:1, 0:1] for h in MH]
    s_raw = [_dot_nt(qh_b[h], kh_b[h]) for h in MH]
    qc = [_dot(qh_b[h], c_prev[h].astype(BF16)) for h in MH]
    log_d = [jnp.where(causal, bc[h] - br[h] + igr[h], -jnp.inf) for h in MH]
    m_intra = [jnp.max(log_d[h], axis=1, keepdims=True) for h in MH]
    m_loc = [jnp.max(btot[h] - br[h] + igr[h], axis=1, keepdims=True) for h in MH]
    kw = [kh_b[h].astype(F32) * jnp.exp(btot[h] - bc[h] + igc[h] - m_loc[h]) for h in MH]
    kv_loc = [_dot_tn(kw[h].astype(BF16), vh[h]) for h in MH]
    n_loc = [jnp.sum(kw[h], axis=0, keepdims=True) for h in MH]
    m_inter = [bc[h] + m_prev[h] for h in MH]
    m_t = [jnp.maximum(m_intra[h], m_inter[h]) for h in MH]
    s_mat = [s_raw[h] * jnp.exp(log_d[h] - m_t[h]) for h in MH]
    inter = [jnp.exp(m_inter[h] - m_t[h]) for h in MH]
    num = [_dot(s_mat[h].astype(BF16), vh[h]) + inter[h] * qc[h] for h in MH]
    den = [jnp.sum(s_mat[h], axis=1, keepdims=True)
           + inter[h] * jnp.sum(qh_b[h].astype(F32) * n_prev[h], axis=1, keepdims=True) for h in MH]
    hh = [num[h] / jnp.maximum(jnp.abs(den[h]), jnp.exp(-m_t[h])) for h in MH]
    for h in MH:
        m_new = jnp.maximum(btot[h] + m_prev[h], m_loc[h])
        s_old = jnp.exp(btot[h] + m_prev[h] - m_new)
        s_loc = jnp.exp(m_loc[h] - m_new)
        c_ref[h] = s_old * c_prev[h] + s_loc * kv_loc[h]
        n_ref[h] = jnp.broadcast_to(s_old * n_prev[h] + s_loc * n_loc[h], (8, ML_DK))
        m_ref[h] = jnp.broadcast_to(m_new, (8, 128))
    ml = jnp.concatenate(_head_norms(hh, mean_w), axis=1) * mlw_ref[...]
    ml = ml * proj_ref[:, OFF_MO:OFF_MO + ML_V].astype(F32)
    out_ref[:, RET_V:RET_V + ML_V] = ml.astype(BF16)


def _mixer_tables(seq):
    L = CHUNK
    half = RET_DK // 2
    inv = ROPE_BASE ** (-np.arange(half, dtype=np.float64) / half)
    ang = np.arange(seq, dtype=np.float64)[:, None] * inv[None, :].astype(np.float32).astype(np.float64)
    cos = np.tile(np.cos(ang), (1, RET_HEADS)).astype(np.float32)
    sin = np.tile(np.sin(ang), (1, RET_HEADS)).astype(np.float32)
    log_g = np.log1p(-np.exp2(-5.0 - np.arange(RET_HEADS, dtype=np.float64)))
    n = np.arange(L, dtype=np.float64)
    lane_head = (np.arange(RET_QK) % (RET_QK // 2)) // half
    qdec = np.exp((n + 1)[:, None] * log_g[lane_head][None, :]).astype(np.float32)
    kdec = np.exp((L - 1 - n)[:, None] * log_g[lane_head][None, :]).astype(np.float32)
    diff = n[:, None] - n[None, :]
    dmat = np.where(diff >= 0, np.exp(log_g[:, None, None] * np.maximum(diff, 0.0)[None]), 0.0).astype(np.float32)
    col_head = np.arange(RET_V) // RET_DV
    bmask = (lane_head[:, None] == col_head[None, :]).astype(np.float32)
    cdec = np.exp(L * log_g[col_head])[None, :].astype(np.float32)
    hmask = (lane_head[None, :] == np.arange(RET_HEADS)[:, None]).astype(np.float32)
    hmask = np.concatenate([hmask, np.zeros((8 - RET_HEADS, RET_QK), np.float32)], axis=0)
    tril = np.tril(np.ones((L, L), np.float32))
    ones = np.full((RET_DV, RET_DV), 1.0 / RET_DV, np.float32)
    return dict(cos=cos, sin=sin, qdec=qdec, kdec=kdec, dmat=dmat, bmask=bmask, cdec=cdec, hmask=hmask,
                tril=tril, triu=np.ascontiguousarray(tril.T), ones=ones)


def _mixer(proj, g, gt, tabs, ret_norm_w, ml_norm_w, gate_b, batch, seq):
    L = CHUNK
    nc = seq // L
    n = batch * seq
    nb = MIXER_BATCHES if batch % MIXER_BATCHES == 0 else 1
    proj = proj.reshape(batch, seq, MAIN_WIDTH)
    g = g.reshape(batch, seq, GATE_LANES)
    gt = gt.reshape(N_GATES, batch, seq).transpose(1, 0, 2)
    const2 = lambda b, c: (0, 0)
    const3 = lambda b, c: (0, 0, 0)
    tok = lambda b, c: (b, c, 0)
    in_specs = [
        pl.BlockSpec((nb, L, MAIN_WIDTH), tok),
        pl.BlockSpec((nb, L, GATE_LANES), tok),
        pl.BlockSpec((nb, N_GATES, L), lambda b, c: (b, 0, c)),
        pl.BlockSpec((L, RET_QK), const2),
        pl.BlockSpec((L, RET_QK), const2),
        pl.BlockSpec((RET_HEADS, L, L), const3),
        pl.BlockSpec((RET_QK, RET_V), const2),
        pl.BlockSpec((1, RET_V), const2),
        pl.BlockSpec((8, RET_QK), const2),
        pl.BlockSpec((L, L), const2),
        pl.BlockSpec((L, L), const2),
        pl.BlockSpec((RET_DV, RET_DV), const2),
        pl.BlockSpec((1, RET_V), const2),
        pl.BlockSpec((1, ML_V), const2),
        pl.BlockSpec((1, N_GATES), const2),
        pl.BlockSpec((N_GATES, 1), const2),
    ]
    return pl.pallas_call(
        _mixer_kernel,
        grid=(batch // nb, nc),
        in_specs=in_specs,
        out_specs=pl.BlockSpec((nb, L, RET_V + ML_V), tok),
        out_shape=jax.ShapeDtypeStruct((batch, seq, RET_V + ML_V), BF16),
        scratch_shapes=[
            pltpu.VMEM((nb, RET_QK, RET_V), F32),
            pltpu.VMEM((nb, ML_HEADS, ML_DK, ML_DV), F32),
            pltpu.VMEM((nb, ML_HEADS, 8, ML_DK), F32),
            pltpu.VMEM((nb, ML_HEADS, 8, 128), F32),
        ],
        compiler_params=pltpu.CompilerParams(
            dimension_semantics=("arbitrary", "arbitrary"), vmem_limit_bytes=VMEM_LIMIT),
        name="mixer",
    )(proj, g, gt, tabs["qdec"], tabs["kdec"], tabs["dmat"], tabs["bmask"],
      tabs["cdec"], tabs["hmask"], tabs["tril"].astype(BF16), tabs["triu"].astype(BF16), tabs["ones"].astype(BF16),
      ret_norm_w, ml_norm_w,
      gate_b.reshape(1, N_GATES), gate_b.reshape(N_GATES, 1)).reshape(n, RET_V + ML_V)


def _memkv_kernel(mem_ref, nw_ref, wkv_ref, k_ref, v_ref):
    d = mem_ref.shape[-1]
    mn = _rms(mem_ref[0], nw_ref[...]).astype(BF16)
    k_ref[0] = _dot(mn, wkv_ref[:, :d]).astype(BF16)
    v_ref[0] = _dot(mn, wkv_ref[:, d:]).astype(BF16)


def _memkv(mem, norm_w, wkv):
    b, m, d = mem.shape
    return pl.pallas_call(
        _memkv_kernel,
        grid=(b,),
        in_specs=[
            pl.BlockSpec((1, m, d), lambda i: (i, 0, 0)),
            pl.BlockSpec((1, d), lambda i: (0, 0)),
            pl.BlockSpec((d, 2 * d), lambda i: (0, 0)),
        ],
        out_specs=[pl.BlockSpec((1, m, d), lambda i: (i, 0, 0))] * 2,
        out_shape=[jax.ShapeDtypeStruct((b, m, d), BF16)] * 2,
        compiler_params=pltpu.CompilerParams(
            dimension_semantics=("arbitrary",), vmem_limit_bytes=VMEM_LIMIT),
        name="memkv",
    )(mem, norm_w, wkv)


def _attn_route_kernel(x_ref, mix_ref, k_ref, v_ref, wout_ref, nxa_ref, wq_ref, wo_ref, nmoe_ref,
                       wr_ref, wrlo_ref, br_ref, sut_ref,
                       x2_ref, h3_ref, ri_ref, rw_ref, cnt_ref, carry_ref):
    tm, d = x_ref.shape
    dh = d // XA_HEADS

    @pl.when((pl.program_id(0) == 0) & (pl.program_id(1) == 0))
    def _():
        carry_ref[...] = jnp.zeros_like(carry_ref)

    groups = [slice(g * (tm // ATTN_GROUPS), (g + 1) * (tm // ATTN_GROUPS)) for g in range(ATTN_GROUPS)]
    x1 = [x_ref[s, :] + _dot(mix_ref[s, :], wout_ref[...]) for s in groups]
    h2 = [_rms(t, nxa_ref[...]).astype(BF16) for t in x1]
    q = [_dot(t, wq_ref[...]).astype(BF16) for t in h2]
    o = []
    for qg in q:
        heads = []
        for h in range(XA_HEADS):
            logits = _dot_nt(qg[:, h * dh:(h + 1) * dh], k_ref[0, :, h * dh:(h + 1) * dh]) * (dh ** -0.5)
            mx = jnp.max(logits, axis=-1, keepdims=True)
            e = jnp.exp(logits - mx)
            p = (e / jnp.sum(e, axis=-1, keepdims=True)).astype(BF16)
            heads.append(_dot(p, v_ref[0, :, h * dh:(h + 1) * dh]).astype(BF16))
        o.append(jnp.concatenate(heads, axis=1))
    x2 = [a + _dot(b, wo_ref[...]) for a, b in zip(x1, o)]
    for s, t in zip(groups, x2):
        x2_ref[s, :] = t
    h3 = [_rms(t, nmoe_ref[...]) for t in x2]
    for s, t in zip(groups, h3):
        h3_ref[s, 0, :] = _pack_rows(t[:, :d // 2], t[:, d // 2:])

    lts = []
    for t in h3:
        t_hi = t.astype(BF16)
        t_lo = (t - t_hi.astype(F32)).astype(BF16)
        lts.append(_dot_nt(wr_ref[...], t_hi) + (_dot_nt(wr_ref[...], t_lo) + _dot_nt(wrlo_ref[...], t_hi)))
    lt = jnp.concatenate(lts, axis=1) + br_ref[...]
    gl = lt[N_EXPERTS:N_EXPERTS + N_GROUPS]
    gmax = jnp.max(gl, axis=0, keepdims=True)
    g_w = 1.0 / jnp.sum(jnp.exp(gl - gmax), axis=0, keepdims=True)
    giota = lax.broadcasted_iota(jnp.int32, gl.shape, 0)
    g_sel = jnp.min(jnp.where(gl == gmax, giota, N_GROUPS), axis=0, keepdims=True)
    el = lt[0:N_EXPERTS]
    eiota = lax.broadcasted_iota(jnp.int32, el.shape, 0)
    in_grp = (eiota // EXP_PER_GROUP) == g_sel
    elm = jnp.where(in_grp, el, -jnp.inf)
    m1 = jnp.max(elm, axis=0, keepdims=True)
    esum = jnp.sum(jnp.where(in_grp, jnp.exp(el - m1), 0.0), axis=0, keepdims=True)
    i1 = jnp.min(jnp.where(elm == m1, eiota, N_EXPERTS), axis=0, keepdims=True)
    elm2 = jnp.where(eiota == i1, -jnp.inf, elm)
    m2 = jnp.max(elm2, axis=0, keepdims=True)
    i2 = jnp.min(jnp.where(elm2 == m2, eiota, N_EXPERTS), axis=0, keepdims=True)
    p1 = 1.0 / esum
    p2 = jnp.exp(m2 - m1) / esum
    psum = p1 + p2
    w1 = g_w * (p1 / psum)
    w2 = g_w * (p2 / psum)

    oh1 = (eiota == i1).astype(F32)
    oh2 = (eiota == i2).astype(F32)
    cnt = oh1 + oh2
    base = carry_ref[:, 0:1] + _dot(cnt.astype(BF16), sut_ref[...])
    r1 = jnp.sum(oh1 * base, axis=0, keepdims=True)
    r2 = jnp.sum(oh2 * base, axis=0, keepdims=True)
    new_carry = carry_ref[...] + jnp.sum(cnt, axis=1, keepdims=True)
    carry_ref[...] = new_carry
    cnt_ref[...] = new_carry

    zi = jnp.zeros((4, tm), jnp.int32)
    ri_ref[...] = jnp.concatenate([i1, i2, r1.astype(jnp.int32), r2.astype(jnp.int32), zi], axis=0)
    rw_ref[...] = jnp.concatenate([w1, w2, jnp.zeros((6, tm), F32)], axis=0)


def _attn_route(xf, mixed, kmem, vmem, w_out, norm_xa_w, wq, wo, norm_moe_w, w_route_t, b_route, sut,
                batch, seq):
    n, d = xf.shape
    w_route_hi = w_route_t.astype(BF16)
    w_route_lo = (w_route_t - w_route_hi.astype(F32)).astype(BF16)
    tm = TOKEN_TILE
    nt = seq // tm
    m = kmem.shape[1]
    tok = lambda b, t: (b * nt + t, 0)
    lane_tok = lambda b, t: (0, b * nt + t)
    const2 = lambda b, t: (0, 0)
    return pl.pallas_call(
        _attn_route_kernel,
        grid=(batch, nt),
        in_specs=[
            pl.BlockSpec((tm, d), tok),
            pl.BlockSpec((tm, d), tok),
            pl.BlockSpec((1, m, d), lambda b, t: (b, 0, 0)),
            pl.BlockSpec((1, m, d), lambda b, t: (b, 0, 0)),
            pl.BlockSpec((d, d), const2),
            pl.BlockSpec((1, d), const2),
            pl.BlockSpec((d, d), const2),
            pl.BlockSpec((d, d), const2),
            pl.BlockSpec((1, d), const2),
            pl.BlockSpec((ROUTE_ROWS, d), const2),
            pl.BlockSpec((ROUTE_ROWS, d), const2),
            pl.BlockSpec((ROUTE_ROWS, 1), const2),
            pl.BlockSpec((tm, tm), const2),
        ],
        out_specs=[
            pl.BlockSpec((tm, d), tok),
            pl.BlockSpec((tm, 1, d // 2), lambda b, t: (b * nt + t, 0, 0)),
            pl.BlockSpec((8, tm), lane_tok),
            pl.BlockSpec((8, tm), lane_tok),
            pl.BlockSpec((N_EXPERTS, 128), const2),
        ],
        out_shape=[
            jax.ShapeDtypeStruct((n, d), F32),
            jax.ShapeDtypeStruct((n, 1, d // 2), jnp.uint32),
            jax.ShapeDtypeStruct((8, n), jnp.int32),
            jax.ShapeDtypeStruct((8, n), F32),
            jax.ShapeDtypeStruct((N_EXPERTS, 128), F32),
        ],
        scratch_shapes=[pltpu.VMEM((N_EXPERTS, 128), F32)],
        compiler_params=pltpu.CompilerParams(
            dimension_semantics=("arbitrary", "arbitrary"), vmem_limit_bytes=VMEM_LIMIT),
        name="attn_route",
    )(xf, mixed, kmem, vmem, w_out, norm_xa_w, wq, wo, norm_moe_w, w_route_hi, w_route_lo, b_route, sut)


def _dispatch_kernel(zpos_ref, dest_ref, h_ref, xs_ref, idx_ref, idx_sem, row_sem, zero_ref, zero_sem):
    i = pl.program_id(0)
    nsteps = pl.num_programs(0)
    td = h_ref.shape[0]
    bm = zero_ref.shape[0]
    slot = i % 2

    def idx_copy(step, sl):
        off = pl.multiple_of(sl * (2 * td), 2 * td)
        return pltpu.make_async_copy(dest_ref.at[step], idx_ref.at[pl.ds(off, 2 * td)], idx_sem.at[sl])

    @pl.when(i == 0)
    def _():
        zero_ref[...] = jnp.zeros_like(zero_ref)

        def zero_copy(e):
            return pltpu.make_async_copy(zero_ref, xs_ref.at[pl.ds(pl.multiple_of(zpos_ref[e], bm), bm), 0], zero_sem)

        def tail_copy(b):
            return pltpu.make_async_copy(zero_ref, xs_ref.at[pl.ds(pl.multiple_of(b * bm, bm), bm), 0], zero_sem)

        nused = zpos_ref[N_EXPERTS]
        nblk = xs_ref.shape[0] // bm
        for e in range(N_EXPERTS):
            pl.when(zpos_ref[e] >= 0)(lambda e=e: zero_copy(e).start())
        lax.fori_loop(nused, nblk, lambda b, c: (tail_copy(b).start(), c)[1], 0)
        for e in range(N_EXPERTS):
            pl.when(zpos_ref[e] >= 0)(lambda e=e: zero_copy(e).wait())
        lax.fori_loop(nused, nblk, lambda b, c: (tail_copy(b).wait(), c)[1], 0)
        idx_copy(0, 0).start()

    idx_copy(i, slot).wait()

    @pl.when(i + 1 < nsteps)
    def _():
        idx_copy(i + 1, 1 - slot).start()

    base = slot * (2 * td)

    for t in range(td):
        for k in range(TOP_K):
            pltpu.make_async_copy(h_ref.at[t], xs_ref.at[idx_ref[base + k * td + t]], row_sem).start(priority=t % 2)
    for _ in range(TOP_K):
        pltpu.make_async_copy(xs_ref.at[pl.ds(0, td)], xs_ref.at[pl.ds(0, td)], row_sem).wait()


def _dispatch(h3p, dest_tiles, zpos, cap):
    w = h3p.shape[-1]
    nt, td2 = dest_tiles.shape
    td = td2 // 2
    grid_spec = pltpu.PrefetchScalarGridSpec(
        num_scalar_prefetch=1,
        grid=(nt,),
        in_specs=[
            pl.BlockSpec(memory_space=pl.ANY),
            pl.BlockSpec((td, 1, w), lambda i, zp: (i, 0, 0)),
        ],
        out_specs=pl.BlockSpec(memory_space=pl.ANY),
        scratch_shapes=[
            pltpu.SMEM((2 * td2,), jnp.int32),
            pltpu.SemaphoreType.DMA((2,)),
            pltpu.SemaphoreType.DMA,
            pltpu.VMEM((MOE_ROWS, w), jnp.uint32),
            pltpu.SemaphoreType.DMA,
        ],
    )
    return pl.pallas_call(
        _dispatch_kernel,
        grid_spec=grid_spec,
        out_shape=jax.ShapeDtypeStruct((cap, 1, w), jnp.uint32),
        compiler_params=pltpu.CompilerParams(
            dimension_semantics=("arbitrary",), vmem_limit_bytes=VMEM_LIMIT),
        name="dispatch",
    )(zpos, dest_tiles, h3p)


def _expert_kernel(blk_e_ref, nused_ref, next_e_ref, xs_ref, wg_ref, wu_ref, wd_ref, ys_ref, wg_b, wu_b, wd_b,
                   wg_s, wu_s, wd_s, xbuf, ybuf, zbuf, in_sem, out_sem, zero_sem, w_sem):
    i = pl.program_id(0)
    nsteps = pl.num_programs(0)
    nused = nused_ref[0]
    bm = xbuf.shape[1]
    slot = i % 2
    prev = blk_e_ref[jnp.maximum(i - 1, 0)]
    fresh = (i < nused) & ((i == 0) | (blk_e_ref[i] != prev))
    half = wd_b.shape[1] // 2

    def w_copies(e):
        return [pltpu.make_async_copy(src.at[e], dst, w_sem.at[j])
                for j, (src, dst) in enumerate(((wg_ref, wg_s), (wu_ref, wu_s), (wd_ref, wd_s)))]

    def rows(ref, step):
        return ref.at[pl.ds(pl.multiple_of(step * bm, bm), bm), 0]

    def in_copy(step, sl):
        return pltpu.make_async_copy(rows(xs_ref, step), xbuf.at[sl], in_sem.at[sl])

    def out_copy(step, sl):
        return pltpu.make_async_copy(ybuf.at[sl], rows(ys_ref, step), out_sem.at[sl])

    def zero_copy(step):
        return pltpu.make_async_copy(zbuf, rows(ys_ref, step), zero_sem)

    @pl.when(i == 0)
    def _():
        zbuf[...] = jnp.zeros_like(zbuf)
        in_copy(0, 0).start()
        for c in w_copies(blk_e_ref[0]):
            c.start()

    @pl.when(i + 1 < nused)
    def _():
        in_copy(i + 1, 1 - slot).start()

    @pl.when(fresh)
    def _():
        for c in w_copies(blk_e_ref[i]):
            c.wait()
        wg_b[...] = wg_s[...].astype(BF16)
        wu_b[...] = wu_s[...].astype(BF16)
        wd_b[...] = wd_s[...].astype(BF16)

        @pl.when(next_e_ref[i] >= 0)
        def _():
            for c in w_copies(next_e_ref[i]):
                c.start()

    @pl.when(i < nused)
    def _():
        in_copy(i, slot).wait()
        pl.when(i >= 2)(lambda: out_copy(i - 2, slot).wait())
        lo, hi = _unpack_rows(xbuf[slot])
        xb = jnp.concatenate([lo.astype(BF16), hi.astype(BF16)], axis=1)
        hid = (_silu(_dot(xb, wg_b[...])) * _dot(xb, wu_b[...])).astype(BF16)
        y = _dot(hid, wd_b[...])
        ybuf[slot] = _pack_rows(y[:, :half], y[:, half:])
        out_copy(i, slot).start(priority=1)

    pl.when(i >= nused)(lambda: zero_copy(i).start())

    @pl.when(i == nsteps - 1)
    def _():
        pl.when(nused >= 2)(lambda: out_copy(nused - 2, nused % 2).wait())
        pl.when(nused >= 1)(lambda: out_copy(nused - 1, (nused - 1) % 2).wait())
        lax.fori_loop(nused, nsteps, lambda b, c: (zero_copy(b).wait(), c)[1], 0)


def _experts(xs, blk_e, nused, next_e, w_gate, w_up, w_down):
    cap, _, w = xs.shape
    _, d, de = w_gate.shape
    bm = MOE_ROWS
    grid_spec = pltpu.PrefetchScalarGridSpec(
        num_scalar_prefetch=3,
        grid=(cap // bm,),
        in_specs=[pl.BlockSpec(memory_space=pl.ANY)] * 4,
        out_specs=pl.BlockSpec(memory_space=pl.ANY),
        scratch_shapes=[
            pltpu.VMEM((d, de), BF16),
            pltpu.VMEM((d, de), BF16),
            pltpu.VMEM((de, d), BF16),
            pltpu.VMEM((d, de), F32),
            pltpu.VMEM((d, de), F32),
            pltpu.VMEM((de, d), F32),
            pltpu.VMEM((2, bm, w), jnp.uint32),
            pltpu.VMEM((2, bm, w), jnp.uint32),
            pltpu.VMEM((bm, w), jnp.uint32),
            pltpu.SemaphoreType.DMA((2,)),
            pltpu.SemaphoreType.DMA((2,)),
            pltpu.SemaphoreType.DMA,
            pltpu.SemaphoreType.DMA((3,)),
        ],
    )
    return pl.pallas_call(
        _expert_kernel,
        grid_spec=grid_spec,
        out_shape=jax.ShapeDtypeStruct((cap, 1, w), jnp.uint32),
        compiler_params=pltpu.CompilerParams(
            dimension_semantics=("arbitrary",), vmem_limit_bytes=VMEM_LIMIT),
        name="experts",
    )(blk_e, nused, next_e, xs, w_gate, w_up, w_down)


def _combine_kernel(dest_ref, ys_ref, x2_ref, rw_ref, eye_ref, nw_ref, o_ref, idx_ref, idx_sem, ybuf, ysem):
    i = pl.program_id(0)
    nsteps = pl.num_programs(0)
    tc, d = x2_ref.shape
    half = d // 2
    n_idx = 2 * tc

    def idx_copy(step):
        sl = step % 3
        off = pl.multiple_of(sl * n_idx, n_idx)
        return pltpu.make_async_copy(dest_ref.at[step], idx_ref.at[pl.ds(off, n_idx)], idx_sem.at[sl])

    def gather(step):
        base = (step % 3) * n_idx
        buf = ybuf.at[step % 2]
        sem = ysem.at[step % 2]

        for t in range(n_idx):
            pltpu.make_async_copy(ys_ref.at[idx_ref[base + t]], buf.at[pl.ds(t, 1)], sem).start(priority=t % 2)

    @pl.when(i == 0)
    def _():
        idx_copy(0).start()
        idx_copy(0).wait()
        gather(0)

        @pl.when(nsteps > 1)
        def _():
            idx_copy(1).start()

    @pl.when(i + 1 < nsteps)
    def _():
        idx_copy(i + 1).wait()

        @pl.when(i + 2 < nsteps)
        def _():
            idx_copy(i + 2).start()

        gather(i + 1)

    slot = i % 2
    pltpu.make_async_copy(ybuf.at[slot], ybuf.at[slot], ysem.at[slot]).wait()
    eye = eye_ref[...]
    ew = eye.shape[0]

    def to_columns(r):
        r_a = r.astype(BF16)
        r_b = (r - r_a.astype(F32)).astype(BF16)
        r_c = (r - r_a.astype(F32) - r_b.astype(F32)).astype(BF16)
        return _dot_nt(eye, r_a) + (_dot_nt(eye, r_b) + _dot_nt(eye, r_c))

    wcol = jnp.concatenate([to_columns(rw_ref[:, j:j + ew]) for j in range(0, tc, ew)], axis=0)
    lo1, hi1 = _unpack_rows(ybuf[slot, 0:tc])
    lo2, hi2 = _unpack_rows(ybuf[slot, tc:n_idx])
    w1 = wcol[:, 0:1]
    w2 = wcol[:, 1:2]
    z_lo = x2_ref[:, :half] + (lo1 * w1 + lo2 * w2)
    z_hi = x2_ref[:, half:] + (hi1 * w1 + hi2 * w2)
    ms = (jnp.sum(z_lo * z_lo, axis=-1, keepdims=True) + jnp.sum(z_hi * z_hi, axis=-1, keepdims=True)) / d
    scale = lax.rsqrt(ms + EPS)
    o_ref[:, :half] = z_lo * scale * nw_ref[:, :half]
    o_ref[:, half:] = z_hi * scale * nw_ref[:, half:]


def _combine(x2, ys, dest_tiles, rw, eye, norm_w):
    n, d = x2.shape
    nt, n_idx = dest_tiles.shape
    tc = n_idx // 2
    w = ys.shape[-1]
    return pl.pallas_call(
        _combine_kernel,
        grid=(nt,),
        in_specs=[pl.BlockSpec(memory_space=pl.ANY)] * 2 + [
            pl.BlockSpec((tc, d), lambda i: (i, 0)),
            pl.BlockSpec((8, tc), lambda i: (0, i)),
            pl.BlockSpec(eye.shape, lambda i: (0, 0)),
            pl.BlockSpec((1, d), lambda i: (0, 0)),
        ],
        out_specs=pl.BlockSpec((tc, d), lambda i: (i, 0)),
        out_shape=jax.ShapeDtypeStruct((n, d), F32),
        scratch_shapes=[
            pltpu.SMEM((3 * n_idx,), jnp.int32),
            pltpu.SemaphoreType.DMA((3,)),
            pltpu.VMEM((2, n_idx, w), jnp.uint32),
            pltpu.SemaphoreType.DMA((2,)),
        ],
        compiler_params=pltpu.CompilerParams(
            dimension_semantics=("arbitrary",), vmem_limit_bytes=VMEM_LIMIT),
        name="combine",
    )(dest_tiles, ys, x2, rw, eye, norm_w)


def _layer(xf, mem, batch, seq, norm_mix_w, w_in, ret_norm_w, ml_conv_w, ml_conv_b, ml_gate_b, ml_norm_w,
           w_out, norm_xa_w, norm_mem_w, xa_wq, xa_wkv, xa_wo, norm_moe_w, moe_w_group, moe_b_group,
           moe_w_router, moe_b_router, moe_w_gate, moe_w_up, moe_w_down, final_norm_w):
    n, d = xf.shape
    def halves_first(w):
        return w.reshape(d, RET_HEADS, 2, RET_DK // 2).transpose(0, 2, 1, 3).reshape(d, RET_QK)

    w_main = jnp.concatenate([halves_first(w_in[:, OFF_RQ:OFF_RQ + RET_QK]),
                              halves_first(w_in[:, OFF_RK:OFF_RK + RET_QK]),
                              w_in[:, OFF_RV:MAIN_WIDTH]], axis=1).astype(BF16)
    w_if = w_in[:, MAIN_WIDTH:].astype(BF16)
    tabs = {k_: jnp.asarray(v_) for k_, v_ in _mixer_tables(seq).items()}
    w_if_pad = jnp.pad(w_if, ((0, 0), (0, GATE_LANES - N_GATES)))
    proj, g, gt = _inproj(xf, norm_mix_w.reshape(1, d), w_main, w_if_pad, w_if.T, tabs["cos"], tabs["sin"],
                          ml_conv_w, ml_conv_b.reshape(1, 2 * ML_QK), seq)
    mixed = _mixer(proj, g, gt, tabs, ret_norm_w.reshape(1, RET_V), ml_norm_w.reshape(1, ML_V), ml_gate_b,
                   batch, seq)

    kmem, vmem = _memkv(mem, norm_mem_w.reshape(1, d), xa_wkv.astype(BF16))

    w_route_t = jnp.concatenate(
        [moe_w_router.T, moe_w_group.T, jnp.zeros((ROUTE_ROWS - N_EXPERTS - N_GROUPS, d), F32)], axis=0)
    b_route = jnp.concatenate(
        [moe_b_router, moe_b_group, jnp.zeros((ROUTE_ROWS - N_EXPERTS - N_GROUPS,), F32)]).reshape(ROUTE_ROWS, 1)
    tm = TOKEN_TILE
    sut = jnp.asarray(np.triu(np.ones((tm, tm), np.float32), 1), dtype=BF16)
    x2, h3, ri, rw, cnt = _attn_route(xf, mixed, kmem, vmem, w_out.astype(BF16), norm_xa_w.reshape(1, d),
                                      xa_wq.astype(BF16), xa_wo.astype(BF16), norm_moe_w.reshape(1, d),
                                      w_route_t, b_route, sut, batch, seq)

    bm = MOE_ROWS
    counts = cnt[:, 0].astype(jnp.int32)
    padded = (counts + bm - 1) // bm * bm
    pends = jnp.cumsum(padded)
    pstarts = pends - padded
    expert = ri[0:TOP_K]
    onehot = expert[None] == jnp.arange(N_EXPERTS, dtype=jnp.int32)[:, None, None]
    dest = jnp.sum(jnp.where(onehot, pstarts[:, None, None], 0), axis=0) + ri[TOP_K:2 * TOP_K]
    cap = n * TOP_K + N_EXPERTS * bm
    nblk = cap // bm
    blk_start = jnp.arange(nblk, dtype=jnp.int32) * bm
    blk_e = jnp.minimum(jnp.sum(blk_start[:, None] >= pends[None, :], axis=1), N_EXPERTS - 1).astype(jnp.int32)
    nused = (pends[-1] // bm).astype(jnp.int32).reshape(1)
    zpos = jnp.where(padded > counts, pends - bm, -1).astype(jnp.int32)
    zpos = jnp.concatenate([zpos, nused])

    def tiles(rows):
        return dest.reshape(TOP_K, n // rows, rows).transpose(1, 0, 2).reshape(n // rows, TOP_K * rows)

    eids = jnp.arange(N_EXPERTS, dtype=jnp.int32)
    later = jnp.where((eids[None, :] > eids[:, None]) & (padded[None, :] > 0), eids[None, :], N_EXPERTS)
    next_of = jnp.min(later, axis=1)
    next_of = jnp.where(next_of == N_EXPERTS, -1, next_of)
    next_e = jnp.sum(jnp.where(blk_e[:, None] == eids[None, :], next_of[None, :], 0), axis=1).astype(jnp.int32)

    xs = _dispatch(h3, tiles(DISPATCH_TILE), zpos, cap)
    ys = _experts(xs, blk_e, nused, next_e, moe_w_gate, moe_w_up, moe_w_down)
    eye = jnp.asarray(np.eye(COMBINE_EYE, dtype=np.float32), dtype=BF16)
    return _combine(x2, ys, tiles(COMBINE_TILE), rw, eye, final_norm_w.reshape(1, d))


def kernel(x, mem, norm_mix_w, w_in, ret_norm_w, ml_conv_w, ml_conv_b, ml_gate_b, ml_norm_w, w_out, norm_xa_w, norm_mem_w, xa_wq, xa_wkv, xa_wo, norm_moe_w, moe_w_group, moe_b_group, moe_w_router, moe_b_router, moe_w_gate, moe_w_up, moe_w_down, norm_final_w):
    batch, seq, d = x.shape
    depth = w_in.shape[0]
    assert depth == 1, "the final norm is fused into the last layer's combine kernel"
    l = 0
    out = _layer(x.reshape(batch * seq, d), mem, batch, seq, norm_mix_w[l], w_in[l], ret_norm_w[l], ml_conv_w[l],
                 ml_conv_b[l], ml_gate_b[l], ml_norm_w[l], w_out[l], norm_xa_w[l], norm_mem_w[l], xa_wq[l],
                 xa_wkv[l], xa_wo[l], norm_moe_w[l], moe_w_group[l], moe_b_group[l], moe_w_router[l],
                 moe_b_router[l], moe_w_gate[l], moe_w_up[l], moe_w_down[l], norm_final_w)
    return out.reshape(batch, seq, d)
```

```python
import functools

import numpy as np
import jax
import jax.numpy as jnp
from jax import lax
from jax.experimental import pallas as pl
from jax.experimental.pallas import tpu as pltpu

F32 = jnp.float32
BF16 = jnp.bfloat16

CHUNK = 128
RET_HEADS = 4
RET_DK = 64
RET_DV = 128
ML_HEADS = 4
ML_DK = 128
ML_DV = 128
CONV_W = 4
XA_HEADS = 4
N_GROUPS = 4
EXP_PER_GROUP = 8
N_EXPERTS = N_GROUPS * EXP_PER_GROUP
TOP_K = 2
ROPE_BASE = 10000.0
EPS = 1e-6

RET_QK = RET_HEADS * RET_DK
RET_V = RET_HEADS * RET_DV
ML_QK = ML_HEADS * ML_DK
ML_V = ML_HEADS * ML_DV
OFF_RQ = 0
OFF_RK = OFF_RQ + RET_QK
OFF_RV = OFF_RK + RET_QK
OFF_RG = OFF_RV + RET_V
OFF_MQK = OFF_RG + RET_V
OFF_MV = OFF_MQK + 2 * ML_QK
OFF_MO = OFF_MV + ML_V
MAIN_WIDTH = OFF_MO + ML_V
N_GATES = 2 * ML_HEADS
GATE_LANES = 128

ROUTE_ROWS = 40
TOKEN_TILE = 1024
MOE_ROWS = 512
DISPATCH_TILE = 1024
COMBINE_TILE = 512
CONV_ROWS = 128
COMBINE_EYE = 512
MIXER_BATCHES = 8
ATTN_GROUPS = 2
VMEM_LIMIT = 56 * 1024 * 1024


def _dot(a, b):
    return jnp.dot(a, b, preferred_element_type=F32)


def _dot_nt(a, b):
    return lax.dot_general(a, b, (((1,), (1,)), ((), ())), preferred_element_type=F32)


def _dot_tn(a, b):
    return lax.dot_general(a, b, (((0,), (0,)), ((), ())), preferred_element_type=F32)


def _rms(x, w):
    return x * lax.rsqrt(jnp.mean(x * x, axis=-1, keepdims=True) + EPS) * w


def _sigmoid(x):
    return 1.0 / (1.0 + jnp.exp(-x))


def _silu(x):
    return x * _sigmoid(x)


def _log_sigmoid(x):
    return jnp.minimum(x, 0.0) - jnp.log1p(jnp.exp(-jnp.abs(x)))


def _head_norms(ts, mean_w):
    mu = [_dot(t.astype(BF16), mean_w) for t in ts]
    dl = [t - m for t, m in zip(ts, mu)]
    var = [_dot((d * d).astype(BF16), mean_w) for d in dl]
    return [d * lax.rsqrt(v + EPS) for d, v in zip(dl, var)]


def _pack_rows(lo, hi):
    def bits(t):
        return lax.bitcast_convert_type(t.astype(BF16), jnp.uint16).astype(jnp.uint32)
    return bits(lo) | (bits(hi) << 16)


def _unpack_rows(u):
    lo = lax.bitcast_convert_type(u << 16, F32)
    hi = lax.bitcast_convert_type(u & jnp.uint32(0xFFFF0000), F32)
    return lo, hi


def _inproj_kernel(tiles_per_seq, x_ref, nw_ref, w_ref, wif_ref, wift_ref, cos_ref, sin_ref, convw_ref, convb_ref,
                   proj_ref, g_ref, gt_ref, carry_ref):
    tm = x_ref.shape[0]

    @pl.when(pl.program_id(0) == 0)
    def _():
        carry_ref[...] = jnp.zeros_like(carry_ref)

    h = _rms(x_ref[...], nw_ref[...]).astype(BF16)

    def mm(off, width):
        return _dot(h, w_ref[:, off:off + width])

    def rotary(qk):
        cos = cos_ref[...]
        sin = sin_ref[...]
        half = RET_QK // 2
        for off, scale in ((OFF_RQ, None), (OFF_RK, RET_DK ** -0.5)):
            t1 = qk[:, off - OFF_RQ:off - OFF_RQ + half]
            t2 = qk[:, off - OFF_RQ + half:off - OFF_RQ + 2 * half]
            r1 = t1 * cos - t2 * sin
            r2 = t1 * sin + t2 * cos
            if scale is not None:
                r1, r2 = r1 * scale, r2 * scale
            proj_ref[:, off:off + half] = r1.astype(BF16)
            proj_ref[:, off + half:off + 2 * half] = r2.astype(BF16)

    def conv_silu(part, scale):
        c0 = part * ML_QK
        first = (pl.program_id(0) % tiles_per_seq) == 0
        row8 = lax.broadcasted_iota(jnp.int32, (8, ML_QK), 0)
        prev = jnp.where(first, 0.0, carry_ref[:, c0:c0 + ML_QK])
        for r0 in range(0, tm, CONV_ROWS):
            cur = _dot(h[r0:r0 + CONV_ROWS], w_ref[:, OFF_MQK + c0:OFF_MQK + c0 + ML_QK])
            acc = cur * convw_ref[CONV_W - 1:CONV_W, c0:c0 + ML_QK] + convb_ref[:, c0:c0 + ML_QK]
            for s in range(1, CONV_W):
                rolled = pltpu.roll(cur, s, 0)
                head8 = jnp.where(row8 < s, pltpu.roll(prev, s, 0), rolled[0:8])
                shifted = jnp.concatenate([head8, rolled[8:]], axis=0)
                acc = acc + shifted * convw_ref[CONV_W - 1 - s:CONV_W - s, c0:c0 + ML_QK]
            prev = cur[CONV_ROWS - 8:CONV_ROWS]
            act = _silu(acc) if scale is None else _silu(acc) * scale
            proj_ref[r0:r0 + CONV_ROWS, OFF_MQK + c0:OFF_MQK + c0 + ML_QK] = act.astype(BF16)
        carry_ref[:, c0:c0 + ML_QK] = prev

    def store(off, width, fn=None):
        def ep(t):
            proj_ref[:, off:off + width] = (t if fn is None else fn(t)).astype(BF16)
        return ep

    rotary(mm(OFF_RQ, 2 * RET_QK))
    store(OFF_RV, RET_V)(mm(OFF_RV, RET_V))
    store(OFF_RG, RET_V, _silu)(mm(OFF_RG, RET_V))
    conv_silu(0, None)
    conv_silu(1, ML_DK ** -0.5)
    store(OFF_MV, ML_V)(mm(OFF_MV, ML_V))
    store(OFF_MO, ML_V, _sigmoid)(mm(OFF_MO, ML_V))
    g_ref[...] = _dot(h, wif_ref[...])
    gt_ref[...] = _dot_nt(wift_ref[...], h)


def _inproj(xf, norm_w, w_main, w_if, w_ift, cos, sin, conv_w, conv_b, seq):
    n, d = xf.shape
    tm = TOKEN_TILE
    tiles_per_seq = seq // tm
    return pl.pallas_call(
        functools.partial(_inproj_kernel, tiles_per_seq),
        grid=(n // tm,),
        in_specs=[
            pl.BlockSpec((tm, d), lambda i: (i, 0)),
            pl.BlockSpec((1, d), lambda i: (0, 0)),
            pl.BlockSpec((d, MAIN_WIDTH), lambda i: (0, 0)),
            pl.BlockSpec((d, GATE_LANES), lambda i: (0, 0)),
            pl.BlockSpec((N_GATES, d), lambda i: (0, 0)),
            pl.BlockSpec((tm, RET_QK // 2), lambda i: (i % tiles_per_seq, 0)),
            pl.BlockSpec((tm, RET_QK // 2), lambda i: (i % tiles_per_seq, 0)),
            pl.BlockSpec((CONV_W, 2 * ML_QK), lambda i: (0, 0)),
            pl.BlockSpec((1, 2 * ML_QK), lambda i: (0, 0)),
        ],
        out_specs=[
            pl.BlockSpec((tm, MAIN_WIDTH), lambda i: (i, 0)),
            pl.BlockSpec((tm, GATE_LANES), lambda i: (i, 0)),
            pl.BlockSpec((N_GATES, tm), lambda i: (0, i)),
        ],
        out_shape=[
            jax.ShapeDtypeStruct((n, MAIN_WIDTH), BF16),
            jax.ShapeDtypeStruct((n, GATE_LANES), F32),
            jax.ShapeDtypeStruct((N_GATES, n), F32),
        ],
        scratch_shapes=[pltpu.VMEM((8, 2 * ML_QK), F32)],
        compiler_params=pltpu.CompilerParams(
            dimension_semantics=("arbitrary",), vmem_limit_bytes=VMEM_LIMIT),
        name="inproj",
    )(xf, norm_w, w_main, w_if, w_ift, cos, sin, conv_w, conv_b)


def _mixer_kernel(*refs):
    proj_hbm, pbuf, psem = refs[0], refs[-2], refs[-1]
    refs = refs[1:-2]
    nb = refs[0].shape[0]
    nc = pl.num_programs(1)
    step = pl.program_id(0) * nc + pl.program_id(1)
    nsteps = pl.num_programs(0) * nc

    def fetch(s):
        rows = pl.ds((s // nc) * nb, nb)
        cols = pl.ds(pl.multiple_of((s % nc) * CHUNK, CHUNK), CHUNK)
        return pltpu.make_async_copy(proj_hbm.at[rows, cols], pbuf.at[s % 3], psem.at[s % 3])

    @pl.when(step == 0)
    def _():
        fetch(0).start()
        pl.when(nsteps > 1)(lambda: fetch(1).start())

    pl.when(step + 2 < nsteps)(lambda: fetch(step + 2).start())
    fetch(step).wait()

    @pl.when(pl.program_id(1) == 0)
    def _():
        for state_ref in refs[-4:]:
            state_ref[...] = jnp.zeros_like(state_ref)

    for bi in range(nb):
        _mixer_one(bi, pbuf.at[step % 3], *refs)


def _mixer_one(bi, proj_ref, g_ref, gt_ref, qdec_ref, kdec_ref, dmat_ref,
               bmask_ref, cdec_ref, hmask_ref, tril_ref, triu_ref, ones_ref, retw_ref, mlw_ref, gbc_ref, gbr_ref,
               out_ref, r_ref, c_ref, n_ref, m_ref):
    L = CHUNK
    proj_ref, g_ref, gt_ref, out_ref = proj_ref.at[bi], g_ref.at[bi], gt_ref.at[bi], out_ref.at[bi]
    r_ref, c_ref, n_ref, m_ref = r_ref.at[bi], c_ref.at[bi], n_ref.at[bi], m_ref.at[bi]
    mean_w = ones_ref[...]

    q = proj_ref[:, OFF_RQ:OFF_RQ + RET_QK].astype(F32)
    k_b = proj_ref[:, OFF_RK:OFF_RK + RET_QK]
    k = k_b.astype(F32)
    v = proj_ref[:, OFF_RV:OFF_RV + RET_V]
    r_prev = r_ref[...]
    cross = _dot((q * qdec_ref[...]).astype(BF16), r_prev.astype(BF16))
    kv = _dot_tn((k * kdec_ref[...]).astype(BF16), v) * bmask_ref[...]
    r_ref[...] = cdec_ref[...] * r_prev + kv
    RH = range(RET_HEADS)
    sc = [_dot_nt((q * hmask_ref[h:h + 1, :]).astype(BF16), k_b) for h in RH]
    sc = [(sc[h] * dmat_ref[h]).astype(BF16) for h in RH]
    tot = [_dot(sc[h], v[:, h * RET_DV:(h + 1) * RET_DV]) + cross[:, h * RET_DV:(h + 1) * RET_DV] for h in RH]
    ret = jnp.concatenate(_head_norms(tot, mean_w), axis=1) * retw_ref[...]
    ret = ret * proj_ref[:, OFF_RG:OFF_RG + RET_V].astype(F32)
    out_ref[:, 0:RET_V] = ret.astype(BF16)

    mq = proj_ref[:, OFF_MQK:OFF_MQK + ML_QK]
    mk = proj_ref[:, OFF_MQK + ML_QK:OFF_MQK + 2 * ML_QK]
    mv = proj_ref[:, OFF_MV:OFF_MV + ML_V]

    gc = g_ref[:, 0:N_GATES] + gbc_ref[...]
    gr = gt_ref[...] + gbr_ref[...]
    lf_c = _log_sigmoid(gc)
    lf_r = _log_sigmoid(gr)
    lf_c_hi = lf_c.astype(BF16)
    lf_r_hi = lf_r.astype(BF16)
    b_c = (_dot(tril_ref[...], lf_c_hi)
           + _dot(tril_ref[...], (lf_c - lf_c_hi.astype(F32)).astype(BF16)))
    b_r = (_dot(lf_r_hi, triu_ref[...])
           + _dot((lf_r - lf_r_hi.astype(F32)).astype(BF16), triu_ref[...]))
    causal = (lax.broadcasted_iota(jnp.int32, (L, L), 0) >= lax.broadcasted_iota(jnp.int32, (L, L), 1))
    MH = range(ML_HEADS)
    bc = [b_c[:, ML_HEADS + h:ML_HEADS + h + 1] for h in MH]
    br = [b_r[ML_HEADS + h:ML_HEADS + h + 1, :] for h in MH]
    igc = [gc[:, h:h + 1] for h in MH]
    igr = [gr[h:h + 1, :] for h in MH]
    btot = [br[h][:, L - 1:L] for h in MH]
    qh_b = [mq[:, h * ML_DK:(h + 1) * ML_DK] for h in MH]
    kh_b = [mk[:, h * ML_DK:(h + 1) * ML_DK] for h in MH]
    vh = [mv[:, h * ML_DV:(h + 1) * ML_DV] for h in MH]
    c_prev = [c_ref[h] for h in MH]
    n_prev = [n_ref[h][0:1, :] for h in MH]
    m_prev = [m_ref[h][0:1, 0:1] for h in MH]
    s_raw = [_dot_nt(qh_b[h], kh_b[h]) for h in MH]
    qc = [_dot(qh_b[h], c_prev[h].astype(BF16)) for h in MH]
    log_d = [jnp.where(causal, bc[h] - br[h] + igr[h], -jnp.inf) for h in MH]
    m_intra = [jnp.max(log_d[h], axis=1, keepdims=True) for h in MH]
    m_loc = [jnp.max(btot[h] - br[h] + igr[h], axis=1, keepdims=True) for h in MH]
    kw = [kh_b[h].astype(F32) * jnp.exp(btot[h] - bc[h] + igc[h] - m_loc[h]) for h in MH]
    kv_loc = [_dot_tn(kw[h].astype(BF16), vh[h]) for h in MH]
    n_loc = [jnp.sum(kw[h], axis=0, keepdims=True) for h in MH]
    m_inter = [bc[h] + m_prev[h] for h in MH]
    m_t = [jnp.maximum(m_intra[h], m_inter[h]) for h in MH]
    s_mat = [s_raw[h] * jnp.exp(log_d[h] - m_t[h]) for h in MH]
    inter = [jnp.exp(m_inter[h] - m_t[h]) for h in MH]
    num = [_dot(s_mat[h].astype(BF16), vh[h]) + inter[h] * qc[h] for h in MH]
    den = [jnp.sum(s_mat[h], axis=1, keepdims=True)
           + inter[h] * jnp.sum(qh_b[h].astype(F32) * n_prev[h], axis=1, keepdims=True) for h in MH]
    hh = [num[h] / jnp.maximum(jnp.abs(den[h]), jnp.exp(-m_t[h])) for h in MH]
    for h in MH:
        m_new = jnp.maximum(btot[h] + m_prev[h], m_loc[h])
        s_old = jnp.exp(btot[h] + m_prev[h] - m_new)
        s_loc = jnp.exp(m_loc[h] - m_new)
        c_ref[h] = s_old * c_prev[h] + s_loc * kv_loc[h]
        n_ref[h] = jnp.broadcast_to(s_old * n_prev[h] + s_loc * n_loc[h], (8, ML_DK))
        m_ref[h] = jnp.broadcast_to(m_new, (8, 128))
    ml = jnp.concatenate(_head_norms(hh, mean_w), axis=1) * mlw_ref[...]
    ml = ml * proj_ref[:, OFF_MO:OFF_MO + ML_V].astype(F32)
    out_ref[:, RET_V:RET_V + ML_V] = ml.astype(BF16)


def _mixer_tables(seq):
    L = CHUNK
    half = RET_DK // 2
    inv = ROPE_BASE ** (-np.arange(half, dtype=np.float64) / half)
    ang = np.arange(seq, dtype=np.float64)[:, None] * inv[None, :].astype(np.float32).astype(np.float64)
    cos = np.tile(np.cos(ang), (1, RET_HEADS)).astype(np.float32)
    sin = np.tile(np.sin(ang), (1, RET_HEADS)).astype(np.float32)
    log_g = np.log1p(-np.exp2(-5.0 - np.arange(RET_HEADS, dtype=np.float64)))
    n = np.arange(L, dtype=np.float64)
    lane_head = (np.arange(RET_QK) % (RET_QK // 2)) // half
    qdec = np.exp((n + 1)[:, None] * log_g[lane_head][None, :]).astype(np.float32)
    kdec = np.exp((L - 1 - n)[:, None] * log_g[lane_head][None, :]).astype(np.float32)
    diff = n[:, None] - n[None, :]
    dmat = np.where(diff >= 0, np.exp(log_g[:, None, None] * np.maximum(diff, 0.0)[None]), 0.0).astype(np.float32)
    col_head = np.arange(RET_V) // RET_DV
    bmask = (lane_head[:, None] == col_head[None, :]).astype(np.float32)
    cdec = np.exp(L * log_g[col_head])[None, :].astype(np.float32)
    hmask = (lane_head[None, :] == np.arange(RET_HEADS)[:, None]).astype(np.float32)
    hmask = np.concatenate([hmask, np.zeros((8 - RET_HEADS, RET_QK), np.float32)], axis=0)
    tril = np.tril(np.ones((L, L), np.float32))
    ones = np.full((RET_DV, RET_DV), 1.0 / RET_DV, np.float32)
    return dict(cos=cos, sin=sin, qdec=qdec, kdec=kdec, dmat=dmat, bmask=bmask, cdec=cdec, hmask=hmask,
                tril=tril, triu=np.ascontiguousarray(tril.T), ones=ones)


def _mixer(proj, g, gt, tabs, ret_norm_w, ml_norm_w, gate_b, batch, seq):
    L = CHUNK
    nc = seq // L
    n = batch * seq
    nb = MIXER_BATCHES if batch % MIXER_BATCHES == 0 else 1
    proj = proj.reshape(batch, seq, MAIN_WIDTH)
    g = g.reshape(batch, seq, GATE_LANES)
    gt = gt.reshape(N_GATES, batch, seq).transpose(1, 0, 2)
    const2 = lambda b, c: (0, 0)
    const3 = lambda b, c: (0, 0, 0)
    tok = lambda b, c: (b, c, 0)
    in_specs = [
        pl.BlockSpec(memory_space=pl.ANY),
        pl.BlockSpec((nb, L, GATE_LANES), tok),
        pl.BlockSpec((nb, N_GATES, L), lambda b, c: (b, 0, c)),
        pl.BlockSpec((L, RET_QK), const2),
        pl.BlockSpec((L, RET_QK), const2),
        pl.BlockSpec((RET_HEADS, L, L), const3),
        pl.BlockSpec((RET_QK, RET_V), const2),
        pl.BlockSpec((1, RET_V), const2),
        pl.BlockSpec((8, RET_QK), const2),
        pl.BlockSpec((L, L), const2),
        pl.BlockSpec((L, L), const2),
        pl.BlockSpec((RET_DV, RET_DV), const2),
        pl.BlockSpec((1, RET_V), const2),
        pl.BlockSpec((1, ML_V), const2),
        pl.BlockSpec((1, N_GATES), const2),
        pl.BlockSpec((N_GATES, 1), const2),
    ]
    return pl.pallas_call(
        _mixer_kernel,
        grid=(batch // nb, nc),
        in_specs=in_specs,
        out_specs=pl.BlockSpec((nb, L, RET_V + ML_V), tok),
        out_shape=jax.ShapeDtypeStruct((batch, seq, RET_V + ML_V), BF16),
        scratch_shapes=[
            pltpu.VMEM((nb, RET_QK, RET_V), F32),
            pltpu.VMEM((nb, ML_HEADS, ML_DK, ML_DV), F32),
            pltpu.VMEM((nb, ML_HEADS, 8, ML_DK), F32),
            pltpu.VMEM((nb, ML_HEADS, 8, 128), F32),
            pltpu.VMEM((3, nb, L, MAIN_WIDTH), BF16),
            pltpu.SemaphoreType.DMA((3,)),
        ],
        compiler_params=pltpu.CompilerParams(
            dimension_semantics=("arbitrary", "arbitrary"), vmem_limit_bytes=VMEM_LIMIT),
        name="mixer",
    )(proj, g, gt, tabs["qdec"], tabs["kdec"], tabs["dmat"], tabs["bmask"],
      tabs["cdec"], tabs["hmask"], tabs["tril"].astype(BF16), tabs["triu"].astype(BF16), tabs["ones"].astype(BF16),
      ret_norm_w, ml_norm_w,
      gate_b.reshape(1, N_GATES), gate_b.reshape(N_GATES, 1)).reshape(n, RET_V + ML_V)


def _memkv_kernel(mem_ref, nw_ref, wkv_ref, k_ref, v_ref):
    d = mem_ref.shape[-1]
    mn = _rms(mem_ref[0], nw_ref[...]).astype(BF16)
    k_ref[0] = _dot(mn, wkv_ref[:, :d]).astype(BF16)
    v_ref[0] = _dot(mn, wkv_ref[:, d:]).astype(BF16)


def _memkv(mem, norm_w, wkv):
    b, m, d = mem.shape
    return pl.pallas_call(
        _memkv_kernel,
        grid=(b,),
        in_specs=[
            pl.BlockSpec((1, m, d), lambda i: (i, 0, 0)),
            pl.BlockSpec((1, d), lambda i: (0, 0)),
            pl.BlockSpec((d, 2 * d), lambda i: (0, 0)),
        ],
        out_specs=[pl.BlockSpec((1, m, d), lambda i: (i, 0, 0))] * 2,
        out_shape=[jax.ShapeDtypeStruct((b, m, d), BF16)] * 2,
        compiler_params=pltpu.CompilerParams(
            dimension_semantics=("arbitrary",), vmem_limit_bytes=VMEM_LIMIT),
        name="memkv",
    )(mem, norm_w, wkv)


def _attn_route_kernel(x_ref, mix_ref, k_ref, v_ref, wout_ref, nxa_ref, wq_ref, wo_ref, nmoe_ref,
                       wr_ref, wrlo_ref, br_ref, sut_ref,
                       x2_ref, h3_ref, ri_ref, rw_ref, cnt_ref, carry_ref):
    tm, d = x_ref.shape
    dh = d // XA_HEADS

    @pl.when((pl.program_id(0) == 0) & (pl.program_id(1) == 0))
    def _():
        carry_ref[...] = jnp.zeros_like(carry_ref)

    groups = [slice(g * (tm // ATTN_GROUPS), (g + 1) * (tm // ATTN_GROUPS)) for g in range(ATTN_GROUPS)]
    x1 = [x_ref[s, :] + _dot(mix_ref[s, :], wout_ref[...]) for s in groups]
    h2 = [_rms(t, nxa_ref[...]).astype(BF16) for t in x1]
    q = [_dot(t, wq_ref[...]).astype(BF16) for t in h2]
    o = []
    for qg in q:
        heads = []
        for h in range(XA_HEADS):
            logits = _dot_nt(qg[:, h * dh:(h + 1) * dh], k_ref[0, :, h * dh:(h + 1) * dh]) * (dh ** -0.5)
            mx = jnp.max(logits, axis=-1, keepdims=True)
            e = jnp.exp(logits - mx)
            p = (e / jnp.sum(e, axis=-1, keepdims=True)).astype(BF16)
            heads.append(_dot(p, v_ref[0, :, h * dh:(h + 1) * dh]).astype(BF16))
        o.append(jnp.concatenate(heads, axis=1))
    x2 = [a + _dot(b, wo_ref[...]) for a, b in zip(x1, o)]
    for s, t in zip(groups, x2):
        x2_ref[s, :] = t
    h3 = [_rms(t, nmoe_ref[...]) for t in x2]
    for s, t in zip(groups, h3):
        h3_ref[s, 0, :] = _pack_rows(t[:, :d // 2], t[:, d // 2:])

    lts = []
    for t in h3:
        t_hi = t.astype(BF16)
        t_lo = (t - t_hi.astype(F32)).astype(BF16)
        lts.append(_dot_nt(wr_ref[...], t_hi) + (_dot_nt(wr_ref[...], t_lo) + _dot_nt(wrlo_ref[...], t_hi)))
    lt = jnp.concatenate(lts, axis=1) + br_ref[...]
    gl = lt[N_EXPERTS:N_EXPERTS + N_GROUPS]
    gmax = jnp.max(gl, axis=0, keepdims=True)
    g_w = 1.0 / jnp.sum(jnp.exp(gl - gmax), axis=0, keepdims=True)
    giota = lax.broadcasted_iota(jnp.int32, gl.shape, 0)
    g_sel = jnp.min(jnp.where(gl == gmax, giota, N_GROUPS), axis=0, keepdims=True)
    el = lt[0:N_EXPERTS]
    eiota = lax.broadcasted_iota(jnp.int32, el.shape, 0)
    in_grp = (eiota // EXP_PER_GROUP) == g_sel
    elm = jnp.where(in_grp, el, -jnp.inf)
    m1 = jnp.max(elm, axis=0, keepdims=True)
    esum = jnp.sum(jnp.where(in_grp, jnp.exp(el - m1), 0.0), axis=0, keepdims=True)
    i1 = jnp.min(jnp.where(elm == m1, eiota, N_EXPERTS), axis=0, keepdims=True)
    elm2 = jnp.where(eiota == i1, -jnp.inf, elm)
    m2 = jnp.max(elm2, axis=0, keepdims=True)
    i2 = jnp.min(jnp.where(elm2 == m2, eiota, N_EXPERTS), axis=0, keepdims=True)
    p1 = 1.0 / esum
    p2 = jnp.exp(m2 - m1) / esum
    psum = p1 + p2
    w1 = g_w * (p1 / psum)
    w2 = g_w * (p2 / psum)

    oh1 = (eiota == i1).astype(F32)
    oh2 = (eiota == i2).astype(F32)
    cnt = oh1 + oh2
    base = carry_ref[:, 0:1] + _dot(cnt.astype(BF16), sut_ref[...])
    r1 = jnp.sum(oh1 * base, axis=0, keepdims=True)
    r2 = jnp.sum(oh2 * base, axis=0, keepdims=True)
    new_carry = carry_ref[...] + jnp.sum(cnt, axis=1, keepdims=True)
    carry_ref[...] = new_carry
    cnt_ref[...] = new_carry

    zi = jnp.zeros((4, tm), jnp.int32)
    ri_ref[...] = jnp.concatenate([i1, i2, r1.astype(jnp.int32), r2.astype(jnp.int32), zi], axis=0)
    rw_ref[...] = jnp.concatenate([w1, w2, jnp.zeros((6, tm), F32)], axis=0)


def _attn_route(xf, mixed, kmem, vmem, w_out, norm_xa_w, wq, wo, norm_moe_w, w_route_t, b_route, sut,
                batch, seq):
    n, d = xf.shape
    w_route_hi = w_route_t.astype(BF16)
    w_route_lo = (w_route_t - w_route_hi.astype(F32)).astype(BF16)
    tm = TOKEN_TILE
    nt = seq // tm
    m = kmem.shape[1]
    tok = lambda b, t: (b * nt + t, 0)
    lane_tok = lambda b, t: (0, b * nt + t)
    const2 = lambda b, t: (0, 0)
    return pl.pallas_call(
        _attn_route_kernel,
        grid=(batch, nt),
        in_specs=[
            pl.BlockSpec((tm, d), tok),
            pl.BlockSpec((tm, d), tok),
            pl.BlockSpec((1, m, d), lambda b, t: (b, 0, 0)),
            pl.BlockSpec((1, m, d), lambda b, t: (b, 0, 0)),
            pl.BlockSpec((d, d), const2),
            pl.BlockSpec((1, d), const2),
            pl.BlockSpec((d, d), const2),
            pl.BlockSpec((d, d), const2),
            pl.BlockSpec((1, d), const2),
            pl.BlockSpec((ROUTE_ROWS, d), const2),
            pl.BlockSpec((ROUTE_ROWS, d), const2),
            pl.BlockSpec((ROUTE_ROWS, 1), const2),
            pl.BlockSpec((tm, tm), const2),
        ],
        out_specs=[
            pl.BlockSpec((tm, d), tok),
            pl.BlockSpec((tm, 1, d // 2), lambda b, t: (b * nt + t, 0, 0)),
            pl.BlockSpec((8, tm), lane_tok),
            pl.BlockSpec((8, tm), lane_tok),
            pl.BlockSpec((N_EXPERTS, 128), const2),
        ],
        out_shape=[
            jax.ShapeDtypeStruct((n, d), F32),
            jax.ShapeDtypeStruct((n, 1, d // 2), jnp.uint32),
            jax.ShapeDtypeStruct((8, n), jnp.int32),
            jax.ShapeDtypeStruct((8, n), F32),
            jax.ShapeDtypeStruct((N_EXPERTS, 128), F32),
        ],
        scratch_shapes=[pltpu.VMEM((N_EXPERTS, 128), F32)],
        compiler_params=pltpu.CompilerParams(
            dimension_semantics=("arbitrary", "arbitrary"), vmem_limit_bytes=VMEM_LIMIT),
        name="attn_route",
    )(xf, mixed, kmem, vmem, w_out, norm_xa_w, wq, wo, norm_moe_w, w_route_hi, w_route_lo, b_route, sut)


def _dispatch_kernel(zpos_ref, dest_ref, h_ref, xs_ref, idx_ref, idx_sem, row_sem, zero_ref, zero_sem):
    i = pl.program_id(0)
    nsteps = pl.num_programs(0)
    td = h_ref.shape[0]
    bm = zero_ref.shape[0]
    slot = i % 2

    def idx_copy(step, sl):
        off = pl.multiple_of(sl * (2 * td), 2 * td)
        return pltpu.make_async_copy(dest_ref.at[step], idx_ref.at[pl.ds(off, 2 * td)], idx_sem.at[sl])

    @pl.when(i == 0)
    def _():
        zero_ref[...] = jnp.zeros_like(zero_ref)

        def zero_copy(e):
            return pltpu.make_async_copy(zero_ref, xs_ref.at[pl.ds(pl.multiple_of(zpos_ref[e], bm), bm), 0], zero_sem)

        def tail_copy(b):
            return pltpu.make_async_copy(zero_ref, xs_ref.at[pl.ds(pl.multiple_of(b * bm, bm), bm), 0], zero_sem)

        nused = zpos_ref[N_EXPERTS]
        nblk = xs_ref.shape[0] // bm
        for e in range(N_EXPERTS):
            pl.when(zpos_ref[e] >= 0)(lambda e=e: zero_copy(e).start())
        lax.fori_loop(nused, nblk, lambda b, c: (tail_copy(b).start(), c)[1], 0)
        for e in range(N_EXPERTS):
            pl.when(zpos_ref[e] >= 0)(lambda e=e: zero_copy(e).wait())
        lax.fori_loop(nused, nblk, lambda b, c: (tail_copy(b).wait(), c)[1], 0)
        idx_copy(0, 0).start()

    idx_copy(i, slot).wait()

    @pl.when(i + 1 < nsteps)
    def _():
        idx_copy(i + 1, 1 - slot).start()

    base = slot * (2 * td)

    for t in range(td):
        for k in range(TOP_K):
            pltpu.make_async_copy(h_ref.at[t], xs_ref.at[idx_ref[base + k * td + t]], row_sem).start(priority=t % 2)
    for _ in range(TOP_K):
        pltpu.make_async_copy(xs_ref.at[pl.ds(0, td)], xs_ref.at[pl.ds(0, td)], row_sem).wait()


def _dispatch(h3p, dest_tiles, zpos, cap):
    w = h3p.shape[-1]
    nt, td2 = dest_tiles.shape
    td = td2 // 2
    grid_spec = pltpu.PrefetchScalarGridSpec(
        num_scalar_prefetch=1,
        grid=(nt,),
        in_specs=[
            pl.BlockSpec(memory_space=pl.ANY),
            pl.BlockSpec((td, 1, w), lambda i, zp: (i, 0, 0)),
        ],
        out_specs=pl.BlockSpec(memory_space=pl.ANY),
        scratch_shapes=[
            pltpu.SMEM((2 * td2,), jnp.int32),
            pltpu.SemaphoreType.DMA((2,)),
            pltpu.SemaphoreType.DMA,
            pltpu.VMEM((MOE_ROWS, w), jnp.uint32),
            pltpu.SemaphoreType.DMA,
        ],
    )
    return pl.pallas_call(
        _dispatch_kernel,
        grid_spec=grid_spec,
        out_shape=jax.ShapeDtypeStruct((cap, 1, w), jnp.uint32),
        compiler_params=pltpu.CompilerParams(
            dimension_semantics=("arbitrary",), vmem_limit_bytes=VMEM_LIMIT),
        name="dispatch",
    )(zpos, dest_tiles, h3p)


def _expert_kernel(blk_e_ref, nused_ref, next_e_ref, xs_ref, wg_ref, wu_ref, wd_ref, ys_ref, wg_b, wu_b, wd_b,
                   wg_s, wu_s, wd_s, xbuf, ybuf, zbuf, in_sem, out_sem, zero_sem, w_sem):
    i = pl.program_id(0)
    nsteps = pl.num_programs(0)
    nused = nused_ref[0]
    bm = xbuf.shape[1]
    slot = i % 2
    prev = blk_e_ref[jnp.maximum(i - 1, 0)]
    fresh = (i < nused) & ((i == 0) | (blk_e_ref[i] != prev))
    half = wd_b.shape[1] // 2

    def w_copies(e):
        return [pltpu.make_async_copy(src.at[e], dst, w_sem.at[j])
                for j, (src, dst) in enumerate(((wg_ref, wg_s), (wu_ref, wu_s), (wd_ref, wd_s)))]

    def rows(ref, step):
        return ref.at[pl.ds(pl.multiple_of(step * bm, bm), bm), 0]

    def in_copy(step, sl):
        return pltpu.make_async_copy(rows(xs_ref, step), xbuf.at[sl], in_sem.at[sl])

    def out_copy(step, sl):
        return pltpu.make_async_copy(ybuf.at[sl], rows(ys_ref, step), out_sem.at[sl])

    def zero_copy(step):
        return pltpu.make_async_copy(zbuf, rows(ys_ref, step), zero_sem)

    @pl.when(i == 0)
    def _():
        zbuf[...] = jnp.zeros_like(zbuf)
        in_copy(0, 0).start()
        for c in w_copies(blk_e_ref[0]):
            c.start()

    @pl.when(i + 1 < nused)
    def _():
        in_copy(i + 1, 1 - slot).start()

    @pl.when(fresh)
    def _():
        for c in w_copies(blk_e_ref[i]):
            c.wait()
        wg_b[...] = wg_s[...].astype(BF16)
        wu_b[...] = wu_s[...].astype(BF16)
        wd_b[...] = wd_s[...].astype(BF16)

        @pl.when(next_e_ref[i] >= 0)
        def _():
            for c in w_copies(next_e_ref[i]):
                c.start()

    @pl.when(i < nused)
    def _():
        in_copy(i, slot).wait()
        pl.when(i >= 2)(lambda: out_copy(i - 2, slot).wait())
        lo, hi = _unpack_rows(xbuf[slot])
        xb = jnp.concatenate([lo.astype(BF16), hi.astype(BF16)], axis=1)
        hid = (_silu(_dot(xb, wg_b[...])) * _dot(xb, wu_b[...])).astype(BF16)
        y = _dot(hid, wd_b[...])
        ybuf[slot] = _pack_rows(y[:, :half], y[:, half:])
        out_copy(i, slot).start(priority=1)

    pl.when(i >= nused)(lambda: zero_copy(i).start())

    @pl.when(i == nsteps - 1)
    def _():
        pl.when(nused >= 2)(lambda: out_copy(nused - 2, nused % 2).wait())
        pl.when(nused >= 1)(lambda: out_copy(nused - 1, (nused - 1) % 2).wait())
        lax.fori_loop(nused, nsteps, lambda b, c: (zero_copy(b).wait(), c)[1], 0)


def _experts(xs, blk_e, nused, next_e, w_gate, w_up, w_down):
    cap, _, w = xs.shape
    _, d, de = w_gate.shape
    bm = MOE_ROWS
    grid_spec = pltpu.PrefetchScalarGridSpec(
        num_scalar_prefetch=3,
        grid=(cap // bm,),
        in_specs=[pl.BlockSpec(memory_space=pl.ANY)] * 4,
        out_specs=pl.BlockSpec(memory_space=pl.ANY),
        scratch_shapes=[
            pltpu.VMEM((d, de), BF16),
            pltpu.VMEM((d, de), BF16),
            pltpu.VMEM((de, d), BF16),
            pltpu.VMEM((d, de), F32),
            pltpu.VMEM((d, de), F32),
            pltpu.VMEM((de, d), F32),
            pltpu.VMEM((2, bm, w), jnp.uint32),
            pltpu.VMEM((2, bm, w), jnp.uint32),
            pltpu.VMEM((bm, w), jnp.uint32),
            pltpu.SemaphoreType.DMA((2,)),
            pltpu.SemaphoreType.DMA((2,)),
            pltpu.SemaphoreType.DMA,
            pltpu.SemaphoreType.DMA((3,)),
        ],
    )
    return pl.pallas_call(
        _expert_kernel,
        grid_spec=grid_spec,
        out_shape=jax.ShapeDtypeStruct((cap, 1, w), jnp.uint32),
        compiler_params=pltpu.CompilerParams(
            dimension_semantics=("arbitrary",), vmem_limit_bytes=VMEM_LIMIT),
        name="experts",
    )(blk_e, nused, next_e, xs, w_gate, w_up, w_down)


def _combine_kernel(dest_ref, ys_ref, x2_ref, rw_ref, eye_ref, nw_ref, o_ref, idx_ref, idx_sem, ybuf, ysem):
    i = pl.program_id(0)
    nsteps = pl.num_programs(0)
    tc, d = x2_ref.shape
    half = d // 2
    n_idx = 2 * tc

    def idx_copy(step):
        sl = step % 3
        off = pl.multiple_of(sl * n_idx, n_idx)
        return pltpu.make_async_copy(dest_ref.at[step], idx_ref.at[pl.ds(off, n_idx)], idx_sem.at[sl])

    def gather(step):
        base = (step % 3) * n_idx
        buf = ybuf.at[step % 2]
        sem = ysem.at[step % 2]

        for t in range(n_idx):
            pltpu.make_async_copy(ys_ref.at[idx_ref[base + t]], buf.at[pl.ds(t, 1)], sem).start(priority=t % 2)

    @pl.when(i == 0)
    def _():
        idx_copy(0).start()
        idx_copy(0).wait()
        gather(0)

        @pl.when(nsteps > 1)
        def _():
            idx_copy(1).start()

    @pl.when(i + 1 < nsteps)
    def _():
        idx_copy(i + 1).wait()

        @pl.when(i + 2 < nsteps)
        def _():
            idx_copy(i + 2).start()

        gather(i + 1)

    slot = i % 2
    pltpu.make_async_copy(ybuf.at[slot], ybuf.at[slot], ysem.at[slot]).wait()
    eye = eye_ref[...]
    ew = eye.shape[0]

    def to_columns(r):
        r_a = r.astype(BF16)
        r_b = (r - r_a.astype(F32)).astype(BF16)
        r_c = (r - r_a.astype(F32) - r_b.astype(F32)).astype(BF16)
        return _dot_nt(eye, r_a) + (_dot_nt(eye, r_b) + _dot_nt(eye, r_c))

    wcol = jnp.concatenate([to_columns(rw_ref[:, j:j + ew]) for j in range(0, tc, ew)], axis=0)
    lo1, hi1 = _unpack_rows(ybuf[slot, 0:tc])
    lo2, hi2 = _unpack_rows(ybuf[slot, tc:n_idx])
    w1 = wcol[:, 0:1]
    w2 = wcol[:, 1:2]
    z_lo = x2_ref[:, :half] + (lo1 * w1 + lo2 * w2)
    z_hi = x2_ref[:, half:] + (hi1 * w1 + hi2 * w2)
    ms = (jnp.sum(z_lo * z_lo, axis=-1, keepdims=True) + jnp.sum(z_hi * z_hi, axis=-1, keepdims=True)) / d
    scale = lax.rsqrt(ms + EPS)
    o_ref[:, :half] = z_lo * scale * nw_ref[:, :half]
    o_ref[:, half:] = z_hi * scale * nw_ref[:, half:]


def _combine(x2, ys, dest_tiles, rw, eye, norm_w):
    n, d = x2.shape
    nt, n_idx = dest_tiles.shape
    tc = n_idx // 2
    w = ys.shape[-1]
    return pl.pallas_call(
        _combine_kernel,
        grid=(nt,),
        in_specs=[pl.BlockSpec(memory_space=pl.ANY)] * 2 + [
            pl.BlockSpec((tc, d), lambda i: (i, 0)),
            pl.BlockSpec((8, tc), lambda i: (0, i)),
            pl.BlockSpec(eye.shape, lambda i: (0, 0)),
            pl.BlockSpec((1, d), lambda i: (0, 0)),
        ],
        out_specs=pl.BlockSpec((tc, d), lambda i: (i, 0)),
        out_shape=jax.ShapeDtypeStruct((n, d), F32),
        scratch_shapes=[
            pltpu.SMEM((3 * n_idx,), jnp.int32),
            pltpu.SemaphoreType.DMA((3,)),
            pltpu.VMEM((2, n_idx, w), jnp.uint32),
            pltpu.SemaphoreType.DMA((2,)),
        ],
        compiler_params=pltpu.CompilerParams(
            dimension_semantics=("arbitrary",), vmem_limit_bytes=VMEM_LIMIT),
        name="combine",
    )(dest_tiles, ys, x2, rw, eye, norm_w)


def _layer(xf, mem, batch, seq, norm_mix_w, w_in, ret_norm_w, ml_conv_w, ml_conv_b, ml_gate_b, ml_norm_w,
           w_out, norm_xa_w, norm_mem_w, xa_wq, xa_wkv, xa_wo, norm_moe_w, moe_w_group, moe_b_group,
           moe_w_router, moe_b_router, moe_w_gate, moe_w_up, moe_w_down, final_norm_w):
    n, d = xf.shape
    def halves_first(w):
        return w.reshape(d, RET_HEADS, 2, RET_DK // 2).transpose(0, 2, 1, 3).reshape(d, RET_QK)

    w_main = jnp.concatenate([halves_first(w_in[:, OFF_RQ:OFF_RQ + RET_QK]),
                              halves_first(w_in[:, OFF_RK:OFF_RK + RET_QK]),
                              w_in[:, OFF_RV:MAIN_WIDTH]], axis=1).astype(BF16)
    w_if = w_in[:, MAIN_WIDTH:].astype(BF16)
    tabs = {k_: jnp.asarray(v_) for k_, v_ in _mixer_tables(seq).items()}
    w_if_pad = jnp.pad(w_if, ((0, 0), (0, GATE_LANES - N_GATES)))
    proj, g, gt = _inproj(xf, norm_mix_w.reshape(1, d), w_main, w_if_pad, w_if.T, tabs["cos"], tabs["sin"],
                          ml_conv_w, ml_conv_b.reshape(1, 2 * ML_QK), seq)
    mixed = _mixer(proj, g, gt, tabs, ret_norm_w.reshape(1, RET_V), ml_norm_w.reshape(1, ML_V), ml_gate_b,
                   batch, seq)

    kmem, vmem = _memkv(mem, norm_mem_w.reshape(1, d), xa_wkv.astype(BF16))

    w_route_t = jnp.concatenate(
        [moe_w_router.T, moe_w_group.T, jnp.zeros((ROUTE_ROWS - N_EXPERTS - N_GROUPS, d), F32)], axis=0)
    b_route = jnp.concatenate(
        [moe_b_router, moe_b_group, jnp.zeros((ROUTE_ROWS - N_EXPERTS - N_GROUPS,), F32)]).reshape(ROUTE_ROWS, 1)
    tm = TOKEN_TILE
    sut = jnp.asarray(np.triu(np.ones((tm, tm), np.float32), 1), dtype=BF16)
    x2, h3, ri, rw, cnt = _attn_route(xf, mixed, kmem, vmem, w_out.astype(BF16), norm_xa_w.reshape(1, d),
                                      xa_wq.astype(BF16), xa_wo.astype(BF16), norm_moe_w.reshape(1, d),
                                      w_route_t, b_route, sut, batch, seq)

    bm = MOE_ROWS
    counts = cnt[:, 0].astype(jnp.int32)
    padded = (counts + bm - 1) // bm * bm
    pends = jnp.cumsum(padded)
    pstarts = pends - padded
    expert = ri[0:TOP_K]
    onehot = expert[None] == jnp.arange(N_EXPERTS, dtype=jnp.int32)[:, None, None]
    dest = jnp.sum(jnp.where(onehot, pstarts[:, None, None], 0), axis=0) + ri[TOP_K:2 * TOP_K]
    cap = n * TOP_K + N_EXPERTS * bm
    nblk = cap // bm
    blk_start = jnp.arange(nblk, dtype=jnp.int32) * bm
    blk_e = jnp.minimum(jnp.sum(blk_start[:, None] >= pends[None, :], axis=1), N_EXPERTS - 1).astype(jnp.int32)
    nused = (pends[-1] // bm).astype(jnp.int32).reshape(1)
    zpos = jnp.where(padded > counts, pends - bm, -1).astype(jnp.int32)
    zpos = jnp.concatenate([zpos, nused])

    def tiles(rows):
        return dest.reshape(TOP_K, n // rows, rows).transpose(1, 0, 2).reshape(n // rows, TOP_K * rows)

    eids = jnp.arange(N_EXPERTS, dtype=jnp.int32)
    later = jnp.where((eids[None, :] > eids[:, None]) & (padded[None, :] > 0), eids[None, :], N_EXPERTS)
    next_of = jnp.min(later, axis=1)
    next_of = jnp.where(next_of == N_EXPERTS, -1, next_of)
    next_e = jnp.sum(jnp.where(blk_e[:, None] == eids[None, :], next_of[None, :], 0), axis=1).astype(jnp.int32)

    xs = _dispatch(h3, tiles(DISPATCH_TILE), zpos, cap)
    ys = _experts(xs, blk_e, nused, next_e, moe_w_gate, moe_w_up, moe_w_down)
    eye = jnp.asarray(np.eye(COMBINE_EYE, dtype=np.float32), dtype=BF16)
    return _combine(x2, ys, tiles(COMBINE_TILE), rw, eye, final_norm_w.reshape(1, d))


def kernel(x, mem, norm_mix_w, w_in, ret_norm_w, ml_conv_w, ml_conv_b, ml_gate_b, ml_norm_w, w_out, norm_xa_w, norm_mem_w, xa_wq, xa_wkv, xa_wo, norm_moe_w, moe_w_group, moe_b_group, moe_w_router, moe_b_router, moe_w_gate, moe_w_up, moe_w_down, norm_final_w):
    batch, seq, d = x.shape
    depth = w_in.shape[0]
    assert depth == 1, "the final norm is fused into the last layer's combine kernel"
    l = 0
    out = _layer(x.reshape(batch * seq, d), mem, batch, seq, norm_mix_w[l], w_in[l], ret_norm_w[l], ml_conv_w[l],
                 ml_conv_b[l], ml_gate_b[l], ml_norm_w[l], w_out[l], norm_xa_w[l], norm_mem_w[l], xa_wq[l],
                 xa_wkv[l], xa_wo[l], norm_moe_w[l], moe_w_group[l], moe_b_group[l], moe_w_router[l],
                 moe_b_router[l], moe_w_gate[l], moe_w_up[l], moe_w_down[l], norm_final_w)
    return out.reshape(batch, seq, d)
```
